```python
import jax, jax.numpy as jnp
from jax import lax
import numpy as np

D_MODEL = 1024
BATCH = 8
SEQ = 4096
DEPTH = 1

CHUNK = 64
N_PREV_CHUNKS = 8
BAND_CHUNKS = N_PREV_CHUNKS + 1
N_HEADS = 16
HEAD_DIM = 64
D_ATTN = N_HEADS * HEAD_DIM
D_CONV = D_MODEL
CONV_WIDTH = 3
MAX_REL = 256
D_FF = 4 * D_MODEL
N_BRANCHES = 2
EPS = 1e-6
NEG_INF = -1e30

kernel_name = "chunk_causal_hybrid_attn_shortconv_block"


def rms_norm(x, g):
    xf = x.astype(jnp.float32)
    y = xf * lax.rsqrt(jnp.mean(xf * xf, axis=-1, keepdims=True) + EPS)
    return (y * g.astype(jnp.float32)).astype(x.dtype)


def chunk_band(t):
    b, nc, c, h, dh = t.shape
    tp = jnp.pad(t, ((0, 0), (N_PREV_CHUNKS, 0), (0, 0), (0, 0), (0, 0)))
    band = jnp.stack([tp[:, o:o + nc] for o in range(BAND_CHUNKS)], axis=2)
    return band.reshape(b, nc, BAND_CHUNKS * c, h, dh)


def chunked_relpos_attention(q, k, v, q_norm_g, k_norm_g, rel_bias):
    b, s, _ = q.shape
    nc = s // CHUNK
    kw = BAND_CHUNKS * CHUNK
    q = rms_norm(q.reshape(b, nc, CHUNK, N_HEADS, HEAD_DIM), q_norm_g)
    k = rms_norm(k.reshape(b, nc, CHUNK, N_HEADS, HEAD_DIM), k_norm_g)
    v = v.reshape(b, nc, CHUNK, N_HEADS, HEAD_DIM)
    kb = chunk_band(k)
    vb = chunk_band(v)

    q_idx = jnp.arange(CHUNK)[:, None]
    k_idx = jnp.arange(kw)[None, :]
    dist = q_idx - k_idx + N_PREV_CHUNKS * CHUNK
    rel_idx = jnp.clip(dist, -MAX_REL, MAX_REL) + MAX_REL
    bias = rel_bias[:, rel_idx].astype(jnp.float32)

    key_chunk = jnp.arange(nc)[:, None] + (jnp.arange(kw) // CHUNK)[None, :] - N_PREV_CHUNKS
    valid = key_chunk >= 0

    scale = HEAD_DIM ** -0.5
    scores = jnp.einsum('bnqhd,bnkhd->bnhqk', q, kb).astype(jnp.float32) * scale
    scores = scores + bias[None, None]
    scores = jnp.where(valid[None, :, None, None, :], scores, NEG_INF)
    probs = jax.nn.softmax(scores, axis=-1).astype(vb.dtype)
    out = jnp.einsum('bnhqk,bnkhd->bnqhd', probs, vb)
    return out.reshape(b, s, D_ATTN)


def gated_short_conv(bg, cg, xc, conv_w, conv_b):
    s = xc.shape[1]
    u = cg * xc
    up = jnp.pad(u, ((0, 0), (CONV_WIDTH - 1, 0), (0, 0)))
    conv = conv_b + sum(conv_w[j] * up[:, j:j + s] for j in range(CONV_WIDTH))
    return bg * conv


def _fwd_setup_inputs(seed: int = 0) -> dict:
    key = jax.random.key(seed)
    ks = jax.random.split(key, 20)
    f32 = jnp.float32
    d_in = 3 * D_ATTN + 3 * D_CONV
    return {
        "x": jax.random.normal(ks[0], (BATCH, SEQ, D_MODEL), f32),
        "norm1_g": 1.0 + 0.05 * jax.random.normal(ks[1], (D_MODEL,), f32),
        "w_in": jax.random.normal(ks[2], (D_MODEL, d_in), f32) * D_MODEL ** -0.5,
        "q_norm_g": 1.0 + 0.05 * jax.random.normal(ks[3], (HEAD_DIM,), f32),
        "k_norm_g": 1.0 + 0.05 * jax.random.normal(ks[4], (HEAD_DIM,), f32),
        "rel_bias": 0.5 * jax.random.normal(ks[5], (N_HEADS, 2 * MAX_REL + 1), f32),
        "conv_w": jax.random.normal(ks[6], (CONV_WIDTH, D_CONV), f32) * CONV_WIDTH ** -0.5,
        "conv_b": 0.02 * jax.random.normal(ks[7], (D_CONV,), f32),
        "w_attn_proj": jax.random.normal(ks[8], (D_ATTN, D_MODEL), f32) * D_ATTN ** -0.5,
        "w_conv_proj": jax.random.normal(ks[9], (D_CONV, D_MODEL), f32) * D_CONV ** -0.5,
        "w_gate": jax.random.normal(ks[10], (D_MODEL, N_BRANCHES * D_MODEL), f32) * D_MODEL ** -0.5,
        "b_gate": 0.02 * jax.random.normal(ks[11], (N_BRANCHES * D_MODEL,), f32),
        "w_out": jax.random.normal(ks[12], (D_MODEL, D_MODEL), f32) * D_MODEL ** -0.5,
        "norm2_g": 1.0 + 0.05 * jax.random.normal(ks[13], (D_MODEL,), f32),
        "w_up": jax.random.normal(ks[14], (D_MODEL, D_FF), f32) * D_MODEL ** -0.5,
        "w_down": jax.random.normal(ks[15], (D_FF, D_MODEL), f32) * D_FF ** -0.5,
    }


def _fwd_reference(x, norm1_g, w_in, q_norm_g, k_norm_g, rel_bias, conv_w, conv_b,
              w_attn_proj, w_conv_proj, w_gate, b_gate, w_out, norm2_g, w_up, w_down):
    for _ in range(DEPTH):
        h = rms_norm(x, norm1_g)
        proj = jnp.einsum('bsd,de->bse', h, w_in)
        q, k, v, bg, cg, xc = jnp.split(
            proj,
            [D_ATTN, 2 * D_ATTN, 3 * D_ATTN,
             3 * D_ATTN + D_CONV, 3 * D_ATTN + 2 * D_CONV],
            axis=-1)

        y_attn = chunked_relpos_attention(q, k, v, q_norm_g, k_norm_g, rel_bias)
        y_conv = gated_short_conv(bg, cg, xc, conv_w, conv_b)

        y_attn = jnp.einsum('bse,ed->bsd', y_attn, w_attn_proj)
        y_conv = jnp.einsum('bse,ed->bsd', y_conv, w_conv_proj)

        gates = jax.nn.sigmoid(jnp.einsum('bsd,de->bse', h, w_gate) + b_gate)
        g_attn, g_conv = jnp.split(gates, 2, axis=-1)
        merged = g_attn * y_attn + g_conv * y_conv
        x = x + jnp.einsum('bsd,de->bse', merged, w_out)

        h2 = rms_norm(x, norm2_g)
        u = jnp.square(jax.nn.relu(jnp.einsum('bsd,df->bsf', h2, w_up)))
        x = x + jnp.einsum('bsf,fd->bsd', u, w_down)
    return x


import jax as _jax
import jax.numpy as _jnp

TWIN_FORMAT = 'train_step'
FWD_PARAMS = ['x', 'norm1_g', 'w_in', 'q_norm_g', 'k_norm_g', 'rel_bias', 'conv_w', 'conv_b', 'w_attn_proj', 'w_conv_proj', 'w_gate', 'b_gate', 'w_out', 'norm2_g', 'w_up', 'w_down']
TWIN_WEIGHTS = ['norm1_g', 'w_in', 'q_norm_g', 'k_norm_g', 'rel_bias', 'conv_w', 'conv_b', 'w_attn_proj', 'w_conv_proj', 'w_gate', 'b_gate', 'w_out', 'norm2_g', 'w_up', 'w_down']
TWIN_DIFF_INPUT = 'x'
TWIN_INPUTS = ['x', 'norm1_g', 'w_in', 'q_norm_g', 'k_norm_g', 'rel_bias', 'conv_w', 'conv_b', 'w_attn_proj', 'w_conv_proj', 'w_gate', 'b_gate', 'w_out', 'norm2_g', 'w_up', 'w_down', 'loss_target', 'm_norm1_g', 'm_w_in', 'm_q_norm_g', 'm_k_norm_g', 'm_rel_bias', 'm_conv_w', 'm_conv_b', 'm_w_attn_proj', 'm_w_conv_proj', 'm_w_gate', 'm_b_gate', 'm_w_out', 'm_norm2_g', 'm_w_up', 'm_w_down', 'v_norm1_g', 'v_w_in', 'v_q_norm_g', 'v_k_norm_g', 'v_rel_bias', 'v_conv_w', 'v_conv_b', 'v_w_attn_proj', 'v_w_conv_proj', 'v_w_gate', 'v_b_gate', 'v_w_out', 'v_norm2_g', 'v_w_up', 'v_w_down']
TWIN_OUTPUTS = ['loss', 'grad_x', 'grad_norm1_g', 'grad_w_in', 'grad_q_norm_g', 'grad_k_norm_g', 'grad_rel_bias', 'grad_conv_w', 'grad_conv_b', 'grad_w_attn_proj', 'grad_w_conv_proj', 'grad_w_gate', 'grad_b_gate', 'grad_w_out', 'grad_norm2_g', 'grad_w_up', 'grad_w_down', 'delta_norm1_g', 'delta_w_in', 'delta_q_norm_g', 'delta_k_norm_g', 'delta_rel_bias', 'delta_conv_w', 'delta_conv_b', 'delta_w_attn_proj', 'delta_w_conv_proj', 'delta_w_gate', 'delta_b_gate', 'delta_w_out', 'delta_norm2_g', 'delta_w_up', 'delta_w_down', 'new_m_norm1_g', 'new_m_w_in', 'new_m_q_norm_g', 'new_m_k_norm_g', 'new_m_rel_bias', 'new_m_conv_w', 'new_m_conv_b', 'new_m_w_attn_proj', 'new_m_w_conv_proj', 'new_m_w_gate', 'new_m_b_gate', 'new_m_w_out', 'new_m_norm2_g', 'new_m_w_up', 'new_m_w_down', 'new_v_norm1_g', 'new_v_w_in', 'new_v_q_norm_g', 'new_v_k_norm_g', 'new_v_rel_bias', 'new_v_conv_w', 'new_v_conv_b', 'new_v_w_attn_proj', 'new_v_w_conv_proj', 'new_v_w_gate', 'new_v_b_gate', 'new_v_w_out', 'new_v_norm2_g', 'new_v_w_up', 'new_v_w_down']
TWIN_LEAF_KINDS = {'loss': 'loss', 'grad_x': 'grad_x', 'grad_norm1_g': 'grad_w', 'grad_w_in': 'grad_w', 'grad_q_norm_g': 'grad_w', 'grad_k_norm_g': 'grad_w', 'grad_rel_bias': 'grad_w', 'grad_conv_w': 'grad_w', 'grad_conv_b': 'grad_w', 'grad_w_attn_proj': 'grad_w', 'grad_w_conv_proj': 'grad_w', 'grad_w_gate': 'grad_w', 'grad_b_gate': 'grad_w', 'grad_w_out': 'grad_w', 'grad_norm2_g': 'grad_w', 'grad_w_up': 'grad_w', 'grad_w_down': 'grad_w', 'delta_norm1_g': 'delta_w', 'delta_w_in': 'delta_w', 'delta_q_norm_g': 'delta_w', 'delta_k_norm_g': 'delta_w', 'delta_rel_bias': 'delta_w', 'delta_conv_w': 'delta_w', 'delta_conv_b': 'delta_w', 'delta_w_attn_proj': 'delta_w', 'delta_w_conv_proj': 'delta_w', 'delta_w_gate': 'delta_w', 'delta_b_gate': 'delta_w', 'delta_w_out': 'delta_w', 'delta_norm2_g': 'delta_w', 'delta_w_up': 'delta_w', 'delta_w_down': 'delta_w', 'new_m_norm1_g': 'new_m', 'new_m_w_in': 'new_m', 'new_m_q_norm_g': 'new_m', 'new_m_k_norm_g': 'new_m', 'new_m_rel_bias': 'new_m', 'new_m_conv_w': 'new_m', 'new_m_conv_b': 'new_m', 'new_m_w_attn_proj': 'new_m', 'new_m_w_conv_proj': 'new_m', 'new_m_w_gate': 'new_m', 'new_m_b_gate': 'new_m', 'new_m_w_out': 'new_m', 'new_m_norm2_g': 'new_m', 'new_m_w_up': 'new_m', 'new_m_w_down': 'new_m', 'new_v_norm1_g': 'new_v', 'new_v_w_in': 'new_v', 'new_v_q_norm_g': 'new_v', 'new_v_k_norm_g': 'new_v', 'new_v_rel_bias': 'new_v', 'new_v_conv_w': 'new_v', 'new_v_conv_b': 'new_v', 'new_v_w_attn_proj': 'new_v', 'new_v_w_conv_proj': 'new_v', 'new_v_w_gate': 'new_v', 'new_v_b_gate': 'new_v', 'new_v_w_out': 'new_v', 'new_v_norm2_g': 'new_v', 'new_v_w_up': 'new_v', 'new_v_w_down': 'new_v'}


def _forward(args):
    return _fwd_reference(*[args[k] for k in FWD_PARAMS])


def _output_shape():
    def fwd():
        inp = _fwd_setup_inputs(0)
        return _fwd_reference(*[inp[k] for k in FWD_PARAMS])
    out = _jax.eval_shape(fwd)
    return out.shape, out.dtype

N_MICROBATCH = 1
ADAM_LR = 0.001
ADAM_B1 = 0.9
ADAM_B2 = 0.999
ADAM_EPS = 1e-08
ADAM_WD = 0.01
ADAM_STEP = 10
PER_EXAMPLE_BATCH_AXIS = {'x': 0, 'loss_target': 0}
SHARED_INPUTS = []
_WEIGHT_DTYPES = {'norm1_g': _jnp.float32, 'w_in': _jnp.float32, 'q_norm_g': _jnp.float32, 'k_norm_g': _jnp.float32, 'rel_bias': _jnp.float32, 'conv_w': _jnp.float32, 'conv_b': _jnp.float32, 'w_attn_proj': _jnp.float32, 'w_conv_proj': _jnp.float32, 'w_gate': _jnp.float32, 'b_gate': _jnp.float32, 'w_out': _jnp.float32, 'norm2_g': _jnp.float32, 'w_up': _jnp.float32, 'w_down': _jnp.float32}
MOMENT_SCALE = {'norm1_g': 2.866104e+01, 'w_in': 3.504690e-01, 'q_norm_g': 7.870685e-01, 'k_norm_g': 7.874707e-01, 'rel_bias': 2.134449e-02, 'conv_w': 5.280174e+00, 'conv_b': 9.420745e-01, 'w_attn_proj': 1.191013e-01, 'w_conv_proj': 6.463983e-01, 'w_gate': 1.085391e-01, 'b_gate': 2.302163e+00, 'w_out': 7.357617e-01, 'norm2_g': 9.679582e+01, 'w_up': 8.916968e-01, 'w_down': 8.132049e+00}


def _to_microbatches(a, axis):
    t = _jnp.moveaxis(a, axis, 0)
    t = t.reshape((N_MICROBATCH, t.shape[0] // N_MICROBATCH) + t.shape[1:])
    return _jnp.moveaxis(t, 1, axis + 1)


def setup_inputs(seed: int = 0) -> dict:
    inp = _fwd_setup_inputs(seed)
    key = _jax.random.fold_in(_jax.random.key(seed), 7919)
    shape, _ = _output_shape()
    out = dict(inp)
    out["loss_target"] = _jax.random.normal(_jax.random.fold_in(key, 0), shape, _jnp.float32)
    for i, name in enumerate(TWIN_WEIGHTS):
        w = inp[name].astype(_jnp.float32)
        if MOMENT_SCALE is None:
            s = _jnp.sqrt(_jnp.mean(_jnp.square(w)) + 1e-30)
        else:
            s = MOMENT_SCALE[name]
        km, kv = _jax.random.split(_jax.random.fold_in(key, i + 1))
        out[name] = w
        out["m_" + name] = s * _jax.random.normal(km, w.shape, _jnp.float32)
        out["v_" + name] = (s * s) * _jax.random.uniform(kv, w.shape, _jnp.float32, 0.5, 1.5)
    if N_MICROBATCH > 1:
        for name, axis in PER_EXAMPLE_BATCH_AXIS.items():
            out[name] = _to_microbatches(out[name], axis)
    return {'x': out['x'], 'norm1_g': out['norm1_g'], 'w_in': out['w_in'], 'q_norm_g': out['q_norm_g'], 'k_norm_g': out['k_norm_g'], 'rel_bias': out['rel_bias'], 'conv_w': out['conv_w'], 'conv_b': out['conv_b'], 'w_attn_proj': out['w_attn_proj'], 'w_conv_proj': out['w_conv_proj'], 'w_gate': out['w_gate'], 'b_gate': out['b_gate'], 'w_out': out['w_out'], 'norm2_g': out['norm2_g'], 'w_up': out['w_up'], 'w_down': out['w_down'], 'loss_target': out['loss_target'], 'm_norm1_g': out['m_norm1_g'], 'm_w_in': out['m_w_in'], 'm_q_norm_g': out['m_q_norm_g'], 'm_k_norm_g': out['m_k_norm_g'], 'm_rel_bias': out['m_rel_bias'], 'm_conv_w': out['m_conv_w'], 'm_conv_b': out['m_conv_b'], 'm_w_attn_proj': out['m_w_attn_proj'], 'm_w_conv_proj': out['m_w_conv_proj'], 'm_w_gate': out['m_w_gate'], 'm_b_gate': out['m_b_gate'], 'm_w_out': out['m_w_out'], 'm_norm2_g': out['m_norm2_g'], 'm_w_up': out['m_w_up'], 'm_w_down': out['m_w_down'], 'v_norm1_g': out['v_norm1_g'], 'v_w_in': out['v_w_in'], 'v_q_norm_g': out['v_q_norm_g'], 'v_k_norm_g': out['v_k_norm_g'], 'v_rel_bias': out['v_rel_bias'], 'v_conv_w': out['v_conv_w'], 'v_conv_b': out['v_conv_b'], 'v_w_attn_proj': out['v_w_attn_proj'], 'v_w_conv_proj': out['v_w_conv_proj'], 'v_w_gate': out['v_w_gate'], 'v_b_gate': out['v_b_gate'], 'v_w_out': out['v_w_out'], 'v_norm2_g': out['v_norm2_g'], 'v_w_up': out['v_w_up'], 'v_w_down': out['v_w_down']}


def _loss(weights, diff, rest, loss_target):
    with _jax.named_scope("forward"):
        args = {**rest, TWIN_DIFF_INPUT: diff, **{k: w.astype(_WEIGHT_DTYPES[k]) for k, w in weights.items()}}
        y = _forward(args)
    with _jax.named_scope("loss_head"):
        err = _jnp.square(y.astype(_jnp.float32) - loss_target)
        return 0.5 * _jnp.sum(_jnp.mean(err, axis=-1)) if err.ndim else 0.5 * err


def _adamw(w, g, m, v):
    m = ADAM_B1 * m + (1.0 - ADAM_B1) * g
    v = ADAM_B2 * v + (1.0 - ADAM_B2) * _jnp.square(g)
    m_hat = m / (1.0 - ADAM_B1 ** ADAM_STEP)
    v_hat = v / (1.0 - ADAM_B2 ** ADAM_STEP)
    delta = -ADAM_LR * (m_hat / (_jnp.sqrt(v_hat) + ADAM_EPS) + ADAM_WD * w)
    return delta, m, v


def reference(x, norm1_g, w_in, q_norm_g, k_norm_g, rel_bias, conv_w, conv_b, w_attn_proj, w_conv_proj, w_gate, b_gate, w_out, norm2_g, w_up, w_down, loss_target, m_norm1_g, m_w_in, m_q_norm_g, m_k_norm_g, m_rel_bias, m_conv_w, m_conv_b, m_w_attn_proj, m_w_conv_proj, m_w_gate, m_b_gate, m_w_out, m_norm2_g, m_w_up, m_w_down, v_norm1_g, v_w_in, v_q_norm_g, v_k_norm_g, v_rel_bias, v_conv_w, v_conv_b, v_w_attn_proj, v_w_conv_proj, v_w_gate, v_b_gate, v_w_out, v_norm2_g, v_w_up, v_w_down):
    given = dict(x=x, norm1_g=norm1_g, w_in=w_in, q_norm_g=q_norm_g, k_norm_g=k_norm_g, rel_bias=rel_bias, conv_w=conv_w, conv_b=conv_b, w_attn_proj=w_attn_proj, w_conv_proj=w_conv_proj, w_gate=w_gate, b_gate=b_gate, w_out=w_out, norm2_g=norm2_g, w_up=w_up, w_down=w_down, loss_target=loss_target, m_norm1_g=m_norm1_g, m_w_in=m_w_in, m_q_norm_g=m_q_norm_g, m_k_norm_g=m_k_norm_g, m_rel_bias=m_rel_bias, m_conv_w=m_conv_w, m_conv_b=m_conv_b, m_w_attn_proj=m_w_attn_proj, m_w_conv_proj=m_w_conv_proj, m_w_gate=m_w_gate, m_b_gate=m_b_gate, m_w_out=m_w_out, m_norm2_g=m_norm2_g, m_w_up=m_w_up, m_w_down=m_w_down, v_norm1_g=v_norm1_g, v_w_in=v_w_in, v_q_norm_g=v_q_norm_g, v_k_norm_g=v_k_norm_g, v_rel_bias=v_rel_bias, v_conv_w=v_conv_w, v_conv_b=v_conv_b, v_w_attn_proj=v_w_attn_proj, v_w_conv_proj=v_w_conv_proj, v_w_gate=v_w_gate, v_b_gate=v_b_gate, v_w_out=v_w_out, v_norm2_g=v_norm2_g, v_w_up=v_w_up, v_w_down=v_w_down)
    weights = {n: given[n] for n in TWIN_WEIGHTS}
    shared = {n: given[n] for n in SHARED_INPUTS}
    per_example = {n: given[n] for n in ['x']}
    grad_fn = _jax.value_and_grad(_loss, argnums=(0, 1))

    def one_microbatch(ex, loss_target):
        ex = dict(ex)
        diff = ex.pop(TWIN_DIFF_INPUT)
        return grad_fn(weights, diff, {**shared, **ex}, loss_target)

    if N_MICROBATCH == 1:
        loss, (grad_w, grad_x) = one_microbatch(per_example, given["loss_target"])
    else:
        def body(carry, xs):
            loss_sum, grad_sum = carry
            l_k, (gw_k, gx_k) = one_microbatch(xs[0], xs[1])
            with _jax.named_scope("update"):
                return (loss_sum + l_k, _jax.tree.map(_jnp.add, grad_sum, gw_k)), gx_k

        init = (_jnp.zeros((), _jnp.float32), _jax.tree.map(_jnp.zeros_like, weights))
        (loss, grad_w), grad_x = _jax.lax.scan(body, init, (per_example, given["loss_target"]))
    with _jax.named_scope("update"):
        delta_w, new_m, new_v = {}, {}, {}
        for n in TWIN_WEIGHTS:
            delta_w[n], new_m[n], new_v[n] = _adamw(weights[n], grad_w[n], given["m_" + n], given["v_" + n])
    return (loss, grad_x, *[grad_w[n] for n in TWIN_WEIGHTS], *[delta_w[n] for n in TWIN_WEIGHTS],
            *[new_m[n] for n in TWIN_WEIGHTS], *[new_v[n] for n in TWIN_WEIGHTS])
```

```python
import functools

import jax
import jax.numpy as jnp
from jax import lax
from jax.experimental import pallas as pl
from jax.experimental.pallas import tpu as pltpu

F32 = jnp.float32
BF16 = jnp.bfloat16

D_MODEL = 1024
N_HEADS = 16
HEAD_DIM = 64
CHUNK = 64
N_PREV_CHUNKS = 8
MAX_REL = 256
D_FF = 4096
EPS = 1e-6
NEG_INF = -1e30
N_DEV = 8

ADAM_LR = 0.001
ADAM_B1 = 0.9
ADAM_B2 = 0.999
ADAM_EPS = 1e-08
ADAM_WD = 0.01
ADAM_STEP = 10

LANES = 128
SUBLANES = 8
VMEM_LIMIT = 48 * 1024 * 1024
QB = 256
KW = 3 * QB
SKEW = 1024

MESH_T = pl.DeviceIdType.MESH


def _dot(a, b):
    return jnp.dot(a, b, preferred_element_type=F32)


def _dot_nt(a, b):
    return lax.dot_general(a, b, (((1,), (1,)), ((), ())), preferred_element_type=F32)


def _dot_tn(a, b):
    return lax.dot_general(a, b, (((0,), (0,)), ((), ())), preferred_element_type=F32)


def _params(sem=None):
    return pltpu.CompilerParams(dimension_semantics=sem, vmem_limit_bytes=VMEM_LIMIT)


def _fold8(v):
    rows, n = v.shape
    return v.reshape(rows // SUBLANES, SUBLANES, n).sum(axis=0)


def _head_sum_matrix():
    r = lax.broadcasted_iota(jnp.int32, (LANES, LANES), 0) // HEAD_DIM
    c = lax.broadcasted_iota(jnp.int32, (LANES, LANES), 1) // HEAD_DIM
    return (r == c).astype(BF16)


def _head_sums(v, e):
    hi = v.astype(BF16)
    lo = (v - hi.astype(F32)).astype(BF16)
    return _dot(hi, e) + _dot(lo, e)


def _in_proj(x, g1, w_in):
    t = x.shape[0]
    tm = min(t, 1024)
    n_slab = w_in.shape[1] // D_MODEL

    def body(x_ref, g_ref, w_ref, proj_ref, h_ref):
        @pl.when(pl.program_id(1) == 0)
        def _():
            xf = x_ref[...]
            r = lax.rsqrt(jnp.mean(xf * xf, axis=-1, keepdims=True) + EPS)
            h_ref[...] = (xf * r * g_ref[...]).astype(BF16)

        proj_ref[...] = _dot(h_ref[...], w_ref[...]).astype(BF16)

    return pl.pallas_call(
        body, name="in_proj", grid=(t // tm, n_slab),
        in_specs=[pl.BlockSpec((tm, D_MODEL), lambda i, k: (i, 0)),
                  pl.BlockSpec((1, D_MODEL), lambda i, k: (0, 0)),
                  pl.BlockSpec((D_MODEL, D_MODEL), lambda i, k: (0, k))],
        out_specs=[pl.BlockSpec((tm, D_MODEL), lambda i, k: (i, k)),
                   pl.BlockSpec((tm, D_MODEL), lambda i, k: (i, 0))],
        out_shape=[jax.ShapeDtypeStruct((t, n_slab * D_MODEL), BF16), jax.ShapeDtypeStruct((t, D_MODEL), BF16)],
        compiler_params=_params(("parallel", "arbitrary")),
    )(x, g1, w_in)


def _gate_proj(h, w_g, b_g):
    t = h.shape[0]
    tm = min(t, 1024)

    def body(h_ref, w_ref, b_ref, o_ref):
        o_ref[...] = jax.nn.sigmoid(_dot(h_ref[...], w_ref[...]) + b_ref[...]).astype(BF16)

    return pl.pallas_call(
        body, name="gate_proj", grid=(t // tm, 2),
        in_specs=[pl.BlockSpec((tm, D_MODEL), lambda i, k: (i, 0)),
                  pl.BlockSpec((D_MODEL, D_MODEL), lambda i, k: (0, k)),
                  pl.BlockSpec((1, D_MODEL), lambda i, k: (0, k))],
        out_specs=pl.BlockSpec((tm, D_MODEL), lambda i, k: (i, k)),
        out_shape=jax.ShapeDtypeStruct((t, 2 * D_MODEL), BF16),
        compiler_params=_params(("parallel", "arbitrary")),
    )(h, w_g, b_g)


def _qknorm_fwd(proj, gq, gk):
    t = proj.shape[0]
    tm = min(t, 512)
    scale = HEAD_DIM ** -0.5

    def body(q_ref, k_ref, gq_ref, gk_ref, qn_ref, kn_ref):
        e = _head_sum_matrix()
        for src, g_ref, dst, sc in ((q_ref, gq_ref, qn_ref, scale), (k_ref, gk_ref, kn_ref, 1.0)):
            for s in range(D_MODEL // LANES):
                sl = slice(s * LANES, (s + 1) * LANES)
                xf = src[:, sl].astype(F32)
                r = lax.rsqrt(_head_sums(xf * xf, e) * (1.0 / HEAD_DIM) + EPS)
                dst[:, sl] = (xf * r * g_ref[:, sl] * sc).astype(BF16)

    return pl.pallas_call(
        body, name="qknorm_fwd", grid=(t // tm,),
        in_specs=[pl.BlockSpec((tm, D_MODEL), lambda i: (i, 0)),
                  pl.BlockSpec((tm, D_MODEL), lambda i: (i, 1)),
                  pl.BlockSpec((1, D_MODEL), lambda i: (0, 0)),
                  pl.BlockSpec((1, D_MODEL), lambda i: (0, 0))],
        out_specs=[pl.BlockSpec((tm, D_MODEL), lambda i: (i, 0))] * 2,
        out_shape=[jax.ShapeDtypeStruct((t, D_MODEL), BF16)] * 2,
        compiler_params=_params(("parallel",)),
    )(proj, proj, gq, gk)


def _bias_tiles(rel_bias):
    edge = jnp.broadcast_to(rel_bias[:, 2 * MAX_REL:], (N_HEADS, QB))
    mid = rel_bias[:, 1:2 * MAX_REL + 1][:, ::-1]
    vec = jnp.concatenate([edge, mid, edge], axis=1)
    flat = jnp.tile(vec, (1, QB))[:, :QB * (SKEW - 1)]
    tile = flat.reshape(N_HEADS, QB, SKEW - 1)[:, :, :KW]
    qc = jnp.arange(QB)[:, None] // CHUNK
    kc = jnp.arange(KW)[None, :] // CHUNK
    band = (kc >= qc) & (kc <= qc + N_PREV_CHUNKS)
    return jnp.where(band[None], tile, NEG_INF)


def _window_specs(col0):
    return [pl.BlockSpec((QB, LANES), functools.partial(
        lambda p, b, back: (jnp.maximum(b - back, 0), col0 + p), back=back)) for back in (2, 1, 0)]


def _attn_fwd(qn, kn, proj, bias):
    t = qn.shape[0]
    nb = t // QB
    v_col0 = 2 * D_MODEL // LANES

    def body(q_ref, k0, k1, k2, v0, v1, v2, bias_ref, o_ref):
        b = pl.program_id(1)
        q = q_ref[...]
        k = jnp.concatenate([k0[...], k1[...], k2[...]], axis=0)
        v = jnp.concatenate([v0[...], v1[...], v2[...]], axis=0)
        head_a = lax.broadcasted_iota(jnp.int32, (1, LANES), 1) < HEAD_DIM
        valid = lax.broadcasted_iota(jnp.int32, (1, KW), 1) >= (2 - b) * QB
        outs = []
        for hh in range(2):
            mine = head_a if hh == 0 else jnp.logical_not(head_a)
            kh = jnp.where(mine, k, jnp.zeros_like(k))
            s = _dot_nt(q, kh) + bias_ref[hh]
            s = jnp.where(valid, s, NEG_INF)
            p = jnp.exp(s - jnp.max(s, axis=-1, keepdims=True))
            l = jnp.sum(p, axis=-1, keepdims=True)
            outs.append(_dot(p.astype(BF16), v) / l)
        o_ref[...] = jnp.where(head_a, outs[0], outs[1]).astype(BF16)

    return pl.pallas_call(
        body, name="attn_fwd", grid=(D_MODEL // LANES, nb),
        in_specs=[pl.BlockSpec((QB, LANES), lambda p, b: (b, p))] + _window_specs(0) + _window_specs(v_col0)
        + [pl.BlockSpec((2, QB, KW), lambda p, b: (p, 0, 0))],
        out_specs=pl.BlockSpec((QB, LANES), lambda p, b: (b, p)),
        out_shape=jax.ShapeDtypeStruct((t, D_MODEL), BF16),
        compiler_params=_params(("parallel", "arbitrary")),
    )(qn, kn, kn, kn, proj, proj, proj, bias)


def _shift_down(u, halo, n):
    rows = lax.broadcasted_iota(jnp.int32, (u.shape[0], 1), 0)
    out = pltpu.roll(u, n, 0)
    for j in range(n):
        out = jnp.where(rows == j, halo[SUBLANES - n + j:SUBLANES - n + j + 1, :], out)
    return out


def _shift_up(u, halo, n):
    tm = u.shape[0]
    rows = lax.broadcasted_iota(jnp.int32, (tm, 1), 0)
    out = pltpu.roll(u, tm - n, 0)
    for j in range(n):
        out = jnp.where(rows == tm - n + j, halo[j:j + 1, :], out)
    return out


def _conv_fwd(proj, conv_w, conv_b):
    t = proj.shape[0]
    tm = min(t, 512)
    hb = tm // SUBLANES

    def body(bg_ref, cg_ref, xc_ref, cgh_ref, xch_ref, w_ref, b_ref, o_ref):
        i = pl.program_id(0)
        u = cg_ref[...].astype(F32) * xc_ref[...].astype(F32)
        halo = cgh_ref[...].astype(F32) * xch_ref[...].astype(F32)
        halo = jnp.where(i > 0, halo, 0.0)
        w = w_ref[...]
        s = w[0:1] * _shift_down(u, halo, 2) + w[1:2] * _shift_down(u, halo, 1) + w[2:3] * u
        o_ref[...] = (bg_ref[...].astype(F32) * (b_ref[...] + s)).astype(BF16)

    def prev(col):
        return pl.BlockSpec((SUBLANES, D_MODEL), lambda i: (jnp.maximum(i * hb - 1, 0), col))

    return pl.pallas_call(
        body, name="conv_fwd", grid=(t // tm,),
        in_specs=[pl.BlockSpec((tm, D_MODEL), lambda i: (i, 3)),
                  pl.BlockSpec((tm, D_MODEL), lambda i: (i, 4)),
                  pl.BlockSpec((tm, D_MODEL), lambda i: (i, 5)),
                  prev(4), prev(5),
                  pl.BlockSpec((3, D_MODEL), lambda i: (0, 0)),
                  pl.BlockSpec((1, D_MODEL), lambda i: (0, 0))],
        out_specs=pl.BlockSpec((tm, D_MODEL), lambda i: (i, 0)),
        out_shape=jax.ShapeDtypeStruct((t, D_MODEL), BF16),
        compiler_params=_params(("parallel",)),
    )(proj, proj, proj, proj, proj, conv_w, conv_b)


def _mix_out(y_attn, y_conv, gates, x, w_ap, w_cp, w_out, g2):
    t = x.shape[0]
    tm = min(t, 512)

    def body(ya_in, yc_in, g_ref, x_ref, wap, wcp, wout, g2_ref, ya_ref, yc_ref, mg_ref, x1_ref, h2_ref):
        ya = _dot(ya_in[...], wap[...])
        yc = _dot(yc_in[...], wcp[...])
        ya_ref[...] = ya.astype(BF16)
        yc_ref[...] = yc.astype(BF16)
        merged = (g_ref[:, :D_MODEL].astype(F32) * ya + g_ref[:, D_MODEL:].astype(F32) * yc).astype(BF16)
        mg_ref[...] = merged
        x1 = x_ref[...] + _dot(merged, wout[...])
        x1_ref[...] = x1
        r = lax.rsqrt(jnp.mean(x1 * x1, axis=-1, keepdims=True) + EPS)
        h2_ref[...] = (x1 * r * g2_ref[...]).astype(BF16)

    row = pl.BlockSpec((tm, D_MODEL), lambda i: (i, 0))
    full = pl.BlockSpec((D_MODEL, D_MODEL), lambda i: (0, 0))
    return pl.pallas_call(
        body, name="mix_out", grid=(t // tm,),
        in_specs=[row, row, pl.BlockSpec((tm, 2 * D_MODEL), lambda i: (i, 0)), row, full, full, full,
                  pl.BlockSpec((1, D_MODEL), lambda i: (0, 0))],
        out_specs=[row] * 5,
        out_shape=[jax.ShapeDtypeStruct((t, D_MODEL), BF16)] * 3
        + [jax.ShapeDtypeStruct((t, D_MODEL), F32), jax.ShapeDtypeStruct((t, D_MODEL), BF16)],
        compiler_params=_params(("parallel",)),
    )(y_attn, y_conv, gates, x, w_ap, w_cp, w_out, g2)


def _mlp_fwd(h2, w_up, w_down, x1, target):
    t = h2.shape[0]
    tm = min(t, 512)
    tf = 1024
    nf = D_FF // tf

    def body(h2_ref, wup, wdn, x1_ref, tg_ref, a_ref, dy_ref, dyb_ref, loss_ref, acc):
        i, j = pl.program_id(0), pl.program_id(1)
        a = _dot(h2_ref[...], wup[...])
        a_ref[...] = a.astype(BF16)
        u = jnp.square(jnp.maximum(a, 0.0)).astype(BF16)
        part = _dot(u, wdn[...])

        @pl.when(j == 0)
        def _():
            acc[...] = part

        @pl.when(j > 0)
        def _():
            acc[...] += part

        @pl.when((i == 0) & (j == 0))
        def _():
            loss_ref[...] = jnp.zeros_like(loss_ref)

        @pl.when(j == nf - 1)
        def _():
            diff = x1_ref[...] + acc[...] - tg_ref[...]
            loss_ref[...] += _fold8(diff * diff)
            dy = diff * (1.0 / D_MODEL)
            dy_ref[...] = dy
            dyb_ref[...] = dy.astype(BF16)

    row = pl.BlockSpec((tm, D_MODEL), lambda i, j: (i, 0))
    return pl.pallas_call(
        body, name="mlp_fwd", grid=(t // tm, nf),
        in_specs=[row, pl.BlockSpec((D_MODEL, tf), lambda i, j: (0, j)),
                  pl.BlockSpec((tf, D_MODEL), lambda i, j: (j, 0)), row, row],
        out_specs=[pl.BlockSpec((tm, tf), lambda i, j: (i, j)), row, row,
                   pl.BlockSpec((SUBLANES, D_MODEL), lambda i, j: (0, 0))],
        out_shape=[jax.ShapeDtypeStruct((t, D_FF), BF16), jax.ShapeDtypeStruct((t, D_MODEL), F32),
                   jax.ShapeDtypeStruct((t, D_MODEL), BF16), jax.ShapeDtypeStruct((SUBLANES, D_MODEL), F32)],
        scratch_shapes=[pltpu.VMEM((tm, D_MODEL), F32)],
        compiler_params=_params(("arbitrary", "arbitrary")),
    )(h2, w_up, w_down, x1, target)


def _rmsnorm_bwd(xf, g, dh):
    r = lax.rsqrt(jnp.mean(xf * xf, axis=-1, keepdims=True) + EPS)
    xh = xf * r
    dxh = dh * g
    dx = r * (dxh - xh * jnp.mean(dxh * xh, axis=-1, keepdims=True))
    return dx, dh * xh


def _mlp_bwd(dyb, a, w_down, w_up, x1, dy, g2):
    t = dyb.shape[0]
    tm = min(t, 512)
    tf = 1024
    nf = D_FF // tf

    def body(dyb_ref, a_ref, wdn, wup, x1_ref, dy_ref, g2_ref, da_ref, dx1_ref, dx1b_ref, dg2_ref, acc):
        i, j = pl.program_id(0), pl.program_id(1)
        du = _dot_nt(dyb_ref[...], wdn[...])
        da = (du * (2.0 * jnp.maximum(a_ref[...].astype(F32), 0.0))).astype(BF16)
        da_ref[...] = da
        part = _dot_nt(da, wup[...])

        @pl.when(j == 0)
        def _():
            acc[...] = part

        @pl.when(j > 0)
        def _():
            acc[...] += part

        @pl.when((i == 0) & (j == 0))
        def _():
            dg2_ref[...] = jnp.zeros_like(dg2_ref)

        @pl.when(j == nf - 1)
        def _():
            dx, dg = _rmsnorm_bwd(x1_ref[...], g2_ref[...], acc[...])
            dx1 = dy_ref[...] + dx
            dx1_ref[...] = dx1
            dx1b_ref[...] = dx1.astype(BF16)
            dg2_ref[...] += _fold8(dg)

    row = pl.BlockSpec((tm, D_MODEL), lambda i, j: (i, 0))
    return pl.pallas_call(
        body, name="mlp_bwd", grid=(t // tm, nf),
        in_specs=[row, pl.BlockSpec((tm, tf), lambda i, j: (i, j)),
                  pl.BlockSpec((tf, D_MODEL), lambda i, j: (j, 0)),
                  pl.BlockSpec((D_MODEL, tf), lambda i, j: (0, j)), row, row,
                  pl.BlockSpec((1, D_MODEL), lambda i, j: (0, 0))],
        out_specs=[pl.BlockSpec((tm, tf), lambda i, j: (i, j)), row, row,
                   pl.BlockSpec((SUBLANES, D_MODEL), lambda i, j: (0, 0))],
        out_shape=[jax.ShapeDtypeStruct((t, D_FF), BF16), jax.ShapeDtypeStruct((t, D_MODEL), F32),
                   jax.ShapeDtypeStruct((t, D_MODEL), BF16), jax.ShapeDtypeStruct((SUBLANES, D_MODEL), F32)],
        scratch_shapes=[pltpu.VMEM((tm, D_MODEL), F32)],
        compiler_params=_params(("arbitrary", "arbitrary")),
    )(dyb, a, w_down, w_up, x1, dy, g2)


def _wgrad(name, lhs, rhs_list, rhs_slabs, relu_sq=False):
    t, m = lhs.shape
    tt = min(t, 512)
    tmo = min(m, 1024)
    n_slab = sum(rhs_slabs)
    starts = [sum(rhs_slabs[:n]) for n in range(len(rhs_slabs))]
    n_rhs = len(rhs_list)

    def body(*refs):
        l_ref, r_refs, o_ref, acc = refs[0], refs[1:1 + n_rhs], refs[1 + n_rhs], refs[2 + n_rhs]
        k, s = pl.program_id(1), pl.program_id(2)
        lv = l_ref[...]
        if relu_sq:
            lv = jnp.square(jnp.maximum(lv.astype(F32), 0.0)).astype(BF16)

        @pl.when(s == 0)
        def _():
            acc[...] = jnp.zeros_like(acc)

        for n in range(n_rhs):
            @pl.when((k >= starts[n]) & (k < starts[n] + rhs_slabs[n]))
            def _(n=n):
                acc[...] += _dot_tn(lv, r_refs[n][...])

        @pl.when(s == pl.num_programs(2) - 1)
        def _():
            o_ref[...] = acc[...].astype(BF16)

    def rhs_spec(n):
        lo, cnt = starts[n], rhs_slabs[n]

        def index(i, k, s):
            inside = (k >= lo) & (k < lo + cnt)
            return (jnp.where(inside, s, 0), jnp.clip(k - lo, 0, cnt - 1))
        return pl.BlockSpec((tt, D_MODEL), index)

    return pl.pallas_call(
        body, name=name, grid=(m // tmo, n_slab, t // tt),
        in_specs=[pl.BlockSpec((tt, tmo), lambda i, k, s: (s, i))] + [rhs_spec(n) for n in range(n_rhs)],
        out_specs=pl.BlockSpec((tmo, D_MODEL), lambda i, k, s: (i, k)),
        out_shape=jax.ShapeDtypeStruct((m, n_slab * D_MODEL), BF16),
        scratch_shapes=[pltpu.VMEM((tmo, D_MODEL), F32)],
        compiler_params=_params(("parallel", "parallel", "arbitrary")),
    )(lhs, *rhs_list)


def _mix_bwd(dx1b, gates, ya, yc, w_out, w_ap, w_cp, w_g):
    t = dx1b.shape[0]
    tm = min(t, 512)

    def body(dx_ref, g_ref, ya_ref, yc_ref, wout, wap, wcp, wg,
             dgp_ref, dya_ref, dyc_ref, dyat_ref, dycv_ref, dhg_ref, dbg_ref):
        dm = _dot_nt(dx_ref[...], wout[...])
        ga = g_ref[:, :D_MODEL].astype(F32)
        gc = g_ref[:, D_MODEL:].astype(F32)
        dya = (dm * ga).astype(BF16)
        dyc = (dm * gc).astype(BF16)
        dya_ref[...] = dya
        dyc_ref[...] = dyc
        dgpa = dm * ya_ref[...].astype(F32) * ga * (1.0 - ga)
        dgpc = dm * yc_ref[...].astype(F32) * gc * (1.0 - gc)

        @pl.when(pl.program_id(0) == 0)
        def _():
            dbg_ref[...] = jnp.zeros_like(dbg_ref)

        dbg_ref[:, :D_MODEL] += _fold8(dgpa)
        dbg_ref[:, D_MODEL:] += _fold8(dgpc)
        dgpa = dgpa.astype(BF16)
        dgpc = dgpc.astype(BF16)
        dgp_ref[:, :D_MODEL] = dgpa
        dgp_ref[:, D_MODEL:] = dgpc
        dyat_ref[...] = _dot_nt(dya, wap[...]).astype(BF16)
        dycv_ref[...] = _dot_nt(dyc, wcp[...]).astype(BF16)
        dhg_ref[...] = _dot_nt(dgpa, wg[:, :D_MODEL]) + _dot_nt(dgpc, wg[:, D_MODEL:])

    row = pl.BlockSpec((tm, D_MODEL), lambda i: (i, 0))
    row2 = pl.BlockSpec((tm, 2 * D_MODEL), lambda i: (i, 0))
    full = pl.BlockSpec((D_MODEL, D_MODEL), lambda i: (0, 0))
    return pl.pallas_call(
        body, name="mix_bwd", grid=(t // tm,),
        in_specs=[row, row2, row, row, full, full, full, pl.BlockSpec((D_MODEL, 2 * D_MODEL), lambda i: (0, 0))],
        out_specs=[row2, row, row, row, row, row, pl.BlockSpec((SUBLANES, 2 * D_MODEL), lambda i: (0, 0))],
        out_shape=[jax.ShapeDtypeStruct((t, 2 * D_MODEL), BF16)] + [jax.ShapeDtypeStruct((t, D_MODEL), BF16)] * 4
        + [jax.ShapeDtypeStruct((t, D_MODEL), F32), jax.ShapeDtypeStruct((SUBLANES, 2 * D_MODEL), F32)],
        compiler_params=_params(("arbitrary",)),
    )(dx1b, gates, ya, yc, w_out, w_ap, w_cp, w_g)


def _conv_bwd(dyconv, proj, conv_w, conv_b):
    t = proj.shape[0]
    tm = min(t, 512)
    hb = tm // SUBLANES
    last = t // SUBLANES - 1

    def body(dy_ref, dyn_ref, bg_ref, bgn_ref, cg_ref, cgp_ref, xc_ref, xcp_ref, w_ref, b_ref,
             o_ref, dcb_ref, dcw_ref):
        i = pl.program_id(0)
        cg = cg_ref[...].astype(F32)
        xc = xc_ref[...].astype(F32)
        bg = bg_ref[...].astype(F32)
        u = cg * xc
        prev = jnp.where(i > 0, cgp_ref[...].astype(F32) * xcp_ref[...].astype(F32), 0.0)
        u1 = _shift_down(u, prev, 1)
        u2 = _shift_down(u, prev, 2)
        w = w_ref[...]
        conv = b_ref[...] + (w[0:1] * u2 + w[1:2] * u1 + w[2:3] * u)
        dy = dy_ref[...].astype(F32)
        dconv = dy * bg
        nxt = jnp.where(i < pl.num_programs(0) - 1, dyn_ref[...].astype(F32) * bgn_ref[...].astype(F32), 0.0)
        du = w[2:3] * dconv + w[1:2] * _shift_up(dconv, nxt, 1) + w[0:1] * _shift_up(dconv, nxt, 2)
        o_ref[:, :D_MODEL] = (dy * conv).astype(BF16)
        o_ref[:, D_MODEL:2 * D_MODEL] = (du * xc).astype(BF16)
        o_ref[:, 2 * D_MODEL:] = (du * cg).astype(BF16)

        @pl.when(i == 0)
        def _():
            dcb_ref[...] = jnp.zeros_like(dcb_ref)
            dcw_ref[...] = jnp.zeros_like(dcw_ref)

        dcb_ref[...] += _fold8(dconv)
        dcw_ref[0:SUBLANES] += _fold8(dconv * u2)
        dcw_ref[SUBLANES:2 * SUBLANES] += _fold8(dconv * u1)
        dcw_ref[2 * SUBLANES:] += _fold8(dconv * u)

    def prev(col):
        return pl.BlockSpec((SUBLANES, D_MODEL), lambda i: (jnp.maximum(i * hb - 1, 0), col))

    def nxt(col):
        return pl.BlockSpec((SUBLANES, D_MODEL), lambda i: (jnp.minimum((i + 1) * hb, last), col))

    def cur(col):
        return pl.BlockSpec((tm, D_MODEL), lambda i: (i, col))

    return pl.pallas_call(
        body, name="conv_bwd", grid=(t // tm,),
        in_specs=[cur(0), nxt(0), cur(3), nxt(3), cur(4), prev(4), cur(5), prev(5),
                  pl.BlockSpec((3, D_MODEL), lambda i: (0, 0)), pl.BlockSpec((1, D_MODEL), lambda i: (0, 0))],
        out_specs=[pl.BlockSpec((tm, 3 * D_MODEL), lambda i: (i, 0)),
                   pl.BlockSpec((SUBLANES, D_MODEL), lambda i: (0, 0)),
                   pl.BlockSpec((3 * SUBLANES, D_MODEL), lambda i: (0, 0))],
        out_shape=[jax.ShapeDtypeStruct((t, 3 * D_MODEL), BF16), jax.ShapeDtypeStruct((SUBLANES, D_MODEL), F32),
                   jax.ShapeDtypeStruct((3 * SUBLANES, D_MODEL), F32)],
        compiler_params=_params(("arbitrary",)),
    )(dyconv, dyconv, proj, proj, proj, proj, proj, proj, conv_w, conv_b)


def _attn_bwd(qn, kn, proj, dyattn, bias):
    t = qn.shape[0]
    nb = t // QB
    v_col0 = 2 * D_MODEL // LANES

    def body(q_ref, k0, k1, k2, v0, v1, v2, do_ref, bias_ref, dq_ref, dk_ref, dv_ref, db_ref, acck, accv):
        b = pl.program_id(1)

        @pl.when(b == 0)
        def _():
            acck[...] = jnp.zeros_like(acck)
            accv[...] = jnp.zeros_like(accv)
            db_ref[...] = jnp.zeros_like(db_ref)

        @pl.when(b < nb)
        def _():
            q = q_ref[...]
            do = do_ref[...]
            k = jnp.concatenate([k0[...], k1[...], k2[...]], axis=0)
            v = jnp.concatenate([v0[...], v1[...], v2[...]], axis=0)
            head_a = lax.broadcasted_iota(jnp.int32, (1, LANES), 1) < HEAD_DIM
            valid = lax.broadcasted_iota(jnp.int32, (1, KW), 1) >= (2 - b) * QB
            dq = jnp.zeros((QB, LANES), F32)
            dkw = [jnp.zeros((QB, LANES), F32) for _ in range(3)]
            dvw = [jnp.zeros((QB, LANES), F32) for _ in range(3)]
            for hh in range(2):
                mine = head_a if hh == 0 else jnp.logical_not(head_a)
                kh = jnp.where(mine, k, jnp.zeros_like(k))
                vh = jnp.where(mine, v, jnp.zeros_like(v))
                qh = jnp.where(mine, q, jnp.zeros_like(q))
                doh = jnp.where(mine, do, jnp.zeros_like(do))
                s = _dot_nt(q, kh) + bias_ref[hh]
                s = jnp.where(valid, s, NEG_INF)
                p = jnp.exp(s - jnp.max(s, axis=-1, keepdims=True))
                p = p / jnp.sum(p, axis=-1, keepdims=True)
                dp = _dot_nt(do, vh)
                ds = p * (dp - jnp.sum(p * dp, axis=-1, keepdims=True))
                db_ref[hh] += ds
                pb = p.astype(BF16)
                dsb = ds.astype(BF16)
                dq = dq + _dot(dsb, kh)
                for w in range(3):
                    cols = slice(w * QB, (w + 1) * QB)
                    dkw[w] = dkw[w] + _dot_tn(dsb[:, cols], qh)
                    dvw[w] = dvw[w] + _dot_tn(pb[:, cols], doh)
            dq_ref[...] = dq
            for w in range(3):
                slot = lax.rem(b + w + 1, 3)
                if w == 2:
                    acck[slot] = dkw[w]
                    accv[slot] = dvw[w]
                else:
                    acck[slot] += dkw[w]
                    accv[slot] += dvw[w]

        done = lax.rem(b + 1, 3)
        dk_ref[...] = acck[done]
        dv_ref[...] = accv[done].astype(BF16)

    def cur(p, b):
        return (jnp.minimum(b, nb - 1), p)

    def window(col0):
        return [pl.BlockSpec((QB, LANES), functools.partial(
            lambda p, b, back: (jnp.maximum(jnp.minimum(b, nb - 1) - back, 0), col0 + p), back=back))
            for back in (2, 1, 0)]

    def done_block(p, b):
        return (jnp.maximum(b - 2, 0), p)

    return pl.pallas_call(
        body, name="attn_bwd", grid=(D_MODEL // LANES, nb + 2),
        in_specs=[pl.BlockSpec((QB, LANES), cur)] + window(0) + window(v_col0)
        + [pl.BlockSpec((QB, LANES), cur), pl.BlockSpec((2, QB, KW), lambda p, b: (p, 0, 0))],
        out_specs=[pl.BlockSpec((QB, LANES), cur), pl.BlockSpec((QB, LANES), done_block),
                   pl.BlockSpec((QB, LANES), done_block), pl.BlockSpec((2, QB, KW), lambda p, b: (p, 0, 0))],
        out_shape=[jax.ShapeDtypeStruct((t, D_MODEL), F32), jax.ShapeDtypeStruct((t, D_MODEL), F32),
                   jax.ShapeDtypeStruct((t, D_MODEL), BF16), jax.ShapeDtypeStruct((N_HEADS, QB, KW), F32)],
        scratch_shapes=[pltpu.VMEM((3, QB, LANES), F32), pltpu.VMEM((3, QB, LANES), F32)],
        compiler_params=_params(("parallel", "arbitrary")),
    )(qn, kn, kn, kn, proj, proj, proj, dyattn, bias)


def _qknorm_bwd(proj, dqn, dkn, gq, gk):
    t = proj.shape[0]
    tm = min(t, 512)
    scale = HEAD_DIM ** -0.5

    def body(q_ref, k_ref, dqn_ref, dkn_ref, gq_ref, gk_ref, o_ref, dgq_ref, dgk_ref):
        e = _head_sum_matrix()

        @pl.when(pl.program_id(0) == 0)
        def _():
            dgq_ref[...] = jnp.zeros_like(dgq_ref)
            dgk_ref[...] = jnp.zeros_like(dgk_ref)

        for n, (src, dn_ref, g_ref, dg_ref, sc) in enumerate(
                ((q_ref, dqn_ref, gq_ref, dgq_ref, scale), (k_ref, dkn_ref, gk_ref, dgk_ref, 1.0))):
            for s in range(D_MODEL // LANES):
                sl = slice(s * LANES, (s + 1) * LANES)
                xf = src[:, sl].astype(F32)
                r = lax.rsqrt(_head_sums(xf * xf, e) * (1.0 / HEAD_DIM) + EPS)
                xh = xf * r
                dn = dn_ref[:, sl] * sc
                dg_ref[:, sl] += _fold8(dn * xh)
                dxh = dn * g_ref[:, sl]
                mean = _head_sums(dxh * xh, e) * (1.0 / HEAD_DIM)
                o_ref[:, n * D_MODEL + s * LANES:n * D_MODEL + (s + 1) * LANES] = (r * (dxh - xh * mean)).astype(BF16)

    row = pl.BlockSpec((tm, D_MODEL), lambda i: (i, 0))
    vec = pl.BlockSpec((1, D_MODEL), lambda i: (0, 0))
    acc = pl.BlockSpec((SUBLANES, D_MODEL), lambda i: (0, 0))
    return pl.pallas_call(
        body, name="qknorm_bwd", grid=(t // tm,),
        in_specs=[row, pl.BlockSpec((tm, D_MODEL), lambda i: (i, 1)), row, row, vec, vec],
        out_specs=[pl.BlockSpec((tm, 2 * D_MODEL), lambda i: (i, 0)), acc, acc],
        out_shape=[jax.ShapeDtypeStruct((t, 2 * D_MODEL), BF16)] + [jax.ShapeDtypeStruct((SUBLANES, D_MODEL), F32)] * 2,
        compiler_params=_params(("arbitrary",)),
    )(proj, proj, dqn, dkn, gq, gk)


def _in_bwd(dqk, dv, dconv, w_in, dhg, x, g1, dx1):
    t = x.shape[0]
    tm = min(t, 512)

    def body(dqk_ref, dv_ref, dc_ref, w_ref, dhg_ref, x_ref, g_ref, dx1_ref, dx_ref, dg_ref, acc):
        i, k = pl.program_id(0), pl.program_id(1)

        @pl.when(k == 0)
        def _():
            acc[...] = dhg_ref[...]

        @pl.when(k < 2)
        def _():
            acc[...] += _dot_nt(dqk_ref[...], w_ref[...])

        @pl.when(k == 2)
        def _():
            acc[...] += _dot_nt(dv_ref[...], w_ref[...])

        @pl.when(k > 2)
        def _():
            acc[...] += _dot_nt(dc_ref[...], w_ref[...])

        @pl.when((i == 0) & (k == 0))
        def _():
            dg_ref[...] = jnp.zeros_like(dg_ref)

        @pl.when(k == 5)
        def _():
            dx, dg = _rmsnorm_bwd(x_ref[...], g_ref[...], acc[...])
            dx_ref[...] = dx1_ref[...] + dx
            dg_ref[...] += _fold8(dg)

    row = pl.BlockSpec((tm, D_MODEL), lambda i, k: (i, 0))
    return pl.pallas_call(
        body, name="in_bwd", grid=(t // tm, 6),
        in_specs=[pl.BlockSpec((tm, D_MODEL), lambda i, k: (i, jnp.minimum(k, 1))), row,
                  pl.BlockSpec((tm, D_MODEL), lambda i, k: (i, jnp.clip(k - 3, 0, 2))),
                  pl.BlockSpec((D_MODEL, D_MODEL), lambda i, k: (0, k)), row, row,
                  pl.BlockSpec((1, D_MODEL), lambda i, k: (0, 0)), row],
        out_specs=[row, pl.BlockSpec((SUBLANES, D_MODEL), lambda i, k: (0, 0))],
        out_shape=[jax.ShapeDtypeStruct((t, D_MODEL), F32), jax.ShapeDtypeStruct((SUBLANES, D_MODEL), F32)],
        scratch_shapes=[pltpu.VMEM((tm, D_MODEL), F32)],
        compiler_params=_params(("arbitrary", "arbitrary")),
    )(dqk, dv, dconv, w_in, dhg, x, g1, dx1)


def _skew_bias_grad(dbias):
    padded = jnp.pad(dbias, ((0, 0), (0, 0), (0, SKEW - 1 - KW)))
    flat = jnp.pad(padded.reshape(N_HEADS, QB * (SKEW - 1)), ((0, 0), (0, QB)))
    return flat.reshape(N_HEADS, QB, SKEW)


def _small_partials(dg1, dgq, dgk, dcb, dcw, dbg, dg2, dbias_skew, loss_tile):
    def head_fold(v):
        acc = v[:, 0:LANES]
        for s in range(1, D_MODEL // LANES):
            acc = acc + v[:, s * LANES:(s + 1) * LANES]
        return acc + pltpu.roll(acc, HEAD_DIM, 1)

    def body(dg1_ref, dgq_ref, dgk_ref, dcb_ref, dcw_ref, dbg_ref, dg2_ref, db_ref, loss_ref, o_ref):
        col = lax.broadcasted_iota(jnp.int32, (1, SKEW), 1)
        far = (col <= MAX_REL) | (col > KW)
        o_ref[...] = jnp.zeros_like(o_ref)
        o_ref[0:1, :] = jnp.sum(dg1_ref[...], axis=0, keepdims=True)
        o_ref[1:2, 0:LANES] = head_fold(jnp.sum(dgq_ref[...], axis=0, keepdims=True))
        o_ref[2:3, 0:LANES] = head_fold(jnp.sum(dgk_ref[...], axis=0, keepdims=True))
        o_ref[3:4, :] = jnp.sum(dcb_ref[...], axis=0, keepdims=True)
        for j in range(3):
            o_ref[4 + j:5 + j, :] = jnp.sum(dcw_ref[j * SUBLANES:(j + 1) * SUBLANES, :], axis=0, keepdims=True)
        o_ref[7:8, :] = jnp.sum(dbg_ref[:, :D_MODEL], axis=0, keepdims=True)
        o_ref[8:9, :] = jnp.sum(dbg_ref[:, D_MODEL:], axis=0, keepdims=True)
        o_ref[9:10, :] = jnp.sum(dg2_ref[...], axis=0, keepdims=True)
        for h in range(N_HEADS):
            diag = jnp.sum(db_ref[h], axis=0, keepdims=True)
            far_sum = jnp.sum(jnp.where(far, diag, 0.0), axis=-1, keepdims=True)
            o_ref[10 + h:11 + h, :] = jnp.where(col == KW, far_sum, diag)
        loss = (0.5 / D_MODEL) * jnp.sum(jnp.sum(loss_ref[...], axis=0, keepdims=True), axis=-1, keepdims=True)
        o_ref[26:27, :] = jnp.broadcast_to(loss, (1, D_MODEL))

    return pl.pallas_call(
        body, name="small_partials",
        out_shape=jax.ShapeDtypeStruct((32, D_MODEL), F32),
        compiler_params=_params(),
    )(dg1, dgq, dgk, dcb, dcw, dbg, dg2, dbias_skew, loss_tile)


def _local_step(x, target, norm1_g, q_norm_g, k_norm_g, rel_bias, conv_w, conv_b, b_gate, norm2_g,
                w_in, w_ap, w_cp, w_g, w_out, w_up, w_down):
    g1 = norm1_g.reshape(1, D_MODEL)
    g2 = norm2_g.reshape(1, D_MODEL)
    gq = jnp.tile(q_norm_g, N_HEADS).reshape(1, D_MODEL)
    gk = jnp.tile(k_norm_g, N_HEADS).reshape(1, D_MODEL)
    cb = conv_b.reshape(1, D_MODEL)
    bias = _bias_tiles(rel_bias)

    proj, h = _in_proj(x, g1, w_in)
    gates = _gate_proj(h, w_g, b_gate.reshape(1, 2 * D_MODEL))
    qn, kn = _qknorm_fwd(proj, gq, gk)
    y_attn = _attn_fwd(qn, kn, proj, bias)
    y_conv = _conv_fwd(proj, conv_w, cb)
    ya, yc, merged, x1, h2 = _mix_out(y_attn, y_conv, gates, x, w_ap, w_cp, w_out, g2)
    a, dy, dyb, loss_tile = _mlp_fwd(h2, w_up, w_down, x1, target)

    da, dx1, dx1b, dg2 = _mlp_bwd(dyb, a, w_down, w_up, x1, dy, g2)
    gw_down = _wgrad("wgrad_down", a, [dyb], [1], relu_sq=True)
    gw_up = _wgrad("wgrad_up", h2, [da], [D_FF // D_MODEL])
    dgp, dya, dyc, dyattn, dyconv, dhg, dbg = _mix_bwd(dx1b, gates, ya, yc, w_out, w_ap, w_cp, w_g)
    gw_out = _wgrad("wgrad_out", merged, [dx1b], [1])
    gw_ap = _wgrad("wgrad_attn_proj", y_attn, [dya], [1])
    gw_cp = _wgrad("wgrad_conv_proj", y_conv, [dyc], [1])
    gw_g = _wgrad("wgrad_gate", h, [dgp], [2])
    dconv, dcb, dcw = _conv_bwd(dyconv, proj, conv_w, cb)
    dqn, dkn, dv, dbias = _attn_bwd(qn, kn, proj, dyattn, bias)
    dqk, dgq, dgk = _qknorm_bwd(proj, dqn, dkn, gq, gk)
    gw_in = _wgrad("wgrad_in", h, [dqk, dv, dconv], [2, 1, 3])
    dx, dg1 = _in_bwd(dqk, dv, dconv, w_in, dhg, x, g1, dx1)
    small = _small_partials(dg1, dgq, dgk, dcb, dcw, dbg, dg2, _skew_bias_grad(dbias), loss_tile)
    return dx, (gw_in, gw_ap, gw_cp, gw_g, gw_out, gw_up, gw_down), small


def _me():
    return lax.axis_index("x"), lax.axis_index("y"), lax.axis_index("c")


def _peer(me, rel):
    x, y, c = me
    return (1 - x if rel & 4 else x, 1 - y if rel & 2 else y, 1 - c if rel & 1 else c)


def _linear(dev):
    return 4 * dev[0] + 2 * dev[1] + dev[2]


BIG_AXES = (1, 0, 0, 1, 0, 1, 0)


def _block(ref, axis, idx, size):
    return ref.at[pl.ds(idx * size, size), :] if axis == 0 else ref.at[:, pl.ds(idx * size, size)]


def _cast_shards(shards):
    def body(*refs):
        for src, dst in zip(refs[:len(shards)], refs[len(shards):]):
            dst[...] = src[...].astype(BF16)

    return pl.pallas_call(
        body, name="cast_shards",
        out_shape=[jax.ShapeDtypeStruct(s.shape, BF16) for s in shards],
        compiler_params=_params(),
    )(*shards)


def _all_gather(shards, axes):
    n = len(shards)
    sizes = [s.shape[axis] for s, axis in zip(shards, axes)]

    def body(*refs):
        srcs, outs = refs[:n], refs[n:2 * n]
        send_sems, recv_sems, local_sems = refs[2 * n:]
        me = _me()
        sibling = _peer(me, 1)
        chips = [_peer(me, rel) for rel in (4, 2, 6)]

        def rows(a, dev):
            return _block(outs[a], axes[a], _linear(dev), sizes[a])

        def copy(a, k, block_dev, to, src=None):
            return pltpu.make_async_remote_copy(
                src_ref=rows(a, block_dev) if src is None else src, dst_ref=rows(a, block_dev),
                send_sem=send_sems.at[a, k], recv_sem=recv_sems.at[a, k], device_id=to, device_id_type=MESH_T)

        started = []
        for a in range(n):
            own = pltpu.make_async_copy(srcs[a], rows(a, me), local_sems.at[a])
            own.start()
            started.append(own)
        first = []
        for a in range(n):
            first.append(copy(a, 0, me, sibling, src=srcs[a]))
            for j, chip in enumerate(chips):
                first.append(copy(a, 1 + j, me, chip, src=srcs[a]))
        for cp in first:
            cp.start()
        passed = []
        for a in range(n):
            for j, chip in enumerate(chips):
                copy(a, 1 + j, chip, me).wait_recv()
                fwd = copy(a, 4 + j, chip, sibling)
                fwd.start()
                passed.append(fwd)
        for a in range(n):
            copy(a, 0, sibling, me).wait_recv()
            for j, chip in enumerate(chips):
                copy(a, 4 + j, _peer(chip, 1), me).wait_recv()
        for cp in first + passed:
            cp.wait_send()
        for own in started:
            own.wait()

    any_spec = pl.BlockSpec(memory_space=pl.ANY)
    out_shapes = []
    for s, axis in zip(shards, axes):
        shape = (s.shape[0] * N_DEV, s.shape[1]) if axis == 0 else (s.shape[0], s.shape[1] * N_DEV)
        out_shapes.append(jax.ShapeDtypeStruct(shape, s.dtype))
    return pl.pallas_call(
        body, name="all_gather_weights",
        in_specs=[any_spec] * n, out_specs=[any_spec] * n, out_shape=out_shapes,
        scratch_shapes=[pltpu.SemaphoreType.DMA((n, 7)), pltpu.SemaphoreType.DMA((n, 7)),
                        pltpu.SemaphoreType.DMA((n,))],
    )(*shards)


def _reduce_scatter(grads):
    n = len(grads)
    sizes = [g.shape[axis] // N_DEV for g, axis in zip(grads, BIG_AXES)]

    def body(*refs):
        srcs, outs = refs[:n], refs[n:2 * n]
        send_sems, recv_sems, local_sems = refs[2 * n:]
        me = _me()
        my_idx = _linear(me)
        started = []
        for a in range(n):
            own = pltpu.make_async_copy(_block(srcs[a], BIG_AXES[a], my_idx, sizes[a]), outs[a].at[my_idx],
                                        local_sems.at[a])
            own.start()
            started.append(own)
        sends = []
        for a in range(n):
            for rel in range(1, N_DEV):
                to = _peer(me, rel)
                cp = pltpu.make_async_remote_copy(
                    src_ref=_block(srcs[a], BIG_AXES[a], _linear(to), sizes[a]), dst_ref=outs[a].at[my_idx],
                    send_sem=send_sems.at[a, rel - 1], recv_sem=recv_sems.at[a, rel - 1],
                    device_id=to, device_id_type=MESH_T)
                cp.start()
                sends.append(cp)
        for a in range(n):
            for rel in range(1, N_DEV):
                frm = _peer(me, rel)
                pltpu.make_async_remote_copy(
                    src_ref=_block(srcs[a], BIG_AXES[a], my_idx, sizes[a]), dst_ref=outs[a].at[_linear(frm)],
                    send_sem=send_sems.at[a, rel - 1], recv_sem=recv_sems.at[a, rel - 1],
                    device_id=frm, device_id_type=MESH_T).wait_recv()
        for cp in sends:
            cp.wait_send()
        for own in started:
            own.wait()

    any_spec = pl.BlockSpec(memory_space=pl.ANY)
    out_shapes = []
    for g, axis in zip(grads, BIG_AXES):
        shard = (g.shape[0] // N_DEV, g.shape[1]) if axis == 0 else (g.shape[0], g.shape[1] // N_DEV)
        out_shapes.append(jax.ShapeDtypeStruct((N_DEV,) + shard, BF16))
    return pl.pallas_call(
        body, name="reduce_scatter_grads",
        in_specs=[any_spec] * n, out_specs=[any_spec] * n, out_shape=out_shapes,
        scratch_shapes=[pltpu.SemaphoreType.DMA((n, 7)), pltpu.SemaphoreType.DMA((n, 7)),
                        pltpu.SemaphoreType.DMA((n,))],
    )(*grads)


def _all_reduce_small(part):
    def body(p_ref, o_ref, slots, send_sems, recv_sems):
        me = _me()
        my_idx = _linear(me)
        slots[my_idx] = p_ref[...]
        sends = []
        for rel in range(1, N_DEV):
            cp = pltpu.make_async_remote_copy(
                src_ref=p_ref, dst_ref=slots.at[my_idx], send_sem=send_sems.at[rel - 1],
                recv_sem=recv_sems.at[rel - 1], device_id=_peer(me, rel), device_id_type=MESH_T)
            cp.start()
            sends.append(cp)
        for rel in range(1, N_DEV):
            frm = _peer(me, rel)
            pltpu.make_async_remote_copy(
                src_ref=p_ref, dst_ref=slots.at[_linear(frm)], send_sem=send_sems.at[rel - 1],
                recv_sem=recv_sems.at[rel - 1], device_id=frm, device_id_type=MESH_T).wait_recv()
        for cp in sends:
            cp.wait_send()
        total = slots[0]
        for d in range(1, N_DEV):
            total = total + slots[d]
        o_ref[...] = total

    return pl.pallas_call(
        body, name="all_reduce_small",
        in_specs=[pl.BlockSpec(memory_space=pltpu.VMEM)], out_specs=pl.BlockSpec(memory_space=pltpu.VMEM),
        out_shape=jax.ShapeDtypeStruct(part.shape, F32),
        scratch_shapes=[pltpu.VMEM((N_DEV,) + part.shape, F32), pltpu.SemaphoreType.DMA((7,)),
                        pltpu.SemaphoreType.DMA((7,))],
        compiler_params=_params(),
    )(part)


def _adamw_math(w, g, m, v):
    m = ADAM_B1 * m + (1.0 - ADAM_B1) * g
    v = ADAM_B2 * v + (1.0 - ADAM_B2) * jnp.square(g)
    m_hat = m / (1.0 - ADAM_B1 ** ADAM_STEP)
    v_hat = v / (1.0 - ADAM_B2 ** ADAM_STEP)
    delta = -ADAM_LR * (m_hat / (jnp.sqrt(v_hat) + ADAM_EPS) + ADAM_WD * w)
    return delta, m, v


def _adamw_big(name, shares, w, m, v):
    rows, cols = w.shape
    tr = min(rows, 256)

    def body(s_ref, w_ref, m_ref, v_ref, g_ref, d_ref, nm_ref, nv_ref):
        g = s_ref[0].astype(F32)
        for d in range(1, N_DEV):
            g = g + s_ref[d].astype(F32)
        g_ref[...] = g
        d_ref[...], nm_ref[...], nv_ref[...] = _adamw_math(w_ref[...], g, m_ref[...], v_ref[...])

    blk = pl.BlockSpec((tr, cols), lambda i: (i, 0))
    return pl.pallas_call(
        body, name=name, grid=(rows // tr,),
        in_specs=[pl.BlockSpec((N_DEV, tr, cols), lambda i: (0, i, 0)), blk, blk, blk],
        out_specs=[blk] * 4, out_shape=[jax.ShapeDtypeStruct(w.shape, F32)] * 4,
        compiler_params=_params(("parallel",)),
    )(shares, w, m, v)


def _adamw_small(quads):
    n = len(quads)

    def body(*refs):
        ins, outs = refs[:4 * n], refs[4 * n:]
        for p in range(n):
            g_ref, w_ref, m_ref, v_ref = ins[4 * p:4 * p + 4]
            d_ref, nm_ref, nv_ref = outs[3 * p:3 * p + 3]
            d_ref[...], nm_ref[...], nv_ref[...] = _adamw_math(w_ref[...], g_ref[...], m_ref[...], v_ref[...])

    flat = [a for quad in quads for a in quad]
    out = pl.pallas_call(
        body, name="adamw_small",
        out_shape=[jax.ShapeDtypeStruct(quad[1].shape, F32) for quad in quads for _ in range(3)],
        compiler_params=_params(),
    )(*flat)
    return [tuple(out[3 * p:3 * p + 3]) for p in range(n)]


def kernel(x, norm1_g, w_in, q_norm_g, k_norm_g, rel_bias, conv_w, conv_b, w_attn_proj, w_conv_proj, w_gate, b_gate, w_out, norm2_g, w_up, w_down, loss_target, m_norm1_g, m_w_in, m_q_norm_g, m_k_norm_g, m_rel_bias, m_conv_w, m_conv_b, m_w_attn_proj, m_w_conv_proj, m_w_gate, m_b_gate, m_w_out, m_norm2_g, m_w_up, m_w_down, v_norm1_g, v_w_in, v_q_norm_g, v_k_norm_g, v_rel_bias, v_conv_w, v_conv_b, v_w_attn_proj, v_w_conv_proj, v_w_gate, v_b_gate, v_w_out, v_norm2_g, v_w_up, v_w_down):
    my_idx = _linear(_me())
    big_w = (w_in, w_attn_proj, w_conv_proj, w_gate, w_out, w_up, w_down)
    big_m = (m_w_in, m_w_attn_proj, m_w_conv_proj, m_w_gate, m_w_out, m_w_up, m_w_down)
    big_v = (v_w_in, v_w_attn_proj, v_w_conv_proj, v_w_gate, v_w_out, v_w_up, v_w_down)
    big_names = ("w_in", "w_attn_proj", "w_conv_proj", "w_gate", "w_out", "w_up", "w_down")

    conv_w_tile = jnp.pad(conv_w, ((0, SUBLANES - conv_w.shape[0]), (0, 0)))
    gathered = _all_gather(list(_cast_shards(big_w)) + [conv_w_tile], BIG_AXES + (1,))
    conv_w_full = gathered[7][:3]

    dx, big_g, small = _local_step(x[0], loss_target[0], norm1_g, q_norm_g, k_norm_g, rel_bias, conv_w_full, conv_b,
                                   b_gate, norm2_g, *gathered[:7])

    shares = _reduce_scatter(big_g)
    big_out = [_adamw_big("adamw_" + name, s, w, m, v)
               for name, s, w, m, v in zip(big_names, shares, big_w, big_m, big_v)]

    tot = _all_reduce_small(small)
    g_rel_bias = jnp.concatenate(
        [jnp.zeros((N_HEADS, 1), F32), tot[10:26, MAX_REL + 1:KW][:, ::-1], tot[10:26, KW:KW + 1]], axis=1)
    g_conv_w = lax.dynamic_slice(tot[4:7], (0, my_idx * LANES), (3, LANES))
    small_g = [tot[0:1], tot[1:2, :HEAD_DIM], tot[2:3, :HEAD_DIM], g_rel_bias, g_conv_w, tot[3:4],
               tot[7:9].reshape(1, 2 * D_MODEL), tot[9:10]]
    small_w = (norm1_g, q_norm_g, k_norm_g, rel_bias, conv_w, conv_b, b_gate, norm2_g)
    small_m = (m_norm1_g, m_q_norm_g, m_k_norm_g, m_rel_bias, m_conv_w, m_conv_b, m_b_gate, m_norm2_g)
    small_v = (v_norm1_g, v_q_norm_g, v_k_norm_g, v_rel_bias, v_conv_w, v_conv_b, v_b_gate, v_norm2_g)

    def two_d(a):
        return a.reshape(1, -1) if a.ndim == 1 else a

    small_out = _adamw_small([(g, two_d(w), two_d(m), two_d(v))
                              for g, w, m, v in zip(small_g, small_w, small_m, small_v)])

    order = ("norm1_g", "w_in", "q_norm_g", "k_norm_g", "rel_bias", "conv_w", "conv_b", "w_attn_proj", "w_conv_proj",
             "w_gate", "b_gate", "w_out", "norm2_g", "w_up", "w_down")
    small_names = ("norm1_g", "q_norm_g", "k_norm_g", "rel_bias", "conv_w", "conv_b", "b_gate", "norm2_g")
    res = {}
    for name, (g, d, nm, nv) in zip(big_names, big_out):
        res[name] = (g, d, nm, nv)
    for name, g, w, (d, nm, nv) in zip(small_names, small_g, small_w, small_out):
        res[name] = tuple(a.reshape(w.shape) for a in (g, d, nm, nv))
    loss = tot[26, 0]
    return (loss, dx[None], *[res[n][0] for n in order], *[res[n][1] for n in order],
            *[res[n][2] for n in order], *[res[n][3] for n in order])
```

```python
import functools

import jax
import jax.numpy as jnp
from jax import lax
from jax.experimental import pallas as pl
from jax.experimental.pallas import tpu as pltpu

F32 = jnp.float32
BF16 = jnp.bfloat16

D_MODEL = 1024
N_HEADS = 16
HEAD_DIM = 64
CHUNK = 64
N_PREV_CHUNKS = 8
MAX_REL = 256
D_FF = 4096
EPS = 1e-6
NEG_INF = -1e30
N_DEV = 8

ADAM_LR = 0.001
ADAM_B1 = 0.9
ADAM_B2 = 0.999
ADAM_EPS = 1e-08
ADAM_WD = 0.01
ADAM_STEP = 10

LANES = 128
SUBLANES = 8
VMEM_LIMIT = 48 * 1024 * 1024
QB = 256
KW = 3 * QB
SKEW = 1024

MESH_T = pl.DeviceIdType.MESH


def _dot(a, b):
    return jnp.dot(a, b, preferred_element_type=F32)


def _dot_nt(a, b):
    return lax.dot_general(a, b, (((1,), (1,)), ((), ())), preferred_element_type=F32)


def _dot_tn(a, b):
    return lax.dot_general(a, b, (((0,), (0,)), ((), ())), preferred_element_type=F32)


def _params(sem=None):
    return pltpu.CompilerParams(dimension_semantics=sem, vmem_limit_bytes=VMEM_LIMIT)


def _fold8(v):
    rows, n = v.shape
    return v.reshape(rows // SUBLANES, SUBLANES, n).sum(axis=0)


def _head_sum_matrix():
    r = lax.broadcasted_iota(jnp.int32, (LANES, LANES), 0) // HEAD_DIM
    c = lax.broadcasted_iota(jnp.int32, (LANES, LANES), 1) // HEAD_DIM
    return (r == c).astype(BF16)


def _head_sums(v, e):
    hi = v.astype(BF16)
    lo = (v - hi.astype(F32)).astype(BF16)
    return _dot(hi, e) + _dot(lo, e)


def _in_proj(x, g1, w_in, rider=None):
    t = x.shape[0]
    tm = min(t, 1024)
    n_slab = w_in.shape[1] // D_MODEL

    def body(x_ref, g_ref, w_ref, proj_ref, h_ref):
        @pl.when(pl.program_id(1) == 0)
        def _():
            xf = x_ref[...]
            r = lax.rsqrt(jnp.mean(xf * xf, axis=-1, keepdims=True) + EPS)
            h_ref[...] = (xf * r * g_ref[...]).astype(BF16)

        proj_ref[...] = _dot(h_ref[...], w_ref[...]).astype(BF16)

    return _call(
        body, name="in_proj", grid=(t // tm, n_slab), args=(x, g1, w_in),
        in_specs=[pl.BlockSpec((tm, D_MODEL), lambda i, k: (i, 0)),
                  pl.BlockSpec((1, D_MODEL), lambda i, k: (0, 0)),
                  pl.BlockSpec((D_MODEL, D_MODEL), lambda i, k: (0, k))],
        out_specs=[pl.BlockSpec((tm, D_MODEL), lambda i, k: (i, k)),
                   pl.BlockSpec((tm, D_MODEL), lambda i, k: (i, 0))],
        out_shape=[jax.ShapeDtypeStruct((t, n_slab * D_MODEL), BF16), jax.ShapeDtypeStruct((t, D_MODEL), BF16)],
        semantics=("parallel", "arbitrary"), rider=rider)


def _gate_proj(h, w_g, b_g):
    t = h.shape[0]
    tm = min(t, 1024)

    def body(h_ref, w_ref, b_ref, o_ref):
        o_ref[...] = jax.nn.sigmoid(_dot(h_ref[...], w_ref[...]) + b_ref[...]).astype(BF16)

    return pl.pallas_call(
        body, name="gate_proj", grid=(t // tm, 2),
        in_specs=[pl.BlockSpec((tm, D_MODEL), lambda i, k: (i, 0)),
                  pl.BlockSpec((D_MODEL, D_MODEL), lambda i, k: (0, k)),
                  pl.BlockSpec((1, D_MODEL), lambda i, k: (0, k))],
        out_specs=pl.BlockSpec((tm, D_MODEL), lambda i, k: (i, k)),
        out_shape=jax.ShapeDtypeStruct((t, 2 * D_MODEL), BF16),
        compiler_params=_params(("parallel", "arbitrary")),
    )(h, w_g, b_g)


def _qknorm_fwd(proj, gq, gk):
    t = proj.shape[0]
    tm = min(t, 512)
    scale = HEAD_DIM ** -0.5

    def body(q_ref, k_ref, gq_ref, gk_ref, qn_ref, kn_ref):
        e = _head_sum_matrix()
        for src, g_ref, dst, sc in ((q_ref, gq_ref, qn_ref, scale), (k_ref, gk_ref, kn_ref, 1.0)):
            for s in range(D_MODEL // LANES):
                sl = slice(s * LANES, (s + 1) * LANES)
                xf = src[:, sl].astype(F32)
                r = lax.rsqrt(_head_sums(xf * xf, e) * (1.0 / HEAD_DIM) + EPS)
                dst[:, sl] = (xf * r * g_ref[:, sl] * sc).astype(BF16)

    return pl.pallas_call(
        body, name="qknorm_fwd", grid=(t // tm,),
        in_specs=[pl.BlockSpec((tm, D_MODEL), lambda i: (i, 0)),
                  pl.BlockSpec((tm, D_MODEL), lambda i: (i, 1)),
                  pl.BlockSpec((1, D_MODEL), lambda i: (0, 0)),
                  pl.BlockSpec((1, D_MODEL), lambda i: (0, 0))],
        out_specs=[pl.BlockSpec((tm, D_MODEL), lambda i: (i, 0))] * 2,
        out_shape=[jax.ShapeDtypeStruct((t, D_MODEL), BF16)] * 2,
        compiler_params=_params(("parallel",)),
    )(proj, proj, gq, gk)


def _bias_tiles(rel_bias):
    edge = jnp.broadcast_to(rel_bias[:, 2 * MAX_REL:], (N_HEADS, QB))
    mid = rel_bias[:, 1:2 * MAX_REL + 1][:, ::-1]
    vec = jnp.concatenate([edge, mid, edge], axis=1)
    flat = jnp.tile(vec, (1, QB))[:, :QB * (SKEW - 1)]
    tile = flat.reshape(N_HEADS, QB, SKEW - 1)[:, :, :KW]
    qc = jnp.arange(QB)[:, None] // CHUNK
    kc = jnp.arange(KW)[None, :] // CHUNK
    band = (kc >= qc) & (kc <= qc + N_PREV_CHUNKS)
    return jnp.where(band[None], tile, NEG_INF)


def _window_specs(col0):
    return [pl.BlockSpec((QB, LANES), functools.partial(
        lambda p, b, back: (jnp.maximum(b - back, 0), col0 + p), back=back)) for back in (2, 1, 0)]


def _attn_fwd(qn, kn, proj, bias, rider=None):
    t = qn.shape[0]
    nb = t // QB
    v_col0 = 2 * D_MODEL // LANES

    def body(q_ref, k0, k1, k2, v0, v1, v2, bias_ref, o_ref):
        b = pl.program_id(1)
        q = q_ref[...]
        k = jnp.concatenate([k0[...], k1[...], k2[...]], axis=0)
        v = jnp.concatenate([v0[...], v1[...], v2[...]], axis=0)
        head_a = lax.broadcasted_iota(jnp.int32, (1, LANES), 1) < HEAD_DIM
        valid = lax.broadcasted_iota(jnp.int32, (1, KW), 1) >= (2 - b) * QB
        outs = []
        for hh in range(2):
            mine = head_a if hh == 0 else jnp.logical_not(head_a)
            kh = jnp.where(mine, k, jnp.zeros_like(k))
            s = _dot_nt(q, kh) + bias_ref[hh]
            s = jnp.where(valid, s, NEG_INF)
            p = jnp.exp(s - jnp.max(s, axis=-1, keepdims=True))
            l = jnp.sum(p, axis=-1, keepdims=True)
            outs.append(_dot(p.astype(BF16), v) / l)
        o_ref[...] = jnp.where(head_a, outs[0], outs[1]).astype(BF16)

    return _call(
        body, name="attn_fwd", grid=(D_MODEL // LANES, nb), args=(qn, kn, kn, kn, proj, proj, proj, bias),
        in_specs=[pl.BlockSpec((QB, LANES), lambda p, b: (b, p))] + _window_specs(0) + _window_specs(v_col0)
        + [pl.BlockSpec((2, QB, KW), lambda p, b: (p, 0, 0))],
        out_specs=[pl.BlockSpec((QB, LANES), lambda p, b: (b, p))],
        out_shape=[jax.ShapeDtypeStruct((t, D_MODEL), BF16)],
        semantics=("parallel", "arbitrary"), rider=rider)


def _shift_down(u, halo, n):
    rows = lax.broadcasted_iota(jnp.int32, (u.shape[0], 1), 0)
    out = pltpu.roll(u, n, 0)
    for j in range(n):
        out = jnp.where(rows == j, halo[SUBLANES - n + j:SUBLANES - n + j + 1, :], out)
    return out


def _shift_up(u, halo, n):
    tm = u.shape[0]
    rows = lax.broadcasted_iota(jnp.int32, (tm, 1), 0)
    out = pltpu.roll(u, tm - n, 0)
    for j in range(n):
        out = jnp.where(rows == tm - n + j, halo[j:j + 1, :], out)
    return out


def _conv_fwd(proj, conv_w, conv_b):
    t = proj.shape[0]
    tm = min(t, 512)
    hb = tm // SUBLANES

    def body(bg_ref, cg_ref, xc_ref, cgh_ref, xch_ref, w_ref, b_ref, o_ref):
        i = pl.program_id(0)
        u = cg_ref[...].astype(F32) * xc_ref[...].astype(F32)
        halo = cgh_ref[...].astype(F32) * xch_ref[...].astype(F32)
        halo = jnp.where(i > 0, halo, 0.0)
        w = w_ref[...]
        s = w[0:1] * _shift_down(u, halo, 2) + w[1:2] * _shift_down(u, halo, 1) + w[2:3] * u
        o_ref[...] = (bg_ref[...].astype(F32) * (b_ref[...] + s)).astype(BF16)

    def prev(col):
        return pl.BlockSpec((SUBLANES, D_MODEL), lambda i: (jnp.maximum(i * hb - 1, 0), col))

    return pl.pallas_call(
        body, name="conv_fwd", grid=(t // tm,),
        in_specs=[pl.BlockSpec((tm, D_MODEL), lambda i: (i, 3)),
                  pl.BlockSpec((tm, D_MODEL), lambda i: (i, 4)),
                  pl.BlockSpec((tm, D_MODEL), lambda i: (i, 5)),
                  prev(4), prev(5),
                  pl.BlockSpec((3, D_MODEL), lambda i: (0, 0)),
                  pl.BlockSpec((1, D_MODEL), lambda i: (0, 0))],
        out_specs=pl.BlockSpec((tm, D_MODEL), lambda i: (i, 0)),
        out_shape=jax.ShapeDtypeStruct((t, D_MODEL), BF16),
        compiler_params=_params(("parallel",)),
    )(proj, proj, proj, proj, proj, conv_w, conv_b)


def _mix_out(y_attn, y_conv, gates, x, w_ap, w_cp, w_out, g2):
    t = x.shape[0]
    tm = min(t, 512)

    def body(ya_in, yc_in, g_ref, x_ref, wap, wcp, wout, g2_ref, ya_ref, yc_ref, mg_ref, x1_ref, h2_ref):
        ya = _dot(ya_in[...], wap[...])
        yc = _dot(yc_in[...], wcp[...])
        ya_ref[...] = ya.astype(BF16)
        yc_ref[...] = yc.astype(BF16)
        merged = (g_ref[:, :D_MODEL].astype(F32) * ya + g_ref[:, D_MODEL:].astype(F32) * yc).astype(BF16)
        mg_ref[...] = merged
        x1 = x_ref[...] + _dot(merged, wout[...])
        x1_ref[...] = x1
        r = lax.rsqrt(jnp.mean(x1 * x1, axis=-1, keepdims=True) + EPS)
        h2_ref[...] = (x1 * r * g2_ref[...]).astype(BF16)

    row = pl.BlockSpec((tm, D_MODEL), lambda i: (i, 0))
    full = pl.BlockSpec((D_MODEL, D_MODEL), lambda i: (0, 0))
    return pl.pallas_call(
        body, name="mix_out", grid=(t // tm,),
        in_specs=[row, row, pl.BlockSpec((tm, 2 * D_MODEL), lambda i: (i, 0)), row, full, full, full,
                  pl.BlockSpec((1, D_MODEL), lambda i: (0, 0))],
        out_specs=[row] * 5,
        out_shape=[jax.ShapeDtypeStruct((t, D_MODEL), BF16)] * 3
        + [jax.ShapeDtypeStruct((t, D_MODEL), F32), jax.ShapeDtypeStruct((t, D_MODEL), BF16)],
        compiler_params=_params(("parallel",)),
    )(y_attn, y_conv, gates, x, w_ap, w_cp, w_out, g2)


def _mlp_fwd(h2, w_up, w_down, x1, target):
    t = h2.shape[0]
    tm = min(t, 512)
    tf = 1024
    nf = D_FF // tf

    def body(h2_ref, wup, wdn, x1_ref, tg_ref, a_ref, dy_ref, dyb_ref, loss_ref, acc):
        i, j = pl.program_id(0), pl.program_id(1)
        a = _dot(h2_ref[...], wup[...])
        a_ref[...] = a.astype(BF16)
        u = jnp.square(jnp.maximum(a, 0.0)).astype(BF16)
        part = _dot(u, wdn[...])

        @pl.when(j == 0)
        def _():
            acc[...] = part

        @pl.when(j > 0)
        def _():
            acc[...] += part

        @pl.when((i == 0) & (j == 0))
        def _():
            loss_ref[...] = jnp.zeros_like(loss_ref)

        @pl.when(j == nf - 1)
        def _():
            diff = x1_ref[...] + acc[...] - tg_ref[...]
            loss_ref[...] += _fold8(diff * diff)
            dy = diff * (1.0 / D_MODEL)
            dy_ref[...] = dy
            dyb_ref[...] = dy.astype(BF16)

    row = pl.BlockSpec((tm, D_MODEL), lambda i, j: (i, 0))
    return pl.pallas_call(
        body, name="mlp_fwd", grid=(t // tm, nf),
        in_specs=[row, pl.BlockSpec((D_MODEL, tf), lambda i, j: (0, j)),
                  pl.BlockSpec((tf, D_MODEL), lambda i, j: (j, 0)), row, row],
        out_specs=[pl.BlockSpec((tm, tf), lambda i, j: (i, j)), row, row,
                   pl.BlockSpec((SUBLANES, D_MODEL), lambda i, j: (0, 0))],
        out_shape=[jax.ShapeDtypeStruct((t, D_FF), BF16), jax.ShapeDtypeStruct((t, D_MODEL), F32),
                   jax.ShapeDtypeStruct((t, D_MODEL), BF16), jax.ShapeDtypeStruct((SUBLANES, D_MODEL), F32)],
        scratch_shapes=[pltpu.VMEM((tm, D_MODEL), F32)],
        compiler_params=_params(("arbitrary", "arbitrary")),
    )(h2, w_up, w_down, x1, target)


def _rmsnorm_bwd(xf, g, dh):
    r = lax.rsqrt(jnp.mean(xf * xf, axis=-1, keepdims=True) + EPS)
    xh = xf * r
    dxh = dh * g
    dx = r * (dxh - xh * jnp.mean(dxh * xh, axis=-1, keepdims=True))
    return dx, dh * xh


def _mlp_bwd(dyb, a, w_down, w_up, x1, dy, g2):
    t = dyb.shape[0]
    tm = min(t, 512)
    tf = 1024
    nf = D_FF // tf

    def body(dyb_ref, a_ref, wdn, wup, x1_ref, dy_ref, g2_ref, da_ref, dx1_ref, dx1b_ref, dg2_ref, acc):
        i, j = pl.program_id(0), pl.program_id(1)
        du = _dot_nt(dyb_ref[...], wdn[...])
        da = (du * (2.0 * jnp.maximum(a_ref[...].astype(F32), 0.0))).astype(BF16)
        da_ref[...] = da
        part = _dot_nt(da, wup[...])

        @pl.when(j == 0)
        def _():
            acc[...] = part

        @pl.when(j > 0)
        def _():
            acc[...] += part

        @pl.when((i == 0) & (j == 0))
        def _():
            dg2_ref[...] = jnp.zeros_like(dg2_ref)

        @pl.when(j == nf - 1)
        def _():
            dx, dg = _rmsnorm_bwd(x1_ref[...], g2_ref[...], acc[...])
            dx1 = dy_ref[...] + dx
            dx1_ref[...] = dx1
            dx1b_ref[...] = dx1.astype(BF16)
            dg2_ref[...] += _fold8(dg)

    row = pl.BlockSpec((tm, D_MODEL), lambda i, j: (i, 0))
    return pl.pallas_call(
        body, name="mlp_bwd", grid=(t // tm, nf),
        in_specs=[row, pl.BlockSpec((tm, tf), lambda i, j: (i, j)),
                  pl.BlockSpec((tf, D_MODEL), lambda i, j: (j, 0)),
                  pl.BlockSpec((D_MODEL, tf), lambda i, j: (0, j)), row, row,
                  pl.BlockSpec((1, D_MODEL), lambda i, j: (0, 0))],
        out_specs=[pl.BlockSpec((tm, tf), lambda i, j: (i, j)), row, row,
                   pl.BlockSpec((SUBLANES, D_MODEL), lambda i, j: (0, 0))],
        out_shape=[jax.ShapeDtypeStruct((t, D_FF), BF16), jax.ShapeDtypeStruct((t, D_MODEL), F32),
                   jax.ShapeDtypeStruct((t, D_MODEL), BF16), jax.ShapeDtypeStruct((SUBLANES, D_MODEL), F32)],
        scratch_shapes=[pltpu.VMEM((tm, D_MODEL), F32)],
        compiler_params=_params(("arbitrary", "arbitrary")),
    )(dyb, a, w_down, w_up, x1, dy, g2)


def _wgrad(name, lhs, rhs_list, rhs_slabs, relu_sq=False):
    t, m = lhs.shape
    tt = min(t, 512)
    tmo = min(m, 1024)
    n_slab = sum(rhs_slabs)
    starts = [sum(rhs_slabs[:n]) for n in range(len(rhs_slabs))]
    n_rhs = len(rhs_list)

    def body(*refs):
        l_ref, r_refs, o_ref, acc = refs[0], refs[1:1 + n_rhs], refs[1 + n_rhs], refs[2 + n_rhs]
        k, s = pl.program_id(1), pl.program_id(2)
        lv = l_ref[...]
        if relu_sq:
            lv = jnp.square(jnp.maximum(lv.astype(F32), 0.0)).astype(BF16)

        @pl.when(s == 0)
        def _():
            acc[...] = jnp.zeros_like(acc)

        for n in range(n_rhs):
            @pl.when((k >= starts[n]) & (k < starts[n] + rhs_slabs[n]))
            def _(n=n):
                acc[...] += _dot_tn(lv, r_refs[n][...])

        @pl.when(s == pl.num_programs(2) - 1)
        def _():
            o_ref[...] = acc[...].astype(BF16)

    def rhs_spec(n):
        lo, cnt = starts[n], rhs_slabs[n]

        def index(i, k, s):
            inside = (k >= lo) & (k < lo + cnt)
            return (jnp.where(inside, s, 0), jnp.clip(k - lo, 0, cnt - 1))
        return pl.BlockSpec((tt, D_MODEL), index)

    return pl.pallas_call(
        body, name=name, grid=(m // tmo, n_slab, t // tt),
        in_specs=[pl.BlockSpec((tt, tmo), lambda i, k, s: (s, i))] + [rhs_spec(n) for n in range(n_rhs)],
        out_specs=pl.BlockSpec((tmo, D_MODEL), lambda i, k, s: (i, k)),
        out_shape=jax.ShapeDtypeStruct((m, n_slab * D_MODEL), BF16),
        scratch_shapes=[pltpu.VMEM((tmo, D_MODEL), F32)],
        compiler_params=_params(("parallel", "parallel", "arbitrary")),
    )(lhs, *rhs_list)


def _mix_bwd(dx1b, gates, ya, yc, w_out, w_ap, w_cp, w_g):
    t = dx1b.shape[0]
    tm = min(t, 512)

    def body(dx_ref, g_ref, ya_ref, yc_ref, wout, wap, wcp, wg,
             dgp_ref, dya_ref, dyc_ref, dyat_ref, dycv_ref, dhg_ref, dbg_ref):
        dm = _dot_nt(dx_ref[...], wout[...])
        ga = g_ref[:, :D_MODEL].astype(F32)
        gc = g_ref[:, D_MODEL:].astype(F32)
        dya = (dm * ga).astype(BF16)
        dyc = (dm * gc).astype(BF16)
        dya_ref[...] = dya
        dyc_ref[...] = dyc
        dgpa = dm * ya_ref[...].astype(F32) * ga * (1.0 - ga)
        dgpc = dm * yc_ref[...].astype(F32) * gc * (1.0 - gc)

        @pl.when(pl.program_id(0) == 0)
        def _():
            dbg_ref[...] = jnp.zeros_like(dbg_ref)

        dbg_ref[:, :D_MODEL] += _fold8(dgpa)
        dbg_ref[:, D_MODEL:] += _fold8(dgpc)
        dgpa = dgpa.astype(BF16)
        dgpc = dgpc.astype(BF16)
        dgp_ref[:, :D_MODEL] = dgpa
        dgp_ref[:, D_MODEL:] = dgpc
        dyat_ref[...] = _dot_nt(dya, wap[...]).astype(BF16)
        dycv_ref[...] = _dot_nt(dyc, wcp[...]).astype(BF16)
        dhg_ref[...] = _dot_nt(dgpa, wg[:, :D_MODEL]) + _dot_nt(dgpc, wg[:, D_MODEL:])

    row = pl.BlockSpec((tm, D_MODEL), lambda i: (i, 0))
    row2 = pl.BlockSpec((tm, 2 * D_MODEL), lambda i: (i, 0))
    full = pl.BlockSpec((D_MODEL, D_MODEL), lambda i: (0, 0))
    return pl.pallas_call(
        body, name="mix_bwd", grid=(t // tm,),
        in_specs=[row, row2, row, row, full, full, full, pl.BlockSpec((D_MODEL, 2 * D_MODEL), lambda i: (0, 0))],
        out_specs=[row2, row, row, row, row, row, pl.BlockSpec((SUBLANES, 2 * D_MODEL), lambda i: (0, 0))],
        out_shape=[jax.ShapeDtypeStruct((t, 2 * D_MODEL), BF16)] + [jax.ShapeDtypeStruct((t, D_MODEL), BF16)] * 4
        + [jax.ShapeDtypeStruct((t, D_MODEL), F32), jax.ShapeDtypeStruct((SUBLANES, 2 * D_MODEL), F32)],
        compiler_params=_params(("arbitrary",)),
    )(dx1b, gates, ya, yc, w_out, w_ap, w_cp, w_g)


def _conv_bwd(dyconv, proj, conv_w, conv_b):
    t = proj.shape[0]
    tm = min(t, 512)
    hb = tm // SUBLANES
    last = t // SUBLANES - 1

    def body(dy_ref, dyn_ref, bg_ref, bgn_ref, cg_ref, cgp_ref, xc_ref, xcp_ref, w_ref, b_ref,
             o_ref, dcb_ref, dcw_ref):
        i = pl.program_id(0)
        cg = cg_ref[...].astype(F32)
        xc = xc_ref[...].astype(F32)
        bg = bg_ref[...].astype(F32)
        u = cg * xc
        prev = jnp.where(i > 0, cgp_ref[...].astype(F32) * xcp_ref[...].astype(F32), 0.0)
        u1 = _shift_down(u, prev, 1)
        u2 = _shift_down(u, prev, 2)
        w = w_ref[...]
        conv = b_ref[...] + (w[0:1] * u2 + w[1:2] * u1 + w[2:3] * u)
        dy = dy_ref[...].astype(F32)
        dconv = dy * bg
        nxt = jnp.where(i < pl.num_programs(0) - 1, dyn_ref[...].astype(F32) * bgn_ref[...].astype(F32), 0.0)
        du = w[2:3] * dconv + w[1:2] * _shift_up(dconv, nxt, 1) + w[0:1] * _shift_up(dconv, nxt, 2)
        o_ref[:, :D_MODEL] = (dy * conv).astype(BF16)
        o_ref[:, D_MODEL:2 * D_MODEL] = (du * xc).astype(BF16)
        o_ref[:, 2 * D_MODEL:] = (du * cg).astype(BF16)

        @pl.when(i == 0)
        def _():
            dcb_ref[...] = jnp.zeros_like(dcb_ref)
            dcw_ref[...] = jnp.zeros_like(dcw_ref)

        dcb_ref[...] += _fold8(dconv)
        dcw_ref[0:SUBLANES] += _fold8(dconv * u2)
        dcw_ref[SUBLANES:2 * SUBLANES] += _fold8(dconv * u1)
        dcw_ref[2 * SUBLANES:] += _fold8(dconv * u)

    def prev(col):
        return pl.BlockSpec((SUBLANES, D_MODEL), lambda i: (jnp.maximum(i * hb - 1, 0), col))

    def nxt(col):
        return pl.BlockSpec((SUBLANES, D_MODEL), lambda i: (jnp.minimum((i + 1) * hb, last), col))

    def cur(col):
        return pl.BlockSpec((tm, D_MODEL), lambda i: (i, col))

    return pl.pallas_call(
        body, name="conv_bwd", grid=(t // tm,),
        in_specs=[cur(0), nxt(0), cur(3), nxt(3), cur(4), prev(4), cur(5), prev(5),
                  pl.BlockSpec((3, D_MODEL), lambda i: (0, 0)), pl.BlockSpec((1, D_MODEL), lambda i: (0, 0))],
        out_specs=[pl.BlockSpec((tm, 3 * D_MODEL), lambda i: (i, 0)),
                   pl.BlockSpec((SUBLANES, D_MODEL), lambda i: (0, 0)),
                   pl.BlockSpec((3 * SUBLANES, D_MODEL), lambda i: (0, 0))],
        out_shape=[jax.ShapeDtypeStruct((t, 3 * D_MODEL), BF16), jax.ShapeDtypeStruct((SUBLANES, D_MODEL), F32),
                   jax.ShapeDtypeStruct((3 * SUBLANES, D_MODEL), F32)],
        compiler_params=_params(("arbitrary",)),
    )(dyconv, dyconv, proj, proj, proj, proj, proj, proj, conv_w, conv_b)


def _attn_bwd(qn, kn, proj, dyattn, bias, rider=None):
    t = qn.shape[0]
    nb = t // QB
    v_col0 = 2 * D_MODEL // LANES

    def body(q_ref, k0, k1, k2, v0, v1, v2, do_ref, bias_ref, dq_ref, dk_ref, dv_ref, db_ref, acck, accv):
        b = pl.program_id(1)

        @pl.when(b == 0)
        def _():
            acck[...] = jnp.zeros_like(acck)
            accv[...] = jnp.zeros_like(accv)
            db_ref[...] = jnp.zeros_like(db_ref)

        @pl.when(b < nb)
        def _():
            q = q_ref[...]
            do = do_ref[...]
            k = jnp.concatenate([k0[...], k1[...], k2[...]], axis=0)
            v = jnp.concatenate([v0[...], v1[...], v2[...]], axis=0)
            head_a = lax.broadcasted_iota(jnp.int32, (1, LANES), 1) < HEAD_DIM
            valid = lax.broadcasted_iota(jnp.int32, (1, KW), 1) >= (2 - b) * QB
            dq = jnp.zeros((QB, LANES), F32)
            dkw = [jnp.zeros((QB, LANES), F32) for _ in range(3)]
            dvw = [jnp.zeros((QB, LANES), F32) for _ in range(3)]
            for hh in range(2):
                mine = head_a if hh == 0 else jnp.logical_not(head_a)
                kh = jnp.where(mine, k, jnp.zeros_like(k))
                vh = jnp.where(mine, v, jnp.zeros_like(v))
                qh = jnp.where(mine, q, jnp.zeros_like(q))
                doh = jnp.where(mine, do, jnp.zeros_like(do))
                s = _dot_nt(q, kh) + bias_ref[hh]
                s = jnp.where(valid, s, NEG_INF)
                p = jnp.exp(s - jnp.max(s, axis=-1, keepdims=True))
                p = p / jnp.sum(p, axis=-1, keepdims=True)
                dp = _dot_nt(do, vh)
                ds = p * (dp - jnp.sum(p * dp, axis=-1, keepdims=True))
                db_ref[hh] += ds
                pb = p.astype(BF16)
                dsb = ds.astype(BF16)
                dq = dq + _dot(dsb, kh)
                for w in range(3):
                    cols = slice(w * QB, (w + 1) * QB)
                    dkw[w] = dkw[w] + _dot_tn(dsb[:, cols], qh)
                    dvw[w] = dvw[w] + _dot_tn(pb[:, cols], doh)
            dq_ref[...] = dq
            for w in range(3):
                slot = lax.rem(b + w + 1, 3)
                if w == 2:
                    acck[slot] = dkw[w]
                    accv[slot] = dvw[w]
                else:
                    acck[slot] += dkw[w]
                    accv[slot] += dvw[w]

        done = lax.rem(b + 1, 3)
        dk_ref[...] = acck[done]
        dv_ref[...] = accv[done].astype(BF16)

    def cur(p, b):
        return (jnp.minimum(b, nb - 1), p)

    def window(col0):
        return [pl.BlockSpec((QB, LANES), functools.partial(
            lambda p, b, back: (jnp.maximum(jnp.minimum(b, nb - 1) - back, 0), col0 + p), back=back))
            for back in (2, 1, 0)]

    def done_block(p, b):
        return (jnp.maximum(b - 2, 0), p)

    return _call(
        body, name="attn_bwd", grid=(D_MODEL // LANES, nb + 2),
        args=(qn, kn, kn, kn, proj, proj, proj, dyattn, bias),
        in_specs=[pl.BlockSpec((QB, LANES), cur)] + window(0) + window(v_col0)
        + [pl.BlockSpec((QB, LANES), cur), pl.BlockSpec((2, QB, KW), lambda p, b: (p, 0, 0))],
        out_specs=[pl.BlockSpec((QB, LANES), cur), pl.BlockSpec((QB, LANES), done_block),
                   pl.BlockSpec((QB, LANES), done_block), pl.BlockSpec((2, QB, KW), lambda p, b: (p, 0, 0))],
        out_shape=[jax.ShapeDtypeStruct((t, D_MODEL), F32), jax.ShapeDtypeStruct((t, D_MODEL), F32),
                   jax.ShapeDtypeStruct((t, D_MODEL), BF16), jax.ShapeDtypeStruct((N_HEADS, QB, KW), F32)],
        scratch_shapes=[pltpu.VMEM((3, QB, LANES), F32), pltpu.VMEM((3, QB, LANES), F32)],
        semantics=("parallel", "arbitrary"), rider=rider)


def _qknorm_bwd(proj, dqn, dkn, gq, gk):
    t = proj.shape[0]
    tm = min(t, 512)
    scale = HEAD_DIM ** -0.5

    def body(q_ref, k_ref, dqn_ref, dkn_ref, gq_ref, gk_ref, o_ref, dgq_ref, dgk_ref):
        e = _head_sum_matrix()

        @pl.when(pl.program_id(0) == 0)
        def _():
            dgq_ref[...] = jnp.zeros_like(dgq_ref)
            dgk_ref[...] = jnp.zeros_like(dgk_ref)

        for n, (src, dn_ref, g_ref, dg_ref, sc) in enumerate(
                ((q_ref, dqn_ref, gq_ref, dgq_ref, scale), (k_ref, dkn_ref, gk_ref, dgk_ref, 1.0))):
            for s in range(D_MODEL // LANES):
                sl = slice(s * LANES, (s + 1) * LANES)
                xf = src[:, sl].astype(F32)
                r = lax.rsqrt(_head_sums(xf * xf, e) * (1.0 / HEAD_DIM) + EPS)
                xh = xf * r
                dn = dn_ref[:, sl] * sc
                dg_ref[:, sl] += _fold8(dn * xh)
                dxh = dn * g_ref[:, sl]
                mean = _head_sums(dxh * xh, e) * (1.0 / HEAD_DIM)
                o_ref[:, n * D_MODEL + s * LANES:n * D_MODEL + (s + 1) * LANES] = (r * (dxh - xh * mean)).astype(BF16)

    row = pl.BlockSpec((tm, D_MODEL), lambda i: (i, 0))
    vec = pl.BlockSpec((1, D_MODEL), lambda i: (0, 0))
    acc = pl.BlockSpec((SUBLANES, D_MODEL), lambda i: (0, 0))
    return pl.pallas_call(
        body, name="qknorm_bwd", grid=(t // tm,),
        in_specs=[row, pl.BlockSpec((tm, D_MODEL), lambda i: (i, 1)), row, row, vec, vec],
        out_specs=[pl.BlockSpec((tm, 2 * D_MODEL), lambda i: (i, 0)), acc, acc],
        out_shape=[jax.ShapeDtypeStruct((t, 2 * D_MODEL), BF16)] + [jax.ShapeDtypeStruct((SUBLANES, D_MODEL), F32)] * 2,
        compiler_params=_params(("arbitrary",)),
    )(proj, proj, dqn, dkn, gq, gk)


def _in_bwd(dqk, dv, dconv, w_in, dhg, x, g1, dx1, rider=None):
    t = x.shape[0]
    tm = min(t, 512)

    def body(dqk_ref, dv_ref, dc_ref, w_ref, dhg_ref, x_ref, g_ref, dx1_ref, dx_ref, dg_ref, acc):
        i, k = pl.program_id(0), pl.program_id(1)

        @pl.when(k == 0)
        def _():
            acc[...] = dhg_ref[...]

        @pl.when(k < 2)
        def _():
            acc[...] += _dot_nt(dqk_ref[...], w_ref[...])

        @pl.when(k == 2)
        def _():
            acc[...] += _dot_nt(dv_ref[...], w_ref[...])

        @pl.when(k > 2)
        def _():
            acc[...] += _dot_nt(dc_ref[...], w_ref[...])

        @pl.when((i == 0) & (k == 0))
        def _():
            dg_ref[...] = jnp.zeros_like(dg_ref)

        @pl.when(k == 5)
        def _():
            dx, dg = _rmsnorm_bwd(x_ref[...], g_ref[...], acc[...])
            dx_ref[...] = dx1_ref[...] + dx
            dg_ref[...] += _fold8(dg)

    row = pl.BlockSpec((tm, D_MODEL), lambda i, k: (i, 0))
    return _call(
        body, name="in_bwd", grid=(t // tm, 6), args=(dqk, dv, dconv, w_in, dhg, x, g1, dx1),
        in_specs=[pl.BlockSpec((tm, D_MODEL), lambda i, k: (i, jnp.minimum(k, 1))), row,
                  pl.BlockSpec((tm, D_MODEL), lambda i, k: (i, jnp.clip(k - 3, 0, 2))),
                  pl.BlockSpec((D_MODEL, D_MODEL), lambda i, k: (0, k)), row, row,
                  pl.BlockSpec((1, D_MODEL), lambda i, k: (0, 0)), row],
        out_specs=[row, pl.BlockSpec((SUBLANES, D_MODEL), lambda i, k: (0, 0))],
        out_shape=[jax.ShapeDtypeStruct((t, D_MODEL), F32), jax.ShapeDtypeStruct((SUBLANES, D_MODEL), F32)],
        scratch_shapes=[pltpu.VMEM((tm, D_MODEL), F32)],
        semantics=("arbitrary", "arbitrary"), rider=rider)


def _skew_bias_grad(dbias):
    padded = jnp.pad(dbias, ((0, 0), (0, 0), (0, SKEW - 1 - KW)))
    flat = jnp.pad(padded.reshape(N_HEADS, QB * (SKEW - 1)), ((0, 0), (0, QB)))
    return flat.reshape(N_HEADS, QB, SKEW)


def _small_partials(dg1, dgq, dgk, dcb, dcw, dbg, dg2, dbias_skew, loss_tile):
    def head_fold(v):
        acc = v[:, 0:LANES]
        for s in range(1, D_MODEL // LANES):
            acc = acc + v[:, s * LANES:(s + 1) * LANES]
        return acc + pltpu.roll(acc, HEAD_DIM, 1)

    def body(dg1_ref, dgq_ref, dgk_ref, dcb_ref, dcw_ref, dbg_ref, dg2_ref, db_ref, loss_ref, o_ref):
        col = lax.broadcasted_iota(jnp.int32, (1, SKEW), 1)
        far = (col <= MAX_REL) | (col > KW)
        o_ref[...] = jnp.zeros_like(o_ref)
        o_ref[0:1, :] = jnp.sum(dg1_ref[...], axis=0, keepdims=True)
        o_ref[1:2, 0:LANES] = head_fold(jnp.sum(dgq_ref[...], axis=0, keepdims=True))
        o_ref[2:3, 0:LANES] = head_fold(jnp.sum(dgk_ref[...], axis=0, keepdims=True))
        o_ref[3:4, :] = jnp.sum(dcb_ref[...], axis=0, keepdims=True)
        for j in range(3):
            o_ref[4 + j:5 + j, :] = jnp.sum(dcw_ref[j * SUBLANES:(j + 1) * SUBLANES, :], axis=0, keepdims=True)
        o_ref[7:8, :] = jnp.sum(dbg_ref[:, :D_MODEL], axis=0, keepdims=True)
        o_ref[8:9, :] = jnp.sum(dbg_ref[:, D_MODEL:], axis=0, keepdims=True)
        o_ref[9:10, :] = jnp.sum(dg2_ref[...], axis=0, keepdims=True)
        for h in range(N_HEADS):
            diag = jnp.sum(db_ref[h], axis=0, keepdims=True)
            far_sum = jnp.sum(jnp.where(far, diag, 0.0), axis=-1, keepdims=True)
            o_ref[10 + h:11 + h, :] = jnp.where(col == KW, far_sum, diag)
        loss = (0.5 / D_MODEL) * jnp.sum(jnp.sum(loss_ref[...], axis=0, keepdims=True), axis=-1, keepdims=True)
        o_ref[26:27, :] = jnp.broadcast_to(loss, (1, D_MODEL))

    return pl.pallas_call(
        body, name="small_partials",
        out_shape=jax.ShapeDtypeStruct((32, D_MODEL), F32),
        compiler_params=_params(),
    )(dg1, dgq, dgk, dcb, dcw, dbg, dg2, dbias_skew, loss_tile)


MID_AXES = (0, 0, 1, 0)
MLP_AXES = (1, 0)


def _local_step(x, target, norm1_g, q_norm_g, k_norm_g, rel_bias, conv_w, conv_b, b_gate, norm2_g,
                w_in, mid_w, mlp_w, distributed):
    g1 = norm1_g.reshape(1, D_MODEL)
    g2 = norm2_g.reshape(1, D_MODEL)
    gq = jnp.tile(q_norm_g, N_HEADS).reshape(1, D_MODEL)
    gk = jnp.tile(k_norm_g, N_HEADS).reshape(1, D_MODEL)
    cb = conv_b.reshape(1, D_MODEL)
    bias = _bias_tiles(rel_bias)

    (proj, h), got = _in_proj(x, g1, w_in, rider=_Gather(mid_w, MID_AXES) if distributed else None)
    w_ap, w_cp, w_g, w_out = got if distributed else mid_w
    gates = _gate_proj(h, w_g, b_gate.reshape(1, 2 * D_MODEL))
    qn, kn = _qknorm_fwd(proj, gq, gk)
    (y_attn,), got = _attn_fwd(qn, kn, proj, bias, rider=_Gather(mlp_w, MLP_AXES) if distributed else None)
    w_up, w_down = got if distributed else mlp_w
    y_conv = _conv_fwd(proj, conv_w, cb)
    ya, yc, merged, x1, h2 = _mix_out(y_attn, y_conv, gates, x, w_ap, w_cp, w_out, g2)
    a, dy, dyb, loss_tile = _mlp_fwd(h2, w_up, w_down, x1, target)

    da, dx1, dx1b, dg2 = _mlp_bwd(dyb, a, w_down, w_up, x1, dy, g2)
    gw_down = _wgrad("wgrad_down", a, [dyb], [1], relu_sq=True)
    gw_up = _wgrad("wgrad_up", h2, [da], [D_FF // D_MODEL])
    dgp, dya, dyc, dyattn, dyconv, dhg, dbg = _mix_bwd(dx1b, gates, ya, yc, w_out, w_ap, w_cp, w_g)
    gw_out = _wgrad("wgrad_out", merged, [dx1b], [1])
    gw_ap = _wgrad("wgrad_attn_proj", y_attn, [dya], [1])
    gw_cp = _wgrad("wgrad_conv_proj", y_conv, [dyc], [1])
    gw_g = _wgrad("wgrad_gate", h, [dgp], [2])
    dconv, dcb, dcw = _conv_bwd(dyconv, proj, conv_w, cb)
    early = (gw_ap, gw_cp, gw_g, gw_out, gw_up, gw_down)
    (dqn, dkn, dv, dbias), early_shares = _attn_bwd(
        qn, kn, proj, dyattn, bias, rider=_Scatter(early, MID_AXES + MLP_AXES) if distributed else None)
    dqk, dgq, dgk = _qknorm_bwd(proj, dqn, dkn, gq, gk)
    gw_in = _wgrad("wgrad_in", h, [dqk, dv, dconv], [2, 1, 3])
    (dx, dg1), in_shares = _in_bwd(dqk, dv, dconv, w_in, dhg, x, g1, dx1,
                                   rider=_Scatter((gw_in,), (1,)) if distributed else None)
    small = _small_partials(dg1, dgq, dgk, dcb, dcw, dbg, dg2, _skew_bias_grad(dbias), loss_tile)
    grads = tuple(in_shares) + tuple(early_shares) if distributed else (gw_in,) + early
    return dx, grads, small


def _me():
    return lax.axis_index("x"), lax.axis_index("y"), lax.axis_index("c")


def _peer(me, rel):
    x, y, c = me
    return (1 - x if rel & 4 else x, 1 - y if rel & 2 else y, 1 - c if rel & 1 else c)


def _linear(dev):
    return 4 * dev[0] + 2 * dev[1] + dev[2]


BIG_AXES = (1, 0, 0, 1, 0, 1, 0)


def _block(ref, axis, idx, size):
    return ref.at[pl.ds(idx * size, size), :] if axis == 0 else ref.at[:, pl.ds(idx * size, size)]


def _cast_shards(shards):
    def body(*refs):
        for src, dst in zip(refs[:len(shards)], refs[len(shards):]):
            dst[...] = src[...].astype(BF16)

    return pl.pallas_call(
        body, name="cast_shards",
        out_shape=[jax.ShapeDtypeStruct(s.shape, BF16) for s in shards],
        compiler_params=_params(),
    )(*shards)


class _Gather:
    def __init__(self, shards, axes):
        self.arrays, self.axes, self.n = list(shards), tuple(axes), len(shards)
        self.sizes = [s.shape[axis] for s, axis in zip(shards, axes)]
        self.out_shape = []
        for s, axis in zip(shards, axes):
            shape = (s.shape[0] * N_DEV, s.shape[1]) if axis == 0 else (s.shape[0], s.shape[1] * N_DEV)
            self.out_shape.append(jax.ShapeDtypeStruct(shape, s.dtype))
        self.scratch = [pltpu.SemaphoreType.DMA((self.n, 7)), pltpu.SemaphoreType.DMA((self.n, 7)),
                        pltpu.SemaphoreType.DMA((self.n,))]

    def _copies(self, srcs, outs, sems):
        send_sems, recv_sems, local_sems = sems
        me = _me()
        sibling = _peer(me, 1)
        chips = [_peer(me, rel) for rel in (4, 2, 6)]

        def rows(a, dev):
            return _block(outs[a], self.axes[a], _linear(dev), self.sizes[a])

        def copy(a, k, block_dev, to, src=None):
            return pltpu.make_async_remote_copy(
                src_ref=rows(a, block_dev) if src is None else src, dst_ref=rows(a, block_dev),
                send_sem=send_sems.at[a, k], recv_sem=recv_sems.at[a, k], device_id=to, device_id_type=MESH_T)

        own = [pltpu.make_async_copy(srcs[a], rows(a, me), local_sems.at[a]) for a in range(self.n)]
        first = []
        for a in range(self.n):
            first.append(copy(a, 0, me, sibling, src=srcs[a]))
            for j, chip in enumerate(chips):
                first.append(copy(a, 1 + j, me, chip, src=srcs[a]))
        return me, sibling, chips, copy, own, first

    def start(self, srcs, outs, sems):
        _, _, _, _, own, first = self._copies(srcs, outs, sems)
        for cp in own + first:
            cp.start()

    def finish(self, srcs, outs, sems):
        me, sibling, chips, copy, own, first = self._copies(srcs, outs, sems)
        passed = []
        for a in range(self.n):
            for j, chip in enumerate(chips):
                copy(a, 1 + j, chip, me).wait_recv()
                fwd = copy(a, 4 + j, chip, sibling)
                fwd.start()
                passed.append(fwd)
        for a in range(self.n):
            copy(a, 0, sibling, me).wait_recv()
            for j, chip in enumerate(chips):
                copy(a, 4 + j, _peer(chip, 1), me).wait_recv()
        for cp in first + passed:
            cp.wait_send()
        for cp in own:
            cp.wait()


class _Scatter:
    def __init__(self, grads, axes):
        self.arrays, self.axes, self.n = list(grads), tuple(axes), len(grads)
        self.sizes = [g.shape[axis] // N_DEV for g, axis in zip(grads, axes)]
        self.out_shape = []
        for g, axis in zip(grads, axes):
            shard = (g.shape[0] // N_DEV, g.shape[1]) if axis == 0 else (g.shape[0], g.shape[1] // N_DEV)
            self.out_shape.append(jax.ShapeDtypeStruct((N_DEV,) + shard, g.dtype))
        self.scratch = [pltpu.SemaphoreType.DMA((self.n, 7)), pltpu.SemaphoreType.DMA((self.n, 7)),
                        pltpu.SemaphoreType.DMA((self.n,))]

    def _copies(self, srcs, outs, sems):
        send_sems, recv_sems, local_sems = sems
        me = _me()
        my_idx = _linear(me)

        def mine(a):
            return _block(srcs[a], self.axes[a], my_idx, self.sizes[a])

        own = [pltpu.make_async_copy(mine(a), outs[a].at[my_idx], local_sems.at[a]) for a in range(self.n)]
        sends, recvs = [], []
        for a in range(self.n):
            for rel in range(1, N_DEV):
                peer = _peer(me, rel)
                sends.append(pltpu.make_async_remote_copy(
                    src_ref=_block(srcs[a], self.axes[a], _linear(peer), self.sizes[a]), dst_ref=outs[a].at[my_idx],
                    send_sem=send_sems.at[a, rel - 1], recv_sem=recv_sems.at[a, rel - 1],
                    device_id=peer, device_id_type=MESH_T))
                recvs.append(pltpu.make_async_remote_copy(
                    src_ref=mine(a), dst_ref=outs[a].at[_linear(peer)],
                    send_sem=send_sems.at[a, rel - 1], recv_sem=recv_sems.at[a, rel - 1],
                    device_id=peer, device_id_type=MESH_T))
        return own, sends, recvs

    def start(self, srcs, outs, sems):
        own, sends, _ = self._copies(srcs, outs, sems)
        for cp in own + sends:
            cp.start()

    def finish(self, srcs, outs, sems):
        own, sends, recvs = self._copies(srcs, outs, sems)
        for cp in recvs:
            cp.wait_recv()
        for cp in sends:
            cp.wait_send()
        for cp in own:
            cp.wait()


def _call(body, *, name, args, in_specs, out_specs, out_shape, grid=(), scratch_shapes=(), semantics=None,
          rider=None):
    if rider is None:
        return pl.pallas_call(
            body, name=name, grid=grid, in_specs=in_specs, out_specs=out_specs, out_shape=out_shape,
            scratch_shapes=list(scratch_shapes), compiler_params=_params(semantics))(*args), None
    n_in, n_out, n_scr, r = len(in_specs), len(out_specs), len(scratch_shapes), rider.n

    def wrapped(*refs):
        ins, r_ins = refs[:n_in], refs[n_in:n_in + r]
        outs = refs[n_in + r:n_in + r + n_out]
        r_outs = refs[n_in + r + n_out:n_in + 2 * r + n_out]
        scr = refs[n_in + 2 * r + n_out:n_in + 2 * r + n_out + n_scr]
        sems = refs[n_in + 2 * r + n_out + n_scr:]
        first, last = None, None
        for ax in range(len(grid)):
            f, l = pl.program_id(ax) == 0, pl.program_id(ax) == pl.num_programs(ax) - 1
            first = f if first is None else first & f
            last = l if last is None else last & l
        if first is None:
            rider.start(r_ins, r_outs, sems)
            body(*ins, *outs, *scr)
            rider.finish(r_ins, r_outs, sems)
            return

        @pl.when(first)
        def _():
            rider.start(r_ins, r_outs, sems)

        body(*ins, *outs, *scr)

        @pl.when(last)
        def _():
            rider.finish(r_ins, r_outs, sems)

    any_spec = pl.BlockSpec(memory_space=pl.ANY)
    out = pl.pallas_call(
        wrapped, name=name, grid=grid, in_specs=list(in_specs) + [any_spec] * r,
        out_specs=list(out_specs) + [any_spec] * r, out_shape=list(out_shape) + rider.out_shape,
        scratch_shapes=list(scratch_shapes) + rider.scratch,
        compiler_params=_params(None if semantics is None else ("arbitrary",) * len(semantics)),
    )(*args, *rider.arrays)
    return out[:n_out], out[n_out:]


def _exchange(name, rider):
    def body():
        pass

    return _call(body, name=name, args=(), in_specs=[], out_specs=[], out_shape=[], rider=rider)[1]


def _all_reduce_small(part):
    def body(p_ref, o_ref, slots, send_sems, recv_sems):
        me = _me()
        my_idx = _linear(me)
        slots[my_idx] = p_ref[...]
        sends = []
        for rel in range(1, N_DEV):
            cp = pltpu.make_async_remote_copy(
                src_ref=p_ref, dst_ref=slots.at[my_idx], send_sem=send_sems.at[rel - 1],
                recv_sem=recv_sems.at[rel - 1], device_id=_peer(me, rel), device_id_type=MESH_T)
            cp.start()
            sends.append(cp)
        for rel in range(1, N_DEV):
            frm = _peer(me, rel)
            pltpu.make_async_remote_copy(
                src_ref=p_ref, dst_ref=slots.at[_linear(frm)], send_sem=send_sems.at[rel - 1],
                recv_sem=recv_sems.at[rel - 1], device_id=frm, device_id_type=MESH_T).wait_recv()
        for cp in sends:
            cp.wait_send()
        total = slots[0]
        for d in range(1, N_DEV):
            total = total + slots[d]
        o_ref[...] = total

    return pl.pallas_call(
        body, name="all_reduce_small",
        in_specs=[pl.BlockSpec(memory_space=pltpu.VMEM)], out_specs=pl.BlockSpec(memory_space=pltpu.VMEM),
        out_shape=jax.ShapeDtypeStruct(part.shape, F32),
        scratch_shapes=[pltpu.VMEM((N_DEV,) + part.shape, F32), pltpu.SemaphoreType.DMA((7,)),
                        pltpu.SemaphoreType.DMA((7,))],
        compiler_params=_params(),
    )(part)


def _adamw_math(w, g, m, v):
    m = ADAM_B1 * m + (1.0 - ADAM_B1) * g
    v = ADAM_B2 * v + (1.0 - ADAM_B2) * jnp.square(g)
    m_hat = m / (1.0 - ADAM_B1 ** ADAM_STEP)
    v_hat = v / (1.0 - ADAM_B2 ** ADAM_STEP)
    delta = -ADAM_LR * (m_hat / (jnp.sqrt(v_hat) + ADAM_EPS) + ADAM_WD * w)
    return delta, m, v


def _adamw_big(name, shares, w, m, v):
    rows, cols = w.shape
    tr = min(rows, 256)

    def body(s_ref, w_ref, m_ref, v_ref, g_ref, d_ref, nm_ref, nv_ref):
        g = s_ref[0].astype(F32)
        for d in range(1, N_DEV):
            g = g + s_ref[d].astype(F32)
        g_ref[...] = g
        d_ref[...], nm_ref[...], nv_ref[...] = _adamw_math(w_ref[...], g, m_ref[...], v_ref[...])

    blk = pl.BlockSpec((tr, cols), lambda i: (i, 0))
    return pl.pallas_call(
        body, name=name, grid=(rows // tr,),
        in_specs=[pl.BlockSpec((N_DEV, tr, cols), lambda i: (0, i, 0)), blk, blk, blk],
        out_specs=[blk] * 4, out_shape=[jax.ShapeDtypeStruct(w.shape, F32)] * 4,
        compiler_params=_params(("parallel",)),
    )(shares, w, m, v)


def _adamw_small(quads):
    n = len(quads)

    def body(*refs):
        ins, outs = refs[:4 * n], refs[4 * n:]
        for p in range(n):
            g_ref, w_ref, m_ref, v_ref = ins[4 * p:4 * p + 4]
            d_ref, nm_ref, nv_ref = outs[3 * p:3 * p + 3]
            d_ref[...], nm_ref[...], nv_ref[...] = _adamw_math(w_ref[...], g_ref[...], m_ref[...], v_ref[...])

    flat = [a for quad in quads for a in quad]
    out = pl.pallas_call(
        body, name="adamw_small",
        out_shape=[jax.ShapeDtypeStruct(quad[1].shape, F32) for quad in quads for _ in range(3)],
        compiler_params=_params(),
    )(*flat)
    return [tuple(out[3 * p:3 * p + 3]) for p in range(n)]


def kernel(x, norm1_g, w_in, q_norm_g, k_norm_g, rel_bias, conv_w, conv_b, w_attn_proj, w_conv_proj, w_gate, b_gate, w_out, norm2_g, w_up, w_down, loss_target, m_norm1_g, m_w_in, m_q_norm_g, m_k_norm_g, m_rel_bias, m_conv_w, m_conv_b, m_w_attn_proj, m_w_conv_proj, m_w_gate, m_b_gate, m_w_out, m_norm2_g, m_w_up, m_w_down, v_norm1_g, v_w_in, v_q_norm_g, v_k_norm_g, v_rel_bias, v_conv_w, v_conv_b, v_w_attn_proj, v_w_conv_proj, v_w_gate, v_b_gate, v_w_out, v_norm2_g, v_w_up, v_w_down):
    my_idx = _linear(_me())
    big_w = (w_in, w_attn_proj, w_conv_proj, w_gate, w_out, w_up, w_down)
    big_m = (m_w_in, m_w_attn_proj, m_w_conv_proj, m_w_gate, m_w_out, m_w_up, m_w_down)
    big_v = (v_w_in, v_w_attn_proj, v_w_conv_proj, v_w_gate, v_w_out, v_w_up, v_w_down)
    big_names = ("w_in", "w_attn_proj", "w_conv_proj", "w_gate", "w_out", "w_up", "w_down")

    conv_w_tile = jnp.pad(conv_w, ((0, SUBLANES - conv_w.shape[0]), (0, 0)))
    shards = _cast_shards(big_w)
    w_in_full, conv_w_rows = _exchange("all_gather_w_in", _Gather((shards[0], conv_w_tile), (1, 1)))

    dx, shares, small = _local_step(x[0], loss_target[0], norm1_g, q_norm_g, k_norm_g, rel_bias, conv_w_rows[:3],
                                    conv_b, b_gate, norm2_g, w_in_full, tuple(shards[1:5]), tuple(shards[5:7]), True)

    big_out = [_adamw_big("adamw_" + name, s, w, m, v)
               for name, s, w, m, v in zip(big_names, shares, big_w, big_m, big_v)]

    tot = _all_reduce_small(small)
    g_rel_bias = jnp.concatenate(
        [jnp.zeros((N_HEADS, 1), F32), tot[10:26, MAX_REL + 1:KW][:, ::-1], tot[10:26, KW:KW + 1]], axis=1)
    g_conv_w = lax.dynamic_slice(tot[4:7], (0, my_idx * LANES), (3, LANES))
    small_g = [tot[0:1], tot[1:2, :HEAD_DIM], tot[2:3, :HEAD_DIM], g_rel_bias, g_conv_w, tot[3:4],
               tot[7:9].reshape(1, 2 * D_MODEL), tot[9:10]]
    small_w = (norm1_g, q_norm_g, k_norm_g, rel_bias, conv_w, conv_b, b_gate, norm2_g)
    small_m = (m_norm1_g, m_q_norm_g, m_k_norm_g, m_rel_bias, m_conv_w, m_conv_b, m_b_gate, m_norm2_g)
    small_v = (v_norm1_g, v_q_norm_g, v_k_norm_g, v_rel_bias, v_conv_w, v_conv_b, v_b_gate, v_norm2_g)

    def two_d(a):
        return a.reshape(1, -1) if a.ndim == 1 else a

    small_out = _adamw_small([(g, two_d(w), two_d(m), two_d(v))
                              for g, w, m, v in zip(small_g, small_w, small_m, small_v)])

    order = ("norm1_g", "w_in", "q_norm_g", "k_norm_g", "rel_bias", "conv_w", "conv_b", "w_attn_proj", "w_conv_proj",
             "w_gate", "b_gate", "w_out", "norm2_g", "w_up", "w_down")
    small_names = ("norm1_g", "q_norm_g", "k_norm_g", "rel_bias", "conv_w", "conv_b", "b_gate", "norm2_g")
    res = {}
    for name, (g, d, nm, nv) in zip(big_names, big_out):
        res[name] = (g, d, nm, nv)
    for name, g, w, (d, nm, nv) in zip(small_names, small_g, small_w, small_out):
        res[name] = tuple(a.reshape(w.shape) for a in (g, d, nm, nv))
    loss = tot[26, 0]
    return (loss, dx[None], *[res[n][0] for n in order], *[res[n][1] for n in order],
            *[res[n][2] for n in order], *[res[n][3] for n in order])
```

```python
import functools

import jax
import jax.numpy as jnp
from jax import lax
from jax.experimental import pallas as pl
from jax.experimental.pallas import tpu as pltpu

F32 = jnp.float32
BF16 = jnp.bfloat16

D_MODEL = 1024
N_HEADS = 16
HEAD_DIM = 64
CHUNK = 64
N_PREV_CHUNKS = 8
MAX_REL = 256
D_FF = 4096
EPS = 1e-6
NEG_INF = -1e30
N_DEV = 8

ADAM_LR = 0.001
ADAM_B1 = 0.9
ADAM_B2 = 0.999
ADAM_EPS = 1e-08
ADAM_WD = 0.01
ADAM_STEP = 10

LANES = 128
SUBLANES = 8
VMEM_LIMIT = 48 * 1024 * 1024
QB = 256
KW = 3 * QB
SKEW = 1024

MESH_T = pl.DeviceIdType.MESH


def _dot(a, b):
    return jnp.dot(a, b, preferred_element_type=F32)


def _dot_nt(a, b):
    return lax.dot_general(a, b, (((1,), (1,)), ((), ())), preferred_element_type=F32)


def _dot_tn(a, b):
    return lax.dot_general(a, b, (((0,), (0,)), ((), ())), preferred_element_type=F32)


def _params(sem=None):
    return pltpu.CompilerParams(dimension_semantics=sem, vmem_limit_bytes=VMEM_LIMIT)


def _fold8(v):
    rows, n = v.shape
    return v.reshape(rows // SUBLANES, SUBLANES, n).sum(axis=0)


def _head_sum_matrix():
    r = lax.broadcasted_iota(jnp.int32, (LANES, LANES), 0) // HEAD_DIM
    c = lax.broadcasted_iota(jnp.int32, (LANES, LANES), 1) // HEAD_DIM
    return (r == c).astype(BF16)


def _head_sums(v, e):
    hi = v.astype(BF16)
    lo = (v - hi.astype(F32)).astype(BF16)
    return _dot(hi, e) + _dot(lo, e)


def _in_proj(x, g1, w_in, rider=None):
    t = x.shape[0]
    tm = min(t, 1024)
    n_slab = w_in.shape[1] // D_MODEL

    def body(x_ref, g_ref, w_ref, proj_ref, h_ref):
        @pl.when(pl.program_id(1) == 0)
        def _():
            xf = x_ref[...]
            r = lax.rsqrt(jnp.mean(xf * xf, axis=-1, keepdims=True) + EPS)
            h_ref[...] = (xf * r * g_ref[...]).astype(BF16)

        proj_ref[...] = _dot(h_ref[...], w_ref[...]).astype(BF16)

    return _call(
        body, name="in_proj", grid=(t // tm, n_slab), args=(x, g1, w_in),
        in_specs=[pl.BlockSpec((tm, D_MODEL), lambda i, k: (i, 0)),
                  pl.BlockSpec((1, D_MODEL), lambda i, k: (0, 0)),
                  pl.BlockSpec((D_MODEL, D_MODEL), lambda i, k: (0, k))],
        out_specs=[pl.BlockSpec((tm, D_MODEL), lambda i, k: (i, k)),
                   pl.BlockSpec((tm, D_MODEL), lambda i, k: (i, 0))],
        out_shape=[jax.ShapeDtypeStruct((t, n_slab * D_MODEL), BF16), jax.ShapeDtypeStruct((t, D_MODEL), BF16)],
        semantics=("parallel", "arbitrary"), rider=rider)


def _gate_proj(h, w_g, b_g):
    t = h.shape[0]
    tm = min(t, 1024)

    def body(h_ref, w_ref, b_ref, o_ref):
        o_ref[...] = jax.nn.sigmoid(_dot(h_ref[...], w_ref[...]) + b_ref[...]).astype(BF16)

    return pl.pallas_call(
        body, name="gate_proj", grid=(t // tm, 2),
        in_specs=[pl.BlockSpec((tm, D_MODEL), lambda i, k: (i, 0)),
                  pl.BlockSpec((D_MODEL, D_MODEL), lambda i, k: (0, k)),
                  pl.BlockSpec((1, D_MODEL), lambda i, k: (0, k))],
        out_specs=pl.BlockSpec((tm, D_MODEL), lambda i, k: (i, k)),
        out_shape=jax.ShapeDtypeStruct((t, 2 * D_MODEL), BF16),
        compiler_params=_params(("parallel", "arbitrary")),
    )(h, w_g, b_g)


def _qknorm_fwd(proj, gq, gk):
    t = proj.shape[0]
    tm = min(t, 512)
    scale = HEAD_DIM ** -0.5

    def body(q_ref, k_ref, gq_ref, gk_ref, qn_ref, kn_ref):
        e = _head_sum_matrix()
        for src, g_ref, dst, sc in ((q_ref, gq_ref, qn_ref, scale), (k_ref, gk_ref, kn_ref, 1.0)):
            for s in range(D_MODEL // LANES):
                sl = slice(s * LANES, (s + 1) * LANES)
                xf = src[:, sl].astype(F32)
                r = lax.rsqrt(_head_sums(xf * xf, e) * (1.0 / HEAD_DIM) + EPS)
                dst[:, sl] = (xf * r * g_ref[:, sl] * sc).astype(BF16)

    return pl.pallas_call(
        body, name="qknorm_fwd", grid=(t // tm,),
        in_specs=[pl.BlockSpec((tm, D_MODEL), lambda i: (i, 0)),
                  pl.BlockSpec((tm, D_MODEL), lambda i: (i, 1)),
                  pl.BlockSpec((1, D_MODEL), lambda i: (0, 0)),
                  pl.BlockSpec((1, D_MODEL), lambda i: (0, 0))],
        out_specs=[pl.BlockSpec((tm, D_MODEL), lambda i: (i, 0))] * 2,
        out_shape=[jax.ShapeDtypeStruct((t, D_MODEL), BF16)] * 2,
        compiler_params=_params(("parallel",)),
    )(proj, proj, gq, gk)


def _bias_tiles(rel_bias):
    by_dist = jnp.concatenate(
        [rel_bias[:, :2 * MAX_REL], jnp.broadcast_to(rel_bias[:, 2 * MAX_REL:], (N_HEADS, 2 * MAX_REL))], axis=1)
    by_dist = by_dist.reshape(N_HEADS, 1, SKEW)

    def body(f_ref, o_ref):
        jj = lax.broadcasted_iota(jnp.int32, (QB, QB), 0)
        ii = lax.broadcasted_iota(jnp.int32, (QB, QB), 1)
        for w in range(KW // QB):
            pos = jnp.broadcast_to(f_ref[0, :, KW - QB * w:KW - QB * w + QB], (QB, QB))
            neg = jnp.broadcast_to(f_ref[0, :, KW - QB * (w + 1):KW - QB * w], (QB, QB))
            pos = pltpu.roll(pos, 0, 1, stride=1, stride_axis=0)
            neg = pltpu.roll(neg, 0, 1, stride=1, stride_axis=0)
            tile = jnp.where(ii >= jj, pos, neg)
            kc = (jj + QB * w) // CHUNK
            qc = ii // CHUNK
            band = (kc >= qc) & (kc <= qc + N_PREV_CHUNKS)
            o_ref[0, QB * w:QB * (w + 1), :] = jnp.where(band, tile, NEG_INF)

    return pl.pallas_call(
        body, name="bias_tiles", grid=(N_HEADS,),
        in_specs=[pl.BlockSpec((1, 1, SKEW), lambda h: (h, 0, 0))],
        out_specs=pl.BlockSpec((1, KW, QB), lambda h: (h, 0, 0)),
        out_shape=jax.ShapeDtypeStruct((N_HEADS, KW, QB), F32),
        compiler_params=_params(("parallel",)),
    )(by_dist)


def _window_specs(col0):
    return [pl.BlockSpec((QB, LANES), functools.partial(
        lambda p, b, back: (jnp.maximum(b - back, 0), col0 + p), back=back)) for back in (2, 1, 0)]


def _attn_fwd(qn, kn, proj, bias, rider=None):
    t = qn.shape[0]
    nb = t // QB
    v_col0 = 2 * D_MODEL // LANES

    def body(q_ref, k0, k1, k2, v0, v1, v2, bias_ref, o_ref):
        b = pl.program_id(1)
        q = q_ref[...]
        k = jnp.concatenate([k0[...], k1[...], k2[...]], axis=0)
        v = jnp.concatenate([v0[...], v1[...], v2[...]], axis=0)
        vt = v.astype(F32).T.astype(BF16)
        head_a = lax.broadcasted_iota(jnp.int32, (1, LANES), 1) < HEAD_DIM
        rows_a = lax.broadcasted_iota(jnp.int32, (LANES, 1), 0) < HEAD_DIM
        valid = lax.broadcasted_iota(jnp.int32, (KW, 1), 0) >= (2 - b) * QB
        outs = []
        for hh in range(2):
            mine = head_a if hh == 0 else jnp.logical_not(head_a)
            kh = jnp.where(mine, k, jnp.zeros_like(k))
            s = _dot_nt(kh, q) + bias_ref[hh]
            s = jnp.where(valid, s, NEG_INF)
            p = jnp.exp(s - jnp.max(s, axis=0, keepdims=True))
            l = jnp.sum(p, axis=0, keepdims=True)
            outs.append(_dot(vt, p.astype(BF16)) / l)
        o_ref[...] = jnp.where(rows_a, outs[0], outs[1]).T.astype(BF16)

    return _call(
        body, name="attn_fwd", grid=(D_MODEL // LANES, nb), args=(qn, kn, kn, kn, proj, proj, proj, bias),
        in_specs=[pl.BlockSpec((QB, LANES), lambda p, b: (b, p))] + _window_specs(0) + _window_specs(v_col0)
        + [pl.BlockSpec((2, KW, QB), lambda p, b: (p, 0, 0))],
        out_specs=[pl.BlockSpec((QB, LANES), lambda p, b: (b, p))],
        out_shape=[jax.ShapeDtypeStruct((t, D_MODEL), BF16)],
        semantics=("parallel", "arbitrary"), rider=rider)


def _shift_down(u, halo, n):
    rows = lax.broadcasted_iota(jnp.int32, (u.shape[0], 1), 0)
    out = pltpu.roll(u, n, 0)
    for j in range(n):
        out = jnp.where(rows == j, halo[SUBLANES - n + j:SUBLANES - n + j + 1, :], out)
    return out


def _shift_up(u, halo, n):
    tm = u.shape[0]
    rows = lax.broadcasted_iota(jnp.int32, (tm, 1), 0)
    out = pltpu.roll(u, tm - n, 0)
    for j in range(n):
        out = jnp.where(rows == tm - n + j, halo[j:j + 1, :], out)
    return out


def _conv_fwd(proj, conv_w, conv_b):
    t = proj.shape[0]
    tm = min(t, 512)
    hb = tm // SUBLANES

    def body(bg_ref, cg_ref, xc_ref, cgh_ref, xch_ref, w_ref, b_ref, o_ref):
        i = pl.program_id(0)
        u = cg_ref[...].astype(F32) * xc_ref[...].astype(F32)
        halo = cgh_ref[...].astype(F32) * xch_ref[...].astype(F32)
        halo = jnp.where(i > 0, halo, 0.0)
        w = w_ref[...]
        s = w[0:1] * _shift_down(u, halo, 2) + w[1:2] * _shift_down(u, halo, 1) + w[2:3] * u
        o_ref[...] = (bg_ref[...].astype(F32) * (b_ref[...] + s)).astype(BF16)

    def prev(col):
        return pl.BlockSpec((SUBLANES, D_MODEL), lambda i: (jnp.maximum(i * hb - 1, 0), col))

    return pl.pallas_call(
        body, name="conv_fwd", grid=(t // tm,),
        in_specs=[pl.BlockSpec((tm, D_MODEL), lambda i: (i, 3)),
                  pl.BlockSpec((tm, D_MODEL), lambda i: (i, 4)),
                  pl.BlockSpec((tm, D_MODEL), lambda i: (i, 5)),
                  prev(4), prev(5),
                  pl.BlockSpec((3, D_MODEL), lambda i: (0, 0)),
                  pl.BlockSpec((1, D_MODEL), lambda i: (0, 0))],
        out_specs=pl.BlockSpec((tm, D_MODEL), lambda i: (i, 0)),
        out_shape=jax.ShapeDtypeStruct((t, D_MODEL), BF16),
        compiler_params=_params(("parallel",)),
    )(proj, proj, proj, proj, proj, conv_w, conv_b)


def _mix_out(y_attn, y_conv, gates, x, w_ap, w_cp, w_out, g2):
    t = x.shape[0]
    tm = min(t, 512)

    def body(ya_in, yc_in, g_ref, x_ref, wap, wcp, wout, g2_ref, ya_ref, yc_ref, mg_ref, x1_ref, h2_ref):
        ya = _dot(ya_in[...], wap[...])
        yc = _dot(yc_in[...], wcp[...])
        ya_ref[...] = ya.astype(BF16)
        yc_ref[...] = yc.astype(BF16)
        merged = (g_ref[:, :D_MODEL].astype(F32) * ya + g_ref[:, D_MODEL:].astype(F32) * yc).astype(BF16)
        mg_ref[...] = merged
        x1 = x_ref[...] + _dot(merged, wout[...])
        x1_ref[...] = x1
        r = lax.rsqrt(jnp.mean(x1 * x1, axis=-1, keepdims=True) + EPS)
        h2_ref[...] = (x1 * r * g2_ref[...]).astype(BF16)

    row = pl.BlockSpec((tm, D_MODEL), lambda i: (i, 0))
    full = pl.BlockSpec((D_MODEL, D_MODEL), lambda i: (0, 0))
    return pl.pallas_call(
        body, name="mix_out", grid=(t // tm,),
        in_specs=[row, row, pl.BlockSpec((tm, 2 * D_MODEL), lambda i: (i, 0)), row, full, full, full,
                  pl.BlockSpec((1, D_MODEL), lambda i: (0, 0))],
        out_specs=[row] * 5,
        out_shape=[jax.ShapeDtypeStruct((t, D_MODEL), BF16)] * 3
        + [jax.ShapeDtypeStruct((t, D_MODEL), F32), jax.ShapeDtypeStruct((t, D_MODEL), BF16)],
        compiler_params=_params(("parallel",)),
    )(y_attn, y_conv, gates, x, w_ap, w_cp, w_out, g2)


def _mlp_fwd(h2, w_up, w_down, x1, target):
    t = h2.shape[0]
    tm = min(t, 512)
    tf = 1024
    nf = D_FF // tf

    def body(h2_ref, wup, wdn, x1_ref, tg_ref, a_ref, dy_ref, dyb_ref, loss_ref, acc):
        i, j = pl.program_id(0), pl.program_id(1)
        a = _dot(h2_ref[...], wup[...])
        a_ref[...] = a.astype(BF16)
        u = jnp.square(jnp.maximum(a, 0.0)).astype(BF16)
        part = _dot(u, wdn[...])

        @pl.when(j == 0)
        def _():
            acc[...] = part

        @pl.when(j > 0)
        def _():
            acc[...] += part

        @pl.when((i == 0) & (j == 0))
        def _():
            loss_ref[...] = jnp.zeros_like(loss_ref)

        @pl.when(j == nf - 1)
        def _():
            diff = x1_ref[...] + acc[...] - tg_ref[...]
            loss_ref[...] += _fold8(diff * diff)
            dy = diff * (1.0 / D_MODEL)
            dy_ref[...] = dy
            dyb_ref[...] = dy.astype(BF16)

    row = pl.BlockSpec((tm, D_MODEL), lambda i, j: (i, 0))
    return pl.pallas_call(
        body, name="mlp_fwd", grid=(t // tm, nf),
        in_specs=[row, pl.BlockSpec((D_MODEL, tf), lambda i, j: (0, j)),
                  pl.BlockSpec((tf, D_MODEL), lambda i, j: (j, 0)), row, row],
        out_specs=[pl.BlockSpec((tm, tf), lambda i, j: (i, j)), row, row,
                   pl.BlockSpec((SUBLANES, D_MODEL), lambda i, j: (0, 0))],
        out_shape=[jax.ShapeDtypeStruct((t, D_FF), BF16), jax.ShapeDtypeStruct((t, D_MODEL), F32),
                   jax.ShapeDtypeStruct((t, D_MODEL), BF16), jax.ShapeDtypeStruct((SUBLANES, D_MODEL), F32)],
        scratch_shapes=[pltpu.VMEM((tm, D_MODEL), F32)],
        compiler_params=_params(("arbitrary", "arbitrary")),
    )(h2, w_up, w_down, x1, target)


def _rmsnorm_bwd(xf, g, dh):
    r = lax.rsqrt(jnp.mean(xf * xf, axis=-1, keepdims=True) + EPS)
    xh = xf * r
    dxh = dh * g
    dx = r * (dxh - xh * jnp.mean(dxh * xh, axis=-1, keepdims=True))
    return dx, dh * xh


def _mlp_bwd(dyb, a, w_down, w_up, x1, dy, g2):
    t = dyb.shape[0]
    tm = min(t, 512)
    tf = 1024
    nf = D_FF // tf

    def body(dyb_ref, a_ref, wdn, wup, x1_ref, dy_ref, g2_ref, da_ref, dx1_ref, dx1b_ref, dg2_ref, acc):
        i, j = pl.program_id(0), pl.program_id(1)
        du = _dot_nt(dyb_ref[...], wdn[...])
        da = (du * (2.0 * jnp.maximum(a_ref[...].astype(F32), 0.0))).astype(BF16)
        da_ref[...] = da
        part = _dot_nt(da, wup[...])

        @pl.when(j == 0)
        def _():
            acc[...] = part

        @pl.when(j > 0)
        def _():
            acc[...] += part

        @pl.when((i == 0) & (j == 0))
        def _():
            dg2_ref[...] = jnp.zeros_like(dg2_ref)

        @pl.when(j == nf - 1)
        def _():
            dx, dg = _rmsnorm_bwd(x1_ref[...], g2_ref[...], acc[...])
            dx1 = dy_ref[...] + dx
            dx1_ref[...] = dx1
            dx1b_ref[...] = dx1.astype(BF16)
            dg2_ref[...] += _fold8(dg)

    row = pl.BlockSpec((tm, D_MODEL), lambda i, j: (i, 0))
    return pl.pallas_call(
        body, name="mlp_bwd", grid=(t // tm, nf),
        in_specs=[row, pl.BlockSpec((tm, tf), lambda i, j: (i, j)),
                  pl.BlockSpec((tf, D_MODEL), lambda i, j: (j, 0)),
                  pl.BlockSpec((D_MODEL, tf), lambda i, j: (0, j)), row, row,
                  pl.BlockSpec((1, D_MODEL), lambda i, j: (0, 0))],
        out_specs=[pl.BlockSpec((tm, tf), lambda i, j: (i, j)), row, row,
                   pl.BlockSpec((SUBLANES, D_MODEL), lambda i, j: (0, 0))],
        out_shape=[jax.ShapeDtypeStruct((t, D_FF), BF16), jax.ShapeDtypeStruct((t, D_MODEL), F32),
                   jax.ShapeDtypeStruct((t, D_MODEL), BF16), jax.ShapeDtypeStruct((SUBLANES, D_MODEL), F32)],
        scratch_shapes=[pltpu.VMEM((tm, D_MODEL), F32)],
        compiler_params=_params(("arbitrary", "arbitrary")),
    )(dyb, a, w_down, w_up, x1, dy, g2)


def _wgrad(name, lhs, rhs_list, rhs_slabs, relu_sq=False):
    t, m = lhs.shape
    tt = min(t, 512)
    tmo = min(m, 1024)
    n_slab = sum(rhs_slabs)
    starts = [sum(rhs_slabs[:n]) for n in range(len(rhs_slabs))]
    n_rhs = len(rhs_list)

    def body(*refs):
        l_ref, r_refs, o_ref, acc = refs[0], refs[1:1 + n_rhs], refs[1 + n_rhs], refs[2 + n_rhs]
        k, s = pl.program_id(1), pl.program_id(2)
        lv = l_ref[...]
        if relu_sq:
            lv = jnp.square(jnp.maximum(lv.astype(F32), 0.0)).astype(BF16)

        @pl.when(s == 0)
        def _():
            acc[...] = jnp.zeros_like(acc)

        for n in range(n_rhs):
            @pl.when((k >= starts[n]) & (k < starts[n] + rhs_slabs[n]))
            def _(n=n):
                acc[...] += _dot_tn(lv, r_refs[n][...])

        @pl.when(s == pl.num_programs(2) - 1)
        def _():
            o_ref[...] = acc[...].astype(BF16)

    def rhs_spec(n):
        lo, cnt = starts[n], rhs_slabs[n]

        def index(i, k, s):
            inside = (k >= lo) & (k < lo + cnt)
            return (jnp.where(inside, s, 0), jnp.clip(k - lo, 0, cnt - 1))
        return pl.BlockSpec((tt, D_MODEL), index)

    return pl.pallas_call(
        body, name=name, grid=(m // tmo, n_slab, t // tt),
        in_specs=[pl.BlockSpec((tt, tmo), lambda i, k, s: (s, i))] + [rhs_spec(n) for n in range(n_rhs)],
        out_specs=pl.BlockSpec((tmo, D_MODEL), lambda i, k, s: (i, k)),
        out_shape=jax.ShapeDtypeStruct((m, n_slab * D_MODEL), BF16),
        scratch_shapes=[pltpu.VMEM((tmo, D_MODEL), F32)],
        compiler_params=_params(("parallel", "parallel", "arbitrary")),
    )(lhs, *rhs_list)


def _mix_bwd(dx1b, gates, ya, yc, w_out, w_ap, w_cp, w_g):
    t = dx1b.shape[0]
    tm = min(t, 512)

    def body(dx_ref, g_ref, ya_ref, yc_ref, wout, wap, wcp, wg,
             dgp_ref, dya_ref, dyc_ref, dyat_ref, dycv_ref, dhg_ref, dbg_ref):
        dm = _dot_nt(dx_ref[...], wout[...])
        ga = g_ref[:, :D_MODEL].astype(F32)
        gc = g_ref[:, D_MODEL:].astype(F32)
        dya = (dm * ga).astype(BF16)
        dyc = (dm * gc).astype(BF16)
        dya_ref[...] = dya
        dyc_ref[...] = dyc
        dgpa = dm * ya_ref[...].astype(F32) * ga * (1.0 - ga)
        dgpc = dm * yc_ref[...].astype(F32) * gc * (1.0 - gc)

        @pl.when(pl.program_id(0) == 0)
        def _():
            dbg_ref[...] = jnp.zeros_like(dbg_ref)

        dbg_ref[:, :D_MODEL] += _fold8(dgpa)
        dbg_ref[:, D_MODEL:] += _fold8(dgpc)
        dgpa = dgpa.astype(BF16)
        dgpc = dgpc.astype(BF16)
        dgp_ref[:, :D_MODEL] = dgpa
        dgp_ref[:, D_MODEL:] = dgpc
        dyat_ref[...] = _dot_nt(dya, wap[...]).astype(BF16)
        dycv_ref[...] = _dot_nt(dyc, wcp[...]).astype(BF16)
        dhg_ref[...] = _dot_nt(dgpa, wg[:, :D_MODEL]) + _dot_nt(dgpc, wg[:, D_MODEL:])

    row = pl.BlockSpec((tm, D_MODEL), lambda i: (i, 0))
    row2 = pl.BlockSpec((tm, 2 * D_MODEL), lambda i: (i, 0))
    full = pl.BlockSpec((D_MODEL, D_MODEL), lambda i: (0, 0))
    return pl.pallas_call(
        body, name="mix_bwd", grid=(t // tm,),
        in_specs=[row, row2, row, row, full, full, full, pl.BlockSpec((D_MODEL, 2 * D_MODEL), lambda i: (0, 0))],
        out_specs=[row2, row, row, row, row, row, pl.BlockSpec((SUBLANES, 2 * D_MODEL), lambda i: (0, 0))],
        out_shape=[jax.ShapeDtypeStruct((t, 2 * D_MODEL), BF16)] + [jax.ShapeDtypeStruct((t, D_MODEL), BF16)] * 4
        + [jax.ShapeDtypeStruct((t, D_MODEL), F32), jax.ShapeDtypeStruct((SUBLANES, 2 * D_MODEL), F32)],
        compiler_params=_params(("arbitrary",)),
    )(dx1b, gates, ya, yc, w_out, w_ap, w_cp, w_g)


def _conv_bwd(dyconv, proj, conv_w, conv_b):
    t = proj.shape[0]
    tm = min(t, 512)
    hb = tm // SUBLANES
    last = t // SUBLANES - 1

    def body(dy_ref, dyn_ref, bg_ref, bgn_ref, cg_ref, cgp_ref, xc_ref, xcp_ref, w_ref, b_ref,
             o_ref, dcb_ref, dcw_ref):
        i = pl.program_id(0)
        cg = cg_ref[...].astype(F32)
        xc = xc_ref[...].astype(F32)
        bg = bg_ref[...].astype(F32)
        u = cg * xc
        prev = jnp.where(i > 0, cgp_ref[...].astype(F32) * xcp_ref[...].astype(F32), 0.0)
        u1 = _shift_down(u, prev, 1)
        u2 = _shift_down(u, prev, 2)
        w = w_ref[...]
        conv = b_ref[...] + (w[0:1] * u2 + w[1:2] * u1 + w[2:3] * u)
        dy = dy_ref[...].astype(F32)
        dconv = dy * bg
        nxt = jnp.where(i < pl.num_programs(0) - 1, dyn_ref[...].astype(F32) * bgn_ref[...].astype(F32), 0.0)
        du = w[2:3] * dconv + w[1:2] * _shift_up(dconv, nxt, 1) + w[0:1] * _shift_up(dconv, nxt, 2)
        o_ref[:, :D_MODEL] = (dy * conv).astype(BF16)
        o_ref[:, D_MODEL:2 * D_MODEL] = (du * xc).astype(BF16)
        o_ref[:, 2 * D_MODEL:] = (du * cg).astype(BF16)

        @pl.when(i == 0)
        def _():
            dcb_ref[...] = jnp.zeros_like(dcb_ref)
            dcw_ref[...] = jnp.zeros_like(dcw_ref)

        dcb_ref[...] += _fold8(dconv)
        dcw_ref[0:SUBLANES] += _fold8(dconv * u2)
        dcw_ref[SUBLANES:2 * SUBLANES] += _fold8(dconv * u1)
        dcw_ref[2 * SUBLANES:] += _fold8(dconv * u)

    def prev(col):
        return pl.BlockSpec((SUBLANES, D_MODEL), lambda i: (jnp.maximum(i * hb - 1, 0), col))

    def nxt(col):
        return pl.BlockSpec((SUBLANES, D_MODEL), lambda i: (jnp.minimum((i + 1) * hb, last), col))

    def cur(col):
        return pl.BlockSpec((tm, D_MODEL), lambda i: (i, col))

    return pl.pallas_call(
        body, name="conv_bwd", grid=(t // tm,),
        in_specs=[cur(0), nxt(0), cur(3), nxt(3), cur(4), prev(4), cur(5), prev(5),
                  pl.BlockSpec((3, D_MODEL), lambda i: (0, 0)), pl.BlockSpec((1, D_MODEL), lambda i: (0, 0))],
        out_specs=[pl.BlockSpec((tm, 3 * D_MODEL), lambda i: (i, 0)),
                   pl.BlockSpec((SUBLANES, D_MODEL), lambda i: (0, 0)),
                   pl.BlockSpec((3 * SUBLANES, D_MODEL), lambda i: (0, 0))],
        out_shape=[jax.ShapeDtypeStruct((t, 3 * D_MODEL), BF16), jax.ShapeDtypeStruct((SUBLANES, D_MODEL), F32),
                   jax.ShapeDtypeStruct((3 * SUBLANES, D_MODEL), F32)],
        compiler_params=_params(("arbitrary",)),
    )(dyconv, dyconv, proj, proj, proj, proj, proj, proj, conv_w, conv_b)


def _attn_bwd(qn, kn, proj, dyattn, bias, rider=None):
    t = qn.shape[0]
    nb = t // QB
    v_col0 = 2 * D_MODEL // LANES

    def body(q_ref, k0, k1, k2, v0, v1, v2, do_ref, bias_ref, dq_ref, dk_ref, dv_ref, db_ref, acck, accv):
        b = pl.program_id(1)

        @pl.when(b == 0)
        def _():
            acck[...] = jnp.zeros_like(acck)
            accv[...] = jnp.zeros_like(accv)
            db_ref[...] = jnp.zeros_like(db_ref)

        @pl.when(b < nb)
        def _():
            q = q_ref[...]
            do = do_ref[...]
            k = jnp.concatenate([k0[...], k1[...], k2[...]], axis=0)
            v = jnp.concatenate([v0[...], v1[...], v2[...]], axis=0)
            qt = q.astype(F32).T.astype(BF16)
            dot = do.astype(F32).T.astype(BF16)
            kt = k.astype(F32).T.astype(BF16)
            head_a = lax.broadcasted_iota(jnp.int32, (1, LANES), 1) < HEAD_DIM
            rows_a = lax.broadcasted_iota(jnp.int32, (LANES, 1), 0) < HEAD_DIM
            valid = lax.broadcasted_iota(jnp.int32, (KW, 1), 0) >= (2 - b) * QB
            dqt = jnp.zeros((LANES, QB), F32)
            dkt = jnp.zeros((LANES, KW), F32)
            dvt = jnp.zeros((LANES, KW), F32)
            for hh in range(2):
                mine = head_a if hh == 0 else jnp.logical_not(head_a)
                rows = rows_a if hh == 0 else jnp.logical_not(rows_a)
                kh = jnp.where(mine, k, jnp.zeros_like(k))
                vh = jnp.where(mine, v, jnp.zeros_like(v))
                s = _dot_nt(kh, q) + bias_ref[hh]
                s = jnp.where(valid, s, NEG_INF)
                p = jnp.exp(s - jnp.max(s, axis=0, keepdims=True))
                p = p / jnp.sum(p, axis=0, keepdims=True)
                dp = _dot_nt(vh, do)
                ds = p * (dp - jnp.sum(p * dp, axis=0, keepdims=True))
                db_ref[hh] += ds
                pb = p.astype(BF16)
                dsb = ds.astype(BF16)
                dvt = dvt + _dot_nt(jnp.where(rows, dot, jnp.zeros_like(dot)), pb)
                dkt = dkt + _dot_nt(jnp.where(rows, qt, jnp.zeros_like(qt)), dsb)
                dqt = dqt + _dot(jnp.where(rows, kt, jnp.zeros_like(kt)), dsb)
            dq_ref[...] = dqt.T
            for w in range(3):
                slot = lax.rem(b + w + 1, 3)
                cols = slice(w * QB, (w + 1) * QB)
                if w == 2:
                    acck[slot] = dkt[:, cols]
                    accv[slot] = dvt[:, cols]
                else:
                    acck[slot] += dkt[:, cols]
                    accv[slot] += dvt[:, cols]

        done = lax.rem(b + 1, 3)
        dk_ref[...] = acck[done].T
        dv_ref[...] = accv[done].T.astype(BF16)

    def cur(p, b):
        return (jnp.minimum(b, nb - 1), p)

    def window(col0):
        return [pl.BlockSpec((QB, LANES), functools.partial(
            lambda p, b, back: (jnp.maximum(jnp.minimum(b, nb - 1) - back, 0), col0 + p), back=back))
            for back in (2, 1, 0)]

    def done_block(p, b):
        return (jnp.maximum(b - 2, 0), p)

    return _call(
        body, name="attn_bwd", grid=(D_MODEL // LANES, nb + 2),
        args=(qn, kn, kn, kn, proj, proj, proj, dyattn, bias),
        in_specs=[pl.BlockSpec((QB, LANES), cur)] + window(0) + window(v_col0)
        + [pl.BlockSpec((QB, LANES), cur), pl.BlockSpec((2, KW, QB), lambda p, b: (p, 0, 0))],
        out_specs=[pl.BlockSpec((QB, LANES), cur), pl.BlockSpec((QB, LANES), done_block),
                   pl.BlockSpec((QB, LANES), done_block), pl.BlockSpec((2, KW, QB), lambda p, b: (p, 0, 0))],
        out_shape=[jax.ShapeDtypeStruct((t, D_MODEL), F32), jax.ShapeDtypeStruct((t, D_MODEL), F32),
                   jax.ShapeDtypeStruct((t, D_MODEL), BF16), jax.ShapeDtypeStruct((N_HEADS, KW, QB), F32)],
        scratch_shapes=[pltpu.VMEM((3, LANES, QB), F32), pltpu.VMEM((3, LANES, QB), F32)],
        semantics=("parallel", "arbitrary"), rider=rider)


def _qknorm_bwd(proj, dqn, dkn, gq, gk):
    t = proj.shape[0]
    tm = min(t, 512)
    scale = HEAD_DIM ** -0.5

    def body(q_ref, k_ref, dqn_ref, dkn_ref, gq_ref, gk_ref, o_ref, dgq_ref, dgk_ref):
        e = _head_sum_matrix()

        @pl.when(pl.program_id(0) == 0)
        def _():
            dgq_ref[...] = jnp.zeros_like(dgq_ref)
            dgk_ref[...] = jnp.zeros_like(dgk_ref)

        for n, (src, dn_ref, g_ref, dg_ref, sc) in enumerate(
                ((q_ref, dqn_ref, gq_ref, dgq_ref, scale), (k_ref, dkn_ref, gk_ref, dgk_ref, 1.0))):
            for s in range(D_MODEL // LANES):
                sl = slice(s * LANES, (s + 1) * LANES)
                xf = src[:, sl].astype(F32)
                r = lax.rsqrt(_head_sums(xf * xf, e) * (1.0 / HEAD_DIM) + EPS)
                xh = xf * r
                dn = dn_ref[:, sl] * sc
                dg_ref[:, sl] += _fold8(dn * xh)
                dxh = dn * g_ref[:, sl]
                mean = _head_sums(dxh * xh, e) * (1.0 / HEAD_DIM)
                o_ref[:, n * D_MODEL + s * LANES:n * D_MODEL + (s + 1) * LANES] = (r * (dxh - xh * mean)).astype(BF16)

    row = pl.BlockSpec((tm, D_MODEL), lambda i: (i, 0))
    vec = pl.BlockSpec((1, D_MODEL), lambda i: (0, 0))
    acc = pl.BlockSpec((SUBLANES, D_MODEL), lambda i: (0, 0))
    return pl.pallas_call(
        body, name="qknorm_bwd", grid=(t // tm,),
        in_specs=[row, pl.BlockSpec((tm, D_MODEL), lambda i: (i, 1)), row, row, vec, vec],
        out_specs=[pl.BlockSpec((tm, 2 * D_MODEL), lambda i: (i, 0)), acc, acc],
        out_shape=[jax.ShapeDtypeStruct((t, 2 * D_MODEL), BF16)] + [jax.ShapeDtypeStruct((SUBLANES, D_MODEL), F32)] * 2,
        compiler_params=_params(("arbitrary",)),
    )(proj, proj, dqn, dkn, gq, gk)


def _in_bwd(dqk, dv, dconv, w_in, dhg, x, g1, dx1, rider=None):
    t = x.shape[0]
    tm = min(t, 512)

    def body(dqk_ref, dv_ref, dc_ref, w_ref, dhg_ref, x_ref, g_ref, dx1_ref, dx_ref, dg_ref, acc):
        i, k = pl.program_id(0), pl.program_id(1)

        @pl.when(k == 0)
        def _():
            acc[...] = dhg_ref[...]

        @pl.when(k < 2)
        def _():
            acc[...] += _dot_nt(dqk_ref[...], w_ref[...])

        @pl.when(k == 2)
        def _():
            acc[...] += _dot_nt(dv_ref[...], w_ref[...])

        @pl.when(k > 2)
        def _():
            acc[...] += _dot_nt(dc_ref[...], w_ref[...])

        @pl.when((i == 0) & (k == 0))
        def _():
            dg_ref[...] = jnp.zeros_like(dg_ref)

        @pl.when(k == 5)
        def _():
            dx, dg = _rmsnorm_bwd(x_ref[...], g_ref[...], acc[...])
            dx_ref[...] = dx1_ref[...] + dx
            dg_ref[...] += _fold8(dg)

    row = pl.BlockSpec((tm, D_MODEL), lambda i, k: (i, 0))
    return _call(
        body, name="in_bwd", grid=(t // tm, 6), args=(dqk, dv, dconv, w_in, dhg, x, g1, dx1),
        in_specs=[pl.BlockSpec((tm, D_MODEL), lambda i, k: (i, jnp.minimum(k, 1))), row,
                  pl.BlockSpec((tm, D_MODEL), lambda i, k: (i, jnp.clip(k - 3, 0, 2))),
                  pl.BlockSpec((D_MODEL, D_MODEL), lambda i, k: (0, k)), row, row,
                  pl.BlockSpec((1, D_MODEL), lambda i, k: (0, 0)), row],
        out_specs=[row, pl.BlockSpec((SUBLANES, D_MODEL), lambda i, k: (0, 0))],
        out_shape=[jax.ShapeDtypeStruct((t, D_MODEL), F32), jax.ShapeDtypeStruct((SUBLANES, D_MODEL), F32)],
        scratch_shapes=[pltpu.VMEM((tm, D_MODEL), F32)],
        semantics=("arbitrary", "arbitrary"), rider=rider)


def _bias_grad_fold(dbias):
    def body(d_ref, o_ref):
        jj = lax.broadcasted_iota(jnp.int32, (QB, QB), 0)
        ii = lax.broadcasted_iota(jnp.int32, (QB, QB), 1)
        flip = (jj + ii == QB - 1).astype(BF16)
        low = jj + ii < QB
        pos, neg = [], []
        for w in range(KW // QB):
            x = d_ref[0, QB * w:QB * (w + 1), :]
            hi = x.astype(BF16)
            r1 = x - hi.astype(F32)
            mid = r1.astype(BF16)
            lo = (r1 - mid.astype(F32)).astype(BF16)
            xr = _dot(hi, flip) + _dot(mid, flip) + _dot(lo, flip)
            for keep, acc in ((low, pos), (jnp.logical_not(low), neg)):
                part = pltpu.roll(jnp.where(keep, xr, 0.0), 0, 1, stride=1, stride_axis=0)
                acc.append(jnp.sum(part, axis=0, keepdims=True))
        far = pos[1] + neg[0] + pos[0]
        o_ref[0] = jnp.zeros((SUBLANES, QB), F32)
        o_ref[0, 0:1, :] = neg[2]
        o_ref[0, 1:2, :] = pos[2] + neg[1]
        o_ref[0, 2:3, :] = jnp.broadcast_to(jnp.sum(far, axis=-1, keepdims=True), (1, QB))

    return pl.pallas_call(
        body, name="bias_grad_fold", grid=(N_HEADS,),
        in_specs=[pl.BlockSpec((1, KW, QB), lambda h: (h, 0, 0))],
        out_specs=pl.BlockSpec((1, SUBLANES, QB), lambda h: (h, 0, 0)),
        out_shape=jax.ShapeDtypeStruct((N_HEADS, SUBLANES, QB), F32),
        compiler_params=_params(("parallel",)),
    )(dbias)


def _small_partials(dg1, dgq, dgk, dcb, dcw, dbg, dg2, dbias_fold, loss_tile):
    def head_fold(v):
        acc = v[:, 0:LANES]
        for s in range(1, D_MODEL // LANES):
            acc = acc + v[:, s * LANES:(s + 1) * LANES]
        return acc + pltpu.roll(acc, HEAD_DIM, 1)

    def body(dg1_ref, dgq_ref, dgk_ref, dcb_ref, dcw_ref, dbg_ref, dg2_ref, db_ref, loss_ref, o_ref):
        o_ref[...] = jnp.zeros_like(o_ref)
        o_ref[0:1, :] = jnp.sum(dg1_ref[...], axis=0, keepdims=True)
        o_ref[1:2, 0:LANES] = head_fold(jnp.sum(dgq_ref[...], axis=0, keepdims=True))
        o_ref[2:3, 0:LANES] = head_fold(jnp.sum(dgk_ref[...], axis=0, keepdims=True))
        o_ref[3:4, :] = jnp.sum(dcb_ref[...], axis=0, keepdims=True)
        for j in range(3):
            o_ref[4 + j:5 + j, :] = jnp.sum(dcw_ref[j * SUBLANES:(j + 1) * SUBLANES, :], axis=0, keepdims=True)
        o_ref[7:8, :] = jnp.sum(dbg_ref[:, :D_MODEL], axis=0, keepdims=True)
        o_ref[8:9, :] = jnp.sum(dbg_ref[:, D_MODEL:], axis=0, keepdims=True)
        o_ref[9:10, :] = jnp.sum(dg2_ref[...], axis=0, keepdims=True)
        for h in range(N_HEADS):
            for part in range(3):
                o_ref[10 + h:11 + h, part * QB:(part + 1) * QB] = db_ref[h, part:part + 1, :]
        loss = (0.5 / D_MODEL) * jnp.sum(jnp.sum(loss_ref[...], axis=0, keepdims=True), axis=-1, keepdims=True)
        o_ref[26:27, :] = jnp.broadcast_to(loss, (1, D_MODEL))

    return pl.pallas_call(
        body, name="small_partials",
        out_shape=jax.ShapeDtypeStruct((32, D_MODEL), F32),
        compiler_params=_params(),
    )(dg1, dgq, dgk, dcb, dcw, dbg, dg2, dbias_fold, loss_tile)


MID_AXES = (0, 0, 1, 0)
MLP_AXES = (1, 0)


def _local_step(x, target, norm1_g, q_norm_g, k_norm_g, rel_bias, conv_w, conv_b, b_gate, norm2_g,
                w_in, mid_w, mlp_w, distributed):
    g1 = norm1_g.reshape(1, D_MODEL)
    g2 = norm2_g.reshape(1, D_MODEL)
    gq = jnp.tile(q_norm_g, N_HEADS).reshape(1, D_MODEL)
    gk = jnp.tile(k_norm_g, N_HEADS).reshape(1, D_MODEL)
    cb = conv_b.reshape(1, D_MODEL)
    bias = _bias_tiles(rel_bias)

    (proj, h), got = _in_proj(x, g1, w_in, rider=_Gather(mid_w, MID_AXES) if distributed else None)
    w_ap, w_cp, w_g, w_out = got if distributed else mid_w
    gates = _gate_proj(h, w_g, b_gate.reshape(1, 2 * D_MODEL))
    qn, kn = _qknorm_fwd(proj, gq, gk)
    (y_attn,), got = _attn_fwd(qn, kn, proj, bias, rider=_Gather(mlp_w, MLP_AXES) if distributed else None)
    w_up, w_down = got if distributed else mlp_w
    y_conv = _conv_fwd(proj, conv_w, cb)
    ya, yc, merged, x1, h2 = _mix_out(y_attn, y_conv, gates, x, w_ap, w_cp, w_out, g2)
    a, dy, dyb, loss_tile = _mlp_fwd(h2, w_up, w_down, x1, target)

    da, dx1, dx1b, dg2 = _mlp_bwd(dyb, a, w_down, w_up, x1, dy, g2)
    gw_down = _wgrad("wgrad_down", a, [dyb], [1], relu_sq=True)
    gw_up = _wgrad("wgrad_up", h2, [da], [D_FF // D_MODEL])
    dgp, dya, dyc, dyattn, dyconv, dhg, dbg = _mix_bwd(dx1b, gates, ya, yc, w_out, w_ap, w_cp, w_g)
    gw_out = _wgrad("wgrad_out", merged, [dx1b], [1])
    gw_ap = _wgrad("wgrad_attn_proj", y_attn, [dya], [1])
    gw_cp = _wgrad("wgrad_conv_proj", y_conv, [dyc], [1])
    gw_g = _wgrad("wgrad_gate", h, [dgp], [2])
    dconv, dcb, dcw = _conv_bwd(dyconv, proj, conv_w, cb)
    early = (gw_ap, gw_cp, gw_g, gw_out, gw_up, gw_down)
    (dqn, dkn, dv, dbias), early_shares = _attn_bwd(
        qn, kn, proj, dyattn, bias, rider=_Scatter(early, MID_AXES + MLP_AXES) if distributed else None)
    dqk, dgq, dgk = _qknorm_bwd(proj, dqn, dkn, gq, gk)
    gw_in = _wgrad("wgrad_in", h, [dqk, dv, dconv], [2, 1, 3])
    (dx, dg1), in_shares = _in_bwd(dqk, dv, dconv, w_in, dhg, x, g1, dx1,
                                   rider=_Scatter((gw_in,), (1,)) if distributed else None)
    small = _small_partials(dg1, dgq, dgk, dcb, dcw, dbg, dg2, _bias_grad_fold(dbias), loss_tile)
    grads = tuple(in_shares) + tuple(early_shares) if distributed else (gw_in,) + early
    return dx, grads, small


def _me():
    return lax.axis_index("x"), lax.axis_index("y"), lax.axis_index("c")


def _peer(me, rel):
    x, y, c = me
    return (1 - x if rel & 4 else x, 1 - y if rel & 2 else y, 1 - c if rel & 1 else c)


def _linear(dev):
    return 4 * dev[0] + 2 * dev[1] + dev[2]


BIG_AXES = (1, 0, 0, 1, 0, 1, 0)


def _block(ref, axis, idx, size):
    return ref.at[pl.ds(idx * size, size), :] if axis == 0 else ref.at[:, pl.ds(idx * size, size)]


def _cast_shards(shards):
    def body(*refs):
        for src, dst in zip(refs[:len(shards)], refs[len(shards):]):
            dst[...] = src[...].astype(BF16)

    return pl.pallas_call(
        body, name="cast_shards",
        out_shape=[jax.ShapeDtypeStruct(s.shape, BF16) for s in shards],
        compiler_params=_params(),
    )(*shards)


class _Gather:
    def __init__(self, shards, axes):
        self.arrays, self.axes, self.n = list(shards), tuple(axes), len(shards)
        self.sizes = [s.shape[axis] for s, axis in zip(shards, axes)]
        self.out_shape = []
        for s, axis in zip(shards, axes):
            shape = (s.shape[0] * N_DEV, s.shape[1]) if axis == 0 else (s.shape[0], s.shape[1] * N_DEV)
            self.out_shape.append(jax.ShapeDtypeStruct(shape, s.dtype))
        self.scratch = [pltpu.SemaphoreType.DMA((self.n, 7)), pltpu.SemaphoreType.DMA((self.n, 7)),
                        pltpu.SemaphoreType.DMA((self.n,))]

    def _copies(self, srcs, outs, sems):
        send_sems, recv_sems, local_sems = sems
        me = _me()
        sibling = _peer(me, 1)
        chips = [_peer(me, rel) for rel in (4, 2, 6)]

        def rows(a, dev):
            return _block(outs[a], self.axes[a], _linear(dev), self.sizes[a])

        def copy(a, k, block_dev, to, src=None):
            return pltpu.make_async_remote_copy(
                src_ref=rows(a, block_dev) if src is None else src, dst_ref=rows(a, block_dev),
                send_sem=send_sems.at[a, k], recv_sem=recv_sems.at[a, k], device_id=to, device_id_type=MESH_T)

        own = [pltpu.make_async_copy(srcs[a], rows(a, me), local_sems.at[a]) for a in range(self.n)]
        first = []
        for a in range(self.n):
            first.append(copy(a, 0, me, sibling, src=srcs[a]))
            for j, chip in enumerate(chips):
                first.append(copy(a, 1 + j, me, chip, src=srcs[a]))
        return me, sibling, chips, copy, own, first

    def start(self, srcs, outs, sems):
        _, _, _, _, own, first = self._copies(srcs, outs, sems)
        for cp in own + first:
            cp.start()

    def finish(self, srcs, outs, sems):
        me, sibling, chips, copy, own, first = self._copies(srcs, outs, sems)
        passed = []
        for a in range(self.n):
            for j, chip in enumerate(chips):
                copy(a, 1 + j, chip, me).wait_recv()
                fwd = copy(a, 4 + j, chip, sibling)
                fwd.start()
                passed.append(fwd)
        for a in range(self.n):
            copy(a, 0, sibling, me).wait_recv()
            for j, chip in enumerate(chips):
                copy(a, 4 + j, _peer(chip, 1), me).wait_recv()
        for cp in first + passed:
            cp.wait_send()
        for cp in own:
            cp.wait()


class _Scatter:
    def __init__(self, grads, axes):
        self.arrays, self.axes, self.n = list(grads), tuple(axes), len(grads)
        self.sizes = [g.shape[axis] // N_DEV for g, axis in zip(grads, axes)]
        self.out_shape = []
        for g, axis in zip(grads, axes):
            shard = (g.shape[0] // N_DEV, g.shape[1]) if axis == 0 else (g.shape[0], g.shape[1] // N_DEV)
            self.out_shape.append(jax.ShapeDtypeStruct((N_DEV,) + shard, g.dtype))
        self.scratch = [pltpu.SemaphoreType.DMA((self.n, 7)), pltpu.SemaphoreType.DMA((self.n, 7)),
                        pltpu.SemaphoreType.DMA((self.n,))]

    def _copies(self, srcs, outs, sems):
        send_sems, recv_sems, local_sems = sems
        me = _me()
        my_idx = _linear(me)

        def mine(a):
            return _block(srcs[a], self.axes[a], my_idx, self.sizes[a])

        own = [pltpu.make_async_copy(mine(a), outs[a].at[my_idx], local_sems.at[a]) for a in range(self.n)]
        sends, recvs = [], []
        for a in range(self.n):
            for rel in range(1, N_DEV):
                peer = _peer(me, rel)
                sends.append(pltpu.make_async_remote_copy(
                    src_ref=_block(srcs[a], self.axes[a], _linear(peer), self.sizes[a]), dst_ref=outs[a].at[my_idx],
                    send_sem=send_sems.at[a, rel - 1], recv_sem=recv_sems.at[a, rel - 1],
                    device_id=peer, device_id_type=MESH_T))
                recvs.append(pltpu.make_async_remote_copy(
                    src_ref=mine(a), dst_ref=outs[a].at[_linear(peer)],
                    send_sem=send_sems.at[a, rel - 1], recv_sem=recv_sems.at[a, rel - 1],
                    device_id=peer, device_id_type=MESH_T))
        return own, sends, recvs

    def start(self, srcs, outs, sems):
        own, sends, _ = self._copies(srcs, outs, sems)
        for cp in own + sends:
            cp.start()

    def finish(self, srcs, outs, sems):
        own, sends, recvs = self._copies(srcs, outs, sems)
        for cp in recvs:
            cp.wait_recv()
        for cp in sends:
            cp.wait_send()
        for cp in own:
            cp.wait()


def _call(body, *, name, args, in_specs, out_specs, out_shape, grid=(), scratch_shapes=(), semantics=None,
          rider=None):
    if rider is None:
        return pl.pallas_call(
            body, name=name, grid=grid, in_specs=in_specs, out_specs=out_specs, out_shape=out_shape,
            scratch_shapes=list(scratch_shapes), compiler_params=_params(semantics))(*args), None
    n_in, n_out, n_scr, r = len(in_specs), len(out_specs), len(scratch_shapes), rider.n

    def wrapped(*refs):
        ins, r_ins = refs[:n_in], refs[n_in:n_in + r]
        outs = refs[n_in + r:n_in + r + n_out]
        r_outs = refs[n_in + r + n_out:n_in + 2 * r + n_out]
        scr = refs[n_in + 2 * r + n_out:n_in + 2 * r + n_out + n_scr]
        sems = refs[n_in + 2 * r + n_out + n_scr:]
        first, last = None, None
        for ax in range(len(grid)):
            f, l = pl.program_id(ax) == 0, pl.program_id(ax) == pl.num_programs(ax) - 1
            first = f if first is None else first & f
            last = l if last is None else last & l
        if first is None:
            rider.start(r_ins, r_outs, sems)
            body(*ins, *outs, *scr)
            rider.finish(r_ins, r_outs, sems)
            return

        @pl.when(first)
        def _():
            rider.start(r_ins, r_outs, sems)

        body(*ins, *outs, *scr)

        @pl.when(last)
        def _():
            rider.finish(r_ins, r_outs, sems)

    any_spec = pl.BlockSpec(memory_space=pl.ANY)
    out = pl.pallas_call(
        wrapped, name=name, grid=grid, in_specs=list(in_specs) + [any_spec] * r,
        out_specs=list(out_specs) + [any_spec] * r, out_shape=list(out_shape) + rider.out_shape,
        scratch_shapes=list(scratch_shapes) + rider.scratch,
        compiler_params=_params(None if semantics is None else ("arbitrary",) * len(semantics)),
    )(*args, *rider.arrays)
    return out[:n_out], out[n_out:]


def _exchange(name, rider):
    def body():
        pass

    return _call(body, name=name, args=(), in_specs=[], out_specs=[], out_shape=[], rider=rider)[1]


def _all_reduce_small(part):
    def body(p_ref, o_ref, slots, send_sems, recv_sems):
        me = _me()
        my_idx = _linear(me)
        slots[my_idx] = p_ref[...]
        sends = []
        for rel in range(1, N_DEV):
            cp = pltpu.make_async_remote_copy(
                src_ref=p_ref, dst_ref=slots.at[my_idx], send_sem=send_sems.at[rel - 1],
                recv_sem=recv_sems.at[rel - 1], device_id=_peer(me, rel), device_id_type=MESH_T)
            cp.start()
            sends.append(cp)
        for rel in range(1, N_DEV):
            frm = _peer(me, rel)
            pltpu.make_async_remote_copy(
                src_ref=p_ref, dst_ref=slots.at[_linear(frm)], send_sem=send_sems.at[rel - 1],
                recv_sem=recv_sems.at[rel - 1], device_id=frm, device_id_type=MESH_T).wait_recv()
        for cp in sends:
            cp.wait_send()
        total = slots[0]
        for d in range(1, N_DEV):
            total = total + slots[d]
        o_ref[...] = total

    return pl.pallas_call(
        body, name="all_reduce_small",
        in_specs=[pl.BlockSpec(memory_space=pltpu.VMEM)], out_specs=pl.BlockSpec(memory_space=pltpu.VMEM),
        out_shape=jax.ShapeDtypeStruct(part.shape, F32),
        scratch_shapes=[pltpu.VMEM((N_DEV,) + part.shape, F32), pltpu.SemaphoreType.DMA((7,)),
                        pltpu.SemaphoreType.DMA((7,))],
        compiler_params=_params(),
    )(part)


def _adamw_math(w, g, m, v):
    m = ADAM_B1 * m + (1.0 - ADAM_B1) * g
    v = ADAM_B2 * v + (1.0 - ADAM_B2) * jnp.square(g)
    m_hat = m / (1.0 - ADAM_B1 ** ADAM_STEP)
    v_hat = v / (1.0 - ADAM_B2 ** ADAM_STEP)
    delta = -ADAM_LR * (m_hat / (jnp.sqrt(v_hat) + ADAM_EPS) + ADAM_WD * w)
    return delta, m, v


def _adamw_big(name, shares, w, m, v):
    rows, cols = w.shape
    tr = min(rows, 256)

    def body(s_ref, w_ref, m_ref, v_ref, g_ref, d_ref, nm_ref, nv_ref):
        g = s_ref[0].astype(F32)
        for d in range(1, N_DEV):
            g = g + s_ref[d].astype(F32)
        g_ref[...] = g
        d_ref[...], nm_ref[...], nv_ref[...] = _adamw_math(w_ref[...], g, m_ref[...], v_ref[...])

    blk = pl.BlockSpec((tr, cols), lambda i: (i, 0))
    return pl.pallas_call(
        body, name=name, grid=(rows // tr,),
        in_specs=[pl.BlockSpec((N_DEV, tr, cols), lambda i: (0, i, 0)), blk, blk, blk],
        out_specs=[blk] * 4, out_shape=[jax.ShapeDtypeStruct(w.shape, F32)] * 4,
        compiler_params=_params(("parallel",)),
    )(shares, w, m, v)


def _adamw_small(quads):
    n = len(quads)

    def body(*refs):
        ins, outs = refs[:4 * n], refs[4 * n:]
        for p in range(n):
            g_ref, w_ref, m_ref, v_ref = ins[4 * p:4 * p + 4]
            d_ref, nm_ref, nv_ref = outs[3 * p:3 * p + 3]
            d_ref[...], nm_ref[...], nv_ref[...] = _adamw_math(w_ref[...], g_ref[...], m_ref[...], v_ref[...])

    flat = [a for quad in quads for a in quad]
    out = pl.pallas_call(
        body, name="adamw_small",
        out_shape=[jax.ShapeDtypeStruct(quad[1].shape, F32) for quad in quads for _ in range(3)],
        compiler_params=_params(),
    )(*flat)
    return [tuple(out[3 * p:3 * p + 3]) for p in range(n)]


def kernel(x, norm1_g, w_in, q_norm_g, k_norm_g, rel_bias, conv_w, conv_b, w_attn_proj, w_conv_proj, w_gate, b_gate, w_out, norm2_g, w_up, w_down, loss_target, m_norm1_g, m_w_in, m_q_norm_g, m_k_norm_g, m_rel_bias, m_conv_w, m_conv_b, m_w_attn_proj, m_w_conv_proj, m_w_gate, m_b_gate, m_w_out, m_norm2_g, m_w_up, m_w_down, v_norm1_g, v_w_in, v_q_norm_g, v_k_norm_g, v_rel_bias, v_conv_w, v_conv_b, v_w_attn_proj, v_w_conv_proj, v_w_gate, v_b_gate, v_w_out, v_norm2_g, v_w_up, v_w_down):
    my_idx = _linear(_me())
    big_w = (w_in, w_attn_proj, w_conv_proj, w_gate, w_out, w_up, w_down)
    big_m = (m_w_in, m_w_attn_proj, m_w_conv_proj, m_w_gate, m_w_out, m_w_up, m_w_down)
    big_v = (v_w_in, v_w_attn_proj, v_w_conv_proj, v_w_gate, v_w_out, v_w_up, v_w_down)
    big_names = ("w_in", "w_attn_proj", "w_conv_proj", "w_gate", "w_out", "w_up", "w_down")

    conv_w_tile = jnp.pad(conv_w, ((0, SUBLANES - conv_w.shape[0]), (0, 0)))
    shards = _cast_shards(big_w)
    w_in_full, conv_w_rows = _exchange("all_gather_w_in", _Gather((shards[0], conv_w_tile), (1, 1)))

    dx, shares, small = _local_step(x[0], loss_target[0], norm1_g, q_norm_g, k_norm_g, rel_bias, conv_w_rows[:3],
                                    conv_b, b_gate, norm2_g, w_in_full, tuple(shards[1:5]), tuple(shards[5:7]), True)

    big_out = [_adamw_big("adamw_" + name, s, w, m, v)
               for name, s, w, m, v in zip(big_names, shares, big_w, big_m, big_v)]

    tot = _all_reduce_small(small)
    g_rel_bias = jnp.concatenate(
        [tot[10:26, :QB][:, ::-1], tot[10:26, QB:2 * QB][:, ::-1], tot[10:26, 2 * QB:2 * QB + 1]], axis=1)
    g_conv_w = lax.dynamic_slice(tot[4:7], (0, my_idx * LANES), (3, LANES))
    small_g = [tot[0:1], tot[1:2, :HEAD_DIM], tot[2:3, :HEAD_DIM], g_rel_bias, g_conv_w, tot[3:4],
               tot[7:9].reshape(1, 2 * D_MODEL), tot[9:10]]
    small_w = (norm1_g, q_norm_g, k_norm_g, rel_bias, conv_w, conv_b, b_gate, norm2_g)
    small_m = (m_norm1_g, m_q_norm_g, m_k_norm_g, m_rel_bias, m_conv_w, m_conv_b, m_b_gate, m_norm2_g)
    small_v = (v_norm1_g, v_q_norm_g, v_k_norm_g, v_rel_bias, v_conv_w, v_conv_b, v_b_gate, v_norm2_g)

    def two_d(a):
        return a.reshape(1, -1) if a.ndim == 1 else a

    small_out = _adamw_small([(g, two_d(w), two_d(m), two_d(v))
                              for g, w, m, v in zip(small_g, small_w, small_m, small_v)])

    order = ("norm1_g", "w_in", "q_norm_g", "k_norm_g", "rel_bias", "conv_w", "conv_b", "w_attn_proj", "w_conv_proj",
             "w_gate", "b_gate", "w_out", "norm2_g", "w_up", "w_down")
    small_names = ("norm1_g", "q_norm_g", "k_norm_g", "rel_bias", "conv_w", "conv_b", "b_gate", "norm2_g")
    res = {}
    for name, (g, d, nm, nv) in zip(big_names, big_out):
        res[name] = (g, d, nm, nv)
    for name, g, w, (d, nm, nv) in zip(small_names, small_g, small_w, small_out):
        res[name] = tuple(a.reshape(w.shape) for a in (g, d, nm, nv))
    loss = tot[26, 0]
    return (loss, dx[None], *[res[n][0] for n in order], *[res[n][1] for n in order],
            *[res[n][2] for n in order], *[res[n][3] for n in order])
```

```python
import functools

import jax
import jax.numpy as jnp
from jax import lax
from jax.experimental import pallas as pl
from jax.experimental.pallas import tpu as pltpu

F32 = jnp.float32
BF16 = jnp.bfloat16

D_MODEL = 1024
N_HEADS = 16
HEAD_DIM = 64
CHUNK = 64
N_PREV_CHUNKS = 8
MAX_REL = 256
D_FF = 4096
EPS = 1e-6
NEG_INF = -1e30
N_DEV = 8

ADAM_LR = 0.001
ADAM_B1 = 0.9
ADAM_B2 = 0.999
ADAM_EPS = 1e-08
ADAM_WD = 0.01
ADAM_STEP = 10

LANES = 128
SUBLANES = 8
VMEM_LIMIT = 56 * 1024 * 1024
QB = 256
KW = 3 * QB
PAIRS = 4
SLAB = PAIRS * LANES
SKEW = 1024

MESH_T = pl.DeviceIdType.MESH


def _dot(a, b):
    return jnp.dot(a, b, preferred_element_type=F32)


def _dot_nt(a, b):
    return lax.dot_general(a, b, (((1,), (1,)), ((), ())), preferred_element_type=F32)


def _dot_tn(a, b):
    return lax.dot_general(a, b, (((0,), (0,)), ((), ())), preferred_element_type=F32)


def _params(sem=None):
    return pltpu.CompilerParams(dimension_semantics=sem, vmem_limit_bytes=VMEM_LIMIT)


def _resident(shape):
    return pl.BlockSpec(shape, lambda *_: (0,) * len(shape), pipeline_mode=pl.Buffered(1))


def _fold8(v):
    rows, n = v.shape
    return v.reshape(rows // SUBLANES, SUBLANES, n).sum(axis=0)


def _head_sum_matrix():
    r = lax.broadcasted_iota(jnp.int32, (LANES, LANES), 0) // HEAD_DIM
    c = lax.broadcasted_iota(jnp.int32, (LANES, LANES), 1) // HEAD_DIM
    return (r == c).astype(BF16)


def _head_sums(v, e):
    hi = v.astype(BF16)
    lo = (v - hi.astype(F32)).astype(BF16)
    return _dot(hi, e) + _dot(lo, e)


def _in_proj(x, g1, w_in, rider=None):
    t = x.shape[0]
    tm = min(t, 512)
    n_out = w_in.shape[1]

    def body(x_ref, g_ref, w_ref, proj_ref, h_ref):
        xf = x_ref[...]
        r = lax.rsqrt(jnp.mean(xf * xf, axis=-1, keepdims=True) + EPS)
        h = (xf * r * g_ref[...]).astype(BF16)
        h_ref[...] = h
        for k in range(n_out // D_MODEL):
            cols = slice(k * D_MODEL, (k + 1) * D_MODEL)
            proj_ref[:, cols] = _dot(h, w_ref[:, cols]).astype(BF16)

    return _call(
        body, name="in_proj", grid=(t // tm,), args=(x, g1, w_in),
        in_specs=[pl.BlockSpec((tm, D_MODEL), lambda i: (i, 0)),
                  pl.BlockSpec((1, D_MODEL), lambda i: (0, 0)),
                  _resident((D_MODEL, n_out))],
        out_specs=[pl.BlockSpec((tm, n_out), lambda i: (i, 0)),
                   pl.BlockSpec((tm, D_MODEL), lambda i: (i, 0))],
        out_shape=[jax.ShapeDtypeStruct((t, n_out), BF16), jax.ShapeDtypeStruct((t, D_MODEL), BF16)],
        semantics=("parallel",), rider=rider)


def _gate_proj(h, w_g, b_g):
    t = h.shape[0]
    tm = min(t, 1024)

    def body(h_ref, w_ref, b_ref, o_ref):
        o_ref[...] = jax.nn.sigmoid(_dot(h_ref[...], w_ref[...]) + b_ref[...]).astype(BF16)

    return pl.pallas_call(
        body, name="gate_proj", grid=(t // tm, 2),
        in_specs=[pl.BlockSpec((tm, D_MODEL), lambda i, k: (i, 0)),
                  pl.BlockSpec((D_MODEL, D_MODEL), lambda i, k: (0, k)),
                  pl.BlockSpec((1, D_MODEL), lambda i, k: (0, k))],
        out_specs=pl.BlockSpec((tm, D_MODEL), lambda i, k: (i, k)),
        out_shape=jax.ShapeDtypeStruct((t, 2 * D_MODEL), BF16),
        compiler_params=_params(("parallel", "arbitrary")),
    )(h, w_g, b_g)


def _qknorm_fwd(proj, gq, gk):
    t = proj.shape[0]
    tm = min(t, 512)
    scale = HEAD_DIM ** -0.5

    def body(q_ref, k_ref, gq_ref, gk_ref, qn_ref, kn_ref):
        e = _head_sum_matrix()
        for src, g_ref, dst, sc in ((q_ref, gq_ref, qn_ref, scale), (k_ref, gk_ref, kn_ref, 1.0)):
            for s in range(D_MODEL // LANES):
                sl = slice(s * LANES, (s + 1) * LANES)
                xf = src[:, sl].astype(F32)
                r = lax.rsqrt(_head_sums(xf * xf, e) * (1.0 / HEAD_DIM) + EPS)
                dst[:, sl] = (xf * r * g_ref[:, sl] * sc).astype(BF16)

    return pl.pallas_call(
        body, name="qknorm_fwd", grid=(t // tm,),
        in_specs=[pl.BlockSpec((tm, D_MODEL), lambda i: (i, 0)),
                  pl.BlockSpec((tm, D_MODEL), lambda i: (i, 1)),
                  pl.BlockSpec((1, D_MODEL), lambda i: (0, 0)),
                  pl.BlockSpec((1, D_MODEL), lambda i: (0, 0))],
        out_specs=[pl.BlockSpec((tm, D_MODEL), lambda i: (i, 0))] * 2,
        out_shape=[jax.ShapeDtypeStruct((t, D_MODEL), BF16)] * 2,
        compiler_params=_params(("parallel",)),
    )(proj, proj, gq, gk)


def _bias_tiles(rel_bias):
    by_dist = jnp.concatenate(
        [rel_bias[:, :2 * MAX_REL], jnp.broadcast_to(rel_bias[:, 2 * MAX_REL:], (N_HEADS, 2 * MAX_REL))], axis=1)
    by_dist = by_dist.reshape(N_HEADS, 1, SKEW)

    def body(f_ref, o_ref):
        jj = lax.broadcasted_iota(jnp.int32, (QB, QB), 0)
        ii = lax.broadcasted_iota(jnp.int32, (QB, QB), 1)
        for w in range(KW // QB):
            pos = jnp.broadcast_to(f_ref[0, :, KW - QB * w:KW - QB * w + QB], (QB, QB))
            neg = jnp.broadcast_to(f_ref[0, :, KW - QB * (w + 1):KW - QB * w], (QB, QB))
            pos = pltpu.roll(pos, 0, 1, stride=1, stride_axis=0)
            neg = pltpu.roll(neg, 0, 1, stride=1, stride_axis=0)
            tile = jnp.where(ii >= jj, pos, neg)
            kc = (jj + QB * w) // CHUNK
            qc = ii // CHUNK
            band = (kc >= qc) & (kc <= qc + N_PREV_CHUNKS)
            o_ref[0, QB * w:QB * (w + 1), :] = jnp.where(band, tile, NEG_INF)

    return pl.pallas_call(
        body, name="bias_tiles", grid=(N_HEADS,),
        in_specs=[pl.BlockSpec((1, 1, SKEW), lambda h: (h, 0, 0))],
        out_specs=pl.BlockSpec((1, KW, QB), lambda h: (h, 0, 0)),
        out_shape=jax.ShapeDtypeStruct((N_HEADS, KW, QB), F32),
        compiler_params=_params(("parallel",)),
    )(by_dist)


def _window_specs(col0):
    return [pl.BlockSpec((QB, SLAB), functools.partial(
        lambda p, b, back: (jnp.maximum(b - back, 0), col0 + p), back=back)) for back in (2, 1, 0)]


def _attn_fwd(qn, kn, proj, bias, rider=None):
    t = qn.shape[0]
    nb = t // QB
    v_col0 = 2 * D_MODEL // SLAB

    def body(q_ref, k0, k1, k2, v0, v1, v2, bias_ref, o_ref):
        b = pl.program_id(1)
        head_a = lax.broadcasted_iota(jnp.int32, (1, LANES), 1) < HEAD_DIM
        rows_a = lax.broadcasted_iota(jnp.int32, (LANES, 1), 0) < HEAD_DIM
        valid = lax.broadcasted_iota(jnp.int32, (KW, 1), 0) >= (2 - b) * QB
        for hp in range(PAIRS):
            sl = slice(hp * LANES, (hp + 1) * LANES)
            q = q_ref[:, sl]
            k = jnp.concatenate([k0[:, sl], k1[:, sl], k2[:, sl]], axis=0)
            v = jnp.concatenate([v0[:, sl], v1[:, sl], v2[:, sl]], axis=0)
            vt = v.astype(F32).T.astype(BF16)
            outs = []
            for hh in range(2):
                mine = head_a if hh == 0 else jnp.logical_not(head_a)
                kh = jnp.where(mine, k, jnp.zeros_like(k))
                s = _dot_nt(kh, q) + bias_ref[2 * hp + hh]
                s = jnp.where(valid, s, NEG_INF)
                p = jnp.exp(s - jnp.max(s, axis=0, keepdims=True))
                l = jnp.sum(p, axis=0, keepdims=True)
                outs.append(_dot(vt, p.astype(BF16)) / l)
            o_ref[:, sl] = jnp.where(rows_a, outs[0], outs[1]).T.astype(BF16)

    return _call(
        body, name="attn_fwd", grid=(D_MODEL // SLAB, nb), args=(qn, kn, kn, kn, proj, proj, proj, bias),
        in_specs=[pl.BlockSpec((QB, SLAB), lambda p, b: (b, p))] + _window_specs(0) + _window_specs(v_col0)
        + [pl.BlockSpec((2 * PAIRS, KW, QB), lambda p, b: (p, 0, 0))],
        out_specs=[pl.BlockSpec((QB, SLAB), lambda p, b: (b, p))],
        out_shape=[jax.ShapeDtypeStruct((t, D_MODEL), BF16)],
        semantics=("parallel", "arbitrary"), rider=rider)


def _shift_down(u, halo, n):
    rows = lax.broadcasted_iota(jnp.int32, (u.shape[0], 1), 0)
    out = pltpu.roll(u, n, 0)
    for j in range(n):
        out = jnp.where(rows == j, halo[SUBLANES - n + j:SUBLANES - n + j + 1, :], out)
    return out


def _shift_up(u, halo, n):
    tm = u.shape[0]
    rows = lax.broadcasted_iota(jnp.int32, (tm, 1), 0)
    out = pltpu.roll(u, tm - n, 0)
    for j in range(n):
        out = jnp.where(rows == tm - n + j, halo[j:j + 1, :], out)
    return out


def _conv_fwd(proj, conv_w, conv_b):
    t = proj.shape[0]
    tm = min(t, 512)
    hb = tm // SUBLANES

    def body(bg_ref, cg_ref, xc_ref, cgh_ref, xch_ref, w_ref, b_ref, o_ref):
        i = pl.program_id(0)
        u = cg_ref[...].astype(F32) * xc_ref[...].astype(F32)
        halo = cgh_ref[...].astype(F32) * xch_ref[...].astype(F32)
        halo = jnp.where(i > 0, halo, 0.0)
        w = w_ref[...]
        s = w[0:1] * _shift_down(u, halo, 2) + w[1:2] * _shift_down(u, halo, 1) + w[2:3] * u
        o_ref[...] = (bg_ref[...].astype(F32) * (b_ref[...] + s)).astype(BF16)

    def prev(col):
        return pl.BlockSpec((SUBLANES, D_MODEL), lambda i: (jnp.maximum(i * hb - 1, 0), col))

    return pl.pallas_call(
        body, name="conv_fwd", grid=(t // tm,),
        in_specs=[pl.BlockSpec((tm, D_MODEL), lambda i: (i, 3)),
                  pl.BlockSpec((tm, D_MODEL), lambda i: (i, 4)),
                  pl.BlockSpec((tm, D_MODEL), lambda i: (i, 5)),
                  prev(4), prev(5),
                  pl.BlockSpec((3, D_MODEL), lambda i: (0, 0)),
                  pl.BlockSpec((1, D_MODEL), lambda i: (0, 0))],
        out_specs=pl.BlockSpec((tm, D_MODEL), lambda i: (i, 0)),
        out_shape=jax.ShapeDtypeStruct((t, D_MODEL), BF16),
        compiler_params=_params(("parallel",)),
    )(proj, proj, proj, proj, proj, conv_w, conv_b)


def _mix_out(y_attn, y_conv, gates, x, w_ap, w_cp, w_out, g2):
    t = x.shape[0]
    tm = min(t, 512)

    def body(ya_in, yc_in, g_ref, x_ref, wap, wcp, wout, g2_ref, ya_ref, yc_ref, mg_ref, x1_ref, h2_ref):
        ya = _dot(ya_in[...], wap[...])
        yc = _dot(yc_in[...], wcp[...])
        ya_ref[...] = ya.astype(BF16)
        yc_ref[...] = yc.astype(BF16)
        merged = (g_ref[:, :D_MODEL].astype(F32) * ya + g_ref[:, D_MODEL:].astype(F32) * yc).astype(BF16)
        mg_ref[...] = merged
        x1 = x_ref[...] + _dot(merged, wout[...])
        x1_ref[...] = x1
        r = lax.rsqrt(jnp.mean(x1 * x1, axis=-1, keepdims=True) + EPS)
        h2_ref[...] = (x1 * r * g2_ref[...]).astype(BF16)

    row = pl.BlockSpec((tm, D_MODEL), lambda i: (i, 0))
    full = _resident((D_MODEL, D_MODEL))
    return pl.pallas_call(
        body, name="mix_out", grid=(t // tm,),
        in_specs=[row, row, pl.BlockSpec((tm, 2 * D_MODEL), lambda i: (i, 0)), row, full, full, full,
                  pl.BlockSpec((1, D_MODEL), lambda i: (0, 0))],
        out_specs=[row] * 5,
        out_shape=[jax.ShapeDtypeStruct((t, D_MODEL), BF16)] * 3
        + [jax.ShapeDtypeStruct((t, D_MODEL), F32), jax.ShapeDtypeStruct((t, D_MODEL), BF16)],
        compiler_params=_params(("parallel",)),
    )(y_attn, y_conv, gates, x, w_ap, w_cp, w_out, g2)


def _mlp_fwd(h2, w_up, w_down, x1, target):
    t = h2.shape[0]
    tm = min(t, 512)
    tf = 1024
    nf = D_FF // tf

    def body(h2_ref, wup, wdn, x1_ref, tg_ref, a_ref, dy_ref, dyb_ref, loss_ref):
        h2v = h2_ref[...]
        acc = None
        for j in range(nf):
            cols = slice(j * tf, (j + 1) * tf)
            a = _dot(h2v, wup[:, cols])
            a_ref[:, cols] = a.astype(BF16)
            part = _dot(jnp.square(jnp.maximum(a, 0.0)).astype(BF16), wdn[cols, :])
            acc = part if acc is None else acc + part

        @pl.when(pl.program_id(0) == 0)
        def _():
            loss_ref[...] = jnp.zeros_like(loss_ref)

        diff = x1_ref[...] + acc - tg_ref[...]
        loss_ref[...] += _fold8(diff * diff)
        dy = diff * (1.0 / D_MODEL)
        dy_ref[...] = dy
        dyb_ref[...] = dy.astype(BF16)

    row = pl.BlockSpec((tm, D_MODEL), lambda i: (i, 0))
    return pl.pallas_call(
        body, name="mlp_fwd", grid=(t // tm,),
        in_specs=[row, _resident((D_MODEL, D_FF)), _resident((D_FF, D_MODEL)), row, row],
        out_specs=[pl.BlockSpec((tm, D_FF), lambda i: (i, 0)), row, row,
                   pl.BlockSpec((SUBLANES, D_MODEL), lambda i: (0, 0))],
        out_shape=[jax.ShapeDtypeStruct((t, D_FF), BF16), jax.ShapeDtypeStruct((t, D_MODEL), F32),
                   jax.ShapeDtypeStruct((t, D_MODEL), BF16), jax.ShapeDtypeStruct((SUBLANES, D_MODEL), F32)],
        compiler_params=_params(("arbitrary",)),
    )(h2, w_up, w_down, x1, target)


def _rmsnorm_bwd(xf, g, dh):
    r = lax.rsqrt(jnp.mean(xf * xf, axis=-1, keepdims=True) + EPS)
    xh = xf * r
    dxh = dh * g
    dx = r * (dxh - xh * jnp.mean(dxh * xh, axis=-1, keepdims=True))
    return dx, dh * xh


def _mlp_bwd(dyb, a, w_down, w_up, x1, dy, g2):
    t = dyb.shape[0]
    tm = min(t, 512)
    tf = 1024
    nf = D_FF // tf

    def body(dyb_ref, a_ref, wdn, wup, x1_ref, dy_ref, g2_ref, da_ref, dx1_ref, dx1b_ref, dg2_ref):
        dyv = dyb_ref[...]
        acc = None
        for j in range(nf):
            cols = slice(j * tf, (j + 1) * tf)
            du = _dot_nt(dyv, wdn[cols, :])
            da = (du * (2.0 * jnp.maximum(a_ref[:, cols].astype(F32), 0.0))).astype(BF16)
            da_ref[:, cols] = da
            part = _dot_nt(da, wup[:, cols])
            acc = part if acc is None else acc + part

        @pl.when(pl.program_id(0) == 0)
        def _():
            dg2_ref[...] = jnp.zeros_like(dg2_ref)

        dx, dg = _rmsnorm_bwd(x1_ref[...], g2_ref[...], acc)
        dx1 = dy_ref[...] + dx
        dx1_ref[...] = dx1
        dx1b_ref[...] = dx1.astype(BF16)
        dg2_ref[...] += _fold8(dg)

    row = pl.BlockSpec((tm, D_MODEL), lambda i: (i, 0))
    wide = pl.BlockSpec((tm, D_FF), lambda i: (i, 0))
    return pl.pallas_call(
        body, name="mlp_bwd", grid=(t // tm,),
        in_specs=[row, wide, _resident((D_FF, D_MODEL)), _resident((D_MODEL, D_FF)), row, row,
                  pl.BlockSpec((1, D_MODEL), lambda i: (0, 0))],
        out_specs=[wide, row, row, pl.BlockSpec((SUBLANES, D_MODEL), lambda i: (0, 0))],
        out_shape=[jax.ShapeDtypeStruct((t, D_FF), BF16), jax.ShapeDtypeStruct((t, D_MODEL), F32),
                   jax.ShapeDtypeStruct((t, D_MODEL), BF16), jax.ShapeDtypeStruct((SUBLANES, D_MODEL), F32)],
        compiler_params=_params(("arbitrary",)),
    )(dyb, a, w_down, w_up, x1, dy, g2)


def _wgrad(name, lhs, rhs_list, rhs_slabs, relu_sq=False):
    t, m = lhs.shape
    tt = min(t, 512)
    tmo = min(m, 1024)
    n_slab = sum(rhs_slabs)
    starts = [sum(rhs_slabs[:n]) for n in range(len(rhs_slabs))]
    n_rhs = len(rhs_list)

    def body(*refs):
        l_ref, r_refs, o_ref, acc = refs[0], refs[1:1 + n_rhs], refs[1 + n_rhs], refs[2 + n_rhs]
        k, s = pl.program_id(1), pl.program_id(2)
        lv = l_ref[...]
        if relu_sq:
            lv = jnp.square(jnp.maximum(lv.astype(F32), 0.0)).astype(BF16)

        @pl.when(s == 0)
        def _():
            acc[...] = jnp.zeros_like(acc)

        for n in range(n_rhs):
            @pl.when((k >= starts[n]) & (k < starts[n] + rhs_slabs[n]))
            def _(n=n):
                acc[...] += _dot_tn(lv, r_refs[n][...])

        @pl.when(s == pl.num_programs(2) - 1)
        def _():
            o_ref[...] = acc[...].astype(BF16)

    def rhs_spec(n):
        lo, cnt = starts[n], rhs_slabs[n]

        def index(i, k, s):
            inside = (k >= lo) & (k < lo + cnt)
            return (jnp.where(inside, s, 0), jnp.clip(k - lo, 0, cnt - 1))
        return pl.BlockSpec((tt, D_MODEL), index)

    return pl.pallas_call(
        body, name=name, grid=(m // tmo, n_slab, t // tt),
        in_specs=[pl.BlockSpec((tt, tmo), lambda i, k, s: (s, i))] + [rhs_spec(n) for n in range(n_rhs)],
        out_specs=pl.BlockSpec((tmo, D_MODEL), lambda i, k, s: (i, k)),
        out_shape=jax.ShapeDtypeStruct((m, n_slab * D_MODEL), BF16),
        scratch_shapes=[pltpu.VMEM((tmo, D_MODEL), F32)],
        compiler_params=_params(("parallel", "parallel", "arbitrary")),
    )(lhs, *rhs_list)


def _mix_bwd(dx1b, gates, ya, yc, w_out, w_ap, w_cp, w_g, rider=None):
    t = dx1b.shape[0]
    tm = min(t, 512)

    def body(dx_ref, g_ref, ya_ref, yc_ref, wout, wap, wcp, wg,
             dgp_ref, dya_ref, dyc_ref, dyat_ref, dycv_ref, dhg_ref, dbg_ref):
        dm = _dot_nt(dx_ref[...], wout[...])
        ga = g_ref[:, :D_MODEL].astype(F32)
        gc = g_ref[:, D_MODEL:].astype(F32)
        dya = (dm * ga).astype(BF16)
        dyc = (dm * gc).astype(BF16)
        dya_ref[...] = dya
        dyc_ref[...] = dyc
        dgpa = dm * ya_ref[...].astype(F32) * ga * (1.0 - ga)
        dgpc = dm * yc_ref[...].astype(F32) * gc * (1.0 - gc)

        @pl.when(pl.program_id(0) == 0)
        def _():
            dbg_ref[...] = jnp.zeros_like(dbg_ref)

        dbg_ref[:, :D_MODEL] += _fold8(dgpa)
        dbg_ref[:, D_MODEL:] += _fold8(dgpc)
        dgpa = dgpa.astype(BF16)
        dgpc = dgpc.astype(BF16)
        dgp_ref[:, :D_MODEL] = dgpa
        dgp_ref[:, D_MODEL:] = dgpc
        dyat_ref[...] = _dot_nt(dya, wap[...]).astype(BF16)
        dycv_ref[...] = _dot_nt(dyc, wcp[...]).astype(BF16)
        dhg_ref[...] = _dot_nt(dgpa, wg[:, :D_MODEL]) + _dot_nt(dgpc, wg[:, D_MODEL:])

    row = pl.BlockSpec((tm, D_MODEL), lambda i: (i, 0))
    row2 = pl.BlockSpec((tm, 2 * D_MODEL), lambda i: (i, 0))
    full = _resident((D_MODEL, D_MODEL))
    return _call(
        body, name="mix_bwd", grid=(t // tm,), args=(dx1b, gates, ya, yc, w_out, w_ap, w_cp, w_g),
        in_specs=[row, row2, row, row, full, full, full, _resident((D_MODEL, 2 * D_MODEL))],
        out_specs=[row2, row, row, row, row, row, pl.BlockSpec((SUBLANES, 2 * D_MODEL), lambda i: (0, 0))],
        out_shape=[jax.ShapeDtypeStruct((t, 2 * D_MODEL), BF16)] + [jax.ShapeDtypeStruct((t, D_MODEL), BF16)] * 4
        + [jax.ShapeDtypeStruct((t, D_MODEL), F32), jax.ShapeDtypeStruct((SUBLANES, 2 * D_MODEL), F32)],
        semantics=("arbitrary",), rider=rider)


def _conv_bwd(dyconv, proj, conv_w, conv_b, rider=None):
    t = proj.shape[0]
    tm = min(t, 512)
    hb = tm // SUBLANES
    last = t // SUBLANES - 1

    def body(dy_ref, dyn_ref, bg_ref, bgn_ref, cg_ref, cgp_ref, xc_ref, xcp_ref, w_ref, b_ref,
             o_ref, dcb_ref, dcw_ref):
        i = pl.program_id(0)
        cg = cg_ref[...].astype(F32)
        xc = xc_ref[...].astype(F32)
        bg = bg_ref[...].astype(F32)
        u = cg * xc
        prev = jnp.where(i > 0, cgp_ref[...].astype(F32) * xcp_ref[...].astype(F32), 0.0)
        u1 = _shift_down(u, prev, 1)
        u2 = _shift_down(u, prev, 2)
        w = w_ref[...]
        conv = b_ref[...] + (w[0:1] * u2 + w[1:2] * u1 + w[2:3] * u)
        dy = dy_ref[...].astype(F32)
        dconv = dy * bg
        nxt = jnp.where(i < pl.num_programs(0) - 1, dyn_ref[...].astype(F32) * bgn_ref[...].astype(F32), 0.0)
        du = w[2:3] * dconv + w[1:2] * _shift_up(dconv, nxt, 1) + w[0:1] * _shift_up(dconv, nxt, 2)
        o_ref[:, :D_MODEL] = (dy * conv).astype(BF16)
        o_ref[:, D_MODEL:2 * D_MODEL] = (du * xc).astype(BF16)
        o_ref[:, 2 * D_MODEL:] = (du * cg).astype(BF16)

        @pl.when(i == 0)
        def _():
            dcb_ref[...] = jnp.zeros_like(dcb_ref)
            dcw_ref[...] = jnp.zeros_like(dcw_ref)

        dcb_ref[...] += _fold8(dconv)
        dcw_ref[0:SUBLANES] += _fold8(dconv * u2)
        dcw_ref[SUBLANES:2 * SUBLANES] += _fold8(dconv * u1)
        dcw_ref[2 * SUBLANES:] += _fold8(dconv * u)

    def prev(col):
        return pl.BlockSpec((SUBLANES, D_MODEL), lambda i: (jnp.maximum(i * hb - 1, 0), col))

    def nxt(col):
        return pl.BlockSpec((SUBLANES, D_MODEL), lambda i: (jnp.minimum((i + 1) * hb, last), col))

    def cur(col):
        return pl.BlockSpec((tm, D_MODEL), lambda i: (i, col))

    return _call(
        body, name="conv_bwd", grid=(t // tm,),
        args=(dyconv, dyconv, proj, proj, proj, proj, proj, proj, conv_w, conv_b),
        in_specs=[cur(0), nxt(0), cur(3), nxt(3), cur(4), prev(4), cur(5), prev(5),
                  pl.BlockSpec((3, D_MODEL), lambda i: (0, 0)), pl.BlockSpec((1, D_MODEL), lambda i: (0, 0))],
        out_specs=[pl.BlockSpec((tm, 3 * D_MODEL), lambda i: (i, 0)),
                   pl.BlockSpec((SUBLANES, D_MODEL), lambda i: (0, 0)),
                   pl.BlockSpec((3 * SUBLANES, D_MODEL), lambda i: (0, 0))],
        out_shape=[jax.ShapeDtypeStruct((t, 3 * D_MODEL), BF16), jax.ShapeDtypeStruct((SUBLANES, D_MODEL), F32),
                   jax.ShapeDtypeStruct((3 * SUBLANES, D_MODEL), F32)],
        semantics=("arbitrary",), rider=rider)


def _attn_bwd(qn, kn, proj, dyattn, bias, rider=None):
    t = qn.shape[0]
    nb = t // QB
    v_col0 = 2 * D_MODEL // SLAB

    def body(q_ref, k0, k1, k2, v0, v1, v2, do_ref, bias_ref, dq_ref, dk_ref, dv_ref, db_ref, acck, accv):
        b = pl.program_id(1)

        @pl.when(b == 0)
        def _():
            acck[...] = jnp.zeros_like(acck)
            accv[...] = jnp.zeros_like(accv)
            db_ref[...] = jnp.zeros_like(db_ref)

        @pl.when(b < nb)
        def _():
            head_a = lax.broadcasted_iota(jnp.int32, (1, LANES), 1) < HEAD_DIM
            rows_a = lax.broadcasted_iota(jnp.int32, (LANES, 1), 0) < HEAD_DIM
            valid = lax.broadcasted_iota(jnp.int32, (KW, 1), 0) >= (2 - b) * QB
            for hp in range(PAIRS):
                sl = slice(hp * LANES, (hp + 1) * LANES)
                q = q_ref[:, sl]
                do = do_ref[:, sl]
                k = jnp.concatenate([k0[:, sl], k1[:, sl], k2[:, sl]], axis=0)
                v = jnp.concatenate([v0[:, sl], v1[:, sl], v2[:, sl]], axis=0)
                qt = q.astype(F32).T.astype(BF16)
                dot = do.astype(F32).T.astype(BF16)
                kt = k.astype(F32).T.astype(BF16)
                dqt = jnp.zeros((LANES, QB), F32)
                dkt = jnp.zeros((LANES, KW), F32)
                dvt = jnp.zeros((LANES, KW), F32)
                for hh in range(2):
                    mine = head_a if hh == 0 else jnp.logical_not(head_a)
                    rows = rows_a if hh == 0 else jnp.logical_not(rows_a)
                    kh = jnp.where(mine, k, jnp.zeros_like(k))
                    vh = jnp.where(mine, v, jnp.zeros_like(v))
                    s = _dot_nt(kh, q) + bias_ref[2 * hp + hh]
                    s = jnp.where(valid, s, NEG_INF)
                    p = jnp.exp(s - jnp.max(s, axis=0, keepdims=True))
                    p = p / jnp.sum(p, axis=0, keepdims=True)
                    dp = _dot_nt(vh, do)
                    ds = p * (dp - jnp.sum(p * dp, axis=0, keepdims=True))
                    db_ref[2 * hp + hh] += ds
                    pb = p.astype(BF16)
                    dsb = ds.astype(BF16)
                    dvt = dvt + _dot_nt(jnp.where(rows, dot, jnp.zeros_like(dot)), pb)
                    dkt = dkt + _dot_nt(jnp.where(rows, qt, jnp.zeros_like(qt)), dsb)
                    dqt = dqt + _dot(jnp.where(rows, kt, jnp.zeros_like(kt)), dsb)
                dq_ref[:, sl] = dqt.T
                for w in range(3):
                    slot = lax.rem(b + w + 1, 3)
                    cols = slice(w * QB, (w + 1) * QB)
                    if w == 2:
                        acck[hp, slot] = dkt[:, cols]
                        accv[hp, slot] = dvt[:, cols]
                    else:
                        acck[hp, slot] += dkt[:, cols]
                        accv[hp, slot] += dvt[:, cols]

        done = lax.rem(b + 1, 3)
        for hp in range(PAIRS):
            sl = slice(hp * LANES, (hp + 1) * LANES)
            dk_ref[:, sl] = acck[hp, done].T
            dv_ref[:, sl] = accv[hp, done].T.astype(BF16)

    def cur(p, b):
        return (jnp.minimum(b, nb - 1), p)

    def window(col0):
        return [pl.BlockSpec((QB, SLAB), functools.partial(
            lambda p, b, back: (jnp.maximum(jnp.minimum(b, nb - 1) - back, 0), col0 + p), back=back))
            for back in (2, 1, 0)]

    def done_block(p, b):
        return (jnp.maximum(b - 2, 0), p)

    tile = pl.BlockSpec((2 * PAIRS, KW, QB), lambda p, b: (p, 0, 0))
    return _call(
        body, name="attn_bwd", grid=(D_MODEL // SLAB, nb + 2),
        args=(qn, kn, kn, kn, proj, proj, proj, dyattn, bias),
        in_specs=[pl.BlockSpec((QB, SLAB), cur)] + window(0) + window(v_col0) + [pl.BlockSpec((QB, SLAB), cur), tile],
        out_specs=[pl.BlockSpec((QB, SLAB), cur), pl.BlockSpec((QB, SLAB), done_block),
                   pl.BlockSpec((QB, SLAB), done_block), tile],
        out_shape=[jax.ShapeDtypeStruct((t, D_MODEL), F32), jax.ShapeDtypeStruct((t, D_MODEL), F32),
                   jax.ShapeDtypeStruct((t, D_MODEL), BF16), jax.ShapeDtypeStruct((N_HEADS, KW, QB), F32)],
        scratch_shapes=[pltpu.VMEM((PAIRS, 3, LANES, QB), F32), pltpu.VMEM((PAIRS, 3, LANES, QB), F32)],
        semantics=("parallel", "arbitrary"), rider=rider)


def _qknorm_bwd(proj, dqn, dkn, gq, gk):
    t = proj.shape[0]
    tm = min(t, 512)
    scale = HEAD_DIM ** -0.5

    def body(q_ref, k_ref, dqn_ref, dkn_ref, gq_ref, gk_ref, o_ref, dgq_ref, dgk_ref):
        e = _head_sum_matrix()

        @pl.when(pl.program_id(0) == 0)
        def _():
            dgq_ref[...] = jnp.zeros_like(dgq_ref)
            dgk_ref[...] = jnp.zeros_like(dgk_ref)

        for n, (src, dn_ref, g_ref, dg_ref, sc) in enumerate(
                ((q_ref, dqn_ref, gq_ref, dgq_ref, scale), (k_ref, dkn_ref, gk_ref, dgk_ref, 1.0))):
            for s in range(D_MODEL // LANES):
                sl = slice(s * LANES, (s + 1) * LANES)
                xf = src[:, sl].astype(F32)
                r = lax.rsqrt(_head_sums(xf * xf, e) * (1.0 / HEAD_DIM) + EPS)
                xh = xf * r
                dn = dn_ref[:, sl] * sc
                dg_ref[:, sl] += _fold8(dn * xh)
                dxh = dn * g_ref[:, sl]
                mean = _head_sums(dxh * xh, e) * (1.0 / HEAD_DIM)
                o_ref[:, n * D_MODEL + s * LANES:n * D_MODEL + (s + 1) * LANES] = (r * (dxh - xh * mean)).astype(BF16)

    row = pl.BlockSpec((tm, D_MODEL), lambda i: (i, 0))
    vec = pl.BlockSpec((1, D_MODEL), lambda i: (0, 0))
    acc = pl.BlockSpec((SUBLANES, D_MODEL), lambda i: (0, 0))
    return pl.pallas_call(
        body, name="qknorm_bwd", grid=(t // tm,),
        in_specs=[row, pl.BlockSpec((tm, D_MODEL), lambda i: (i, 1)), row, row, vec, vec],
        out_specs=[pl.BlockSpec((tm, 2 * D_MODEL), lambda i: (i, 0)), acc, acc],
        out_shape=[jax.ShapeDtypeStruct((t, 2 * D_MODEL), BF16)] + [jax.ShapeDtypeStruct((SUBLANES, D_MODEL), F32)] * 2,
        compiler_params=_params(("arbitrary",)),
    )(proj, proj, dqn, dkn, gq, gk)


def _in_bwd(dqk, dv, dconv, w_in, dhg, x, g1, dx1, rider=None):
    t = x.shape[0]
    tm = min(t, 512)

    def body(dqk_ref, dv_ref, dc_ref, w_ref, dhg_ref, x_ref, g_ref, dx1_ref, dx_ref, dg_ref):
        acc = dhg_ref[...]
        slab = 0
        for src, n in ((dqk_ref, 2), (dv_ref, 1), (dc_ref, 3)):
            for s in range(n):
                acc = acc + _dot_nt(src[:, s * D_MODEL:(s + 1) * D_MODEL],
                                    w_ref[:, slab * D_MODEL:(slab + 1) * D_MODEL])
                slab += 1

        @pl.when(pl.program_id(0) == 0)
        def _():
            dg_ref[...] = jnp.zeros_like(dg_ref)

        dx, dg = _rmsnorm_bwd(x_ref[...], g_ref[...], acc)
        dx_ref[...] = dx1_ref[...] + dx
        dg_ref[...] += _fold8(dg)

    row = pl.BlockSpec((tm, D_MODEL), lambda i: (i, 0))
    return _call(
        body, name="in_bwd", grid=(t // tm,), args=(dqk, dv, dconv, w_in, dhg, x, g1, dx1),
        in_specs=[pl.BlockSpec((tm, 2 * D_MODEL), lambda i: (i, 0)), row,
                  pl.BlockSpec((tm, 3 * D_MODEL), lambda i: (i, 0)),
                  _resident(w_in.shape), row, row, pl.BlockSpec((1, D_MODEL), lambda i: (0, 0)), row],
        out_specs=[row, pl.BlockSpec((SUBLANES, D_MODEL), lambda i: (0, 0))],
        out_shape=[jax.ShapeDtypeStruct((t, D_MODEL), F32), jax.ShapeDtypeStruct((SUBLANES, D_MODEL), F32)],
        semantics=("arbitrary",), rider=rider)


def _bias_grad_fold(dbias):
    def body(d_ref, o_ref):
        jj = lax.broadcasted_iota(jnp.int32, (QB, QB), 0)
        ii = lax.broadcasted_iota(jnp.int32, (QB, QB), 1)
        flip = (jj + ii == QB - 1).astype(BF16)
        low = jj + ii < QB
        pos, neg = [], []
        for w in range(KW // QB):
            x = d_ref[0, QB * w:QB * (w + 1), :]
            hi = x.astype(BF16)
            r1 = x - hi.astype(F32)
            mid = r1.astype(BF16)
            lo = (r1 - mid.astype(F32)).astype(BF16)
            xr = _dot(hi, flip) + _dot(mid, flip) + _dot(lo, flip)
            for keep, acc in ((low, pos), (jnp.logical_not(low), neg)):
                part = pltpu.roll(jnp.where(keep, xr, 0.0), 0, 1, stride=1, stride_axis=0)
                acc.append(jnp.sum(part, axis=0, keepdims=True))
        far = pos[1] + neg[0] + pos[0]
        o_ref[0] = jnp.zeros((SUBLANES, QB), F32)
        o_ref[0, 0:1, :] = neg[2]
        o_ref[0, 1:2, :] = pos[2] + neg[1]
        o_ref[0, 2:3, :] = jnp.broadcast_to(jnp.sum(far, axis=-1, keepdims=True), (1, QB))

    return pl.pallas_call(
        body, name="bias_grad_fold", grid=(N_HEADS,),
        in_specs=[pl.BlockSpec((1, KW, QB), lambda h: (h, 0, 0))],
        out_specs=pl.BlockSpec((1, SUBLANES, QB), lambda h: (h, 0, 0)),
        out_shape=jax.ShapeDtypeStruct((N_HEADS, SUBLANES, QB), F32),
        compiler_params=_params(("parallel",)),
    )(dbias)


def _small_partials(dg1, dgq, dgk, dcb, dcw, dbg, dg2, dbias_fold, loss_tile):
    def head_fold(v):
        acc = v[:, 0:LANES]
        for s in range(1, D_MODEL // LANES):
            acc = acc + v[:, s * LANES:(s + 1) * LANES]
        return acc + pltpu.roll(acc, HEAD_DIM, 1)

    def body(dg1_ref, dgq_ref, dgk_ref, dcb_ref, dcw_ref, dbg_ref, dg2_ref, db_ref, loss_ref, o_ref):
        o_ref[...] = jnp.zeros_like(o_ref)
        o_ref[0:1, :] = jnp.sum(dg1_ref[...], axis=0, keepdims=True)
        o_ref[1:2, 0:LANES] = head_fold(jnp.sum(dgq_ref[...], axis=0, keepdims=True))
        o_ref[2:3, 0:LANES] = head_fold(jnp.sum(dgk_ref[...], axis=0, keepdims=True))
        o_ref[3:4, :] = jnp.sum(dcb_ref[...], axis=0, keepdims=True)
        for j in range(3):
            o_ref[4 + j:5 + j, :] = jnp.sum(dcw_ref[j * SUBLANES:(j + 1) * SUBLANES, :], axis=0, keepdims=True)
        o_ref[7:8, :] = jnp.sum(dbg_ref[:, :D_MODEL], axis=0, keepdims=True)
        o_ref[8:9, :] = jnp.sum(dbg_ref[:, D_MODEL:], axis=0, keepdims=True)
        o_ref[9:10, :] = jnp.sum(dg2_ref[...], axis=0, keepdims=True)
        for h in range(N_HEADS):
            for part in range(3):
                o_ref[10 + h:11 + h, part * QB:(part + 1) * QB] = db_ref[h, part:part + 1, :]
        loss = (0.5 / D_MODEL) * jnp.sum(jnp.sum(loss_ref[...], axis=0, keepdims=True), axis=-1, keepdims=True)
        o_ref[26:27, :] = jnp.broadcast_to(loss, (1, D_MODEL))

    return pl.pallas_call(
        body, name="small_partials",
        out_shape=jax.ShapeDtypeStruct((32, D_MODEL), F32),
        compiler_params=_params(),
    )(dg1, dgq, dgk, dcb, dcw, dbg, dg2, dbias_fold, loss_tile)


MID_AXES = (0, 0, 1, 0)
MLP_AXES = (1, 0)


def _local_step(x, target, norm1_g, q_norm_g, k_norm_g, rel_bias, conv_w, conv_b, b_gate, norm2_g,
                w_in, mid_w, mlp_w, distributed):
    g1 = norm1_g.reshape(1, D_MODEL)
    g2 = norm2_g.reshape(1, D_MODEL)
    gq = jnp.tile(q_norm_g, N_HEADS).reshape(1, D_MODEL)
    gk = jnp.tile(k_norm_g, N_HEADS).reshape(1, D_MODEL)
    cb = conv_b.reshape(1, D_MODEL)
    bias = _bias_tiles(rel_bias)

    (proj, h), got = _in_proj(x, g1, w_in, rider=_Gather(mid_w, MID_AXES) if distributed else None)
    w_ap, w_cp, w_g, w_out = got if distributed else mid_w
    gates = _gate_proj(h, w_g, b_gate.reshape(1, 2 * D_MODEL))
    qn, kn = _qknorm_fwd(proj, gq, gk)
    (y_attn,), got = _attn_fwd(qn, kn, proj, bias, rider=_Gather(mlp_w, MLP_AXES) if distributed else None)
    w_up, w_down = got if distributed else mlp_w
    y_conv = _conv_fwd(proj, conv_w, cb)
    ya, yc, merged, x1, h2 = _mix_out(y_attn, y_conv, gates, x, w_ap, w_cp, w_out, g2)
    a, dy, dyb, loss_tile = _mlp_fwd(h2, w_up, w_down, x1, target)

    da, dx1, dx1b, dg2 = _mlp_bwd(dyb, a, w_down, w_up, x1, dy, g2)
    gw_down = _wgrad("wgrad_down", a, [dyb], [1], relu_sq=True)
    gw_up = _wgrad("wgrad_up", h2, [da], [D_FF // D_MODEL])
    (dgp, dya, dyc, dyattn, dyconv, dhg, dbg), mlp_swapped = _mix_bwd(
        dx1b, gates, ya, yc, w_out, w_ap, w_cp, w_g,
        rider=_PairSwap((gw_up, gw_down), MLP_AXES) if distributed else None)
    gw_out = _wgrad("wgrad_out", merged, [dx1b], [1])
    gw_ap = _wgrad("wgrad_attn_proj", y_attn, [dya], [1])
    gw_cp = _wgrad("wgrad_conv_proj", y_conv, [dyc], [1])
    gw_g = _wgrad("wgrad_gate", h, [dgp], [2])
    mid = (gw_ap, gw_cp, gw_g, gw_out)
    (dconv, dcb, dcw), mid_swapped = _conv_bwd(
        dyconv, proj, conv_w, cb, rider=_PairSwap(mid, MID_AXES) if distributed else None)
    early = mid + (gw_up, gw_down)
    early_sums = (_pair_add(early, tuple(mid_swapped) + tuple(mlp_swapped), MID_AXES + MLP_AXES)
                  if distributed else None)
    (dqn, dkn, dv, dbias), early_shares = _attn_bwd(
        qn, kn, proj, dyattn, bias, rider=_ChipScatter(early_sums) if distributed else None)
    dqk, dgq, dgk = _qknorm_bwd(proj, dqn, dkn, gq, gk)
    gw_in = _wgrad("wgrad_in", h, [dqk, dv, dconv], [2, 1, 3])
    in_sums = (_pair_add((gw_in,), _exchange("swap_w_in_grad", _PairSwap((gw_in,), (1,))), (1,))
               if distributed else None)
    (dx, dg1), in_shares = _in_bwd(dqk, dv, dconv, w_in, dhg, x, g1, dx1,
                                   rider=_ChipScatter(in_sums) if distributed else None)
    small = _small_partials(dg1, dgq, dgk, dcb, dcw, dbg, dg2, _bias_grad_fold(dbias), loss_tile)
    grads = tuple(in_shares) + tuple(early_shares) if distributed else (gw_in,) + early
    return dx, grads, small


def _me():
    return lax.axis_index("x"), lax.axis_index("y"), lax.axis_index("c")


def _peer(me, rel):
    x, y, c = me
    return (1 - x if rel & 4 else x, 1 - y if rel & 2 else y, 1 - c if rel & 1 else c)


def _linear(dev):
    return 4 * dev[0] + 2 * dev[1] + dev[2]


BIG_AXES = (1, 0, 0, 1, 0, 1, 0)


def _block(ref, axis, idx, size):
    return ref.at[pl.ds(idx * size, size), :] if axis == 0 else ref.at[:, pl.ds(idx * size, size)]


def _cast_shards(shards):
    def body(*refs):
        for src, dst in zip(refs[:len(shards)], refs[len(shards):]):
            dst[...] = src[...].astype(BF16)

    return pl.pallas_call(
        body, name="cast_shards",
        out_shape=[jax.ShapeDtypeStruct(s.shape, BF16) for s in shards],
        compiler_params=_params(),
    )(*shards)


class _Gather:
    def __init__(self, shards, axes):
        self.arrays, self.axes, self.n = list(shards), tuple(axes), len(shards)
        self.sizes = [s.shape[axis] for s, axis in zip(shards, axes)]
        self.out_shape = []
        for s, axis in zip(shards, axes):
            shape = (s.shape[0] * N_DEV, s.shape[1]) if axis == 0 else (s.shape[0], s.shape[1] * N_DEV)
            self.out_shape.append(jax.ShapeDtypeStruct(shape, s.dtype))
        self.scratch = [pltpu.SemaphoreType.DMA((self.n, 7)), pltpu.SemaphoreType.DMA((self.n, 7)),
                        pltpu.SemaphoreType.DMA((self.n,))]

    def _copies(self, srcs, outs, sems):
        send_sems, recv_sems, local_sems = sems
        me = _me()
        sibling = _peer(me, 1)
        chips = [_peer(me, rel) for rel in (4, 2, 6)]

        def rows(a, dev):
            return _block(outs[a], self.axes[a], _linear(dev), self.sizes[a])

        def copy(a, k, block_dev, to, src=None):
            return pltpu.make_async_remote_copy(
                src_ref=rows(a, block_dev) if src is None else src, dst_ref=rows(a, block_dev),
                send_sem=send_sems.at[a, k], recv_sem=recv_sems.at[a, k], device_id=to, device_id_type=MESH_T)

        own = [pltpu.make_async_copy(srcs[a], rows(a, me), local_sems.at[a]) for a in range(self.n)]
        first = []
        for a in range(self.n):
            first.append(copy(a, 0, me, sibling, src=srcs[a]))
            for j, chip in enumerate(chips):
                first.append(copy(a, 1 + j, me, chip, src=srcs[a]))
        return me, sibling, chips, copy, own, first

    def start(self, srcs, outs, sems):
        _, _, _, _, own, first = self._copies(srcs, outs, sems)
        for cp in own + first:
            cp.start()

    def finish(self, srcs, outs, sems):
        me, sibling, chips, copy, own, first = self._copies(srcs, outs, sems)
        passed = []
        for a in range(self.n):
            for j, chip in enumerate(chips):
                copy(a, 1 + j, chip, me).wait_recv()
                fwd = copy(a, 4 + j, chip, sibling)
                fwd.start()
                passed.append(fwd)
        for a in range(self.n):
            copy(a, 0, sibling, me).wait_recv()
            for j, chip in enumerate(chips):
                copy(a, 4 + j, _peer(chip, 1), me).wait_recv()
        for cp in first + passed:
            cp.wait_send()
        for cp in own:
            cp.wait()


N_CHIPS = 4


def _shard_shape(g, axis):
    return (g.shape[0] // N_DEV, g.shape[1]) if axis == 0 else (g.shape[0], g.shape[1] // N_DEV)


class _PairSwap:
    def __init__(self, grads, axes):
        self.arrays, self.axes, self.n = list(grads), tuple(axes), len(grads)
        self.sizes = [g.shape[axis] // N_DEV for g, axis in zip(grads, axes)]
        self.out_shape = [jax.ShapeDtypeStruct((N_CHIPS,) + _shard_shape(g, axis), g.dtype)
                          for g, axis in zip(grads, axes)]
        self.scratch = [pltpu.SemaphoreType.DMA((self.n, N_CHIPS)), pltpu.SemaphoreType.DMA((self.n, N_CHIPS))]

    def _copies(self, srcs, outs, sems):
        send_sems, recv_sems = sems
        x, y, c = _me()
        sibling = (x, y, 1 - c)
        copies = []
        for a in range(self.n):
            for chip in range(N_CHIPS):
                owner_idx = 2 * chip + (1 - c)
                copies.append(pltpu.make_async_remote_copy(
                    src_ref=_block(srcs[a], self.axes[a], owner_idx, self.sizes[a]), dst_ref=outs[a].at[chip],
                    send_sem=send_sems.at[a, chip], recv_sem=recv_sems.at[a, chip],
                    device_id=sibling, device_id_type=MESH_T))
        return copies

    def start(self, srcs, outs, sems):
        for cp in self._copies(srcs, outs, sems):
            cp.start()

    def finish(self, srcs, outs, sems):
        for cp in self._copies(srcs, outs, sems):
            cp.wait()


def _pair_add(grads, swapped, axes):
    n = len(grads)
    c_arr = lax.axis_index("c").astype(jnp.int32).reshape(1)

    def body(c_ref, *refs):
        del c_ref
        mine, got, outs = refs[:n], refs[n:2 * n], refs[2 * n:]
        for a in range(n):
            outs[a][0] = (mine[a][...].astype(F32) + got[a][0].astype(F32)).astype(BF16)

    in_specs, out_specs, out_shape = [], [], []
    for g, axis in zip(grads, axes):
        shard = _shard_shape(g, axis)
        if axis == 0:
            in_specs.append(pl.BlockSpec(shard, lambda s, c_ref: (2 * s + c_ref[0], 0)))
        else:
            in_specs.append(pl.BlockSpec(shard, lambda s, c_ref: (0, 2 * s + c_ref[0])))
    for g, axis in zip(grads, axes):
        shard = _shard_shape(g, axis)
        in_specs.append(pl.BlockSpec((1,) + shard, lambda s, c_ref: (s, 0, 0)))
        out_specs.append(pl.BlockSpec((1,) + shard, lambda s, c_ref: (s, 0, 0)))
        out_shape.append(jax.ShapeDtypeStruct((N_CHIPS,) + shard, BF16))
    return pl.pallas_call(
        body, name="pair_add_" + str(n),
        grid_spec=pltpu.PrefetchScalarGridSpec(num_scalar_prefetch=1, grid=(N_CHIPS,), in_specs=in_specs,
                                               out_specs=out_specs),
        out_shape=out_shape, compiler_params=_params(("arbitrary",)),
    )(c_arr, *grads, *swapped)


class _ChipScatter:
    def __init__(self, sums):
        self.arrays, self.n = list(sums), len(sums)
        self.out_shape = [jax.ShapeDtypeStruct(s.shape, s.dtype) for s in sums]
        self.scratch = [pltpu.SemaphoreType.DMA((self.n, 3)), pltpu.SemaphoreType.DMA((self.n, 3)),
                        pltpu.SemaphoreType.DMA((self.n,))]

    def _copies(self, srcs, outs, sems):
        send_sems, recv_sems, local_sems = sems
        me = _me()
        my_chip = 2 * me[0] + me[1]
        own = [pltpu.make_async_copy(srcs[a].at[my_chip], outs[a].at[my_chip], local_sems.at[a])
               for a in range(self.n)]
        sends, recvs = [], []
        for a in range(self.n):
            for k, rel in enumerate((4, 2, 6)):
                peer = _peer(me, rel)
                peer_chip = 2 * peer[0] + peer[1]
                sends.append(pltpu.make_async_remote_copy(
                    src_ref=srcs[a].at[peer_chip], dst_ref=outs[a].at[my_chip],
                    send_sem=send_sems.at[a, k], recv_sem=recv_sems.at[a, k], device_id=peer, device_id_type=MESH_T))
                recvs.append(pltpu.make_async_remote_copy(
                    src_ref=srcs[a].at[my_chip], dst_ref=outs[a].at[peer_chip],
                    send_sem=send_sems.at[a, k], recv_sem=recv_sems.at[a, k], device_id=peer, device_id_type=MESH_T))
        return own, sends, recvs

    def start(self, srcs, outs, sems):
        own, sends, _ = self._copies(srcs, outs, sems)
        for cp in own + sends:
            cp.start()

    def finish(self, srcs, outs, sems):
        own, sends, recvs = self._copies(srcs, outs, sems)
        for cp in recvs:
            cp.wait_recv()
        for cp in sends:
            cp.wait_send()
        for cp in own:
            cp.wait()


def _call(body, *, name, args, in_specs, out_specs, out_shape, grid=(), scratch_shapes=(), semantics=None,
          rider=None):
    if rider is None:
        return pl.pallas_call(
            body, name=name, grid=grid, in_specs=in_specs, out_specs=out_specs, out_shape=out_shape,
            scratch_shapes=list(scratch_shapes), compiler_params=_params(semantics))(*args), None
    n_in, n_out, n_scr, r = len(in_specs), len(out_specs), len(scratch_shapes), rider.n

    def wrapped(*refs):
        ins, r_ins = refs[:n_in], refs[n_in:n_in + r]
        outs = refs[n_in + r:n_in + r + n_out]
        r_outs = refs[n_in + r + n_out:n_in + 2 * r + n_out]
        scr = refs[n_in + 2 * r + n_out:n_in + 2 * r + n_out + n_scr]
        sems = refs[n_in + 2 * r + n_out + n_scr:]
        first, last = None, None
        for ax in range(len(grid)):
            f, l = pl.program_id(ax) == 0, pl.program_id(ax) == pl.num_programs(ax) - 1
            first = f if first is None else first & f
            last = l if last is None else last & l
        if first is None:
            rider.start(r_ins, r_outs, sems)
            body(*ins, *outs, *scr)
            rider.finish(r_ins, r_outs, sems)
            return

        @pl.when(first)
        def _():
            rider.start(r_ins, r_outs, sems)

        body(*ins, *outs, *scr)

        @pl.when(last)
        def _():
            rider.finish(r_ins, r_outs, sems)

    any_spec = pl.BlockSpec(memory_space=pl.ANY)
    out = pl.pallas_call(
        wrapped, name=name, grid=grid, in_specs=list(in_specs) + [any_spec] * r,
        out_specs=list(out_specs) + [any_spec] * r, out_shape=list(out_shape) + rider.out_shape,
        scratch_shapes=list(scratch_shapes) + rider.scratch,
        compiler_params=_params(None if semantics is None else ("arbitrary",) * len(semantics)),
    )(*args, *rider.arrays)
    return out[:n_out], out[n_out:]


def _exchange(name, rider):
    def body():
        pass

    return _call(body, name=name, args=(), in_specs=[], out_specs=[], out_shape=[], rider=rider)[1]


def _all_reduce_small(part):
    def body(p_ref, o_ref, slots, send_sems, recv_sems):
        me = _me()
        my_idx = _linear(me)
        slots[my_idx] = p_ref[...]
        sends = []
        for rel in range(1, N_DEV):
            cp = pltpu.make_async_remote_copy(
                src_ref=p_ref, dst_ref=slots.at[my_idx], send_sem=send_sems.at[rel - 1],
                recv_sem=recv_sems.at[rel - 1], device_id=_peer(me, rel), device_id_type=MESH_T)
            cp.start()
            sends.append(cp)
        for rel in range(1, N_DEV):
            frm = _peer(me, rel)
            pltpu.make_async_remote_copy(
                src_ref=p_ref, dst_ref=slots.at[_linear(frm)], send_sem=send_sems.at[rel - 1],
                recv_sem=recv_sems.at[rel - 1], device_id=frm, device_id_type=MESH_T).wait_recv()
        for cp in sends:
            cp.wait_send()
        total = slots[0]
        for d in range(1, N_DEV):
            total = total + slots[d]
        o_ref[...] = total

    return pl.pallas_call(
        body, name="all_reduce_small",
        in_specs=[pl.BlockSpec(memory_space=pltpu.VMEM)], out_specs=pl.BlockSpec(memory_space=pltpu.VMEM),
        out_shape=jax.ShapeDtypeStruct(part.shape, F32),
        scratch_shapes=[pltpu.VMEM((N_DEV,) + part.shape, F32), pltpu.SemaphoreType.DMA((7,)),
                        pltpu.SemaphoreType.DMA((7,))],
        compiler_params=_params(),
    )(part)


def _adamw_math(w, g, m, v):
    m = ADAM_B1 * m + (1.0 - ADAM_B1) * g
    v = ADAM_B2 * v + (1.0 - ADAM_B2) * jnp.square(g)
    m_hat = m / (1.0 - ADAM_B1 ** ADAM_STEP)
    v_hat = v / (1.0 - ADAM_B2 ** ADAM_STEP)
    delta = -ADAM_LR * (m_hat / (jnp.sqrt(v_hat) + ADAM_EPS) + ADAM_WD * w)
    return delta, m, v


def _adamw_big(name, shares, w, m, v):
    rows, cols = w.shape
    tr = min(rows, 256)

    def body(s_ref, w_ref, m_ref, v_ref, g_ref, d_ref, nm_ref, nv_ref):
        g = s_ref[0].astype(F32)
        for d in range(1, N_CHIPS):
            g = g + s_ref[d].astype(F32)
        g_ref[...] = g
        d_ref[...], nm_ref[...], nv_ref[...] = _adamw_math(w_ref[...], g, m_ref[...], v_ref[...])

    blk = pl.BlockSpec((tr, cols), lambda i: (i, 0))
    return pl.pallas_call(
        body, name=name, grid=(rows // tr,),
        in_specs=[pl.BlockSpec((N_CHIPS, tr, cols), lambda i: (0, i, 0)), blk, blk, blk],
        out_specs=[blk] * 4, out_shape=[jax.ShapeDtypeStruct(w.shape, F32)] * 4,
        compiler_params=_params(("parallel",)),
    )(shares, w, m, v)


def _adamw_small(quads):
    n = len(quads)

    def body(*refs):
        ins, outs = refs[:4 * n], refs[4 * n:]
        for p in range(n):
            g_ref, w_ref, m_ref, v_ref = ins[4 * p:4 * p + 4]
            d_ref, nm_ref, nv_ref = outs[3 * p:3 * p + 3]
            d_ref[...], nm_ref[...], nv_ref[...] = _adamw_math(w_ref[...], g_ref[...], m_ref[...], v_ref[...])

    flat = [a for quad in quads for a in quad]
    out = pl.pallas_call(
        body, name="adamw_small",
        out_shape=[jax.ShapeDtypeStruct(quad[1].shape, F32) for quad in quads for _ in range(3)],
        compiler_params=_params(),
    )(*flat)
    return [tuple(out[3 * p:3 * p + 3]) for p in range(n)]


def kernel(x, norm1_g, w_in, q_norm_g, k_norm_g, rel_bias, conv_w, conv_b, w_attn_proj, w_conv_proj, w_gate, b_gate, w_out, norm2_g, w_up, w_down, loss_target, m_norm1_g, m_w_in, m_q_norm_g, m_k_norm_g, m_rel_bias, m_conv_w, m_conv_b, m_w_attn_proj, m_w_conv_proj, m_w_gate, m_b_gate, m_w_out, m_norm2_g, m_w_up, m_w_down, v_norm1_g, v_w_in, v_q_norm_g, v_k_norm_g, v_rel_bias, v_conv_w, v_conv_b, v_w_attn_proj, v_w_conv_proj, v_w_gate, v_b_gate, v_w_out, v_norm2_g, v_w_up, v_w_down):
    my_idx = _linear(_me())
    big_w = (w_in, w_attn_proj, w_conv_proj, w_gate, w_out, w_up, w_down)
    big_m = (m_w_in, m_w_attn_proj, m_w_conv_proj, m_w_gate, m_w_out, m_w_up, m_w_down)
    big_v = (v_w_in, v_w_attn_proj, v_w_conv_proj, v_w_gate, v_w_out, v_w_up, v_w_down)
    big_names = ("w_in", "w_attn_proj", "w_conv_proj", "w_gate", "w_out", "w_up", "w_down")

    conv_w_tile = jnp.pad(conv_w, ((0, SUBLANES - conv_w.shape[0]), (0, 0)))
    shards = _cast_shards(big_w)
    w_in_full, conv_w_rows = _exchange("all_gather_w_in", _Gather((shards[0], conv_w_tile), (1, 1)))

    dx, shares, small = _local_step(x[0], loss_target[0], norm1_g, q_norm_g, k_norm_g, rel_bias, conv_w_rows[:3],
                                    conv_b, b_gate, norm2_g, w_in_full, tuple(shards[1:5]), tuple(shards[5:7]), True)

    big_out = [_adamw_big("adamw_" + name, s, w, m, v)
               for name, s, w, m, v in zip(big_names, shares, big_w, big_m, big_v)]

    tot = _all_reduce_small(small)
    g_rel_bias = jnp.concatenate(
        [tot[10:26, :QB][:, ::-1], tot[10:26, QB:2 * QB][:, ::-1], tot[10:26, 2 * QB:2 * QB + 1]], axis=1)
    g_conv_w = lax.dynamic_slice(tot[4:7], (0, my_idx * LANES), (3, LANES))
    small_g = [tot[0:1], tot[1:2, :HEAD_DIM], tot[2:3, :HEAD_DIM], g_rel_bias, g_conv_w, tot[3:4],
               tot[7:9].reshape(1, 2 * D_MODEL), tot[9:10]]
    small_w = (norm1_g, q_norm_g, k_norm_g, rel_bias, conv_w, conv_b, b_gate, norm2_g)
    small_m = (m_norm1_g, m_q_norm_g, m_k_norm_g, m_rel_bias, m_conv_w, m_conv_b, m_b_gate, m_norm2_g)
    small_v = (v_norm1_g, v_q_norm_g, v_k_norm_g, v_rel_bias, v_conv_w, v_conv_b, v_b_gate, v_norm2_g)

    def two_d(a):
        return a.reshape(1, -1) if a.ndim == 1 else a

    small_out = _adamw_small([(g, two_d(w), two_d(m), two_d(v))
                              for g, w, m, v in zip(small_g, small_w, small_m, small_v)])

    order = ("norm1_g", "w_in", "q_norm_g", "k_norm_g", "rel_bias", "conv_w", "conv_b", "w_attn_proj", "w_conv_proj",
             "w_gate", "b_gate", "w_out", "norm2_g", "w_up", "w_down")
    small_names = ("norm1_g", "q_norm_g", "k_norm_g", "rel_bias", "conv_w", "conv_b", "b_gate", "norm2_g")
    res = {}
    for name, (g, d, nm, nv) in zip(big_names, big_out):
        res[name] = (g, d, nm, nv)
    for name, g, w, (d, nm, nv) in zip(small_names, small_g, small_w, small_out):
        res[name] = tuple(a.reshape(w.shape) for a in (g, d, nm, nv))
    loss = tot[26, 0]
    return (loss, dx[None], *[res[n][0] for n in order], *[res[n][1] for n in order],
            *[res[n][2] for n in order], *[res[n][3] for n in order])
```

```python
import functools

import jax
import jax.numpy as jnp
from jax import lax
from jax.experimental import pallas as pl
from jax.experimental.pallas import tpu as pltpu

F32 = jnp.float32
BF16 = jnp.bfloat16

D_MODEL = 1024
N_HEADS = 16
HEAD_DIM = 64
CHUNK = 64
N_PREV_CHUNKS = 8
MAX_REL = 256
D_FF = 4096
EPS = 1e-6
NEG_INF = -1e30
N_DEV = 8

ADAM_LR = 0.001
ADAM_B1 = 0.9
ADAM_B2 = 0.999
ADAM_EPS = 1e-08
ADAM_WD = 0.01
ADAM_STEP = 10

LANES = 128
SUBLANES = 8
VMEM_LIMIT = 56 * 1024 * 1024
QB = 256
KW = 3 * QB
PAIRS = 4
SUB = 128
SLAB = PAIRS * LANES
SKEW = 1024

MESH_T = pl.DeviceIdType.MESH


def _dot(a, b):
    return jnp.dot(a, b, preferred_element_type=F32)


def _dot_nt(a, b):
    return lax.dot_general(a, b, (((1,), (1,)), ((), ())), preferred_element_type=F32)


def _dot_tn(a, b):
    return lax.dot_general(a, b, (((0,), (0,)), ((), ())), preferred_element_type=F32)


def _params(sem=None):
    return pltpu.CompilerParams(dimension_semantics=sem, vmem_limit_bytes=VMEM_LIMIT)


def _resident(shape):
    return pl.BlockSpec(shape, lambda *_: (0,) * len(shape), pipeline_mode=pl.Buffered(1))


def _fold8(v):
    rows, n = v.shape
    return v.reshape(rows // SUBLANES, SUBLANES, n).sum(axis=0)


def _head_sum_matrix():
    r = lax.broadcasted_iota(jnp.int32, (LANES, LANES), 0) // HEAD_DIM
    c = lax.broadcasted_iota(jnp.int32, (LANES, LANES), 1) // HEAD_DIM
    return (r == c).astype(BF16)


def _head_sums(v, e):
    hi = v.astype(BF16)
    lo = (v - hi.astype(F32)).astype(BF16)
    return _dot(hi, e) + _dot(lo, e)


def _in_proj(x, g1, w_in, rider=None):
    t = x.shape[0]
    tm = min(t, 512)
    n_out = w_in.shape[1]

    def body(x_ref, g_ref, w_ref, proj_ref, h_ref):
        xf = x_ref[...]
        r = lax.rsqrt(jnp.mean(xf * xf, axis=-1, keepdims=True) + EPS)
        h = (xf * r * g_ref[...]).astype(BF16)
        h_ref[...] = h
        for k in range(n_out // D_MODEL):
            cols = slice(k * D_MODEL, (k + 1) * D_MODEL)
            proj_ref[:, cols] = _dot(h, w_ref[:, cols]).astype(BF16)

    return _call(
        body, name="in_proj", grid=(t // tm,), args=(x, g1, w_in),
        in_specs=[pl.BlockSpec((tm, D_MODEL), lambda i: (i, 0)),
                  pl.BlockSpec((1, D_MODEL), lambda i: (0, 0)),
                  _resident((D_MODEL, n_out))],
        out_specs=[pl.BlockSpec((tm, n_out), lambda i: (i, 0)),
                   pl.BlockSpec((tm, D_MODEL), lambda i: (i, 0))],
        out_shape=[jax.ShapeDtypeStruct((t, n_out), BF16), jax.ShapeDtypeStruct((t, D_MODEL), BF16)],
        semantics=("parallel",), rider=rider)


def _gate_proj(h, w_g, b_g):
    t = h.shape[0]
    tm = min(t, 1024)

    def body(h_ref, w_ref, b_ref, o_ref):
        o_ref[...] = jax.nn.sigmoid(_dot(h_ref[...], w_ref[...]) + b_ref[...]).astype(BF16)

    return pl.pallas_call(
        body, name="gate_proj", grid=(t // tm, 2),
        in_specs=[pl.BlockSpec((tm, D_MODEL), lambda i, k: (i, 0)),
                  pl.BlockSpec((D_MODEL, D_MODEL), lambda i, k: (0, k)),
                  pl.BlockSpec((1, D_MODEL), lambda i, k: (0, k))],
        out_specs=pl.BlockSpec((tm, D_MODEL), lambda i, k: (i, k)),
        out_shape=jax.ShapeDtypeStruct((t, 2 * D_MODEL), BF16),
        compiler_params=_params(("parallel", "arbitrary")),
    )(h, w_g, b_g)


def _qknorm_fwd(proj, gq, gk):
    t = proj.shape[0]
    tm = min(t, 512)
    scale = HEAD_DIM ** -0.5

    def body(q_ref, k_ref, gq_ref, gk_ref, qn_ref, kn_ref):
        e = _head_sum_matrix()
        for src, g_ref, dst, sc in ((q_ref, gq_ref, qn_ref, scale), (k_ref, gk_ref, kn_ref, 1.0)):
            for s in range(D_MODEL // LANES):
                sl = slice(s * LANES, (s + 1) * LANES)
                xf = src[:, sl].astype(F32)
                r = lax.rsqrt(_head_sums(xf * xf, e) * (1.0 / HEAD_DIM) + EPS)
                dst[:, sl] = (xf * r * g_ref[:, sl] * sc).astype(BF16)

    return pl.pallas_call(
        body, name="qknorm_fwd", grid=(t // tm,),
        in_specs=[pl.BlockSpec((tm, D_MODEL), lambda i: (i, 0)),
                  pl.BlockSpec((tm, D_MODEL), lambda i: (i, 1)),
                  pl.BlockSpec((1, D_MODEL), lambda i: (0, 0)),
                  pl.BlockSpec((1, D_MODEL), lambda i: (0, 0))],
        out_specs=[pl.BlockSpec((tm, D_MODEL), lambda i: (i, 0))] * 2,
        out_shape=[jax.ShapeDtypeStruct((t, D_MODEL), BF16)] * 2,
        compiler_params=_params(("parallel",)),
    )(proj, proj, gq, gk)


def _bias_tiles(rel_bias):
    by_dist = jnp.concatenate(
        [rel_bias[:, :2 * MAX_REL], jnp.broadcast_to(rel_bias[:, 2 * MAX_REL:], (N_HEADS, 2 * MAX_REL))], axis=1)
    by_dist = by_dist.reshape(N_HEADS, 1, SKEW)

    def body(f_ref, o_ref):
        jj = lax.broadcasted_iota(jnp.int32, (QB, QB), 0)
        ii = lax.broadcasted_iota(jnp.int32, (QB, QB), 1)
        for w in range(KW // QB):
            pos = jnp.broadcast_to(f_ref[0, :, KW - QB * w:KW - QB * w + QB], (QB, QB))
            neg = jnp.broadcast_to(f_ref[0, :, KW - QB * (w + 1):KW - QB * w], (QB, QB))
            pos = pltpu.roll(pos, 0, 1, stride=1, stride_axis=0)
            neg = pltpu.roll(neg, 0, 1, stride=1, stride_axis=0)
            tile = jnp.where(ii >= jj, pos, neg)
            kc = (jj + QB * w) // CHUNK
            qc = ii // CHUNK
            band = (kc >= qc) & (kc <= qc + N_PREV_CHUNKS)
            o_ref[0, QB * w:QB * (w + 1), :] = jnp.where(band, tile, NEG_INF)

    return pl.pallas_call(
        body, name="bias_tiles", grid=(N_HEADS,),
        in_specs=[pl.BlockSpec((1, 1, SKEW), lambda h: (h, 0, 0))],
        out_specs=pl.BlockSpec((1, KW, QB), lambda h: (h, 0, 0)),
        out_shape=jax.ShapeDtypeStruct((N_HEADS, KW, QB), F32),
        compiler_params=_params(("parallel",)),
    )(by_dist)


def _window_specs(col0):
    return [pl.BlockSpec((QB, SLAB), functools.partial(
        lambda p, b, back: (jnp.maximum(b - back, 0), col0 + p), back=back)) for back in (2, 1, 0)]


def _attn_fwd(qn, kn, proj, bias, rider=None):
    t = qn.shape[0]
    nb = t // QB
    v_col0 = 2 * D_MODEL // SLAB

    def body(q_ref, k0, k1, k2, v0, v1, v2, bias_ref, o_ref, lse_ref):
        b = pl.program_id(1)
        head_a = lax.broadcasted_iota(jnp.int32, (1, LANES), 1) < HEAD_DIM
        rows_a = lax.broadcasted_iota(jnp.int32, (LANES, 1), 0) < HEAD_DIM
        valid = lax.broadcasted_iota(jnp.int32, (KW, 1), 0) >= (2 - b) * QB
        def scores(head):
            hp, hh = divmod(head, 2)
            sl = slice(hp * LANES, (hp + 1) * LANES)
            k = jnp.concatenate([k0[:, sl], k1[:, sl], k2[:, sl]], axis=0)
            mine = head_a if hh == 0 else jnp.logical_not(head_a)
            s = _dot_nt(jnp.where(mine, k, jnp.zeros_like(k)), q_ref[:, sl]) + bias_ref[head]
            return jnp.where(valid, s, NEG_INF)

        def weighted_values(head, s):
            hp, hh = divmod(head, 2)
            sl = slice(hp * LANES, (hp + 1) * LANES)
            v = jnp.concatenate([v0[:, sl], v1[:, sl], v2[:, sl]], axis=0)
            vt = v.astype(F32).T.astype(BF16)[hh * HEAD_DIM:(hh + 1) * HEAD_DIM]
            m = jnp.max(s, axis=0, keepdims=True)
            p = jnp.exp(s - m)
            l = jnp.sum(p, axis=0, keepdims=True)
            return _dot(vt, p.astype(BF16)) / l, m + jnp.log(l)

        outs, lses = [], []
        pending = scores(0)
        for head in range(2 * PAIRS):
            nxt = scores(head + 1) if head + 1 < 2 * PAIRS else None
            o, lse = weighted_values(head, pending)
            outs.append(o)
            lses.append(lse)
            pending = nxt
        for hp in range(PAIRS):
            sl = slice(hp * LANES, (hp + 1) * LANES)
            o_ref[:, sl] = jnp.concatenate([outs[2 * hp], outs[2 * hp + 1]], axis=0).T.astype(BF16)
        lse_ref[...] = jnp.concatenate(lses, axis=0)

    return _call(
        body, name="attn_fwd", grid=(D_MODEL // SLAB, nb), args=(qn, kn, kn, kn, proj, proj, proj, bias),
        in_specs=[pl.BlockSpec((QB, SLAB), lambda p, b: (b, p))] + _window_specs(0) + _window_specs(v_col0)
        + [pl.BlockSpec((2 * PAIRS, KW, QB), lambda p, b: (p, 0, 0))],
        out_specs=[pl.BlockSpec((QB, SLAB), lambda p, b: (b, p)),
                   pl.BlockSpec((2 * PAIRS, QB), lambda p, b: (p, b))],
        out_shape=[jax.ShapeDtypeStruct((t, D_MODEL), BF16), jax.ShapeDtypeStruct((N_HEADS, t), F32)],
        semantics=("parallel", "arbitrary"), rider=rider)


def _shift_down(u, halo, n):
    rows = lax.broadcasted_iota(jnp.int32, (u.shape[0], 1), 0)
    out = pltpu.roll(u, n, 0)
    for j in range(n):
        out = jnp.where(rows == j, halo[SUBLANES - n + j:SUBLANES - n + j + 1, :], out)
    return out


def _shift_up(u, halo, n):
    tm = u.shape[0]
    rows = lax.broadcasted_iota(jnp.int32, (tm, 1), 0)
    out = pltpu.roll(u, tm - n, 0)
    for j in range(n):
        out = jnp.where(rows == tm - n + j, halo[j:j + 1, :], out)
    return out


def _conv_fwd(proj, conv_w, conv_b):
    t = proj.shape[0]
    tm = min(t, 512)
    hb = tm // SUBLANES

    def body(bg_ref, cg_ref, xc_ref, cgh_ref, xch_ref, w_ref, b_ref, o_ref):
        i = pl.program_id(0)
        u = cg_ref[...].astype(F32) * xc_ref[...].astype(F32)
        halo = cgh_ref[...].astype(F32) * xch_ref[...].astype(F32)
        halo = jnp.where(i > 0, halo, 0.0)
        w = w_ref[...]
        s = w[0:1] * _shift_down(u, halo, 2) + w[1:2] * _shift_down(u, halo, 1) + w[2:3] * u
        o_ref[...] = (bg_ref[...].astype(F32) * (b_ref[...] + s)).astype(BF16)

    def prev(col):
        return pl.BlockSpec((SUBLANES, D_MODEL), lambda i: (jnp.maximum(i * hb - 1, 0), col))

    return pl.pallas_call(
        body, name="conv_fwd", grid=(t // tm,),
        in_specs=[pl.BlockSpec((tm, D_MODEL), lambda i: (i, 3)),
                  pl.BlockSpec((tm, D_MODEL), lambda i: (i, 4)),
                  pl.BlockSpec((tm, D_MODEL), lambda i: (i, 5)),
                  prev(4), prev(5),
                  pl.BlockSpec((3, D_MODEL), lambda i: (0, 0)),
                  pl.BlockSpec((1, D_MODEL), lambda i: (0, 0))],
        out_specs=pl.BlockSpec((tm, D_MODEL), lambda i: (i, 0)),
        out_shape=jax.ShapeDtypeStruct((t, D_MODEL), BF16),
        compiler_params=_params(("parallel",)),
    )(proj, proj, proj, proj, proj, conv_w, conv_b)


def _mix_out(y_attn, y_conv, gates, x, w_ap, w_cp, w_out, g2):
    t = x.shape[0]
    tm = min(t, 512)

    def body(ya_in, yc_in, g_ref, x_ref, wap, wcp, wout, g2_ref, ya_ref, yc_ref, mg_ref, x1_ref, h2_ref):
        ya = _dot(ya_in[...], wap[...])
        yc = _dot(yc_in[...], wcp[...])
        ya_ref[...] = ya.astype(BF16)
        yc_ref[...] = yc.astype(BF16)
        merged = (g_ref[:, :D_MODEL].astype(F32) * ya + g_ref[:, D_MODEL:].astype(F32) * yc).astype(BF16)
        mg_ref[...] = merged
        x1 = x_ref[...] + _dot(merged, wout[...])
        x1_ref[...] = x1
        r = lax.rsqrt(jnp.mean(x1 * x1, axis=-1, keepdims=True) + EPS)
        h2_ref[...] = (x1 * r * g2_ref[...]).astype(BF16)

    row = pl.BlockSpec((tm, D_MODEL), lambda i: (i, 0))
    full = _resident((D_MODEL, D_MODEL))
    return pl.pallas_call(
        body, name="mix_out", grid=(t // tm,),
        in_specs=[row, row, pl.BlockSpec((tm, 2 * D_MODEL), lambda i: (i, 0)), row, full, full, full,
                  pl.BlockSpec((1, D_MODEL), lambda i: (0, 0))],
        out_specs=[row] * 5,
        out_shape=[jax.ShapeDtypeStruct((t, D_MODEL), BF16)] * 3
        + [jax.ShapeDtypeStruct((t, D_MODEL), F32), jax.ShapeDtypeStruct((t, D_MODEL), BF16)],
        compiler_params=_params(("parallel",)),
    )(y_attn, y_conv, gates, x, w_ap, w_cp, w_out, g2)


def _mlp_fwd(h2, w_up, w_down, x1, target):
    t = h2.shape[0]
    tm = min(t, 512)
    tf = 1024
    nf = D_FF // tf

    def body(h2_ref, wup, wdn, x1_ref, tg_ref, a_ref, dy_ref, dyb_ref, loss_ref):
        h2v = h2_ref[...]
        acc = None
        for j in range(nf):
            cols = slice(j * tf, (j + 1) * tf)
            a = _dot(h2v, wup[:, cols])
            a_ref[:, cols] = a.astype(BF16)
            part = _dot(jnp.square(jnp.maximum(a, 0.0)).astype(BF16), wdn[cols, :])
            acc = part if acc is None else acc + part

        @pl.when(pl.program_id(0) == 0)
        def _():
            loss_ref[...] = jnp.zeros_like(loss_ref)

        diff = x1_ref[...] + acc - tg_ref[...]
        loss_ref[...] += _fold8(diff * diff)
        dy = diff * (1.0 / D_MODEL)
        dy_ref[...] = dy
        dyb_ref[...] = dy.astype(BF16)

    row = pl.BlockSpec((tm, D_MODEL), lambda i: (i, 0))
    return pl.pallas_call(
        body, name="mlp_fwd", grid=(t // tm,),
        in_specs=[row, _resident((D_MODEL, D_FF)), _resident((D_FF, D_MODEL)), row, row],
        out_specs=[pl.BlockSpec((tm, D_FF), lambda i: (i, 0)), row, row,
                   pl.BlockSpec((SUBLANES, D_MODEL), lambda i: (0, 0))],
        out_shape=[jax.ShapeDtypeStruct((t, D_FF), BF16), jax.ShapeDtypeStruct((t, D_MODEL), F32),
                   jax.ShapeDtypeStruct((t, D_MODEL), BF16), jax.ShapeDtypeStruct((SUBLANES, D_MODEL), F32)],
        compiler_params=_params(("arbitrary",)),
    )(h2, w_up, w_down, x1, target)


def _rmsnorm_bwd(xf, g, dh):
    r = lax.rsqrt(jnp.mean(xf * xf, axis=-1, keepdims=True) + EPS)
    xh = xf * r
    dxh = dh * g
    dx = r * (dxh - xh * jnp.mean(dxh * xh, axis=-1, keepdims=True))
    return dx, dh * xh


def _mlp_bwd(dyb, a, w_down, w_up, x1, dy, g2):
    t = dyb.shape[0]
    tm = min(t, 512)
    tf = 1024
    nf = D_FF // tf

    def body(dyb_ref, a_ref, wdn, wup, x1_ref, dy_ref, g2_ref, da_ref, dx1_ref, dx1b_ref, dg2_ref):
        dyv = dyb_ref[...]
        acc = None
        for j in range(nf):
            cols = slice(j * tf, (j + 1) * tf)
            du = _dot_nt(dyv, wdn[cols, :])
            da = (du * (2.0 * jnp.maximum(a_ref[:, cols].astype(F32), 0.0))).astype(BF16)
            da_ref[:, cols] = da
            part = _dot_nt(da, wup[:, cols])
            acc = part if acc is None else acc + part

        @pl.when(pl.program_id(0) == 0)
        def _():
            dg2_ref[...] = jnp.zeros_like(dg2_ref)

        dx, dg = _rmsnorm_bwd(x1_ref[...], g2_ref[...], acc)
        dx1 = dy_ref[...] + dx
        dx1_ref[...] = dx1
        dx1b_ref[...] = dx1.astype(BF16)
        dg2_ref[...] += _fold8(dg)

    row = pl.BlockSpec((tm, D_MODEL), lambda i: (i, 0))
    wide = pl.BlockSpec((tm, D_FF), lambda i: (i, 0))
    return pl.pallas_call(
        body, name="mlp_bwd", grid=(t // tm,),
        in_specs=[row, wide, _resident((D_FF, D_MODEL)), _resident((D_MODEL, D_FF)), row, row,
                  pl.BlockSpec((1, D_MODEL), lambda i: (0, 0))],
        out_specs=[wide, row, row, pl.BlockSpec((SUBLANES, D_MODEL), lambda i: (0, 0))],
        out_shape=[jax.ShapeDtypeStruct((t, D_FF), BF16), jax.ShapeDtypeStruct((t, D_MODEL), F32),
                   jax.ShapeDtypeStruct((t, D_MODEL), BF16), jax.ShapeDtypeStruct((SUBLANES, D_MODEL), F32)],
        compiler_params=_params(("arbitrary",)),
    )(dyb, a, w_down, w_up, x1, dy, g2)


def _wgrad(name, lhs, rhs_list, rhs_slabs, relu_sq=False):
    t, m = lhs.shape
    tt = min(t, 512)
    tmo = min(m, 1024)
    n_slab = sum(rhs_slabs)
    starts = [sum(rhs_slabs[:n]) for n in range(len(rhs_slabs))]
    n_rhs = len(rhs_list)

    def body(*refs):
        l_ref, r_refs, o_ref, acc = refs[0], refs[1:1 + n_rhs], refs[1 + n_rhs], refs[2 + n_rhs]
        k, s = pl.program_id(1), pl.program_id(2)
        lv = l_ref[...]
        if relu_sq:
            lv = jnp.square(jnp.maximum(lv.astype(F32), 0.0)).astype(BF16)

        @pl.when(s == 0)
        def _():
            acc[...] = jnp.zeros_like(acc)

        for n in range(n_rhs):
            @pl.when((k >= starts[n]) & (k < starts[n] + rhs_slabs[n]))
            def _(n=n):
                acc[...] += _dot_tn(lv, r_refs[n][...])

        @pl.when(s == pl.num_programs(2) - 1)
        def _():
            o_ref[...] = acc[...].astype(BF16)

    def rhs_spec(n):
        lo, cnt = starts[n], rhs_slabs[n]

        def index(i, k, s):
            inside = (k >= lo) & (k < lo + cnt)
            return (jnp.where(inside, s, 0), jnp.clip(k - lo, 0, cnt - 1))
        return pl.BlockSpec((tt, D_MODEL), index)

    return pl.pallas_call(
        body, name=name, grid=(m // tmo, n_slab, t // tt),
        in_specs=[pl.BlockSpec((tt, tmo), lambda i, k, s: (s, i))] + [rhs_spec(n) for n in range(n_rhs)],
        out_specs=pl.BlockSpec((tmo, D_MODEL), lambda i, k, s: (i, k)),
        out_shape=jax.ShapeDtypeStruct((m, n_slab * D_MODEL), BF16),
        scratch_shapes=[pltpu.VMEM((tmo, D_MODEL), F32)],
        compiler_params=_params(("parallel", "parallel", "arbitrary")),
    )(lhs, *rhs_list)


def _mix_bwd(dx1b, gates, ya, yc, w_out, w_ap, w_cp, w_g, rider=None):
    t = dx1b.shape[0]
    tm = min(t, 512)

    def body(dx_ref, g_ref, ya_ref, yc_ref, wout, wap, wcp, wg,
             dgp_ref, dya_ref, dyc_ref, dyat_ref, dycv_ref, dhg_ref, dbg_ref):
        dm = _dot_nt(dx_ref[...], wout[...])
        ga = g_ref[:, :D_MODEL].astype(F32)
        gc = g_ref[:, D_MODEL:].astype(F32)
        dya = (dm * ga).astype(BF16)
        dyc = (dm * gc).astype(BF16)
        dya_ref[...] = dya
        dyc_ref[...] = dyc
        dgpa = dm * ya_ref[...].astype(F32) * ga * (1.0 - ga)
        dgpc = dm * yc_ref[...].astype(F32) * gc * (1.0 - gc)

        @pl.when(pl.program_id(0) == 0)
        def _():
            dbg_ref[...] = jnp.zeros_like(dbg_ref)

        dbg_ref[:, :D_MODEL] += _fold8(dgpa)
        dbg_ref[:, D_MODEL:] += _fold8(dgpc)
        dgpa = dgpa.astype(BF16)
        dgpc = dgpc.astype(BF16)
        dgp_ref[:, :D_MODEL] = dgpa
        dgp_ref[:, D_MODEL:] = dgpc
        dyat_ref[...] = _dot_nt(dya, wap[...]).astype(BF16)
        dycv_ref[...] = _dot_nt(dyc, wcp[...]).astype(BF16)
        dhg_ref[...] = _dot_nt(dgpa, wg[:, :D_MODEL]) + _dot_nt(dgpc, wg[:, D_MODEL:])

    row = pl.BlockSpec((tm, D_MODEL), lambda i: (i, 0))
    row2 = pl.BlockSpec((tm, 2 * D_MODEL), lambda i: (i, 0))
    full = _resident((D_MODEL, D_MODEL))
    return _call(
        body, name="mix_bwd", grid=(t // tm,), args=(dx1b, gates, ya, yc, w_out, w_ap, w_cp, w_g),
        in_specs=[row, row2, row, row, full, full, full, _resident((D_MODEL, 2 * D_MODEL))],
        out_specs=[row2, row, row, row, row, row, pl.BlockSpec((SUBLANES, 2 * D_MODEL), lambda i: (0, 0))],
        out_shape=[jax.ShapeDtypeStruct((t, 2 * D_MODEL), BF16)] + [jax.ShapeDtypeStruct((t, D_MODEL), BF16)] * 4
        + [jax.ShapeDtypeStruct((t, D_MODEL), F32), jax.ShapeDtypeStruct((SUBLANES, 2 * D_MODEL), F32)],
        semantics=("arbitrary",), rider=rider)


def _conv_bwd(dyconv, proj, conv_w, conv_b, rider=None):
    t = proj.shape[0]
    tm = min(t, 512)
    hb = tm // SUBLANES
    last = t // SUBLANES - 1

    def body(dy_ref, dyn_ref, bg_ref, bgn_ref, cg_ref, cgp_ref, xc_ref, xcp_ref, w_ref, b_ref,
             o_ref, dcb_ref, dcw_ref):
        i = pl.program_id(0)
        cg = cg_ref[...].astype(F32)
        xc = xc_ref[...].astype(F32)
        bg = bg_ref[...].astype(F32)
        u = cg * xc
        prev = jnp.where(i > 0, cgp_ref[...].astype(F32) * xcp_ref[...].astype(F32), 0.0)
        u1 = _shift_down(u, prev, 1)
        u2 = _shift_down(u, prev, 2)
        w = w_ref[...]
        conv = b_ref[...] + (w[0:1] * u2 + w[1:2] * u1 + w[2:3] * u)
        dy = dy_ref[...].astype(F32)
        dconv = dy * bg
        nxt = jnp.where(i < pl.num_programs(0) - 1, dyn_ref[...].astype(F32) * bgn_ref[...].astype(F32), 0.0)
        du = w[2:3] * dconv + w[1:2] * _shift_up(dconv, nxt, 1) + w[0:1] * _shift_up(dconv, nxt, 2)
        o_ref[:, :D_MODEL] = (dy * conv).astype(BF16)
        o_ref[:, D_MODEL:2 * D_MODEL] = (du * xc).astype(BF16)
        o_ref[:, 2 * D_MODEL:] = (du * cg).astype(BF16)

        @pl.when(i == 0)
        def _():
            dcb_ref[...] = jnp.zeros_like(dcb_ref)
            dcw_ref[...] = jnp.zeros_like(dcw_ref)

        dcb_ref[...] += _fold8(dconv)
        dcw_ref[0:SUBLANES] += _fold8(dconv * u2)
        dcw_ref[SUBLANES:2 * SUBLANES] += _fold8(dconv * u1)
        dcw_ref[2 * SUBLANES:] += _fold8(dconv * u)

    def prev(col):
        return pl.BlockSpec((SUBLANES, D_MODEL), lambda i: (jnp.maximum(i * hb - 1, 0), col))

    def nxt(col):
        return pl.BlockSpec((SUBLANES, D_MODEL), lambda i: (jnp.minimum((i + 1) * hb, last), col))

    def cur(col):
        return pl.BlockSpec((tm, D_MODEL), lambda i: (i, col))

    return _call(
        body, name="conv_bwd", grid=(t // tm,),
        args=(dyconv, dyconv, proj, proj, proj, proj, proj, proj, conv_w, conv_b),
        in_specs=[cur(0), nxt(0), cur(3), nxt(3), cur(4), prev(4), cur(5), prev(5),
                  pl.BlockSpec((3, D_MODEL), lambda i: (0, 0)), pl.BlockSpec((1, D_MODEL), lambda i: (0, 0))],
        out_specs=[pl.BlockSpec((tm, 3 * D_MODEL), lambda i: (i, 0)),
                   pl.BlockSpec((SUBLANES, D_MODEL), lambda i: (0, 0)),
                   pl.BlockSpec((3 * SUBLANES, D_MODEL), lambda i: (0, 0))],
        out_shape=[jax.ShapeDtypeStruct((t, 3 * D_MODEL), BF16), jax.ShapeDtypeStruct((SUBLANES, D_MODEL), F32),
                   jax.ShapeDtypeStruct((3 * SUBLANES, D_MODEL), F32)],
        semantics=("arbitrary",), rider=rider)


def _attn_bwd(qn, kn, proj, dyattn, y_attn, lse, bias, rider=None):
    t = qn.shape[0]
    nb = t // QB
    v_col0 = 2 * D_MODEL // SLAB

    def body(q_ref, k0, k1, k2, v0, v1, v2, do_ref, o_ref, lse_ref, bias_ref,
             dq_ref, dk_ref, dv_ref, db_ref, acck, accv):
        b = pl.program_id(1)

        @pl.when(b == 0)
        def _():
            acck[...] = jnp.zeros_like(acck)
            accv[...] = jnp.zeros_like(accv)
            db_ref[...] = jnp.zeros_like(db_ref)

        @pl.when(b < nb)
        def _():
            head_a = lax.broadcasted_iota(jnp.int32, (1, LANES), 1) < HEAD_DIM
            valid = lax.broadcasted_iota(jnp.int32, (KW, 1), 0) >= (2 - b) * QB

            def window(refs, hp):
                sl = slice(hp * LANES, (hp + 1) * LANES)
                return jnp.concatenate([r[:, sl] for r in refs], axis=0)

            def transposed(x, hh):
                return x.astype(F32).T.astype(BF16)[hh * HEAD_DIM:(hh + 1) * HEAD_DIM]

            def probs(head):
                hp, hh = divmod(head, 2)
                sl = slice(hp * LANES, (hp + 1) * LANES)
                mine = head_a if hh == 0 else jnp.logical_not(head_a)
                k = window((k0, k1, k2), hp)
                s = _dot_nt(jnp.where(mine, k, jnp.zeros_like(k)), q_ref[:, sl]) + bias_ref[head]
                return jnp.exp(jnp.where(valid, s, NEG_INF) - lse_ref[head:head + 1, :])

            def grads(head, p):
                hp, hh = divmod(head, 2)
                sl = slice(hp * LANES, (hp + 1) * LANES)
                rows = slice(hh * HEAD_DIM, (hh + 1) * HEAD_DIM)
                mine = head_a if hh == 0 else jnp.logical_not(head_a)
                do = do_ref[:, sl]
                v = window((v0, v1, v2), hp)
                delta = jnp.sum((do.astype(F32).T * o_ref[:, sl].astype(F32).T)[rows], axis=0, keepdims=True)
                ds = p * (_dot_nt(jnp.where(mine, v, jnp.zeros_like(v)), do) - delta)
                db_ref[head] += ds
                pb, dsb = p.astype(BF16), ds.astype(BF16)
                dvt = _dot_nt(transposed(do, hh), pb)
                dkt = _dot_nt(transposed(q_ref[:, sl], hh), dsb)
                dqt = _dot(transposed(window((k0, k1, k2), hp), hh), dsb)
                return dqt, dkt, dvt

            out = []
            pending = probs(0)
            for head in range(2 * PAIRS):
                nxt = probs(head + 1) if head + 1 < 2 * PAIRS else None
                out.append(grads(head, pending))
                pending = nxt
            for hp in range(PAIRS):
                sl = slice(hp * LANES, (hp + 1) * LANES)
                dqt, dkt, dvt = (jnp.concatenate([out[2 * hp][n], out[2 * hp + 1][n]], axis=0) for n in range(3))
                dq_ref[:, sl] = dqt.T
                for w in range(3):
                    slot = lax.rem(b + w + 1, 3)
                    cols = slice(w * QB, (w + 1) * QB)
                    if w == 2:
                        acck[hp, slot] = dkt[:, cols]
                        accv[hp, slot] = dvt[:, cols]
                    else:
                        acck[hp, slot] += dkt[:, cols]
                        accv[hp, slot] += dvt[:, cols]

        done = lax.rem(b + 1, 3)
        for hp in range(PAIRS):
            sl = slice(hp * LANES, (hp + 1) * LANES)
            dk_ref[:, sl] = acck[hp, done].T
            dv_ref[:, sl] = accv[hp, done].T.astype(BF16)

    def cur(p, b):
        return (jnp.minimum(b, nb - 1), p)

    def window(col0):
        return [pl.BlockSpec((QB, SLAB), functools.partial(
            lambda p, b, back: (jnp.maximum(jnp.minimum(b, nb - 1) - back, 0), col0 + p), back=back))
            for back in (2, 1, 0)]

    def done_block(p, b):
        return (jnp.maximum(b - 2, 0), p)

    tile = pl.BlockSpec((2 * PAIRS, KW, QB), lambda p, b: (p, 0, 0))
    here = pl.BlockSpec((QB, SLAB), cur)
    return _call(
        body, name="attn_bwd", grid=(D_MODEL // SLAB, nb + 2),
        args=(qn, kn, kn, kn, proj, proj, proj, dyattn, y_attn, lse, bias),
        in_specs=[here] + window(0) + window(v_col0)
        + [here, here, pl.BlockSpec((2 * PAIRS, QB), lambda p, b: (p, jnp.minimum(b, nb - 1))), tile],
        out_specs=[here, pl.BlockSpec((QB, SLAB), done_block), pl.BlockSpec((QB, SLAB), done_block), tile],
        out_shape=[jax.ShapeDtypeStruct((t, D_MODEL), F32), jax.ShapeDtypeStruct((t, D_MODEL), F32),
                   jax.ShapeDtypeStruct((t, D_MODEL), BF16), jax.ShapeDtypeStruct((N_HEADS, KW, QB), F32)],
        scratch_shapes=[pltpu.VMEM((PAIRS, 3, LANES, QB), F32), pltpu.VMEM((PAIRS, 3, LANES, QB), F32)],
        semantics=("parallel", "arbitrary"), rider=rider)


def _qknorm_bwd(proj, dqn, dkn, gq, gk):
    t = proj.shape[0]
    tm = min(t, 512)
    scale = HEAD_DIM ** -0.5

    def body(q_ref, k_ref, dqn_ref, dkn_ref, gq_ref, gk_ref, o_ref, dgq_ref, dgk_ref):
        e = _head_sum_matrix()

        @pl.when(pl.program_id(0) == 0)
        def _():
            dgq_ref[...] = jnp.zeros_like(dgq_ref)
            dgk_ref[...] = jnp.zeros_like(dgk_ref)

        for n, (src, dn_ref, g_ref, dg_ref, sc) in enumerate(
                ((q_ref, dqn_ref, gq_ref, dgq_ref, scale), (k_ref, dkn_ref, gk_ref, dgk_ref, 1.0))):
            for s in range(D_MODEL // LANES):
                sl = slice(s * LANES, (s + 1) * LANES)
                xf = src[:, sl].astype(F32)
                r = lax.rsqrt(_head_sums(xf * xf, e) * (1.0 / HEAD_DIM) + EPS)
                xh = xf * r
                dn = dn_ref[:, sl] * sc
                dg_ref[:, sl] += _fold8(dn * xh)
                dxh = dn * g_ref[:, sl]
                mean = _head_sums(dxh * xh, e) * (1.0 / HEAD_DIM)
                o_ref[:, n * D_MODEL + s * LANES:n * D_MODEL + (s + 1) * LANES] = (r * (dxh - xh * mean)).astype(BF16)

    row = pl.BlockSpec((tm, D_MODEL), lambda i: (i, 0))
    vec = pl.BlockSpec((1, D_MODEL), lambda i: (0, 0))
    acc = pl.BlockSpec((SUBLANES, D_MODEL), lambda i: (0, 0))
    return pl.pallas_call(
        body, name="qknorm_bwd", grid=(t // tm,),
        in_specs=[row, pl.BlockSpec((tm, D_MODEL), lambda i: (i, 1)), row, row, vec, vec],
        out_specs=[pl.BlockSpec((tm, 2 * D_MODEL), lambda i: (i, 0)), acc, acc],
        out_shape=[jax.ShapeDtypeStruct((t, 2 * D_MODEL), BF16)] + [jax.ShapeDtypeStruct((SUBLANES, D_MODEL), F32)] * 2,
        compiler_params=_params(("arbitrary",)),
    )(proj, proj, dqn, dkn, gq, gk)


def _in_bwd(dqk, dv, dconv, w_in, dhg, x, g1, dx1, rider=None):
    t = x.shape[0]
    tm = min(t, 512)

    def body(dqk_ref, dv_ref, dc_ref, w_ref, dhg_ref, x_ref, g_ref, dx1_ref, dx_ref, dg_ref):
        acc = dhg_ref[...]
        slab = 0
        for src, n in ((dqk_ref, 2), (dv_ref, 1), (dc_ref, 3)):
            for s in range(n):
                acc = acc + _dot_nt(src[:, s * D_MODEL:(s + 1) * D_MODEL],
                                    w_ref[:, slab * D_MODEL:(slab + 1) * D_MODEL])
                slab += 1

        @pl.when(pl.program_id(0) == 0)
        def _():
            dg_ref[...] = jnp.zeros_like(dg_ref)

        dx, dg = _rmsnorm_bwd(x_ref[...], g_ref[...], acc)
        dx_ref[...] = dx1_ref[...] + dx
        dg_ref[...] += _fold8(dg)

    row = pl.BlockSpec((tm, D_MODEL), lambda i: (i, 0))
    return _call(
        body, name="in_bwd", grid=(t // tm,), args=(dqk, dv, dconv, w_in, dhg, x, g1, dx1),
        in_specs=[pl.BlockSpec((tm, 2 * D_MODEL), lambda i: (i, 0)), row,
                  pl.BlockSpec((tm, 3 * D_MODEL), lambda i: (i, 0)),
                  _resident(w_in.shape), row, row, pl.BlockSpec((1, D_MODEL), lambda i: (0, 0)), row],
        out_specs=[row, pl.BlockSpec((SUBLANES, D_MODEL), lambda i: (0, 0))],
        out_shape=[jax.ShapeDtypeStruct((t, D_MODEL), F32), jax.ShapeDtypeStruct((SUBLANES, D_MODEL), F32)],
        semantics=("arbitrary",), rider=rider)


def _bias_grad_fold(dbias):
    def body(d_ref, o_ref):
        jj = lax.broadcasted_iota(jnp.int32, (QB, QB), 0)
        ii = lax.broadcasted_iota(jnp.int32, (QB, QB), 1)
        flip = (jj + ii == QB - 1).astype(BF16)
        low = jj + ii < QB
        pos, neg = [], []
        for w in range(KW // QB):
            x = d_ref[0, QB * w:QB * (w + 1), :]
            hi = x.astype(BF16)
            r1 = x - hi.astype(F32)
            mid = r1.astype(BF16)
            lo = (r1 - mid.astype(F32)).astype(BF16)
            xr = _dot(hi, flip) + _dot(mid, flip) + _dot(lo, flip)
            for keep, acc in ((low, pos), (jnp.logical_not(low), neg)):
                part = pltpu.roll(jnp.where(keep, xr, 0.0), 0, 1, stride=1, stride_axis=0)
                acc.append(jnp.sum(part, axis=0, keepdims=True))
        far = pos[1] + neg[0] + pos[0]
        o_ref[0] = jnp.zeros((SUBLANES, QB), F32)
        o_ref[0, 0:1, :] = neg[2]
        o_ref[0, 1:2, :] = pos[2] + neg[1]
        o_ref[0, 2:3, :] = jnp.broadcast_to(jnp.sum(far, axis=-1, keepdims=True), (1, QB))

    return pl.pallas_call(
        body, name="bias_grad_fold", grid=(N_HEADS,),
        in_specs=[pl.BlockSpec((1, KW, QB), lambda h: (h, 0, 0))],
        out_specs=pl.BlockSpec((1, SUBLANES, QB), lambda h: (h, 0, 0)),
        out_shape=jax.ShapeDtypeStruct((N_HEADS, SUBLANES, QB), F32),
        compiler_params=_params(("parallel",)),
    )(dbias)


def _small_partials(dg1, dgq, dgk, dcb, dcw, dbg, dg2, dbias_fold, loss_tile):
    def head_fold(v):
        acc = v[:, 0:LANES]
        for s in range(1, D_MODEL // LANES):
            acc = acc + v[:, s * LANES:(s + 1) * LANES]
        return acc + pltpu.roll(acc, HEAD_DIM, 1)

    def body(dg1_ref, dgq_ref, dgk_ref, dcb_ref, dcw_ref, dbg_ref, dg2_ref, db_ref, loss_ref, o_ref):
        o_ref[...] = jnp.zeros_like(o_ref)
        o_ref[0:1, :] = jnp.sum(dg1_ref[...], axis=0, keepdims=True)
        o_ref[1:2, 0:LANES] = head_fold(jnp.sum(dgq_ref[...], axis=0, keepdims=True))
        o_ref[2:3, 0:LANES] = head_fold(jnp.sum(dgk_ref[...], axis=0, keepdims=True))
        o_ref[3:4, :] = jnp.sum(dcb_ref[...], axis=0, keepdims=True)
        for j in range(3):
            o_ref[4 + j:5 + j, :] = jnp.sum(dcw_ref[j * SUBLANES:(j + 1) * SUBLANES, :], axis=0, keepdims=True)
        o_ref[7:8, :] = jnp.sum(dbg_ref[:, :D_MODEL], axis=0, keepdims=True)
        o_ref[8:9, :] = jnp.sum(dbg_ref[:, D_MODEL:], axis=0, keepdims=True)
        o_ref[9:10, :] = jnp.sum(dg2_ref[...], axis=0, keepdims=True)
        for h in range(N_HEADS):
            for part in range(3):
                o_ref[10 + h:11 + h, part * QB:(part + 1) * QB] = db_ref[h, part:part + 1, :]
        loss = (0.5 / D_MODEL) * jnp.sum(jnp.sum(loss_ref[...], axis=0, keepdims=True), axis=-1, keepdims=True)
        o_ref[26:27, :] = jnp.broadcast_to(loss, (1, D_MODEL))

    return pl.pallas_call(
        body, name="small_partials",
        out_shape=jax.ShapeDtypeStruct((32, D_MODEL), F32),
        compiler_params=_params(),
    )(dg1, dgq, dgk, dcb, dcw, dbg, dg2, dbias_fold, loss_tile)


MID_AXES = (0, 0, 1, 0)
MLP_AXES = (1, 0)


def _local_step(x, target, norm1_g, q_norm_g, k_norm_g, rel_bias, conv_w, conv_b, b_gate, norm2_g,
                w_in, mid_w, mlp_w, distributed):
    g1 = norm1_g.reshape(1, D_MODEL)
    g2 = norm2_g.reshape(1, D_MODEL)
    gq = jnp.tile(q_norm_g, N_HEADS).reshape(1, D_MODEL)
    gk = jnp.tile(k_norm_g, N_HEADS).reshape(1, D_MODEL)
    cb = conv_b.reshape(1, D_MODEL)
    bias = _bias_tiles(rel_bias)

    (proj, h), got = _in_proj(x, g1, w_in, rider=_Gather(mid_w, MID_AXES) if distributed else None)
    w_ap, w_cp, w_g, w_out = got if distributed else mid_w
    gates = _gate_proj(h, w_g, b_gate.reshape(1, 2 * D_MODEL))
    qn, kn = _qknorm_fwd(proj, gq, gk)
    (y_attn, lse), got = _attn_fwd(qn, kn, proj, bias, rider=_Gather(mlp_w, MLP_AXES) if distributed else None)
    w_up, w_down = got if distributed else mlp_w
    y_conv = _conv_fwd(proj, conv_w, cb)
    ya, yc, merged, x1, h2 = _mix_out(y_attn, y_conv, gates, x, w_ap, w_cp, w_out, g2)
    a, dy, dyb, loss_tile = _mlp_fwd(h2, w_up, w_down, x1, target)

    da, dx1, dx1b, dg2 = _mlp_bwd(dyb, a, w_down, w_up, x1, dy, g2)
    gw_down = _wgrad("wgrad_down", a, [dyb], [1], relu_sq=True)
    gw_up = _wgrad("wgrad_up", h2, [da], [D_FF // D_MODEL])
    (dgp, dya, dyc, dyattn, dyconv, dhg, dbg), mlp_swapped = _mix_bwd(
        dx1b, gates, ya, yc, w_out, w_ap, w_cp, w_g,
        rider=_PairSwap((gw_up, gw_down), MLP_AXES) if distributed else None)
    gw_out = _wgrad("wgrad_out", merged, [dx1b], [1])
    gw_ap = _wgrad("wgrad_attn_proj", y_attn, [dya], [1])
    gw_cp = _wgrad("wgrad_conv_proj", y_conv, [dyc], [1])
    gw_g = _wgrad("wgrad_gate", h, [dgp], [2])
    mid = (gw_ap, gw_cp, gw_g, gw_out)
    (dconv, dcb, dcw), mid_swapped = _conv_bwd(
        dyconv, proj, conv_w, cb, rider=_PairSwap(mid, MID_AXES) if distributed else None)
    early = mid + (gw_up, gw_down)
    early_sums = (_pair_add(early, tuple(mid_swapped) + tuple(mlp_swapped), MID_AXES + MLP_AXES)
                  if distributed else None)
    (dqn, dkn, dv, dbias), early_shares = _attn_bwd(
        qn, kn, proj, dyattn, y_attn, lse, bias, rider=_ChipScatter(early_sums) if distributed else None)
    dqk, dgq, dgk = _qknorm_bwd(proj, dqn, dkn, gq, gk)
    gw_in = _wgrad("wgrad_in", h, [dqk, dv, dconv], [2, 1, 3])
    in_sums = (_pair_add((gw_in,), _exchange("swap_w_in_grad", _PairSwap((gw_in,), (1,))), (1,))
               if distributed else None)
    (dx, dg1), in_shares = _in_bwd(dqk, dv, dconv, w_in, dhg, x, g1, dx1,
                                   rider=_ChipScatter(in_sums) if distributed else None)
    small = _small_partials(dg1, dgq, dgk, dcb, dcw, dbg, dg2, _bias_grad_fold(dbias), loss_tile)
    grads = tuple(in_shares) + tuple(early_shares) if distributed else (gw_in,) + early
    return dx, grads, small


def _me():
    return lax.axis_index("x"), lax.axis_index("y"), lax.axis_index("c")


def _peer(me, rel):
    x, y, c = me
    return (1 - x if rel & 4 else x, 1 - y if rel & 2 else y, 1 - c if rel & 1 else c)


def _linear(dev):
    return 4 * dev[0] + 2 * dev[1] + dev[2]


BIG_AXES = (1, 0, 0, 1, 0, 1, 0)


def _block(ref, axis, idx, size):
    return ref.at[pl.ds(idx * size, size), :] if axis == 0 else ref.at[:, pl.ds(idx * size, size)]


def _cast_shards(shards):
    def body(*refs):
        for src, dst in zip(refs[:len(shards)], refs[len(shards):]):
            dst[...] = src[...].astype(BF16)

    return pl.pallas_call(
        body, name="cast_shards",
        out_shape=[jax.ShapeDtypeStruct(s.shape, BF16) for s in shards],
        compiler_params=_params(),
    )(*shards)


class _Gather:
    def __init__(self, shards, axes):
        self.arrays, self.axes, self.n = list(shards), tuple(axes), len(shards)
        self.sizes = [s.shape[axis] for s, axis in zip(shards, axes)]
        self.out_shape = []
        for s, axis in zip(shards, axes):
            shape = (s.shape[0] * N_DEV, s.shape[1]) if axis == 0 else (s.shape[0], s.shape[1] * N_DEV)
            self.out_shape.append(jax.ShapeDtypeStruct(shape, s.dtype))
        self.scratch = [pltpu.SemaphoreType.DMA((self.n, 7)), pltpu.SemaphoreType.DMA((self.n, 7)),
                        pltpu.SemaphoreType.DMA((self.n,))]

    def _copies(self, srcs, outs, sems):
        send_sems, recv_sems, local_sems = sems
        me = _me()
        sibling = _peer(me, 1)
        chips = [_peer(me, rel) for rel in (4, 2, 6)]

        def rows(a, dev):
            return _block(outs[a], self.axes[a], _linear(dev), self.sizes[a])

        def copy(a, k, block_dev, to, src=None):
            return pltpu.make_async_remote_copy(
                src_ref=rows(a, block_dev) if src is None else src, dst_ref=rows(a, block_dev),
                send_sem=send_sems.at[a, k], recv_sem=recv_sems.at[a, k], device_id=to, device_id_type=MESH_T)

        own = [pltpu.make_async_copy(srcs[a], rows(a, me), local_sems.at[a]) for a in range(self.n)]
        first = []
        for a in range(self.n):
            first.append(copy(a, 0, me, sibling, src=srcs[a]))
            for j, chip in enumerate(chips):
                first.append(copy(a, 1 + j, me, chip, src=srcs[a]))
        return me, sibling, chips, copy, own, first

    def start(self, srcs, outs, sems):
        _, _, _, _, own, first = self._copies(srcs, outs, sems)
        for cp in own + first:
            cp.start()

    def finish(self, srcs, outs, sems):
        me, sibling, chips, copy, own, first = self._copies(srcs, outs, sems)
        passed = []
        for a in range(self.n):
            for j, chip in enumerate(chips):
                copy(a, 1 + j, chip, me).wait_recv()
                fwd = copy(a, 4 + j, chip, sibling)
                fwd.start()
                passed.append(fwd)
        for a in range(self.n):
            copy(a, 0, sibling, me).wait_recv()
            for j, chip in enumerate(chips):
                copy(a, 4 + j, _peer(chip, 1), me).wait_recv()
        for cp in first + passed:
            cp.wait_send()
        for cp in own:
            cp.wait()


N_CHIPS = 4


def _shard_shape(g, axis):
    return (g.shape[0] // N_DEV, g.shape[1]) if axis == 0 else (g.shape[0], g.shape[1] // N_DEV)


class _PairSwap:
    def __init__(self, grads, axes):
        self.arrays, self.axes, self.n = list(grads), tuple(axes), len(grads)
        self.sizes = [g.shape[axis] // N_DEV for g, axis in zip(grads, axes)]
        self.out_shape = [jax.ShapeDtypeStruct((N_CHIPS,) + _shard_shape(g, axis), g.dtype)
                          for g, axis in zip(grads, axes)]
        self.scratch = [pltpu.SemaphoreType.DMA((self.n, N_CHIPS)), pltpu.SemaphoreType.DMA((self.n, N_CHIPS))]

    def _copies(self, srcs, outs, sems):
        send_sems, recv_sems = sems
        x, y, c = _me()
        sibling = (x, y, 1 - c)
        copies = []
        for a in range(self.n):
            for chip in range(N_CHIPS):
                owner_idx = 2 * chip + (1 - c)
                copies.append(pltpu.make_async_remote_copy(
                    src_ref=_block(srcs[a], self.axes[a], owner_idx, self.sizes[a]), dst_ref=outs[a].at[chip],
                    send_sem=send_sems.at[a, chip], recv_sem=recv_sems.at[a, chip],
                    device_id=sibling, device_id_type=MESH_T))
        return copies

    def start(self, srcs, outs, sems):
        for cp in self._copies(srcs, outs, sems):
            cp.start()

    def finish(self, srcs, outs, sems):
        for cp in self._copies(srcs, outs, sems):
            cp.wait()


def _pair_add(grads, swapped, axes):
    n = len(grads)
    c_arr = lax.axis_index("c").astype(jnp.int32).reshape(1)

    def body(c_ref, *refs):
        del c_ref
        mine, got, outs = refs[:n], refs[n:2 * n], refs[2 * n:]
        for a in range(n):
            outs[a][0] = (mine[a][...].astype(F32) + got[a][0].astype(F32)).astype(BF16)

    in_specs, out_specs, out_shape = [], [], []
    for g, axis in zip(grads, axes):
        shard = _shard_shape(g, axis)
        if axis == 0:
            in_specs.append(pl.BlockSpec(shard, lambda s, c_ref: (2 * s + c_ref[0], 0)))
        else:
            in_specs.append(pl.BlockSpec(shard, lambda s, c_ref: (0, 2 * s + c_ref[0])))
    for g, axis in zip(grads, axes):
        shard = _shard_shape(g, axis)
        in_specs.append(pl.BlockSpec((1,) + shard, lambda s, c_ref: (s, 0, 0)))
        out_specs.append(pl.BlockSpec((1,) + shard, lambda s, c_ref: (s, 0, 0)))
        out_shape.append(jax.ShapeDtypeStruct((N_CHIPS,) + shard, BF16))
    return pl.pallas_call(
        body, name="pair_add_" + str(n),
        grid_spec=pltpu.PrefetchScalarGridSpec(num_scalar_prefetch=1, grid=(N_CHIPS,), in_specs=in_specs,
                                               out_specs=out_specs),
        out_shape=out_shape, compiler_params=_params(("arbitrary",)),
    )(c_arr, *grads, *swapped)


class _ChipScatter:
    def __init__(self, sums):
        self.arrays, self.n = list(sums), len(sums)
        self.out_shape = [jax.ShapeDtypeStruct(s.shape, s.dtype) for s in sums]
        self.scratch = [pltpu.SemaphoreType.DMA((self.n, 3)), pltpu.SemaphoreType.DMA((self.n, 3)),
                        pltpu.SemaphoreType.DMA((self.n,))]

    def _copies(self, srcs, outs, sems):
        send_sems, recv_sems, local_sems = sems
        me = _me()
        my_chip = 2 * me[0] + me[1]
        own = [pltpu.make_async_copy(srcs[a].at[my_chip], outs[a].at[my_chip], local_sems.at[a])
               for a in range(self.n)]
        sends, recvs = [], []
        for a in range(self.n):
            for k, rel in enumerate((4, 2, 6)):
                peer = _peer(me, rel)
                peer_chip = 2 * peer[0] + peer[1]
                sends.append(pltpu.make_async_remote_copy(
                    src_ref=srcs[a].at[peer_chip], dst_ref=outs[a].at[my_chip],
                    send_sem=send_sems.at[a, k], recv_sem=recv_sems.at[a, k], device_id=peer, device_id_type=MESH_T))
                recvs.append(pltpu.make_async_remote_copy(
                    src_ref=srcs[a].at[my_chip], dst_ref=outs[a].at[peer_chip],
                    send_sem=send_sems.at[a, k], recv_sem=recv_sems.at[a, k], device_id=peer, device_id_type=MESH_T))
        return own, sends, recvs

    def start(self, srcs, outs, sems):
        own, sends, _ = self._copies(srcs, outs, sems)
        for cp in own + sends:
            cp.start()

    def finish(self, srcs, outs, sems):
        own, sends, recvs = self._copies(srcs, outs, sems)
        for cp in recvs:
            cp.wait_recv()
        for cp in sends:
            cp.wait_send()
        for cp in own:
            cp.wait()


def _call(body, *, name, args, in_specs, out_specs, out_shape, grid=(), scratch_shapes=(), semantics=None,
          rider=None):
    if rider is None:
        return pl.pallas_call(
            body, name=name, grid=grid, in_specs=in_specs, out_specs=out_specs, out_shape=out_shape,
            scratch_shapes=list(scratch_shapes), compiler_params=_params(semantics))(*args), None
    n_in, n_out, n_scr, r = len(in_specs), len(out_specs), len(scratch_shapes), rider.n

    def wrapped(*refs):
        ins, r_ins = refs[:n_in], refs[n_in:n_in + r]
        outs = refs[n_in + r:n_in + r + n_out]
        r_outs = refs[n_in + r + n_out:n_in + 2 * r + n_out]
        scr = refs[n_in + 2 * r + n_out:n_in + 2 * r + n_out + n_scr]
        sems = refs[n_in + 2 * r + n_out + n_scr:]
        first, last = None, None
        for ax in range(len(grid)):
            f, l = pl.program_id(ax) == 0, pl.program_id(ax) == pl.num_programs(ax) - 1
            first = f if first is None else first & f
            last = l if last is None else last & l
        if first is None:
            rider.start(r_ins, r_outs, sems)
            body(*ins, *outs, *scr)
            rider.finish(r_ins, r_outs, sems)
            return

        @pl.when(first)
        def _():
            rider.start(r_ins, r_outs, sems)

        body(*ins, *outs, *scr)

        @pl.when(last)
        def _():
            rider.finish(r_ins, r_outs, sems)

    any_spec = pl.BlockSpec(memory_space=pl.ANY)
    out = pl.pallas_call(
        wrapped, name=name, grid=grid, in_specs=list(in_specs) + [any_spec] * r,
        out_specs=list(out_specs) + [any_spec] * r, out_shape=list(out_shape) + rider.out_shape,
        scratch_shapes=list(scratch_shapes) + rider.scratch,
        compiler_params=_params(None if semantics is None else ("arbitrary",) * len(semantics)),
    )(*args, *rider.arrays)
    return out[:n_out], out[n_out:]


def _exchange(name, rider):
    def body():
        pass

    return _call(body, name=name, args=(), in_specs=[], out_specs=[], out_shape=[], rider=rider)[1]


def _all_reduce_small(part):
    def body(p_ref, o_ref, slots, send_sems, recv_sems):
        me = _me()
        my_idx = _linear(me)
        slots[my_idx] = p_ref[...]
        sends = []
        for rel in range(1, N_DEV):
            cp = pltpu.make_async_remote_copy(
                src_ref=p_ref, dst_ref=slots.at[my_idx], send_sem=send_sems.at[rel - 1],
                recv_sem=recv_sems.at[rel - 1], device_id=_peer(me, rel), device_id_type=MESH_T)
            cp.start()
            sends.append(cp)
        for rel in range(1, N_DEV):
            frm = _peer(me, rel)
            pltpu.make_async_remote_copy(
                src_ref=p_ref, dst_ref=slots.at[_linear(frm)], send_sem=send_sems.at[rel - 1],
                recv_sem=recv_sems.at[rel - 1], device_id=frm, device_id_type=MESH_T).wait_recv()
        for cp in sends:
            cp.wait_send()
        total = slots[0]
        for d in range(1, N_DEV):
            total = total + slots[d]
        o_ref[...] = total

    return pl.pallas_call(
        body, name="all_reduce_small",
        in_specs=[pl.BlockSpec(memory_space=pltpu.VMEM)], out_specs=pl.BlockSpec(memory_space=pltpu.VMEM),
        out_shape=jax.ShapeDtypeStruct(part.shape, F32),
        scratch_shapes=[pltpu.VMEM((N_DEV,) + part.shape, F32), pltpu.SemaphoreType.DMA((7,)),
                        pltpu.SemaphoreType.DMA((7,))],
        compiler_params=_params(),
    )(part)


def _adamw_math(w, g, m, v):
    m = ADAM_B1 * m + (1.0 - ADAM_B1) * g
    v = ADAM_B2 * v + (1.0 - ADAM_B2) * jnp.square(g)
    m_hat = m / (1.0 - ADAM_B1 ** ADAM_STEP)
    v_hat = v / (1.0 - ADAM_B2 ** ADAM_STEP)
    delta = -ADAM_LR * (m_hat / (jnp.sqrt(v_hat) + ADAM_EPS) + ADAM_WD * w)
    return delta, m, v


def _adamw_big(name, shares, w, m, v):
    rows, cols = w.shape
    tr = min(rows, 256)

    def body(s_ref, w_ref, m_ref, v_ref, g_ref, d_ref, nm_ref, nv_ref):
        g = s_ref[0].astype(F32)
        for d in range(1, N_CHIPS):
            g = g + s_ref[d].astype(F32)
        g_ref[...] = g
        d_ref[...], nm_ref[...], nv_ref[...] = _adamw_math(w_ref[...], g, m_ref[...], v_ref[...])

    blk = pl.BlockSpec((tr, cols), lambda i: (i, 0))
    return pl.pallas_call(
        body, name=name, grid=(rows // tr,),
        in_specs=[pl.BlockSpec((N_CHIPS, tr, cols), lambda i: (0, i, 0)), blk, blk, blk],
        out_specs=[blk] * 4, out_shape=[jax.ShapeDtypeStruct(w.shape, F32)] * 4,
        compiler_params=_params(("parallel",)),
    )(shares, w, m, v)


def _adamw_small(quads):
    n = len(quads)

    def body(*refs):
        ins, outs = refs[:4 * n], refs[4 * n:]
        for p in range(n):
            g_ref, w_ref, m_ref, v_ref = ins[4 * p:4 * p + 4]
            d_ref, nm_ref, nv_ref = outs[3 * p:3 * p + 3]
            d_ref[...], nm_ref[...], nv_ref[...] = _adamw_math(w_ref[...], g_ref[...], m_ref[...], v_ref[...])

    flat = [a for quad in quads for a in quad]
    out = pl.pallas_call(
        body, name="adamw_small",
        out_shape=[jax.ShapeDtypeStruct(quad[1].shape, F32) for quad in quads for _ in range(3)],
        compiler_params=_params(),
    )(*flat)
    return [tuple(out[3 * p:3 * p + 3]) for p in range(n)]


def kernel(x, norm1_g, w_in, q_norm_g, k_norm_g, rel_bias, conv_w, conv_b, w_attn_proj, w_conv_proj, w_gate, b_gate, w_out, norm2_g, w_up, w_down, loss_target, m_norm1_g, m_w_in, m_q_norm_g, m_k_norm_g, m_rel_bias, m_conv_w, m_conv_b, m_w_attn_proj, m_w_conv_proj, m_w_gate, m_b_gate, m_w_out, m_norm2_g, m_w_up, m_w_down, v_norm1_g, v_w_in, v_q_norm_g, v_k_norm_g, v_rel_bias, v_conv_w, v_conv_b, v_w_attn_proj, v_w_conv_proj, v_w_gate, v_b_gate, v_w_out, v_norm2_g, v_w_up, v_w_down):
    my_idx = _linear(_me())
    big_w = (w_in, w_attn_proj, w_conv_proj, w_gate, w_out, w_up, w_down)
    big_m = (m_w_in, m_w_attn_proj, m_w_conv_proj, m_w_gate, m_w_out, m_w_up, m_w_down)
    big_v = (v_w_in, v_w_attn_proj, v_w_conv_proj, v_w_gate, v_w_out, v_w_up, v_w_down)
    big_names = ("w_in", "w_attn_proj", "w_conv_proj", "w_gate", "w_out", "w_up", "w_down")

    conv_w_tile = jnp.pad(conv_w, ((0, SUBLANES - conv_w.shape[0]), (0, 0)))
    shards = _cast_shards(big_w)
    w_in_full, conv_w_rows = _exchange("all_gather_w_in", _Gather((shards[0], conv_w_tile), (1, 1)))

    dx, shares, small = _local_step(x[0], loss_target[0], norm1_g, q_norm_g, k_norm_g, rel_bias, conv_w_rows[:3],
                                    conv_b, b_gate, norm2_g, w_in_full, tuple(shards[1:5]), tuple(shards[5:7]), True)

    big_out = [_adamw_big("adamw_" + name, s, w, m, v)
               for name, s, w, m, v in zip(big_names, shares, big_w, big_m, big_v)]

    tot = _all_reduce_small(small)
    g_rel_bias = jnp.concatenate(
        [tot[10:26, :QB][:, ::-1], tot[10:26, QB:2 * QB][:, ::-1], tot[10:26, 2 * QB:2 * QB + 1]], axis=1)
    g_conv_w = lax.dynamic_slice(tot[4:7], (0, my_idx * LANES), (3, LANES))
    small_g = [tot[0:1], tot[1:2, :HEAD_DIM], tot[2:3, :HEAD_DIM], g_rel_bias, g_conv_w, tot[3:4],
               tot[7:9].reshape(1, 2 * D_MODEL), tot[9:10]]
    small_w = (norm1_g, q_norm_g, k_norm_g, rel_bias, conv_w, conv_b, b_gate, norm2_g)
    small_m = (m_norm1_g, m_q_norm_g, m_k_norm_g, m_rel_bias, m_conv_w, m_conv_b, m_b_gate, m_norm2_g)
    small_v = (v_norm1_g, v_q_norm_g, v_k_norm_g, v_rel_bias, v_conv_w, v_conv_b, v_b_gate, v_norm2_g)

    def two_d(a):
        return a.reshape(1, -1) if a.ndim == 1 else a

    small_out = _adamw_small([(g, two_d(w), two_d(m), two_d(v))
                              for g, w, m, v in zip(small_g, small_w, small_m, small_v)])

    order = ("norm1_g", "w_in", "q_norm_g", "k_norm_g", "rel_bias", "conv_w", "conv_b", "w_attn_proj", "w_conv_proj",
             "w_gate", "b_gate", "w_out", "norm2_g", "w_up", "w_down")
    small_names = ("norm1_g", "q_norm_g", "k_norm_g", "rel_bias", "conv_w", "conv_b", "b_gate", "norm2_g")
    res = {}
    for name, (g, d, nm, nv) in zip(big_names, big_out):
        res[name] = (g, d, nm, nv)
    for name, g, w, (d, nm, nv) in zip(small_names, small_g, small_w, small_out):
        res[name] = tuple(a.reshape(w.shape) for a in (g, d, nm, nv))
    loss = tot[26, 0]
    return (loss, dx[None], *[res[n][0] for n in order], *[res[n][1] for n in order],
            *[res[n][2] for n in order], *[res[n][3] for n in order])
```

```python
import functools

import jax
import jax.numpy as jnp
from jax import lax
from jax.experimental import pallas as pl
from jax.experimental.pallas import tpu as pltpu

F32 = jnp.float32
BF16 = jnp.bfloat16

D_MODEL = 1024
N_HEADS = 16
HEAD_DIM = 64
CHUNK = 64
N_PREV_CHUNKS = 8
MAX_REL = 256
D_FF = 4096
EPS = 1e-6
NEG_INF = -1e30
N_DEV = 8

ADAM_LR = 0.001
ADAM_B1 = 0.9
ADAM_B2 = 0.999
ADAM_EPS = 1e-08
ADAM_WD = 0.01
ADAM_STEP = 10

LANES = 128
SUBLANES = 8
VMEM_LIMIT = 56 * 1024 * 1024
QB = 256
KW = 3 * QB
PAIRS = 4
SUB = 128
SLAB = PAIRS * LANES
SKEW = 1024

MESH_T = pl.DeviceIdType.MESH


def _dot(a, b):
    return jnp.dot(a, b, preferred_element_type=F32)


def _dot_nt(a, b):
    return lax.dot_general(a, b, (((1,), (1,)), ((), ())), preferred_element_type=F32)


def _dot_tn(a, b):
    return lax.dot_general(a, b, (((0,), (0,)), ((), ())), preferred_element_type=F32)


def _params(sem=None):
    return pltpu.CompilerParams(dimension_semantics=sem, vmem_limit_bytes=VMEM_LIMIT)


def _resident(shape):
    return pl.BlockSpec(shape, lambda *_: (0,) * len(shape), pipeline_mode=pl.Buffered(1))


def _fold8(v):
    rows, n = v.shape
    return v.reshape(rows // SUBLANES, SUBLANES, n).sum(axis=0)


def _head_sum_matrix():
    r = lax.broadcasted_iota(jnp.int32, (LANES, LANES), 0) // HEAD_DIM
    c = lax.broadcasted_iota(jnp.int32, (LANES, LANES), 1) // HEAD_DIM
    return (r == c).astype(BF16)


def _head_sums(v, e):
    hi = v.astype(BF16)
    lo = (v - hi.astype(F32)).astype(BF16)
    return _dot(hi, e) + _dot(lo, e)


def _in_proj(x, g1, w_in, rider=None):
    t = x.shape[0]
    tm = min(t, 512)
    n_out = w_in.shape[1]

    def body(x_ref, g_ref, w_ref, proj_ref, h_ref):
        xf = x_ref[...]
        r = lax.rsqrt(jnp.mean(xf * xf, axis=-1, keepdims=True) + EPS)
        h = (xf * r * g_ref[...]).astype(BF16)
        h_ref[...] = h
        for k in range(n_out // D_MODEL):
            cols = slice(k * D_MODEL, (k + 1) * D_MODEL)
            proj_ref[:, cols] = _dot(h, w_ref[:, cols]).astype(BF16)

    return _call(
        body, name="in_proj", grid=(t // tm,), args=(x, g1, w_in),
        in_specs=[pl.BlockSpec((tm, D_MODEL), lambda i: (i, 0)),
                  pl.BlockSpec((1, D_MODEL), lambda i: (0, 0)),
                  _resident((D_MODEL, n_out))],
        out_specs=[pl.BlockSpec((tm, n_out), lambda i: (i, 0)),
                   pl.BlockSpec((tm, D_MODEL), lambda i: (i, 0))],
        out_shape=[jax.ShapeDtypeStruct((t, n_out), BF16), jax.ShapeDtypeStruct((t, D_MODEL), BF16)],
        semantics=("parallel",), rider=rider)


def _gate_proj(h, w_g, b_g):
    t = h.shape[0]
    tm = min(t, 1024)

    def body(h_ref, w_ref, b_ref, o_ref):
        o_ref[...] = jax.nn.sigmoid(_dot(h_ref[...], w_ref[...]) + b_ref[...]).astype(BF16)

    return pl.pallas_call(
        body, name="gate_proj", grid=(t // tm, 2),
        in_specs=[pl.BlockSpec((tm, D_MODEL), lambda i, k: (i, 0)),
                  pl.BlockSpec((D_MODEL, D_MODEL), lambda i, k: (0, k)),
                  pl.BlockSpec((1, D_MODEL), lambda i, k: (0, k))],
        out_specs=pl.BlockSpec((tm, D_MODEL), lambda i, k: (i, k)),
        out_shape=jax.ShapeDtypeStruct((t, 2 * D_MODEL), BF16),
        compiler_params=_params(("parallel", "arbitrary")),
    )(h, w_g, b_g)


def _qknorm_fwd(proj, gq, gk):
    t = proj.shape[0]
    tm = min(t, 512)
    scale = HEAD_DIM ** -0.5

    def body(q_ref, k_ref, gq_ref, gk_ref, qn_ref, kn_ref):
        e = _head_sum_matrix()
        for src, g_ref, dst, sc in ((q_ref, gq_ref, qn_ref, scale), (k_ref, gk_ref, kn_ref, 1.0)):
            for s in range(D_MODEL // LANES):
                sl = slice(s * LANES, (s + 1) * LANES)
                xf = src[:, sl].astype(F32)
                r = lax.rsqrt(_head_sums(xf * xf, e) * (1.0 / HEAD_DIM) + EPS)
                dst[:, sl] = (xf * r * g_ref[:, sl] * sc).astype(BF16)

    return pl.pallas_call(
        body, name="qknorm_fwd", grid=(t // tm,),
        in_specs=[pl.BlockSpec((tm, D_MODEL), lambda i: (i, 0)),
                  pl.BlockSpec((tm, D_MODEL), lambda i: (i, 1)),
                  pl.BlockSpec((1, D_MODEL), lambda i: (0, 0)),
                  pl.BlockSpec((1, D_MODEL), lambda i: (0, 0))],
        out_specs=[pl.BlockSpec((tm, D_MODEL), lambda i: (i, 0))] * 2,
        out_shape=[jax.ShapeDtypeStruct((t, D_MODEL), BF16)] * 2,
        compiler_params=_params(("parallel",)),
    )(proj, proj, gq, gk)


def _bias_tiles(rel_bias, rider=None):
    by_dist = jnp.concatenate(
        [rel_bias[:, :2 * MAX_REL], jnp.broadcast_to(rel_bias[:, 2 * MAX_REL:], (N_HEADS, 2 * MAX_REL))], axis=1)
    by_dist = by_dist.reshape(N_HEADS, 1, SKEW)

    def body(f_ref, o_ref):
        jj = lax.broadcasted_iota(jnp.int32, (QB, QB), 0)
        ii = lax.broadcasted_iota(jnp.int32, (QB, QB), 1)
        for w in range(KW // QB):
            pos = jnp.broadcast_to(f_ref[0, :, KW - QB * w:KW - QB * w + QB], (QB, QB))
            neg = jnp.broadcast_to(f_ref[0, :, KW - QB * (w + 1):KW - QB * w], (QB, QB))
            pos = pltpu.roll(pos, 0, 1, stride=1, stride_axis=0)
            neg = pltpu.roll(neg, 0, 1, stride=1, stride_axis=0)
            tile = jnp.where(ii >= jj, pos, neg)
            kc = (jj + QB * w) // CHUNK
            qc = ii // CHUNK
            band = (kc >= qc) & (kc <= qc + N_PREV_CHUNKS)
            o_ref[0, QB * w:QB * (w + 1), :] = jnp.where(band, tile, NEG_INF)

    return _call(
        body, name="bias_tiles", grid=(N_HEADS,), args=(by_dist,),
        in_specs=[pl.BlockSpec((1, 1, SKEW), lambda h: (h, 0, 0))],
        out_specs=[pl.BlockSpec((1, KW, QB), lambda h: (h, 0, 0))],
        out_shape=[jax.ShapeDtypeStruct((N_HEADS, KW, QB), F32)],
        semantics=("parallel",), rider=rider)


def _window_specs(col0):
    return [pl.BlockSpec((QB, SLAB), functools.partial(
        lambda p, b, back: (jnp.maximum(b - back, 0), col0 + p), back=back)) for back in (2, 1, 0)]


def _attn_fwd(qn, kn, proj, bias, rider=None):
    t = qn.shape[0]
    nb = t // QB
    v_col0 = 2 * D_MODEL // SLAB

    def body(q_ref, k0, k1, k2, v0, v1, v2, bias_ref, o_ref, lse_ref):
        b = pl.program_id(1)
        head_a = lax.broadcasted_iota(jnp.int32, (1, LANES), 1) < HEAD_DIM
        rows_a = lax.broadcasted_iota(jnp.int32, (LANES, 1), 0) < HEAD_DIM
        valid = lax.broadcasted_iota(jnp.int32, (KW, 1), 0) >= (2 - b) * QB
        def scores(head):
            hp, hh = divmod(head, 2)
            sl = slice(hp * LANES, (hp + 1) * LANES)
            k = jnp.concatenate([k0[:, sl], k1[:, sl], k2[:, sl]], axis=0)
            mine = head_a if hh == 0 else jnp.logical_not(head_a)
            s = _dot_nt(jnp.where(mine, k, jnp.zeros_like(k)), q_ref[:, sl]) + bias_ref[head]
            return jnp.where(valid, s, NEG_INF)

        def weighted_values(head, s):
            hp, hh = divmod(head, 2)
            sl = slice(hp * LANES, (hp + 1) * LANES)
            v = jnp.concatenate([v0[:, sl], v1[:, sl], v2[:, sl]], axis=0)
            vt = v.astype(F32).T.astype(BF16)[hh * HEAD_DIM:(hh + 1) * HEAD_DIM]
            m = jnp.max(s, axis=0, keepdims=True)
            p = jnp.exp(s - m)
            l = jnp.sum(p, axis=0, keepdims=True)
            return _dot(vt, p.astype(BF16)) / l, m + jnp.log(l)

        outs, lses = [], []
        pending = scores(0)
        for head in range(2 * PAIRS):
            nxt = scores(head + 1) if head + 1 < 2 * PAIRS else None
            o, lse = weighted_values(head, pending)
            outs.append(o)
            lses.append(lse)
            pending = nxt
        for hp in range(PAIRS):
            sl = slice(hp * LANES, (hp + 1) * LANES)
            o_ref[:, sl] = jnp.concatenate([outs[2 * hp], outs[2 * hp + 1]], axis=0).T.astype(BF16)
        lse_ref[...] = jnp.concatenate(lses, axis=0)

    return _call(
        body, name="attn_fwd", grid=(D_MODEL // SLAB, nb), args=(qn, kn, kn, kn, proj, proj, proj, bias),
        in_specs=[pl.BlockSpec((QB, SLAB), lambda p, b: (b, p))] + _window_specs(0) + _window_specs(v_col0)
        + [pl.BlockSpec((2 * PAIRS, KW, QB), lambda p, b: (p, 0, 0))],
        out_specs=[pl.BlockSpec((QB, SLAB), lambda p, b: (b, p)),
                   pl.BlockSpec((2 * PAIRS, QB), lambda p, b: (p, b))],
        out_shape=[jax.ShapeDtypeStruct((t, D_MODEL), BF16), jax.ShapeDtypeStruct((N_HEADS, t), F32)],
        semantics=("parallel", "arbitrary"), rider=rider)


def _shift_down(u, halo, n):
    rows = lax.broadcasted_iota(jnp.int32, (u.shape[0], 1), 0)
    out = pltpu.roll(u, n, 0)
    for j in range(n):
        out = jnp.where(rows == j, halo[SUBLANES - n + j:SUBLANES - n + j + 1, :], out)
    return out


def _shift_up(u, halo, n):
    tm = u.shape[0]
    rows = lax.broadcasted_iota(jnp.int32, (tm, 1), 0)
    out = pltpu.roll(u, tm - n, 0)
    for j in range(n):
        out = jnp.where(rows == tm - n + j, halo[j:j + 1, :], out)
    return out


def _conv_fwd(proj, conv_w, conv_b):
    t = proj.shape[0]
    tm = min(t, 512)
    hb = tm // SUBLANES

    def body(bg_ref, cg_ref, xc_ref, cgh_ref, xch_ref, w_ref, b_ref, o_ref):
        i = pl.program_id(0)
        u = cg_ref[...].astype(F32) * xc_ref[...].astype(F32)
        halo = cgh_ref[...].astype(F32) * xch_ref[...].astype(F32)
        halo = jnp.where(i > 0, halo, 0.0)
        w = w_ref[...]
        s = w[0:1] * _shift_down(u, halo, 2) + w[1:2] * _shift_down(u, halo, 1) + w[2:3] * u
        o_ref[...] = (bg_ref[...].astype(F32) * (b_ref[...] + s)).astype(BF16)

    def prev(col):
        return pl.BlockSpec((SUBLANES, D_MODEL), lambda i: (jnp.maximum(i * hb - 1, 0), col))

    return pl.pallas_call(
        body, name="conv_fwd", grid=(t // tm,),
        in_specs=[pl.BlockSpec((tm, D_MODEL), lambda i: (i, 3)),
                  pl.BlockSpec((tm, D_MODEL), lambda i: (i, 4)),
                  pl.BlockSpec((tm, D_MODEL), lambda i: (i, 5)),
                  prev(4), prev(5),
                  pl.BlockSpec((3, D_MODEL), lambda i: (0, 0)),
                  pl.BlockSpec((1, D_MODEL), lambda i: (0, 0))],
        out_specs=pl.BlockSpec((tm, D_MODEL), lambda i: (i, 0)),
        out_shape=jax.ShapeDtypeStruct((t, D_MODEL), BF16),
        compiler_params=_params(("parallel",)),
    )(proj, proj, proj, proj, proj, conv_w, conv_b)


def _mix_out(y_attn, y_conv, gates, x, w_ap, w_cp, w_out, g2):
    t = x.shape[0]
    tm = min(t, 512)

    def body(ya_in, yc_in, g_ref, x_ref, wap, wcp, wout, g2_ref, ya_ref, yc_ref, mg_ref, x1_ref, h2_ref):
        ya = _dot(ya_in[...], wap[...])
        yc = _dot(yc_in[...], wcp[...])
        ya_ref[...] = ya.astype(BF16)
        yc_ref[...] = yc.astype(BF16)
        merged = (g_ref[:, :D_MODEL].astype(F32) * ya + g_ref[:, D_MODEL:].astype(F32) * yc).astype(BF16)
        mg_ref[...] = merged
        x1 = x_ref[...] + _dot(merged, wout[...])
        x1_ref[...] = x1
        r = lax.rsqrt(jnp.mean(x1 * x1, axis=-1, keepdims=True) + EPS)
        h2_ref[...] = (x1 * r * g2_ref[...]).astype(BF16)

    row = pl.BlockSpec((tm, D_MODEL), lambda i: (i, 0))
    full = _resident((D_MODEL, D_MODEL))
    return pl.pallas_call(
        body, name="mix_out", grid=(t // tm,),
        in_specs=[row, row, pl.BlockSpec((tm, 2 * D_MODEL), lambda i: (i, 0)), row, full, full, full,
                  pl.BlockSpec((1, D_MODEL), lambda i: (0, 0))],
        out_specs=[row] * 5,
        out_shape=[jax.ShapeDtypeStruct((t, D_MODEL), BF16)] * 3
        + [jax.ShapeDtypeStruct((t, D_MODEL), F32), jax.ShapeDtypeStruct((t, D_MODEL), BF16)],
        compiler_params=_params(("parallel",)),
    )(y_attn, y_conv, gates, x, w_ap, w_cp, w_out, g2)


def _mlp_fwd(h2, w_up, w_down, x1, target):
    t = h2.shape[0]
    tm = min(t, 512)
    tf = 1024
    nf = D_FF // tf

    def body(h2_ref, wup, wdn, x1_ref, tg_ref, a_ref, dy_ref, dyb_ref, loss_ref):
        h2v = h2_ref[...]
        acc = None
        pending = _dot(h2v, wup[:, 0:tf])
        for j in range(nf):
            cols = slice(j * tf, (j + 1) * tf)
            a = pending
            if j + 1 < nf:
                pending = _dot(h2v, wup[:, (j + 1) * tf:(j + 2) * tf])
            a_ref[:, cols] = a.astype(BF16)
            part = _dot(jnp.square(jnp.maximum(a, 0.0)).astype(BF16), wdn[cols, :])
            acc = part if acc is None else acc + part

        @pl.when(pl.program_id(0) == 0)
        def _():
            loss_ref[...] = jnp.zeros_like(loss_ref)

        diff = x1_ref[...] + acc - tg_ref[...]
        loss_ref[...] += _fold8(diff * diff)
        dy = diff * (1.0 / D_MODEL)
        dy_ref[...] = dy
        dyb_ref[...] = dy.astype(BF16)

    row = pl.BlockSpec((tm, D_MODEL), lambda i: (i, 0))
    return pl.pallas_call(
        body, name="mlp_fwd", grid=(t // tm,),
        in_specs=[row, _resident((D_MODEL, D_FF)), _resident((D_FF, D_MODEL)), row, row],
        out_specs=[pl.BlockSpec((tm, D_FF), lambda i: (i, 0)), row, row,
                   pl.BlockSpec((SUBLANES, D_MODEL), lambda i: (0, 0))],
        out_shape=[jax.ShapeDtypeStruct((t, D_FF), BF16), jax.ShapeDtypeStruct((t, D_MODEL), F32),
                   jax.ShapeDtypeStruct((t, D_MODEL), BF16), jax.ShapeDtypeStruct((SUBLANES, D_MODEL), F32)],
        compiler_params=_params(("arbitrary",)),
    )(h2, w_up, w_down, x1, target)


def _rmsnorm_bwd(xf, g, dh):
    r = lax.rsqrt(jnp.mean(xf * xf, axis=-1, keepdims=True) + EPS)
    xh = xf * r
    dxh = dh * g
    dx = r * (dxh - xh * jnp.mean(dxh * xh, axis=-1, keepdims=True))
    return dx, dh * xh


def _mlp_bwd(dyb, a, w_down, w_up, x1, dy, g2):
    t = dyb.shape[0]
    tm = min(t, 512)
    tf = 1024
    nf = D_FF // tf

    def body(dyb_ref, a_ref, wdn, wup, x1_ref, dy_ref, g2_ref, da_ref, dx1_ref, dx1b_ref, dg2_ref):
        dyv = dyb_ref[...]
        acc = None
        pending = _dot_nt(dyv, wdn[0:tf, :])
        for j in range(nf):
            cols = slice(j * tf, (j + 1) * tf)
            du = pending
            if j + 1 < nf:
                pending = _dot_nt(dyv, wdn[(j + 1) * tf:(j + 2) * tf, :])
            da = (du * (2.0 * jnp.maximum(a_ref[:, cols].astype(F32), 0.0))).astype(BF16)
            da_ref[:, cols] = da
            part = _dot_nt(da, wup[:, cols])
            acc = part if acc is None else acc + part

        @pl.when(pl.program_id(0) == 0)
        def _():
            dg2_ref[...] = jnp.zeros_like(dg2_ref)

        dx, dg = _rmsnorm_bwd(x1_ref[...], g2_ref[...], acc)
        dx1 = dy_ref[...] + dx
        dx1_ref[...] = dx1
        dx1b_ref[...] = dx1.astype(BF16)
        dg2_ref[...] += _fold8(dg)

    row = pl.BlockSpec((tm, D_MODEL), lambda i: (i, 0))
    wide = pl.BlockSpec((tm, D_FF), lambda i: (i, 0))
    return pl.pallas_call(
        body, name="mlp_bwd", grid=(t // tm,),
        in_specs=[row, wide, _resident((D_FF, D_MODEL)), _resident((D_MODEL, D_FF)), row, row,
                  pl.BlockSpec((1, D_MODEL), lambda i: (0, 0))],
        out_specs=[wide, row, row, pl.BlockSpec((SUBLANES, D_MODEL), lambda i: (0, 0))],
        out_shape=[jax.ShapeDtypeStruct((t, D_FF), BF16), jax.ShapeDtypeStruct((t, D_MODEL), F32),
                   jax.ShapeDtypeStruct((t, D_MODEL), BF16), jax.ShapeDtypeStruct((SUBLANES, D_MODEL), F32)],
        compiler_params=_params(("arbitrary",)),
    )(dyb, a, w_down, w_up, x1, dy, g2)


def _wgrad(name, lhs, rhs_list, rhs_slabs, relu_sq=False):
    t, m = lhs.shape
    tt = min(t, 2048)
    tmo = min(m, 1024)
    n_slab = sum(rhs_slabs)
    starts = [sum(rhs_slabs[:n]) for n in range(len(rhs_slabs))]
    n_rhs = len(rhs_list)

    def body(*refs):
        l_ref, r_refs, o_ref, acc = refs[0], refs[1:1 + n_rhs], refs[1 + n_rhs], refs[2 + n_rhs]
        k, s = pl.program_id(1), pl.program_id(2)
        lv = l_ref[...]
        if relu_sq:
            lv = jnp.square(jnp.maximum(lv.astype(F32), 0.0)).astype(BF16)

        @pl.when(s == 0)
        def _():
            acc[...] = jnp.zeros_like(acc)

        for n in range(n_rhs):
            @pl.when((k >= starts[n]) & (k < starts[n] + rhs_slabs[n]))
            def _(n=n):
                acc[...] += _dot_tn(lv, r_refs[n][...])

        @pl.when(s == pl.num_programs(2) - 1)
        def _():
            o_ref[...] = acc[...].astype(BF16)

    def rhs_spec(n):
        lo, cnt = starts[n], rhs_slabs[n]

        def index(i, k, s):
            inside = (k >= lo) & (k < lo + cnt)
            return (jnp.where(inside, s, 0), jnp.clip(k - lo, 0, cnt - 1))
        return pl.BlockSpec((tt, D_MODEL), index)

    return pl.pallas_call(
        body, name=name, grid=(m // tmo, n_slab, t // tt),
        in_specs=[pl.BlockSpec((tt, tmo), lambda i, k, s: (s, i))] + [rhs_spec(n) for n in range(n_rhs)],
        out_specs=pl.BlockSpec((tmo, D_MODEL), lambda i, k, s: (i, k)),
        out_shape=jax.ShapeDtypeStruct((m, n_slab * D_MODEL), BF16),
        scratch_shapes=[pltpu.VMEM((tmo, D_MODEL), F32)],
        compiler_params=_params(("parallel", "parallel", "arbitrary")),
    )(lhs, *rhs_list)


def _wgrad_group(name, triples):
    t = triples[0][0].shape[0]
    tt = min(t, 1024)
    counts = [n for _, _, n in triples]
    starts = [sum(counts[:n]) for n in range(len(counts))]
    n_prod = len(triples)

    def inside(n, k):
        return (k >= starts[n]) & (k < starts[n] + counts[n])

    def body(*refs):
        l_refs, r_refs, o_refs = refs[:n_prod], refs[n_prod:2 * n_prod], refs[2 * n_prod:3 * n_prod]
        acc = refs[3 * n_prod]
        k, s = pl.program_id(0), pl.program_id(1)

        @pl.when(s == 0)
        def _():
            acc[...] = jnp.zeros_like(acc)

        for n in range(n_prod):
            @pl.when(inside(n, k))
            def _(n=n):
                acc[...] += _dot_tn(l_refs[n][...], r_refs[n][...])

            @pl.when(inside(n, k) & (s == pl.num_programs(1) - 1))
            def _(n=n):
                o_refs[n][...] = acc[...].astype(BF16)

    def lhs_spec(n):
        return pl.BlockSpec((tt, D_MODEL), lambda k, s: (jnp.where(inside(n, k), s, 0), 0))

    def rhs_spec(n):
        return pl.BlockSpec((tt, D_MODEL), lambda k, s: (jnp.where(inside(n, k), s, 0),
                                                         jnp.clip(k - starts[n], 0, counts[n] - 1)))

    def out_spec(n):
        return pl.BlockSpec((D_MODEL, D_MODEL), lambda k, s: (0, jnp.clip(k - starts[n], 0, counts[n] - 1)))

    return pl.pallas_call(
        body, name=name, grid=(sum(counts), t // tt),
        in_specs=[lhs_spec(n) for n in range(n_prod)] + [rhs_spec(n) for n in range(n_prod)],
        out_specs=[out_spec(n) for n in range(n_prod)],
        out_shape=[jax.ShapeDtypeStruct((D_MODEL, n * D_MODEL), BF16) for n in counts],
        scratch_shapes=[pltpu.VMEM((D_MODEL, D_MODEL), F32)],
        compiler_params=_params(("arbitrary", "arbitrary")),
    )(*[tr[0] for tr in triples], *[tr[1] for tr in triples])


def _mix_bwd(dx1b, gates, ya, yc, w_out, w_ap, w_cp, w_g, rider=None):
    t = dx1b.shape[0]
    tm = min(t, 512)

    def body(dx_ref, g_ref, ya_ref, yc_ref, wout, wap, wcp, wg,
             dgp_ref, dya_ref, dyc_ref, dyat_ref, dycv_ref, dhg_ref, dbg_ref):
        dm = _dot_nt(dx_ref[...], wout[...])
        ga = g_ref[:, :D_MODEL].astype(F32)
        gc = g_ref[:, D_MODEL:].astype(F32)
        dya = (dm * ga).astype(BF16)
        dyc = (dm * gc).astype(BF16)
        dya_ref[...] = dya
        dyc_ref[...] = dyc
        dgpa = dm * ya_ref[...].astype(F32) * ga * (1.0 - ga)
        dgpc = dm * yc_ref[...].astype(F32) * gc * (1.0 - gc)

        @pl.when(pl.program_id(0) == 0)
        def _():
            dbg_ref[...] = jnp.zeros_like(dbg_ref)

        dbg_ref[:, :D_MODEL] += _fold8(dgpa)
        dbg_ref[:, D_MODEL:] += _fold8(dgpc)
        dgpa = dgpa.astype(BF16)
        dgpc = dgpc.astype(BF16)
        dgp_ref[:, :D_MODEL] = dgpa
        dgp_ref[:, D_MODEL:] = dgpc
        dyat_ref[...] = _dot_nt(dya, wap[...]).astype(BF16)
        dycv_ref[...] = _dot_nt(dyc, wcp[...]).astype(BF16)
        dhg_ref[...] = _dot_nt(dgpa, wg[:, :D_MODEL]) + _dot_nt(dgpc, wg[:, D_MODEL:])

    row = pl.BlockSpec((tm, D_MODEL), lambda i: (i, 0))
    row2 = pl.BlockSpec((tm, 2 * D_MODEL), lambda i: (i, 0))
    full = _resident((D_MODEL, D_MODEL))
    return _call(
        body, name="mix_bwd", grid=(t // tm,), args=(dx1b, gates, ya, yc, w_out, w_ap, w_cp, w_g),
        in_specs=[row, row2, row, row, full, full, full, _resident((D_MODEL, 2 * D_MODEL))],
        out_specs=[row2, row, row, row, row, row, pl.BlockSpec((SUBLANES, 2 * D_MODEL), lambda i: (0, 0))],
        out_shape=[jax.ShapeDtypeStruct((t, 2 * D_MODEL), BF16)] + [jax.ShapeDtypeStruct((t, D_MODEL), BF16)] * 4
        + [jax.ShapeDtypeStruct((t, D_MODEL), F32), jax.ShapeDtypeStruct((SUBLANES, 2 * D_MODEL), F32)],
        semantics=("arbitrary",), rider=rider)


def _conv_bwd(dyconv, proj, conv_w, conv_b, rider=None):
    t = proj.shape[0]
    tm = min(t, 512)
    hb = tm // SUBLANES
    last = t // SUBLANES - 1

    def body(dy_ref, dyn_ref, bg_ref, bgn_ref, cg_ref, cgp_ref, xc_ref, xcp_ref, w_ref, b_ref,
             o_ref, dcb_ref, dcw_ref):
        i = pl.program_id(0)
        cg = cg_ref[...].astype(F32)
        xc = xc_ref[...].astype(F32)
        bg = bg_ref[...].astype(F32)
        u = cg * xc
        prev = jnp.where(i > 0, cgp_ref[...].astype(F32) * xcp_ref[...].astype(F32), 0.0)
        u1 = _shift_down(u, prev, 1)
        u2 = _shift_down(u, prev, 2)
        w = w_ref[...]
        conv = b_ref[...] + (w[0:1] * u2 + w[1:2] * u1 + w[2:3] * u)
        dy = dy_ref[...].astype(F32)
        dconv = dy * bg
        nxt = jnp.where(i < pl.num_programs(0) - 1, dyn_ref[...].astype(F32) * bgn_ref[...].astype(F32), 0.0)
        du = w[2:3] * dconv + w[1:2] * _shift_up(dconv, nxt, 1) + w[0:1] * _shift_up(dconv, nxt, 2)
        o_ref[:, :D_MODEL] = (dy * conv).astype(BF16)
        o_ref[:, D_MODEL:2 * D_MODEL] = (du * xc).astype(BF16)
        o_ref[:, 2 * D_MODEL:] = (du * cg).astype(BF16)

        @pl.when(i == 0)
        def _():
            dcb_ref[...] = jnp.zeros_like(dcb_ref)
            dcw_ref[...] = jnp.zeros_like(dcw_ref)

        dcb_ref[...] += _fold8(dconv)
        dcw_ref[0:SUBLANES] += _fold8(dconv * u2)
        dcw_ref[SUBLANES:2 * SUBLANES] += _fold8(dconv * u1)
        dcw_ref[2 * SUBLANES:] += _fold8(dconv * u)

    def prev(col):
        return pl.BlockSpec((SUBLANES, D_MODEL), lambda i: (jnp.maximum(i * hb - 1, 0), col))

    def nxt(col):
        return pl.BlockSpec((SUBLANES, D_MODEL), lambda i: (jnp.minimum((i + 1) * hb, last), col))

    def cur(col):
        return pl.BlockSpec((tm, D_MODEL), lambda i: (i, col))

    return _call(
        body, name="conv_bwd", grid=(t // tm,),
        args=(dyconv, dyconv, proj, proj, proj, proj, proj, proj, conv_w, conv_b),
        in_specs=[cur(0), nxt(0), cur(3), nxt(3), cur(4), prev(4), cur(5), prev(5),
                  pl.BlockSpec((3, D_MODEL), lambda i: (0, 0)), pl.BlockSpec((1, D_MODEL), lambda i: (0, 0))],
        out_specs=[pl.BlockSpec((tm, 3 * D_MODEL), lambda i: (i, 0)),
                   pl.BlockSpec((SUBLANES, D_MODEL), lambda i: (0, 0)),
                   pl.BlockSpec((3 * SUBLANES, D_MODEL), lambda i: (0, 0))],
        out_shape=[jax.ShapeDtypeStruct((t, 3 * D_MODEL), BF16), jax.ShapeDtypeStruct((SUBLANES, D_MODEL), F32),
                   jax.ShapeDtypeStruct((3 * SUBLANES, D_MODEL), F32)],
        semantics=("arbitrary",), rider=rider)


def _attn_bwd(qn, kn, proj, dyattn, y_attn, lse, bias, rider=None):
    t = qn.shape[0]
    nb = t // QB
    v_col0 = 2 * D_MODEL // SLAB

    def body(q_ref, k0, k1, k2, v0, v1, v2, do_ref, o_ref, lse_ref, bias_ref,
             dq_ref, dk_ref, dv_ref, db_ref, acck, accv):
        b = pl.program_id(1)

        @pl.when(b == 0)
        def _():
            acck[...] = jnp.zeros_like(acck)
            accv[...] = jnp.zeros_like(accv)
            db_ref[...] = jnp.zeros_like(db_ref)

        @pl.when(b < nb)
        def _():
            head_a = lax.broadcasted_iota(jnp.int32, (1, LANES), 1) < HEAD_DIM
            valid = lax.broadcasted_iota(jnp.int32, (KW, 1), 0) >= (2 - b) * QB

            def window(refs, hp):
                sl = slice(hp * LANES, (hp + 1) * LANES)
                return jnp.concatenate([r[:, sl] for r in refs], axis=0)

            def transposed(x, hh):
                return x.astype(F32).T.astype(BF16)[hh * HEAD_DIM:(hh + 1) * HEAD_DIM]

            def probs(head):
                hp, hh = divmod(head, 2)
                sl = slice(hp * LANES, (hp + 1) * LANES)
                mine = head_a if hh == 0 else jnp.logical_not(head_a)
                k = window((k0, k1, k2), hp)
                s = _dot_nt(jnp.where(mine, k, jnp.zeros_like(k)), q_ref[:, sl]) + bias_ref[head]
                return jnp.exp(jnp.where(valid, s, NEG_INF) - lse_ref[head:head + 1, :])

            def grads(head, p):
                hp, hh = divmod(head, 2)
                sl = slice(hp * LANES, (hp + 1) * LANES)
                rows = slice(hh * HEAD_DIM, (hh + 1) * HEAD_DIM)
                mine = head_a if hh == 0 else jnp.logical_not(head_a)
                do = do_ref[:, sl]
                v = window((v0, v1, v2), hp)
                delta = jnp.sum((do.astype(F32).T * o_ref[:, sl].astype(F32).T)[rows], axis=0, keepdims=True)
                ds = p * (_dot_nt(jnp.where(mine, v, jnp.zeros_like(v)), do) - delta)
                db_ref[head] += ds
                pb, dsb = p.astype(BF16), ds.astype(BF16)
                dvt = _dot_nt(transposed(do, hh), pb)
                dkt = _dot_nt(transposed(q_ref[:, sl], hh), dsb)
                dqt = _dot(transposed(window((k0, k1, k2), hp), hh), dsb)
                return dqt, dkt, dvt

            out = []
            pending = probs(0)
            for head in range(2 * PAIRS):
                nxt = probs(head + 1) if head + 1 < 2 * PAIRS else None
                out.append(grads(head, pending))
                pending = nxt
            for hp in range(PAIRS):
                sl = slice(hp * LANES, (hp + 1) * LANES)
                dqt, dkt, dvt = (jnp.concatenate([out[2 * hp][n], out[2 * hp + 1][n]], axis=0) for n in range(3))
                dq_ref[:, sl] = dqt.T
                for w in range(3):
                    slot = lax.rem(b + w + 1, 3)
                    cols = slice(w * QB, (w + 1) * QB)
                    if w == 2:
                        acck[hp, slot] = dkt[:, cols]
                        accv[hp, slot] = dvt[:, cols]
                    else:
                        acck[hp, slot] += dkt[:, cols]
                        accv[hp, slot] += dvt[:, cols]

        done = lax.rem(b + 1, 3)
        for hp in range(PAIRS):
            sl = slice(hp * LANES, (hp + 1) * LANES)
            dk_ref[:, sl] = acck[hp, done].T
            dv_ref[:, sl] = accv[hp, done].T.astype(BF16)

    def cur(p, b):
        return (jnp.minimum(b, nb - 1), p)

    def window(col0):
        return [pl.BlockSpec((QB, SLAB), functools.partial(
            lambda p, b, back: (jnp.maximum(jnp.minimum(b, nb - 1) - back, 0), col0 + p), back=back))
            for back in (2, 1, 0)]

    def done_block(p, b):
        return (jnp.maximum(b - 2, 0), p)

    tile = pl.BlockSpec((2 * PAIRS, KW, QB), lambda p, b: (p, 0, 0))
    here = pl.BlockSpec((QB, SLAB), cur)
    return _call(
        body, name="attn_bwd", grid=(D_MODEL // SLAB, nb + 2),
        args=(qn, kn, kn, kn, proj, proj, proj, dyattn, y_attn, lse, bias),
        in_specs=[here] + window(0) + window(v_col0)
        + [here, here, pl.BlockSpec((2 * PAIRS, QB), lambda p, b: (p, jnp.minimum(b, nb - 1))), tile],
        out_specs=[here, pl.BlockSpec((QB, SLAB), done_block), pl.BlockSpec((QB, SLAB), done_block), tile],
        out_shape=[jax.ShapeDtypeStruct((t, D_MODEL), F32), jax.ShapeDtypeStruct((t, D_MODEL), F32),
                   jax.ShapeDtypeStruct((t, D_MODEL), BF16), jax.ShapeDtypeStruct((N_HEADS, KW, QB), F32)],
        scratch_shapes=[pltpu.VMEM((PAIRS, 3, LANES, QB), F32), pltpu.VMEM((PAIRS, 3, LANES, QB), F32)],
        semantics=("parallel", "arbitrary"), rider=rider)


def _qknorm_bwd(proj, dqn, dkn, gq, gk):
    t = proj.shape[0]
    tm = min(t, 512)
    scale = HEAD_DIM ** -0.5

    def body(q_ref, k_ref, dqn_ref, dkn_ref, gq_ref, gk_ref, o_ref, dgq_ref, dgk_ref):
        e = _head_sum_matrix()

        @pl.when(pl.program_id(0) == 0)
        def _():
            dgq_ref[...] = jnp.zeros_like(dgq_ref)
            dgk_ref[...] = jnp.zeros_like(dgk_ref)

        for n, (src, dn_ref, g_ref, dg_ref, sc) in enumerate(
                ((q_ref, dqn_ref, gq_ref, dgq_ref, scale), (k_ref, dkn_ref, gk_ref, dgk_ref, 1.0))):
            for s in range(D_MODEL // LANES):
                sl = slice(s * LANES, (s + 1) * LANES)
                xf = src[:, sl].astype(F32)
                r = lax.rsqrt(_head_sums(xf * xf, e) * (1.0 / HEAD_DIM) + EPS)
                xh = xf * r
                dn = dn_ref[:, sl] * sc
                dg_ref[:, sl] += _fold8(dn * xh)
                dxh = dn * g_ref[:, sl]
                mean = _head_sums(dxh * xh, e) * (1.0 / HEAD_DIM)
                o_ref[:, n * D_MODEL + s * LANES:n * D_MODEL + (s + 1) * LANES] = (r * (dxh - xh * mean)).astype(BF16)

    row = pl.BlockSpec((tm, D_MODEL), lambda i: (i, 0))
    vec = pl.BlockSpec((1, D_MODEL), lambda i: (0, 0))
    acc = pl.BlockSpec((SUBLANES, D_MODEL), lambda i: (0, 0))
    return pl.pallas_call(
        body, name="qknorm_bwd", grid=(t // tm,),
        in_specs=[row, pl.BlockSpec((tm, D_MODEL), lambda i: (i, 1)), row, row, vec, vec],
        out_specs=[pl.BlockSpec((tm, 2 * D_MODEL), lambda i: (i, 0)), acc, acc],
        out_shape=[jax.ShapeDtypeStruct((t, 2 * D_MODEL), BF16)] + [jax.ShapeDtypeStruct((SUBLANES, D_MODEL), F32)] * 2,
        compiler_params=_params(("arbitrary",)),
    )(proj, proj, dqn, dkn, gq, gk)


def _in_bwd(dqk, dv, dconv, w_in, dhg, x, g1, dx1, rider=None):
    t = x.shape[0]
    tm = min(t, 512)

    def body(dqk_ref, dv_ref, dc_ref, w_ref, dhg_ref, x_ref, g_ref, dx1_ref, dx_ref, dg_ref):
        acc = dhg_ref[...]
        slab = 0
        for src, n in ((dqk_ref, 2), (dv_ref, 1), (dc_ref, 3)):
            for s in range(n):
                acc = acc + _dot_nt(src[:, s * D_MODEL:(s + 1) * D_MODEL],
                                    w_ref[:, slab * D_MODEL:(slab + 1) * D_MODEL])
                slab += 1

        @pl.when(pl.program_id(0) == 0)
        def _():
            dg_ref[...] = jnp.zeros_like(dg_ref)

        dx, dg = _rmsnorm_bwd(x_ref[...], g_ref[...], acc)
        dx_ref[...] = dx1_ref[...] + dx
        dg_ref[...] += _fold8(dg)

    row = pl.BlockSpec((tm, D_MODEL), lambda i: (i, 0))
    return _call(
        body, name="in_bwd", grid=(t // tm,), args=(dqk, dv, dconv, w_in, dhg, x, g1, dx1),
        in_specs=[pl.BlockSpec((tm, 2 * D_MODEL), lambda i: (i, 0)), row,
                  pl.BlockSpec((tm, 3 * D_MODEL), lambda i: (i, 0)),
                  _resident(w_in.shape), row, row, pl.BlockSpec((1, D_MODEL), lambda i: (0, 0)), row],
        out_specs=[row, pl.BlockSpec((SUBLANES, D_MODEL), lambda i: (0, 0))],
        out_shape=[jax.ShapeDtypeStruct((t, D_MODEL), F32), jax.ShapeDtypeStruct((SUBLANES, D_MODEL), F32)],
        semantics=("arbitrary",), rider=rider)


def _bias_grad_fold(dbias):
    def body(d_ref, o_ref):
        jj = lax.broadcasted_iota(jnp.int32, (QB, QB), 0)
        ii = lax.broadcasted_iota(jnp.int32, (QB, QB), 1)
        flip = (jj + ii == QB - 1).astype(BF16)
        low = jj + ii < QB
        pos, neg = [], []
        for w in range(KW // QB):
            x = d_ref[0, QB * w:QB * (w + 1), :]
            hi = x.astype(BF16)
            r1 = x - hi.astype(F32)
            mid = r1.astype(BF16)
            lo = (r1 - mid.astype(F32)).astype(BF16)
            xr = _dot(hi, flip) + _dot(mid, flip) + _dot(lo, flip)
            for keep, acc in ((low, pos), (jnp.logical_not(low), neg)):
                part = pltpu.roll(jnp.where(keep, xr, 0.0), 0, 1, stride=1, stride_axis=0)
                acc.append(jnp.sum(part, axis=0, keepdims=True))
        far = pos[1] + neg[0] + pos[0]
        o_ref[0] = jnp.zeros((SUBLANES, QB), F32)
        o_ref[0, 0:1, :] = neg[2]
        o_ref[0, 1:2, :] = pos[2] + neg[1]
        o_ref[0, 2:3, :] = jnp.broadcast_to(jnp.sum(far, axis=-1, keepdims=True), (1, QB))

    return pl.pallas_call(
        body, name="bias_grad_fold", grid=(N_HEADS,),
        in_specs=[pl.BlockSpec((1, KW, QB), lambda h: (h, 0, 0))],
        out_specs=pl.BlockSpec((1, SUBLANES, QB), lambda h: (h, 0, 0)),
        out_shape=jax.ShapeDtypeStruct((N_HEADS, SUBLANES, QB), F32),
        compiler_params=_params(("parallel",)),
    )(dbias)


def _small_partials(dg1, dgq, dgk, dcb, dcw, dbg, dg2, dbias_fold, loss_tile):
    def head_fold(v):
        acc = v[:, 0:LANES]
        for s in range(1, D_MODEL // LANES):
            acc = acc + v[:, s * LANES:(s + 1) * LANES]
        return acc + pltpu.roll(acc, HEAD_DIM, 1)

    def body(dg1_ref, dgq_ref, dgk_ref, dcb_ref, dcw_ref, dbg_ref, dg2_ref, db_ref, loss_ref, o_ref):
        o_ref[...] = jnp.zeros_like(o_ref)
        o_ref[0:1, :] = jnp.sum(dg1_ref[...], axis=0, keepdims=True)
        o_ref[1:2, 0:LANES] = head_fold(jnp.sum(dgq_ref[...], axis=0, keepdims=True))
        o_ref[2:3, 0:LANES] = head_fold(jnp.sum(dgk_ref[...], axis=0, keepdims=True))
        o_ref[3:4, :] = jnp.sum(dcb_ref[...], axis=0, keepdims=True)
        for j in range(3):
            o_ref[4 + j:5 + j, :] = jnp.sum(dcw_ref[j * SUBLANES:(j + 1) * SUBLANES, :], axis=0, keepdims=True)
        o_ref[7:8, :] = jnp.sum(dbg_ref[:, :D_MODEL], axis=0, keepdims=True)
        o_ref[8:9, :] = jnp.sum(dbg_ref[:, D_MODEL:], axis=0, keepdims=True)
        o_ref[9:10, :] = jnp.sum(dg2_ref[...], axis=0, keepdims=True)
        for h in range(N_HEADS):
            for part in range(3):
                o_ref[10 + h:11 + h, part * QB:(part + 1) * QB] = db_ref[h, part:part + 1, :]
        loss = (0.5 / D_MODEL) * jnp.sum(jnp.sum(loss_ref[...], axis=0, keepdims=True), axis=-1, keepdims=True)
        o_ref[26:27, :] = jnp.broadcast_to(loss, (1, D_MODEL))

    return pl.pallas_call(
        body, name="small_partials",
        out_shape=jax.ShapeDtypeStruct((32, D_MODEL), F32),
        compiler_params=_params(),
    )(dg1, dgq, dgk, dcb, dcw, dbg, dg2, dbias_fold, loss_tile)


MID_AXES = (0, 0, 1, 0)
MLP_AXES = (1, 0)


def _local_step(x, target, norm1_g, q_norm_g, k_norm_g, bias, conv_w, conv_b, b_gate, norm2_g,
                w_in, mid_w, mlp_w, distributed):
    g1 = norm1_g.reshape(1, D_MODEL)
    g2 = norm2_g.reshape(1, D_MODEL)
    gq = jnp.tile(q_norm_g, N_HEADS).reshape(1, D_MODEL)
    gk = jnp.tile(k_norm_g, N_HEADS).reshape(1, D_MODEL)
    cb = conv_b.reshape(1, D_MODEL)

    (proj, h), got = _in_proj(x, g1, w_in, rider=_Gather(mid_w, MID_AXES) if distributed else None)
    w_ap, w_cp, w_g, w_out = got if distributed else mid_w
    gates = _gate_proj(h, w_g, b_gate.reshape(1, 2 * D_MODEL))
    qn, kn = _qknorm_fwd(proj, gq, gk)
    (y_attn, lse), got = _attn_fwd(qn, kn, proj, bias, rider=_Gather(mlp_w, MLP_AXES) if distributed else None)
    w_up, w_down = got if distributed else mlp_w
    y_conv = _conv_fwd(proj, conv_w, cb)
    ya, yc, merged, x1, h2 = _mix_out(y_attn, y_conv, gates, x, w_ap, w_cp, w_out, g2)
    a, dy, dyb, loss_tile = _mlp_fwd(h2, w_up, w_down, x1, target)

    da, dx1, dx1b, dg2 = _mlp_bwd(dyb, a, w_down, w_up, x1, dy, g2)
    gw_down = _wgrad("wgrad_down", a, [dyb], [1], relu_sq=True)
    gw_up = _wgrad("wgrad_up", h2, [da], [D_FF // D_MODEL])
    (dgp, dya, dyc, dyattn, dyconv, dhg, dbg), mlp_swapped = _mix_bwd(
        dx1b, gates, ya, yc, w_out, w_ap, w_cp, w_g,
        rider=_PairSwap((gw_up, gw_down), MLP_AXES) if distributed else None)
    mid = tuple(_wgrad_group("wgrad_mid", [(y_attn, dya, 1), (y_conv, dyc, 1), (h, dgp, 2), (merged, dx1b, 1)]))
    gw_ap, gw_cp, gw_g, gw_out = mid
    (dconv, dcb, dcw), mid_swapped = _conv_bwd(
        dyconv, proj, conv_w, cb, rider=_PairSwap(mid, MID_AXES) if distributed else None)
    early = mid + (gw_up, gw_down)
    early_sums = (_pair_add(early, tuple(mid_swapped) + tuple(mlp_swapped), MID_AXES + MLP_AXES)
                  if distributed else None)
    (dqn, dkn, dv, dbias), early_shares = _attn_bwd(
        qn, kn, proj, dyattn, y_attn, lse, bias, rider=_ChipScatter(early_sums) if distributed else None)
    dqk, dgq, dgk = _qknorm_bwd(proj, dqn, dkn, gq, gk)
    gw_in = _wgrad("wgrad_in", h, [dqk, dv, dconv], [2, 1, 3])
    in_sums = (_pair_add((gw_in,), _exchange("swap_w_in_grad", _PairSwap((gw_in,), (1,))), (1,))
               if distributed else None)
    (dx, dg1), in_shares = _in_bwd(dqk, dv, dconv, w_in, dhg, x, g1, dx1,
                                   rider=_ChipScatter(in_sums) if distributed else None)
    small = _small_partials(dg1, dgq, dgk, dcb, dcw, dbg, dg2, _bias_grad_fold(dbias), loss_tile)
    grads = tuple(in_shares) + tuple(early_shares) if distributed else (gw_in,) + early
    return dx, grads, small


def _me():
    return lax.axis_index("x"), lax.axis_index("y"), lax.axis_index("c")


def _peer(me, rel):
    x, y, c = me
    return (1 - x if rel & 4 else x, 1 - y if rel & 2 else y, 1 - c if rel & 1 else c)


def _linear(dev):
    return 4 * dev[0] + 2 * dev[1] + dev[2]


BIG_AXES = (1, 0, 0, 1, 0, 1, 0)


def _block(ref, axis, idx, size):
    return ref.at[pl.ds(idx * size, size), :] if axis == 0 else ref.at[:, pl.ds(idx * size, size)]


def _cast_shards(shards):
    def body(*refs):
        for src, dst in zip(refs[:len(shards)], refs[len(shards):]):
            dst[...] = src[...].astype(BF16)

    return pl.pallas_call(
        body, name="cast_shards",
        out_shape=[jax.ShapeDtypeStruct(s.shape, BF16) for s in shards],
        compiler_params=_params(),
    )(*shards)


class _Gather:
    def __init__(self, shards, axes):
        self.arrays, self.axes, self.n = list(shards), tuple(axes), len(shards)
        self.sizes = [s.shape[axis] for s, axis in zip(shards, axes)]
        self.out_shape = []
        for s, axis in zip(shards, axes):
            shape = (s.shape[0] * N_DEV, s.shape[1]) if axis == 0 else (s.shape[0], s.shape[1] * N_DEV)
            self.out_shape.append(jax.ShapeDtypeStruct(shape, s.dtype))
        self.scratch = [pltpu.SemaphoreType.DMA((self.n, 7)), pltpu.SemaphoreType.DMA((self.n, 7)),
                        pltpu.SemaphoreType.DMA((self.n,))]

    def _copies(self, srcs, outs, sems):
        send_sems, recv_sems, local_sems = sems
        me = _me()
        sibling = _peer(me, 1)
        chips = [_peer(me, rel) for rel in (4, 2, 6)]

        def rows(a, dev):
            return _block(outs[a], self.axes[a], _linear(dev), self.sizes[a])

        def copy(a, k, block_dev, to, src=None):
            return pltpu.make_async_remote_copy(
                src_ref=rows(a, block_dev) if src is None else src, dst_ref=rows(a, block_dev),
                send_sem=send_sems.at[a, k], recv_sem=recv_sems.at[a, k], device_id=to, device_id_type=MESH_T)

        own = [pltpu.make_async_copy(srcs[a], rows(a, me), local_sems.at[a]) for a in range(self.n)]
        first = []
        for a in range(self.n):
            first.append(copy(a, 0, me, sibling, src=srcs[a]))
            for j, chip in enumerate(chips):
                first.append(copy(a, 1 + j, me, chip, src=srcs[a]))
        return me, sibling, chips, copy, own, first

    def start(self, srcs, outs, sems):
        _, _, _, _, own, first = self._copies(srcs, outs, sems)
        for cp in own + first:
            cp.start()

    def finish(self, srcs, outs, sems):
        me, sibling, chips, copy, own, first = self._copies(srcs, outs, sems)
        passed = []
        for a in range(self.n):
            for j, chip in enumerate(chips):
                copy(a, 1 + j, chip, me).wait_recv()
                fwd = copy(a, 4 + j, chip, sibling)
                fwd.start()
                passed.append(fwd)
        for a in range(self.n):
            copy(a, 0, sibling, me).wait_recv()
            for j, chip in enumerate(chips):
                copy(a, 4 + j, _peer(chip, 1), me).wait_recv()
        for cp in first + passed:
            cp.wait_send()
        for cp in own:
            cp.wait()


N_CHIPS = 4


def _shard_shape(g, axis):
    return (g.shape[0] // N_DEV, g.shape[1]) if axis == 0 else (g.shape[0], g.shape[1] // N_DEV)


class _PairSwap:
    def __init__(self, grads, axes):
        self.arrays, self.axes, self.n = list(grads), tuple(axes), len(grads)
        self.sizes = [g.shape[axis] // N_DEV for g, axis in zip(grads, axes)]
        self.out_shape = [jax.ShapeDtypeStruct((N_CHIPS,) + _shard_shape(g, axis), g.dtype)
                          for g, axis in zip(grads, axes)]
        self.scratch = [pltpu.SemaphoreType.DMA((self.n, N_CHIPS)), pltpu.SemaphoreType.DMA((self.n, N_CHIPS))]

    def _copies(self, srcs, outs, sems):
        send_sems, recv_sems = sems
        x, y, c = _me()
        sibling = (x, y, 1 - c)
        copies = []
        for a in range(self.n):
            for chip in range(N_CHIPS):
                owner_idx = 2 * chip + (1 - c)
                copies.append(pltpu.make_async_remote_copy(
                    src_ref=_block(srcs[a], self.axes[a], owner_idx, self.sizes[a]), dst_ref=outs[a].at[chip],
                    send_sem=send_sems.at[a, chip], recv_sem=recv_sems.at[a, chip],
                    device_id=sibling, device_id_type=MESH_T))
        return copies

    def start(self, srcs, outs, sems):
        for cp in self._copies(srcs, outs, sems):
            cp.start()

    def finish(self, srcs, outs, sems):
        for cp in self._copies(srcs, outs, sems):
            cp.wait()


def _pair_add(grads, swapped, axes):
    n = len(grads)
    c_arr = lax.axis_index("c").astype(jnp.int32).reshape(1)

    def body(c_ref, *refs):
        del c_ref
        mine, got, outs = refs[:n], refs[n:2 * n], refs[2 * n:]
        for a in range(n):
            outs[a][0] = (mine[a][...].astype(F32) + got[a][0].astype(F32)).astype(BF16)

    in_specs, out_specs, out_shape = [], [], []
    for g, axis in zip(grads, axes):
        shard = _shard_shape(g, axis)
        if axis == 0:
            in_specs.append(pl.BlockSpec(shard, lambda s, c_ref: (2 * s + c_ref[0], 0)))
        else:
            in_specs.append(pl.BlockSpec(shard, lambda s, c_ref: (0, 2 * s + c_ref[0])))
    for g, axis in zip(grads, axes):
        shard = _shard_shape(g, axis)
        in_specs.append(pl.BlockSpec((1,) + shard, lambda s, c_ref: (s, 0, 0)))
        out_specs.append(pl.BlockSpec((1,) + shard, lambda s, c_ref: (s, 0, 0)))
        out_shape.append(jax.ShapeDtypeStruct((N_CHIPS,) + shard, BF16))
    return pl.pallas_call(
        body, name="pair_add_" + str(n),
        grid_spec=pltpu.PrefetchScalarGridSpec(num_scalar_prefetch=1, grid=(N_CHIPS,), in_specs=in_specs,
                                               out_specs=out_specs),
        out_shape=out_shape, compiler_params=_params(("arbitrary",)),
    )(c_arr, *grads, *swapped)


class _ChipScatter:
    def __init__(self, sums):
        self.arrays, self.n = list(sums), len(sums)
        self.out_shape = [jax.ShapeDtypeStruct(s.shape, s.dtype) for s in sums]
        self.scratch = [pltpu.SemaphoreType.DMA((self.n, 3)), pltpu.SemaphoreType.DMA((self.n, 3)),
                        pltpu.SemaphoreType.DMA((self.n,))]

    def _copies(self, srcs, outs, sems):
        send_sems, recv_sems, local_sems = sems
        me = _me()
        my_chip = 2 * me[0] + me[1]
        own = [pltpu.make_async_copy(srcs[a].at[my_chip], outs[a].at[my_chip], local_sems.at[a])
               for a in range(self.n)]
        sends, recvs = [], []
        for a in range(self.n):
            for k, rel in enumerate((4, 2, 6)):
                peer = _peer(me, rel)
                peer_chip = 2 * peer[0] + peer[1]
                sends.append(pltpu.make_async_remote_copy(
                    src_ref=srcs[a].at[peer_chip], dst_ref=outs[a].at[my_chip],
                    send_sem=send_sems.at[a, k], recv_sem=recv_sems.at[a, k], device_id=peer, device_id_type=MESH_T))
                recvs.append(pltpu.make_async_remote_copy(
                    src_ref=srcs[a].at[my_chip], dst_ref=outs[a].at[peer_chip],
                    send_sem=send_sems.at[a, k], recv_sem=recv_sems.at[a, k], device_id=peer, device_id_type=MESH_T))
        return own, sends, recvs

    def start(self, srcs, outs, sems):
        own, sends, _ = self._copies(srcs, outs, sems)
        for cp in own + sends:
            cp.start()

    def finish(self, srcs, outs, sems):
        own, sends, recvs = self._copies(srcs, outs, sems)
        for cp in recvs:
            cp.wait_recv()
        for cp in sends:
            cp.wait_send()
        for cp in own:
            cp.wait()


def _call(body, *, name, args, in_specs, out_specs, out_shape, grid=(), scratch_shapes=(), semantics=None,
          rider=None):
    if rider is None:
        return pl.pallas_call(
            body, name=name, grid=grid, in_specs=in_specs, out_specs=out_specs, out_shape=out_shape,
            scratch_shapes=list(scratch_shapes), compiler_params=_params(semantics))(*args), None
    n_in, n_out, n_scr, r = len(in_specs), len(out_specs), len(scratch_shapes), rider.n

    def wrapped(*refs):
        ins, r_ins = refs[:n_in], refs[n_in:n_in + r]
        outs = refs[n_in + r:n_in + r + n_out]
        r_outs = refs[n_in + r + n_out:n_in + 2 * r + n_out]
        scr = refs[n_in + 2 * r + n_out:n_in + 2 * r + n_out + n_scr]
        sems = refs[n_in + 2 * r + n_out + n_scr:]
        first, last = None, None
        for ax in range(len(grid)):
            f, l = pl.program_id(ax) == 0, pl.program_id(ax) == pl.num_programs(ax) - 1
            first = f if first is None else first & f
            last = l if last is None else last & l
        if first is None:
            rider.start(r_ins, r_outs, sems)
            body(*ins, *outs, *scr)
            rider.finish(r_ins, r_outs, sems)
            return

        @pl.when(first)
        def _():
            rider.start(r_ins, r_outs, sems)

        body(*ins, *outs, *scr)

        @pl.when(last)
        def _():
            rider.finish(r_ins, r_outs, sems)

    any_spec = pl.BlockSpec(memory_space=pl.ANY)
    out = pl.pallas_call(
        wrapped, name=name, grid=grid, in_specs=list(in_specs) + [any_spec] * r,
        out_specs=list(out_specs) + [any_spec] * r, out_shape=list(out_shape) + rider.out_shape,
        scratch_shapes=list(scratch_shapes) + rider.scratch,
        compiler_params=_params(None if semantics is None else ("arbitrary",) * len(semantics)),
    )(*args, *rider.arrays)
    return out[:n_out], out[n_out:]


def _exchange(name, rider):
    def body():
        pass

    return _call(body, name=name, args=(), in_specs=[], out_specs=[], out_shape=[], rider=rider)[1]


def _all_reduce_small(part):
    def body(p_ref, o_ref, slots, send_sems, recv_sems):
        me = _me()
        my_idx = _linear(me)
        slots[my_idx] = p_ref[...]
        sends = []
        for rel in range(1, N_DEV):
            cp = pltpu.make_async_remote_copy(
                src_ref=p_ref, dst_ref=slots.at[my_idx], send_sem=send_sems.at[rel - 1],
                recv_sem=recv_sems.at[rel - 1], device_id=_peer(me, rel), device_id_type=MESH_T)
            cp.start()
            sends.append(cp)
        for rel in range(1, N_DEV):
            frm = _peer(me, rel)
            pltpu.make_async_remote_copy(
                src_ref=p_ref, dst_ref=slots.at[_linear(frm)], send_sem=send_sems.at[rel - 1],
                recv_sem=recv_sems.at[rel - 1], device_id=frm, device_id_type=MESH_T).wait_recv()
        for cp in sends:
            cp.wait_send()
        total = slots[0]
        for d in range(1, N_DEV):
            total = total + slots[d]
        o_ref[...] = total

    return pl.pallas_call(
        body, name="all_reduce_small",
        in_specs=[pl.BlockSpec(memory_space=pltpu.VMEM)], out_specs=pl.BlockSpec(memory_space=pltpu.VMEM),
        out_shape=jax.ShapeDtypeStruct(part.shape, F32),
        scratch_shapes=[pltpu.VMEM((N_DEV,) + part.shape, F32), pltpu.SemaphoreType.DMA((7,)),
                        pltpu.SemaphoreType.DMA((7,))],
        compiler_params=_params(),
    )(part)


def _adamw_math(w, g, m, v):
    m = ADAM_B1 * m + (1.0 - ADAM_B1) * g
    v = ADAM_B2 * v + (1.0 - ADAM_B2) * jnp.square(g)
    m_hat = m / (1.0 - ADAM_B1 ** ADAM_STEP)
    v_hat = v / (1.0 - ADAM_B2 ** ADAM_STEP)
    delta = -ADAM_LR * (m_hat / (jnp.sqrt(v_hat) + ADAM_EPS) + ADAM_WD * w)
    return delta, m, v


ADAMW_STEPS = 4


def _adamw_big(shares, ws, ms, vs):
    n = len(ws)

    def body(*refs):
        s_refs, w_refs, m_refs, v_refs = (refs[a * n:(a + 1) * n] for a in range(4))
        outs = refs[4 * n:]
        for a in range(n):
            g = s_refs[a][0].astype(F32)
            for d in range(1, N_CHIPS):
                g = g + s_refs[a][d].astype(F32)
            outs[4 * a][...] = g
            outs[4 * a + 1][...], outs[4 * a + 2][...], outs[4 * a + 3][...] = _adamw_math(
                w_refs[a][...], g, m_refs[a][...], v_refs[a][...])

    def chunk(w):
        return pl.BlockSpec((w.shape[0] // ADAMW_STEPS, w.shape[1]), lambda i: (i, 0))

    def share_chunk(w):
        return pl.BlockSpec((N_CHIPS, w.shape[0] // ADAMW_STEPS, w.shape[1]), lambda i: (0, i, 0))

    out = pl.pallas_call(
        body, name="adamw_big", grid=(ADAMW_STEPS,),
        in_specs=[share_chunk(w) for w in ws] + [chunk(w) for w in ws] * 3,
        out_specs=[chunk(w) for w in ws for _ in range(4)],
        out_shape=[jax.ShapeDtypeStruct(w.shape, F32) for w in ws for _ in range(4)],
        compiler_params=_params(("parallel",)),
    )(*shares, *ws, *ms, *vs)
    return [tuple(out[4 * a:4 * a + 4]) for a in range(n)]


def _adamw_small(quads):
    n = len(quads)

    def body(*refs):
        ins, outs = refs[:4 * n], refs[4 * n:]
        for p in range(n):
            g_ref, w_ref, m_ref, v_ref = ins[4 * p:4 * p + 4]
            d_ref, nm_ref, nv_ref = outs[3 * p:3 * p + 3]
            d_ref[...], nm_ref[...], nv_ref[...] = _adamw_math(w_ref[...], g_ref[...], m_ref[...], v_ref[...])

    flat = [a for quad in quads for a in quad]
    out = pl.pallas_call(
        body, name="adamw_small",
        out_shape=[jax.ShapeDtypeStruct(quad[1].shape, F32) for quad in quads for _ in range(3)],
        compiler_params=_params(),
    )(*flat)
    return [tuple(out[3 * p:3 * p + 3]) for p in range(n)]


def kernel(x, norm1_g, w_in, q_norm_g, k_norm_g, rel_bias, conv_w, conv_b, w_attn_proj, w_conv_proj, w_gate, b_gate, w_out, norm2_g, w_up, w_down, loss_target, m_norm1_g, m_w_in, m_q_norm_g, m_k_norm_g, m_rel_bias, m_conv_w, m_conv_b, m_w_attn_proj, m_w_conv_proj, m_w_gate, m_b_gate, m_w_out, m_norm2_g, m_w_up, m_w_down, v_norm1_g, v_w_in, v_q_norm_g, v_k_norm_g, v_rel_bias, v_conv_w, v_conv_b, v_w_attn_proj, v_w_conv_proj, v_w_gate, v_b_gate, v_w_out, v_norm2_g, v_w_up, v_w_down):
    my_idx = _linear(_me())
    big_w = (w_in, w_attn_proj, w_conv_proj, w_gate, w_out, w_up, w_down)
    big_m = (m_w_in, m_w_attn_proj, m_w_conv_proj, m_w_gate, m_w_out, m_w_up, m_w_down)
    big_v = (v_w_in, v_w_attn_proj, v_w_conv_proj, v_w_gate, v_w_out, v_w_up, v_w_down)
    big_names = ("w_in", "w_attn_proj", "w_conv_proj", "w_gate", "w_out", "w_up", "w_down")

    conv_w_tile = jnp.pad(conv_w, ((0, SUBLANES - conv_w.shape[0]), (0, 0)))
    shards = _cast_shards(big_w)
    (bias,), (w_in_full, conv_w_rows) = _bias_tiles(rel_bias, rider=_Gather((shards[0], conv_w_tile), (1, 1)))

    dx, shares, small = _local_step(x[0], loss_target[0], norm1_g, q_norm_g, k_norm_g, bias, conv_w_rows[:3],
                                    conv_b, b_gate, norm2_g, w_in_full, tuple(shards[1:5]), tuple(shards[5:7]), True)

    big_out = _adamw_big(shares, big_w, big_m, big_v)

    tot = _all_reduce_small(small)
    g_rel_bias = jnp.concatenate(
        [tot[10:26, :QB][:, ::-1], tot[10:26, QB:2 * QB][:, ::-1], tot[10:26, 2 * QB:2 * QB + 1]], axis=1)
    g_conv_w = lax.dynamic_slice(tot[4:7], (0, my_idx * LANES), (3, LANES))
    small_g = [tot[0:1], tot[1:2, :HEAD_DIM], tot[2:3, :HEAD_DIM], g_rel_bias, g_conv_w, tot[3:4],
               tot[7:9].reshape(1, 2 * D_MODEL), tot[9:10]]
    small_w = (norm1_g, q_norm_g, k_norm_g, rel_bias, conv_w, conv_b, b_gate, norm2_g)
    small_m = (m_norm1_g, m_q_norm_g, m_k_norm_g, m_rel_bias, m_conv_w, m_conv_b, m_b_gate, m_norm2_g)
    small_v = (v_norm1_g, v_q_norm_g, v_k_norm_g, v_rel_bias, v_conv_w, v_conv_b, v_b_gate, v_norm2_g)

    def two_d(a):
        return a.reshape(1, -1) if a.ndim == 1 else a

    small_out = _adamw_small([(g, two_d(w), two_d(m), two_d(v))
                              for g, w, m, v in zip(small_g, small_w, small_m, small_v)])

    order = ("norm1_g", "w_in", "q_norm_g", "k_norm_g", "rel_bias", "conv_w", "conv_b", "w_attn_proj", "w_conv_proj",
             "w_gate", "b_gate", "w_out", "norm2_g", "w_up", "w_down")
    small_names = ("norm1_g", "q_norm_g", "k_norm_g", "rel_bias", "conv_w", "conv_b", "b_gate", "norm2_g")
    res = {}
    for name, (g, d, nm, nv) in zip(big_names, big_out):
        res[name] = (g, d, nm, nv)
    for name, g, w, (d, nm, nv) in zip(small_names, small_g, small_w, small_out):
        res[name] = tuple(a.reshape(w.shape) for a in (g, d, nm, nv))
    loss = tot[26, 0]
    return (loss, dx[None], *[res[n][0] for n in order], *[res[n][1] for n in order],
            *[res[n][2] for n in order], *[res[n][3] for n in order])
```

```python
import functools

import jax
import jax.numpy as jnp
from jax import lax
from jax.experimental import pallas as pl
from jax.experimental.pallas import tpu as pltpu

F32 = jnp.float32
BF16 = jnp.bfloat16

D_MODEL = 1024
N_HEADS = 16
HEAD_DIM = 64
CHUNK = 64
N_PREV_CHUNKS = 8
MAX_REL = 256
D_FF = 4096
EPS = 1e-6
NEG_INF = -1e30
N_DEV = 8

ADAM_LR = 0.001
ADAM_B1 = 0.9
ADAM_B2 = 0.999
ADAM_EPS = 1e-08
ADAM_WD = 0.01
ADAM_STEP = 10

LANES = 128
SUBLANES = 8
VMEM_LIMIT = 56 * 1024 * 1024
QB = 256
KW = 3 * QB
PAIRS = 4
SUB = 128
SLAB = PAIRS * LANES
SKEW = 1024

MESH_T = pl.DeviceIdType.MESH


def _dot(a, b):
    return jnp.dot(a, b, preferred_element_type=F32)


def _dot_nt(a, b):
    return lax.dot_general(a, b, (((1,), (1,)), ((), ())), preferred_element_type=F32)


def _dot_tn(a, b):
    return lax.dot_general(a, b, (((0,), (0,)), ((), ())), preferred_element_type=F32)


def _params(sem=None):
    return pltpu.CompilerParams(dimension_semantics=sem, vmem_limit_bytes=VMEM_LIMIT)


def _resident(shape):
    return pl.BlockSpec(shape, lambda *_: (0,) * len(shape), pipeline_mode=pl.Buffered(1))


def _fold8(v):
    rows, n = v.shape
    return v.reshape(rows // SUBLANES, SUBLANES, n).sum(axis=0)


def _head_sum_matrix():
    r = lax.broadcasted_iota(jnp.int32, (LANES, LANES), 0) // HEAD_DIM
    c = lax.broadcasted_iota(jnp.int32, (LANES, LANES), 1) // HEAD_DIM
    return (r == c).astype(BF16)


def _head_sums(v, e):
    hi = v.astype(BF16)
    lo = (v - hi.astype(F32)).astype(BF16)
    return _dot(hi, e) + _dot(lo, e)


def _in_proj(x, g1, w_in, rider=None):
    t = x.shape[0]
    tm = min(t, 512)
    n_out = w_in.shape[1]

    def body(x_ref, g_ref, w_ref, proj_ref, h_ref):
        xf = x_ref[...]
        r = lax.rsqrt(jnp.mean(xf * xf, axis=-1, keepdims=True) + EPS)
        h = (xf * r * g_ref[...]).astype(BF16)
        h_ref[...] = h
        for k in range(n_out // D_MODEL):
            cols = slice(k * D_MODEL, (k + 1) * D_MODEL)
            proj_ref[:, cols] = _dot(h, w_ref[:, cols]).astype(BF16)

    return _call(
        body, name="in_proj", grid=(t // tm,), args=(x, g1, w_in),
        in_specs=[pl.BlockSpec((tm, D_MODEL), lambda i: (i, 0)),
                  pl.BlockSpec((1, D_MODEL), lambda i: (0, 0)),
                  _resident((D_MODEL, n_out))],
        out_specs=[pl.BlockSpec((tm, n_out), lambda i: (i, 0)),
                   pl.BlockSpec((tm, D_MODEL), lambda i: (i, 0))],
        out_shape=[jax.ShapeDtypeStruct((t, n_out), BF16), jax.ShapeDtypeStruct((t, D_MODEL), BF16)],
        semantics=("parallel",), rider=rider)


def _proj(name, h, w, b=None, rider=None):
    t = h.shape[0]
    tm = min(t, 512)
    n_out = w.shape[1]

    def body(*refs):
        h_ref, w_ref, o_ref = refs[0], refs[1], refs[-1]
        hv = h_ref[...]
        for k in range(n_out // D_MODEL):
            cols = slice(k * D_MODEL, (k + 1) * D_MODEL)
            y = _dot(hv, w_ref[:, cols])
            if b is not None:
                y = jax.nn.sigmoid(y + refs[2][:, cols])
            o_ref[:, cols] = y.astype(BF16)

    in_specs = [pl.BlockSpec((tm, D_MODEL), lambda i: (i, 0)), _resident((D_MODEL, n_out))]
    args = (h, w)
    if b is not None:
        in_specs.append(pl.BlockSpec((1, n_out), lambda i: (0, 0)))
        args = (h, w, b)
    return _call(
        body, name=name, grid=(t // tm,), args=args, in_specs=in_specs,
        out_specs=[pl.BlockSpec((tm, n_out), lambda i: (i, 0))],
        out_shape=[jax.ShapeDtypeStruct((t, n_out), BF16)],
        semantics=("parallel",), rider=rider)


def _qknorm_fwd(proj, gq, gk):
    t = proj.shape[0]
    tm = min(t, 512)
    scale = HEAD_DIM ** -0.5

    def body(q_ref, k_ref, gq_ref, gk_ref, qn_ref, kn_ref):
        e = _head_sum_matrix()
        for src, g_ref, dst, sc in ((q_ref, gq_ref, qn_ref, scale), (k_ref, gk_ref, kn_ref, 1.0)):
            for s in range(D_MODEL // LANES):
                sl = slice(s * LANES, (s + 1) * LANES)
                xf = src[:, sl].astype(F32)
                r = lax.rsqrt(_head_sums(xf * xf, e) * (1.0 / HEAD_DIM) + EPS)
                dst[:, sl] = (xf * r * g_ref[:, sl] * sc).astype(BF16)

    return pl.pallas_call(
        body, name="qknorm_fwd", grid=(t // tm,),
        in_specs=[pl.BlockSpec((tm, D_MODEL), lambda i: (i, 0)),
                  pl.BlockSpec((tm, D_MODEL), lambda i: (i, 1)),
                  pl.BlockSpec((1, D_MODEL), lambda i: (0, 0)),
                  pl.BlockSpec((1, D_MODEL), lambda i: (0, 0))],
        out_specs=[pl.BlockSpec((tm, D_MODEL), lambda i: (i, 0))] * 2,
        out_shape=[jax.ShapeDtypeStruct((t, D_MODEL), BF16)] * 2,
        compiler_params=_params(("parallel",)),
    )(proj, proj, gq, gk)


def _bias_tiles(rel_bias, rider=None):
    by_dist = jnp.concatenate(
        [rel_bias[:, :2 * MAX_REL], jnp.broadcast_to(rel_bias[:, 2 * MAX_REL:], (N_HEADS, 2 * MAX_REL))], axis=1)
    by_dist = by_dist.reshape(N_HEADS, 1, SKEW)

    def body(f_ref, o_ref):
        jj = lax.broadcasted_iota(jnp.int32, (QB, QB), 0)
        ii = lax.broadcasted_iota(jnp.int32, (QB, QB), 1)
        for w in range(KW // QB):
            pos = jnp.broadcast_to(f_ref[0, :, KW - QB * w:KW - QB * w + QB], (QB, QB))
            neg = jnp.broadcast_to(f_ref[0, :, KW - QB * (w + 1):KW - QB * w], (QB, QB))
            pos = pltpu.roll(pos, 0, 1, stride=1, stride_axis=0)
            neg = pltpu.roll(neg, 0, 1, stride=1, stride_axis=0)
            tile = jnp.where(ii >= jj, pos, neg)
            kc = (jj + QB * w) // CHUNK
            qc = ii // CHUNK
            band = (kc >= qc) & (kc <= qc + N_PREV_CHUNKS)
            o_ref[0, QB * w:QB * (w + 1), :] = jnp.where(band, tile, NEG_INF)

    return _call(
        body, name="bias_tiles", grid=(N_HEADS,), args=(by_dist,),
        in_specs=[pl.BlockSpec((1, 1, SKEW), lambda h: (h, 0, 0))],
        out_specs=[pl.BlockSpec((1, KW, QB), lambda h: (h, 0, 0))],
        out_shape=[jax.ShapeDtypeStruct((N_HEADS, KW, QB), F32)],
        semantics=("parallel",), rider=rider)


def _window_specs(col0):
    return [pl.BlockSpec((QB, SLAB), functools.partial(
        lambda p, b, back: (jnp.maximum(b - back, 0), col0 + p), back=back)) for back in (2, 1, 0)]


def _attn_fwd(qn, kn, proj, bias, rider=None):
    t = qn.shape[0]
    nb = t // QB
    v_col0 = 2 * D_MODEL // SLAB

    def body(q_ref, k0, k1, k2, v0, v1, v2, bias_ref, o_ref, lse_ref):
        b = pl.program_id(1)

        @pl.when(b < 2)
        def _():
            compute(q_ref, k0, k1, k2, v0, v1, v2, bias_ref, o_ref, lse_ref,
                    lax.broadcasted_iota(jnp.int32, (KW, 1), 0) >= (2 - b) * QB)

        @pl.when(b >= 2)
        def _():
            compute(q_ref, k0, k1, k2, v0, v1, v2, bias_ref, o_ref, lse_ref, None)

    def compute(q_ref, k0, k1, k2, v0, v1, v2, bias_ref, o_ref, lse_ref, valid):
        head_a = lax.broadcasted_iota(jnp.int32, (1, LANES), 1) < HEAD_DIM

        def scores(head):
            hp, hh = divmod(head, 2)
            sl = slice(hp * LANES, (hp + 1) * LANES)
            k = jnp.concatenate([k0[:, sl], k1[:, sl], k2[:, sl]], axis=0)
            mine = head_a if hh == 0 else jnp.logical_not(head_a)
            s = _dot_nt(jnp.where(mine, k, jnp.zeros_like(k)), q_ref[:, sl]) + bias_ref[head]
            return s if valid is None else jnp.where(valid, s, NEG_INF)

        def weighted_values(head, s):
            hp, hh = divmod(head, 2)
            sl = slice(hp * LANES, (hp + 1) * LANES)
            v = jnp.concatenate([v0[:, sl], v1[:, sl], v2[:, sl]], axis=0)
            vt = v.astype(F32).T.astype(BF16)[hh * HEAD_DIM:(hh + 1) * HEAD_DIM]
            m = jnp.max(s, axis=0, keepdims=True)
            p = jnp.exp(s - m)
            l = jnp.sum(p, axis=0, keepdims=True)
            return _dot(vt, p.astype(BF16)) / l, m + jnp.log(l)

        outs, lses = [], []
        pending = scores(0)
        for head in range(2 * PAIRS):
            nxt = scores(head + 1) if head + 1 < 2 * PAIRS else None
            o, lse = weighted_values(head, pending)
            outs.append(o)
            lses.append(lse)
            pending = nxt
        for hp in range(PAIRS):
            sl = slice(hp * LANES, (hp + 1) * LANES)
            o_ref[:, sl] = jnp.concatenate([outs[2 * hp], outs[2 * hp + 1]], axis=0).T.astype(BF16)
        lse_ref[...] = jnp.concatenate(lses, axis=0)

    return _call(
        body, name="attn_fwd", grid=(D_MODEL // SLAB, nb), args=(qn, kn, kn, kn, proj, proj, proj, bias),
        in_specs=[pl.BlockSpec((QB, SLAB), lambda p, b: (b, p))] + _window_specs(0) + _window_specs(v_col0)
        + [pl.BlockSpec((2 * PAIRS, KW, QB), lambda p, b: (p, 0, 0))],
        out_specs=[pl.BlockSpec((QB, SLAB), lambda p, b: (b, p)),
                   pl.BlockSpec((2 * PAIRS, QB), lambda p, b: (p, b))],
        out_shape=[jax.ShapeDtypeStruct((t, D_MODEL), BF16), jax.ShapeDtypeStruct((N_HEADS, t), F32)],
        semantics=("parallel", "arbitrary"), rider=rider)


def _shift_down(u, halo, n):
    rows = lax.broadcasted_iota(jnp.int32, (u.shape[0], 1), 0)
    out = pltpu.roll(u, n, 0)
    for j in range(n):
        out = jnp.where(rows == j, halo[SUBLANES - n + j:SUBLANES - n + j + 1, :], out)
    return out


def _shift_up(u, halo, n):
    tm = u.shape[0]
    rows = lax.broadcasted_iota(jnp.int32, (tm, 1), 0)
    out = pltpu.roll(u, tm - n, 0)
    for j in range(n):
        out = jnp.where(rows == tm - n + j, halo[j:j + 1, :], out)
    return out


def _conv_fwd(proj, conv_w, conv_b):
    t = proj.shape[0]
    tm = min(t, 512)
    hb = tm // SUBLANES

    def body(bg_ref, cg_ref, xc_ref, cgh_ref, xch_ref, w_ref, b_ref, o_ref):
        i = pl.program_id(0)
        u = cg_ref[...].astype(F32) * xc_ref[...].astype(F32)
        halo = cgh_ref[...].astype(F32) * xch_ref[...].astype(F32)
        halo = jnp.where(i > 0, halo, 0.0)
        w = w_ref[...]
        s = w[0:1] * _shift_down(u, halo, 2) + w[1:2] * _shift_down(u, halo, 1) + w[2:3] * u
        o_ref[...] = (bg_ref[...].astype(F32) * (b_ref[...] + s)).astype(BF16)

    def prev(col):
        return pl.BlockSpec((SUBLANES, D_MODEL), lambda i: (jnp.maximum(i * hb - 1, 0), col))

    return pl.pallas_call(
        body, name="conv_fwd", grid=(t // tm,),
        in_specs=[pl.BlockSpec((tm, D_MODEL), lambda i: (i, 3)),
                  pl.BlockSpec((tm, D_MODEL), lambda i: (i, 4)),
                  pl.BlockSpec((tm, D_MODEL), lambda i: (i, 5)),
                  prev(4), prev(5),
                  pl.BlockSpec((3, D_MODEL), lambda i: (0, 0)),
                  pl.BlockSpec((1, D_MODEL), lambda i: (0, 0))],
        out_specs=pl.BlockSpec((tm, D_MODEL), lambda i: (i, 0)),
        out_shape=jax.ShapeDtypeStruct((t, D_MODEL), BF16),
        compiler_params=_params(("parallel",)),
    )(proj, proj, proj, proj, proj, conv_w, conv_b)


def _mix_out(y_attn, y_conv, gates, x, w_ap, w_cp, w_out, g2):
    t = x.shape[0]
    tm = min(t, 512)

    def body(ya_in, yc_in, g_ref, x_ref, wap, wcp, wout, g2_ref, ya_ref, yc_ref, mg_ref, x1_ref, h2_ref):
        ya = _dot(ya_in[...], wap[...])
        yc = _dot(yc_in[...], wcp[...])
        ya_ref[...] = ya.astype(BF16)
        yc_ref[...] = yc.astype(BF16)
        merged = (g_ref[:, :D_MODEL].astype(F32) * ya + g_ref[:, D_MODEL:].astype(F32) * yc).astype(BF16)
        mg_ref[...] = merged
        x1 = x_ref[...] + _dot(merged, wout[...])
        x1_ref[...] = x1
        r = lax.rsqrt(jnp.mean(x1 * x1, axis=-1, keepdims=True) + EPS)
        h2_ref[...] = (x1 * r * g2_ref[...]).astype(BF16)

    row = pl.BlockSpec((tm, D_MODEL), lambda i: (i, 0))
    full = _resident((D_MODEL, D_MODEL))
    return pl.pallas_call(
        body, name="mix_out", grid=(t // tm,),
        in_specs=[row, row, pl.BlockSpec((tm, 2 * D_MODEL), lambda i: (i, 0)), row, full, full, full,
                  pl.BlockSpec((1, D_MODEL), lambda i: (0, 0))],
        out_specs=[row] * 5,
        out_shape=[jax.ShapeDtypeStruct((t, D_MODEL), BF16)] * 3
        + [jax.ShapeDtypeStruct((t, D_MODEL), F32), jax.ShapeDtypeStruct((t, D_MODEL), BF16)],
        compiler_params=_params(("parallel",)),
    )(y_attn, y_conv, gates, x, w_ap, w_cp, w_out, g2)


def _mlp_fwd(h2, w_up, w_down, x1, target):
    t = h2.shape[0]
    tm = min(t, 512)
    tf = 1024
    nf = D_FF // tf

    def body(h2_ref, wup, wdn, x1_ref, tg_ref, a_ref, dy_ref, dyb_ref, loss_ref):
        h2v = h2_ref[...]
        acc = None
        pending = _dot(h2v, wup[:, 0:tf])
        for j in range(nf):
            cols = slice(j * tf, (j + 1) * tf)
            a = pending
            if j + 1 < nf:
                pending = _dot(h2v, wup[:, (j + 1) * tf:(j + 2) * tf])
            a_ref[:, cols] = a.astype(BF16)
            part = _dot(jnp.square(jnp.maximum(a, 0.0)).astype(BF16), wdn[cols, :])
            acc = part if acc is None else acc + part

        @pl.when(pl.program_id(0) == 0)
        def _():
            loss_ref[...] = jnp.zeros_like(loss_ref)

        diff = x1_ref[...] + acc - tg_ref[...]
        loss_ref[...] += _fold8(diff * diff)
        dy = diff * (1.0 / D_MODEL)
        dy_ref[...] = dy
        dyb_ref[...] = dy.astype(BF16)

    row = pl.BlockSpec((tm, D_MODEL), lambda i: (i, 0))
    return pl.pallas_call(
        body, name="mlp_fwd", grid=(t // tm,),
        in_specs=[row, _resident((D_MODEL, D_FF)), _resident((D_FF, D_MODEL)), row, row],
        out_specs=[pl.BlockSpec((tm, D_FF), lambda i: (i, 0)), row, row,
                   pl.BlockSpec((SUBLANES, D_MODEL), lambda i: (0, 0))],
        out_shape=[jax.ShapeDtypeStruct((t, D_FF), BF16), jax.ShapeDtypeStruct((t, D_MODEL), F32),
                   jax.ShapeDtypeStruct((t, D_MODEL), BF16), jax.ShapeDtypeStruct((SUBLANES, D_MODEL), F32)],
        compiler_params=_params(("arbitrary",)),
    )(h2, w_up, w_down, x1, target)


def _rmsnorm_bwd(xf, g, dh):
    r = lax.rsqrt(jnp.mean(xf * xf, axis=-1, keepdims=True) + EPS)
    xh = xf * r
    dxh = dh * g
    dx = r * (dxh - xh * jnp.mean(dxh * xh, axis=-1, keepdims=True))
    return dx, dh * xh


def _mlp_bwd(dyb, a, w_down, w_up, x1, dy, g2):
    t = dyb.shape[0]
    tm = min(t, 512)
    tf = 1024
    nf = D_FF // tf

    def body(dyb_ref, a_ref, wdn, wup, x1_ref, dy_ref, g2_ref, da_ref, dx1_ref, dx1b_ref, dg2_ref):
        dyv = dyb_ref[...]
        acc = None
        pending = _dot_nt(dyv, wdn[0:tf, :])
        for j in range(nf):
            cols = slice(j * tf, (j + 1) * tf)
            du = pending
            if j + 1 < nf:
                pending = _dot_nt(dyv, wdn[(j + 1) * tf:(j + 2) * tf, :])
            da = (du * (2.0 * jnp.maximum(a_ref[:, cols].astype(F32), 0.0))).astype(BF16)
            da_ref[:, cols] = da
            part = _dot_nt(da, wup[:, cols])
            acc = part if acc is None else acc + part

        @pl.when(pl.program_id(0) == 0)
        def _():
            dg2_ref[...] = jnp.zeros_like(dg2_ref)

        dx, dg = _rmsnorm_bwd(x1_ref[...], g2_ref[...], acc)
        dx1 = dy_ref[...] + dx
        dx1_ref[...] = dx1
        dx1b_ref[...] = dx1.astype(BF16)
        dg2_ref[...] += _fold8(dg)

    row = pl.BlockSpec((tm, D_MODEL), lambda i: (i, 0))
    wide = pl.BlockSpec((tm, D_FF), lambda i: (i, 0))
    return pl.pallas_call(
        body, name="mlp_bwd", grid=(t // tm,),
        in_specs=[row, wide, _resident((D_FF, D_MODEL)), _resident((D_MODEL, D_FF)), row, row,
                  pl.BlockSpec((1, D_MODEL), lambda i: (0, 0))],
        out_specs=[wide, row, row, pl.BlockSpec((SUBLANES, D_MODEL), lambda i: (0, 0))],
        out_shape=[jax.ShapeDtypeStruct((t, D_FF), BF16), jax.ShapeDtypeStruct((t, D_MODEL), F32),
                   jax.ShapeDtypeStruct((t, D_MODEL), BF16), jax.ShapeDtypeStruct((SUBLANES, D_MODEL), F32)],
        compiler_params=_params(("arbitrary",)),
    )(dyb, a, w_down, w_up, x1, dy, g2)


def _wgrad(name, lhs, rhs_list, rhs_slabs, relu_sq=False):
    t, m = lhs.shape
    tt = min(t, 2048)
    tmo = min(m, 1024)
    n_slab = sum(rhs_slabs)
    starts = [sum(rhs_slabs[:n]) for n in range(len(rhs_slabs))]
    n_rhs = len(rhs_list)

    def body(*refs):
        l_ref, r_refs, o_ref, acc = refs[0], refs[1:1 + n_rhs], refs[1 + n_rhs], refs[2 + n_rhs]
        k, s = pl.program_id(1), pl.program_id(2)
        lv = l_ref[...]
        if relu_sq:
            lv = jnp.square(jnp.maximum(lv.astype(F32), 0.0)).astype(BF16)

        @pl.when(s == 0)
        def _():
            acc[...] = jnp.zeros_like(acc)

        for n in range(n_rhs):
            @pl.when((k >= starts[n]) & (k < starts[n] + rhs_slabs[n]))
            def _(n=n):
                acc[...] += _dot_tn(lv, r_refs[n][...])

        @pl.when(s == pl.num_programs(2) - 1)
        def _():
            o_ref[...] = acc[...].astype(BF16)

    def rhs_spec(n):
        lo, cnt = starts[n], rhs_slabs[n]

        def index(i, k, s):
            inside = (k >= lo) & (k < lo + cnt)
            return (jnp.where(inside, s, 0), jnp.clip(k - lo, 0, cnt - 1))
        return pl.BlockSpec((tt, D_MODEL), index)

    return pl.pallas_call(
        body, name=name, grid=(m // tmo, n_slab, t // tt),
        in_specs=[pl.BlockSpec((tt, tmo), lambda i, k, s: (s, i))] + [rhs_spec(n) for n in range(n_rhs)],
        out_specs=pl.BlockSpec((tmo, D_MODEL), lambda i, k, s: (i, k)),
        out_shape=jax.ShapeDtypeStruct((m, n_slab * D_MODEL), BF16),
        scratch_shapes=[pltpu.VMEM((tmo, D_MODEL), F32)],
        compiler_params=_params(("parallel", "parallel", "arbitrary")),
    )(lhs, *rhs_list)


def _wgrad_group(name, triples):
    t = triples[0][0].shape[0]
    tt = min(t, 1024)
    counts = [n for _, _, n in triples]
    starts = [sum(counts[:n]) for n in range(len(counts))]
    n_prod = len(triples)

    def inside(n, k):
        return (k >= starts[n]) & (k < starts[n] + counts[n])

    def body(*refs):
        l_refs, r_refs, o_refs = refs[:n_prod], refs[n_prod:2 * n_prod], refs[2 * n_prod:3 * n_prod]
        acc = refs[3 * n_prod]
        k, s = pl.program_id(0), pl.program_id(1)

        @pl.when(s == 0)
        def _():
            acc[...] = jnp.zeros_like(acc)

        for n in range(n_prod):
            @pl.when(inside(n, k))
            def _(n=n):
                acc[...] += _dot_tn(l_refs[n][...], r_refs[n][...])

            @pl.when(inside(n, k) & (s == pl.num_programs(1) - 1))
            def _(n=n):
                o_refs[n][...] = acc[...].astype(BF16)

    def lhs_spec(n):
        return pl.BlockSpec((tt, D_MODEL), lambda k, s: (jnp.where(inside(n, k), s, 0), 0))

    def rhs_spec(n):
        return pl.BlockSpec((tt, D_MODEL), lambda k, s: (jnp.where(inside(n, k), s, 0),
                                                         jnp.clip(k - starts[n], 0, counts[n] - 1)))

    def out_spec(n):
        return pl.BlockSpec((D_MODEL, D_MODEL), lambda k, s: (0, jnp.clip(k - starts[n], 0, counts[n] - 1)))

    return pl.pallas_call(
        body, name=name, grid=(sum(counts), t // tt),
        in_specs=[lhs_spec(n) for n in range(n_prod)] + [rhs_spec(n) for n in range(n_prod)],
        out_specs=[out_spec(n) for n in range(n_prod)],
        out_shape=[jax.ShapeDtypeStruct((D_MODEL, n * D_MODEL), BF16) for n in counts],
        scratch_shapes=[pltpu.VMEM((D_MODEL, D_MODEL), F32)],
        compiler_params=_params(("arbitrary", "arbitrary")),
    )(*[tr[0] for tr in triples], *[tr[1] for tr in triples])


def _mix_bwd(dx1b, gates, ya, yc, w_out, w_ap, w_cp, w_g, rider=None):
    t = dx1b.shape[0]
    tm = min(t, 512)

    def body(dx_ref, g_ref, ya_ref, yc_ref, wout, wap, wcp, wg,
             dgp_ref, dya_ref, dyc_ref, dyat_ref, dycv_ref, dhg_ref, dbg_ref):
        dm = _dot_nt(dx_ref[...], wout[...])
        ga = g_ref[:, :D_MODEL].astype(F32)
        gc = g_ref[:, D_MODEL:].astype(F32)
        dya = (dm * ga).astype(BF16)
        dyc = (dm * gc).astype(BF16)
        dya_ref[...] = dya
        dyc_ref[...] = dyc
        dgpa = dm * ya_ref[...].astype(F32) * ga * (1.0 - ga)
        dgpc = dm * yc_ref[...].astype(F32) * gc * (1.0 - gc)

        @pl.when(pl.program_id(0) == 0)
        def _():
            dbg_ref[...] = jnp.zeros_like(dbg_ref)

        dbg_ref[:, :D_MODEL] += _fold8(dgpa)
        dbg_ref[:, D_MODEL:] += _fold8(dgpc)
        dgpa = dgpa.astype(BF16)
        dgpc = dgpc.astype(BF16)
        dgp_ref[:, :D_MODEL] = dgpa
        dgp_ref[:, D_MODEL:] = dgpc
        dyat_ref[...] = _dot_nt(dya, wap[...]).astype(BF16)
        dycv_ref[...] = _dot_nt(dyc, wcp[...]).astype(BF16)
        dhg_ref[...] = _dot_nt(dgpa, wg[:, :D_MODEL]) + _dot_nt(dgpc, wg[:, D_MODEL:])

    row = pl.BlockSpec((tm, D_MODEL), lambda i: (i, 0))
    row2 = pl.BlockSpec((tm, 2 * D_MODEL), lambda i: (i, 0))
    full = _resident((D_MODEL, D_MODEL))
    return _call(
        body, name="mix_bwd", grid=(t // tm,), args=(dx1b, gates, ya, yc, w_out, w_ap, w_cp, w_g),
        in_specs=[row, row2, row, row, full, full, full, _resident((D_MODEL, 2 * D_MODEL))],
        out_specs=[row2, row, row, row, row, row, pl.BlockSpec((SUBLANES, 2 * D_MODEL), lambda i: (0, 0))],
        out_shape=[jax.ShapeDtypeStruct((t, 2 * D_MODEL), BF16)] + [jax.ShapeDtypeStruct((t, D_MODEL), BF16)] * 4
        + [jax.ShapeDtypeStruct((t, D_MODEL), F32), jax.ShapeDtypeStruct((SUBLANES, 2 * D_MODEL), F32)],
        semantics=("arbitrary",), rider=rider)


def _conv_bwd(dyconv, proj, conv_w, conv_b, rider=None):
    t = proj.shape[0]
    tm = min(t, 512)
    hb = tm // SUBLANES
    last = t // SUBLANES - 1

    def body(dy_ref, dyn_ref, bg_ref, bgn_ref, cg_ref, cgp_ref, xc_ref, xcp_ref, w_ref, b_ref,
             o_ref, dcb_ref, dcw_ref):
        i = pl.program_id(0)
        cg = cg_ref[...].astype(F32)
        xc = xc_ref[...].astype(F32)
        bg = bg_ref[...].astype(F32)
        u = cg * xc
        prev = jnp.where(i > 0, cgp_ref[...].astype(F32) * xcp_ref[...].astype(F32), 0.0)
        u1 = _shift_down(u, prev, 1)
        u2 = _shift_down(u, prev, 2)
        w = w_ref[...]
        conv = b_ref[...] + (w[0:1] * u2 + w[1:2] * u1 + w[2:3] * u)
        dy = dy_ref[...].astype(F32)
        dconv = dy * bg
        nxt = jnp.where(i < pl.num_programs(0) - 1, dyn_ref[...].astype(F32) * bgn_ref[...].astype(F32), 0.0)
        du = w[2:3] * dconv + w[1:2] * _shift_up(dconv, nxt, 1) + w[0:1] * _shift_up(dconv, nxt, 2)
        o_ref[:, :D_MODEL] = (dy * conv).astype(BF16)
        o_ref[:, D_MODEL:2 * D_MODEL] = (du * xc).astype(BF16)
        o_ref[:, 2 * D_MODEL:] = (du * cg).astype(BF16)

        @pl.when(i == 0)
        def _():
            dcb_ref[...] = jnp.zeros_like(dcb_ref)
            dcw_ref[...] = jnp.zeros_like(dcw_ref)

        dcb_ref[...] += _fold8(dconv)
        dcw_ref[0:SUBLANES] += _fold8(dconv * u2)
        dcw_ref[SUBLANES:2 * SUBLANES] += _fold8(dconv * u1)
        dcw_ref[2 * SUBLANES:] += _fold8(dconv * u)

    def prev(col):
        return pl.BlockSpec((SUBLANES, D_MODEL), lambda i: (jnp.maximum(i * hb - 1, 0), col))

    def nxt(col):
        return pl.BlockSpec((SUBLANES, D_MODEL), lambda i: (jnp.minimum((i + 1) * hb, last), col))

    def cur(col):
        return pl.BlockSpec((tm, D_MODEL), lambda i: (i, col))

    return _call(
        body, name="conv_bwd", grid=(t // tm,),
        args=(dyconv, dyconv, proj, proj, proj, proj, proj, proj, conv_w, conv_b),
        in_specs=[cur(0), nxt(0), cur(3), nxt(3), cur(4), prev(4), cur(5), prev(5),
                  pl.BlockSpec((3, D_MODEL), lambda i: (0, 0)), pl.BlockSpec((1, D_MODEL), lambda i: (0, 0))],
        out_specs=[pl.BlockSpec((tm, 3 * D_MODEL), lambda i: (i, 0)),
                   pl.BlockSpec((SUBLANES, D_MODEL), lambda i: (0, 0)),
                   pl.BlockSpec((3 * SUBLANES, D_MODEL), lambda i: (0, 0))],
        out_shape=[jax.ShapeDtypeStruct((t, 3 * D_MODEL), BF16), jax.ShapeDtypeStruct((SUBLANES, D_MODEL), F32),
                   jax.ShapeDtypeStruct((3 * SUBLANES, D_MODEL), F32)],
        semantics=("arbitrary",), rider=rider)


def _attn_bwd(qn, kn, proj, dyattn, y_attn, lse, bias, rider=None):
    t = qn.shape[0]
    nb = t // QB
    v_col0 = 2 * D_MODEL // SLAB

    def body(q_ref, k0, k1, k2, v0, v1, v2, do_ref, o_ref, lse_ref, bias_ref,
             dq_ref, dk_ref, dv_ref, db_ref, acck, accv):
        b = pl.program_id(1)

        @pl.when(b == 0)
        def _():
            acck[...] = jnp.zeros_like(acck)
            accv[...] = jnp.zeros_like(accv)
            db_ref[...] = jnp.zeros_like(db_ref)

        def block(valid):
            head_a = lax.broadcasted_iota(jnp.int32, (1, LANES), 1) < HEAD_DIM

            def window(refs, hp):
                sl = slice(hp * LANES, (hp + 1) * LANES)
                return jnp.concatenate([r[:, sl] for r in refs], axis=0)

            def transposed(x, hh):
                return x.astype(F32).T.astype(BF16)[hh * HEAD_DIM:(hh + 1) * HEAD_DIM]

            def probs(head):
                hp, hh = divmod(head, 2)
                sl = slice(hp * LANES, (hp + 1) * LANES)
                mine = head_a if hh == 0 else jnp.logical_not(head_a)
                k = window((k0, k1, k2), hp)
                s = _dot_nt(jnp.where(mine, k, jnp.zeros_like(k)), q_ref[:, sl]) + bias_ref[head]
                s = s if valid is None else jnp.where(valid, s, NEG_INF)
                return jnp.exp(s - lse_ref[head:head + 1, :])

            def grads(head, p):
                hp, hh = divmod(head, 2)
                sl = slice(hp * LANES, (hp + 1) * LANES)
                rows = slice(hh * HEAD_DIM, (hh + 1) * HEAD_DIM)
                mine = head_a if hh == 0 else jnp.logical_not(head_a)
                do = do_ref[:, sl]
                v = window((v0, v1, v2), hp)
                delta = jnp.sum((do.astype(F32).T * o_ref[:, sl].astype(F32).T)[rows], axis=0, keepdims=True)
                ds = p * (_dot_nt(jnp.where(mine, v, jnp.zeros_like(v)), do) - delta)
                db_ref[head] += ds
                pb, dsb = p.astype(BF16), ds.astype(BF16)
                dvt = _dot_nt(transposed(do, hh), pb)
                dkt = _dot_nt(transposed(q_ref[:, sl], hh), dsb)
                dqt = _dot(transposed(window((k0, k1, k2), hp), hh), dsb)
                return dqt, dkt, dvt

            out = []
            pending = probs(0)
            for head in range(2 * PAIRS):
                nxt = probs(head + 1) if head + 1 < 2 * PAIRS else None
                out.append(grads(head, pending))
                pending = nxt
            for hp in range(PAIRS):
                sl = slice(hp * LANES, (hp + 1) * LANES)
                dqt, dkt, dvt = (jnp.concatenate([out[2 * hp][n], out[2 * hp + 1][n]], axis=0) for n in range(3))
                dq_ref[:, sl] = dqt.T.astype(BF16)
                for w in range(3):
                    slot = lax.rem(b + w + 1, 3)
                    cols = slice(w * QB, (w + 1) * QB)
                    if w == 2:
                        acck[hp, slot] = dkt[:, cols]
                        accv[hp, slot] = dvt[:, cols]
                    else:
                        acck[hp, slot] += dkt[:, cols]
                        accv[hp, slot] += dvt[:, cols]

        @pl.when(b < 2)
        def _():
            block(lax.broadcasted_iota(jnp.int32, (KW, 1), 0) >= (2 - b) * QB)

        @pl.when((b >= 2) & (b < nb))
        def _():
            block(None)

        done = lax.rem(b + 1, 3)
        for hp in range(PAIRS):
            sl = slice(hp * LANES, (hp + 1) * LANES)
            dk_ref[:, sl] = acck[hp, done].T.astype(BF16)
            dv_ref[:, sl] = accv[hp, done].T.astype(BF16)

    def cur(p, b):
        return (jnp.minimum(b, nb - 1), p)

    def window(col0):
        return [pl.BlockSpec((QB, SLAB), functools.partial(
            lambda p, b, back: (jnp.maximum(jnp.minimum(b, nb - 1) - back, 0), col0 + p), back=back))
            for back in (2, 1, 0)]

    def done_block(p, b):
        return (jnp.maximum(b - 2, 0), p)

    tile = pl.BlockSpec((2 * PAIRS, KW, QB), lambda p, b: (p, 0, 0))
    here = pl.BlockSpec((QB, SLAB), cur)
    return _call(
        body, name="attn_bwd", grid=(D_MODEL // SLAB, nb + 2),
        args=(qn, kn, kn, kn, proj, proj, proj, dyattn, y_attn, lse, bias),
        in_specs=[here] + window(0) + window(v_col0)
        + [here, here, pl.BlockSpec((2 * PAIRS, QB), lambda p, b: (p, jnp.minimum(b, nb - 1))), tile],
        out_specs=[here, pl.BlockSpec((QB, SLAB), done_block), pl.BlockSpec((QB, SLAB), done_block), tile],
        out_shape=[jax.ShapeDtypeStruct((t, D_MODEL), BF16)] * 3 + [jax.ShapeDtypeStruct((N_HEADS, KW, QB), F32)],
        scratch_shapes=[pltpu.VMEM((PAIRS, 3, LANES, QB), F32), pltpu.VMEM((PAIRS, 3, LANES, QB), F32)],
        semantics=("parallel", "arbitrary"), rider=rider)


def _qknorm_bwd(proj, dqn, dkn, gq, gk):
    t = proj.shape[0]
    tm = min(t, 512)
    scale = HEAD_DIM ** -0.5

    def body(q_ref, k_ref, dqn_ref, dkn_ref, gq_ref, gk_ref, o_ref, dgq_ref, dgk_ref):
        e = _head_sum_matrix()

        @pl.when(pl.program_id(0) == 0)
        def _():
            dgq_ref[...] = jnp.zeros_like(dgq_ref)
            dgk_ref[...] = jnp.zeros_like(dgk_ref)

        for n, (src, dn_ref, g_ref, dg_ref, sc) in enumerate(
                ((q_ref, dqn_ref, gq_ref, dgq_ref, scale), (k_ref, dkn_ref, gk_ref, dgk_ref, 1.0))):
            for s in range(D_MODEL // LANES):
                sl = slice(s * LANES, (s + 1) * LANES)
                xf = src[:, sl].astype(F32)
                r = lax.rsqrt(_head_sums(xf * xf, e) * (1.0 / HEAD_DIM) + EPS)
                xh = xf * r
                dn = dn_ref[:, sl].astype(F32) * sc
                dg_ref[:, sl] += _fold8(dn * xh)
                dxh = dn * g_ref[:, sl]
                mean = _head_sums(dxh * xh, e) * (1.0 / HEAD_DIM)
                o_ref[:, n * D_MODEL + s * LANES:n * D_MODEL + (s + 1) * LANES] = (r * (dxh - xh * mean)).astype(BF16)

    row = pl.BlockSpec((tm, D_MODEL), lambda i: (i, 0))
    vec = pl.BlockSpec((1, D_MODEL), lambda i: (0, 0))
    acc = pl.BlockSpec((SUBLANES, D_MODEL), lambda i: (0, 0))
    return pl.pallas_call(
        body, name="qknorm_bwd", grid=(t // tm,),
        in_specs=[row, pl.BlockSpec((tm, D_MODEL), lambda i: (i, 1)), row, row, vec, vec],
        out_specs=[pl.BlockSpec((tm, 2 * D_MODEL), lambda i: (i, 0)), acc, acc],
        out_shape=[jax.ShapeDtypeStruct((t, 2 * D_MODEL), BF16)] + [jax.ShapeDtypeStruct((SUBLANES, D_MODEL), F32)] * 2,
        compiler_params=_params(("arbitrary",)),
    )(proj, proj, dqn, dkn, gq, gk)


def _in_bwd(dqk, dv, dconv, w_in, dhg, x, g1, dx1, rider=None):
    t = x.shape[0]
    tm = min(t, 512)

    def body(dqk_ref, dv_ref, dc_ref, w_ref, dhg_ref, x_ref, g_ref, dx1_ref, dx_ref, dg_ref):
        acc = dhg_ref[...]
        slab = 0
        for src, n in ((dqk_ref, 2), (dv_ref, 1), (dc_ref, 3)):
            for s in range(n):
                acc = acc + _dot_nt(src[:, s * D_MODEL:(s + 1) * D_MODEL],
                                    w_ref[:, slab * D_MODEL:(slab + 1) * D_MODEL])
                slab += 1

        @pl.when(pl.program_id(0) == 0)
        def _():
            dg_ref[...] = jnp.zeros_like(dg_ref)

        dx, dg = _rmsnorm_bwd(x_ref[...], g_ref[...], acc)
        dx_ref[...] = dx1_ref[...] + dx
        dg_ref[...] += _fold8(dg)

    row = pl.BlockSpec((tm, D_MODEL), lambda i: (i, 0))
    return _call(
        body, name="in_bwd", grid=(t // tm,), args=(dqk, dv, dconv, w_in, dhg, x, g1, dx1),
        in_specs=[pl.BlockSpec((tm, 2 * D_MODEL), lambda i: (i, 0)), row,
                  pl.BlockSpec((tm, 3 * D_MODEL), lambda i: (i, 0)),
                  _resident(w_in.shape), row, row, pl.BlockSpec((1, D_MODEL), lambda i: (0, 0)), row],
        out_specs=[row, pl.BlockSpec((SUBLANES, D_MODEL), lambda i: (0, 0))],
        out_shape=[jax.ShapeDtypeStruct((t, D_MODEL), F32), jax.ShapeDtypeStruct((SUBLANES, D_MODEL), F32)],
        semantics=("arbitrary",), rider=rider)


def _bias_grad_fold(dbias, rider=None):
    def body(d_ref, o_ref):
        jj = lax.broadcasted_iota(jnp.int32, (QB, QB), 0)
        ii = lax.broadcasted_iota(jnp.int32, (QB, QB), 1)
        flip = (jj + ii == QB - 1).astype(BF16)
        low = jj + ii < QB
        pos, neg = [], []
        for w in range(KW // QB):
            x = d_ref[0, QB * w:QB * (w + 1), :]
            hi = x.astype(BF16)
            r1 = x - hi.astype(F32)
            mid = r1.astype(BF16)
            lo = (r1 - mid.astype(F32)).astype(BF16)
            xr = _dot(hi, flip) + _dot(mid, flip) + _dot(lo, flip)
            for keep, acc in ((low, pos), (jnp.logical_not(low), neg)):
                part = pltpu.roll(jnp.where(keep, xr, 0.0), 0, 1, stride=1, stride_axis=0)
                acc.append(jnp.sum(part, axis=0, keepdims=True))
        far = pos[1] + neg[0] + pos[0]
        o_ref[0] = jnp.zeros((SUBLANES, QB), F32)
        o_ref[0, 0:1, :] = neg[2]
        o_ref[0, 1:2, :] = pos[2] + neg[1]
        o_ref[0, 2:3, :] = jnp.broadcast_to(jnp.sum(far, axis=-1, keepdims=True), (1, QB))

    return _call(
        body, name="bias_grad_fold", grid=(N_HEADS,), args=(dbias,),
        in_specs=[pl.BlockSpec((1, KW, QB), lambda h: (h, 0, 0))],
        out_specs=[pl.BlockSpec((1, SUBLANES, QB), lambda h: (h, 0, 0))],
        out_shape=[jax.ShapeDtypeStruct((N_HEADS, SUBLANES, QB), F32)],
        semantics=("parallel",), rider=rider)


def _small_partials(dg1, dgq, dgk, dcb, dcw, dbg, dg2, dbias_fold, loss_tile):
    def head_fold(v):
        acc = v[:, 0:LANES]
        for s in range(1, D_MODEL // LANES):
            acc = acc + v[:, s * LANES:(s + 1) * LANES]
        return acc + pltpu.roll(acc, HEAD_DIM, 1)

    def body(dg1_ref, dgq_ref, dgk_ref, dcb_ref, dcw_ref, dbg_ref, dg2_ref, db_ref, loss_ref, o_ref):
        o_ref[...] = jnp.zeros_like(o_ref)
        o_ref[0:1, :] = jnp.sum(dg1_ref[...], axis=0, keepdims=True)
        o_ref[1:2, 0:LANES] = head_fold(jnp.sum(dgq_ref[...], axis=0, keepdims=True))
        o_ref[2:3, 0:LANES] = head_fold(jnp.sum(dgk_ref[...], axis=0, keepdims=True))
        o_ref[3:4, :] = jnp.sum(dcb_ref[...], axis=0, keepdims=True)
        for j in range(3):
            o_ref[4 + j:5 + j, :] = jnp.sum(dcw_ref[j * SUBLANES:(j + 1) * SUBLANES, :], axis=0, keepdims=True)
        o_ref[7:8, :] = jnp.sum(dbg_ref[:, :D_MODEL], axis=0, keepdims=True)
        o_ref[8:9, :] = jnp.sum(dbg_ref[:, D_MODEL:], axis=0, keepdims=True)
        o_ref[9:10, :] = jnp.sum(dg2_ref[...], axis=0, keepdims=True)
        for h in range(N_HEADS):
            for part in range(3):
                o_ref[10 + h:11 + h, part * QB:(part + 1) * QB] = db_ref[h, part:part + 1, :]
        loss = (0.5 / D_MODEL) * jnp.sum(jnp.sum(loss_ref[...], axis=0, keepdims=True), axis=-1, keepdims=True)
        o_ref[26:27, :] = jnp.broadcast_to(loss, (1, D_MODEL))

    return pl.pallas_call(
        body, name="small_partials",
        out_shape=jax.ShapeDtypeStruct((32, D_MODEL), F32),
        compiler_params=_params(),
    )(dg1, dgq, dgk, dcb, dcw, dbg, dg2, dbias_fold, loss_tile)


MID_AXES = (0, 0, 1, 0)
MLP_AXES = (1, 0)


def _local_step(x, target, norm1_g, q_norm_g, k_norm_g, bias, conv_w, conv_b, b_gate, norm2_g,
                w_in, mid_w, mlp_w, distributed):
    g1 = norm1_g.reshape(1, D_MODEL)
    g2 = norm2_g.reshape(1, D_MODEL)
    gq = jnp.tile(q_norm_g, N_HEADS).reshape(1, D_MODEL)
    gk = jnp.tile(k_norm_g, N_HEADS).reshape(1, D_MODEL)
    cb = conv_b.reshape(1, D_MODEL)

    (proj, h), got = _in_proj(x, g1, w_in, rider=_Gather(mid_w, MID_AXES) if distributed else None)
    w_ap, w_cp, w_g, w_out = got if distributed else mid_w
    (gates,), _ = _proj("gate_proj", h, w_g, b=b_gate.reshape(1, 2 * D_MODEL))
    qn, kn = _qknorm_fwd(proj, gq, gk)
    (y_attn, lse), got = _attn_fwd(qn, kn, proj, bias, rider=_Gather(mlp_w, MLP_AXES) if distributed else None)
    w_up, w_down = got if distributed else mlp_w
    y_conv = _conv_fwd(proj, conv_w, cb)
    ya, yc, merged, x1, h2 = _mix_out(y_attn, y_conv, gates, x, w_ap, w_cp, w_out, g2)
    a, dy, dyb, loss_tile = _mlp_fwd(h2, w_up, w_down, x1, target)

    da, dx1, dx1b, dg2 = _mlp_bwd(dyb, a, w_down, w_up, x1, dy, g2)
    gw_down = _wgrad("wgrad_down", a, [dyb], [1], relu_sq=True)
    gw_up = _wgrad("wgrad_up", h2, [da], [D_FF // D_MODEL])
    (dgp, dya, dyc, dyattn, dyconv, dhg, dbg), mlp_swapped = _mix_bwd(
        dx1b, gates, ya, yc, w_out, w_ap, w_cp, w_g,
        rider=_PairSwap((gw_up, gw_down), MLP_AXES) if distributed else None)
    mid = tuple(_wgrad_group("wgrad_mid", [(y_attn, dya, 1), (y_conv, dyc, 1), (h, dgp, 2), (merged, dx1b, 1)]))
    gw_ap, gw_cp, gw_g, gw_out = mid
    (dconv, dcb, dcw), mid_swapped = _conv_bwd(
        dyconv, proj, conv_w, cb, rider=_PairSwap(mid, MID_AXES) if distributed else None)
    early = mid + (gw_up, gw_down)
    early_sums = (_pair_add(early, tuple(mid_swapped) + tuple(mlp_swapped), MID_AXES + MLP_AXES)
                  if distributed else None)
    (dqn, dkn, dv, dbias), early_shares = _attn_bwd(
        qn, kn, proj, dyattn, y_attn, lse, bias, rider=_ChipScatter(early_sums) if distributed else None)
    dqk, dgq, dgk = _qknorm_bwd(proj, dqn, dkn, gq, gk)
    gw_in = _wgrad("wgrad_in", h, [dqk, dv, dconv], [2, 1, 3])
    (dbias_fold,), in_swapped = _bias_grad_fold(dbias, rider=_PairSwap((gw_in,), (1,)) if distributed else None)
    in_sums = _pair_add((gw_in,), in_swapped, (1,)) if distributed else None
    (dx, dg1), in_shares = _in_bwd(dqk, dv, dconv, w_in, dhg, x, g1, dx1,
                                   rider=_ChipScatter(in_sums) if distributed else None)
    small = _small_partials(dg1, dgq, dgk, dcb, dcw, dbg, dg2, dbias_fold, loss_tile)
    grads = tuple(in_shares) + tuple(early_shares) if distributed else (gw_in,) + early
    return dx, grads, small


def _me():
    return lax.axis_index("x"), lax.axis_index("y"), lax.axis_index("c")


def _peer(me, rel):
    x, y, c = me
    return (1 - x if rel & 4 else x, 1 - y if rel & 2 else y, 1 - c if rel & 1 else c)


def _linear(dev):
    return 4 * dev[0] + 2 * dev[1] + dev[2]


BIG_AXES = (1, 0, 0, 1, 0, 1, 0)


def _block(ref, axis, idx, size):
    return ref.at[pl.ds(idx * size, size), :] if axis == 0 else ref.at[:, pl.ds(idx * size, size)]


def _cast_shards(shards):
    def body(*refs):
        for src, dst in zip(refs[:len(shards)], refs[len(shards):]):
            dst[...] = src[...].astype(BF16)

    return pl.pallas_call(
        body, name="cast_shards",
        out_shape=[jax.ShapeDtypeStruct(s.shape, BF16) for s in shards],
        compiler_params=_params(),
    )(*shards)


class _Gather:
    def __init__(self, shards, axes):
        self.arrays, self.axes, self.n = list(shards), tuple(axes), len(shards)
        self.sizes = [s.shape[axis] for s, axis in zip(shards, axes)]
        self.out_shape = []
        for s, axis in zip(shards, axes):
            shape = (s.shape[0] * N_DEV, s.shape[1]) if axis == 0 else (s.shape[0], s.shape[1] * N_DEV)
            self.out_shape.append(jax.ShapeDtypeStruct(shape, s.dtype))
        self.scratch = [pltpu.SemaphoreType.DMA((self.n, 7)), pltpu.SemaphoreType.DMA((self.n, 7)),
                        pltpu.SemaphoreType.DMA((self.n,))]

    def _copies(self, srcs, outs, sems):
        send_sems, recv_sems, local_sems = sems
        me = _me()
        sibling = _peer(me, 1)
        chips = [_peer(me, rel) for rel in (4, 2, 6)]

        def rows(a, dev):
            return _block(outs[a], self.axes[a], _linear(dev), self.sizes[a])

        def copy(a, k, block_dev, to, src=None):
            return pltpu.make_async_remote_copy(
                src_ref=rows(a, block_dev) if src is None else src, dst_ref=rows(a, block_dev),
                send_sem=send_sems.at[a, k], recv_sem=recv_sems.at[a, k], device_id=to, device_id_type=MESH_T)

        own = [pltpu.make_async_copy(srcs[a], rows(a, me), local_sems.at[a]) for a in range(self.n)]
        first = []
        for a in range(self.n):
            first.append(copy(a, 0, me, sibling, src=srcs[a]))
            for j, chip in enumerate(chips):
                first.append(copy(a, 1 + j, me, chip, src=srcs[a]))
        return me, sibling, chips, copy, own, first

    def start(self, srcs, outs, sems):
        _, _, _, _, own, first = self._copies(srcs, outs, sems)
        for cp in own + first:
            cp.start()

    def finish(self, srcs, outs, sems):
        me, sibling, chips, copy, own, first = self._copies(srcs, outs, sems)
        passed = []
        for a in range(self.n):
            for j, chip in enumerate(chips):
                copy(a, 1 + j, chip, me).wait_recv()
                fwd = copy(a, 4 + j, chip, sibling)
                fwd.start()
                passed.append(fwd)
        for a in range(self.n):
            copy(a, 0, sibling, me).wait_recv()
            for j, chip in enumerate(chips):
                copy(a, 4 + j, _peer(chip, 1), me).wait_recv()
        for cp in first + passed:
            cp.wait_send()
        for cp in own:
            cp.wait()


N_CHIPS = 4


def _shard_shape(g, axis):
    return (g.shape[0] // N_DEV, g.shape[1]) if axis == 0 else (g.shape[0], g.shape[1] // N_DEV)


class _PairSwap:
    def __init__(self, grads, axes):
        self.arrays, self.axes, self.n = list(grads), tuple(axes), len(grads)
        self.sizes = [g.shape[axis] // N_DEV for g, axis in zip(grads, axes)]
        self.out_shape = [jax.ShapeDtypeStruct((N_CHIPS,) + _shard_shape(g, axis), g.dtype)
                          for g, axis in zip(grads, axes)]
        self.scratch = [pltpu.SemaphoreType.DMA((self.n, N_CHIPS)), pltpu.SemaphoreType.DMA((self.n, N_CHIPS))]

    def _copies(self, srcs, outs, sems):
        send_sems, recv_sems = sems
        x, y, c = _me()
        sibling = (x, y, 1 - c)
        copies = []
        for a in range(self.n):
            for chip in range(N_CHIPS):
                owner_idx = 2 * chip + (1 - c)
                copies.append(pltpu.make_async_remote_copy(
                    src_ref=_block(srcs[a], self.axes[a], owner_idx, self.sizes[a]), dst_ref=outs[a].at[chip],
                    send_sem=send_sems.at[a, chip], recv_sem=recv_sems.at[a, chip],
                    device_id=sibling, device_id_type=MESH_T))
        return copies

    def start(self, srcs, outs, sems):
        for cp in self._copies(srcs, outs, sems):
            cp.start()

    def finish(self, srcs, outs, sems):
        for cp in self._copies(srcs, outs, sems):
            cp.wait()


def _pair_add(grads, swapped, axes):
    n = len(grads)
    c_arr = lax.axis_index("c").astype(jnp.int32).reshape(1)

    def body(c_ref, *refs):
        del c_ref
        mine, got, outs = refs[:n], refs[n:2 * n], refs[2 * n:]
        for a in range(n):
            outs[a][0] = (mine[a][...].astype(F32) + got[a][0].astype(F32)).astype(BF16)

    in_specs, out_specs, out_shape = [], [], []
    for g, axis in zip(grads, axes):
        shard = _shard_shape(g, axis)
        if axis == 0:
            in_specs.append(pl.BlockSpec(shard, lambda s, c_ref: (2 * s + c_ref[0], 0)))
        else:
            in_specs.append(pl.BlockSpec(shard, lambda s, c_ref: (0, 2 * s + c_ref[0])))
    for g, axis in zip(grads, axes):
        shard = _shard_shape(g, axis)
        in_specs.append(pl.BlockSpec((1,) + shard, lambda s, c_ref: (s, 0, 0)))
        out_specs.append(pl.BlockSpec((1,) + shard, lambda s, c_ref: (s, 0, 0)))
        out_shape.append(jax.ShapeDtypeStruct((N_CHIPS,) + shard, BF16))
    return pl.pallas_call(
        body, name="pair_add_" + str(n),
        grid_spec=pltpu.PrefetchScalarGridSpec(num_scalar_prefetch=1, grid=(N_CHIPS,), in_specs=in_specs,
                                               out_specs=out_specs),
        out_shape=out_shape, compiler_params=_params(("arbitrary",)),
    )(c_arr, *grads, *swapped)


class _ChipScatter:
    def __init__(self, sums):
        self.arrays, self.n = list(sums), len(sums)
        self.out_shape = [jax.ShapeDtypeStruct(s.shape, s.dtype) for s in sums]
        self.scratch = [pltpu.SemaphoreType.DMA((self.n, 3)), pltpu.SemaphoreType.DMA((self.n, 3)),
                        pltpu.SemaphoreType.DMA((self.n,))]

    def _copies(self, srcs, outs, sems):
        send_sems, recv_sems, local_sems = sems
        me = _me()
        my_chip = 2 * me[0] + me[1]
        own = [pltpu.make_async_copy(srcs[a].at[my_chip], outs[a].at[my_chip], local_sems.at[a])
               for a in range(self.n)]
        sends, recvs = [], []
        for a in range(self.n):
            for k, rel in enumerate((4, 2, 6)):
                peer = _peer(me, rel)
                peer_chip = 2 * peer[0] + peer[1]
                sends.append(pltpu.make_async_remote_copy(
                    src_ref=srcs[a].at[peer_chip], dst_ref=outs[a].at[my_chip],
                    send_sem=send_sems.at[a, k], recv_sem=recv_sems.at[a, k], device_id=peer, device_id_type=MESH_T))
                recvs.append(pltpu.make_async_remote_copy(
                    src_ref=srcs[a].at[my_chip], dst_ref=outs[a].at[peer_chip],
                    send_sem=send_sems.at[a, k], recv_sem=recv_sems.at[a, k], device_id=peer, device_id_type=MESH_T))
        return own, sends, recvs

    def start(self, srcs, outs, sems):
        own, sends, _ = self._copies(srcs, outs, sems)
        for cp in own + sends:
            cp.start()

    def finish(self, srcs, outs, sems):
        own, sends, recvs = self._copies(srcs, outs, sems)
        for cp in recvs:
            cp.wait_recv()
        for cp in sends:
            cp.wait_send()
        for cp in own:
            cp.wait()


def _call(body, *, name, args, in_specs, out_specs, out_shape, grid=(), scratch_shapes=(), semantics=None,
          rider=None):
    if rider is None:
        return pl.pallas_call(
            body, name=name, grid=grid, in_specs=in_specs, out_specs=out_specs, out_shape=out_shape,
            scratch_shapes=list(scratch_shapes), compiler_params=_params(semantics))(*args), None
    n_in, n_out, n_scr, r = len(in_specs), len(out_specs), len(scratch_shapes), rider.n

    def wrapped(*refs):
        ins, r_ins = refs[:n_in], refs[n_in:n_in + r]
        outs = refs[n_in + r:n_in + r + n_out]
        r_outs = refs[n_in + r + n_out:n_in + 2 * r + n_out]
        scr = refs[n_in + 2 * r + n_out:n_in + 2 * r + n_out + n_scr]
        sems = refs[n_in + 2 * r + n_out + n_scr:]
        first, last = None, None
        for ax in range(len(grid)):
            f, l = pl.program_id(ax) == 0, pl.program_id(ax) == pl.num_programs(ax) - 1
            first = f if first is None else first & f
            last = l if last is None else last & l
        if first is None:
            rider.start(r_ins, r_outs, sems)
            body(*ins, *outs, *scr)
            rider.finish(r_ins, r_outs, sems)
            return

        @pl.when(first)
        def _():
            rider.start(r_ins, r_outs, sems)

        body(*ins, *outs, *scr)

        @pl.when(last)
        def _():
            rider.finish(r_ins, r_outs, sems)

    any_spec = pl.BlockSpec(memory_space=pl.ANY)
    out = pl.pallas_call(
        wrapped, name=name, grid=grid, in_specs=list(in_specs) + [any_spec] * r,
        out_specs=list(out_specs) + [any_spec] * r, out_shape=list(out_shape) + rider.out_shape,
        scratch_shapes=list(scratch_shapes) + rider.scratch,
        compiler_params=_params(None if semantics is None else ("arbitrary",) * len(semantics)),
    )(*args, *rider.arrays)
    return out[:n_out], out[n_out:]


def _exchange(name, rider):
    def body():
        pass

    return _call(body, name=name, args=(), in_specs=[], out_specs=[], out_shape=[], rider=rider)[1]


def _all_reduce_small(part):
    def body(p_ref, o_ref, slots, send_sems, recv_sems):
        me = _me()
        my_idx = _linear(me)
        slots[my_idx] = p_ref[...]
        sends = []
        for rel in range(1, N_DEV):
            cp = pltpu.make_async_remote_copy(
                src_ref=p_ref, dst_ref=slots.at[my_idx], send_sem=send_sems.at[rel - 1],
                recv_sem=recv_sems.at[rel - 1], device_id=_peer(me, rel), device_id_type=MESH_T)
            cp.start()
            sends.append(cp)
        for rel in range(1, N_DEV):
            frm = _peer(me, rel)
            pltpu.make_async_remote_copy(
                src_ref=p_ref, dst_ref=slots.at[_linear(frm)], send_sem=send_sems.at[rel - 1],
                recv_sem=recv_sems.at[rel - 1], device_id=frm, device_id_type=MESH_T).wait_recv()
        for cp in sends:
            cp.wait_send()
        total = slots[0]
        for d in range(1, N_DEV):
            total = total + slots[d]
        o_ref[...] = total

    return pl.pallas_call(
        body, name="all_reduce_small",
        in_specs=[pl.BlockSpec(memory_space=pltpu.VMEM)], out_specs=pl.BlockSpec(memory_space=pltpu.VMEM),
        out_shape=jax.ShapeDtypeStruct(part.shape, F32),
        scratch_shapes=[pltpu.VMEM((N_DEV,) + part.shape, F32), pltpu.SemaphoreType.DMA((7,)),
                        pltpu.SemaphoreType.DMA((7,))],
        compiler_params=_params(),
    )(part)


def _adamw_math(w, g, m, v):
    m = ADAM_B1 * m + (1.0 - ADAM_B1) * g
    v = ADAM_B2 * v + (1.0 - ADAM_B2) * jnp.square(g)
    m_hat = m / (1.0 - ADAM_B1 ** ADAM_STEP)
    v_hat = v / (1.0 - ADAM_B2 ** ADAM_STEP)
    delta = -ADAM_LR * (m_hat / (jnp.sqrt(v_hat) + ADAM_EPS) + ADAM_WD * w)
    return delta, m, v


ADAMW_STEPS = 4


def _adamw_big(shares, ws, ms, vs):
    n = len(ws)

    def body(*refs):
        s_refs, w_refs, m_refs, v_refs = (refs[a * n:(a + 1) * n] for a in range(4))
        outs = refs[4 * n:]
        for a in range(n):
            g = s_refs[a][0].astype(F32)
            for d in range(1, N_CHIPS):
                g = g + s_refs[a][d].astype(F32)
            outs[4 * a][...] = g
            outs[4 * a + 1][...], outs[4 * a + 2][...], outs[4 * a + 3][...] = _adamw_math(
                w_refs[a][...], g, m_refs[a][...], v_refs[a][...])

    def chunk(w):
        return pl.BlockSpec((w.shape[0] // ADAMW_STEPS, w.shape[1]), lambda i: (i, 0))

    def share_chunk(w):
        return pl.BlockSpec((N_CHIPS, w.shape[0] // ADAMW_STEPS, w.shape[1]), lambda i: (0, i, 0))

    out = pl.pallas_call(
        body, name="adamw_big", grid=(ADAMW_STEPS,),
        in_specs=[share_chunk(w) for w in ws] + [chunk(w) for w in ws] * 3,
        out_specs=[chunk(w) for w in ws for _ in range(4)],
        out_shape=[jax.ShapeDtypeStruct(w.shape, F32) for w in ws for _ in range(4)],
        compiler_params=_params(("parallel",)),
    )(*shares, *ws, *ms, *vs)
    return [tuple(out[4 * a:4 * a + 4]) for a in range(n)]


def _adamw_small(quads):
    n = len(quads)

    def body(*refs):
        ins, outs = refs[:4 * n], refs[4 * n:]
        for p in range(n):
            g_ref, w_ref, m_ref, v_ref = ins[4 * p:4 * p + 4]
            d_ref, nm_ref, nv_ref = outs[3 * p:3 * p + 3]
            d_ref[...], nm_ref[...], nv_ref[...] = _adamw_math(w_ref[...], g_ref[...], m_ref[...], v_ref[...])

    flat = [a for quad in quads for a in quad]
    out = pl.pallas_call(
        body, name="adamw_small",
        out_shape=[jax.ShapeDtypeStruct(quad[1].shape, F32) for quad in quads for _ in range(3)],
        compiler_params=_params(),
    )(*flat)
    return [tuple(out[3 * p:3 * p + 3]) for p in range(n)]


def kernel(x, norm1_g, w_in, q_norm_g, k_norm_g, rel_bias, conv_w, conv_b, w_attn_proj, w_conv_proj, w_gate, b_gate, w_out, norm2_g, w_up, w_down, loss_target, m_norm1_g, m_w_in, m_q_norm_g, m_k_norm_g, m_rel_bias, m_conv_w, m_conv_b, m_w_attn_proj, m_w_conv_proj, m_w_gate, m_b_gate, m_w_out, m_norm2_g, m_w_up, m_w_down, v_norm1_g, v_w_in, v_q_norm_g, v_k_norm_g, v_rel_bias, v_conv_w, v_conv_b, v_w_attn_proj, v_w_conv_proj, v_w_gate, v_b_gate, v_w_out, v_norm2_g, v_w_up, v_w_down):
    my_idx = _linear(_me())
    big_w = (w_in, w_attn_proj, w_conv_proj, w_gate, w_out, w_up, w_down)
    big_m = (m_w_in, m_w_attn_proj, m_w_conv_proj, m_w_gate, m_w_out, m_w_up, m_w_down)
    big_v = (v_w_in, v_w_attn_proj, v_w_conv_proj, v_w_gate, v_w_out, v_w_up, v_w_down)
    big_names = ("w_in", "w_attn_proj", "w_conv_proj", "w_gate", "w_out", "w_up", "w_down")

    conv_w_tile = jnp.pad(conv_w, ((0, SUBLANES - conv_w.shape[0]), (0, 0)))
    shards = _cast_shards(big_w)
    (bias,), (w_in_full, conv_w_rows) = _bias_tiles(rel_bias, rider=_Gather((shards[0], conv_w_tile), (1, 1)))

    dx, shares, small = _local_step(x[0], loss_target[0], norm1_g, q_norm_g, k_norm_g, bias, conv_w_rows[:3],
                                    conv_b, b_gate, norm2_g, w_in_full, tuple(shards[1:5]), tuple(shards[5:7]), True)

    big_out = _adamw_big(shares, big_w, big_m, big_v)

    tot = _all_reduce_small(small)
    g_rel_bias = jnp.concatenate(
        [tot[10:26, :QB][:, ::-1], tot[10:26, QB:2 * QB][:, ::-1], tot[10:26, 2 * QB:2 * QB + 1]], axis=1)
    g_conv_w = lax.dynamic_slice(tot[4:7], (0, my_idx * LANES), (3, LANES))
    small_g = [tot[0:1], tot[1:2, :HEAD_DIM], tot[2:3, :HEAD_DIM], g_rel_bias, g_conv_w, tot[3:4],
               tot[7:9].reshape(1, 2 * D_MODEL), tot[9:10]]
    small_w = (norm1_g, q_norm_g, k_norm_g, rel_bias, conv_w, conv_b, b_gate, norm2_g)
    small_m = (m_norm1_g, m_q_norm_g, m_k_norm_g, m_rel_bias, m_conv_w, m_conv_b, m_b_gate, m_norm2_g)
    small_v = (v_norm1_g, v_q_norm_g, v_k_norm_g, v_rel_bias, v_conv_w, v_conv_b, v_b_gate, v_norm2_g)

    def two_d(a):
        return a.reshape(1, -1) if a.ndim == 1 else a

    small_out = _adamw_small([(g, two_d(w), two_d(m), two_d(v))
                              for g, w, m, v in zip(small_g, small_w, small_m, small_v)])

    order = ("norm1_g", "w_in", "q_norm_g", "k_norm_g", "rel_bias", "conv_w", "conv_b", "w_attn_proj", "w_conv_proj",
             "w_gate", "b_gate", "w_out", "norm2_g", "w_up", "w_down")
    small_names = ("norm1_g", "q_norm_g", "k_norm_g", "rel_bias", "conv_w", "conv_b", "b_gate", "norm2_g")
    res = {}
    for name, (g, d, nm, nv) in zip(big_names, big_out):
        res[name] = (g, d, nm, nv)
    for name, g, w, (d, nm, nv) in zip(small_names, small_g, small_w, small_out):
        res[name] = tuple(a.reshape(w.shape) for a in (g, d, nm, nv))
    loss = tot[26, 0]
    return (loss, dx[None], *[res[n][0] for n in order], *[res[n][1] for n in order],
            *[res[n][2] for n in order], *[res[n][3] for n in order])
```

```python
import functools

import jax
import jax.numpy as jnp
from jax import lax
from jax.experimental import pallas as pl
from jax.experimental.pallas import tpu as pltpu

F32 = jnp.float32
BF16 = jnp.bfloat16

D_MODEL = 1024
N_HEADS = 16
HEAD_DIM = 64
CHUNK = 64
N_PREV_CHUNKS = 8
MAX_REL = 256
D_FF = 4096
EPS = 1e-6
NEG_INF = -1e30
LOG2E = 1.4426950408889634
N_DEV = 8

ADAM_LR = 0.001
ADAM_B1 = 0.9
ADAM_B2 = 0.999
ADAM_EPS = 1e-08
ADAM_WD = 0.01
ADAM_STEP = 10

LANES = 128
SUBLANES = 8
VMEM_LIMIT = 56 * 1024 * 1024
QB = 256
KW = 3 * QB
PAIRS = 4
SUB = 128
SLAB = PAIRS * LANES
SKEW = 1024

MESH_T = pl.DeviceIdType.MESH


def _dot(a, b):
    return jnp.dot(a, b, preferred_element_type=F32)


def _dot_nt(a, b):
    return lax.dot_general(a, b, (((1,), (1,)), ((), ())), preferred_element_type=F32)


def _dot_tn(a, b):
    return lax.dot_general(a, b, (((0,), (0,)), ((), ())), preferred_element_type=F32)


def _params(sem=None):
    return pltpu.CompilerParams(dimension_semantics=sem, vmem_limit_bytes=VMEM_LIMIT)


def _resident(shape):
    return pl.BlockSpec(shape, lambda *_: (0,) * len(shape), pipeline_mode=pl.Buffered(1))


def _fold8(v):
    rows, n = v.shape
    return v.reshape(rows // SUBLANES, SUBLANES, n).sum(axis=0)


def _head_sum_matrix():
    r = lax.broadcasted_iota(jnp.int32, (LANES, LANES), 0) // HEAD_DIM
    c = lax.broadcasted_iota(jnp.int32, (LANES, LANES), 1) // HEAD_DIM
    return (r == c).astype(BF16)


def _head_sums(v, e):
    hi = v.astype(BF16)
    lo = (v - hi.astype(F32)).astype(BF16)
    return _dot(hi, e) + _dot(lo, e)


def _in_proj(x, g1, w_in, rider=None):
    t = x.shape[0]
    tm = min(t, 512)
    n_out = w_in.shape[1]

    def body(x_ref, g_ref, w_ref, proj_ref, h_ref):
        xf = x_ref[...]
        r = lax.rsqrt(jnp.mean(xf * xf, axis=-1, keepdims=True) + EPS)
        h = (xf * r * g_ref[...]).astype(BF16)
        h_ref[...] = h
        for k in range(n_out // D_MODEL):
            cols = slice(k * D_MODEL, (k + 1) * D_MODEL)
            proj_ref[:, cols] = _dot(h, w_ref[:, cols]).astype(BF16)

    return _call(
        body, name="in_proj", grid=(t // tm,), args=(x, g1, w_in),
        in_specs=[pl.BlockSpec((tm, D_MODEL), lambda i: (i, 0)),
                  pl.BlockSpec((1, D_MODEL), lambda i: (0, 0)),
                  _resident((D_MODEL, n_out))],
        out_specs=[pl.BlockSpec((tm, n_out), lambda i: (i, 0)),
                   pl.BlockSpec((tm, D_MODEL), lambda i: (i, 0))],
        out_shape=[jax.ShapeDtypeStruct((t, n_out), BF16), jax.ShapeDtypeStruct((t, D_MODEL), BF16)],
        semantics=("parallel",), rider=rider)


def _proj(name, h, w, b=None, rider=None):
    t = h.shape[0]
    tm = min(t, 512)
    n_out = w.shape[1]

    def body(*refs):
        h_ref, w_ref, o_ref = refs[0], refs[1], refs[-1]
        hv = h_ref[...]
        for k in range(n_out // D_MODEL):
            cols = slice(k * D_MODEL, (k + 1) * D_MODEL)
            y = _dot(hv, w_ref[:, cols])
            if b is not None:
                y = jax.nn.sigmoid(y + refs[2][:, cols])
            o_ref[:, cols] = y.astype(BF16)

    in_specs = [pl.BlockSpec((tm, D_MODEL), lambda i: (i, 0)), _resident((D_MODEL, n_out))]
    args = (h, w)
    if b is not None:
        in_specs.append(pl.BlockSpec((1, n_out), lambda i: (0, 0)))
        args = (h, w, b)
    return _call(
        body, name=name, grid=(t // tm,), args=args, in_specs=in_specs,
        out_specs=[pl.BlockSpec((tm, n_out), lambda i: (i, 0))],
        out_shape=[jax.ShapeDtypeStruct((t, n_out), BF16)],
        semantics=("parallel",), rider=rider)


def _qknorm_fwd(proj, gq, gk):
    t = proj.shape[0]
    tm = min(t, 512)
    scale = HEAD_DIM ** -0.5 * LOG2E

    def body(q_ref, k_ref, gq_ref, gk_ref, qn_ref, kn_ref):
        e = _head_sum_matrix()
        for src, g_ref, dst, sc in ((q_ref, gq_ref, qn_ref, scale), (k_ref, gk_ref, kn_ref, 1.0)):
            for s in range(D_MODEL // LANES):
                sl = slice(s * LANES, (s + 1) * LANES)
                xf = src[:, sl].astype(F32)
                r = lax.rsqrt(_head_sums(xf * xf, e) * (1.0 / HEAD_DIM) + EPS)
                dst[:, sl] = (xf * r * g_ref[:, sl] * sc).astype(BF16)

    return pl.pallas_call(
        body, name="qknorm_fwd", grid=(t // tm,),
        in_specs=[pl.BlockSpec((tm, D_MODEL), lambda i: (i, 0)),
                  pl.BlockSpec((tm, D_MODEL), lambda i: (i, 1)),
                  pl.BlockSpec((1, D_MODEL), lambda i: (0, 0)),
                  pl.BlockSpec((1, D_MODEL), lambda i: (0, 0))],
        out_specs=[pl.BlockSpec((tm, D_MODEL), lambda i: (i, 0))] * 2,
        out_shape=[jax.ShapeDtypeStruct((t, D_MODEL), BF16)] * 2,
        compiler_params=_params(("parallel",)),
    )(proj, proj, gq, gk)


def _bias_tiles(rel_bias, rider=None):
    by_dist = jnp.concatenate(
        [rel_bias[:, :2 * MAX_REL], jnp.broadcast_to(rel_bias[:, 2 * MAX_REL:], (N_HEADS, 2 * MAX_REL))], axis=1)
    by_dist = by_dist.reshape(N_HEADS, 1, SKEW)

    def body(f_ref, o_ref):
        jj = lax.broadcasted_iota(jnp.int32, (QB, QB), 0)
        ii = lax.broadcasted_iota(jnp.int32, (QB, QB), 1)
        for w in range(KW // QB):
            pos = jnp.broadcast_to(f_ref[0, :, KW - QB * w:KW - QB * w + QB], (QB, QB))
            neg = jnp.broadcast_to(f_ref[0, :, KW - QB * (w + 1):KW - QB * w], (QB, QB))
            pos = pltpu.roll(pos, 0, 1, stride=1, stride_axis=0)
            neg = pltpu.roll(neg, 0, 1, stride=1, stride_axis=0)
            tile = jnp.where(ii >= jj, pos, neg)
            kc = (jj + QB * w) // CHUNK
            qc = ii // CHUNK
            band = (kc >= qc) & (kc <= qc + N_PREV_CHUNKS)
            o_ref[0, QB * w:QB * (w + 1), :] = jnp.where(band, tile * LOG2E, NEG_INF)

    return _call(
        body, name="bias_tiles", grid=(N_HEADS,), args=(by_dist,),
        in_specs=[pl.BlockSpec((1, 1, SKEW), lambda h: (h, 0, 0))],
        out_specs=[pl.BlockSpec((1, KW, QB), lambda h: (h, 0, 0))],
        out_shape=[jax.ShapeDtypeStruct((N_HEADS, KW, QB), F32)],
        semantics=("parallel",), rider=rider)


def _window_specs(col0):
    return [pl.BlockSpec((QB, SLAB), functools.partial(
        lambda p, b, back: (jnp.maximum(b - back, 0), col0 + p), back=back)) for back in (2, 1, 0)]


def _attn_fwd(qn, kn, proj, bias, rider=None):
    t = qn.shape[0]
    nb = t // QB
    v_col0 = 2 * D_MODEL // SLAB

    def body(q_ref, k0, k1, k2, v0, v1, v2, bias_ref, o_ref, lse_ref):
        b = pl.program_id(1)

        @pl.when(b < 2)
        def _():
            compute(q_ref, k0, k1, k2, v0, v1, v2, bias_ref, o_ref, lse_ref,
                    lax.broadcasted_iota(jnp.int32, (KW, 1), 0) >= (2 - b) * QB)

        @pl.when(b >= 2)
        def _():
            compute(q_ref, k0, k1, k2, v0, v1, v2, bias_ref, o_ref, lse_ref, None)

    def compute(q_ref, k0, k1, k2, v0, v1, v2, bias_ref, o_ref, lse_ref, valid):
        head_a = lax.broadcasted_iota(jnp.int32, (1, LANES), 1) < HEAD_DIM

        def scores(head):
            hp, hh = divmod(head, 2)
            sl = slice(hp * LANES, (hp + 1) * LANES)
            k = jnp.concatenate([k0[:, sl], k1[:, sl], k2[:, sl]], axis=0)
            mine = head_a if hh == 0 else jnp.logical_not(head_a)
            s = _dot_nt(jnp.where(mine, k, jnp.zeros_like(k)), q_ref[:, sl]) + bias_ref[head]
            return s if valid is None else jnp.where(valid, s, NEG_INF)

        def weighted_values(head, s):
            hp, hh = divmod(head, 2)
            sl = slice(hp * LANES, (hp + 1) * LANES)
            v = jnp.concatenate([v0[:, sl], v1[:, sl], v2[:, sl]], axis=0)
            vt = v.astype(F32).T.astype(BF16)[hh * HEAD_DIM:(hh + 1) * HEAD_DIM]
            vt = jnp.concatenate([vt, jnp.ones((SUBLANES, KW), BF16)], axis=0)
            m = jnp.max(s, axis=0, keepdims=True)
            ov = _dot(vt, jnp.exp2(s - m).astype(BF16))
            l = ov[HEAD_DIM:HEAD_DIM + 1]
            return ov[:HEAD_DIM] / l, m + jnp.log2(l)

        outs, lses = [], []
        pending = scores(0)
        for head in range(2 * PAIRS):
            nxt = scores(head + 1) if head + 1 < 2 * PAIRS else None
            o, lse = weighted_values(head, pending)
            outs.append(o)
            lses.append(lse)
            pending = nxt
        for hp in range(PAIRS):
            sl = slice(hp * LANES, (hp + 1) * LANES)
            o_ref[:, sl] = jnp.concatenate([outs[2 * hp], outs[2 * hp + 1]], axis=0).T.astype(BF16)
        lse_ref[...] = jnp.concatenate(lses, axis=0)

    return _call(
        body, name="attn_fwd", grid=(D_MODEL // SLAB, nb), args=(qn, kn, kn, kn, proj, proj, proj, bias),
        in_specs=[pl.BlockSpec((QB, SLAB), lambda p, b: (b, p))] + _window_specs(0) + _window_specs(v_col0)
        + [pl.BlockSpec((2 * PAIRS, KW, QB), lambda p, b: (p, 0, 0))],
        out_specs=[pl.BlockSpec((QB, SLAB), lambda p, b: (b, p)),
                   pl.BlockSpec((2 * PAIRS, QB), lambda p, b: (p, b))],
        out_shape=[jax.ShapeDtypeStruct((t, D_MODEL), BF16), jax.ShapeDtypeStruct((N_HEADS, t), F32)],
        semantics=("parallel", "arbitrary"), rider=rider)


def _shift_down(u, halo, n):
    rows = lax.broadcasted_iota(jnp.int32, (u.shape[0], 1), 0)
    out = pltpu.roll(u, n, 0)
    for j in range(n):
        out = jnp.where(rows == j, halo[SUBLANES - n + j:SUBLANES - n + j + 1, :], out)
    return out


def _shift_up(u, halo, n):
    tm = u.shape[0]
    rows = lax.broadcasted_iota(jnp.int32, (tm, 1), 0)
    out = pltpu.roll(u, tm - n, 0)
    for j in range(n):
        out = jnp.where(rows == tm - n + j, halo[j:j + 1, :], out)
    return out


def _conv_fwd(proj, conv_w, conv_b):
    t = proj.shape[0]
    tm = min(t, 512)
    hb = tm // SUBLANES

    def body(bg_ref, cg_ref, xc_ref, cgh_ref, xch_ref, w_ref, b_ref, o_ref):
        i = pl.program_id(0)
        u = cg_ref[...].astype(F32) * xc_ref[...].astype(F32)
        halo = cgh_ref[...].astype(F32) * xch_ref[...].astype(F32)
        halo = jnp.where(i > 0, halo, 0.0)
        w = w_ref[...]
        s = w[0:1] * _shift_down(u, halo, 2) + w[1:2] * _shift_down(u, halo, 1) + w[2:3] * u
        o_ref[...] = (bg_ref[...].astype(F32) * (b_ref[...] + s)).astype(BF16)

    def prev(col):
        return pl.BlockSpec((SUBLANES, D_MODEL), lambda i: (jnp.maximum(i * hb - 1, 0), col))

    return pl.pallas_call(
        body, name="conv_fwd", grid=(t // tm,),
        in_specs=[pl.BlockSpec((tm, D_MODEL), lambda i: (i, 3)),
                  pl.BlockSpec((tm, D_MODEL), lambda i: (i, 4)),
                  pl.BlockSpec((tm, D_MODEL), lambda i: (i, 5)),
                  prev(4), prev(5),
                  pl.BlockSpec((3, D_MODEL), lambda i: (0, 0)),
                  pl.BlockSpec((1, D_MODEL), lambda i: (0, 0))],
        out_specs=pl.BlockSpec((tm, D_MODEL), lambda i: (i, 0)),
        out_shape=jax.ShapeDtypeStruct((t, D_MODEL), BF16),
        compiler_params=_params(("parallel",)),
    )(proj, proj, proj, proj, proj, conv_w, conv_b)


def _mix_out(y_attn, y_conv, gates, x, w_ap, w_cp, w_out, g2):
    t = x.shape[0]
    tm = min(t, 512)

    def body(ya_in, yc_in, g_ref, x_ref, wap, wcp, wout, g2_ref, ya_ref, yc_ref, mg_ref, x1_ref, h2_ref):
        ya = _dot(ya_in[...], wap[...])
        yc = _dot(yc_in[...], wcp[...])
        ya_ref[...] = ya.astype(BF16)
        yc_ref[...] = yc.astype(BF16)
        merged = (g_ref[:, :D_MODEL].astype(F32) * ya + g_ref[:, D_MODEL:].astype(F32) * yc).astype(BF16)
        mg_ref[...] = merged
        x1 = x_ref[...] + _dot(merged, wout[...])
        x1_ref[...] = x1
        r = lax.rsqrt(jnp.mean(x1 * x1, axis=-1, keepdims=True) + EPS)
        h2_ref[...] = (x1 * r * g2_ref[...]).astype(BF16)

    row = pl.BlockSpec((tm, D_MODEL), lambda i: (i, 0))
    full = _resident((D_MODEL, D_MODEL))
    return pl.pallas_call(
        body, name="mix_out", grid=(t // tm,),
        in_specs=[row, row, pl.BlockSpec((tm, 2 * D_MODEL), lambda i: (i, 0)), row, full, full, full,
                  pl.BlockSpec((1, D_MODEL), lambda i: (0, 0))],
        out_specs=[row] * 5,
        out_shape=[jax.ShapeDtypeStruct((t, D_MODEL), BF16)] * 3
        + [jax.ShapeDtypeStruct((t, D_MODEL), F32), jax.ShapeDtypeStruct((t, D_MODEL), BF16)],
        compiler_params=_params(("parallel",)),
    )(y_attn, y_conv, gates, x, w_ap, w_cp, w_out, g2)


def _mlp_fwd(h2, w_up, w_down, x1, target):
    t = h2.shape[0]
    tm = min(t, 512)
    tf = 1024
    nf = D_FF // tf

    def body(h2_ref, wup, wdn, x1_ref, tg_ref, a_ref, dy_ref, dyb_ref, loss_ref):
        h2v = h2_ref[...]
        acc = None
        pending = _dot(h2v, wup[:, 0:tf])
        for j in range(nf):
            cols = slice(j * tf, (j + 1) * tf)
            a = pending
            if j + 1 < nf:
                pending = _dot(h2v, wup[:, (j + 1) * tf:(j + 2) * tf])
            a_ref[:, cols] = a.astype(BF16)
            part = _dot(jnp.square(jnp.maximum(a, 0.0)).astype(BF16), wdn[cols, :])
            acc = part if acc is None else acc + part

        @pl.when(pl.program_id(0) == 0)
        def _():
            loss_ref[...] = jnp.zeros_like(loss_ref)

        diff = x1_ref[...] + acc - tg_ref[...]
        loss_ref[...] += _fold8(diff * diff)
        dy = diff * (1.0 / D_MODEL)
        dy_ref[...] = dy
        dyb_ref[...] = dy.astype(BF16)

    row = pl.BlockSpec((tm, D_MODEL), lambda i: (i, 0))
    return pl.pallas_call(
        body, name="mlp_fwd", grid=(t // tm,),
        in_specs=[row, _resident((D_MODEL, D_FF)), _resident((D_FF, D_MODEL)), row, row],
        out_specs=[pl.BlockSpec((tm, D_FF), lambda i: (i, 0)), row, row,
                   pl.BlockSpec((SUBLANES, D_MODEL), lambda i: (0, 0))],
        out_shape=[jax.ShapeDtypeStruct((t, D_FF), BF16), jax.ShapeDtypeStruct((t, D_MODEL), F32),
                   jax.ShapeDtypeStruct((t, D_MODEL), BF16), jax.ShapeDtypeStruct((SUBLANES, D_MODEL), F32)],
        compiler_params=_params(("arbitrary",)),
    )(h2, w_up, w_down, x1, target)


def _rmsnorm_bwd(xf, g, dh):
    r = lax.rsqrt(jnp.mean(xf * xf, axis=-1, keepdims=True) + EPS)
    xh = xf * r
    dxh = dh * g
    dx = r * (dxh - xh * jnp.mean(dxh * xh, axis=-1, keepdims=True))
    return dx, dh * xh


def _mlp_bwd(dyb, a, w_down, w_up, x1, dy, g2):
    t = dyb.shape[0]
    tm = min(t, 512)
    tf = 1024
    nf = D_FF // tf

    def body(dyb_ref, a_ref, wdn, wup, x1_ref, dy_ref, g2_ref, da_ref, dx1_ref, dx1b_ref, dg2_ref):
        dyv = dyb_ref[...]
        acc = None
        pending = _dot_nt(dyv, wdn[0:tf, :])
        for j in range(nf):
            cols = slice(j * tf, (j + 1) * tf)
            du = pending
            if j + 1 < nf:
                pending = _dot_nt(dyv, wdn[(j + 1) * tf:(j + 2) * tf, :])
            da = (du * (2.0 * jnp.maximum(a_ref[:, cols].astype(F32), 0.0))).astype(BF16)
            da_ref[:, cols] = da
            part = _dot_nt(da, wup[:, cols])
            acc = part if acc is None else acc + part

        @pl.when(pl.program_id(0) == 0)
        def _():
            dg2_ref[...] = jnp.zeros_like(dg2_ref)

        dx, dg = _rmsnorm_bwd(x1_ref[...], g2_ref[...], acc)
        dx1 = dy_ref[...] + dx
        dx1_ref[...] = dx1
        dx1b_ref[...] = dx1.astype(BF16)
        dg2_ref[...] += _fold8(dg)

    row = pl.BlockSpec((tm, D_MODEL), lambda i: (i, 0))
    wide = pl.BlockSpec((tm, D_FF), lambda i: (i, 0))
    return pl.pallas_call(
        body, name="mlp_bwd", grid=(t // tm,),
        in_specs=[row, wide, _resident((D_FF, D_MODEL)), _resident((D_MODEL, D_FF)), row, row,
                  pl.BlockSpec((1, D_MODEL), lambda i: (0, 0))],
        out_specs=[wide, row, row, pl.BlockSpec((SUBLANES, D_MODEL), lambda i: (0, 0))],
        out_shape=[jax.ShapeDtypeStruct((t, D_FF), BF16), jax.ShapeDtypeStruct((t, D_MODEL), F32),
                   jax.ShapeDtypeStruct((t, D_MODEL), BF16), jax.ShapeDtypeStruct((SUBLANES, D_MODEL), F32)],
        compiler_params=_params(("arbitrary",)),
    )(dyb, a, w_down, w_up, x1, dy, g2)


def _wgrad(name, lhs, rhs_list, rhs_slabs, relu_sq=False):
    t, m = lhs.shape
    tt = min(t, 2048)
    tmo = min(m, 1024)
    n_slab = sum(rhs_slabs)
    starts = [sum(rhs_slabs[:n]) for n in range(len(rhs_slabs))]
    n_rhs = len(rhs_list)

    def body(*refs):
        l_ref, r_refs, o_ref, acc = refs[0], refs[1:1 + n_rhs], refs[1 + n_rhs], refs[2 + n_rhs]
        k, s = pl.program_id(1), pl.program_id(2)
        lv = l_ref[...]
        if relu_sq:
            lv = jnp.square(jnp.maximum(lv.astype(F32), 0.0)).astype(BF16)

        @pl.when(s == 0)
        def _():
            acc[...] = jnp.zeros_like(acc)

        for n in range(n_rhs):
            @pl.when((k >= starts[n]) & (k < starts[n] + rhs_slabs[n]))
            def _(n=n):
                acc[...] += _dot_tn(lv, r_refs[n][...])

        @pl.when(s == pl.num_programs(2) - 1)
        def _():
            o_ref[...] = acc[...].astype(BF16)

    def rhs_spec(n):
        lo, cnt = starts[n], rhs_slabs[n]

        def index(i, k, s):
            inside = (k >= lo) & (k < lo + cnt)
            return (jnp.where(inside, s, 0), jnp.clip(k - lo, 0, cnt - 1))
        return pl.BlockSpec((tt, D_MODEL), index)

    return pl.pallas_call(
        body, name=name, grid=(m // tmo, n_slab, t // tt),
        in_specs=[pl.BlockSpec((tt, tmo), lambda i, k, s: (s, i))] + [rhs_spec(n) for n in range(n_rhs)],
        out_specs=pl.BlockSpec((tmo, D_MODEL), lambda i, k, s: (i, k)),
        out_shape=jax.ShapeDtypeStruct((m, n_slab * D_MODEL), BF16),
        scratch_shapes=[pltpu.VMEM((tmo, D_MODEL), F32)],
        compiler_params=_params(("parallel", "parallel", "arbitrary")),
    )(lhs, *rhs_list)


def _wgrad_group(name, triples):
    t = triples[0][0].shape[0]
    tt = min(t, 1024)
    counts = [n for _, _, n in triples]
    starts = [sum(counts[:n]) for n in range(len(counts))]
    n_prod = len(triples)

    def inside(n, k):
        return (k >= starts[n]) & (k < starts[n] + counts[n])

    def body(*refs):
        l_refs, r_refs, o_refs = refs[:n_prod], refs[n_prod:2 * n_prod], refs[2 * n_prod:3 * n_prod]
        acc = refs[3 * n_prod]
        k, s = pl.program_id(0), pl.program_id(1)

        @pl.when(s == 0)
        def _():
            acc[...] = jnp.zeros_like(acc)

        for n in range(n_prod):
            @pl.when(inside(n, k))
            def _(n=n):
                acc[...] += _dot_tn(l_refs[n][...], r_refs[n][...])

            @pl.when(inside(n, k) & (s == pl.num_programs(1) - 1))
            def _(n=n):
                o_refs[n][...] = acc[...].astype(BF16)

    def lhs_spec(n):
        return pl.BlockSpec((tt, D_MODEL), lambda k, s: (jnp.where(inside(n, k), s, 0), 0))

    def rhs_spec(n):
        return pl.BlockSpec((tt, D_MODEL), lambda k, s: (jnp.where(inside(n, k), s, 0),
                                                         jnp.clip(k - starts[n], 0, counts[n] - 1)))

    def out_spec(n):
        return pl.BlockSpec((D_MODEL, D_MODEL), lambda k, s: (0, jnp.clip(k - starts[n], 0, counts[n] - 1)))

    return pl.pallas_call(
        body, name=name, grid=(sum(counts), t // tt),
        in_specs=[lhs_spec(n) for n in range(n_prod)] + [rhs_spec(n) for n in range(n_prod)],
        out_specs=[out_spec(n) for n in range(n_prod)],
        out_shape=[jax.ShapeDtypeStruct((D_MODEL, n * D_MODEL), BF16) for n in counts],
        scratch_shapes=[pltpu.VMEM((D_MODEL, D_MODEL), F32)],
        compiler_params=_params(("arbitrary", "arbitrary")),
    )(*[tr[0] for tr in triples], *[tr[1] for tr in triples])


def _mix_bwd(dx1b, gates, ya, yc, w_out, w_ap, w_cp, w_g, rider=None):
    t = dx1b.shape[0]
    tm = min(t, 512)

    def body(dx_ref, g_ref, ya_ref, yc_ref, wout, wap, wcp, wg,
             dgp_ref, dya_ref, dyc_ref, dyat_ref, dycv_ref, dhg_ref, dbg_ref):
        dm = _dot_nt(dx_ref[...], wout[...])
        ga = g_ref[:, :D_MODEL].astype(F32)
        gc = g_ref[:, D_MODEL:].astype(F32)
        dya = (dm * ga).astype(BF16)
        dyc = (dm * gc).astype(BF16)
        dya_ref[...] = dya
        dyc_ref[...] = dyc
        dgpa = dm * ya_ref[...].astype(F32) * ga * (1.0 - ga)
        dgpc = dm * yc_ref[...].astype(F32) * gc * (1.0 - gc)

        @pl.when(pl.program_id(0) == 0)
        def _():
            dbg_ref[...] = jnp.zeros_like(dbg_ref)

        dbg_ref[:, :D_MODEL] += _fold8(dgpa)
        dbg_ref[:, D_MODEL:] += _fold8(dgpc)
        dgpa = dgpa.astype(BF16)
        dgpc = dgpc.astype(BF16)
        dgp_ref[:, :D_MODEL] = dgpa
        dgp_ref[:, D_MODEL:] = dgpc
        dyat_ref[...] = _dot_nt(dya, wap[...]).astype(BF16)
        dycv_ref[...] = _dot_nt(dyc, wcp[...]).astype(BF16)
        dhg_ref[...] = _dot_nt(dgpa, wg[:, :D_MODEL]) + _dot_nt(dgpc, wg[:, D_MODEL:])

    row = pl.BlockSpec((tm, D_MODEL), lambda i: (i, 0))
    row2 = pl.BlockSpec((tm, 2 * D_MODEL), lambda i: (i, 0))
    full = _resident((D_MODEL, D_MODEL))
    return _call(
        body, name="mix_bwd", grid=(t // tm,), args=(dx1b, gates, ya, yc, w_out, w_ap, w_cp, w_g),
        in_specs=[row, row2, row, row, full, full, full, _resident((D_MODEL, 2 * D_MODEL))],
        out_specs=[row2, row, row, row, row, row, pl.BlockSpec((SUBLANES, 2 * D_MODEL), lambda i: (0, 0))],
        out_shape=[jax.ShapeDtypeStruct((t, 2 * D_MODEL), BF16)] + [jax.ShapeDtypeStruct((t, D_MODEL), BF16)] * 4
        + [jax.ShapeDtypeStruct((t, D_MODEL), F32), jax.ShapeDtypeStruct((SUBLANES, 2 * D_MODEL), F32)],
        semantics=("arbitrary",), rider=rider)


def _conv_bwd(dyconv, proj, conv_w, conv_b, rider=None):
    t = proj.shape[0]
    tm = min(t, 512)
    hb = tm // SUBLANES
    last = t // SUBLANES - 1

    def body(dy_ref, dyn_ref, bg_ref, bgn_ref, cg_ref, cgp_ref, xc_ref, xcp_ref, w_ref, b_ref,
             o_ref, dcb_ref, dcw_ref):
        i = pl.program_id(0)
        cg = cg_ref[...].astype(F32)
        xc = xc_ref[...].astype(F32)
        bg = bg_ref[...].astype(F32)
        u = cg * xc
        prev = jnp.where(i > 0, cgp_ref[...].astype(F32) * xcp_ref[...].astype(F32), 0.0)
        u1 = _shift_down(u, prev, 1)
        u2 = _shift_down(u, prev, 2)
        w = w_ref[...]
        conv = b_ref[...] + (w[0:1] * u2 + w[1:2] * u1 + w[2:3] * u)
        dy = dy_ref[...].astype(F32)
        dconv = dy * bg
        nxt = jnp.where(i < pl.num_programs(0) - 1, dyn_ref[...].astype(F32) * bgn_ref[...].astype(F32), 0.0)
        du = w[2:3] * dconv + w[1:2] * _shift_up(dconv, nxt, 1) + w[0:1] * _shift_up(dconv, nxt, 2)
        o_ref[:, :D_MODEL] = (dy * conv).astype(BF16)
        o_ref[:, D_MODEL:2 * D_MODEL] = (du * xc).astype(BF16)
        o_ref[:, 2 * D_MODEL:] = (du * cg).astype(BF16)

        @pl.when(i == 0)
        def _():
            dcb_ref[...] = jnp.zeros_like(dcb_ref)
            dcw_ref[...] = jnp.zeros_like(dcw_ref)

        dcb_ref[...] += _fold8(dconv)
        dcw_ref[0:SUBLANES] += _fold8(dconv * u2)
        dcw_ref[SUBLANES:2 * SUBLANES] += _fold8(dconv * u1)
        dcw_ref[2 * SUBLANES:] += _fold8(dconv * u)

    def prev(col):
        return pl.BlockSpec((SUBLANES, D_MODEL), lambda i: (jnp.maximum(i * hb - 1, 0), col))

    def nxt(col):
        return pl.BlockSpec((SUBLANES, D_MODEL), lambda i: (jnp.minimum((i + 1) * hb, last), col))

    def cur(col):
        return pl.BlockSpec((tm, D_MODEL), lambda i: (i, col))

    return _call(
        body, name="conv_bwd", grid=(t // tm,),
        args=(dyconv, dyconv, proj, proj, proj, proj, proj, proj, conv_w, conv_b),
        in_specs=[cur(0), nxt(0), cur(3), nxt(3), cur(4), prev(4), cur(5), prev(5),
                  pl.BlockSpec((3, D_MODEL), lambda i: (0, 0)), pl.BlockSpec((1, D_MODEL), lambda i: (0, 0))],
        out_specs=[pl.BlockSpec((tm, 3 * D_MODEL), lambda i: (i, 0)),
                   pl.BlockSpec((SUBLANES, D_MODEL), lambda i: (0, 0)),
                   pl.BlockSpec((3 * SUBLANES, D_MODEL), lambda i: (0, 0))],
        out_shape=[jax.ShapeDtypeStruct((t, 3 * D_MODEL), BF16), jax.ShapeDtypeStruct((SUBLANES, D_MODEL), F32),
                   jax.ShapeDtypeStruct((3 * SUBLANES, D_MODEL), F32)],
        semantics=("arbitrary",), rider=rider)


def _attn_bwd(qn, kn, proj, dyattn, y_attn, lse, bias, rider=None):
    t = qn.shape[0]
    nb = t // QB
    v_col0 = 2 * D_MODEL // SLAB

    def body(q_ref, k0, k1, k2, v0, v1, v2, do_ref, o_ref, lse_ref, bias_ref,
             dq_ref, dk_ref, dv_ref, db_ref, acck, accv):
        b = pl.program_id(1)

        @pl.when(b == 0)
        def _():
            acck[...] = jnp.zeros_like(acck)
            accv[...] = jnp.zeros_like(accv)
            db_ref[...] = jnp.zeros_like(db_ref)

        def block(valid):
            head_a = lax.broadcasted_iota(jnp.int32, (1, LANES), 1) < HEAD_DIM

            def window(refs, hp):
                sl = slice(hp * LANES, (hp + 1) * LANES)
                return jnp.concatenate([r[:, sl] for r in refs], axis=0)

            def transposed(x, hh):
                return x.astype(F32).T.astype(BF16)[hh * HEAD_DIM:(hh + 1) * HEAD_DIM]

            def probs(head):
                hp, hh = divmod(head, 2)
                sl = slice(hp * LANES, (hp + 1) * LANES)
                mine = head_a if hh == 0 else jnp.logical_not(head_a)
                k = window((k0, k1, k2), hp)
                s = _dot_nt(jnp.where(mine, k, jnp.zeros_like(k)), q_ref[:, sl]) + bias_ref[head]
                s = s if valid is None else jnp.where(valid, s, NEG_INF)
                return jnp.exp2(s - lse_ref[head:head + 1, :])

            def grads(head, p):
                hp, hh = divmod(head, 2)
                sl = slice(hp * LANES, (hp + 1) * LANES)
                rows = slice(hh * HEAD_DIM, (hh + 1) * HEAD_DIM)
                mine = head_a if hh == 0 else jnp.logical_not(head_a)
                do = do_ref[:, sl]
                v = window((v0, v1, v2), hp)
                delta = jnp.sum((do.astype(F32).T * o_ref[:, sl].astype(F32).T)[rows], axis=0, keepdims=True)
                ds = p * (_dot_nt(jnp.where(mine, v, jnp.zeros_like(v)), do) - delta)
                db_ref[head] += ds
                pb, dsb = p.astype(BF16), ds.astype(BF16)
                dvt = _dot_nt(transposed(do, hh), pb)
                dkt = _dot_nt(transposed(q_ref[:, sl], hh), dsb) * (1.0 / LOG2E)
                dqt = _dot(transposed(window((k0, k1, k2), hp), hh), dsb)
                return dqt, dkt, dvt

            out = []
            pending = probs(0)
            for head in range(2 * PAIRS):
                nxt = probs(head + 1) if head + 1 < 2 * PAIRS else None
                out.append(grads(head, pending))
                pending = nxt
            for hp in range(PAIRS):
                sl = slice(hp * LANES, (hp + 1) * LANES)
                dqt, dkt, dvt = (jnp.concatenate([out[2 * hp][n], out[2 * hp + 1][n]], axis=0) for n in range(3))
                dq_ref[:, sl] = dqt.T.astype(BF16)
                for w in range(3):
                    slot = lax.rem(b + w + 1, 3)
                    cols = slice(w * QB, (w + 1) * QB)
                    if w == 2:
                        acck[hp, slot] = dkt[:, cols]
                        accv[hp, slot] = dvt[:, cols]
                    else:
                        acck[hp, slot] += dkt[:, cols]
                        accv[hp, slot] += dvt[:, cols]

        @pl.when(b < 2)
        def _():
            block(lax.broadcasted_iota(jnp.int32, (KW, 1), 0) >= (2 - b) * QB)

        @pl.when((b >= 2) & (b < nb))
        def _():
            block(None)

        done = lax.rem(b + 1, 3)
        for hp in range(PAIRS):
            sl = slice(hp * LANES, (hp + 1) * LANES)
            dk_ref[:, sl] = acck[hp, done].T.astype(BF16)
            dv_ref[:, sl] = accv[hp, done].T.astype(BF16)

    def cur(p, b):
        return (jnp.minimum(b, nb - 1), p)

    def window(col0):
        return [pl.BlockSpec((QB, SLAB), functools.partial(
            lambda p, b, back: (jnp.maximum(jnp.minimum(b, nb - 1) - back, 0), col0 + p), back=back))
            for back in (2, 1, 0)]

    def done_block(p, b):
        return (jnp.maximum(b - 2, 0), p)

    tile = pl.BlockSpec((2 * PAIRS, KW, QB), lambda p, b: (p, 0, 0))
    here = pl.BlockSpec((QB, SLAB), cur)
    return _call(
        body, name="attn_bwd", grid=(D_MODEL // SLAB, nb + 2),
        args=(qn, kn, kn, kn, proj, proj, proj, dyattn, y_attn, lse, bias),
        in_specs=[here] + window(0) + window(v_col0)
        + [here, here, pl.BlockSpec((2 * PAIRS, QB), lambda p, b: (p, jnp.minimum(b, nb - 1))), tile],
        out_specs=[here, pl.BlockSpec((QB, SLAB), done_block), pl.BlockSpec((QB, SLAB), done_block), tile],
        out_shape=[jax.ShapeDtypeStruct((t, D_MODEL), BF16)] * 3 + [jax.ShapeDtypeStruct((N_HEADS, KW, QB), F32)],
        scratch_shapes=[pltpu.VMEM((PAIRS, 3, LANES, QB), F32), pltpu.VMEM((PAIRS, 3, LANES, QB), F32)],
        semantics=("parallel", "arbitrary"), rider=rider)


def _qknorm_bwd(proj, dqn, dkn, gq, gk):
    t = proj.shape[0]
    tm = min(t, 512)
    scale = HEAD_DIM ** -0.5

    def body(q_ref, k_ref, dqn_ref, dkn_ref, gq_ref, gk_ref, o_ref, dgq_ref, dgk_ref):
        e = _head_sum_matrix()

        @pl.when(pl.program_id(0) == 0)
        def _():
            dgq_ref[...] = jnp.zeros_like(dgq_ref)
            dgk_ref[...] = jnp.zeros_like(dgk_ref)

        for n, (src, dn_ref, g_ref, dg_ref, sc) in enumerate(
                ((q_ref, dqn_ref, gq_ref, dgq_ref, scale), (k_ref, dkn_ref, gk_ref, dgk_ref, 1.0))):
            for s in range(D_MODEL // LANES):
                sl = slice(s * LANES, (s + 1) * LANES)
                xf = src[:, sl].astype(F32)
                r = lax.rsqrt(_head_sums(xf * xf, e) * (1.0 / HEAD_DIM) + EPS)
                xh = xf * r
                dn = dn_ref[:, sl].astype(F32) * sc
                dg_ref[:, sl] += _fold8(dn * xh)
                dxh = dn * g_ref[:, sl]
                mean = _head_sums(dxh * xh, e) * (1.0 / HEAD_DIM)
                o_ref[:, n * D_MODEL + s * LANES:n * D_MODEL + (s + 1) * LANES] = (r * (dxh - xh * mean)).astype(BF16)

    row = pl.BlockSpec((tm, D_MODEL), lambda i: (i, 0))
    vec = pl.BlockSpec((1, D_MODEL), lambda i: (0, 0))
    acc = pl.BlockSpec((SUBLANES, D_MODEL), lambda i: (0, 0))
    return pl.pallas_call(
        body, name="qknorm_bwd", grid=(t // tm,),
        in_specs=[row, pl.BlockSpec((tm, D_MODEL), lambda i: (i, 1)), row, row, vec, vec],
        out_specs=[pl.BlockSpec((tm, 2 * D_MODEL), lambda i: (i, 0)), acc, acc],
        out_shape=[jax.ShapeDtypeStruct((t, 2 * D_MODEL), BF16)] + [jax.ShapeDtypeStruct((SUBLANES, D_MODEL), F32)] * 2,
        compiler_params=_params(("arbitrary",)),
    )(proj, proj, dqn, dkn, gq, gk)


def _in_bwd(dqk, dv, dconv, w_in, dhg, x, g1, dx1, rider=None):
    t = x.shape[0]
    tm = min(t, 512)

    def body(dqk_ref, dv_ref, dc_ref, w_ref, dhg_ref, x_ref, g_ref, dx1_ref, dx_ref, dg_ref):
        acc = dhg_ref[...]
        slab = 0
        for src, n in ((dqk_ref, 2), (dv_ref, 1), (dc_ref, 3)):
            for s in range(n):
                acc = acc + _dot_nt(src[:, s * D_MODEL:(s + 1) * D_MODEL],
                                    w_ref[:, slab * D_MODEL:(slab + 1) * D_MODEL])
                slab += 1

        @pl.when(pl.program_id(0) == 0)
        def _():
            dg_ref[...] = jnp.zeros_like(dg_ref)

        dx, dg = _rmsnorm_bwd(x_ref[...], g_ref[...], acc)
        dx_ref[...] = dx1_ref[...] + dx
        dg_ref[...] += _fold8(dg)

    row = pl.BlockSpec((tm, D_MODEL), lambda i: (i, 0))
    return _call(
        body, name="in_bwd", grid=(t // tm,), args=(dqk, dv, dconv, w_in, dhg, x, g1, dx1),
        in_specs=[pl.BlockSpec((tm, 2 * D_MODEL), lambda i: (i, 0)), row,
                  pl.BlockSpec((tm, 3 * D_MODEL), lambda i: (i, 0)),
                  _resident(w_in.shape), row, row, pl.BlockSpec((1, D_MODEL), lambda i: (0, 0)), row],
        out_specs=[row, pl.BlockSpec((SUBLANES, D_MODEL), lambda i: (0, 0))],
        out_shape=[jax.ShapeDtypeStruct((t, D_MODEL), F32), jax.ShapeDtypeStruct((SUBLANES, D_MODEL), F32)],
        semantics=("arbitrary",), rider=rider)


def _bias_grad_fold(dbias, rider=None):
    def body(d_ref, o_ref):
        jj = lax.broadcasted_iota(jnp.int32, (QB, QB), 0)
        ii = lax.broadcasted_iota(jnp.int32, (QB, QB), 1)
        flip = (jj + ii == QB - 1).astype(BF16)
        low = jj + ii < QB
        pos, neg = [], []
        for w in range(KW // QB):
            x = d_ref[0, QB * w:QB * (w + 1), :]
            hi = x.astype(BF16)
            r1 = x - hi.astype(F32)
            mid = r1.astype(BF16)
            lo = (r1 - mid.astype(F32)).astype(BF16)
            xr = _dot(hi, flip) + _dot(mid, flip) + _dot(lo, flip)
            for keep, acc in ((low, pos), (jnp.logical_not(low), neg)):
                part = pltpu.roll(jnp.where(keep, xr, 0.0), 0, 1, stride=1, stride_axis=0)
                acc.append(jnp.sum(part, axis=0, keepdims=True))
        far = pos[1] + neg[0] + pos[0]
        o_ref[0] = jnp.zeros((SUBLANES, QB), F32)
        o_ref[0, 0:1, :] = neg[2]
        o_ref[0, 1:2, :] = pos[2] + neg[1]
        o_ref[0, 2:3, :] = jnp.broadcast_to(jnp.sum(far, axis=-1, keepdims=True), (1, QB))

    return _call(
        body, name="bias_grad_fold", grid=(N_HEADS,), args=(dbias,),
        in_specs=[pl.BlockSpec((1, KW, QB), lambda h: (h, 0, 0))],
        out_specs=[pl.BlockSpec((1, SUBLANES, QB), lambda h: (h, 0, 0))],
        out_shape=[jax.ShapeDtypeStruct((N_HEADS, SUBLANES, QB), F32)],
        semantics=("parallel",), rider=rider)


def _small_partials(dg1, dgq, dgk, dcb, dcw, dbg, dg2, dbias_fold, loss_tile):
    def head_fold(v):
        acc = v[:, 0:LANES]
        for s in range(1, D_MODEL // LANES):
            acc = acc + v[:, s * LANES:(s + 1) * LANES]
        return acc + pltpu.roll(acc, HEAD_DIM, 1)

    def body(dg1_ref, dgq_ref, dgk_ref, dcb_ref, dcw_ref, dbg_ref, dg2_ref, db_ref, loss_ref, o_ref):
        o_ref[...] = jnp.zeros_like(o_ref)
        o_ref[0:1, :] = jnp.sum(dg1_ref[...], axis=0, keepdims=True)
        o_ref[1:2, 0:LANES] = head_fold(jnp.sum(dgq_ref[...], axis=0, keepdims=True))
        o_ref[2:3, 0:LANES] = head_fold(jnp.sum(dgk_ref[...], axis=0, keepdims=True))
        o_ref[3:4, :] = jnp.sum(dcb_ref[...], axis=0, keepdims=True)
        for j in range(3):
            o_ref[4 + j:5 + j, :] = jnp.sum(dcw_ref[j * SUBLANES:(j + 1) * SUBLANES, :], axis=0, keepdims=True)
        o_ref[7:8, :] = jnp.sum(dbg_ref[:, :D_MODEL], axis=0, keepdims=True)
        o_ref[8:9, :] = jnp.sum(dbg_ref[:, D_MODEL:], axis=0, keepdims=True)
        o_ref[9:10, :] = jnp.sum(dg2_ref[...], axis=0, keepdims=True)
        for h in range(N_HEADS):
            for part in range(3):
                o_ref[10 + h:11 + h, part * QB:(part + 1) * QB] = db_ref[h, part:part + 1, :]
        loss = (0.5 / D_MODEL) * jnp.sum(jnp.sum(loss_ref[...], axis=0, keepdims=True), axis=-1, keepdims=True)
        o_ref[26:27, :] = jnp.broadcast_to(loss, (1, D_MODEL))

    return pl.pallas_call(
        body, name="small_partials",
        out_shape=jax.ShapeDtypeStruct((32, D_MODEL), F32),
        compiler_params=_params(),
    )(dg1, dgq, dgk, dcb, dcw, dbg, dg2, dbias_fold, loss_tile)


MID_AXES = (0, 0, 1, 0)
MLP_AXES = (1, 0)


def _local_step(x, target, norm1_g, q_norm_g, k_norm_g, bias, conv_w, conv_b, b_gate, norm2_g,
                w_in, mid_w, mlp_w, distributed):
    g1 = norm1_g.reshape(1, D_MODEL)
    g2 = norm2_g.reshape(1, D_MODEL)
    gq = jnp.tile(q_norm_g, N_HEADS).reshape(1, D_MODEL)
    gk = jnp.tile(k_norm_g, N_HEADS).reshape(1, D_MODEL)
    cb = conv_b.reshape(1, D_MODEL)

    (proj, h), got = _in_proj(x, g1, w_in, rider=_Gather(mid_w, MID_AXES) if distributed else None)
    w_ap, w_cp, w_g, w_out = got if distributed else mid_w
    (gates,), _ = _proj("gate_proj", h, w_g, b=b_gate.reshape(1, 2 * D_MODEL))
    qn, kn = _qknorm_fwd(proj, gq, gk)
    (y_attn, lse), got = _attn_fwd(qn, kn, proj, bias, rider=_Gather(mlp_w, MLP_AXES) if distributed else None)
    w_up, w_down = got if distributed else mlp_w
    y_conv = _conv_fwd(proj, conv_w, cb)
    ya, yc, merged, x1, h2 = _mix_out(y_attn, y_conv, gates, x, w_ap, w_cp, w_out, g2)
    a, dy, dyb, loss_tile = _mlp_fwd(h2, w_up, w_down, x1, target)

    da, dx1, dx1b, dg2 = _mlp_bwd(dyb, a, w_down, w_up, x1, dy, g2)
    gw_down = _wgrad("wgrad_down", a, [dyb], [1], relu_sq=True)
    gw_up = _wgrad("wgrad_up", h2, [da], [D_FF // D_MODEL])
    (dgp, dya, dyc, dyattn, dyconv, dhg, dbg), mlp_swapped = _mix_bwd(
        dx1b, gates, ya, yc, w_out, w_ap, w_cp, w_g,
        rider=_PairSwap((gw_up, gw_down), MLP_AXES) if distributed else None)
    mid = tuple(_wgrad_group("wgrad_mid", [(y_attn, dya, 1), (y_conv, dyc, 1), (h, dgp, 2), (merged, dx1b, 1)]))
    gw_ap, gw_cp, gw_g, gw_out = mid
    (dconv, dcb, dcw), mid_swapped = _conv_bwd(
        dyconv, proj, conv_w, cb, rider=_PairSwap(mid, MID_AXES) if distributed else None)
    early = mid + (gw_up, gw_down)
    early_sums = (_pair_add(early, tuple(mid_swapped) + tuple(mlp_swapped), MID_AXES + MLP_AXES)
                  if distributed else None)
    (dqn, dkn, dv, dbias), early_shares = _attn_bwd(
        qn, kn, proj, dyattn, y_attn, lse, bias, rider=_ChipScatter(early_sums) if distributed else None)
    dqk, dgq, dgk = _qknorm_bwd(proj, dqn, dkn, gq, gk)
    gw_in = _wgrad("wgrad_in", h, [dqk, dv, dconv], [2, 1, 3])
    (dbias_fold,), in_swapped = _bias_grad_fold(dbias, rider=_PairSwap((gw_in,), (1,)) if distributed else None)
    in_sums = _pair_add((gw_in,), in_swapped, (1,)) if distributed else None
    (dx, dg1), in_shares = _in_bwd(dqk, dv, dconv, w_in, dhg, x, g1, dx1,
                                   rider=_ChipScatter(in_sums) if distributed else None)
    small = _small_partials(dg1, dgq, dgk, dcb, dcw, dbg, dg2, dbias_fold, loss_tile)
    grads = tuple(in_shares) + tuple(early_shares) if distributed else (gw_in,) + early
    return dx, grads, small


def _me():
    return lax.axis_index("x"), lax.axis_index("y"), lax.axis_index("c")


def _peer(me, rel):
    x, y, c = me
    return (1 - x if rel & 4 else x, 1 - y if rel & 2 else y, 1 - c if rel & 1 else c)


def _linear(dev):
    return 4 * dev[0] + 2 * dev[1] + dev[2]


BIG_AXES = (1, 0, 0, 1, 0, 1, 0)


def _block(ref, axis, idx, size):
    return ref.at[pl.ds(idx * size, size), :] if axis == 0 else ref.at[:, pl.ds(idx * size, size)]


def _cast_shards(shards):
    def body(*refs):
        for src, dst in zip(refs[:len(shards)], refs[len(shards):]):
            dst[...] = src[...].astype(BF16)

    return pl.pallas_call(
        body, name="cast_shards",
        out_shape=[jax.ShapeDtypeStruct(s.shape, BF16) for s in shards],
        compiler_params=_params(),
    )(*shards)


class _Gather:
    def __init__(self, shards, axes):
        self.arrays, self.axes, self.n = list(shards), tuple(axes), len(shards)
        self.sizes = [s.shape[axis] for s, axis in zip(shards, axes)]
        self.out_shape = []
        for s, axis in zip(shards, axes):
            shape = (s.shape[0] * N_DEV, s.shape[1]) if axis == 0 else (s.shape[0], s.shape[1] * N_DEV)
            self.out_shape.append(jax.ShapeDtypeStruct(shape, s.dtype))
        self.scratch = [pltpu.SemaphoreType.DMA((self.n, 7)), pltpu.SemaphoreType.DMA((self.n, 7)),
                        pltpu.SemaphoreType.DMA((self.n,))]

    def _copies(self, srcs, outs, sems):
        send_sems, recv_sems, local_sems = sems
        me = _me()
        sibling = _peer(me, 1)
        chips = [_peer(me, rel) for rel in (4, 2, 6)]

        def rows(a, dev):
            return _block(outs[a], self.axes[a], _linear(dev), self.sizes[a])

        def copy(a, k, block_dev, to, src=None):
            return pltpu.make_async_remote_copy(
                src_ref=rows(a, block_dev) if src is None else src, dst_ref=rows(a, block_dev),
                send_sem=send_sems.at[a, k], recv_sem=recv_sems.at[a, k], device_id=to, device_id_type=MESH_T)

        own = [pltpu.make_async_copy(srcs[a], rows(a, me), local_sems.at[a]) for a in range(self.n)]
        first = []
        for a in range(self.n):
            first.append(copy(a, 0, me, sibling, src=srcs[a]))
            for j, chip in enumerate(chips):
                first.append(copy(a, 1 + j, me, chip, src=srcs[a]))
        return me, sibling, chips, copy, own, first

    def start(self, srcs, outs, sems):
        _, _, _, _, own, first = self._copies(srcs, outs, sems)
        for cp in own + first:
            cp.start()

    def finish(self, srcs, outs, sems):
        me, sibling, chips, copy, own, first = self._copies(srcs, outs, sems)
        passed = []
        for a in range(self.n):
            for j, chip in enumerate(chips):
                copy(a, 1 + j, chip, me).wait_recv()
                fwd = copy(a, 4 + j, chip, sibling)
                fwd.start()
                passed.append(fwd)
        for a in range(self.n):
            copy(a, 0, sibling, me).wait_recv()
            for j, chip in enumerate(chips):
                copy(a, 4 + j, _peer(chip, 1), me).wait_recv()
        for cp in first + passed:
            cp.wait_send()
        for cp in own:
            cp.wait()


N_CHIPS = 4


def _shard_shape(g, axis):
    return (g.shape[0] // N_DEV, g.shape[1]) if axis == 0 else (g.shape[0], g.shape[1] // N_DEV)


class _PairSwap:
    def __init__(self, grads, axes):
        self.arrays, self.axes, self.n = list(grads), tuple(axes), len(grads)
        self.sizes = [g.shape[axis] // N_DEV for g, axis in zip(grads, axes)]
        self.out_shape = [jax.ShapeDtypeStruct((N_CHIPS,) + _shard_shape(g, axis), g.dtype)
                          for g, axis in zip(grads, axes)]
        self.scratch = [pltpu.SemaphoreType.DMA((self.n, N_CHIPS)), pltpu.SemaphoreType.DMA((self.n, N_CHIPS))]

    def _copies(self, srcs, outs, sems):
        send_sems, recv_sems = sems
        x, y, c = _me()
        sibling = (x, y, 1 - c)
        copies = []
        for a in range(self.n):
            for chip in range(N_CHIPS):
                owner_idx = 2 * chip + (1 - c)
                copies.append(pltpu.make_async_remote_copy(
                    src_ref=_block(srcs[a], self.axes[a], owner_idx, self.sizes[a]), dst_ref=outs[a].at[chip],
                    send_sem=send_sems.at[a, chip], recv_sem=recv_sems.at[a, chip],
                    device_id=sibling, device_id_type=MESH_T))
        return copies

    def start(self, srcs, outs, sems):
        for cp in self._copies(srcs, outs, sems):
            cp.start()

    def finish(self, srcs, outs, sems):
        for cp in self._copies(srcs, outs, sems):
            cp.wait()


def _pair_add(grads, swapped, axes):
    n = len(grads)
    c_arr = lax.axis_index("c").astype(jnp.int32).reshape(1)

    def body(c_ref, *refs):
        del c_ref
        mine, got, outs = refs[:n], refs[n:2 * n], refs[2 * n:]
        for a in range(n):
            outs[a][0] = (mine[a][...].astype(F32) + got[a][0].astype(F32)).astype(BF16)

    in_specs, out_specs, out_shape = [], [], []
    for g, axis in zip(grads, axes):
        shard = _shard_shape(g, axis)
        if axis == 0:
            in_specs.append(pl.BlockSpec(shard, lambda s, c_ref: (2 * s + c_ref[0], 0)))
        else:
            in_specs.append(pl.BlockSpec(shard, lambda s, c_ref: (0, 2 * s + c_ref[0])))
    for g, axis in zip(grads, axes):
        shard = _shard_shape(g, axis)
        in_specs.append(pl.BlockSpec((1,) + shard, lambda s, c_ref: (s, 0, 0)))
        out_specs.append(pl.BlockSpec((1,) + shard, lambda s, c_ref: (s, 0, 0)))
        out_shape.append(jax.ShapeDtypeStruct((N_CHIPS,) + shard, BF16))
    return pl.pallas_call(
        body, name="pair_add_" + str(n),
        grid_spec=pltpu.PrefetchScalarGridSpec(num_scalar_prefetch=1, grid=(N_CHIPS,), in_specs=in_specs,
                                               out_specs=out_specs),
        out_shape=out_shape, compiler_params=_params(("arbitrary",)),
    )(c_arr, *grads, *swapped)


class _ChipScatter:
    def __init__(self, sums):
        self.arrays, self.n = list(sums), len(sums)
        self.out_shape = [jax.ShapeDtypeStruct(s.shape, s.dtype) for s in sums]
        self.scratch = [pltpu.SemaphoreType.DMA((self.n, 3)), pltpu.SemaphoreType.DMA((self.n, 3)),
                        pltpu.SemaphoreType.DMA((self.n,))]

    def _copies(self, srcs, outs, sems):
        send_sems, recv_sems, local_sems = sems
        me = _me()
        my_chip = 2 * me[0] + me[1]
        own = [pltpu.make_async_copy(srcs[a].at[my_chip], outs[a].at[my_chip], local_sems.at[a])
               for a in range(self.n)]
        sends, recvs = [], []
        for a in range(self.n):
            for k, rel in enumerate((4, 2, 6)):
                peer = _peer(me, rel)
                peer_chip = 2 * peer[0] + peer[1]
                sends.append(pltpu.make_async_remote_copy(
                    src_ref=srcs[a].at[peer_chip], dst_ref=outs[a].at[my_chip],
                    send_sem=send_sems.at[a, k], recv_sem=recv_sems.at[a, k], device_id=peer, device_id_type=MESH_T))
                recvs.append(pltpu.make_async_remote_copy(
                    src_ref=srcs[a].at[my_chip], dst_ref=outs[a].at[peer_chip],
                    send_sem=send_sems.at[a, k], recv_sem=recv_sems.at[a, k], device_id=peer, device_id_type=MESH_T))
        return own, sends, recvs

    def start(self, srcs, outs, sems):
        own, sends, _ = self._copies(srcs, outs, sems)
        for cp in own + sends:
            cp.start()

    def finish(self, srcs, outs, sems):
        own, sends, recvs = self._copies(srcs, outs, sems)
        for cp in recvs:
            cp.wait_recv()
        for cp in sends:
            cp.wait_send()
        for cp in own:
            cp.wait()


def _call(body, *, name, args, in_specs, out_specs, out_shape, grid=(), scratch_shapes=(), semantics=None,
          rider=None):
    if rider is None:
        return pl.pallas_call(
            body, name=name, grid=grid, in_specs=in_specs, out_specs=out_specs, out_shape=out_shape,
            scratch_shapes=list(scratch_shapes), compiler_params=_params(semantics))(*args), None
    n_in, n_out, n_scr, r = len(in_specs), len(out_specs), len(scratch_shapes), rider.n

    def wrapped(*refs):
        ins, r_ins = refs[:n_in], refs[n_in:n_in + r]
        outs = refs[n_in + r:n_in + r + n_out]
        r_outs = refs[n_in + r + n_out:n_in + 2 * r + n_out]
        scr = refs[n_in + 2 * r + n_out:n_in + 2 * r + n_out + n_scr]
        sems = refs[n_in + 2 * r + n_out + n_scr:]
        first, last = None, None
        for ax in range(len(grid)):
            f, l = pl.program_id(ax) == 0, pl.program_id(ax) == pl.num_programs(ax) - 1
            first = f if first is None else first & f
            last = l if last is None else last & l
        if first is None:
            rider.start(r_ins, r_outs, sems)
            body(*ins, *outs, *scr)
            rider.finish(r_ins, r_outs, sems)
            return

        @pl.when(first)
        def _():
            rider.start(r_ins, r_outs, sems)

        body(*ins, *outs, *scr)

        @pl.when(last)
        def _():
            rider.finish(r_ins, r_outs, sems)

    any_spec = pl.BlockSpec(memory_space=pl.ANY)
    out = pl.pallas_call(
        wrapped, name=name, grid=grid, in_specs=list(in_specs) + [any_spec] * r,
        out_specs=list(out_specs) + [any_spec] * r, out_shape=list(out_shape) + rider.out_shape,
        scratch_shapes=list(scratch_shapes) + rider.scratch,
        compiler_params=_params(None if semantics is None else ("arbitrary",) * len(semantics)),
    )(*args, *rider.arrays)
    return out[:n_out], out[n_out:]


def _exchange(name, rider):
    def body():
        pass

    return _call(body, name=name, args=(), in_specs=[], out_specs=[], out_shape=[], rider=rider)[1]


def _all_reduce_small(part):
    def body(p_ref, o_ref, slots, send_sems, recv_sems):
        me = _me()
        my_idx = _linear(me)
        slots[my_idx] = p_ref[...]
        sends = []
        for rel in range(1, N_DEV):
            cp = pltpu.make_async_remote_copy(
                src_ref=p_ref, dst_ref=slots.at[my_idx], send_sem=send_sems.at[rel - 1],
                recv_sem=recv_sems.at[rel - 1], device_id=_peer(me, rel), device_id_type=MESH_T)
            cp.start()
            sends.append(cp)
        for rel in range(1, N_DEV):
            frm = _peer(me, rel)
            pltpu.make_async_remote_copy(
                src_ref=p_ref, dst_ref=slots.at[_linear(frm)], send_sem=send_sems.at[rel - 1],
                recv_sem=recv_sems.at[rel - 1], device_id=frm, device_id_type=MESH_T).wait_recv()
        for cp in sends:
            cp.wait_send()
        total = slots[0]
        for d in range(1, N_DEV):
            total = total + slots[d]
        o_ref[...] = total

    return pl.pallas_call(
        body, name="all_reduce_small",
        in_specs=[pl.BlockSpec(memory_space=pltpu.VMEM)], out_specs=pl.BlockSpec(memory_space=pltpu.VMEM),
        out_shape=jax.ShapeDtypeStruct(part.shape, F32),
        scratch_shapes=[pltpu.VMEM((N_DEV,) + part.shape, F32), pltpu.SemaphoreType.DMA((7,)),
                        pltpu.SemaphoreType.DMA((7,))],
        compiler_params=_params(),
    )(part)


def _adamw_math(w, g, m, v):
    m = ADAM_B1 * m + (1.0 - ADAM_B1) * g
    v = ADAM_B2 * v + (1.0 - ADAM_B2) * jnp.square(g)
    m_hat = m / (1.0 - ADAM_B1 ** ADAM_STEP)
    v_hat = v / (1.0 - ADAM_B2 ** ADAM_STEP)
    delta = -ADAM_LR * (m_hat / (jnp.sqrt(v_hat) + ADAM_EPS) + ADAM_WD * w)
    return delta, m, v


ADAMW_STEPS = 4


def _adamw_big(shares, ws, ms, vs, rider=None):
    n = len(ws)

    def body(*refs):
        s_refs, w_refs, m_refs, v_refs = (refs[a * n:(a + 1) * n] for a in range(4))
        outs = refs[4 * n:]
        for a in range(n):
            g = s_refs[a][0].astype(F32)
            for d in range(1, N_CHIPS):
                g = g + s_refs[a][d].astype(F32)
            outs[4 * a][...] = g
            outs[4 * a + 1][...], outs[4 * a + 2][...], outs[4 * a + 3][...] = _adamw_math(
                w_refs[a][...], g, m_refs[a][...], v_refs[a][...])

    def chunk(w):
        return pl.BlockSpec((w.shape[0] // ADAMW_STEPS, w.shape[1]), lambda i: (i, 0))

    def share_chunk(w):
        return pl.BlockSpec((N_CHIPS, w.shape[0] // ADAMW_STEPS, w.shape[1]), lambda i: (0, i, 0))

    out, rider_out = _call(
        body, name="adamw_big", grid=(ADAMW_STEPS,), args=(*shares, *ws, *ms, *vs),
        in_specs=[share_chunk(w) for w in ws] + [chunk(w) for w in ws] * 3,
        out_specs=[chunk(w) for w in ws for _ in range(4)],
        out_shape=[jax.ShapeDtypeStruct(w.shape, F32) for w in ws for _ in range(4)],
        semantics=("parallel",), rider=rider)
    return [tuple(out[4 * a:4 * a + 4]) for a in range(n)], rider_out


def _adamw_small(quads):
    n = len(quads)

    def body(*refs):
        ins, outs = refs[:4 * n], refs[4 * n:]
        for p in range(n):
            g_ref, w_ref, m_ref, v_ref = ins[4 * p:4 * p + 4]
            d_ref, nm_ref, nv_ref = outs[3 * p:3 * p + 3]
            d_ref[...], nm_ref[...], nv_ref[...] = _adamw_math(w_ref[...], g_ref[...], m_ref[...], v_ref[...])

    flat = [a for quad in quads for a in quad]
    out = pl.pallas_call(
        body, name="adamw_small",
        out_shape=[jax.ShapeDtypeStruct(quad[1].shape, F32) for quad in quads for _ in range(3)],
        compiler_params=_params(),
    )(*flat)
    return [tuple(out[3 * p:3 * p + 3]) for p in range(n)]


def kernel(x, norm1_g, w_in, q_norm_g, k_norm_g, rel_bias, conv_w, conv_b, w_attn_proj, w_conv_proj, w_gate, b_gate, w_out, norm2_g, w_up, w_down, loss_target, m_norm1_g, m_w_in, m_q_norm_g, m_k_norm_g, m_rel_bias, m_conv_w, m_conv_b, m_w_attn_proj, m_w_conv_proj, m_w_gate, m_b_gate, m_w_out, m_norm2_g, m_w_up, m_w_down, v_norm1_g, v_w_in, v_q_norm_g, v_k_norm_g, v_rel_bias, v_conv_w, v_conv_b, v_w_attn_proj, v_w_conv_proj, v_w_gate, v_b_gate, v_w_out, v_norm2_g, v_w_up, v_w_down):
    my_idx = _linear(_me())
    big_w = (w_in, w_attn_proj, w_conv_proj, w_gate, w_out, w_up, w_down)
    big_m = (m_w_in, m_w_attn_proj, m_w_conv_proj, m_w_gate, m_w_out, m_w_up, m_w_down)
    big_v = (v_w_in, v_w_attn_proj, v_w_conv_proj, v_w_gate, v_w_out, v_w_up, v_w_down)
    big_names = ("w_in", "w_attn_proj", "w_conv_proj", "w_gate", "w_out", "w_up", "w_down")

    conv_w_tile = jnp.pad(conv_w, ((0, SUBLANES - conv_w.shape[0]), (0, 0)))
    shards = _cast_shards(big_w)
    (bias,), (w_in_full, conv_w_rows) = _bias_tiles(rel_bias, rider=_Gather((shards[0], conv_w_tile), (1, 1)))

    dx, shares, small = _local_step(x[0], loss_target[0], norm1_g, q_norm_g, k_norm_g, bias, conv_w_rows[:3],
                                    conv_b, b_gate, norm2_g, w_in_full, tuple(shards[1:5]), tuple(shards[5:7]), True)

    big_out, _ = _adamw_big(shares, big_w, big_m, big_v)
    tot = _all_reduce_small(small)
    g_rel_bias = jnp.concatenate(
        [tot[10:26, :QB][:, ::-1], tot[10:26, QB:2 * QB][:, ::-1], tot[10:26, 2 * QB:2 * QB + 1]], axis=1)
    g_conv_w = lax.dynamic_slice(tot[4:7], (0, my_idx * LANES), (3, LANES))
    small_g = [tot[0:1], tot[1:2, :HEAD_DIM], tot[2:3, :HEAD_DIM], g_rel_bias, g_conv_w, tot[3:4],
               tot[7:9].reshape(1, 2 * D_MODEL), tot[9:10]]
    small_w = (norm1_g, q_norm_g, k_norm_g, rel_bias, conv_w, conv_b, b_gate, norm2_g)
    small_m = (m_norm1_g, m_q_norm_g, m_k_norm_g, m_rel_bias, m_conv_w, m_conv_b, m_b_gate, m_norm2_g)
    small_v = (v_norm1_g, v_q_norm_g, v_k_norm_g, v_rel_bias, v_conv_w, v_conv_b, v_b_gate, v_norm2_g)

    def two_d(a):
        return a.reshape(1, -1) if a.ndim == 1 else a

    small_out = _adamw_small([(g, two_d(w), two_d(m), two_d(v))
                              for g, w, m, v in zip(small_g, small_w, small_m, small_v)])

    order = ("norm1_g", "w_in", "q_norm_g", "k_norm_g", "rel_bias", "conv_w", "conv_b", "w_attn_proj", "w_conv_proj",
             "w_gate", "b_gate", "w_out", "norm2_g", "w_up", "w_down")
    small_names = ("norm1_g", "q_norm_g", "k_norm_g", "rel_bias", "conv_w", "conv_b", "b_gate", "norm2_g")
    res = {}
    for name, (g, d, nm, nv) in zip(big_names, big_out):
        res[name] = (g, d, nm, nv)
    for name, g, w, (d, nm, nv) in zip(small_names, small_g, small_w, small_out):
        res[name] = tuple(a.reshape(w.shape) for a in (g, d, nm, nv))
    loss = tot[26, 0]
    return (loss, dx[None], *[res[n][0] for n in order], *[res[n][1] for n in order],
            *[res[n][2] for n in order], *[res[n][3] for n in order])
```

```python
import functools

import jax
import jax.numpy as jnp
from jax import lax
from jax.experimental import pallas as pl
from jax.experimental.pallas import tpu as pltpu

F32 = jnp.float32
BF16 = jnp.bfloat16

D_MODEL = 1024
N_HEADS = 16
HEAD_DIM = 64
CHUNK = 64
N_PREV_CHUNKS = 8
MAX_REL = 256
D_FF = 4096
EPS = 1e-6
NEG_INF = -1e30
LOG2E = 1.4426950408889634
N_DEV = 8

ADAM_LR = 0.001
ADAM_B1 = 0.9
ADAM_B2 = 0.999
ADAM_EPS = 1e-08
ADAM_WD = 0.01
ADAM_STEP = 10

LANES = 128
SUBLANES = 8
VMEM_LIMIT = 56 * 1024 * 1024
QB = 256
KW = 3 * QB
PAIRS = 4
SUB = 128
SLAB = PAIRS * LANES
SKEW = 1024

MESH_T = pl.DeviceIdType.MESH


def _dot(a, b):
    return jnp.dot(a, b, preferred_element_type=F32)


def _dot_nt(a, b):
    return lax.dot_general(a, b, (((1,), (1,)), ((), ())), preferred_element_type=F32)


def _dot_tn(a, b):
    return lax.dot_general(a, b, (((0,), (0,)), ((), ())), preferred_element_type=F32)


def _params(sem=None):
    return pltpu.CompilerParams(dimension_semantics=sem, vmem_limit_bytes=VMEM_LIMIT)


def _resident(shape):
    return pl.BlockSpec(shape, lambda *_: (0,) * len(shape), pipeline_mode=pl.Buffered(1))


def _fold8(v):
    rows, n = v.shape
    return v.reshape(rows // SUBLANES, SUBLANES, n).sum(axis=0)


def _head_sum_matrix():
    r = lax.broadcasted_iota(jnp.int32, (LANES, LANES), 0) // HEAD_DIM
    c = lax.broadcasted_iota(jnp.int32, (LANES, LANES), 1) // HEAD_DIM
    return (r == c).astype(BF16)


def _head_sums(v, e):
    hi = v.astype(BF16)
    lo = (v - hi.astype(F32)).astype(BF16)
    return _dot(hi, e) + _dot(lo, e)


def _in_proj(x, g1, w_in, rider=None):
    t = x.shape[0]
    tm = min(t, 512)
    n_out = w_in.shape[1]

    def body(x_ref, g_ref, w_ref, proj_ref, h_ref):
        xf = x_ref[...]
        r = lax.rsqrt(jnp.mean(xf * xf, axis=-1, keepdims=True) + EPS)
        h = (xf * r * g_ref[...]).astype(BF16)
        h_ref[...] = h
        for k in range(n_out // D_MODEL):
            cols = slice(k * D_MODEL, (k + 1) * D_MODEL)
            proj_ref[:, cols] = _dot(h, w_ref[:, cols]).astype(BF16)

    return _call(
        body, name="in_proj", grid=(t // tm,), args=(x, g1, w_in),
        in_specs=[pl.BlockSpec((tm, D_MODEL), lambda i: (i, 0)),
                  pl.BlockSpec((1, D_MODEL), lambda i: (0, 0)),
                  _resident((D_MODEL, n_out))],
        out_specs=[pl.BlockSpec((tm, n_out), lambda i: (i, 0)),
                   pl.BlockSpec((tm, D_MODEL), lambda i: (i, 0))],
        out_shape=[jax.ShapeDtypeStruct((t, n_out), BF16), jax.ShapeDtypeStruct((t, D_MODEL), BF16)],
        semantics=("parallel",), rider=rider)


def _proj(name, h, w, b=None, rider=None):
    t = h.shape[0]
    tm = min(t, 512)
    n_out = w.shape[1]

    def body(*refs):
        h_ref, w_ref, o_ref = refs[0], refs[1], refs[-1]
        hv = h_ref[...]
        for k in range(n_out // D_MODEL):
            cols = slice(k * D_MODEL, (k + 1) * D_MODEL)
            y = _dot(hv, w_ref[:, cols])
            if b is not None:
                y = jax.nn.sigmoid(y + refs[2][:, cols])
            o_ref[:, cols] = y.astype(BF16)

    in_specs = [pl.BlockSpec((tm, D_MODEL), lambda i: (i, 0)), _resident((D_MODEL, n_out))]
    args = (h, w)
    if b is not None:
        in_specs.append(pl.BlockSpec((1, n_out), lambda i: (0, 0)))
        args = (h, w, b)
    return _call(
        body, name=name, grid=(t // tm,), args=args, in_specs=in_specs,
        out_specs=[pl.BlockSpec((tm, n_out), lambda i: (i, 0))],
        out_shape=[jax.ShapeDtypeStruct((t, n_out), BF16)],
        semantics=("parallel",), rider=rider)


def _qknorm_fwd(proj, gq, gk):
    t = proj.shape[0]
    tm = min(t, 512)
    scale = HEAD_DIM ** -0.5 * LOG2E

    def body(q_ref, k_ref, gq_ref, gk_ref, qn_ref, kn_ref):
        e = _head_sum_matrix()
        for src, g_ref, dst, sc in ((q_ref, gq_ref, qn_ref, scale), (k_ref, gk_ref, kn_ref, 1.0)):
            for s in range(D_MODEL // LANES):
                sl = slice(s * LANES, (s + 1) * LANES)
                xf = src[:, sl].astype(F32)
                r = lax.rsqrt(_head_sums(xf * xf, e) * (1.0 / HEAD_DIM) + EPS)
                dst[:, sl] = (xf * r * g_ref[:, sl] * sc).astype(BF16)

    return pl.pallas_call(
        body, name="qknorm_fwd", grid=(t // tm,),
        in_specs=[pl.BlockSpec((tm, D_MODEL), lambda i: (i, 0)),
                  pl.BlockSpec((tm, D_MODEL), lambda i: (i, 1)),
                  pl.BlockSpec((1, D_MODEL), lambda i: (0, 0)),
                  pl.BlockSpec((1, D_MODEL), lambda i: (0, 0))],
        out_specs=[pl.BlockSpec((tm, D_MODEL), lambda i: (i, 0))] * 2,
        out_shape=[jax.ShapeDtypeStruct((t, D_MODEL), BF16)] * 2,
        compiler_params=_params(("parallel",)),
    )(proj, proj, gq, gk)


def _bias_tiles(rel_bias, rider=None):
    by_dist = jnp.concatenate(
        [rel_bias[:, :2 * MAX_REL], jnp.broadcast_to(rel_bias[:, 2 * MAX_REL:], (N_HEADS, 2 * MAX_REL))], axis=1)
    by_dist = by_dist.reshape(N_HEADS, 1, SKEW)

    def body(f_ref, o_ref):
        jj = lax.broadcasted_iota(jnp.int32, (QB, QB), 0)
        ii = lax.broadcasted_iota(jnp.int32, (QB, QB), 1)
        for w in range(KW // QB):
            pos = jnp.broadcast_to(f_ref[0, :, KW - QB * w:KW - QB * w + QB], (QB, QB))
            neg = jnp.broadcast_to(f_ref[0, :, KW - QB * (w + 1):KW - QB * w], (QB, QB))
            pos = pltpu.roll(pos, 0, 1, stride=1, stride_axis=0)
            neg = pltpu.roll(neg, 0, 1, stride=1, stride_axis=0)
            tile = jnp.where(ii >= jj, pos, neg)
            kc = (jj + QB * w) // CHUNK
            qc = ii // CHUNK
            band = (kc >= qc) & (kc <= qc + N_PREV_CHUNKS)
            o_ref[0, QB * w:QB * (w + 1), :] = jnp.where(band, tile * LOG2E, NEG_INF)

    return _call(
        body, name="bias_tiles", grid=(N_HEADS,), args=(by_dist,),
        in_specs=[pl.BlockSpec((1, 1, SKEW), lambda h: (h, 0, 0))],
        out_specs=[pl.BlockSpec((1, KW, QB), lambda h: (h, 0, 0))],
        out_shape=[jax.ShapeDtypeStruct((N_HEADS, KW, QB), F32)],
        semantics=("parallel",), rider=rider)


def _window_specs(col0):
    return [pl.BlockSpec((QB, SLAB), functools.partial(
        lambda p, b, back: (jnp.maximum(b - back, 0), col0 + p), back=back)) for back in (2, 1, 0)]


def _attn_fwd(qn, kn, proj, bias, rider=None):
    t = qn.shape[0]
    nb = t // QB
    v_col0 = 2 * D_MODEL // SLAB

    def body(q_ref, k0, k1, k2, v0, v1, v2, bias_ref, o_ref, lse_ref):
        b = pl.program_id(1)

        @pl.when(b < 2)
        def _():
            compute(q_ref, k0, k1, k2, v0, v1, v2, bias_ref, o_ref, lse_ref,
                    lax.broadcasted_iota(jnp.int32, (KW, 1), 0) >= (2 - b) * QB)

        @pl.when(b >= 2)
        def _():
            compute(q_ref, k0, k1, k2, v0, v1, v2, bias_ref, o_ref, lse_ref, None)

    def compute(q_ref, k0, k1, k2, v0, v1, v2, bias_ref, o_ref, lse_ref, valid):
        head_a = lax.broadcasted_iota(jnp.int32, (1, LANES), 1) < HEAD_DIM

        def scores(head):
            hp, hh = divmod(head, 2)
            sl = slice(hp * LANES, (hp + 1) * LANES)
            k = jnp.concatenate([k0[:, sl], k1[:, sl], k2[:, sl]], axis=0)
            mine = head_a if hh == 0 else jnp.logical_not(head_a)
            s = _dot_nt(jnp.where(mine, k, jnp.zeros_like(k)), q_ref[:, sl]) + bias_ref[head]
            return s if valid is None else jnp.where(valid, s, NEG_INF)

        def weighted_values(head, s):
            hp, hh = divmod(head, 2)
            sl = slice(hp * LANES, (hp + 1) * LANES)
            v = jnp.concatenate([v0[:, sl], v1[:, sl], v2[:, sl]], axis=0)
            vt = v.astype(F32).T.astype(BF16)[hh * HEAD_DIM:(hh + 1) * HEAD_DIM]
            vt = jnp.concatenate([vt, jnp.ones((SUBLANES, KW), BF16)], axis=0)
            m = jnp.max(s, axis=0, keepdims=True)
            ov = _dot(vt, jnp.exp2(s - m).astype(BF16))
            l = ov[HEAD_DIM:HEAD_DIM + 1]
            return ov[:HEAD_DIM] / l, m + jnp.log2(l)

        outs, lses = [], []
        pending = scores(0)
        for head in range(2 * PAIRS):
            nxt = scores(head + 1) if head + 1 < 2 * PAIRS else None
            o, lse = weighted_values(head, pending)
            outs.append(o)
            lses.append(lse)
            pending = nxt
        for hp in range(PAIRS):
            sl = slice(hp * LANES, (hp + 1) * LANES)
            o_ref[:, sl] = jnp.concatenate([outs[2 * hp], outs[2 * hp + 1]], axis=0).T.astype(BF16)
        lse_ref[...] = jnp.concatenate(lses, axis=0)

    return _call(
        body, name="attn_fwd", grid=(D_MODEL // SLAB, nb), args=(qn, kn, kn, kn, proj, proj, proj, bias),
        in_specs=[pl.BlockSpec((QB, SLAB), lambda p, b: (b, p))] + _window_specs(0) + _window_specs(v_col0)
        + [pl.BlockSpec((2 * PAIRS, KW, QB), lambda p, b: (p, 0, 0))],
        out_specs=[pl.BlockSpec((QB, SLAB), lambda p, b: (b, p)),
                   pl.BlockSpec((2 * PAIRS, QB), lambda p, b: (p, b))],
        out_shape=[jax.ShapeDtypeStruct((t, D_MODEL), BF16), jax.ShapeDtypeStruct((N_HEADS, t), F32)],
        semantics=("parallel", "arbitrary"), rider=rider)


def _shift_down(u, halo, n):
    rows = lax.broadcasted_iota(jnp.int32, (u.shape[0], 1), 0)
    out = pltpu.roll(u, n, 0)
    for j in range(n):
        out = jnp.where(rows == j, halo[SUBLANES - n + j:SUBLANES - n + j + 1, :], out)
    return out


def _shift_up(u, halo, n):
    tm = u.shape[0]
    rows = lax.broadcasted_iota(jnp.int32, (tm, 1), 0)
    out = pltpu.roll(u, tm - n, 0)
    for j in range(n):
        out = jnp.where(rows == tm - n + j, halo[j:j + 1, :], out)
    return out


def _conv_fwd(proj, conv_w, conv_b):
    t = proj.shape[0]
    tm = min(t, 512)
    hb = tm // SUBLANES

    def body(bg_ref, cg_ref, xc_ref, cgh_ref, xch_ref, w_ref, b_ref, o_ref):
        i = pl.program_id(0)
        u = cg_ref[...].astype(F32) * xc_ref[...].astype(F32)
        halo = cgh_ref[...].astype(F32) * xch_ref[...].astype(F32)
        halo = jnp.where(i > 0, halo, 0.0)
        w = w_ref[...]
        s = w[0:1] * _shift_down(u, halo, 2) + w[1:2] * _shift_down(u, halo, 1) + w[2:3] * u
        o_ref[...] = (bg_ref[...].astype(F32) * (b_ref[...] + s)).astype(BF16)

    def prev(col):
        return pl.BlockSpec((SUBLANES, D_MODEL), lambda i: (jnp.maximum(i * hb - 1, 0), col))

    return pl.pallas_call(
        body, name="conv_fwd", grid=(t // tm,),
        in_specs=[pl.BlockSpec((tm, D_MODEL), lambda i: (i, 3)),
                  pl.BlockSpec((tm, D_MODEL), lambda i: (i, 4)),
                  pl.BlockSpec((tm, D_MODEL), lambda i: (i, 5)),
                  prev(4), prev(5),
                  pl.BlockSpec((3, D_MODEL), lambda i: (0, 0)),
                  pl.BlockSpec((1, D_MODEL), lambda i: (0, 0))],
        out_specs=pl.BlockSpec((tm, D_MODEL), lambda i: (i, 0)),
        out_shape=jax.ShapeDtypeStruct((t, D_MODEL), BF16),
        compiler_params=_params(("parallel",)),
    )(proj, proj, proj, proj, proj, conv_w, conv_b)


def _mix_out(y_attn, y_conv, gates, x, w_ap, w_cp, w_out, g2):
    t = x.shape[0]
    tm = min(t, 512)

    def body(ya_in, yc_in, g_ref, x_ref, wap, wcp, wout, g2_ref, ya_ref, yc_ref, mg_ref, x1_ref, h2_ref):
        ya = _dot(ya_in[...], wap[...])
        yc = _dot(yc_in[...], wcp[...])
        ya_ref[...] = ya.astype(BF16)
        yc_ref[...] = yc.astype(BF16)
        merged = (g_ref[:, :D_MODEL].astype(F32) * ya + g_ref[:, D_MODEL:].astype(F32) * yc).astype(BF16)
        mg_ref[...] = merged
        x1 = x_ref[...] + _dot(merged, wout[...])
        x1_ref[...] = x1
        r = lax.rsqrt(jnp.mean(x1 * x1, axis=-1, keepdims=True) + EPS)
        h2_ref[...] = (x1 * r * g2_ref[...]).astype(BF16)

    row = pl.BlockSpec((tm, D_MODEL), lambda i: (i, 0))
    full = _resident((D_MODEL, D_MODEL))
    return pl.pallas_call(
        body, name="mix_out", grid=(t // tm,),
        in_specs=[row, row, pl.BlockSpec((tm, 2 * D_MODEL), lambda i: (i, 0)), row, full, full, full,
                  pl.BlockSpec((1, D_MODEL), lambda i: (0, 0))],
        out_specs=[row] * 5,
        out_shape=[jax.ShapeDtypeStruct((t, D_MODEL), BF16)] * 3
        + [jax.ShapeDtypeStruct((t, D_MODEL), F32), jax.ShapeDtypeStruct((t, D_MODEL), BF16)],
        compiler_params=_params(("parallel",)),
    )(y_attn, y_conv, gates, x, w_ap, w_cp, w_out, g2)


def _mlp_fwd(h2, w_up, w_down, x1, target):
    t = h2.shape[0]
    tm = min(t, 512)
    tf = 1024
    nf = D_FF // tf

    def body(h2_ref, wup, wdn, x1_ref, tg_ref, a_ref, dy_ref, dyb_ref, loss_ref):
        h2v = h2_ref[...]
        acc = None
        pending = _dot(h2v, wup[:, 0:tf])
        for j in range(nf):
            cols = slice(j * tf, (j + 1) * tf)
            a = pending
            if j + 1 < nf:
                pending = _dot(h2v, wup[:, (j + 1) * tf:(j + 2) * tf])
            a_ref[:, cols] = a.astype(BF16)
            part = _dot(jnp.square(jnp.maximum(a, 0.0)).astype(BF16), wdn[cols, :])
            acc = part if acc is None else acc + part

        @pl.when(pl.program_id(0) == 0)
        def _():
            loss_ref[...] = jnp.zeros_like(loss_ref)

        diff = x1_ref[...] + acc - tg_ref[...]
        loss_ref[...] += _fold8(diff * diff)
        dy = diff * (1.0 / D_MODEL)
        dy_ref[...] = dy
        dyb_ref[...] = dy.astype(BF16)

    row = pl.BlockSpec((tm, D_MODEL), lambda i: (i, 0))
    return pl.pallas_call(
        body, name="mlp_fwd", grid=(t // tm,),
        in_specs=[row, _resident((D_MODEL, D_FF)), _resident((D_FF, D_MODEL)), row, row],
        out_specs=[pl.BlockSpec((tm, D_FF), lambda i: (i, 0)), row, row,
                   pl.BlockSpec((SUBLANES, D_MODEL), lambda i: (0, 0))],
        out_shape=[jax.ShapeDtypeStruct((t, D_FF), BF16), jax.ShapeDtypeStruct((t, D_MODEL), F32),
                   jax.ShapeDtypeStruct((t, D_MODEL), BF16), jax.ShapeDtypeStruct((SUBLANES, D_MODEL), F32)],
        compiler_params=_params(("arbitrary",)),
    )(h2, w_up, w_down, x1, target)


def _rmsnorm_bwd(xf, g, dh):
    r = lax.rsqrt(jnp.mean(xf * xf, axis=-1, keepdims=True) + EPS)
    xh = xf * r
    dxh = dh * g
    dx = r * (dxh - xh * jnp.mean(dxh * xh, axis=-1, keepdims=True))
    return dx, dh * xh


def _mlp_bwd(dyb, a, w_down, w_up, x1, dy, g2):
    t = dyb.shape[0]
    tm = min(t, 512)
    tf = 1024
    nf = D_FF // tf

    def body(dyb_ref, a_ref, wdn, wup, x1_ref, dy_ref, g2_ref, da_ref, dx1_ref, dx1b_ref, dg2_ref):
        dyv = dyb_ref[...]
        acc = None
        pending = _dot_nt(dyv, wdn[0:tf, :])
        for j in range(nf):
            cols = slice(j * tf, (j + 1) * tf)
            du = pending
            if j + 1 < nf:
                pending = _dot_nt(dyv, wdn[(j + 1) * tf:(j + 2) * tf, :])
            da = (du * (2.0 * jnp.maximum(a_ref[:, cols].astype(F32), 0.0))).astype(BF16)
            da_ref[:, cols] = da
            part = _dot_nt(da, wup[:, cols])
            acc = part if acc is None else acc + part

        @pl.when(pl.program_id(0) == 0)
        def _():
            dg2_ref[...] = jnp.zeros_like(dg2_ref)

        dx, dg = _rmsnorm_bwd(x1_ref[...], g2_ref[...], acc)
        dx1 = dy_ref[...] + dx
        dx1_ref[...] = dx1
        dx1b_ref[...] = dx1.astype(BF16)
        dg2_ref[...] += _fold8(dg)

    row = pl.BlockSpec((tm, D_MODEL), lambda i: (i, 0))
    wide = pl.BlockSpec((tm, D_FF), lambda i: (i, 0))
    return pl.pallas_call(
        body, name="mlp_bwd", grid=(t // tm,),
        in_specs=[row, wide, _resident((D_FF, D_MODEL)), _resident((D_MODEL, D_FF)), row, row,
                  pl.BlockSpec((1, D_MODEL), lambda i: (0, 0))],
        out_specs=[wide, row, row, pl.BlockSpec((SUBLANES, D_MODEL), lambda i: (0, 0))],
        out_shape=[jax.ShapeDtypeStruct((t, D_FF), BF16), jax.ShapeDtypeStruct((t, D_MODEL), F32),
                   jax.ShapeDtypeStruct((t, D_MODEL), BF16), jax.ShapeDtypeStruct((SUBLANES, D_MODEL), F32)],
        compiler_params=_params(("arbitrary",)),
    )(dyb, a, w_down, w_up, x1, dy, g2)


def _wgrad(name, lhs, rhs_list, rhs_slabs, relu_sq=False, token_block=2048):
    t, m = lhs.shape
    tt = min(t, token_block)
    tmo = min(m, 1024)
    n_slab = sum(rhs_slabs)
    starts = [sum(rhs_slabs[:n]) for n in range(len(rhs_slabs))]
    n_rhs = len(rhs_list)

    def body(*refs):
        l_ref, r_refs, o_ref, acc = refs[0], refs[1:1 + n_rhs], refs[1 + n_rhs], refs[2 + n_rhs]
        k, s = pl.program_id(1), pl.program_id(2)
        lv = l_ref[...]
        if relu_sq:
            lv = jnp.square(jnp.maximum(lv.astype(F32), 0.0)).astype(BF16)

        @pl.when(s == 0)
        def _():
            acc[...] = jnp.zeros_like(acc)

        for n in range(n_rhs):
            @pl.when((k >= starts[n]) & (k < starts[n] + rhs_slabs[n]))
            def _(n=n):
                acc[...] += _dot_tn(lv, r_refs[n][...])

        @pl.when(s == pl.num_programs(2) - 1)
        def _():
            o_ref[...] = acc[...].astype(BF16)

    def rhs_spec(n):
        lo, cnt = starts[n], rhs_slabs[n]

        def index(i, k, s):
            inside = (k >= lo) & (k < lo + cnt)
            return (jnp.where(inside, s, 0), jnp.clip(k - lo, 0, cnt - 1))
        return pl.BlockSpec((tt, D_MODEL), index)

    return pl.pallas_call(
        body, name=name, grid=(m // tmo, n_slab, t // tt),
        in_specs=[pl.BlockSpec((tt, tmo), lambda i, k, s: (s, i))] + [rhs_spec(n) for n in range(n_rhs)],
        out_specs=pl.BlockSpec((tmo, D_MODEL), lambda i, k, s: (i, k)),
        out_shape=jax.ShapeDtypeStruct((m, n_slab * D_MODEL), BF16),
        scratch_shapes=[pltpu.VMEM((tmo, D_MODEL), F32)],
        compiler_params=_params(("parallel", "parallel", "arbitrary")),
    )(lhs, *rhs_list)


def _wgrad_group(name, triples):
    t = triples[0][0].shape[0]
    tt = min(t, 1024)
    counts = [n for _, _, n in triples]
    starts = [sum(counts[:n]) for n in range(len(counts))]
    n_prod = len(triples)

    def inside(n, k):
        return (k >= starts[n]) & (k < starts[n] + counts[n])

    def body(*refs):
        l_refs, r_refs, o_refs = refs[:n_prod], refs[n_prod:2 * n_prod], refs[2 * n_prod:3 * n_prod]
        acc = refs[3 * n_prod]
        k, s = pl.program_id(0), pl.program_id(1)

        @pl.when(s == 0)
        def _():
            acc[...] = jnp.zeros_like(acc)

        for n in range(n_prod):
            @pl.when(inside(n, k))
            def _(n=n):
                acc[...] += _dot_tn(l_refs[n][...], r_refs[n][...])

            @pl.when(inside(n, k) & (s == pl.num_programs(1) - 1))
            def _(n=n):
                o_refs[n][...] = acc[...].astype(BF16)

    def lhs_spec(n):
        return pl.BlockSpec((tt, D_MODEL), lambda k, s: (jnp.where(inside(n, k), s, 0), 0))

    def rhs_spec(n):
        return pl.BlockSpec((tt, D_MODEL), lambda k, s: (jnp.where(inside(n, k), s, 0),
                                                         jnp.clip(k - starts[n], 0, counts[n] - 1)))

    def out_spec(n):
        return pl.BlockSpec((D_MODEL, D_MODEL), lambda k, s: (0, jnp.clip(k - starts[n], 0, counts[n] - 1)))

    return pl.pallas_call(
        body, name=name, grid=(sum(counts), t // tt),
        in_specs=[lhs_spec(n) for n in range(n_prod)] + [rhs_spec(n) for n in range(n_prod)],
        out_specs=[out_spec(n) for n in range(n_prod)],
        out_shape=[jax.ShapeDtypeStruct((D_MODEL, n * D_MODEL), BF16) for n in counts],
        scratch_shapes=[pltpu.VMEM((D_MODEL, D_MODEL), F32)],
        compiler_params=_params(("arbitrary", "arbitrary")),
    )(*[tr[0] for tr in triples], *[tr[1] for tr in triples])


def _mix_bwd(dx1b, gates, ya, yc, w_out, w_ap, w_cp, w_g, rider=None):
    t = dx1b.shape[0]
    tm = min(t, 512)

    def body(dx_ref, g_ref, ya_ref, yc_ref, wout, wap, wcp, wg,
             dgp_ref, dya_ref, dyc_ref, dyat_ref, dycv_ref, dhg_ref, dbg_ref):
        dm = _dot_nt(dx_ref[...], wout[...])
        ga = g_ref[:, :D_MODEL].astype(F32)
        gc = g_ref[:, D_MODEL:].astype(F32)
        dya = (dm * ga).astype(BF16)
        dyc = (dm * gc).astype(BF16)
        dya_ref[...] = dya
        dyc_ref[...] = dyc
        dgpa = dm * ya_ref[...].astype(F32) * ga * (1.0 - ga)
        dgpc = dm * yc_ref[...].astype(F32) * gc * (1.0 - gc)

        @pl.when(pl.program_id(0) == 0)
        def _():
            dbg_ref[...] = jnp.zeros_like(dbg_ref)

        dbg_ref[:, :D_MODEL] += _fold8(dgpa)
        dbg_ref[:, D_MODEL:] += _fold8(dgpc)
        dgpa = dgpa.astype(BF16)
        dgpc = dgpc.astype(BF16)
        dgp_ref[:, :D_MODEL] = dgpa
        dgp_ref[:, D_MODEL:] = dgpc
        dyat_ref[...] = _dot_nt(dya, wap[...]).astype(BF16)
        dycv_ref[...] = _dot_nt(dyc, wcp[...]).astype(BF16)
        dhg_ref[...] = _dot_nt(dgpa, wg[:, :D_MODEL]) + _dot_nt(dgpc, wg[:, D_MODEL:])

    row = pl.BlockSpec((tm, D_MODEL), lambda i: (i, 0))
    row2 = pl.BlockSpec((tm, 2 * D_MODEL), lambda i: (i, 0))
    full = _resident((D_MODEL, D_MODEL))
    return _call(
        body, name="mix_bwd", grid=(t // tm,), args=(dx1b, gates, ya, yc, w_out, w_ap, w_cp, w_g),
        in_specs=[row, row2, row, row, full, full, full, _resident((D_MODEL, 2 * D_MODEL))],
        out_specs=[row2, row, row, row, row, row, pl.BlockSpec((SUBLANES, 2 * D_MODEL), lambda i: (0, 0))],
        out_shape=[jax.ShapeDtypeStruct((t, 2 * D_MODEL), BF16)] + [jax.ShapeDtypeStruct((t, D_MODEL), BF16)] * 4
        + [jax.ShapeDtypeStruct((t, D_MODEL), F32), jax.ShapeDtypeStruct((SUBLANES, 2 * D_MODEL), F32)],
        semantics=("arbitrary",), rider=rider)


def _conv_bwd(dyconv, proj, conv_w, conv_b, rider=None):
    t = proj.shape[0]
    tm = min(t, 512)
    hb = tm // SUBLANES
    last = t // SUBLANES - 1

    def body(dy_ref, dyn_ref, bg_ref, bgn_ref, cg_ref, cgp_ref, xc_ref, xcp_ref, w_ref, b_ref,
             o_ref, dcb_ref, dcw_ref):
        i = pl.program_id(0)
        cg = cg_ref[...].astype(F32)
        xc = xc_ref[...].astype(F32)
        bg = bg_ref[...].astype(F32)
        u = cg * xc
        prev = jnp.where(i > 0, cgp_ref[...].astype(F32) * xcp_ref[...].astype(F32), 0.0)
        u1 = _shift_down(u, prev, 1)
        u2 = _shift_down(u, prev, 2)
        w = w_ref[...]
        conv = b_ref[...] + (w[0:1] * u2 + w[1:2] * u1 + w[2:3] * u)
        dy = dy_ref[...].astype(F32)
        dconv = dy * bg
        nxt = jnp.where(i < pl.num_programs(0) - 1, dyn_ref[...].astype(F32) * bgn_ref[...].astype(F32), 0.0)
        du = w[2:3] * dconv + w[1:2] * _shift_up(dconv, nxt, 1) + w[0:1] * _shift_up(dconv, nxt, 2)
        o_ref[:, :D_MODEL] = (dy * conv).astype(BF16)
        o_ref[:, D_MODEL:2 * D_MODEL] = (du * xc).astype(BF16)
        o_ref[:, 2 * D_MODEL:] = (du * cg).astype(BF16)

        @pl.when(i == 0)
        def _():
            dcb_ref[...] = jnp.zeros_like(dcb_ref)
            dcw_ref[...] = jnp.zeros_like(dcw_ref)

        dcb_ref[...] += _fold8(dconv)
        dcw_ref[0:SUBLANES] += _fold8(dconv * u2)
        dcw_ref[SUBLANES:2 * SUBLANES] += _fold8(dconv * u1)
        dcw_ref[2 * SUBLANES:] += _fold8(dconv * u)

    def prev(col):
        return pl.BlockSpec((SUBLANES, D_MODEL), lambda i: (jnp.maximum(i * hb - 1, 0), col))

    def nxt(col):
        return pl.BlockSpec((SUBLANES, D_MODEL), lambda i: (jnp.minimum((i + 1) * hb, last), col))

    def cur(col):
        return pl.BlockSpec((tm, D_MODEL), lambda i: (i, col))

    return _call(
        body, name="conv_bwd", grid=(t // tm,),
        args=(dyconv, dyconv, proj, proj, proj, proj, proj, proj, conv_w, conv_b),
        in_specs=[cur(0), nxt(0), cur(3), nxt(3), cur(4), prev(4), cur(5), prev(5),
                  pl.BlockSpec((3, D_MODEL), lambda i: (0, 0)), pl.BlockSpec((1, D_MODEL), lambda i: (0, 0))],
        out_specs=[pl.BlockSpec((tm, 3 * D_MODEL), lambda i: (i, 0)),
                   pl.BlockSpec((SUBLANES, D_MODEL), lambda i: (0, 0)),
                   pl.BlockSpec((3 * SUBLANES, D_MODEL), lambda i: (0, 0))],
        out_shape=[jax.ShapeDtypeStruct((t, 3 * D_MODEL), BF16), jax.ShapeDtypeStruct((SUBLANES, D_MODEL), F32),
                   jax.ShapeDtypeStruct((3 * SUBLANES, D_MODEL), F32)],
        semantics=("arbitrary",), rider=rider)


def _attn_bwd(qn, kn, proj, dyattn, y_attn, lse, bias, rider=None):
    t = qn.shape[0]
    nb = t // QB
    v_col0 = 2 * D_MODEL // SLAB

    def body(q_ref, k0, k1, k2, v0, v1, v2, do_ref, o_ref, lse_ref, bias_ref,
             dq_ref, dk_ref, dv_ref, db_ref, acck, accv):
        b = pl.program_id(1)

        @pl.when(b == 0)
        def _():
            acck[...] = jnp.zeros_like(acck)
            accv[...] = jnp.zeros_like(accv)
            db_ref[...] = jnp.zeros_like(db_ref)

        def block(valid):
            head_a = lax.broadcasted_iota(jnp.int32, (1, LANES), 1) < HEAD_DIM

            def window(refs, hp):
                sl = slice(hp * LANES, (hp + 1) * LANES)
                return jnp.concatenate([r[:, sl] for r in refs], axis=0)

            def transposed(x, hh):
                return x.astype(F32).T.astype(BF16)[hh * HEAD_DIM:(hh + 1) * HEAD_DIM]

            def probs(head):
                hp, hh = divmod(head, 2)
                sl = slice(hp * LANES, (hp + 1) * LANES)
                mine = head_a if hh == 0 else jnp.logical_not(head_a)
                k = window((k0, k1, k2), hp)
                s = _dot_nt(jnp.where(mine, k, jnp.zeros_like(k)), q_ref[:, sl]) + bias_ref[head]
                s = s if valid is None else jnp.where(valid, s, NEG_INF)
                return jnp.exp2(s - lse_ref[head:head + 1, :])

            def grads(head, p):
                hp, hh = divmod(head, 2)
                sl = slice(hp * LANES, (hp + 1) * LANES)
                rows = slice(hh * HEAD_DIM, (hh + 1) * HEAD_DIM)
                mine = head_a if hh == 0 else jnp.logical_not(head_a)
                do = do_ref[:, sl]
                v = window((v0, v1, v2), hp)
                delta = jnp.sum((do.astype(F32).T * o_ref[:, sl].astype(F32).T)[rows], axis=0, keepdims=True)
                ds = p * (_dot_nt(jnp.where(mine, v, jnp.zeros_like(v)), do) - delta)
                db_ref[head] += ds
                pb, dsb = p.astype(BF16), ds.astype(BF16)
                dvt = _dot_nt(transposed(do, hh), pb)
                dkt = _dot_nt(transposed(q_ref[:, sl], hh), dsb) * (1.0 / LOG2E)
                dqt = _dot(transposed(window((k0, k1, k2), hp), hh), dsb)
                return dqt, dkt, dvt

            out = []
            pending = probs(0)
            for head in range(2 * PAIRS):
                nxt = probs(head + 1) if head + 1 < 2 * PAIRS else None
                out.append(grads(head, pending))
                pending = nxt
            for hp in range(PAIRS):
                sl = slice(hp * LANES, (hp + 1) * LANES)
                dqt, dkt, dvt = (jnp.concatenate([out[2 * hp][n], out[2 * hp + 1][n]], axis=0) for n in range(3))
                dq_ref[:, sl] = dqt.T.astype(BF16)
                for w in range(3):
                    slot = lax.rem(b + w + 1, 3)
                    cols = slice(w * QB, (w + 1) * QB)
                    if w == 2:
                        acck[hp, slot] = dkt[:, cols]
                        accv[hp, slot] = dvt[:, cols]
                    else:
                        acck[hp, slot] += dkt[:, cols]
                        accv[hp, slot] += dvt[:, cols]

        @pl.when(b < 2)
        def _():
            block(lax.broadcasted_iota(jnp.int32, (KW, 1), 0) >= (2 - b) * QB)

        @pl.when((b >= 2) & (b < nb))
        def _():
            block(None)

        done = lax.rem(b + 1, 3)
        for hp in range(PAIRS):
            sl = slice(hp * LANES, (hp + 1) * LANES)
            dk_ref[:, sl] = acck[hp, done].T.astype(BF16)
            dv_ref[:, sl] = accv[hp, done].T.astype(BF16)

    def cur(p, b):
        return (jnp.minimum(b, nb - 1), p)

    def window(col0):
        return [pl.BlockSpec((QB, SLAB), functools.partial(
            lambda p, b, back: (jnp.maximum(jnp.minimum(b, nb - 1) - back, 0), col0 + p), back=back))
            for back in (2, 1, 0)]

    def done_block(p, b):
        return (jnp.maximum(b - 2, 0), p)

    tile = pl.BlockSpec((2 * PAIRS, KW, QB), lambda p, b: (p, 0, 0))
    here = pl.BlockSpec((QB, SLAB), cur)
    return _call(
        body, name="attn_bwd", grid=(D_MODEL // SLAB, nb + 2),
        args=(qn, kn, kn, kn, proj, proj, proj, dyattn, y_attn, lse, bias),
        in_specs=[here] + window(0) + window(v_col0)
        + [here, here, pl.BlockSpec((2 * PAIRS, QB), lambda p, b: (p, jnp.minimum(b, nb - 1))), tile],
        out_specs=[here, pl.BlockSpec((QB, SLAB), done_block), pl.BlockSpec((QB, SLAB), done_block), tile],
        out_shape=[jax.ShapeDtypeStruct((t, D_MODEL), BF16)] * 3 + [jax.ShapeDtypeStruct((N_HEADS, KW, QB), F32)],
        scratch_shapes=[pltpu.VMEM((PAIRS, 3, LANES, QB), F32), pltpu.VMEM((PAIRS, 3, LANES, QB), F32)],
        semantics=("parallel", "arbitrary"), rider=rider)


def _qknorm_bwd(proj, dqn, dkn, gq, gk):
    t = proj.shape[0]
    tm = min(t, 512)
    scale = HEAD_DIM ** -0.5

    def body(q_ref, k_ref, dqn_ref, dkn_ref, gq_ref, gk_ref, o_ref, dgq_ref, dgk_ref):
        e = _head_sum_matrix()

        @pl.when(pl.program_id(0) == 0)
        def _():
            dgq_ref[...] = jnp.zeros_like(dgq_ref)
            dgk_ref[...] = jnp.zeros_like(dgk_ref)

        for n, (src, dn_ref, g_ref, dg_ref, sc) in enumerate(
                ((q_ref, dqn_ref, gq_ref, dgq_ref, scale), (k_ref, dkn_ref, gk_ref, dgk_ref, 1.0))):
            for s in range(D_MODEL // LANES):
                sl = slice(s * LANES, (s + 1) * LANES)
                xf = src[:, sl].astype(F32)
                r = lax.rsqrt(_head_sums(xf * xf, e) * (1.0 / HEAD_DIM) + EPS)
                xh = xf * r
                dn = dn_ref[:, sl].astype(F32) * sc
                dg_ref[:, sl] += _fold8(dn * xh)
                dxh = dn * g_ref[:, sl]
                mean = _head_sums(dxh * xh, e) * (1.0 / HEAD_DIM)
                o_ref[:, n * D_MODEL + s * LANES:n * D_MODEL + (s + 1) * LANES] = (r * (dxh - xh * mean)).astype(BF16)

    row = pl.BlockSpec((tm, D_MODEL), lambda i: (i, 0))
    vec = pl.BlockSpec((1, D_MODEL), lambda i: (0, 0))
    acc = pl.BlockSpec((SUBLANES, D_MODEL), lambda i: (0, 0))
    return pl.pallas_call(
        body, name="qknorm_bwd", grid=(t // tm,),
        in_specs=[row, pl.BlockSpec((tm, D_MODEL), lambda i: (i, 1)), row, row, vec, vec],
        out_specs=[pl.BlockSpec((tm, 2 * D_MODEL), lambda i: (i, 0)), acc, acc],
        out_shape=[jax.ShapeDtypeStruct((t, 2 * D_MODEL), BF16)] + [jax.ShapeDtypeStruct((SUBLANES, D_MODEL), F32)] * 2,
        compiler_params=_params(("arbitrary",)),
    )(proj, proj, dqn, dkn, gq, gk)


def _in_bwd(dqk, dv, dconv, w_in, dhg, x, g1, dx1, rider=None):
    t = x.shape[0]
    tm = min(t, 512)

    def body(dqk_ref, dv_ref, dc_ref, w_ref, dhg_ref, x_ref, g_ref, dx1_ref, dx_ref, dg_ref):
        acc = dhg_ref[...]
        slab = 0
        for src, n in ((dqk_ref, 2), (dv_ref, 1), (dc_ref, 3)):
            for s in range(n):
                acc = acc + _dot_nt(src[:, s * D_MODEL:(s + 1) * D_MODEL],
                                    w_ref[:, slab * D_MODEL:(slab + 1) * D_MODEL])
                slab += 1

        @pl.when(pl.program_id(0) == 0)
        def _():
            dg_ref[...] = jnp.zeros_like(dg_ref)

        dx, dg = _rmsnorm_bwd(x_ref[...], g_ref[...], acc)
        dx_ref[...] = dx1_ref[...] + dx
        dg_ref[...] += _fold8(dg)

    row = pl.BlockSpec((tm, D_MODEL), lambda i: (i, 0))
    return _call(
        body, name="in_bwd", grid=(t // tm,), args=(dqk, dv, dconv, w_in, dhg, x, g1, dx1),
        in_specs=[pl.BlockSpec((tm, 2 * D_MODEL), lambda i: (i, 0)), row,
                  pl.BlockSpec((tm, 3 * D_MODEL), lambda i: (i, 0)),
                  _resident(w_in.shape), row, row, pl.BlockSpec((1, D_MODEL), lambda i: (0, 0)), row],
        out_specs=[row, pl.BlockSpec((SUBLANES, D_MODEL), lambda i: (0, 0))],
        out_shape=[jax.ShapeDtypeStruct((t, D_MODEL), F32), jax.ShapeDtypeStruct((SUBLANES, D_MODEL), F32)],
        semantics=("arbitrary",), rider=rider)


def _bias_grad_fold(dbias, rider=None):
    def body(d_ref, o_ref):
        jj = lax.broadcasted_iota(jnp.int32, (QB, QB), 0)
        ii = lax.broadcasted_iota(jnp.int32, (QB, QB), 1)
        flip = (jj + ii == QB - 1).astype(BF16)
        low = jj + ii < QB
        pos, neg = [], []
        for w in range(KW // QB):
            x = d_ref[0, QB * w:QB * (w + 1), :]
            hi = x.astype(BF16)
            r1 = x - hi.astype(F32)
            mid = r1.astype(BF16)
            lo = (r1 - mid.astype(F32)).astype(BF16)
            xr = _dot(hi, flip) + _dot(mid, flip) + _dot(lo, flip)
            for keep, acc in ((low, pos), (jnp.logical_not(low), neg)):
                part = pltpu.roll(jnp.where(keep, xr, 0.0), 0, 1, stride=1, stride_axis=0)
                acc.append(jnp.sum(part, axis=0, keepdims=True))
        far = pos[1] + neg[0] + pos[0]
        o_ref[0] = jnp.zeros((SUBLANES, QB), F32)
        o_ref[0, 0:1, :] = neg[2]
        o_ref[0, 1:2, :] = pos[2] + neg[1]
        o_ref[0, 2:3, :] = jnp.broadcast_to(jnp.sum(far, axis=-1, keepdims=True), (1, QB))

    return _call(
        body, name="bias_grad_fold", grid=(N_HEADS,), args=(dbias,),
        in_specs=[pl.BlockSpec((1, KW, QB), lambda h: (h, 0, 0))],
        out_specs=[pl.BlockSpec((1, SUBLANES, QB), lambda h: (h, 0, 0))],
        out_shape=[jax.ShapeDtypeStruct((N_HEADS, SUBLANES, QB), F32)],
        semantics=("parallel",), rider=rider)


def _small_partials(dg1, dgq, dgk, dcb, dcw, dbg, dg2, dbias_fold, loss_tile):
    def head_fold(v):
        acc = v[:, 0:LANES]
        for s in range(1, D_MODEL // LANES):
            acc = acc + v[:, s * LANES:(s + 1) * LANES]
        return acc + pltpu.roll(acc, HEAD_DIM, 1)

    def body(dg1_ref, dgq_ref, dgk_ref, dcb_ref, dcw_ref, dbg_ref, dg2_ref, db_ref, loss_ref, o_ref):
        o_ref[...] = jnp.zeros_like(o_ref)
        o_ref[0:1, :] = jnp.sum(dg1_ref[...], axis=0, keepdims=True)
        o_ref[1:2, 0:LANES] = head_fold(jnp.sum(dgq_ref[...], axis=0, keepdims=True))
        o_ref[2:3, 0:LANES] = head_fold(jnp.sum(dgk_ref[...], axis=0, keepdims=True))
        o_ref[3:4, :] = jnp.sum(dcb_ref[...], axis=0, keepdims=True)
        for j in range(3):
            o_ref[4 + j:5 + j, :] = jnp.sum(dcw_ref[j * SUBLANES:(j + 1) * SUBLANES, :], axis=0, keepdims=True)
        o_ref[7:8, :] = jnp.sum(dbg_ref[:, :D_MODEL], axis=0, keepdims=True)
        o_ref[8:9, :] = jnp.sum(dbg_ref[:, D_MODEL:], axis=0, keepdims=True)
        o_ref[9:10, :] = jnp.sum(dg2_ref[...], axis=0, keepdims=True)
        for h in range(N_HEADS):
            for part in range(3):
                o_ref[10 + h:11 + h, part * QB:(part + 1) * QB] = db_ref[h, part:part + 1, :]
        loss = (0.5 / D_MODEL) * jnp.sum(jnp.sum(loss_ref[...], axis=0, keepdims=True), axis=-1, keepdims=True)
        o_ref[26:27, :] = jnp.broadcast_to(loss, (1, D_MODEL))

    return pl.pallas_call(
        body, name="small_partials",
        out_shape=jax.ShapeDtypeStruct((32, D_MODEL), F32),
        compiler_params=_params(),
    )(dg1, dgq, dgk, dcb, dcw, dbg, dg2, dbias_fold, loss_tile)


MID_AXES = (0, 0, 1, 0)
MLP_AXES = (1, 0)


def _local_step(x, target, norm1_g, q_norm_g, k_norm_g, bias, conv_w, conv_b, b_gate, norm2_g,
                w_in, mid_w, mlp_w, distributed):
    g1 = norm1_g.reshape(1, D_MODEL)
    g2 = norm2_g.reshape(1, D_MODEL)
    gq = jnp.tile(q_norm_g, N_HEADS).reshape(1, D_MODEL)
    gk = jnp.tile(k_norm_g, N_HEADS).reshape(1, D_MODEL)
    cb = conv_b.reshape(1, D_MODEL)

    (proj, h), got = _in_proj(x, g1, w_in, rider=_Gather(mid_w, MID_AXES) if distributed else None)
    w_ap, w_cp, w_g, w_out = got if distributed else mid_w
    (gates,), _ = _proj("gate_proj", h, w_g, b=b_gate.reshape(1, 2 * D_MODEL))
    qn, kn = _qknorm_fwd(proj, gq, gk)
    (y_attn, lse), got = _attn_fwd(qn, kn, proj, bias, rider=_Gather(mlp_w, MLP_AXES) if distributed else None)
    w_up, w_down = got if distributed else mlp_w
    y_conv = _conv_fwd(proj, conv_w, cb)
    ya, yc, merged, x1, h2 = _mix_out(y_attn, y_conv, gates, x, w_ap, w_cp, w_out, g2)
    a, dy, dyb, loss_tile = _mlp_fwd(h2, w_up, w_down, x1, target)

    da, dx1, dx1b, dg2 = _mlp_bwd(dyb, a, w_down, w_up, x1, dy, g2)
    gw_down = _wgrad("wgrad_down", a, [dyb], [1], relu_sq=True, token_block=4096)
    gw_up = _wgrad("wgrad_up", h2, [da], [D_FF // D_MODEL])
    (dgp, dya, dyc, dyattn, dyconv, dhg, dbg), mlp_swapped = _mix_bwd(
        dx1b, gates, ya, yc, w_out, w_ap, w_cp, w_g,
        rider=_PairSwap((gw_up, gw_down), MLP_AXES) if distributed else None)
    mid = tuple(_wgrad_group("wgrad_mid", [(y_attn, dya, 1), (y_conv, dyc, 1), (h, dgp, 2), (merged, dx1b, 1)]))
    gw_ap, gw_cp, gw_g, gw_out = mid
    (dconv, dcb, dcw), mid_swapped = _conv_bwd(
        dyconv, proj, conv_w, cb, rider=_PairSwap(mid, MID_AXES) if distributed else None)
    early = mid + (gw_up, gw_down)
    early_sums = (_pair_add(early, tuple(mid_swapped) + tuple(mlp_swapped), MID_AXES + MLP_AXES)
                  if distributed else None)
    (dqn, dkn, dv, dbias), early_shares = _attn_bwd(
        qn, kn, proj, dyattn, y_attn, lse, bias, rider=_ChipScatter(early_sums) if distributed else None)
    dqk, dgq, dgk = _qknorm_bwd(proj, dqn, dkn, gq, gk)
    gw_in = _wgrad("wgrad_in", h, [dqk, dv, dconv], [2, 1, 3])
    (dbias_fold,), in_swapped = _bias_grad_fold(dbias, rider=_PairSwap((gw_in,), (1,)) if distributed else None)
    in_sums = _pair_add((gw_in,), in_swapped, (1,)) if distributed else None
    (dx, dg1), in_shares = _in_bwd(dqk, dv, dconv, w_in, dhg, x, g1, dx1,
                                   rider=_ChipScatter(in_sums) if distributed else None)
    small = _small_partials(dg1, dgq, dgk, dcb, dcw, dbg, dg2, dbias_fold, loss_tile)
    grads = tuple(in_shares) + tuple(early_shares) if distributed else (gw_in,) + early
    return dx, grads, small


def _me():
    return lax.axis_index("x"), lax.axis_index("y"), lax.axis_index("c")


def _peer(me, rel):
    x, y, c = me
    return (1 - x if rel & 4 else x, 1 - y if rel & 2 else y, 1 - c if rel & 1 else c)


def _linear(dev):
    return 4 * dev[0] + 2 * dev[1] + dev[2]


BIG_AXES = (1, 0, 0, 1, 0, 1, 0)


def _block(ref, axis, idx, size):
    return ref.at[pl.ds(idx * size, size), :] if axis == 0 else ref.at[:, pl.ds(idx * size, size)]


def _cast_shards(shards):
    def body(*refs):
        for src, dst in zip(refs[:len(shards)], refs[len(shards):]):
            dst[...] = src[...].astype(BF16)

    return pl.pallas_call(
        body, name="cast_shards",
        out_shape=[jax.ShapeDtypeStruct(s.shape, BF16) for s in shards],
        compiler_params=_params(),
    )(*shards)


class _Gather:
    def __init__(self, shards, axes):
        self.arrays, self.axes, self.n = list(shards), tuple(axes), len(shards)
        self.sizes = [s.shape[axis] for s, axis in zip(shards, axes)]
        self.out_shape = []
        for s, axis in zip(shards, axes):
            shape = (s.shape[0] * N_DEV, s.shape[1]) if axis == 0 else (s.shape[0], s.shape[1] * N_DEV)
            self.out_shape.append(jax.ShapeDtypeStruct(shape, s.dtype))
        self.scratch = [pltpu.SemaphoreType.DMA((self.n, 7)), pltpu.SemaphoreType.DMA((self.n, 7)),
                        pltpu.SemaphoreType.DMA((self.n,))]

    def _copies(self, srcs, outs, sems):
        send_sems, recv_sems, local_sems = sems
        me = _me()
        sibling = _peer(me, 1)
        chips = [_peer(me, rel) for rel in (4, 2, 6)]

        def rows(a, dev):
            return _block(outs[a], self.axes[a], _linear(dev), self.sizes[a])

        def copy(a, k, block_dev, to, src=None):
            return pltpu.make_async_remote_copy(
                src_ref=rows(a, block_dev) if src is None else src, dst_ref=rows(a, block_dev),
                send_sem=send_sems.at[a, k], recv_sem=recv_sems.at[a, k], device_id=to, device_id_type=MESH_T)

        own = [pltpu.make_async_copy(srcs[a], rows(a, me), local_sems.at[a]) for a in range(self.n)]
        first = []
        for a in range(self.n):
            first.append(copy(a, 0, me, sibling, src=srcs[a]))
            for j, chip in enumerate(chips):
                first.append(copy(a, 1 + j, me, chip, src=srcs[a]))
        return me, sibling, chips, copy, own, first

    def start(self, srcs, outs, sems):
        _, _, _, _, own, first = self._copies(srcs, outs, sems)
        for cp in own + first:
            cp.start()

    def finish(self, srcs, outs, sems):
        me, sibling, chips, copy, own, first = self._copies(srcs, outs, sems)
        passed = []
        for a in range(self.n):
            for j, chip in enumerate(chips):
                copy(a, 1 + j, chip, me).wait_recv()
                fwd = copy(a, 4 + j, chip, sibling)
                fwd.start()
                passed.append(fwd)
        for a in range(self.n):
            copy(a, 0, sibling, me).wait_recv()
            for j, chip in enumerate(chips):
                copy(a, 4 + j, _peer(chip, 1), me).wait_recv()
        for cp in first + passed:
            cp.wait_send()
        for cp in own:
            cp.wait()


N_CHIPS = 4


def _shard_shape(g, axis):
    return (g.shape[0] // N_DEV, g.shape[1]) if axis == 0 else (g.shape[0], g.shape[1] // N_DEV)


class _PairSwap:
    def __init__(self, grads, axes):
        self.arrays, self.axes, self.n = list(grads), tuple(axes), len(grads)
        self.sizes = [g.shape[axis] // N_DEV for g, axis in zip(grads, axes)]
        self.out_shape = [jax.ShapeDtypeStruct((N_CHIPS,) + _shard_shape(g, axis), g.dtype)
                          for g, axis in zip(grads, axes)]
        self.scratch = [pltpu.SemaphoreType.DMA((self.n, N_CHIPS)), pltpu.SemaphoreType.DMA((self.n, N_CHIPS))]

    def _copies(self, srcs, outs, sems):
        send_sems, recv_sems = sems
        x, y, c = _me()
        sibling = (x, y, 1 - c)
        copies = []
        for a in range(self.n):
            for chip in range(N_CHIPS):
                owner_idx = 2 * chip + (1 - c)
                copies.append(pltpu.make_async_remote_copy(
                    src_ref=_block(srcs[a], self.axes[a], owner_idx, self.sizes[a]), dst_ref=outs[a].at[chip],
                    send_sem=send_sems.at[a, chip], recv_sem=recv_sems.at[a, chip],
                    device_id=sibling, device_id_type=MESH_T))
        return copies

    def start(self, srcs, outs, sems):
        for cp in self._copies(srcs, outs, sems):
            cp.start()

    def finish(self, srcs, outs, sems):
        for cp in self._copies(srcs, outs, sems):
            cp.wait()


def _pair_add(grads, swapped, axes):
    n = len(grads)
    c_arr = lax.axis_index("c").astype(jnp.int32).reshape(1)

    def body(c_ref, *refs):
        del c_ref
        mine, got, outs = refs[:n], refs[n:2 * n], refs[2 * n:]
        for a in range(n):
            outs[a][0] = (mine[a][...].astype(F32) + got[a][0].astype(F32)).astype(BF16)

    in_specs, out_specs, out_shape = [], [], []
    for g, axis in zip(grads, axes):
        shard = _shard_shape(g, axis)
        if axis == 0:
            in_specs.append(pl.BlockSpec(shard, lambda s, c_ref: (2 * s + c_ref[0], 0)))
        else:
            in_specs.append(pl.BlockSpec(shard, lambda s, c_ref: (0, 2 * s + c_ref[0])))
    for g, axis in zip(grads, axes):
        shard = _shard_shape(g, axis)
        in_specs.append(pl.BlockSpec((1,) + shard, lambda s, c_ref: (s, 0, 0)))
        out_specs.append(pl.BlockSpec((1,) + shard, lambda s, c_ref: (s, 0, 0)))
        out_shape.append(jax.ShapeDtypeStruct((N_CHIPS,) + shard, BF16))
    return pl.pallas_call(
        body, name="pair_add_" + str(n),
        grid_spec=pltpu.PrefetchScalarGridSpec(num_scalar_prefetch=1, grid=(N_CHIPS,), in_specs=in_specs,
                                               out_specs=out_specs),
        out_shape=out_shape, compiler_params=_params(("arbitrary",)),
    )(c_arr, *grads, *swapped)


class _ChipScatter:
    def __init__(self, sums):
        self.arrays, self.n = list(sums), len(sums)
        self.out_shape = [jax.ShapeDtypeStruct(s.shape, s.dtype) for s in sums]
        self.scratch = [pltpu.SemaphoreType.DMA((self.n, 3)), pltpu.SemaphoreType.DMA((self.n, 3)),
                        pltpu.SemaphoreType.DMA((self.n,))]

    def _copies(self, srcs, outs, sems):
        send_sems, recv_sems, local_sems = sems
        me = _me()
        my_chip = 2 * me[0] + me[1]
        own = [pltpu.make_async_copy(srcs[a].at[my_chip], outs[a].at[my_chip], local_sems.at[a])
               for a in range(self.n)]
        sends, recvs = [], []
        for a in range(self.n):
            for k, rel in enumerate((4, 2, 6)):
                peer = _peer(me, rel)
                peer_chip = 2 * peer[0] + peer[1]
                sends.append(pltpu.make_async_remote_copy(
                    src_ref=srcs[a].at[peer_chip], dst_ref=outs[a].at[my_chip],
                    send_sem=send_sems.at[a, k], recv_sem=recv_sems.at[a, k], device_id=peer, device_id_type=MESH_T))
                recvs.append(pltpu.make_async_remote_copy(
                    src_ref=srcs[a].at[my_chip], dst_ref=outs[a].at[peer_chip],
                    send_sem=send_sems.at[a, k], recv_sem=recv_sems.at[a, k], device_id=peer, device_id_type=MESH_T))
        return own, sends, recvs

    def start(self, srcs, outs, sems):
        own, sends, _ = self._copies(srcs, outs, sems)
        for cp in own + sends:
            cp.start()

    def finish(self, srcs, outs, sems):
        own, sends, recvs = self._copies(srcs, outs, sems)
        for cp in recvs:
            cp.wait_recv()
        for cp in sends:
            cp.wait_send()
        for cp in own:
            cp.wait()


def _call(body, *, name, args, in_specs, out_specs, out_shape, grid=(), scratch_shapes=(), semantics=None,
          rider=None):
    if rider is None:
        return pl.pallas_call(
            body, name=name, grid=grid, in_specs=in_specs, out_specs=out_specs, out_shape=out_shape,
            scratch_shapes=list(scratch_shapes), compiler_params=_params(semantics))(*args), None
    n_in, n_out, n_scr, r = len(in_specs), len(out_specs), len(scratch_shapes), rider.n

    def wrapped(*refs):
        ins, r_ins = refs[:n_in], refs[n_in:n_in + r]
        outs = refs[n_in + r:n_in + r + n_out]
        r_outs = refs[n_in + r + n_out:n_in + 2 * r + n_out]
        scr = refs[n_in + 2 * r + n_out:n_in + 2 * r + n_out + n_scr]
        sems = refs[n_in + 2 * r + n_out + n_scr:]
        first, last = None, None
        for ax in range(len(grid)):
            f, l = pl.program_id(ax) == 0, pl.program_id(ax) == pl.num_programs(ax) - 1
            first = f if first is None else first & f
            last = l if last is None else last & l
        if first is None:
            rider.start(r_ins, r_outs, sems)
            body(*ins, *outs, *scr)
            rider.finish(r_ins, r_outs, sems)
            return

        @pl.when(first)
        def _():
            rider.start(r_ins, r_outs, sems)

        body(*ins, *outs, *scr)

        @pl.when(last)
        def _():
            rider.finish(r_ins, r_outs, sems)

    any_spec = pl.BlockSpec(memory_space=pl.ANY)
    out = pl.pallas_call(
        wrapped, name=name, grid=grid, in_specs=list(in_specs) + [any_spec] * r,
        out_specs=list(out_specs) + [any_spec] * r, out_shape=list(out_shape) + rider.out_shape,
        scratch_shapes=list(scratch_shapes) + rider.scratch,
        compiler_params=_params(None if semantics is None else ("arbitrary",) * len(semantics)),
    )(*args, *rider.arrays)
    return out[:n_out], out[n_out:]


def _exchange(name, rider):
    def body():
        pass

    return _call(body, name=name, args=(), in_specs=[], out_specs=[], out_shape=[], rider=rider)[1]


def _all_reduce_small(part):
    def body(p_ref, o_ref, slots, send_sems, recv_sems):
        me = _me()
        my_idx = _linear(me)
        slots[my_idx] = p_ref[...]
        sends = []
        for rel in range(1, N_DEV):
            cp = pltpu.make_async_remote_copy(
                src_ref=p_ref, dst_ref=slots.at[my_idx], send_sem=send_sems.at[rel - 1],
                recv_sem=recv_sems.at[rel - 1], device_id=_peer(me, rel), device_id_type=MESH_T)
            cp.start()
            sends.append(cp)
        for rel in range(1, N_DEV):
            frm = _peer(me, rel)
            pltpu.make_async_remote_copy(
                src_ref=p_ref, dst_ref=slots.at[_linear(frm)], send_sem=send_sems.at[rel - 1],
                recv_sem=recv_sems.at[rel - 1], device_id=frm, device_id_type=MESH_T).wait_recv()
        for cp in sends:
            cp.wait_send()
        total = slots[0]
        for d in range(1, N_DEV):
            total = total + slots[d]
        o_ref[...] = total

    return pl.pallas_call(
        body, name="all_reduce_small",
        in_specs=[pl.BlockSpec(memory_space=pltpu.VMEM)], out_specs=pl.BlockSpec(memory_space=pltpu.VMEM),
        out_shape=jax.ShapeDtypeStruct(part.shape, F32),
        scratch_shapes=[pltpu.VMEM((N_DEV,) + part.shape, F32), pltpu.SemaphoreType.DMA((7,)),
                        pltpu.SemaphoreType.DMA((7,))],
        compiler_params=_params(),
    )(part)


def _adamw_math(w, g, m, v):
    m = ADAM_B1 * m + (1.0 - ADAM_B1) * g
    v = ADAM_B2 * v + (1.0 - ADAM_B2) * jnp.square(g)
    m_hat = m / (1.0 - ADAM_B1 ** ADAM_STEP)
    v_hat = v / (1.0 - ADAM_B2 ** ADAM_STEP)
    delta = -ADAM_LR * (m_hat / (jnp.sqrt(v_hat) + ADAM_EPS) + ADAM_WD * w)
    return delta, m, v


ADAMW_STEPS = 4


def _adamw_big(shares, ws, ms, vs, rider=None):
    n = len(ws)

    def body(*refs):
        s_refs, w_refs, m_refs, v_refs = (refs[a * n:(a + 1) * n] for a in range(4))
        outs = refs[4 * n:]
        for a in range(n):
            g = s_refs[a][0].astype(F32)
            for d in range(1, N_CHIPS):
                g = g + s_refs[a][d].astype(F32)
            outs[4 * a][...] = g
            outs[4 * a + 1][...], outs[4 * a + 2][...], outs[4 * a + 3][...] = _adamw_math(
                w_refs[a][...], g, m_refs[a][...], v_refs[a][...])

    def chunk(w):
        return pl.BlockSpec((w.shape[0] // ADAMW_STEPS, w.shape[1]), lambda i: (i, 0))

    def share_chunk(w):
        return pl.BlockSpec((N_CHIPS, w.shape[0] // ADAMW_STEPS, w.shape[1]), lambda i: (0, i, 0))

    out, rider_out = _call(
        body, name="adamw_big", grid=(ADAMW_STEPS,), args=(*shares, *ws, *ms, *vs),
        in_specs=[share_chunk(w) for w in ws] + [chunk(w) for w in ws] * 3,
        out_specs=[chunk(w) for w in ws for _ in range(4)],
        out_shape=[jax.ShapeDtypeStruct(w.shape, F32) for w in ws for _ in range(4)],
        semantics=("parallel",), rider=rider)
    return [tuple(out[4 * a:4 * a + 4]) for a in range(n)], rider_out


def _adamw_small(quads):
    n = len(quads)

    def body(*refs):
        ins, outs = refs[:4 * n], refs[4 * n:]
        for p in range(n):
            g_ref, w_ref, m_ref, v_ref = ins[4 * p:4 * p + 4]
            d_ref, nm_ref, nv_ref = outs[3 * p:3 * p + 3]
            d_ref[...], nm_ref[...], nv_ref[...] = _adamw_math(w_ref[...], g_ref[...], m_ref[...], v_ref[...])

    flat = [a for quad in quads for a in quad]
    out = pl.pallas_call(
        body, name="adamw_small",
        out_shape=[jax.ShapeDtypeStruct(quad[1].shape, F32) for quad in quads for _ in range(3)],
        compiler_params=_params(),
    )(*flat)
    return [tuple(out[3 * p:3 * p + 3]) for p in range(n)]


def kernel(x, norm1_g, w_in, q_norm_g, k_norm_g, rel_bias, conv_w, conv_b, w_attn_proj, w_conv_proj, w_gate, b_gate, w_out, norm2_g, w_up, w_down, loss_target, m_norm1_g, m_w_in, m_q_norm_g, m_k_norm_g, m_rel_bias, m_conv_w, m_conv_b, m_w_attn_proj, m_w_conv_proj, m_w_gate, m_b_gate, m_w_out, m_norm2_g, m_w_up, m_w_down, v_norm1_g, v_w_in, v_q_norm_g, v_k_norm_g, v_rel_bias, v_conv_w, v_conv_b, v_w_attn_proj, v_w_conv_proj, v_w_gate, v_b_gate, v_w_out, v_norm2_g, v_w_up, v_w_down):
    my_idx = _linear(_me())
    big_w = (w_in, w_attn_proj, w_conv_proj, w_gate, w_out, w_up, w_down)
    big_m = (m_w_in, m_w_attn_proj, m_w_conv_proj, m_w_gate, m_w_out, m_w_up, m_w_down)
    big_v = (v_w_in, v_w_attn_proj, v_w_conv_proj, v_w_gate, v_w_out, v_w_up, v_w_down)
    big_names = ("w_in", "w_attn_proj", "w_conv_proj", "w_gate", "w_out", "w_up", "w_down")

    conv_w_tile = jnp.pad(conv_w, ((0, SUBLANES - conv_w.shape[0]), (0, 0)))
    shards = _cast_shards(big_w)
    (bias,), (w_in_full, conv_w_rows) = _bias_tiles(rel_bias, rider=_Gather((shards[0], conv_w_tile), (1, 1)))

    dx, shares, small = _local_step(x[0], loss_target[0], norm1_g, q_norm_g, k_norm_g, bias, conv_w_rows[:3],
                                    conv_b, b_gate, norm2_g, w_in_full, tuple(shards[1:5]), tuple(shards[5:7]), True)

    big_out, _ = _adamw_big(shares, big_w, big_m, big_v)
    tot = _all_reduce_small(small)
    g_rel_bias = jnp.concatenate(
        [tot[10:26, :QB][:, ::-1], tot[10:26, QB:2 * QB][:, ::-1], tot[10:26, 2 * QB:2 * QB + 1]], axis=1)
    g_conv_w = lax.dynamic_slice(tot[4:7], (0, my_idx * LANES), (3, LANES))
    small_g = [tot[0:1], tot[1:2, :HEAD_DIM], tot[2:3, :HEAD_DIM], g_rel_bias, g_conv_w, tot[3:4],
               tot[7:9].reshape(1, 2 * D_MODEL), tot[9:10]]
    small_w = (norm1_g, q_norm_g, k_norm_g, rel_bias, conv_w, conv_b, b_gate, norm2_g)
    small_m = (m_norm1_g, m_q_norm_g, m_k_norm_g, m_rel_bias, m_conv_w, m_conv_b, m_b_gate, m_norm2_g)
    small_v = (v_norm1_g, v_q_norm_g, v_k_norm_g, v_rel_bias, v_conv_w, v_conv_b, v_b_gate, v_norm2_g)

    def two_d(a):
        return a.reshape(1, -1) if a.ndim == 1 else a

    small_out = _adamw_small([(g, two_d(w), two_d(m), two_d(v))
                              for g, w, m, v in zip(small_g, small_w, small_m, small_v)])

    order = ("norm1_g", "w_in", "q_norm_g", "k_norm_g", "rel_bias", "conv_w", "conv_b", "w_attn_proj", "w_conv_proj",
             "w_gate", "b_gate", "w_out", "norm2_g", "w_up", "w_down")
    small_names = ("norm1_g", "q_norm_g", "k_norm_g", "rel_bias", "conv_w", "conv_b", "b_gate", "norm2_g")
    res = {}
    for name, (g, d, nm, nv) in zip(big_names, big_out):
        res[name] = (g, d, nm, nv)
    for name, g, w, (d, nm, nv) in zip(small_names, small_g, small_w, small_out):
        res[name] = tuple(a.reshape(w.shape) for a in (g, d, nm, nv))
    loss = tot[26, 0]
    return (loss, dx[None], *[res[n][0] for n in order], *[res[n][1] for n in order],
            *[res[n][2] for n in order], *[res[n][3] for n in order])
```

```python
import functools

import jax
import jax.numpy as jnp
from jax import lax
from jax.experimental import pallas as pl
from jax.experimental.pallas import tpu as pltpu

F32 = jnp.float32
BF16 = jnp.bfloat16

D_MODEL = 1024
N_HEADS = 16
HEAD_DIM = 64
CHUNK = 64
N_PREV_CHUNKS = 8
MAX_REL = 256
D_FF = 4096
EPS = 1e-6
NEG_INF = -1e30
LOG2E = 1.4426950408889634
N_DEV = 8

ADAM_LR = 0.001
ADAM_B1 = 0.9
ADAM_B2 = 0.999
ADAM_EPS = 1e-08
ADAM_WD = 0.01
ADAM_STEP = 10

LANES = 128
SUBLANES = 8
VMEM_LIMIT = 56 * 1024 * 1024
QB = 256
KW = 3 * QB
PAIRS = 4
SLAB = PAIRS * LANES
PAIRS_FWD = 8
SKEW = 1024

MESH_T = pl.DeviceIdType.MESH


def _dot(a, b):
    return jnp.dot(a, b, preferred_element_type=F32)


def _dot_nt(a, b):
    return lax.dot_general(a, b, (((1,), (1,)), ((), ())), preferred_element_type=F32)


def _dot_tn(a, b):
    return lax.dot_general(a, b, (((0,), (0,)), ((), ())), preferred_element_type=F32)


def _params(sem=None):
    return pltpu.CompilerParams(dimension_semantics=sem, vmem_limit_bytes=VMEM_LIMIT)


def _resident(shape):
    return pl.BlockSpec(shape, lambda *_: (0,) * len(shape), pipeline_mode=pl.Buffered(1))


def _fold8(v):
    rows, n = v.shape
    return v.reshape(rows // SUBLANES, SUBLANES, n).sum(axis=0)


def _head_sum_matrix():
    r = lax.broadcasted_iota(jnp.int32, (LANES, LANES), 0) // HEAD_DIM
    c = lax.broadcasted_iota(jnp.int32, (LANES, LANES), 1) // HEAD_DIM
    return (r == c).astype(BF16)


def _head_sums(v, e):
    hi = v.astype(BF16)
    lo = (v - hi.astype(F32)).astype(BF16)
    return _dot(hi, e) + _dot(lo, e)


def _in_proj(x, g1, w_in, rider=None):
    t = x.shape[0]
    tm = min(t, 512)
    n_out = w_in.shape[1]

    def body(x_ref, g_ref, w_ref, proj_ref, h_ref):
        xf = x_ref[...]
        r = lax.rsqrt(jnp.mean(xf * xf, axis=-1, keepdims=True) + EPS)
        h = (xf * r * g_ref[...]).astype(BF16)
        h_ref[...] = h
        for k in range(n_out // D_MODEL):
            cols = slice(k * D_MODEL, (k + 1) * D_MODEL)
            proj_ref[:, cols] = _dot(h, w_ref[:, cols]).astype(BF16)

    return _call(
        body, name="in_proj", grid=(t // tm,), args=(x, g1, w_in),
        in_specs=[pl.BlockSpec((tm, D_MODEL), lambda i: (i, 0)),
                  pl.BlockSpec((1, D_MODEL), lambda i: (0, 0)),
                  _resident((D_MODEL, n_out))],
        out_specs=[pl.BlockSpec((tm, n_out), lambda i: (i, 0)),
                   pl.BlockSpec((tm, D_MODEL), lambda i: (i, 0))],
        out_shape=[jax.ShapeDtypeStruct((t, n_out), BF16), jax.ShapeDtypeStruct((t, D_MODEL), BF16)],
        semantics=("parallel",), rider=rider)


def _proj(name, h, w, b=None, rider=None):
    t = h.shape[0]
    tm = min(t, 512)
    n_out = w.shape[1]

    def body(*refs):
        h_ref, w_ref, o_ref = refs[0], refs[1], refs[-1]
        hv = h_ref[...]
        for k in range(n_out // D_MODEL):
            cols = slice(k * D_MODEL, (k + 1) * D_MODEL)
            y = _dot(hv, w_ref[:, cols])
            if b is not None:
                y = jax.nn.sigmoid(y + refs[2][:, cols])
            o_ref[:, cols] = y.astype(BF16)

    in_specs = [pl.BlockSpec((tm, D_MODEL), lambda i: (i, 0)), _resident((D_MODEL, n_out))]
    args = (h, w)
    if b is not None:
        in_specs.append(pl.BlockSpec((1, n_out), lambda i: (0, 0)))
        args = (h, w, b)
    return _call(
        body, name=name, grid=(t // tm,), args=args, in_specs=in_specs,
        out_specs=[pl.BlockSpec((tm, n_out), lambda i: (i, 0))],
        out_shape=[jax.ShapeDtypeStruct((t, n_out), BF16)],
        semantics=("parallel",), rider=rider)


def _qknorm_fwd(proj, gq, gk):
    t = proj.shape[0]
    tm = min(t, 512)
    scale = HEAD_DIM ** -0.5 * LOG2E

    def body(q_ref, k_ref, gq_ref, gk_ref, qn_ref, kn_ref):
        e = _head_sum_matrix()
        for src, g_ref, dst, sc in ((q_ref, gq_ref, qn_ref, scale), (k_ref, gk_ref, kn_ref, 1.0)):
            for s in range(D_MODEL // LANES):
                sl = slice(s * LANES, (s + 1) * LANES)
                xf = src[:, sl].astype(F32)
                r = lax.rsqrt(_head_sums(xf * xf, e) * (1.0 / HEAD_DIM) + EPS)
                dst[:, sl] = (xf * r * g_ref[:, sl] * sc).astype(BF16)

    return pl.pallas_call(
        body, name="qknorm_fwd", grid=(t // tm,),
        in_specs=[pl.BlockSpec((tm, D_MODEL), lambda i: (i, 0)),
                  pl.BlockSpec((tm, D_MODEL), lambda i: (i, 1)),
                  pl.BlockSpec((1, D_MODEL), lambda i: (0, 0)),
                  pl.BlockSpec((1, D_MODEL), lambda i: (0, 0))],
        out_specs=[pl.BlockSpec((tm, D_MODEL), lambda i: (i, 0))] * 2,
        out_shape=[jax.ShapeDtypeStruct((t, D_MODEL), BF16)] * 2,
        compiler_params=_params(("parallel",)),
    )(proj, proj, gq, gk)


def _bias_tiles(rel_bias, rider=None):
    by_dist = jnp.concatenate(
        [rel_bias[:, :2 * MAX_REL], jnp.broadcast_to(rel_bias[:, 2 * MAX_REL:], (N_HEADS, 2 * MAX_REL))], axis=1)
    by_dist = by_dist.reshape(N_HEADS, 1, SKEW)

    def body(f_ref, o_ref):
        jj = lax.broadcasted_iota(jnp.int32, (QB, QB), 0)
        ii = lax.broadcasted_iota(jnp.int32, (QB, QB), 1)
        for w in range(KW // QB):
            pos = jnp.broadcast_to(f_ref[0, :, KW - QB * w:KW - QB * w + QB], (QB, QB))
            neg = jnp.broadcast_to(f_ref[0, :, KW - QB * (w + 1):KW - QB * w], (QB, QB))
            pos = pltpu.roll(pos, 0, 1, stride=1, stride_axis=0)
            neg = pltpu.roll(neg, 0, 1, stride=1, stride_axis=0)
            tile = jnp.where(ii >= jj, pos, neg)
            kc = (jj + QB * w) // CHUNK
            qc = ii // CHUNK
            band = (kc >= qc) & (kc <= qc + N_PREV_CHUNKS)
            o_ref[0, QB * w:QB * (w + 1), :] = jnp.where(band, tile * LOG2E, NEG_INF)

    return _call(
        body, name="bias_tiles", grid=(N_HEADS,), args=(by_dist,),
        in_specs=[pl.BlockSpec((1, 1, SKEW), lambda h: (h, 0, 0))],
        out_specs=[pl.BlockSpec((1, KW, QB), lambda h: (h, 0, 0))],
        out_shape=[jax.ShapeDtypeStruct((N_HEADS, KW, QB), F32)],
        semantics=("parallel",), rider=rider)


def _window_specs(col0, slab):
    return [pl.BlockSpec((QB, slab), functools.partial(
        lambda p, b, back: (jnp.maximum(b - back, 0), col0 + p), back=back)) for back in (2, 1, 0)]


def _attn_fwd(qn, kn, proj, bias, rider=None):
    t = qn.shape[0]
    nb = t // QB
    pairs = PAIRS_FWD
    slab = pairs * LANES
    v_col0 = 2 * D_MODEL // slab

    def body(q_ref, k0, k1, k2, v0, v1, v2, bias_ref, o_ref, lse_ref):
        b = pl.program_id(1)

        @pl.when(b < 2)
        def _():
            compute(q_ref, k0, k1, k2, v0, v1, v2, bias_ref, o_ref, lse_ref,
                    lax.broadcasted_iota(jnp.int32, (KW, 1), 0) >= (2 - b) * QB)

        @pl.when(b >= 2)
        def _():
            compute(q_ref, k0, k1, k2, v0, v1, v2, bias_ref, o_ref, lse_ref, None)

    def compute(q_ref, k0, k1, k2, v0, v1, v2, bias_ref, o_ref, lse_ref, valid):
        head_a = lax.broadcasted_iota(jnp.int32, (1, LANES), 1) < HEAD_DIM

        def scores(head):
            hp, hh = divmod(head, 2)
            sl = slice(hp * LANES, (hp + 1) * LANES)
            k = jnp.concatenate([k0[:, sl], k1[:, sl], k2[:, sl]], axis=0)
            mine = head_a if hh == 0 else jnp.logical_not(head_a)
            s = _dot_nt(jnp.where(mine, k, jnp.zeros_like(k)), q_ref[:, sl]) + bias_ref[head]
            return s if valid is None else jnp.where(valid, s, NEG_INF)

        def weighted_values(head, s):
            hp, hh = divmod(head, 2)
            sl = slice(hp * LANES, (hp + 1) * LANES)
            v = jnp.concatenate([v0[:, sl], v1[:, sl], v2[:, sl]], axis=0)
            vt = v.astype(F32).T.astype(BF16)[hh * HEAD_DIM:(hh + 1) * HEAD_DIM]
            vt = jnp.concatenate([vt, jnp.ones((SUBLANES, KW), BF16)], axis=0)
            m = jnp.max(s, axis=0, keepdims=True)
            ov = _dot(vt, jnp.exp2(s - m).astype(BF16))
            l = ov[HEAD_DIM:HEAD_DIM + 1]
            return ov[:HEAD_DIM] / l, m + jnp.log2(l)

        outs, lses = [], []
        pending = scores(0)
        for head in range(2 * pairs):
            nxt = scores(head + 1) if head + 1 < 2 * pairs else None
            o, lse = weighted_values(head, pending)
            outs.append(o)
            lses.append(lse)
            pending = nxt
        for hp in range(pairs):
            sl = slice(hp * LANES, (hp + 1) * LANES)
            o_ref[:, sl] = jnp.concatenate([outs[2 * hp], outs[2 * hp + 1]], axis=0).T.astype(BF16)
        lse_ref[...] = jnp.concatenate(lses, axis=0)

    return _call(
        body, name="attn_fwd", grid=(D_MODEL // slab, nb), args=(qn, kn, kn, kn, proj, proj, proj, bias),
        in_specs=[pl.BlockSpec((QB, slab), lambda p, b: (b, p))] + _window_specs(0, slab)
        + _window_specs(v_col0, slab) + [pl.BlockSpec((2 * pairs, KW, QB), lambda p, b: (p, 0, 0))],
        out_specs=[pl.BlockSpec((QB, slab), lambda p, b: (b, p)),
                   pl.BlockSpec((2 * pairs, QB), lambda p, b: (p, b))],
        out_shape=[jax.ShapeDtypeStruct((t, D_MODEL), BF16), jax.ShapeDtypeStruct((N_HEADS, t), F32)],
        semantics=("parallel", "arbitrary"), rider=rider)


def _shift_down(u, halo, n):
    rows = lax.broadcasted_iota(jnp.int32, (u.shape[0], 1), 0)
    out = pltpu.roll(u, n, 0)
    for j in range(n):
        out = jnp.where(rows == j, halo[SUBLANES - n + j:SUBLANES - n + j + 1, :], out)
    return out


def _shift_up(u, halo, n):
    tm = u.shape[0]
    rows = lax.broadcasted_iota(jnp.int32, (tm, 1), 0)
    out = pltpu.roll(u, tm - n, 0)
    for j in range(n):
        out = jnp.where(rows == tm - n + j, halo[j:j + 1, :], out)
    return out


def _conv_fwd(proj, conv_w, conv_b):
    t = proj.shape[0]
    tm = min(t, 512)
    hb = tm // SUBLANES

    def body(bg_ref, cg_ref, xc_ref, cgh_ref, xch_ref, w_ref, b_ref, o_ref):
        i = pl.program_id(0)
        u = cg_ref[...].astype(F32) * xc_ref[...].astype(F32)
        halo = cgh_ref[...].astype(F32) * xch_ref[...].astype(F32)
        halo = jnp.where(i > 0, halo, 0.0)
        w = w_ref[...]
        s = w[0:1] * _shift_down(u, halo, 2) + w[1:2] * _shift_down(u, halo, 1) + w[2:3] * u
        o_ref[...] = (bg_ref[...].astype(F32) * (b_ref[...] + s)).astype(BF16)

    def prev(col):
        return pl.BlockSpec((SUBLANES, D_MODEL), lambda i: (jnp.maximum(i * hb - 1, 0), col))

    return pl.pallas_call(
        body, name="conv_fwd", grid=(t // tm,),
        in_specs=[pl.BlockSpec((tm, D_MODEL), lambda i: (i, 3)),
                  pl.BlockSpec((tm, D_MODEL), lambda i: (i, 4)),
                  pl.BlockSpec((tm, D_MODEL), lambda i: (i, 5)),
                  prev(4), prev(5),
                  pl.BlockSpec((3, D_MODEL), lambda i: (0, 0)),
                  pl.BlockSpec((1, D_MODEL), lambda i: (0, 0))],
        out_specs=pl.BlockSpec((tm, D_MODEL), lambda i: (i, 0)),
        out_shape=jax.ShapeDtypeStruct((t, D_MODEL), BF16),
        compiler_params=_params(("parallel",)),
    )(proj, proj, proj, proj, proj, conv_w, conv_b)


def _mix_out(y_attn, y_conv, gates, x, w_ap, w_cp, w_out, g2):
    t = x.shape[0]
    tm = min(t, 512)

    def body(ya_in, yc_in, g_ref, x_ref, wap, wcp, wout, g2_ref, ya_ref, yc_ref, mg_ref, x1_ref, h2_ref):
        ya = _dot(ya_in[...], wap[...])
        yc = _dot(yc_in[...], wcp[...])
        ya_ref[...] = ya.astype(BF16)
        yc_ref[...] = yc.astype(BF16)
        merged = (g_ref[:, :D_MODEL].astype(F32) * ya + g_ref[:, D_MODEL:].astype(F32) * yc).astype(BF16)
        mg_ref[...] = merged
        x1 = x_ref[...] + _dot(merged, wout[...])
        x1_ref[...] = x1
        r = lax.rsqrt(jnp.mean(x1 * x1, axis=-1, keepdims=True) + EPS)
        h2_ref[...] = (x1 * r * g2_ref[...]).astype(BF16)

    row = pl.BlockSpec((tm, D_MODEL), lambda i: (i, 0))
    full = _resident((D_MODEL, D_MODEL))
    return pl.pallas_call(
        body, name="mix_out", grid=(t // tm,),
        in_specs=[row, row, pl.BlockSpec((tm, 2 * D_MODEL), lambda i: (i, 0)), row, full, full, full,
                  pl.BlockSpec((1, D_MODEL), lambda i: (0, 0))],
        out_specs=[row] * 5,
        out_shape=[jax.ShapeDtypeStruct((t, D_MODEL), BF16)] * 3
        + [jax.ShapeDtypeStruct((t, D_MODEL), F32), jax.ShapeDtypeStruct((t, D_MODEL), BF16)],
        compiler_params=_params(("parallel",)),
    )(y_attn, y_conv, gates, x, w_ap, w_cp, w_out, g2)


def _mlp_fwd(h2, w_up, w_down, x1, target):
    t = h2.shape[0]
    tm = min(t, 512)
    tf = 1024
    nf = D_FF // tf

    def body(h2_ref, wup, wdn, x1_ref, tg_ref, a_ref, dy_ref, dyb_ref, loss_ref):
        h2v = h2_ref[...]
        acc = None
        pending = _dot(h2v, wup[:, 0:tf])
        for j in range(nf):
            cols = slice(j * tf, (j + 1) * tf)
            a = pending
            if j + 1 < nf:
                pending = _dot(h2v, wup[:, (j + 1) * tf:(j + 2) * tf])
            a_ref[:, cols] = a.astype(BF16)
            part = _dot(jnp.square(jnp.maximum(a, 0.0)).astype(BF16), wdn[cols, :])
            acc = part if acc is None else acc + part

        @pl.when(pl.program_id(0) == 0)
        def _():
            loss_ref[...] = jnp.zeros_like(loss_ref)

        diff = x1_ref[...] + acc - tg_ref[...]
        loss_ref[...] += _fold8(diff * diff)
        dy = diff * (1.0 / D_MODEL)
        dy_ref[...] = dy
        dyb_ref[...] = dy.astype(BF16)

    row = pl.BlockSpec((tm, D_MODEL), lambda i: (i, 0))
    return pl.pallas_call(
        body, name="mlp_fwd", grid=(t // tm,),
        in_specs=[row, _resident((D_MODEL, D_FF)), _resident((D_FF, D_MODEL)), row, row],
        out_specs=[pl.BlockSpec((tm, D_FF), lambda i: (i, 0)), row, row,
                   pl.BlockSpec((SUBLANES, D_MODEL), lambda i: (0, 0))],
        out_shape=[jax.ShapeDtypeStruct((t, D_FF), BF16), jax.ShapeDtypeStruct((t, D_MODEL), F32),
                   jax.ShapeDtypeStruct((t, D_MODEL), BF16), jax.ShapeDtypeStruct((SUBLANES, D_MODEL), F32)],
        compiler_params=_params(("arbitrary",)),
    )(h2, w_up, w_down, x1, target)


def _rmsnorm_bwd(xf, g, dh):
    r = lax.rsqrt(jnp.mean(xf * xf, axis=-1, keepdims=True) + EPS)
    xh = xf * r
    dxh = dh * g
    dx = r * (dxh - xh * jnp.mean(dxh * xh, axis=-1, keepdims=True))
    return dx, dh * xh


def _mlp_bwd(dyb, a, w_down, w_up, x1, dy, g2):
    t = dyb.shape[0]
    tm = min(t, 512)
    tf = 1024
    nf = D_FF // tf

    def body(dyb_ref, a_ref, wdn, wup, x1_ref, dy_ref, g2_ref, da_ref, dx1_ref, dx1b_ref, dg2_ref):
        dyv = dyb_ref[...]
        acc = None
        pending = _dot_nt(dyv, wdn[0:tf, :])
        for j in range(nf):
            cols = slice(j * tf, (j + 1) * tf)
            du = pending
            if j + 1 < nf:
                pending = _dot_nt(dyv, wdn[(j + 1) * tf:(j + 2) * tf, :])
            da = (du * (2.0 * jnp.maximum(a_ref[:, cols].astype(F32), 0.0))).astype(BF16)
            da_ref[:, cols] = da
            part = _dot_nt(da, wup[:, cols])
            acc = part if acc is None else acc + part

        @pl.when(pl.program_id(0) == 0)
        def _():
            dg2_ref[...] = jnp.zeros_like(dg2_ref)

        dx, dg = _rmsnorm_bwd(x1_ref[...], g2_ref[...], acc)
        dx1 = dy_ref[...] + dx
        dx1_ref[...] = dx1
        dx1b_ref[...] = dx1.astype(BF16)
        dg2_ref[...] += _fold8(dg)

    row = pl.BlockSpec((tm, D_MODEL), lambda i: (i, 0))
    wide = pl.BlockSpec((tm, D_FF), lambda i: (i, 0))
    return pl.pallas_call(
        body, name="mlp_bwd", grid=(t // tm,),
        in_specs=[row, wide, _resident((D_FF, D_MODEL)), _resident((D_MODEL, D_FF)), row, row,
                  pl.BlockSpec((1, D_MODEL), lambda i: (0, 0))],
        out_specs=[wide, row, row, pl.BlockSpec((SUBLANES, D_MODEL), lambda i: (0, 0))],
        out_shape=[jax.ShapeDtypeStruct((t, D_FF), BF16), jax.ShapeDtypeStruct((t, D_MODEL), F32),
                   jax.ShapeDtypeStruct((t, D_MODEL), BF16), jax.ShapeDtypeStruct((SUBLANES, D_MODEL), F32)],
        compiler_params=_params(("arbitrary",)),
    )(dyb, a, w_down, w_up, x1, dy, g2)


def _wgrad(name, lhs, rhs_list, rhs_slabs, relu_sq=False, token_block=2048):
    t, m = lhs.shape
    tt = min(t, token_block)
    tmo = min(m, 1024)
    n_slab = sum(rhs_slabs)
    starts = [sum(rhs_slabs[:n]) for n in range(len(rhs_slabs))]
    n_rhs = len(rhs_list)

    def body(*refs):
        l_ref, r_refs, o_ref, acc = refs[0], refs[1:1 + n_rhs], refs[1 + n_rhs], refs[2 + n_rhs]
        k, s = pl.program_id(1), pl.program_id(2)
        lv = l_ref[...]
        if relu_sq:
            lv = jnp.square(jnp.maximum(lv.astype(F32), 0.0)).astype(BF16)

        @pl.when(s == 0)
        def _():
            acc[...] = jnp.zeros_like(acc)

        for n in range(n_rhs):
            @pl.when((k >= starts[n]) & (k < starts[n] + rhs_slabs[n]))
            def _(n=n):
                acc[...] += _dot_tn(lv, r_refs[n][...])

        @pl.when(s == pl.num_programs(2) - 1)
        def _():
            o_ref[...] = acc[...].astype(BF16)

    def rhs_spec(n):
        lo, cnt = starts[n], rhs_slabs[n]

        def index(i, k, s):
            inside = (k >= lo) & (k < lo + cnt)
            return (jnp.where(inside, s, 0), jnp.clip(k - lo, 0, cnt - 1))
        return pl.BlockSpec((tt, D_MODEL), index)

    return pl.pallas_call(
        body, name=name, grid=(m // tmo, n_slab, t // tt),
        in_specs=[pl.BlockSpec((tt, tmo), lambda i, k, s: (s, i))] + [rhs_spec(n) for n in range(n_rhs)],
        out_specs=pl.BlockSpec((tmo, D_MODEL), lambda i, k, s: (i, k)),
        out_shape=jax.ShapeDtypeStruct((m, n_slab * D_MODEL), BF16),
        scratch_shapes=[pltpu.VMEM((tmo, D_MODEL), F32)],
        compiler_params=_params(("parallel", "parallel", "arbitrary")),
    )(lhs, *rhs_list)


def _wgrad_group(name, triples):
    t = triples[0][0].shape[0]
    tt = min(t, 1024)
    counts = [n for _, _, n in triples]
    starts = [sum(counts[:n]) for n in range(len(counts))]
    n_prod = len(triples)

    def inside(n, k):
        return (k >= starts[n]) & (k < starts[n] + counts[n])

    def body(*refs):
        l_refs, r_refs, o_refs = refs[:n_prod], refs[n_prod:2 * n_prod], refs[2 * n_prod:3 * n_prod]
        acc = refs[3 * n_prod]
        k, s = pl.program_id(0), pl.program_id(1)

        @pl.when(s == 0)
        def _():
            acc[...] = jnp.zeros_like(acc)

        for n in range(n_prod):
            @pl.when(inside(n, k))
            def _(n=n):
                acc[...] += _dot_tn(l_refs[n][...], r_refs[n][...])

            @pl.when(inside(n, k) & (s == pl.num_programs(1) - 1))
            def _(n=n):
                o_refs[n][...] = acc[...].astype(BF16)

    def lhs_spec(n):
        return pl.BlockSpec((tt, D_MODEL), lambda k, s: (jnp.where(inside(n, k), s, 0), 0))

    def rhs_spec(n):
        return pl.BlockSpec((tt, D_MODEL), lambda k, s: (jnp.where(inside(n, k), s, 0),
                                                         jnp.clip(k - starts[n], 0, counts[n] - 1)))

    def out_spec(n):
        return pl.BlockSpec((D_MODEL, D_MODEL), lambda k, s: (0, jnp.clip(k - starts[n], 0, counts[n] - 1)))

    return pl.pallas_call(
        body, name=name, grid=(sum(counts), t // tt),
        in_specs=[lhs_spec(n) for n in range(n_prod)] + [rhs_spec(n) for n in range(n_prod)],
        out_specs=[out_spec(n) for n in range(n_prod)],
        out_shape=[jax.ShapeDtypeStruct((D_MODEL, n * D_MODEL), BF16) for n in counts],
        scratch_shapes=[pltpu.VMEM((D_MODEL, D_MODEL), F32)],
        compiler_params=_params(("arbitrary", "arbitrary")),
    )(*[tr[0] for tr in triples], *[tr[1] for tr in triples])


def _mix_bwd(dx1b, gates, ya, yc, w_out, w_ap, w_cp, w_g, rider=None):
    t = dx1b.shape[0]
    tm = min(t, 512)

    def body(dx_ref, g_ref, ya_ref, yc_ref, wout, wap, wcp, wg,
             dgp_ref, dya_ref, dyc_ref, dyat_ref, dycv_ref, dhg_ref, dbg_ref):
        dm = _dot_nt(dx_ref[...], wout[...])
        ga = g_ref[:, :D_MODEL].astype(F32)
        gc = g_ref[:, D_MODEL:].astype(F32)
        dya = (dm * ga).astype(BF16)
        dyc = (dm * gc).astype(BF16)
        dya_ref[...] = dya
        dyc_ref[...] = dyc
        dgpa = dm * ya_ref[...].astype(F32) * ga * (1.0 - ga)
        dgpc = dm * yc_ref[...].astype(F32) * gc * (1.0 - gc)

        @pl.when(pl.program_id(0) == 0)
        def _():
            dbg_ref[...] = jnp.zeros_like(dbg_ref)

        dbg_ref[:, :D_MODEL] += _fold8(dgpa)
        dbg_ref[:, D_MODEL:] += _fold8(dgpc)
        dgpa = dgpa.astype(BF16)
        dgpc = dgpc.astype(BF16)
        dgp_ref[:, :D_MODEL] = dgpa
        dgp_ref[:, D_MODEL:] = dgpc
        dyat_ref[...] = _dot_nt(dya, wap[...]).astype(BF16)
        dycv_ref[...] = _dot_nt(dyc, wcp[...]).astype(BF16)
        dhg_ref[...] = _dot_nt(dgpa, wg[:, :D_MODEL]) + _dot_nt(dgpc, wg[:, D_MODEL:])

    row = pl.BlockSpec((tm, D_MODEL), lambda i: (i, 0))
    row2 = pl.BlockSpec((tm, 2 * D_MODEL), lambda i: (i, 0))
    full = _resident((D_MODEL, D_MODEL))
    return _call(
        body, name="mix_bwd", grid=(t // tm,), args=(dx1b, gates, ya, yc, w_out, w_ap, w_cp, w_g),
        in_specs=[row, row2, row, row, full, full, full, _resident((D_MODEL, 2 * D_MODEL))],
        out_specs=[row2, row, row, row, row, row, pl.BlockSpec((SUBLANES, 2 * D_MODEL), lambda i: (0, 0))],
        out_shape=[jax.ShapeDtypeStruct((t, 2 * D_MODEL), BF16)] + [jax.ShapeDtypeStruct((t, D_MODEL), BF16)] * 4
        + [jax.ShapeDtypeStruct((t, D_MODEL), F32), jax.ShapeDtypeStruct((SUBLANES, 2 * D_MODEL), F32)],
        semantics=("arbitrary",), rider=rider)


def _conv_bwd(dyconv, proj, conv_w, conv_b, rider=None):
    t = proj.shape[0]
    tm = min(t, 512)
    hb = tm // SUBLANES
    last = t // SUBLANES - 1

    def body(dy_ref, dyn_ref, bg_ref, bgn_ref, cg_ref, cgp_ref, xc_ref, xcp_ref, w_ref, b_ref,
             o_ref, dcb_ref, dcw_ref):
        i = pl.program_id(0)
        cg = cg_ref[...].astype(F32)
        xc = xc_ref[...].astype(F32)
        bg = bg_ref[...].astype(F32)
        u = cg * xc
        prev = jnp.where(i > 0, cgp_ref[...].astype(F32) * xcp_ref[...].astype(F32), 0.0)
        u1 = _shift_down(u, prev, 1)
        u2 = _shift_down(u, prev, 2)
        w = w_ref[...]
        conv = b_ref[...] + (w[0:1] * u2 + w[1:2] * u1 + w[2:3] * u)
        dy = dy_ref[...].astype(F32)
        dconv = dy * bg
        nxt = jnp.where(i < pl.num_programs(0) - 1, dyn_ref[...].astype(F32) * bgn_ref[...].astype(F32), 0.0)
        du = w[2:3] * dconv + w[1:2] * _shift_up(dconv, nxt, 1) + w[0:1] * _shift_up(dconv, nxt, 2)
        o_ref[:, :D_MODEL] = (dy * conv).astype(BF16)
        o_ref[:, D_MODEL:2 * D_MODEL] = (du * xc).astype(BF16)
        o_ref[:, 2 * D_MODEL:] = (du * cg).astype(BF16)

        @pl.when(i == 0)
        def _():
            dcb_ref[...] = jnp.zeros_like(dcb_ref)
            dcw_ref[...] = jnp.zeros_like(dcw_ref)

        dcb_ref[...] += _fold8(dconv)
        dcw_ref[0:SUBLANES] += _fold8(dconv * u2)
        dcw_ref[SUBLANES:2 * SUBLANES] += _fold8(dconv * u1)
        dcw_ref[2 * SUBLANES:] += _fold8(dconv * u)

    def prev(col):
        return pl.BlockSpec((SUBLANES, D_MODEL), lambda i: (jnp.maximum(i * hb - 1, 0), col))

    def nxt(col):
        return pl.BlockSpec((SUBLANES, D_MODEL), lambda i: (jnp.minimum((i + 1) * hb, last), col))

    def cur(col):
        return pl.BlockSpec((tm, D_MODEL), lambda i: (i, col))

    return _call(
        body, name="conv_bwd", grid=(t // tm,),
        args=(dyconv, dyconv, proj, proj, proj, proj, proj, proj, conv_w, conv_b),
        in_specs=[cur(0), nxt(0), cur(3), nxt(3), cur(4), prev(4), cur(5), prev(5),
                  pl.BlockSpec((3, D_MODEL), lambda i: (0, 0)), pl.BlockSpec((1, D_MODEL), lambda i: (0, 0))],
        out_specs=[pl.BlockSpec((tm, 3 * D_MODEL), lambda i: (i, 0)),
                   pl.BlockSpec((SUBLANES, D_MODEL), lambda i: (0, 0)),
                   pl.BlockSpec((3 * SUBLANES, D_MODEL), lambda i: (0, 0))],
        out_shape=[jax.ShapeDtypeStruct((t, 3 * D_MODEL), BF16), jax.ShapeDtypeStruct((SUBLANES, D_MODEL), F32),
                   jax.ShapeDtypeStruct((3 * SUBLANES, D_MODEL), F32)],
        semantics=("arbitrary",), rider=rider)


def _attn_bwd(qn, kn, proj, dyattn, y_attn, lse, bias, rider=None):
    t = qn.shape[0]
    nb = t // QB
    v_col0 = 2 * D_MODEL // SLAB

    def body(q_ref, k0, k1, k2, v0, v1, v2, do_ref, o_ref, lse_ref, bias_ref,
             dq_ref, dk_ref, dv_ref, db_ref, acck, accv):
        b = pl.program_id(1)

        @pl.when(b == 0)
        def _():
            acck[...] = jnp.zeros_like(acck)
            accv[...] = jnp.zeros_like(accv)
            db_ref[...] = jnp.zeros_like(db_ref)

        def block(valid):
            head_a = lax.broadcasted_iota(jnp.int32, (1, LANES), 1) < HEAD_DIM

            def window(refs, hp):
                sl = slice(hp * LANES, (hp + 1) * LANES)
                return jnp.concatenate([r[:, sl] for r in refs], axis=0)

            def transposed(x, hh):
                return x.astype(F32).T.astype(BF16)[hh * HEAD_DIM:(hh + 1) * HEAD_DIM]

            def probs(head):
                hp, hh = divmod(head, 2)
                sl = slice(hp * LANES, (hp + 1) * LANES)
                mine = head_a if hh == 0 else jnp.logical_not(head_a)
                k = window((k0, k1, k2), hp)
                s = _dot_nt(jnp.where(mine, k, jnp.zeros_like(k)), q_ref[:, sl]) + bias_ref[head]
                s = s if valid is None else jnp.where(valid, s, NEG_INF)
                return jnp.exp2(s - lse_ref[head:head + 1, :])

            def grads(head, p):
                hp, hh = divmod(head, 2)
                sl = slice(hp * LANES, (hp + 1) * LANES)
                rows = slice(hh * HEAD_DIM, (hh + 1) * HEAD_DIM)
                mine = head_a if hh == 0 else jnp.logical_not(head_a)
                do = do_ref[:, sl]
                v = window((v0, v1, v2), hp)
                delta = jnp.sum((do.astype(F32).T * o_ref[:, sl].astype(F32).T)[rows], axis=0, keepdims=True)
                ds = p * (_dot_nt(jnp.where(mine, v, jnp.zeros_like(v)), do) - delta)
                db_ref[head] += ds
                pb, dsb = p.astype(BF16), ds.astype(BF16)
                dvt = _dot_nt(transposed(do, hh), pb)
                dkt = _dot_nt(transposed(q_ref[:, sl], hh), dsb) * (1.0 / LOG2E)
                dqt = _dot(transposed(window((k0, k1, k2), hp), hh), dsb)
                return dqt, dkt, dvt

            out = []
            pending = probs(0)
            for head in range(2 * PAIRS):
                nxt = probs(head + 1) if head + 1 < 2 * PAIRS else None
                out.append(grads(head, pending))
                pending = nxt
            for hp in range(PAIRS):
                sl = slice(hp * LANES, (hp + 1) * LANES)
                dqt, dkt, dvt = (jnp.concatenate([out[2 * hp][n], out[2 * hp + 1][n]], axis=0) for n in range(3))
                dq_ref[:, sl] = dqt.T.astype(BF16)
                for w in range(3):
                    slot = lax.rem(b + w + 1, 3)
                    cols = slice(w * QB, (w + 1) * QB)
                    if w == 2:
                        acck[hp, slot] = dkt[:, cols]
                        accv[hp, slot] = dvt[:, cols]
                    else:
                        acck[hp, slot] += dkt[:, cols]
                        accv[hp, slot] += dvt[:, cols]

        @pl.when(b < 2)
        def _():
            block(lax.broadcasted_iota(jnp.int32, (KW, 1), 0) >= (2 - b) * QB)

        @pl.when((b >= 2) & (b < nb))
        def _():
            block(None)

        done = lax.rem(b + 1, 3)
        for hp in range(PAIRS):
            sl = slice(hp * LANES, (hp + 1) * LANES)
            dk_ref[:, sl] = acck[hp, done].T.astype(BF16)
            dv_ref[:, sl] = accv[hp, done].T.astype(BF16)

    def cur(p, b):
        return (jnp.minimum(b, nb - 1), p)

    def window(col0):
        return [pl.BlockSpec((QB, SLAB), functools.partial(
            lambda p, b, back: (jnp.maximum(jnp.minimum(b, nb - 1) - back, 0), col0 + p), back=back))
            for back in (2, 1, 0)]

    def done_block(p, b):
        return (jnp.maximum(b - 2, 0), p)

    tile = pl.BlockSpec((2 * PAIRS, KW, QB), lambda p, b: (p, 0, 0))
    here = pl.BlockSpec((QB, SLAB), cur)
    return _call(
        body, name="attn_bwd", grid=(D_MODEL // SLAB, nb + 2),
        args=(qn, kn, kn, kn, proj, proj, proj, dyattn, y_attn, lse, bias),
        in_specs=[here] + window(0) + window(v_col0)
        + [here, here, pl.BlockSpec((2 * PAIRS, QB), lambda p, b: (p, jnp.minimum(b, nb - 1))), tile],
        out_specs=[here, pl.BlockSpec((QB, SLAB), done_block), pl.BlockSpec((QB, SLAB), done_block), tile],
        out_shape=[jax.ShapeDtypeStruct((t, D_MODEL), BF16)] * 3 + [jax.ShapeDtypeStruct((N_HEADS, KW, QB), F32)],
        scratch_shapes=[pltpu.VMEM((PAIRS, 3, LANES, QB), F32), pltpu.VMEM((PAIRS, 3, LANES, QB), F32)],
        semantics=("parallel", "arbitrary"), rider=rider)


def _qknorm_bwd(proj, dqn, dkn, gq, gk):
    t = proj.shape[0]
    tm = min(t, 512)
    scale = HEAD_DIM ** -0.5

    def body(q_ref, k_ref, dqn_ref, dkn_ref, gq_ref, gk_ref, o_ref, dgq_ref, dgk_ref):
        e = _head_sum_matrix()

        @pl.when(pl.program_id(0) == 0)
        def _():
            dgq_ref[...] = jnp.zeros_like(dgq_ref)
            dgk_ref[...] = jnp.zeros_like(dgk_ref)

        for n, (src, dn_ref, g_ref, dg_ref, sc) in enumerate(
                ((q_ref, dqn_ref, gq_ref, dgq_ref, scale), (k_ref, dkn_ref, gk_ref, dgk_ref, 1.0))):
            for s in range(D_MODEL // LANES):
                sl = slice(s * LANES, (s + 1) * LANES)
                xf = src[:, sl].astype(F32)
                r = lax.rsqrt(_head_sums(xf * xf, e) * (1.0 / HEAD_DIM) + EPS)
                xh = xf * r
                dn = dn_ref[:, sl].astype(F32) * sc
                dg_ref[:, sl] += _fold8(dn * xh)
                dxh = dn * g_ref[:, sl]
                mean = _head_sums(dxh * xh, e) * (1.0 / HEAD_DIM)
                o_ref[:, n * D_MODEL + s * LANES:n * D_MODEL + (s + 1) * LANES] = (r * (dxh - xh * mean)).astype(BF16)

    row = pl.BlockSpec((tm, D_MODEL), lambda i: (i, 0))
    vec = pl.BlockSpec((1, D_MODEL), lambda i: (0, 0))
    acc = pl.BlockSpec((SUBLANES, D_MODEL), lambda i: (0, 0))
    return pl.pallas_call(
        body, name="qknorm_bwd", grid=(t // tm,),
        in_specs=[row, pl.BlockSpec((tm, D_MODEL), lambda i: (i, 1)), row, row, vec, vec],
        out_specs=[pl.BlockSpec((tm, 2 * D_MODEL), lambda i: (i, 0)), acc, acc],
        out_shape=[jax.ShapeDtypeStruct((t, 2 * D_MODEL), BF16)] + [jax.ShapeDtypeStruct((SUBLANES, D_MODEL), F32)] * 2,
        compiler_params=_params(("arbitrary",)),
    )(proj, proj, dqn, dkn, gq, gk)


def _in_bwd(dqk, dv, dconv, w_in, dhg, x, g1, dx1, rider=None):
    t = x.shape[0]
    tm = min(t, 512)

    def body(dqk_ref, dv_ref, dc_ref, w_ref, dhg_ref, x_ref, g_ref, dx1_ref, dx_ref, dg_ref):
        acc = dhg_ref[...]
        slab = 0
        for src, n in ((dqk_ref, 2), (dv_ref, 1), (dc_ref, 3)):
            for s in range(n):
                acc = acc + _dot_nt(src[:, s * D_MODEL:(s + 1) * D_MODEL],
                                    w_ref[:, slab * D_MODEL:(slab + 1) * D_MODEL])
                slab += 1

        @pl.when(pl.program_id(0) == 0)
        def _():
            dg_ref[...] = jnp.zeros_like(dg_ref)

        dx, dg = _rmsnorm_bwd(x_ref[...], g_ref[...], acc)
        dx_ref[...] = dx1_ref[...] + dx
        dg_ref[...] += _fold8(dg)

    row = pl.BlockSpec((tm, D_MODEL), lambda i: (i, 0))
    return _call(
        body, name="in_bwd", grid=(t // tm,), args=(dqk, dv, dconv, w_in, dhg, x, g1, dx1),
        in_specs=[pl.BlockSpec((tm, 2 * D_MODEL), lambda i: (i, 0)), row,
                  pl.BlockSpec((tm, 3 * D_MODEL), lambda i: (i, 0)),
                  _resident(w_in.shape), row, row, pl.BlockSpec((1, D_MODEL), lambda i: (0, 0)), row],
        out_specs=[row, pl.BlockSpec((SUBLANES, D_MODEL), lambda i: (0, 0))],
        out_shape=[jax.ShapeDtypeStruct((t, D_MODEL), F32), jax.ShapeDtypeStruct((SUBLANES, D_MODEL), F32)],
        semantics=("arbitrary",), rider=rider)


def _bias_grad_fold(dbias, rider=None):
    def body(d_ref, o_ref):
        jj = lax.broadcasted_iota(jnp.int32, (QB, QB), 0)
        ii = lax.broadcasted_iota(jnp.int32, (QB, QB), 1)
        flip = (jj + ii == QB - 1).astype(BF16)
        low = jj + ii < QB
        pos, neg = [], []
        for w in range(KW // QB):
            x = d_ref[0, QB * w:QB * (w + 1), :]
            hi = x.astype(BF16)
            r1 = x - hi.astype(F32)
            mid = r1.astype(BF16)
            lo = (r1 - mid.astype(F32)).astype(BF16)
            xr = _dot(hi, flip) + _dot(mid, flip) + _dot(lo, flip)
            for keep, acc in ((low, pos), (jnp.logical_not(low), neg)):
                part = pltpu.roll(jnp.where(keep, xr, 0.0), 0, 1, stride=1, stride_axis=0)
                acc.append(jnp.sum(part, axis=0, keepdims=True))
        far = pos[1] + neg[0] + pos[0]
        o_ref[0] = jnp.zeros((SUBLANES, QB), F32)
        o_ref[0, 0:1, :] = neg[2]
        o_ref[0, 1:2, :] = pos[2] + neg[1]
        o_ref[0, 2:3, :] = jnp.broadcast_to(jnp.sum(far, axis=-1, keepdims=True), (1, QB))

    return _call(
        body, name="bias_grad_fold", grid=(N_HEADS,), args=(dbias,),
        in_specs=[pl.BlockSpec((1, KW, QB), lambda h: (h, 0, 0))],
        out_specs=[pl.BlockSpec((1, SUBLANES, QB), lambda h: (h, 0, 0))],
        out_shape=[jax.ShapeDtypeStruct((N_HEADS, SUBLANES, QB), F32)],
        semantics=("parallel",), rider=rider)


def _small_partials(dg1, dgq, dgk, dcb, dcw, dbg, dg2, dbias_fold, loss_tile):
    def head_fold(v):
        acc = v[:, 0:LANES]
        for s in range(1, D_MODEL // LANES):
            acc = acc + v[:, s * LANES:(s + 1) * LANES]
        return acc + pltpu.roll(acc, HEAD_DIM, 1)

    def body(dg1_ref, dgq_ref, dgk_ref, dcb_ref, dcw_ref, dbg_ref, dg2_ref, db_ref, loss_ref, o_ref):
        o_ref[...] = jnp.zeros_like(o_ref)
        o_ref[0:1, :] = jnp.sum(dg1_ref[...], axis=0, keepdims=True)
        o_ref[1:2, 0:LANES] = head_fold(jnp.sum(dgq_ref[...], axis=0, keepdims=True))
        o_ref[2:3, 0:LANES] = head_fold(jnp.sum(dgk_ref[...], axis=0, keepdims=True))
        o_ref[3:4, :] = jnp.sum(dcb_ref[...], axis=0, keepdims=True)
        for j in range(3):
            o_ref[4 + j:5 + j, :] = jnp.sum(dcw_ref[j * SUBLANES:(j + 1) * SUBLANES, :], axis=0, keepdims=True)
        o_ref[7:8, :] = jnp.sum(dbg_ref[:, :D_MODEL], axis=0, keepdims=True)
        o_ref[8:9, :] = jnp.sum(dbg_ref[:, D_MODEL:], axis=0, keepdims=True)
        o_ref[9:10, :] = jnp.sum(dg2_ref[...], axis=0, keepdims=True)
        for h in range(N_HEADS):
            for part in range(3):
                o_ref[10 + h:11 + h, part * QB:(part + 1) * QB] = db_ref[h, part:part + 1, :]
        loss = (0.5 / D_MODEL) * jnp.sum(jnp.sum(loss_ref[...], axis=0, keepdims=True), axis=-1, keepdims=True)
        o_ref[26:27, :] = jnp.broadcast_to(loss, (1, D_MODEL))

    return pl.pallas_call(
        body, name="small_partials",
        out_shape=jax.ShapeDtypeStruct((32, D_MODEL), F32),
        compiler_params=_params(),
    )(dg1, dgq, dgk, dcb, dcw, dbg, dg2, dbias_fold, loss_tile)


MID_AXES = (0, 0, 1, 0)
MLP_AXES = (1, 0)


def _local_step(x, target, norm1_g, q_norm_g, k_norm_g, bias, conv_w, conv_b, b_gate, norm2_g,
                w_in, mid_w, mlp_w, distributed):
    g1 = norm1_g.reshape(1, D_MODEL)
    g2 = norm2_g.reshape(1, D_MODEL)
    gq = jnp.tile(q_norm_g, N_HEADS).reshape(1, D_MODEL)
    gk = jnp.tile(k_norm_g, N_HEADS).reshape(1, D_MODEL)
    cb = conv_b.reshape(1, D_MODEL)

    (proj, h), got = _in_proj(x, g1, w_in, rider=_Gather(mid_w, MID_AXES) if distributed else None)
    w_ap, w_cp, w_g, w_out = got if distributed else mid_w
    (gates,), _ = _proj("gate_proj", h, w_g, b=b_gate.reshape(1, 2 * D_MODEL))
    qn, kn = _qknorm_fwd(proj, gq, gk)
    (y_attn, lse), got = _attn_fwd(qn, kn, proj, bias, rider=_Gather(mlp_w, MLP_AXES) if distributed else None)
    w_up, w_down = got if distributed else mlp_w
    y_conv = _conv_fwd(proj, conv_w, cb)
    ya, yc, merged, x1, h2 = _mix_out(y_attn, y_conv, gates, x, w_ap, w_cp, w_out, g2)
    a, dy, dyb, loss_tile = _mlp_fwd(h2, w_up, w_down, x1, target)

    da, dx1, dx1b, dg2 = _mlp_bwd(dyb, a, w_down, w_up, x1, dy, g2)
    gw_down = _wgrad("wgrad_down", a, [dyb], [1], relu_sq=True, token_block=4096)
    gw_up = _wgrad("wgrad_up", h2, [da], [D_FF // D_MODEL])
    (dgp, dya, dyc, dyattn, dyconv, dhg, dbg), mlp_swapped = _mix_bwd(
        dx1b, gates, ya, yc, w_out, w_ap, w_cp, w_g,
        rider=_PairSwap((gw_up, gw_down), MLP_AXES) if distributed else None)
    mid = tuple(_wgrad_group("wgrad_mid", [(y_attn, dya, 1), (y_conv, dyc, 1), (h, dgp, 2), (merged, dx1b, 1)]))
    gw_ap, gw_cp, gw_g, gw_out = mid
    (dconv, dcb, dcw), mid_swapped = _conv_bwd(
        dyconv, proj, conv_w, cb, rider=_PairSwap(mid, MID_AXES) if distributed else None)
    early = mid + (gw_up, gw_down)
    early_sums = (_pair_add(early, tuple(mid_swapped) + tuple(mlp_swapped), MID_AXES + MLP_AXES)
                  if distributed else None)
    (dqn, dkn, dv, dbias), early_shares = _attn_bwd(
        qn, kn, proj, dyattn, y_attn, lse, bias, rider=_ChipScatter(early_sums) if distributed else None)
    dqk, dgq, dgk = _qknorm_bwd(proj, dqn, dkn, gq, gk)
    gw_in = _wgrad("wgrad_in", h, [dqk, dv, dconv], [2, 1, 3])
    (dbias_fold,), in_swapped = _bias_grad_fold(dbias, rider=_PairSwap((gw_in,), (1,)) if distributed else None)
    in_sums = _pair_add((gw_in,), in_swapped, (1,)) if distributed else None
    (dx, dg1), in_shares = _in_bwd(dqk, dv, dconv, w_in, dhg, x, g1, dx1,
                                   rider=_ChipScatter(in_sums) if distributed else None)
    small = _small_partials(dg1, dgq, dgk, dcb, dcw, dbg, dg2, dbias_fold, loss_tile)
    grads = tuple(in_shares) + tuple(early_shares) if distributed else (gw_in,) + early
    return dx, grads, small


def _me():
    return lax.axis_index("x"), lax.axis_index("y"), lax.axis_index("c")


def _peer(me, rel):
    x, y, c = me
    return (1 - x if rel & 4 else x, 1 - y if rel & 2 else y, 1 - c if rel & 1 else c)


def _linear(dev):
    return 4 * dev[0] + 2 * dev[1] + dev[2]


def _block(ref, axis, idx, size):
    return ref.at[pl.ds(idx * size, size), :] if axis == 0 else ref.at[:, pl.ds(idx * size, size)]


def _cast_shards(shards):
    def body(*refs):
        for src, dst in zip(refs[:len(shards)], refs[len(shards):]):
            dst[...] = src[...].astype(BF16)

    return pl.pallas_call(
        body, name="cast_shards",
        out_shape=[jax.ShapeDtypeStruct(s.shape, BF16) for s in shards],
        compiler_params=_params(),
    )(*shards)


class _Gather:
    def __init__(self, shards, axes):
        self.arrays, self.axes, self.n = list(shards), tuple(axes), len(shards)
        self.sizes = [s.shape[axis] for s, axis in zip(shards, axes)]
        self.out_shape = []
        for s, axis in zip(shards, axes):
            shape = (s.shape[0] * N_DEV, s.shape[1]) if axis == 0 else (s.shape[0], s.shape[1] * N_DEV)
            self.out_shape.append(jax.ShapeDtypeStruct(shape, s.dtype))
        self.scratch = [pltpu.SemaphoreType.DMA((self.n, 7)), pltpu.SemaphoreType.DMA((self.n, 7)),
                        pltpu.SemaphoreType.DMA((self.n,))]

    def _copies(self, srcs, outs, sems):
        send_sems, recv_sems, local_sems = sems
        me = _me()
        sibling = _peer(me, 1)
        chips = [_peer(me, rel) for rel in (4, 2, 6)]

        def rows(a, dev):
            return _block(outs[a], self.axes[a], _linear(dev), self.sizes[a])

        def copy(a, k, block_dev, to, src=None):
            return pltpu.make_async_remote_copy(
                src_ref=rows(a, block_dev) if src is None else src, dst_ref=rows(a, block_dev),
                send_sem=send_sems.at[a, k], recv_sem=recv_sems.at[a, k], device_id=to, device_id_type=MESH_T)

        own = [pltpu.make_async_copy(srcs[a], rows(a, me), local_sems.at[a]) for a in range(self.n)]
        first = []
        for a in range(self.n):
            first.append(copy(a, 0, me, sibling, src=srcs[a]))
            for j, chip in enumerate(chips):
                first.append(copy(a, 1 + j, me, chip, src=srcs[a]))
        return me, sibling, chips, copy, own, first

    def start(self, srcs, outs, sems):
        _, _, _, _, own, first = self._copies(srcs, outs, sems)
        for cp in own + first:
            cp.start()

    def finish(self, srcs, outs, sems):
        me, sibling, chips, copy, own, first = self._copies(srcs, outs, sems)
        passed = []
        for a in range(self.n):
            for j, chip in enumerate(chips):
                copy(a, 1 + j, chip, me).wait_recv()
                fwd = copy(a, 4 + j, chip, sibling)
                fwd.start()
                passed.append(fwd)
        for a in range(self.n):
            copy(a, 0, sibling, me).wait_recv()
            for j, chip in enumerate(chips):
                copy(a, 4 + j, _peer(chip, 1), me).wait_recv()
        for cp in first + passed:
            cp.wait_send()
        for cp in own:
            cp.wait()


N_CHIPS = 4


def _shard_shape(g, axis):
    return (g.shape[0] // N_DEV, g.shape[1]) if axis == 0 else (g.shape[0], g.shape[1] // N_DEV)


class _PairSwap:
    def __init__(self, grads, axes):
        self.arrays, self.axes, self.n = list(grads), tuple(axes), len(grads)
        self.sizes = [g.shape[axis] // N_DEV for g, axis in zip(grads, axes)]
        self.out_shape = [jax.ShapeDtypeStruct((N_CHIPS,) + _shard_shape(g, axis), g.dtype)
                          for g, axis in zip(grads, axes)]
        self.scratch = [pltpu.SemaphoreType.DMA((self.n, N_CHIPS)), pltpu.SemaphoreType.DMA((self.n, N_CHIPS))]

    def _copies(self, srcs, outs, sems):
        send_sems, recv_sems = sems
        x, y, c = _me()
        sibling = (x, y, 1 - c)
        copies = []
        for a in range(self.n):
            for chip in range(N_CHIPS):
                owner_idx = 2 * chip + (1 - c)
                copies.append(pltpu.make_async_remote_copy(
                    src_ref=_block(srcs[a], self.axes[a], owner_idx, self.sizes[a]), dst_ref=outs[a].at[chip],
                    send_sem=send_sems.at[a, chip], recv_sem=recv_sems.at[a, chip],
                    device_id=sibling, device_id_type=MESH_T))
        return copies

    def start(self, srcs, outs, sems):
        for cp in self._copies(srcs, outs, sems):
            cp.start()

    def finish(self, srcs, outs, sems):
        for cp in self._copies(srcs, outs, sems):
            cp.wait()


def _pair_add(grads, swapped, axes):
    n = len(grads)
    c_arr = lax.axis_index("c").astype(jnp.int32).reshape(1)

    def body(c_ref, *refs):
        del c_ref
        mine, got, outs = refs[:n], refs[n:2 * n], refs[2 * n:]
        for a in range(n):
            outs[a][0] = (mine[a][...].astype(F32) + got[a][0].astype(F32)).astype(BF16)

    in_specs, out_specs, out_shape = [], [], []
    for g, axis in zip(grads, axes):
        shard = _shard_shape(g, axis)
        if axis == 0:
            in_specs.append(pl.BlockSpec(shard, lambda s, c_ref: (2 * s + c_ref[0], 0)))
        else:
            in_specs.append(pl.BlockSpec(shard, lambda s, c_ref: (0, 2 * s + c_ref[0])))
    for g, axis in zip(grads, axes):
        shard = _shard_shape(g, axis)
        in_specs.append(pl.BlockSpec((1,) + shard, lambda s, c_ref: (s, 0, 0)))
        out_specs.append(pl.BlockSpec((1,) + shard, lambda s, c_ref: (s, 0, 0)))
        out_shape.append(jax.ShapeDtypeStruct((N_CHIPS,) + shard, BF16))
    return pl.pallas_call(
        body, name="pair_add_" + str(n),
        grid_spec=pltpu.PrefetchScalarGridSpec(num_scalar_prefetch=1, grid=(N_CHIPS,), in_specs=in_specs,
                                               out_specs=out_specs),
        out_shape=out_shape, compiler_params=_params(("arbitrary",)),
    )(c_arr, *grads, *swapped)


class _ChipScatter:
    def __init__(self, sums):
        self.arrays, self.n = list(sums), len(sums)
        self.out_shape = [jax.ShapeDtypeStruct(s.shape, s.dtype) for s in sums]
        self.scratch = [pltpu.SemaphoreType.DMA((self.n, 3)), pltpu.SemaphoreType.DMA((self.n, 3)),
                        pltpu.SemaphoreType.DMA((self.n,))]

    def _copies(self, srcs, outs, sems):
        send_sems, recv_sems, local_sems = sems
        me = _me()
        my_chip = 2 * me[0] + me[1]
        own = [pltpu.make_async_copy(srcs[a].at[my_chip], outs[a].at[my_chip], local_sems.at[a])
               for a in range(self.n)]
        sends, recvs = [], []
        for a in range(self.n):
            for k, rel in enumerate((4, 2, 6)):
                peer = _peer(me, rel)
                peer_chip = 2 * peer[0] + peer[1]
                sends.append(pltpu.make_async_remote_copy(
                    src_ref=srcs[a].at[peer_chip], dst_ref=outs[a].at[my_chip],
                    send_sem=send_sems.at[a, k], recv_sem=recv_sems.at[a, k], device_id=peer, device_id_type=MESH_T))
                recvs.append(pltpu.make_async_remote_copy(
                    src_ref=srcs[a].at[my_chip], dst_ref=outs[a].at[peer_chip],
                    send_sem=send_sems.at[a, k], recv_sem=recv_sems.at[a, k], device_id=peer, device_id_type=MESH_T))
        return own, sends, recvs

    def start(self, srcs, outs, sems):
        own, sends, _ = self._copies(srcs, outs, sems)
        for cp in own + sends:
            cp.start()

    def finish(self, srcs, outs, sems):
        own, sends, recvs = self._copies(srcs, outs, sems)
        for cp in recvs:
            cp.wait_recv()
        for cp in sends:
            cp.wait_send()
        for cp in own:
            cp.wait()


def _call(body, *, name, args, in_specs, out_specs, out_shape, grid=(), scratch_shapes=(), semantics=None,
          rider=None):
    if rider is None:
        return pl.pallas_call(
            body, name=name, grid=grid, in_specs=in_specs, out_specs=out_specs, out_shape=out_shape,
            scratch_shapes=list(scratch_shapes), compiler_params=_params(semantics))(*args), None
    n_in, n_out, n_scr, r = len(in_specs), len(out_specs), len(scratch_shapes), rider.n

    def wrapped(*refs):
        ins, r_ins = refs[:n_in], refs[n_in:n_in + r]
        outs = refs[n_in + r:n_in + r + n_out]
        r_outs = refs[n_in + r + n_out:n_in + 2 * r + n_out]
        scr = refs[n_in + 2 * r + n_out:n_in + 2 * r + n_out + n_scr]
        sems = refs[n_in + 2 * r + n_out + n_scr:]
        first, last = None, None
        for ax in range(len(grid)):
            f, l = pl.program_id(ax) == 0, pl.program_id(ax) == pl.num_programs(ax) - 1
            first = f if first is None else first & f
            last = l if last is None else last & l
        if first is None:
            rider.start(r_ins, r_outs, sems)
            body(*ins, *outs, *scr)
            rider.finish(r_ins, r_outs, sems)
            return

        @pl.when(first)
        def _():
            rider.start(r_ins, r_outs, sems)

        body(*ins, *outs, *scr)

        @pl.when(last)
        def _():
            rider.finish(r_ins, r_outs, sems)

    any_spec = pl.BlockSpec(memory_space=pl.ANY)
    out = pl.pallas_call(
        wrapped, name=name, grid=grid, in_specs=list(in_specs) + [any_spec] * r,
        out_specs=list(out_specs) + [any_spec] * r, out_shape=list(out_shape) + rider.out_shape,
        scratch_shapes=list(scratch_shapes) + rider.scratch,
        compiler_params=_params(None if semantics is None else ("arbitrary",) * len(semantics)),
    )(*args, *rider.arrays)
    return out[:n_out], out[n_out:]


def _all_reduce_small(part):
    def body(p_ref, o_ref, slots, send_sems, recv_sems):
        me = _me()
        my_idx = _linear(me)
        slots[my_idx] = p_ref[...]
        sends = []
        for rel in range(1, N_DEV):
            cp = pltpu.make_async_remote_copy(
                src_ref=p_ref, dst_ref=slots.at[my_idx], send_sem=send_sems.at[rel - 1],
                recv_sem=recv_sems.at[rel - 1], device_id=_peer(me, rel), device_id_type=MESH_T)
            cp.start()
            sends.append(cp)
        for rel in range(1, N_DEV):
            frm = _peer(me, rel)
            pltpu.make_async_remote_copy(
                src_ref=p_ref, dst_ref=slots.at[_linear(frm)], send_sem=send_sems.at[rel - 1],
                recv_sem=recv_sems.at[rel - 1], device_id=frm, device_id_type=MESH_T).wait_recv()
        for cp in sends:
            cp.wait_send()
        total = slots[0]
        for d in range(1, N_DEV):
            total = total + slots[d]
        o_ref[...] = total

    return pl.pallas_call(
        body, name="all_reduce_small",
        in_specs=[pl.BlockSpec(memory_space=pltpu.VMEM)], out_specs=pl.BlockSpec(memory_space=pltpu.VMEM),
        out_shape=jax.ShapeDtypeStruct(part.shape, F32),
        scratch_shapes=[pltpu.VMEM((N_DEV,) + part.shape, F32), pltpu.SemaphoreType.DMA((7,)),
                        pltpu.SemaphoreType.DMA((7,))],
        compiler_params=_params(),
    )(part)


def _adamw_math(w, g, m, v):
    m = ADAM_B1 * m + (1.0 - ADAM_B1) * g
    v = ADAM_B2 * v + (1.0 - ADAM_B2) * jnp.square(g)
    m_hat = m / (1.0 - ADAM_B1 ** ADAM_STEP)
    v_hat = v / (1.0 - ADAM_B2 ** ADAM_STEP)
    delta = -ADAM_LR * (m_hat / (jnp.sqrt(v_hat) + ADAM_EPS) + ADAM_WD * w)
    return delta, m, v


ADAMW_STEPS = 4


def _adamw_big(shares, ws, ms, vs, rider=None):
    n = len(ws)

    def body(*refs):
        s_refs, w_refs, m_refs, v_refs = (refs[a * n:(a + 1) * n] for a in range(4))
        outs = refs[4 * n:]
        for a in range(n):
            g = s_refs[a][0].astype(F32)
            for d in range(1, N_CHIPS):
                g = g + s_refs[a][d].astype(F32)
            outs[4 * a][...] = g
            outs[4 * a + 1][...], outs[4 * a + 2][...], outs[4 * a + 3][...] = _adamw_math(
                w_refs[a][...], g, m_refs[a][...], v_refs[a][...])

    def chunk(w):
        return pl.BlockSpec((w.shape[0] // ADAMW_STEPS, w.shape[1]), lambda i: (i, 0))

    def share_chunk(w):
        return pl.BlockSpec((N_CHIPS, w.shape[0] // ADAMW_STEPS, w.shape[1]), lambda i: (0, i, 0))

    out, rider_out = _call(
        body, name="adamw_big", grid=(ADAMW_STEPS,), args=(*shares, *ws, *ms, *vs),
        in_specs=[share_chunk(w) for w in ws] + [chunk(w) for w in ws] * 3,
        out_specs=[chunk(w) for w in ws for _ in range(4)],
        out_shape=[jax.ShapeDtypeStruct(w.shape, F32) for w in ws for _ in range(4)],
        semantics=("parallel",), rider=rider)
    return [tuple(out[4 * a:4 * a + 4]) for a in range(n)], rider_out


def _adamw_small(quads):
    n = len(quads)

    def body(*refs):
        ins, outs = refs[:4 * n], refs[4 * n:]
        for p in range(n):
            g_ref, w_ref, m_ref, v_ref = ins[4 * p:4 * p + 4]
            d_ref, nm_ref, nv_ref = outs[3 * p:3 * p + 3]
            d_ref[...], nm_ref[...], nv_ref[...] = _adamw_math(w_ref[...], g_ref[...], m_ref[...], v_ref[...])

    flat = [a for quad in quads for a in quad]
    out = pl.pallas_call(
        body, name="adamw_small",
        out_shape=[jax.ShapeDtypeStruct(quad[1].shape, F32) for quad in quads for _ in range(3)],
        compiler_params=_params(),
    )(*flat)
    return [tuple(out[3 * p:3 * p + 3]) for p in range(n)]


def kernel(x, norm1_g, w_in, q_norm_g, k_norm_g, rel_bias, conv_w, conv_b, w_attn_proj, w_conv_proj, w_gate, b_gate, w_out, norm2_g, w_up, w_down, loss_target, m_norm1_g, m_w_in, m_q_norm_g, m_k_norm_g, m_rel_bias, m_conv_w, m_conv_b, m_w_attn_proj, m_w_conv_proj, m_w_gate, m_b_gate, m_w_out, m_norm2_g, m_w_up, m_w_down, v_norm1_g, v_w_in, v_q_norm_g, v_k_norm_g, v_rel_bias, v_conv_w, v_conv_b, v_w_attn_proj, v_w_conv_proj, v_w_gate, v_b_gate, v_w_out, v_norm2_g, v_w_up, v_w_down):
    my_idx = _linear(_me())
    big_w = (w_in, w_attn_proj, w_conv_proj, w_gate, w_out, w_up, w_down)
    big_m = (m_w_in, m_w_attn_proj, m_w_conv_proj, m_w_gate, m_w_out, m_w_up, m_w_down)
    big_v = (v_w_in, v_w_attn_proj, v_w_conv_proj, v_w_gate, v_w_out, v_w_up, v_w_down)
    big_names = ("w_in", "w_attn_proj", "w_conv_proj", "w_gate", "w_out", "w_up", "w_down")

    conv_w_tile = jnp.pad(conv_w, ((0, SUBLANES - conv_w.shape[0]), (0, 0)))
    shards = _cast_shards(big_w)
    (bias,), (w_in_full, conv_w_rows) = _bias_tiles(rel_bias, rider=_Gather((shards[0], conv_w_tile), (1, 1)))

    dx, shares, small = _local_step(x[0], loss_target[0], norm1_g, q_norm_g, k_norm_g, bias, conv_w_rows[:3],
                                    conv_b, b_gate, norm2_g, w_in_full, tuple(shards[1:5]), tuple(shards[5:7]), True)

    big_out, _ = _adamw_big(shares, big_w, big_m, big_v)
    tot = _all_reduce_small(small)
    g_rel_bias = jnp.concatenate(
        [tot[10:26, :QB][:, ::-1], tot[10:26, QB:2 * QB][:, ::-1], tot[10:26, 2 * QB:2 * QB + 1]], axis=1)
    g_conv_w = lax.dynamic_slice(tot[4:7], (0, my_idx * LANES), (3, LANES))
    small_g = [tot[0:1], tot[1:2, :HEAD_DIM], tot[2:3, :HEAD_DIM], g_rel_bias, g_conv_w, tot[3:4],
               tot[7:9].reshape(1, 2 * D_MODEL), tot[9:10]]
    small_w = (norm1_g, q_norm_g, k_norm_g, rel_bias, conv_w, conv_b, b_gate, norm2_g)
    small_m = (m_norm1_g, m_q_norm_g, m_k_norm_g, m_rel_bias, m_conv_w, m_conv_b, m_b_gate, m_norm2_g)
    small_v = (v_norm1_g, v_q_norm_g, v_k_norm_g, v_rel_bias, v_conv_w, v_conv_b, v_b_gate, v_norm2_g)

    def two_d(a):
        return a.reshape(1, -1) if a.ndim == 1 else a

    small_out = _adamw_small([(g, two_d(w), two_d(m), two_d(v))
                              for g, w, m, v in zip(small_g, small_w, small_m, small_v)])

    order = ("norm1_g", "w_in", "q_norm_g", "k_norm_g", "rel_bias", "conv_w", "conv_b", "w_attn_proj", "w_conv_proj",
             "w_gate", "b_gate", "w_out", "norm2_g", "w_up", "w_down")
    small_names = ("norm1_g", "q_norm_g", "k_norm_g", "rel_bias", "conv_w", "conv_b", "b_gate", "norm2_g")
    res = {}
    for name, (g, d, nm, nv) in zip(big_names, big_out):
        res[name] = (g, d, nm, nv)
    for name, g, w, (d, nm, nv) in zip(small_names, small_g, small_w, small_out):
        res[name] = tuple(a.reshape(w.shape) for a in (g, d, nm, nv))
    loss = tot[26, 0]
    return (loss, dx[None], *[res[n][0] for n in order], *[res[n][1] for n in order],
            *[res[n][2] for n in order], *[res[n][3] for n in order])
```

```python
import functools

import jax
import jax.numpy as jnp
from jax import lax
from jax.experimental import pallas as pl
from jax.experimental.pallas import tpu as pltpu

F32 = jnp.float32
BF16 = jnp.bfloat16

D_MODEL = 1024
N_HEADS = 16
HEAD_DIM = 64
CHUNK = 64
N_PREV_CHUNKS = 8
MAX_REL = 256
D_FF = 4096
EPS = 1e-6
NEG_INF = -1e30
LOG2E = 1.4426950408889634
N_DEV = 8

ADAM_LR = 0.001
ADAM_B1 = 0.9
ADAM_B2 = 0.999
ADAM_EPS = 1e-08
ADAM_WD = 0.01
ADAM_STEP = 10

LANES = 128
SUBLANES = 8
VMEM_LIMIT = 56 * 1024 * 1024
QB = 256
KW = 3 * QB
PAIRS = 4
SLAB = PAIRS * LANES
PAIRS_FWD = 8
SKEW = 1024

MESH_T = pl.DeviceIdType.MESH


def _dot(a, b):
    return jnp.dot(a, b, preferred_element_type=F32)


def _dot_nt(a, b):
    return lax.dot_general(a, b, (((1,), (1,)), ((), ())), preferred_element_type=F32)


def _dot_tn(a, b):
    return lax.dot_general(a, b, (((0,), (0,)), ((), ())), preferred_element_type=F32)


def _params(sem=None):
    return pltpu.CompilerParams(dimension_semantics=sem, vmem_limit_bytes=VMEM_LIMIT)


def _resident(shape):
    return pl.BlockSpec(shape, lambda *_: (0,) * len(shape), pipeline_mode=pl.Buffered(1))


def _fold8(v):
    rows, n = v.shape
    return v.reshape(rows // SUBLANES, SUBLANES, n).sum(axis=0)


def _head_sum_matrix():
    r = lax.broadcasted_iota(jnp.int32, (LANES, LANES), 0) // HEAD_DIM
    c = lax.broadcasted_iota(jnp.int32, (LANES, LANES), 1) // HEAD_DIM
    return (r == c).astype(BF16)


def _head_sums(v, e):
    hi = v.astype(BF16)
    lo = (v - hi.astype(F32)).astype(BF16)
    return _dot(hi, e) + _dot(lo, e)


def _in_proj(x, g1, w_in, rider=None):
    t = x.shape[0]
    tm = min(t, 512)
    n_out = w_in.shape[1]

    def body(x_ref, g_ref, w_ref, proj_ref, h_ref):
        xf = x_ref[...]
        r = lax.rsqrt(jnp.mean(xf * xf, axis=-1, keepdims=True) + EPS)
        h = (xf * r * g_ref[...]).astype(BF16)
        h_ref[...] = h
        for k in range(n_out // D_MODEL):
            cols = slice(k * D_MODEL, (k + 1) * D_MODEL)
            proj_ref[:, cols] = _dot(h, w_ref[:, cols]).astype(BF16)

    return _call(
        body, name="in_proj", grid=(t // tm,), args=(x, g1, w_in),
        in_specs=[pl.BlockSpec((tm, D_MODEL), lambda i: (i, 0)),
                  pl.BlockSpec((1, D_MODEL), lambda i: (0, 0)),
                  _resident((D_MODEL, n_out))],
        out_specs=[pl.BlockSpec((tm, n_out), lambda i: (i, 0)),
                   pl.BlockSpec((tm, D_MODEL), lambda i: (i, 0))],
        out_shape=[jax.ShapeDtypeStruct((t, n_out), BF16), jax.ShapeDtypeStruct((t, D_MODEL), BF16)],
        semantics=("parallel",), rider=rider)


def _proj(name, h, w, b=None, rider=None):
    t = h.shape[0]
    tm = min(t, 512)
    n_out = w.shape[1]

    def body(*refs):
        h_ref, w_ref, o_ref = refs[0], refs[1], refs[-1]
        hv = h_ref[...]
        for k in range(n_out // D_MODEL):
            cols = slice(k * D_MODEL, (k + 1) * D_MODEL)
            y = _dot(hv, w_ref[:, cols])
            if b is not None:
                y = jax.nn.sigmoid(y + refs[2][:, cols])
            o_ref[:, cols] = y.astype(BF16)

    in_specs = [pl.BlockSpec((tm, D_MODEL), lambda i: (i, 0)), _resident((D_MODEL, n_out))]
    args = (h, w)
    if b is not None:
        in_specs.append(pl.BlockSpec((1, n_out), lambda i: (0, 0)))
        args = (h, w, b)
    return _call(
        body, name=name, grid=(t // tm,), args=args, in_specs=in_specs,
        out_specs=[pl.BlockSpec((tm, n_out), lambda i: (i, 0))],
        out_shape=[jax.ShapeDtypeStruct((t, n_out), BF16)],
        semantics=("parallel",), rider=rider)


def _qknorm_fwd(proj, gq, gk):
    t = proj.shape[0]
    tm = min(t, 512)
    scale = HEAD_DIM ** -0.5 * LOG2E

    def body(q_ref, k_ref, gq_ref, gk_ref, qn_ref, kn_ref):
        e = _head_sum_matrix()
        for src, g_ref, dst, sc in ((q_ref, gq_ref, qn_ref, scale), (k_ref, gk_ref, kn_ref, 1.0)):
            for s in range(D_MODEL // LANES):
                sl = slice(s * LANES, (s + 1) * LANES)
                xf = src[:, sl].astype(F32)
                r = lax.rsqrt(_head_sums(xf * xf, e) * (1.0 / HEAD_DIM) + EPS)
                dst[:, sl] = (xf * r * g_ref[:, sl] * sc).astype(BF16)

    return pl.pallas_call(
        body, name="qknorm_fwd", grid=(t // tm,),
        in_specs=[pl.BlockSpec((tm, D_MODEL), lambda i: (i, 0)),
                  pl.BlockSpec((tm, D_MODEL), lambda i: (i, 1)),
                  pl.BlockSpec((1, D_MODEL), lambda i: (0, 0)),
                  pl.BlockSpec((1, D_MODEL), lambda i: (0, 0))],
        out_specs=[pl.BlockSpec((tm, D_MODEL), lambda i: (i, 0))] * 2,
        out_shape=[jax.ShapeDtypeStruct((t, D_MODEL), BF16)] * 2,
        compiler_params=_params(("parallel",)),
    )(proj, proj, gq, gk)


def _bias_tiles(rel_bias, rider=None):
    by_dist = jnp.concatenate(
        [rel_bias[:, :2 * MAX_REL], jnp.broadcast_to(rel_bias[:, 2 * MAX_REL:], (N_HEADS, 2 * MAX_REL))], axis=1)
    by_dist = by_dist.reshape(N_HEADS, 1, SKEW)

    def body(f_ref, o_ref):
        jj = lax.broadcasted_iota(jnp.int32, (QB, QB), 0)
        ii = lax.broadcasted_iota(jnp.int32, (QB, QB), 1)
        for w in range(KW // QB):
            pos = jnp.broadcast_to(f_ref[0, :, KW - QB * w:KW - QB * w + QB], (QB, QB))
            neg = jnp.broadcast_to(f_ref[0, :, KW - QB * (w + 1):KW - QB * w], (QB, QB))
            pos = pltpu.roll(pos, 0, 1, stride=1, stride_axis=0)
            neg = pltpu.roll(neg, 0, 1, stride=1, stride_axis=0)
            tile = jnp.where(ii >= jj, pos, neg)
            kc = (jj + QB * w) // CHUNK
            qc = ii // CHUNK
            band = (kc >= qc) & (kc <= qc + N_PREV_CHUNKS)
            o_ref[0, QB * w:QB * (w + 1), :] = jnp.where(band, tile * LOG2E, NEG_INF)

    return _call(
        body, name="bias_tiles", grid=(N_HEADS,), args=(by_dist,),
        in_specs=[pl.BlockSpec((1, 1, SKEW), lambda h: (h, 0, 0))],
        out_specs=[pl.BlockSpec((1, KW, QB), lambda h: (h, 0, 0))],
        out_shape=[jax.ShapeDtypeStruct((N_HEADS, KW, QB), F32)],
        semantics=("parallel",), rider=rider)


def _window_specs(col0, slab):
    return [pl.BlockSpec((QB, slab), functools.partial(
        lambda p, b, back: (jnp.maximum(b - back, 0), col0 + p), back=back)) for back in (2, 1, 0)]


def _attn_fwd(qn, kn, proj, bias, rider=None):
    t = qn.shape[0]
    nb = t // QB
    pairs = PAIRS_FWD
    slab = pairs * LANES
    v_col0 = 2 * D_MODEL // slab

    def body(q_ref, k0, k1, k2, v0, v1, v2, bias_ref, o_ref, lse_ref):
        b = pl.program_id(1)

        @pl.when(b < 2)
        def _():
            compute(q_ref, k0, k1, k2, v0, v1, v2, bias_ref, o_ref, lse_ref,
                    lax.broadcasted_iota(jnp.int32, (KW, 1), 0) >= (2 - b) * QB)

        @pl.when(b >= 2)
        def _():
            compute(q_ref, k0, k1, k2, v0, v1, v2, bias_ref, o_ref, lse_ref, None)

    def compute(q_ref, k0, k1, k2, v0, v1, v2, bias_ref, o_ref, lse_ref, valid):
        head_a = lax.broadcasted_iota(jnp.int32, (1, LANES), 1) < HEAD_DIM

        def scores(head):
            hp, hh = divmod(head, 2)
            sl = slice(hp * LANES, (hp + 1) * LANES)
            k = jnp.concatenate([k0[:, sl], k1[:, sl], k2[:, sl]], axis=0)
            mine = head_a if hh == 0 else jnp.logical_not(head_a)
            s = _dot_nt(jnp.where(mine, k, jnp.zeros_like(k)), q_ref[:, sl]) + bias_ref[head]
            return s if valid is None else jnp.where(valid, s, NEG_INF)

        def weighted_values(head, s):
            hp, hh = divmod(head, 2)
            sl = slice(hp * LANES, (hp + 1) * LANES)
            v = jnp.concatenate([v0[:, sl], v1[:, sl], v2[:, sl]], axis=0)
            vt = v.astype(F32).T.astype(BF16)[hh * HEAD_DIM:(hh + 1) * HEAD_DIM]
            vt = jnp.concatenate([vt, jnp.ones((SUBLANES, KW), BF16)], axis=0)
            m = jnp.max(s, axis=0, keepdims=True)
            ov = _dot(vt, jnp.exp2(s - m).astype(BF16))
            l = ov[HEAD_DIM:HEAD_DIM + 1]
            return ov[:HEAD_DIM] / l, m + jnp.log2(l)

        outs, lses = [], []
        pending = scores(0)
        for head in range(2 * pairs):
            nxt = scores(head + 1) if head + 1 < 2 * pairs else None
            o, lse = weighted_values(head, pending)
            outs.append(o)
            lses.append(lse)
            pending = nxt
        for hp in range(pairs):
            sl = slice(hp * LANES, (hp + 1) * LANES)
            o_ref[:, sl] = jnp.concatenate([outs[2 * hp], outs[2 * hp + 1]], axis=0).T.astype(BF16)
        lse_ref[...] = jnp.concatenate(lses, axis=0)

    return _call(
        body, name="attn_fwd", grid=(D_MODEL // slab, nb), args=(qn, kn, kn, kn, proj, proj, proj, bias),
        in_specs=[pl.BlockSpec((QB, slab), lambda p, b: (b, p))] + _window_specs(0, slab)
        + _window_specs(v_col0, slab) + [pl.BlockSpec((2 * pairs, KW, QB), lambda p, b: (p, 0, 0))],
        out_specs=[pl.BlockSpec((QB, slab), lambda p, b: (b, p)),
                   pl.BlockSpec((2 * pairs, QB), lambda p, b: (p, b))],
        out_shape=[jax.ShapeDtypeStruct((t, D_MODEL), BF16), jax.ShapeDtypeStruct((N_HEADS, t), F32)],
        semantics=("parallel", "arbitrary"), rider=rider)


def _shift_down(u, halo, n):
    rows = lax.broadcasted_iota(jnp.int32, (u.shape[0], 1), 0)
    out = pltpu.roll(u, n, 0)
    for j in range(n):
        out = jnp.where(rows == j, halo[SUBLANES - n + j:SUBLANES - n + j + 1, :], out)
    return out


def _shift_up(u, halo, n):
    tm = u.shape[0]
    rows = lax.broadcasted_iota(jnp.int32, (tm, 1), 0)
    out = pltpu.roll(u, tm - n, 0)
    for j in range(n):
        out = jnp.where(rows == tm - n + j, halo[j:j + 1, :], out)
    return out


def _conv_fwd(proj, conv_w, conv_b):
    t = proj.shape[0]
    tm = min(t, 512)
    hb = tm // SUBLANES

    def body(bg_ref, cg_ref, xc_ref, cgh_ref, xch_ref, w_ref, b_ref, o_ref):
        i = pl.program_id(0)
        u = cg_ref[...].astype(F32) * xc_ref[...].astype(F32)
        halo = cgh_ref[...].astype(F32) * xch_ref[...].astype(F32)
        halo = jnp.where(i > 0, halo, 0.0)
        w = w_ref[...]
        s = w[0:1] * _shift_down(u, halo, 2) + w[1:2] * _shift_down(u, halo, 1) + w[2:3] * u
        o_ref[...] = (bg_ref[...].astype(F32) * (b_ref[...] + s)).astype(BF16)

    def prev(col):
        return pl.BlockSpec((SUBLANES, D_MODEL), lambda i: (jnp.maximum(i * hb - 1, 0), col))

    return pl.pallas_call(
        body, name="conv_fwd", grid=(t // tm,),
        in_specs=[pl.BlockSpec((tm, D_MODEL), lambda i: (i, 3)),
                  pl.BlockSpec((tm, D_MODEL), lambda i: (i, 4)),
                  pl.BlockSpec((tm, D_MODEL), lambda i: (i, 5)),
                  prev(4), prev(5),
                  pl.BlockSpec((3, D_MODEL), lambda i: (0, 0)),
                  pl.BlockSpec((1, D_MODEL), lambda i: (0, 0))],
        out_specs=pl.BlockSpec((tm, D_MODEL), lambda i: (i, 0)),
        out_shape=jax.ShapeDtypeStruct((t, D_MODEL), BF16),
        compiler_params=_params(("parallel",)),
    )(proj, proj, proj, proj, proj, conv_w, conv_b)


def _gates_norms_conv(h, proj, w_g, b_g, gq, gk, conv_w, conv_b):
    t = h.shape[0]
    tm = min(t, 512)
    hb = tm // SUBLANES
    scale = HEAD_DIM ** -0.5 * LOG2E

    def body(h_ref, wg_ref, bgate_ref, q_ref, k_ref, gq_ref, gk_ref, bg_ref, cg_ref, xc_ref, cgh_ref, xch_ref,
             cw_ref, cb_ref, gates_ref, qn_ref, kn_ref, yc_ref):
        i = pl.program_id(0)
        hv = h_ref[...]
        e = _head_sum_matrix()

        def gate_cols(j, width):
            cols = slice(j * width, (j + 1) * width)
            gates_ref[:, cols] = jax.nn.sigmoid(_dot(hv, wg_ref[:, cols]) + bgate_ref[:, cols]).astype(BF16)

        def head_norm(src, g_ref, dst, sc, s):
            sl = slice(s * LANES, (s + 1) * LANES)
            xf = src[:, sl].astype(F32)
            r = lax.rsqrt(_head_sums(xf * xf, e) * (1.0 / HEAD_DIM) + EPS)
            dst[:, sl] = (xf * r * g_ref[:, sl] * sc).astype(BF16)

        def conv_cols(c, width):
            cols = slice(c * width, (c + 1) * width)
            u = cg_ref[:, cols].astype(F32) * xc_ref[:, cols].astype(F32)
            halo = jnp.where(i > 0, cgh_ref[:, cols].astype(F32) * xch_ref[:, cols].astype(F32), 0.0)
            w = cw_ref[:, cols]
            s = w[0:1] * _shift_down(u, halo, 2) + w[1:2] * _shift_down(u, halo, 1) + w[2:3] * u
            yc_ref[:, cols] = (bg_ref[:, cols].astype(F32) * (cb_ref[:, cols] + s)).astype(BF16)

        n_norm = D_MODEL // LANES
        for j in range(n_norm):
            gate_cols(j, 2 * D_MODEL // n_norm)
            head_norm(q_ref, gq_ref, qn_ref, scale, j)
            head_norm(k_ref, gk_ref, kn_ref, 1.0, j)
            if j % 2 == 1:
                conv_cols(j // 2, 2 * D_MODEL // n_norm)

    def slab(col):
        return pl.BlockSpec((tm, D_MODEL), lambda i: (i, col))

    def prev(col):
        return pl.BlockSpec((SUBLANES, D_MODEL), lambda i: (jnp.maximum(i * hb - 1, 0), col))

    vec = pl.BlockSpec((1, D_MODEL), lambda i: (0, 0))
    row = pl.BlockSpec((tm, D_MODEL), lambda i: (i, 0))
    return pl.pallas_call(
        body, name="gates_norms_conv", grid=(t // tm,),
        in_specs=[row, _resident(w_g.shape), pl.BlockSpec((1, 2 * D_MODEL), lambda i: (0, 0)),
                  slab(0), slab(1), vec, vec, slab(3), slab(4), slab(5), prev(4), prev(5),
                  pl.BlockSpec((3, D_MODEL), lambda i: (0, 0)), vec],
        out_specs=[pl.BlockSpec((tm, 2 * D_MODEL), lambda i: (i, 0)), row, row, row],
        out_shape=[jax.ShapeDtypeStruct((t, 2 * D_MODEL), BF16)] + [jax.ShapeDtypeStruct((t, D_MODEL), BF16)] * 3,
        compiler_params=_params(("parallel",)),
    )(h, w_g, b_g, proj, proj, gq, gk, proj, proj, proj, proj, proj, conv_w, conv_b)


def _mix_out(y_attn, y_conv, gates, x, w_ap, w_cp, w_out, g2):
    t = x.shape[0]
    tm = min(t, 512)

    def body(ya_in, yc_in, g_ref, x_ref, wap, wcp, wout, g2_ref, ya_ref, yc_ref, mg_ref, x1_ref, h2_ref):
        ya = _dot(ya_in[...], wap[...])
        yc = _dot(yc_in[...], wcp[...])
        ya_ref[...] = ya.astype(BF16)
        yc_ref[...] = yc.astype(BF16)
        merged = (g_ref[:, :D_MODEL].astype(F32) * ya + g_ref[:, D_MODEL:].astype(F32) * yc).astype(BF16)
        mg_ref[...] = merged
        x1 = x_ref[...] + _dot(merged, wout[...])
        x1_ref[...] = x1
        r = lax.rsqrt(jnp.mean(x1 * x1, axis=-1, keepdims=True) + EPS)
        h2_ref[...] = (x1 * r * g2_ref[...]).astype(BF16)

    row = pl.BlockSpec((tm, D_MODEL), lambda i: (i, 0))
    full = _resident((D_MODEL, D_MODEL))
    return pl.pallas_call(
        body, name="mix_out", grid=(t // tm,),
        in_specs=[row, row, pl.BlockSpec((tm, 2 * D_MODEL), lambda i: (i, 0)), row, full, full, full,
                  pl.BlockSpec((1, D_MODEL), lambda i: (0, 0))],
        out_specs=[row] * 5,
        out_shape=[jax.ShapeDtypeStruct((t, D_MODEL), BF16)] * 3
        + [jax.ShapeDtypeStruct((t, D_MODEL), F32), jax.ShapeDtypeStruct((t, D_MODEL), BF16)],
        compiler_params=_params(("parallel",)),
    )(y_attn, y_conv, gates, x, w_ap, w_cp, w_out, g2)


def _mlp_fwd(h2, w_up, w_down, x1, target):
    t = h2.shape[0]
    tm = min(t, 512)
    tf = 1024
    nf = D_FF // tf

    def body(h2_ref, wup, wdn, x1_ref, tg_ref, a_ref, dy_ref, dyb_ref, loss_ref):
        h2v = h2_ref[...]
        acc = None
        pending = _dot(h2v, wup[:, 0:tf])
        for j in range(nf):
            cols = slice(j * tf, (j + 1) * tf)
            a = pending
            if j + 1 < nf:
                pending = _dot(h2v, wup[:, (j + 1) * tf:(j + 2) * tf])
            a_ref[:, cols] = a.astype(BF16)
            part = _dot(jnp.square(jnp.maximum(a, 0.0)).astype(BF16), wdn[cols, :])
            acc = part if acc is None else acc + part

        @pl.when(pl.program_id(0) == 0)
        def _():
            loss_ref[...] = jnp.zeros_like(loss_ref)

        diff = x1_ref[...] + acc - tg_ref[...]
        loss_ref[...] += _fold8(diff * diff)
        dy = diff * (1.0 / D_MODEL)
        dy_ref[...] = dy
        dyb_ref[...] = dy.astype(BF16)

    row = pl.BlockSpec((tm, D_MODEL), lambda i: (i, 0))
    return pl.pallas_call(
        body, name="mlp_fwd", grid=(t // tm,),
        in_specs=[row, _resident((D_MODEL, D_FF)), _resident((D_FF, D_MODEL)), row, row],
        out_specs=[pl.BlockSpec((tm, D_FF), lambda i: (i, 0)), row, row,
                   pl.BlockSpec((SUBLANES, D_MODEL), lambda i: (0, 0))],
        out_shape=[jax.ShapeDtypeStruct((t, D_FF), BF16), jax.ShapeDtypeStruct((t, D_MODEL), F32),
                   jax.ShapeDtypeStruct((t, D_MODEL), BF16), jax.ShapeDtypeStruct((SUBLANES, D_MODEL), F32)],
        compiler_params=_params(("arbitrary",)),
    )(h2, w_up, w_down, x1, target)


def _rmsnorm_bwd(xf, g, dh):
    r = lax.rsqrt(jnp.mean(xf * xf, axis=-1, keepdims=True) + EPS)
    xh = xf * r
    dxh = dh * g
    dx = r * (dxh - xh * jnp.mean(dxh * xh, axis=-1, keepdims=True))
    return dx, dh * xh


def _mlp_bwd(dyb, a, w_down, w_up, x1, dy, g2):
    t = dyb.shape[0]
    tm = min(t, 512)
    tf = 1024
    nf = D_FF // tf

    def body(dyb_ref, a_ref, wdn, wup, x1_ref, dy_ref, g2_ref, da_ref, dx1_ref, dx1b_ref, dg2_ref):
        dyv = dyb_ref[...]
        acc = None
        pending = _dot_nt(dyv, wdn[0:tf, :])
        for j in range(nf):
            cols = slice(j * tf, (j + 1) * tf)
            du = pending
            if j + 1 < nf:
                pending = _dot_nt(dyv, wdn[(j + 1) * tf:(j + 2) * tf, :])
            da = (du * (2.0 * jnp.maximum(a_ref[:, cols].astype(F32), 0.0))).astype(BF16)
            da_ref[:, cols] = da
            part = _dot_nt(da, wup[:, cols])
            acc = part if acc is None else acc + part

        @pl.when(pl.program_id(0) == 0)
        def _():
            dg2_ref[...] = jnp.zeros_like(dg2_ref)

        dx, dg = _rmsnorm_bwd(x1_ref[...], g2_ref[...], acc)
        dx1 = dy_ref[...] + dx
        dx1_ref[...] = dx1
        dx1b_ref[...] = dx1.astype(BF16)
        dg2_ref[...] += _fold8(dg)

    row = pl.BlockSpec((tm, D_MODEL), lambda i: (i, 0))
    wide = pl.BlockSpec((tm, D_FF), lambda i: (i, 0))
    return pl.pallas_call(
        body, name="mlp_bwd", grid=(t // tm,),
        in_specs=[row, wide, _resident((D_FF, D_MODEL)), _resident((D_MODEL, D_FF)), row, row,
                  pl.BlockSpec((1, D_MODEL), lambda i: (0, 0))],
        out_specs=[wide, row, row, pl.BlockSpec((SUBLANES, D_MODEL), lambda i: (0, 0))],
        out_shape=[jax.ShapeDtypeStruct((t, D_FF), BF16), jax.ShapeDtypeStruct((t, D_MODEL), F32),
                   jax.ShapeDtypeStruct((t, D_MODEL), BF16), jax.ShapeDtypeStruct((SUBLANES, D_MODEL), F32)],
        compiler_params=_params(("arbitrary",)),
    )(dyb, a, w_down, w_up, x1, dy, g2)


def _wgrad(name, lhs, rhs_list, rhs_slabs, relu_sq=False, token_block=2048):
    t, m = lhs.shape
    tt = min(t, token_block)
    tmo = min(m, 1024)
    n_slab = sum(rhs_slabs)
    starts = [sum(rhs_slabs[:n]) for n in range(len(rhs_slabs))]
    n_rhs = len(rhs_list)

    def body(*refs):
        l_ref, r_refs, o_ref, acc = refs[0], refs[1:1 + n_rhs], refs[1 + n_rhs], refs[2 + n_rhs]
        k, s = pl.program_id(1), pl.program_id(2)
        lv = l_ref[...]
        if relu_sq:
            lv = jnp.square(jnp.maximum(lv.astype(F32), 0.0)).astype(BF16)

        @pl.when(s == 0)
        def _():
            acc[...] = jnp.zeros_like(acc)

        for n in range(n_rhs):
            @pl.when((k >= starts[n]) & (k < starts[n] + rhs_slabs[n]))
            def _(n=n):
                acc[...] += _dot_tn(lv, r_refs[n][...])

        @pl.when(s == pl.num_programs(2) - 1)
        def _():
            o_ref[...] = acc[...].astype(BF16)

    def rhs_spec(n):
        lo, cnt = starts[n], rhs_slabs[n]

        def index(i, k, s):
            inside = (k >= lo) & (k < lo + cnt)
            return (jnp.where(inside, s, 0), jnp.clip(k - lo, 0, cnt - 1))
        return pl.BlockSpec((tt, D_MODEL), index)

    return pl.pallas_call(
        body, name=name, grid=(m // tmo, n_slab, t // tt),
        in_specs=[pl.BlockSpec((tt, tmo), lambda i, k, s: (s, i))] + [rhs_spec(n) for n in range(n_rhs)],
        out_specs=pl.BlockSpec((tmo, D_MODEL), lambda i, k, s: (i, k)),
        out_shape=jax.ShapeDtypeStruct((m, n_slab * D_MODEL), BF16),
        scratch_shapes=[pltpu.VMEM((tmo, D_MODEL), F32)],
        compiler_params=_params(("parallel", "parallel", "arbitrary")),
    )(lhs, *rhs_list)


def _wgrad_group(name, triples):
    t = triples[0][0].shape[0]
    tt = min(t, 1024)
    counts = [n for _, _, n in triples]
    starts = [sum(counts[:n]) for n in range(len(counts))]
    n_prod = len(triples)

    def inside(n, k):
        return (k >= starts[n]) & (k < starts[n] + counts[n])

    def body(*refs):
        l_refs, r_refs, o_refs = refs[:n_prod], refs[n_prod:2 * n_prod], refs[2 * n_prod:3 * n_prod]
        acc = refs[3 * n_prod]
        k, s = pl.program_id(0), pl.program_id(1)

        @pl.when(s == 0)
        def _():
            acc[...] = jnp.zeros_like(acc)

        for n in range(n_prod):
            @pl.when(inside(n, k))
            def _(n=n):
                acc[...] += _dot_tn(l_refs[n][...], r_refs[n][...])

            @pl.when(inside(n, k) & (s == pl.num_programs(1) - 1))
            def _(n=n):
                o_refs[n][...] = acc[...].astype(BF16)

    def lhs_spec(n):
        return pl.BlockSpec((tt, D_MODEL), lambda k, s: (jnp.where(inside(n, k), s, 0), 0))

    def rhs_spec(n):
        return pl.BlockSpec((tt, D_MODEL), lambda k, s: (jnp.where(inside(n, k), s, 0),
                                                         jnp.clip(k - starts[n], 0, counts[n] - 1)))

    def out_spec(n):
        return pl.BlockSpec((D_MODEL, D_MODEL), lambda k, s: (0, jnp.clip(k - starts[n], 0, counts[n] - 1)))

    return pl.pallas_call(
        body, name=name, grid=(sum(counts), t // tt),
        in_specs=[lhs_spec(n) for n in range(n_prod)] + [rhs_spec(n) for n in range(n_prod)],
        out_specs=[out_spec(n) for n in range(n_prod)],
        out_shape=[jax.ShapeDtypeStruct((D_MODEL, n * D_MODEL), BF16) for n in counts],
        scratch_shapes=[pltpu.VMEM((D_MODEL, D_MODEL), F32)],
        compiler_params=_params(("arbitrary", "arbitrary")),
    )(*[tr[0] for tr in triples], *[tr[1] for tr in triples])


def _mix_bwd(dx1b, gates, ya, yc, w_out, w_ap, w_cp, w_g, rider=None):
    t = dx1b.shape[0]
    tm = min(t, 512)

    def body(dx_ref, g_ref, ya_ref, yc_ref, wout, wap, wcp, wg,
             dgp_ref, dya_ref, dyc_ref, dyat_ref, dycv_ref, dhg_ref, dbg_ref):
        dm = _dot_nt(dx_ref[...], wout[...])
        ga = g_ref[:, :D_MODEL].astype(F32)
        gc = g_ref[:, D_MODEL:].astype(F32)
        dya = (dm * ga).astype(BF16)
        dyc = (dm * gc).astype(BF16)
        dya_ref[...] = dya
        dyc_ref[...] = dyc
        dgpa = dm * ya_ref[...].astype(F32) * ga * (1.0 - ga)
        dgpc = dm * yc_ref[...].astype(F32) * gc * (1.0 - gc)

        @pl.when(pl.program_id(0) == 0)
        def _():
            dbg_ref[...] = jnp.zeros_like(dbg_ref)

        dbg_ref[:, :D_MODEL] += _fold8(dgpa)
        dbg_ref[:, D_MODEL:] += _fold8(dgpc)
        dgpa = dgpa.astype(BF16)
        dgpc = dgpc.astype(BF16)
        dgp_ref[:, :D_MODEL] = dgpa
        dgp_ref[:, D_MODEL:] = dgpc
        dyat_ref[...] = _dot_nt(dya, wap[...]).astype(BF16)
        dycv_ref[...] = _dot_nt(dyc, wcp[...]).astype(BF16)
        dhg_ref[...] = _dot_nt(dgpa, wg[:, :D_MODEL]) + _dot_nt(dgpc, wg[:, D_MODEL:])

    row = pl.BlockSpec((tm, D_MODEL), lambda i: (i, 0))
    row2 = pl.BlockSpec((tm, 2 * D_MODEL), lambda i: (i, 0))
    full = _resident((D_MODEL, D_MODEL))
    return _call(
        body, name="mix_bwd", grid=(t // tm,), args=(dx1b, gates, ya, yc, w_out, w_ap, w_cp, w_g),
        in_specs=[row, row2, row, row, full, full, full, _resident((D_MODEL, 2 * D_MODEL))],
        out_specs=[row2, row, row, row, row, row, pl.BlockSpec((SUBLANES, 2 * D_MODEL), lambda i: (0, 0))],
        out_shape=[jax.ShapeDtypeStruct((t, 2 * D_MODEL), BF16)] + [jax.ShapeDtypeStruct((t, D_MODEL), BF16)] * 4
        + [jax.ShapeDtypeStruct((t, D_MODEL), F32), jax.ShapeDtypeStruct((SUBLANES, 2 * D_MODEL), F32)],
        semantics=("arbitrary",), rider=rider)


def _conv_bwd(dyconv, proj, conv_w, conv_b, rider=None):
    t = proj.shape[0]
    tm = min(t, 512)
    hb = tm // SUBLANES
    last = t // SUBLANES - 1

    def body(dy_ref, dyn_ref, bg_ref, bgn_ref, cg_ref, cgp_ref, xc_ref, xcp_ref, w_ref, b_ref,
             o_ref, dcb_ref, dcw_ref):
        i = pl.program_id(0)
        cg = cg_ref[...].astype(F32)
        xc = xc_ref[...].astype(F32)
        bg = bg_ref[...].astype(F32)
        u = cg * xc
        prev = jnp.where(i > 0, cgp_ref[...].astype(F32) * xcp_ref[...].astype(F32), 0.0)
        u1 = _shift_down(u, prev, 1)
        u2 = _shift_down(u, prev, 2)
        w = w_ref[...]
        conv = b_ref[...] + (w[0:1] * u2 + w[1:2] * u1 + w[2:3] * u)
        dy = dy_ref[...].astype(F32)
        dconv = dy * bg
        nxt = jnp.where(i < pl.num_programs(0) - 1, dyn_ref[...].astype(F32) * bgn_ref[...].astype(F32), 0.0)
        du = w[2:3] * dconv + w[1:2] * _shift_up(dconv, nxt, 1) + w[0:1] * _shift_up(dconv, nxt, 2)
        o_ref[:, :D_MODEL] = (dy * conv).astype(BF16)
        o_ref[:, D_MODEL:2 * D_MODEL] = (du * xc).astype(BF16)
        o_ref[:, 2 * D_MODEL:] = (du * cg).astype(BF16)

        @pl.when(i == 0)
        def _():
            dcb_ref[...] = jnp.zeros_like(dcb_ref)
            dcw_ref[...] = jnp.zeros_like(dcw_ref)

        dcb_ref[...] += _fold8(dconv)
        dcw_ref[0:SUBLANES] += _fold8(dconv * u2)
        dcw_ref[SUBLANES:2 * SUBLANES] += _fold8(dconv * u1)
        dcw_ref[2 * SUBLANES:] += _fold8(dconv * u)

    def prev(col):
        return pl.BlockSpec((SUBLANES, D_MODEL), lambda i: (jnp.maximum(i * hb - 1, 0), col))

    def nxt(col):
        return pl.BlockSpec((SUBLANES, D_MODEL), lambda i: (jnp.minimum((i + 1) * hb, last), col))

    def cur(col):
        return pl.BlockSpec((tm, D_MODEL), lambda i: (i, col))

    return _call(
        body, name="conv_bwd", grid=(t // tm,),
        args=(dyconv, dyconv, proj, proj, proj, proj, proj, proj, conv_w, conv_b),
        in_specs=[cur(0), nxt(0), cur(3), nxt(3), cur(4), prev(4), cur(5), prev(5),
                  pl.BlockSpec((3, D_MODEL), lambda i: (0, 0)), pl.BlockSpec((1, D_MODEL), lambda i: (0, 0))],
        out_specs=[pl.BlockSpec((tm, 3 * D_MODEL), lambda i: (i, 0)),
                   pl.BlockSpec((SUBLANES, D_MODEL), lambda i: (0, 0)),
                   pl.BlockSpec((3 * SUBLANES, D_MODEL), lambda i: (0, 0))],
        out_shape=[jax.ShapeDtypeStruct((t, 3 * D_MODEL), BF16), jax.ShapeDtypeStruct((SUBLANES, D_MODEL), F32),
                   jax.ShapeDtypeStruct((3 * SUBLANES, D_MODEL), F32)],
        semantics=("arbitrary",), rider=rider)


def _attn_bwd(qn, kn, proj, dyattn, y_attn, lse, bias, rider=None):
    t = qn.shape[0]
    nb = t // QB
    v_col0 = 2 * D_MODEL // SLAB

    def body(q_ref, k0, k1, k2, v0, v1, v2, do_ref, o_ref, lse_ref, bias_ref,
             dq_ref, dk_ref, dv_ref, db_ref, acck, accv):
        b = pl.program_id(1)

        @pl.when(b == 0)
        def _():
            acck[...] = jnp.zeros_like(acck)
            accv[...] = jnp.zeros_like(accv)
            db_ref[...] = jnp.zeros_like(db_ref)

        def block(valid):
            head_a = lax.broadcasted_iota(jnp.int32, (1, LANES), 1) < HEAD_DIM

            def window(refs, hp):
                sl = slice(hp * LANES, (hp + 1) * LANES)
                return jnp.concatenate([r[:, sl] for r in refs], axis=0)

            def transposed(x, hh):
                return x.astype(F32).T.astype(BF16)[hh * HEAD_DIM:(hh + 1) * HEAD_DIM]

            def probs(head):
                hp, hh = divmod(head, 2)
                sl = slice(hp * LANES, (hp + 1) * LANES)
                mine = head_a if hh == 0 else jnp.logical_not(head_a)
                k = window((k0, k1, k2), hp)
                s = _dot_nt(jnp.where(mine, k, jnp.zeros_like(k)), q_ref[:, sl]) + bias_ref[head]
                s = s if valid is None else jnp.where(valid, s, NEG_INF)
                return jnp.exp2(s - lse_ref[head:head + 1, :])

            def grads(head, p):
                hp, hh = divmod(head, 2)
                sl = slice(hp * LANES, (hp + 1) * LANES)
                rows = slice(hh * HEAD_DIM, (hh + 1) * HEAD_DIM)
                mine = head_a if hh == 0 else jnp.logical_not(head_a)
                do = do_ref[:, sl]
                v = window((v0, v1, v2), hp)
                delta = jnp.sum((do.astype(F32).T * o_ref[:, sl].astype(F32).T)[rows], axis=0, keepdims=True)
                ds = p * (_dot_nt(jnp.where(mine, v, jnp.zeros_like(v)), do) - delta)
                db_ref[head] += ds
                pb, dsb = p.astype(BF16), ds.astype(BF16)
                dvt = _dot_nt(transposed(do, hh), pb)
                dkt = _dot_nt(transposed(q_ref[:, sl], hh), dsb) * (1.0 / LOG2E)
                dqt = _dot(transposed(window((k0, k1, k2), hp), hh), dsb)
                return dqt, dkt, dvt

            out = []
            pending = probs(0)
            for head in range(2 * PAIRS):
                nxt = probs(head + 1) if head + 1 < 2 * PAIRS else None
                out.append(grads(head, pending))
                pending = nxt
            for hp in range(PAIRS):
                sl = slice(hp * LANES, (hp + 1) * LANES)
                dqt, dkt, dvt = (jnp.concatenate([out[2 * hp][n], out[2 * hp + 1][n]], axis=0) for n in range(3))
                dq_ref[:, sl] = dqt.T.astype(BF16)
                for w in range(3):
                    slot = lax.rem(b + w + 1, 3)
                    cols = slice(w * QB, (w + 1) * QB)
                    if w == 2:
                        acck[hp, slot] = dkt[:, cols]
                        accv[hp, slot] = dvt[:, cols]
                    else:
                        acck[hp, slot] += dkt[:, cols]
                        accv[hp, slot] += dvt[:, cols]

        @pl.when(b < 2)
        def _():
            block(lax.broadcasted_iota(jnp.int32, (KW, 1), 0) >= (2 - b) * QB)

        @pl.when((b >= 2) & (b < nb))
        def _():
            block(None)

        done = lax.rem(b + 1, 3)
        for hp in range(PAIRS):
            sl = slice(hp * LANES, (hp + 1) * LANES)
            dk_ref[:, sl] = acck[hp, done].T.astype(BF16)
            dv_ref[:, sl] = accv[hp, done].T.astype(BF16)

    def cur(p, b):
        return (jnp.minimum(b, nb - 1), p)

    def window(col0):
        return [pl.BlockSpec((QB, SLAB), functools.partial(
            lambda p, b, back: (jnp.maximum(jnp.minimum(b, nb - 1) - back, 0), col0 + p), back=back))
            for back in (2, 1, 0)]

    def done_block(p, b):
        return (jnp.maximum(b - 2, 0), p)

    tile = pl.BlockSpec((2 * PAIRS, KW, QB), lambda p, b: (p, 0, 0))
    here = pl.BlockSpec((QB, SLAB), cur)
    return _call(
        body, name="attn_bwd", grid=(D_MODEL // SLAB, nb + 2),
        args=(qn, kn, kn, kn, proj, proj, proj, dyattn, y_attn, lse, bias),
        in_specs=[here] + window(0) + window(v_col0)
        + [here, here, pl.BlockSpec((2 * PAIRS, QB), lambda p, b: (p, jnp.minimum(b, nb - 1))), tile],
        out_specs=[here, pl.BlockSpec((QB, SLAB), done_block), pl.BlockSpec((QB, SLAB), done_block), tile],
        out_shape=[jax.ShapeDtypeStruct((t, D_MODEL), BF16)] * 3 + [jax.ShapeDtypeStruct((N_HEADS, KW, QB), F32)],
        scratch_shapes=[pltpu.VMEM((PAIRS, 3, LANES, QB), F32), pltpu.VMEM((PAIRS, 3, LANES, QB), F32)],
        semantics=("parallel", "arbitrary"), rider=rider)


def _qknorm_bwd(proj, dqn, dkn, gq, gk):
    t = proj.shape[0]
    tm = min(t, 512)
    scale = HEAD_DIM ** -0.5

    def body(q_ref, k_ref, dqn_ref, dkn_ref, gq_ref, gk_ref, o_ref, dgq_ref, dgk_ref):
        e = _head_sum_matrix()

        @pl.when(pl.program_id(0) == 0)
        def _():
            dgq_ref[...] = jnp.zeros_like(dgq_ref)
            dgk_ref[...] = jnp.zeros_like(dgk_ref)

        for n, (src, dn_ref, g_ref, dg_ref, sc) in enumerate(
                ((q_ref, dqn_ref, gq_ref, dgq_ref, scale), (k_ref, dkn_ref, gk_ref, dgk_ref, 1.0))):
            for s in range(D_MODEL // LANES):
                sl = slice(s * LANES, (s + 1) * LANES)
                xf = src[:, sl].astype(F32)
                r = lax.rsqrt(_head_sums(xf * xf, e) * (1.0 / HEAD_DIM) + EPS)
                xh = xf * r
                dn = dn_ref[:, sl].astype(F32) * sc
                dg_ref[:, sl] += _fold8(dn * xh)
                dxh = dn * g_ref[:, sl]
                mean = _head_sums(dxh * xh, e) * (1.0 / HEAD_DIM)
                o_ref[:, n * D_MODEL + s * LANES:n * D_MODEL + (s + 1) * LANES] = (r * (dxh - xh * mean)).astype(BF16)

    row = pl.BlockSpec((tm, D_MODEL), lambda i: (i, 0))
    vec = pl.BlockSpec((1, D_MODEL), lambda i: (0, 0))
    acc = pl.BlockSpec((SUBLANES, D_MODEL), lambda i: (0, 0))
    return pl.pallas_call(
        body, name="qknorm_bwd", grid=(t // tm,),
        in_specs=[row, pl.BlockSpec((tm, D_MODEL), lambda i: (i, 1)), row, row, vec, vec],
        out_specs=[pl.BlockSpec((tm, 2 * D_MODEL), lambda i: (i, 0)), acc, acc],
        out_shape=[jax.ShapeDtypeStruct((t, 2 * D_MODEL), BF16)] + [jax.ShapeDtypeStruct((SUBLANES, D_MODEL), F32)] * 2,
        compiler_params=_params(("arbitrary",)),
    )(proj, proj, dqn, dkn, gq, gk)


def _in_bwd(dqk, dv, dconv, w_in, dhg, x, g1, dx1, rider=None):
    t = x.shape[0]
    tm = min(t, 512)

    def body(dqk_ref, dv_ref, dc_ref, w_ref, dhg_ref, x_ref, g_ref, dx1_ref, dx_ref, dg_ref):
        acc = dhg_ref[...]
        slab = 0
        for src, n in ((dqk_ref, 2), (dv_ref, 1), (dc_ref, 3)):
            for s in range(n):
                acc = acc + _dot_nt(src[:, s * D_MODEL:(s + 1) * D_MODEL],
                                    w_ref[:, slab * D_MODEL:(slab + 1) * D_MODEL])
                slab += 1

        @pl.when(pl.program_id(0) == 0)
        def _():
            dg_ref[...] = jnp.zeros_like(dg_ref)

        dx, dg = _rmsnorm_bwd(x_ref[...], g_ref[...], acc)
        dx_ref[...] = dx1_ref[...] + dx
        dg_ref[...] += _fold8(dg)

    row = pl.BlockSpec((tm, D_MODEL), lambda i: (i, 0))
    return _call(
        body, name="in_bwd", grid=(t // tm,), args=(dqk, dv, dconv, w_in, dhg, x, g1, dx1),
        in_specs=[pl.BlockSpec((tm, 2 * D_MODEL), lambda i: (i, 0)), row,
                  pl.BlockSpec((tm, 3 * D_MODEL), lambda i: (i, 0)),
                  _resident(w_in.shape), row, row, pl.BlockSpec((1, D_MODEL), lambda i: (0, 0)), row],
        out_specs=[row, pl.BlockSpec((SUBLANES, D_MODEL), lambda i: (0, 0))],
        out_shape=[jax.ShapeDtypeStruct((t, D_MODEL), F32), jax.ShapeDtypeStruct((SUBLANES, D_MODEL), F32)],
        semantics=("arbitrary",), rider=rider)


def _bias_grad_fold(dbias, rider=None):
    def body(d_ref, o_ref):
        jj = lax.broadcasted_iota(jnp.int32, (QB, QB), 0)
        ii = lax.broadcasted_iota(jnp.int32, (QB, QB), 1)
        flip = (jj + ii == QB - 1).astype(BF16)
        low = jj + ii < QB
        pos, neg = [], []
        for w in range(KW // QB):
            x = d_ref[0, QB * w:QB * (w + 1), :]
            hi = x.astype(BF16)
            r1 = x - hi.astype(F32)
            mid = r1.astype(BF16)
            lo = (r1 - mid.astype(F32)).astype(BF16)
            xr = _dot(hi, flip) + _dot(mid, flip) + _dot(lo, flip)
            for keep, acc in ((low, pos), (jnp.logical_not(low), neg)):
                part = pltpu.roll(jnp.where(keep, xr, 0.0), 0, 1, stride=1, stride_axis=0)
                acc.append(jnp.sum(part, axis=0, keepdims=True))
        far = pos[1] + neg[0] + pos[0]
        o_ref[0] = jnp.zeros((SUBLANES, QB), F32)
        o_ref[0, 0:1, :] = neg[2]
        o_ref[0, 1:2, :] = pos[2] + neg[1]
        o_ref[0, 2:3, :] = jnp.broadcast_to(jnp.sum(far, axis=-1, keepdims=True), (1, QB))

    return _call(
        body, name="bias_grad_fold", grid=(N_HEADS,), args=(dbias,),
        in_specs=[pl.BlockSpec((1, KW, QB), lambda h: (h, 0, 0))],
        out_specs=[pl.BlockSpec((1, SUBLANES, QB), lambda h: (h, 0, 0))],
        out_shape=[jax.ShapeDtypeStruct((N_HEADS, SUBLANES, QB), F32)],
        semantics=("parallel",), rider=rider)


def _small_partials(dg1, dgq, dgk, dcb, dcw, dbg, dg2, dbias_fold, loss_tile):
    def head_fold(v):
        acc = v[:, 0:LANES]
        for s in range(1, D_MODEL // LANES):
            acc = acc + v[:, s * LANES:(s + 1) * LANES]
        return acc + pltpu.roll(acc, HEAD_DIM, 1)

    def body(dg1_ref, dgq_ref, dgk_ref, dcb_ref, dcw_ref, dbg_ref, dg2_ref, db_ref, loss_ref, o_ref):
        o_ref[...] = jnp.zeros_like(o_ref)
        o_ref[0:1, :] = jnp.sum(dg1_ref[...], axis=0, keepdims=True)
        o_ref[1:2, 0:LANES] = head_fold(jnp.sum(dgq_ref[...], axis=0, keepdims=True))
        o_ref[2:3, 0:LANES] = head_fold(jnp.sum(dgk_ref[...], axis=0, keepdims=True))
        o_ref[3:4, :] = jnp.sum(dcb_ref[...], axis=0, keepdims=True)
        for j in range(3):
            o_ref[4 + j:5 + j, :] = jnp.sum(dcw_ref[j * SUBLANES:(j + 1) * SUBLANES, :], axis=0, keepdims=True)
        o_ref[7:8, :] = jnp.sum(dbg_ref[:, :D_MODEL], axis=0, keepdims=True)
        o_ref[8:9, :] = jnp.sum(dbg_ref[:, D_MODEL:], axis=0, keepdims=True)
        o_ref[9:10, :] = jnp.sum(dg2_ref[...], axis=0, keepdims=True)
        for h in range(N_HEADS):
            for part in range(3):
                o_ref[10 + h:11 + h, part * QB:(part + 1) * QB] = db_ref[h, part:part + 1, :]
        loss = (0.5 / D_MODEL) * jnp.sum(jnp.sum(loss_ref[...], axis=0, keepdims=True), axis=-1, keepdims=True)
        o_ref[26:27, :] = jnp.broadcast_to(loss, (1, D_MODEL))

    return pl.pallas_call(
        body, name="small_partials",
        out_shape=jax.ShapeDtypeStruct((32, D_MODEL), F32),
        compiler_params=_params(),
    )(dg1, dgq, dgk, dcb, dcw, dbg, dg2, dbias_fold, loss_tile)


MID_AXES = (0, 0, 1, 0)
MLP_AXES = (1, 0)


def _local_step(x, target, norm1_g, q_norm_g, k_norm_g, bias, conv_w, conv_b, b_gate, norm2_g,
                w_in, mid_w, mlp_w, distributed):
    g1 = norm1_g.reshape(1, D_MODEL)
    g2 = norm2_g.reshape(1, D_MODEL)
    gq = jnp.tile(q_norm_g, N_HEADS).reshape(1, D_MODEL)
    gk = jnp.tile(k_norm_g, N_HEADS).reshape(1, D_MODEL)
    cb = conv_b.reshape(1, D_MODEL)

    (proj, h), got = _in_proj(x, g1, w_in, rider=_Gather(mid_w, MID_AXES) if distributed else None)
    w_ap, w_cp, w_g, w_out = got if distributed else mid_w
    gates, qn, kn, y_conv = _gates_norms_conv(h, proj, w_g, b_gate.reshape(1, 2 * D_MODEL), gq, gk, conv_w, cb)
    (y_attn, lse), got = _attn_fwd(qn, kn, proj, bias, rider=_Gather(mlp_w, MLP_AXES) if distributed else None)
    w_up, w_down = got if distributed else mlp_w
    ya, yc, merged, x1, h2 = _mix_out(y_attn, y_conv, gates, x, w_ap, w_cp, w_out, g2)
    a, dy, dyb, loss_tile = _mlp_fwd(h2, w_up, w_down, x1, target)

    da, dx1, dx1b, dg2 = _mlp_bwd(dyb, a, w_down, w_up, x1, dy, g2)
    gw_down = _wgrad("wgrad_down", a, [dyb], [1], relu_sq=True, token_block=4096)
    gw_up = _wgrad("wgrad_up", h2, [da], [D_FF // D_MODEL])
    (dgp, dya, dyc, dyattn, dyconv, dhg, dbg), mlp_swapped = _mix_bwd(
        dx1b, gates, ya, yc, w_out, w_ap, w_cp, w_g,
        rider=_PairSwap((gw_up, gw_down), MLP_AXES) if distributed else None)
    mid = tuple(_wgrad_group("wgrad_mid", [(y_attn, dya, 1), (y_conv, dyc, 1), (h, dgp, 2), (merged, dx1b, 1)]))
    gw_ap, gw_cp, gw_g, gw_out = mid
    (dconv, dcb, dcw), mid_swapped = _conv_bwd(
        dyconv, proj, conv_w, cb, rider=_PairSwap(mid, MID_AXES) if distributed else None)
    early = mid + (gw_up, gw_down)
    early_sums = (_pair_add(early, tuple(mid_swapped) + tuple(mlp_swapped), MID_AXES + MLP_AXES)
                  if distributed else None)
    (dqn, dkn, dv, dbias), early_shares = _attn_bwd(
        qn, kn, proj, dyattn, y_attn, lse, bias, rider=_ChipScatter(early_sums) if distributed else None)
    dqk, dgq, dgk = _qknorm_bwd(proj, dqn, dkn, gq, gk)
    gw_in = _wgrad("wgrad_in", h, [dqk, dv, dconv], [2, 1, 3])
    (dbias_fold,), in_swapped = _bias_grad_fold(dbias, rider=_PairSwap((gw_in,), (1,)) if distributed else None)
    in_sums = _pair_add((gw_in,), in_swapped, (1,)) if distributed else None
    (dx, dg1), in_shares = _in_bwd(dqk, dv, dconv, w_in, dhg, x, g1, dx1,
                                   rider=_ChipScatter(in_sums) if distributed else None)
    small = _small_partials(dg1, dgq, dgk, dcb, dcw, dbg, dg2, dbias_fold, loss_tile)
    grads = tuple(in_shares) + tuple(early_shares) if distributed else (gw_in,) + early
    return dx, grads, small


def _me():
    return lax.axis_index("x"), lax.axis_index("y"), lax.axis_index("c")


def _peer(me, rel):
    x, y, c = me
    return (1 - x if rel & 4 else x, 1 - y if rel & 2 else y, 1 - c if rel & 1 else c)


def _linear(dev):
    return 4 * dev[0] + 2 * dev[1] + dev[2]


def _block(ref, axis, idx, size):
    return ref.at[pl.ds(idx * size, size), :] if axis == 0 else ref.at[:, pl.ds(idx * size, size)]


def _cast_shards(shards):
    def body(*refs):
        for src, dst in zip(refs[:len(shards)], refs[len(shards):]):
            dst[...] = src[...].astype(BF16)

    return pl.pallas_call(
        body, name="cast_shards",
        out_shape=[jax.ShapeDtypeStruct(s.shape, BF16) for s in shards],
        compiler_params=_params(),
    )(*shards)


class _Gather:
    def __init__(self, shards, axes):
        self.arrays, self.axes, self.n = list(shards), tuple(axes), len(shards)
        self.sizes = [s.shape[axis] for s, axis in zip(shards, axes)]
        self.out_shape = []
        for s, axis in zip(shards, axes):
            shape = (s.shape[0] * N_DEV, s.shape[1]) if axis == 0 else (s.shape[0], s.shape[1] * N_DEV)
            self.out_shape.append(jax.ShapeDtypeStruct(shape, s.dtype))
        self.scratch = [pltpu.SemaphoreType.DMA((self.n, 7)), pltpu.SemaphoreType.DMA((self.n, 7)),
                        pltpu.SemaphoreType.DMA((self.n,))]

    def _copies(self, srcs, outs, sems):
        send_sems, recv_sems, local_sems = sems
        me = _me()
        sibling = _peer(me, 1)
        chips = [_peer(me, rel) for rel in (4, 2, 6)]

        def rows(a, dev):
            return _block(outs[a], self.axes[a], _linear(dev), self.sizes[a])

        def copy(a, k, block_dev, to, src=None):
            return pltpu.make_async_remote_copy(
                src_ref=rows(a, block_dev) if src is None else src, dst_ref=rows(a, block_dev),
                send_sem=send_sems.at[a, k], recv_sem=recv_sems.at[a, k], device_id=to, device_id_type=MESH_T)

        own = [pltpu.make_async_copy(srcs[a], rows(a, me), local_sems.at[a]) for a in range(self.n)]
        first = []
        for a in range(self.n):
            first.append(copy(a, 0, me, sibling, src=srcs[a]))
            for j, chip in enumerate(chips):
                first.append(copy(a, 1 + j, me, chip, src=srcs[a]))
        return me, sibling, chips, copy, own, first

    def start(self, srcs, outs, sems):
        _, _, _, _, own, first = self._copies(srcs, outs, sems)
        for cp in own + first:
            cp.start()

    def finish(self, srcs, outs, sems):
        me, sibling, chips, copy, own, first = self._copies(srcs, outs, sems)
        passed = []
        for a in range(self.n):
            for j, chip in enumerate(chips):
                copy(a, 1 + j, chip, me).wait_recv()
                fwd = copy(a, 4 + j, chip, sibling)
                fwd.start()
                passed.append(fwd)
        for a in range(self.n):
            copy(a, 0, sibling, me).wait_recv()
            for j, chip in enumerate(chips):
                copy(a, 4 + j, _peer(chip, 1), me).wait_recv()
        for cp in first + passed:
            cp.wait_send()
        for cp in own:
            cp.wait()


N_CHIPS = 4


def _shard_shape(g, axis):
    return (g.shape[0] // N_DEV, g.shape[1]) if axis == 0 else (g.shape[0], g.shape[1] // N_DEV)


class _PairSwap:
    def __init__(self, grads, axes):
        self.arrays, self.axes, self.n = list(grads), tuple(axes), len(grads)
        self.sizes = [g.shape[axis] // N_DEV for g, axis in zip(grads, axes)]
        self.out_shape = [jax.ShapeDtypeStruct((N_CHIPS,) + _shard_shape(g, axis), g.dtype)
                          for g, axis in zip(grads, axes)]
        self.scratch = [pltpu.SemaphoreType.DMA((self.n, N_CHIPS)), pltpu.SemaphoreType.DMA((self.n, N_CHIPS))]

    def _copies(self, srcs, outs, sems):
        send_sems, recv_sems = sems
        x, y, c = _me()
        sibling = (x, y, 1 - c)
        copies = []
        for a in range(self.n):
            for chip in range(N_CHIPS):
                owner_idx = 2 * chip + (1 - c)
                copies.append(pltpu.make_async_remote_copy(
                    src_ref=_block(srcs[a], self.axes[a], owner_idx, self.sizes[a]), dst_ref=outs[a].at[chip],
                    send_sem=send_sems.at[a, chip], recv_sem=recv_sems.at[a, chip],
                    device_id=sibling, device_id_type=MESH_T))
        return copies

    def start(self, srcs, outs, sems):
        for cp in self._copies(srcs, outs, sems):
            cp.start()

    def finish(self, srcs, outs, sems):
        for cp in self._copies(srcs, outs, sems):
            cp.wait()


def _pair_add(grads, swapped, axes):
    n = len(grads)
    c_arr = lax.axis_index("c").astype(jnp.int32).reshape(1)

    def body(c_ref, *refs):
        del c_ref
        mine, got, outs = refs[:n], refs[n:2 * n], refs[2 * n:]
        for a in range(n):
            outs[a][0] = (mine[a][...].astype(F32) + got[a][0].astype(F32)).astype(BF16)

    in_specs, out_specs, out_shape = [], [], []
    for g, axis in zip(grads, axes):
        shard = _shard_shape(g, axis)
        if axis == 0:
            in_specs.append(pl.BlockSpec(shard, lambda s, c_ref: (2 * s + c_ref[0], 0)))
        else:
            in_specs.append(pl.BlockSpec(shard, lambda s, c_ref: (0, 2 * s + c_ref[0])))
    for g, axis in zip(grads, axes):
        shard = _shard_shape(g, axis)
        in_specs.append(pl.BlockSpec((1,) + shard, lambda s, c_ref: (s, 0, 0)))
        out_specs.append(pl.BlockSpec((1,) + shard, lambda s, c_ref: (s, 0, 0)))
        out_shape.append(jax.ShapeDtypeStruct((N_CHIPS,) + shard, BF16))
    return pl.pallas_call(
        body, name="pair_add_" + str(n),
        grid_spec=pltpu.PrefetchScalarGridSpec(num_scalar_prefetch=1, grid=(N_CHIPS,), in_specs=in_specs,
                                               out_specs=out_specs),
        out_shape=out_shape, compiler_params=_params(("arbitrary",)),
    )(c_arr, *grads, *swapped)


class _ChipScatter:
    def __init__(self, sums):
        self.arrays, self.n = list(sums), len(sums)
        self.out_shape = [jax.ShapeDtypeStruct(s.shape, s.dtype) for s in sums]
        self.scratch = [pltpu.SemaphoreType.DMA((self.n, 3)), pltpu.SemaphoreType.DMA((self.n, 3)),
                        pltpu.SemaphoreType.DMA((self.n,))]

    def _copies(self, srcs, outs, sems):
        send_sems, recv_sems, local_sems = sems
        me = _me()
        my_chip = 2 * me[0] + me[1]
        own = [pltpu.make_async_copy(srcs[a].at[my_chip], outs[a].at[my_chip], local_sems.at[a])
               for a in range(self.n)]
        sends, recvs = [], []
        for a in range(self.n):
            for k, rel in enumerate((4, 2, 6)):
                peer = _peer(me, rel)
                peer_chip = 2 * peer[0] + peer[1]
                sends.append(pltpu.make_async_remote_copy(
                    src_ref=srcs[a].at[peer_chip], dst_ref=outs[a].at[my_chip],
                    send_sem=send_sems.at[a, k], recv_sem=recv_sems.at[a, k], device_id=peer, device_id_type=MESH_T))
                recvs.append(pltpu.make_async_remote_copy(
                    src_ref=srcs[a].at[my_chip], dst_ref=outs[a].at[peer_chip],
                    send_sem=send_sems.at[a, k], recv_sem=recv_sems.at[a, k], device_id=peer, device_id_type=MESH_T))
        return own, sends, recvs

    def start(self, srcs, outs, sems):
        own, sends, _ = self._copies(srcs, outs, sems)
        for cp in own + sends:
            cp.start()

    def finish(self, srcs, outs, sems):
        own, sends, recvs = self._copies(srcs, outs, sems)
        for cp in recvs:
            cp.wait_recv()
        for cp in sends:
            cp.wait_send()
        for cp in own:
            cp.wait()


def _call(body, *, name, args, in_specs, out_specs, out_shape, grid=(), scratch_shapes=(), semantics=None,
          rider=None):
    if rider is None:
        return pl.pallas_call(
            body, name=name, grid=grid, in_specs=in_specs, out_specs=out_specs, out_shape=out_shape,
            scratch_shapes=list(scratch_shapes), compiler_params=_params(semantics))(*args), None
    n_in, n_out, n_scr, r = len(in_specs), len(out_specs), len(scratch_shapes), rider.n

    def wrapped(*refs):
        ins, r_ins = refs[:n_in], refs[n_in:n_in + r]
        outs = refs[n_in + r:n_in + r + n_out]
        r_outs = refs[n_in + r + n_out:n_in + 2 * r + n_out]
        scr = refs[n_in + 2 * r + n_out:n_in + 2 * r + n_out + n_scr]
        sems = refs[n_in + 2 * r + n_out + n_scr:]
        first, last = None, None
        for ax in range(len(grid)):
            f, l = pl.program_id(ax) == 0, pl.program_id(ax) == pl.num_programs(ax) - 1
            first = f if first is None else first & f
            last = l if last is None else last & l
        if first is None:
            rider.start(r_ins, r_outs, sems)
            body(*ins, *outs, *scr)
            rider.finish(r_ins, r_outs, sems)
            return

        @pl.when(first)
        def _():
            rider.start(r_ins, r_outs, sems)

        body(*ins, *outs, *scr)

        @pl.when(last)
        def _():
            rider.finish(r_ins, r_outs, sems)

    any_spec = pl.BlockSpec(memory_space=pl.ANY)
    out = pl.pallas_call(
        wrapped, name=name, grid=grid, in_specs=list(in_specs) + [any_spec] * r,
        out_specs=list(out_specs) + [any_spec] * r, out_shape=list(out_shape) + rider.out_shape,
        scratch_shapes=list(scratch_shapes) + rider.scratch,
        compiler_params=_params(None if semantics is None else ("arbitrary",) * len(semantics)),
    )(*args, *rider.arrays)
    return out[:n_out], out[n_out:]


def _all_reduce_small(part):
    def body(p_ref, o_ref, slots, send_sems, recv_sems):
        me = _me()
        my_idx = _linear(me)
        slots[my_idx] = p_ref[...]
        sends = []
        for rel in range(1, N_DEV):
            cp = pltpu.make_async_remote_copy(
                src_ref=p_ref, dst_ref=slots.at[my_idx], send_sem=send_sems.at[rel - 1],
                recv_sem=recv_sems.at[rel - 1], device_id=_peer(me, rel), device_id_type=MESH_T)
            cp.start()
            sends.append(cp)
        for rel in range(1, N_DEV):
            frm = _peer(me, rel)
            pltpu.make_async_remote_copy(
                src_ref=p_ref, dst_ref=slots.at[_linear(frm)], send_sem=send_sems.at[rel - 1],
                recv_sem=recv_sems.at[rel - 1], device_id=frm, device_id_type=MESH_T).wait_recv()
        for cp in sends:
            cp.wait_send()
        total = slots[0]
        for d in range(1, N_DEV):
            total = total + slots[d]
        o_ref[...] = total

    return pl.pallas_call(
        body, name="all_reduce_small",
        in_specs=[pl.BlockSpec(memory_space=pltpu.VMEM)], out_specs=pl.BlockSpec(memory_space=pltpu.VMEM),
        out_shape=jax.ShapeDtypeStruct(part.shape, F32),
        scratch_shapes=[pltpu.VMEM((N_DEV,) + part.shape, F32), pltpu.SemaphoreType.DMA((7,)),
                        pltpu.SemaphoreType.DMA((7,))],
        compiler_params=_params(),
    )(part)


def _adamw_math(w, g, m, v):
    m = ADAM_B1 * m + (1.0 - ADAM_B1) * g
    v = ADAM_B2 * v + (1.0 - ADAM_B2) * jnp.square(g)
    m_hat = m / (1.0 - ADAM_B1 ** ADAM_STEP)
    v_hat = v / (1.0 - ADAM_B2 ** ADAM_STEP)
    delta = -ADAM_LR * (m_hat / (jnp.sqrt(v_hat) + ADAM_EPS) + ADAM_WD * w)
    return delta, m, v


ADAMW_STEPS = 4


def _adamw_big(shares, ws, ms, vs, rider=None):
    n = len(ws)

    def body(*refs):
        s_refs, w_refs, m_refs, v_refs = (refs[a * n:(a + 1) * n] for a in range(4))
        outs = refs[4 * n:]
        for a in range(n):
            g = s_refs[a][0].astype(F32)
            for d in range(1, N_CHIPS):
                g = g + s_refs[a][d].astype(F32)
            outs[4 * a][...] = g
            outs[4 * a + 1][...], outs[4 * a + 2][...], outs[4 * a + 3][...] = _adamw_math(
                w_refs[a][...], g, m_refs[a][...], v_refs[a][...])

    def chunk(w):
        return pl.BlockSpec((w.shape[0] // ADAMW_STEPS, w.shape[1]), lambda i: (i, 0))

    def share_chunk(w):
        return pl.BlockSpec((N_CHIPS, w.shape[0] // ADAMW_STEPS, w.shape[1]), lambda i: (0, i, 0))

    out, rider_out = _call(
        body, name="adamw_big", grid=(ADAMW_STEPS,), args=(*shares, *ws, *ms, *vs),
        in_specs=[share_chunk(w) for w in ws] + [chunk(w) for w in ws] * 3,
        out_specs=[chunk(w) for w in ws for _ in range(4)],
        out_shape=[jax.ShapeDtypeStruct(w.shape, F32) for w in ws for _ in range(4)],
        semantics=("parallel",), rider=rider)
    return [tuple(out[4 * a:4 * a + 4]) for a in range(n)], rider_out


def _adamw_small(quads):
    n = len(quads)

    def body(*refs):
        ins, outs = refs[:4 * n], refs[4 * n:]
        for p in range(n):
            g_ref, w_ref, m_ref, v_ref = ins[4 * p:4 * p + 4]
            d_ref, nm_ref, nv_ref = outs[3 * p:3 * p + 3]
            d_ref[...], nm_ref[...], nv_ref[...] = _adamw_math(w_ref[...], g_ref[...], m_ref[...], v_ref[...])

    flat = [a for quad in quads for a in quad]
    out = pl.pallas_call(
        body, name="adamw_small",
        out_shape=[jax.ShapeDtypeStruct(quad[1].shape, F32) for quad in quads for _ in range(3)],
        compiler_params=_params(),
    )(*flat)
    return [tuple(out[3 * p:3 * p + 3]) for p in range(n)]


def kernel(x, norm1_g, w_in, q_norm_g, k_norm_g, rel_bias, conv_w, conv_b, w_attn_proj, w_conv_proj, w_gate, b_gate, w_out, norm2_g, w_up, w_down, loss_target, m_norm1_g, m_w_in, m_q_norm_g, m_k_norm_g, m_rel_bias, m_conv_w, m_conv_b, m_w_attn_proj, m_w_conv_proj, m_w_gate, m_b_gate, m_w_out, m_norm2_g, m_w_up, m_w_down, v_norm1_g, v_w_in, v_q_norm_g, v_k_norm_g, v_rel_bias, v_conv_w, v_conv_b, v_w_attn_proj, v_w_conv_proj, v_w_gate, v_b_gate, v_w_out, v_norm2_g, v_w_up, v_w_down):
    my_idx = _linear(_me())
    big_w = (w_in, w_attn_proj, w_conv_proj, w_gate, w_out, w_up, w_down)
    big_m = (m_w_in, m_w_attn_proj, m_w_conv_proj, m_w_gate, m_w_out, m_w_up, m_w_down)
    big_v = (v_w_in, v_w_attn_proj, v_w_conv_proj, v_w_gate, v_w_out, v_w_up, v_w_down)
    big_names = ("w_in", "w_attn_proj", "w_conv_proj", "w_gate", "w_out", "w_up", "w_down")

    conv_w_tile = jnp.pad(conv_w, ((0, SUBLANES - conv_w.shape[0]), (0, 0)))
    shards = _cast_shards(big_w)
    (bias,), (w_in_full, conv_w_rows) = _bias_tiles(rel_bias, rider=_Gather((shards[0], conv_w_tile), (1, 1)))

    dx, shares, small = _local_step(x[0], loss_target[0], norm1_g, q_norm_g, k_norm_g, bias, conv_w_rows[:3],
                                    conv_b, b_gate, norm2_g, w_in_full, tuple(shards[1:5]), tuple(shards[5:7]), True)

    big_out, _ = _adamw_big(shares, big_w, big_m, big_v)
    tot = _all_reduce_small(small)
    g_rel_bias = jnp.concatenate(
        [tot[10:26, :QB][:, ::-1], tot[10:26, QB:2 * QB][:, ::-1], tot[10:26, 2 * QB:2 * QB + 1]], axis=1)
    g_conv_w = lax.dynamic_slice(tot[4:7], (0, my_idx * LANES), (3, LANES))
    small_g = [tot[0:1], tot[1:2, :HEAD_DIM], tot[2:3, :HEAD_DIM], g_rel_bias, g_conv_w, tot[3:4],
               tot[7:9].reshape(1, 2 * D_MODEL), tot[9:10]]
    small_w = (norm1_g, q_norm_g, k_norm_g, rel_bias, conv_w, conv_b, b_gate, norm2_g)
    small_m = (m_norm1_g, m_q_norm_g, m_k_norm_g, m_rel_bias, m_conv_w, m_conv_b, m_b_gate, m_norm2_g)
    small_v = (v_norm1_g, v_q_norm_g, v_k_norm_g, v_rel_bias, v_conv_w, v_conv_b, v_b_gate, v_norm2_g)

    def two_d(a):
        return a.reshape(1, -1) if a.ndim == 1 else a

    small_out = _adamw_small([(g, two_d(w), two_d(m), two_d(v))
                              for g, w, m, v in zip(small_g, small_w, small_m, small_v)])

    order = ("norm1_g", "w_in", "q_norm_g", "k_norm_g", "rel_bias", "conv_w", "conv_b", "w_attn_proj", "w_conv_proj",
             "w_gate", "b_gate", "w_out", "norm2_g", "w_up", "w_down")
    small_names = ("norm1_g", "q_norm_g", "k_norm_g", "rel_bias", "conv_w", "conv_b", "b_gate", "norm2_g")
    res = {}
    for name, (g, d, nm, nv) in zip(big_names, big_out):
        res[name] = (g, d, nm, nv)
    for name, g, w, (d, nm, nv) in zip(small_names, small_g, small_w, small_out):
        res[name] = tuple(a.reshape(w.shape) for a in (g, d, nm, nv))
    loss = tot[26, 0]
    return (loss, dx[None], *[res[n][0] for n in order], *[res[n][1] for n in order],
            *[res[n][2] for n in order], *[res[n][3] for n in order])
```

```python
import functools

import jax
import jax.numpy as jnp
from jax import lax
from jax.experimental import pallas as pl
from jax.experimental.pallas import tpu as pltpu

F32 = jnp.float32
BF16 = jnp.bfloat16

D_MODEL = 1024
N_HEADS = 16
HEAD_DIM = 64
CHUNK = 64
N_PREV_CHUNKS = 8
MAX_REL = 256
D_FF = 4096
EPS = 1e-6
NEG_INF = -1e30
LOG2E = 1.4426950408889634
N_DEV = 8

ADAM_LR = 0.001
ADAM_B1 = 0.9
ADAM_B2 = 0.999
ADAM_EPS = 1e-08
ADAM_WD = 0.01
ADAM_STEP = 10

LANES = 128
SUBLANES = 8
VMEM_LIMIT = 56 * 1024 * 1024
QB = 256
KW = 3 * QB
PAIRS = 4
SLAB = PAIRS * LANES
PAIRS_FWD = 8
SKEW = 1024

MESH_T = pl.DeviceIdType.MESH


def _dot(a, b):
    return jnp.dot(a, b, preferred_element_type=F32)


def _dot_nt(a, b):
    return lax.dot_general(a, b, (((1,), (1,)), ((), ())), preferred_element_type=F32)


def _dot_tn(a, b):
    return lax.dot_general(a, b, (((0,), (0,)), ((), ())), preferred_element_type=F32)


def _params(sem=None):
    return pltpu.CompilerParams(dimension_semantics=sem, vmem_limit_bytes=VMEM_LIMIT)


def _resident(shape):
    return pl.BlockSpec(shape, lambda *_: (0,) * len(shape), pipeline_mode=pl.Buffered(1))


def _fold8(v):
    rows, n = v.shape
    return v.reshape(rows // SUBLANES, SUBLANES, n).sum(axis=0)


def _head_sum_matrix():
    r = lax.broadcasted_iota(jnp.int32, (LANES, LANES), 0) // HEAD_DIM
    c = lax.broadcasted_iota(jnp.int32, (LANES, LANES), 1) // HEAD_DIM
    return (r == c).astype(BF16)


def _head_sums(v, e):
    hi = v.astype(BF16)
    lo = (v - hi.astype(F32)).astype(BF16)
    return _dot(hi, e) + _dot(lo, e)


def _in_proj(h, w_in, rider=None):
    t = h.shape[0]
    tm = min(t, 512)
    n_out = w_in.shape[1]

    def body(h_ref, w_ref, proj_ref):
        h = h_ref[...]
        for k in range(n_out // D_MODEL):
            cols = slice(k * D_MODEL, (k + 1) * D_MODEL)
            proj_ref[:, cols] = _dot(h, w_ref[:, cols]).astype(BF16)

    return _call(
        body, name="in_proj", grid=(t // tm,), args=(h, w_in),
        in_specs=[pl.BlockSpec((tm, D_MODEL), lambda i: (i, 0)), _resident((D_MODEL, n_out))],
        out_specs=[pl.BlockSpec((tm, n_out), lambda i: (i, 0))],
        out_shape=[jax.ShapeDtypeStruct((t, n_out), BF16)],
        semantics=("parallel",), rider=rider)


def _bias_tiles(rel_bias, x, g1, later_shards=(), rider=None):
    t = x.shape[0]
    rows = t // N_HEADS
    n_later = len(later_shards)
    by_dist = jnp.concatenate(
        [rel_bias[:, :2 * MAX_REL], jnp.broadcast_to(rel_bias[:, 2 * MAX_REL:], (N_HEADS, 2 * MAX_REL))], axis=1)
    by_dist = by_dist.reshape(N_HEADS, 1, SKEW)

    def body(f_ref, x_ref, g_ref, *refs):
        src_refs, o_ref, h_ref, dst_refs = refs[:n_later], refs[n_later], refs[n_later + 1], refs[n_later + 2:]

        @pl.when(pl.program_id(0) == 0)
        def _():
            for src, dst in zip(src_refs, dst_refs):
                dst[...] = src[...].astype(BF16)

        xf = x_ref[...]
        r = lax.rsqrt(jnp.mean(xf * xf, axis=-1, keepdims=True) + EPS)
        h_ref[...] = (xf * r * g_ref[...]).astype(BF16)
        jj = lax.broadcasted_iota(jnp.int32, (QB, QB), 0)
        ii = lax.broadcasted_iota(jnp.int32, (QB, QB), 1)
        for w in range(KW // QB):
            pos = jnp.broadcast_to(f_ref[0, :, KW - QB * w:KW - QB * w + QB], (QB, QB))
            neg = jnp.broadcast_to(f_ref[0, :, KW - QB * (w + 1):KW - QB * w], (QB, QB))
            pos = pltpu.roll(pos, 0, 1, stride=1, stride_axis=0)
            neg = pltpu.roll(neg, 0, 1, stride=1, stride_axis=0)
            tile = jnp.where(ii >= jj, pos, neg)
            kc = (jj + QB * w) // CHUNK
            qc = ii // CHUNK
            band = (kc >= qc) & (kc <= qc + N_PREV_CHUNKS)
            o_ref[0, QB * w:QB * (w + 1), :] = jnp.where(band, tile * LOG2E, NEG_INF)

    def whole(s):
        return pl.BlockSpec(s.shape, lambda h: (0, 0))

    return _call(
        body, name="bias_tiles", grid=(N_HEADS,), args=(by_dist, x, g1, *later_shards),
        in_specs=[pl.BlockSpec((1, 1, SKEW), lambda h: (h, 0, 0)),
                  pl.BlockSpec((rows, D_MODEL), lambda h: (h, 0)),
                  pl.BlockSpec((1, D_MODEL), lambda h: (0, 0))] + [whole(s) for s in later_shards],
        out_specs=[pl.BlockSpec((1, KW, QB), lambda h: (h, 0, 0)),
                   pl.BlockSpec((rows, D_MODEL), lambda h: (h, 0))] + [whole(s) for s in later_shards],
        out_shape=[jax.ShapeDtypeStruct((N_HEADS, KW, QB), F32), jax.ShapeDtypeStruct((t, D_MODEL), BF16)]
        + [jax.ShapeDtypeStruct(s.shape, BF16) for s in later_shards],
        semantics=("parallel",), rider=rider)


def _window_specs(col0, slab):
    return [pl.BlockSpec((QB, slab), functools.partial(
        lambda p, b, back: (jnp.maximum(b - back, 0), col0 + p), back=back)) for back in (2, 1, 0)]


def _attn_fwd(qn, kn, proj, bias, rider=None):
    t = qn.shape[0]
    nb = t // QB
    pairs = PAIRS_FWD
    slab = pairs * LANES
    v_col0 = 2 * D_MODEL // slab

    def body(q_ref, k0, k1, k2, v0, v1, v2, bias_ref, o_ref, lse_ref):
        b = pl.program_id(1)

        @pl.when(b < 2)
        def _():
            compute(q_ref, k0, k1, k2, v0, v1, v2, bias_ref, o_ref, lse_ref,
                    lax.broadcasted_iota(jnp.int32, (KW, 1), 0) >= (2 - b) * QB)

        @pl.when(b >= 2)
        def _():
            compute(q_ref, k0, k1, k2, v0, v1, v2, bias_ref, o_ref, lse_ref, None)

    def compute(q_ref, k0, k1, k2, v0, v1, v2, bias_ref, o_ref, lse_ref, valid):
        head_a = lax.broadcasted_iota(jnp.int32, (1, LANES), 1) < HEAD_DIM

        def scores(head):
            hp, hh = divmod(head, 2)
            sl = slice(hp * LANES, (hp + 1) * LANES)
            k = jnp.concatenate([k0[:, sl], k1[:, sl], k2[:, sl]], axis=0)
            mine = head_a if hh == 0 else jnp.logical_not(head_a)
            s = _dot_nt(jnp.where(mine, k, jnp.zeros_like(k)), q_ref[:, sl]) + bias_ref[head]
            return s if valid is None else jnp.where(valid, s, NEG_INF)

        def weighted_values(head, s):
            hp, hh = divmod(head, 2)
            sl = slice(hp * LANES, (hp + 1) * LANES)
            v = jnp.concatenate([v0[:, sl], v1[:, sl], v2[:, sl]], axis=0)
            vt = v.astype(F32).T.astype(BF16)[hh * HEAD_DIM:(hh + 1) * HEAD_DIM]
            vt = jnp.concatenate([vt, jnp.ones((SUBLANES, KW), BF16)], axis=0)
            m = jnp.max(s, axis=0, keepdims=True)
            ov = _dot(vt, jnp.exp2(s - m).astype(BF16))
            l = ov[HEAD_DIM:HEAD_DIM + 1]
            return ov[:HEAD_DIM] / l, m + jnp.log2(l)

        outs, lses = [], []
        pending = scores(0)
        for head in range(2 * pairs):
            nxt = scores(head + 1) if head + 1 < 2 * pairs else None
            o, lse = weighted_values(head, pending)
            outs.append(o)
            lses.append(lse)
            pending = nxt
        for hp in range(pairs):
            sl = slice(hp * LANES, (hp + 1) * LANES)
            o_ref[:, sl] = jnp.concatenate([outs[2 * hp], outs[2 * hp + 1]], axis=0).T.astype(BF16)
        lse_ref[...] = jnp.concatenate(lses, axis=0)

    return _call(
        body, name="attn_fwd", grid=(D_MODEL // slab, nb), args=(qn, kn, kn, kn, proj, proj, proj, bias),
        in_specs=[pl.BlockSpec((QB, slab), lambda p, b: (b, p))] + _window_specs(0, slab)
        + _window_specs(v_col0, slab) + [pl.BlockSpec((2 * pairs, KW, QB), lambda p, b: (p, 0, 0))],
        out_specs=[pl.BlockSpec((QB, slab), lambda p, b: (b, p)),
                   pl.BlockSpec((2 * pairs, QB), lambda p, b: (p, b))],
        out_shape=[jax.ShapeDtypeStruct((t, D_MODEL), BF16), jax.ShapeDtypeStruct((N_HEADS, t), F32)],
        semantics=("parallel", "arbitrary"), rider=rider)


def _shift_down(u, halo, n):
    rows = lax.broadcasted_iota(jnp.int32, (u.shape[0], 1), 0)
    out = pltpu.roll(u, n, 0)
    for j in range(n):
        out = jnp.where(rows == j, halo[SUBLANES - n + j:SUBLANES - n + j + 1, :], out)
    return out


def _shift_up(u, halo, n):
    tm = u.shape[0]
    rows = lax.broadcasted_iota(jnp.int32, (tm, 1), 0)
    out = pltpu.roll(u, tm - n, 0)
    for j in range(n):
        out = jnp.where(rows == tm - n + j, halo[j:j + 1, :], out)
    return out


def _gates_norms_conv(h, proj, w_g, b_g, gq, gk, conv_w, conv_b):
    t = h.shape[0]
    tm = min(t, 512)
    hb = tm // SUBLANES
    scale = HEAD_DIM ** -0.5 * LOG2E

    def body(h_ref, wg_ref, bgate_ref, q_ref, k_ref, gq_ref, gk_ref, bg_ref, cg_ref, xc_ref, cgh_ref, xch_ref,
             cw_ref, cb_ref, gates_ref, qn_ref, kn_ref, yc_ref):
        i = pl.program_id(0)
        hv = h_ref[...]
        e = _head_sum_matrix()

        def gate_cols(j, width):
            cols = slice(j * width, (j + 1) * width)
            gates_ref[:, cols] = jax.nn.sigmoid(_dot(hv, wg_ref[:, cols]) + bgate_ref[:, cols]).astype(BF16)

        def head_norm(src, g_ref, dst, sc, s):
            sl = slice(s * LANES, (s + 1) * LANES)
            xf = src[:, sl].astype(F32)
            r = lax.rsqrt(_head_sums(xf * xf, e) * (1.0 / HEAD_DIM) + EPS)
            dst[:, sl] = (xf * r * g_ref[:, sl] * sc).astype(BF16)

        def conv_cols(c, width):
            cols = slice(c * width, (c + 1) * width)
            u = cg_ref[:, cols].astype(F32) * xc_ref[:, cols].astype(F32)
            halo = jnp.where(i > 0, cgh_ref[:, cols].astype(F32) * xch_ref[:, cols].astype(F32), 0.0)
            w = cw_ref[:, cols]
            s = w[0:1] * _shift_down(u, halo, 2) + w[1:2] * _shift_down(u, halo, 1) + w[2:3] * u
            yc_ref[:, cols] = (bg_ref[:, cols].astype(F32) * (cb_ref[:, cols] + s)).astype(BF16)

        n_norm = D_MODEL // LANES
        for j in range(n_norm):
            gate_cols(j, 2 * D_MODEL // n_norm)
            head_norm(q_ref, gq_ref, qn_ref, scale, j)
            head_norm(k_ref, gk_ref, kn_ref, 1.0, j)
            if j % 2 == 1:
                conv_cols(j // 2, 2 * D_MODEL // n_norm)

    def slab(col):
        return pl.BlockSpec((tm, D_MODEL), lambda i: (i, col))

    def prev(col):
        return pl.BlockSpec((SUBLANES, D_MODEL), lambda i: (jnp.maximum(i * hb - 1, 0), col))

    vec = pl.BlockSpec((1, D_MODEL), lambda i: (0, 0))
    row = pl.BlockSpec((tm, D_MODEL), lambda i: (i, 0))
    return pl.pallas_call(
        body, name="gates_norms_conv", grid=(t // tm,),
        in_specs=[row, _resident(w_g.shape), pl.BlockSpec((1, 2 * D_MODEL), lambda i: (0, 0)),
                  slab(0), slab(1), vec, vec, slab(3), slab(4), slab(5), prev(4), prev(5),
                  pl.BlockSpec((3, D_MODEL), lambda i: (0, 0)), vec],
        out_specs=[pl.BlockSpec((tm, 2 * D_MODEL), lambda i: (i, 0)), row, row, row],
        out_shape=[jax.ShapeDtypeStruct((t, 2 * D_MODEL), BF16)] + [jax.ShapeDtypeStruct((t, D_MODEL), BF16)] * 3,
        compiler_params=_params(("parallel",)),
    )(h, w_g, b_g, proj, proj, gq, gk, proj, proj, proj, proj, proj, conv_w, conv_b)


def _mix_out(y_attn, y_conv, gates, x, w_ap, w_cp, w_out, g2):
    t = x.shape[0]
    tm = min(t, 512)

    def body(ya_in, yc_in, g_ref, x_ref, wap, wcp, wout, g2_ref, ya_ref, yc_ref, mg_ref, x1_ref, h2_ref):
        ya = _dot(ya_in[...], wap[...])
        yc = _dot(yc_in[...], wcp[...])
        ya_ref[...] = ya.astype(BF16)
        yc_ref[...] = yc.astype(BF16)
        merged = (g_ref[:, :D_MODEL].astype(F32) * ya + g_ref[:, D_MODEL:].astype(F32) * yc).astype(BF16)
        mg_ref[...] = merged
        x1 = x_ref[...] + _dot(merged, wout[...])
        x1_ref[...] = x1
        r = lax.rsqrt(jnp.mean(x1 * x1, axis=-1, keepdims=True) + EPS)
        h2_ref[...] = (x1 * r * g2_ref[...]).astype(BF16)

    row = pl.BlockSpec((tm, D_MODEL), lambda i: (i, 0))
    full = _resident((D_MODEL, D_MODEL))
    return pl.pallas_call(
        body, name="mix_out", grid=(t // tm,),
        in_specs=[row, row, pl.BlockSpec((tm, 2 * D_MODEL), lambda i: (i, 0)), row, full, full, full,
                  pl.BlockSpec((1, D_MODEL), lambda i: (0, 0))],
        out_specs=[row] * 5,
        out_shape=[jax.ShapeDtypeStruct((t, D_MODEL), BF16)] * 3
        + [jax.ShapeDtypeStruct((t, D_MODEL), F32), jax.ShapeDtypeStruct((t, D_MODEL), BF16)],
        compiler_params=_params(("parallel",)),
    )(y_attn, y_conv, gates, x, w_ap, w_cp, w_out, g2)


def _mlp_fwd(h2, w_up, w_down, x1, target):
    t = h2.shape[0]
    tm = min(t, 512)
    tf = 1024
    nf = D_FF // tf

    def body(h2_ref, wup, wdn, x1_ref, tg_ref, a_ref, dy_ref, dyb_ref, loss_ref):
        h2v = h2_ref[...]
        acc = None
        pending = _dot(h2v, wup[:, 0:tf])
        for j in range(nf):
            cols = slice(j * tf, (j + 1) * tf)
            a = pending
            if j + 1 < nf:
                pending = _dot(h2v, wup[:, (j + 1) * tf:(j + 2) * tf])
            a_ref[:, cols] = a.astype(BF16)
            part = _dot(jnp.square(jnp.maximum(a, 0.0)).astype(BF16), wdn[cols, :])
            acc = part if acc is None else acc + part

        @pl.when(pl.program_id(0) == 0)
        def _():
            loss_ref[...] = jnp.zeros_like(loss_ref)

        diff = x1_ref[...] + acc - tg_ref[...]
        loss_ref[...] += _fold8(diff * diff)
        dy = diff * (1.0 / D_MODEL)
        dy_ref[...] = dy
        dyb_ref[...] = dy.astype(BF16)

    row = pl.BlockSpec((tm, D_MODEL), lambda i: (i, 0))
    return pl.pallas_call(
        body, name="mlp_fwd", grid=(t // tm,),
        in_specs=[row, _resident((D_MODEL, D_FF)), _resident((D_FF, D_MODEL)), row, row],
        out_specs=[pl.BlockSpec((tm, D_FF), lambda i: (i, 0)), row, row,
                   pl.BlockSpec((SUBLANES, D_MODEL), lambda i: (0, 0))],
        out_shape=[jax.ShapeDtypeStruct((t, D_FF), BF16), jax.ShapeDtypeStruct((t, D_MODEL), F32),
                   jax.ShapeDtypeStruct((t, D_MODEL), BF16), jax.ShapeDtypeStruct((SUBLANES, D_MODEL), F32)],
        compiler_params=_params(("arbitrary",)),
    )(h2, w_up, w_down, x1, target)


def _rmsnorm_bwd(xf, g, dh):
    r = lax.rsqrt(jnp.mean(xf * xf, axis=-1, keepdims=True) + EPS)
    xh = xf * r
    dxh = dh * g
    dx = r * (dxh - xh * jnp.mean(dxh * xh, axis=-1, keepdims=True))
    return dx, dh * xh


def _mlp_bwd(dyb, a, w_down, w_up, x1, dy, g2):
    t = dyb.shape[0]
    tm = min(t, 512)
    tf = 1024
    nf = D_FF // tf

    def body(dyb_ref, a_ref, wdn, wup, x1_ref, dy_ref, g2_ref, da_ref, dx1_ref, dx1b_ref, dg2_ref):
        dyv = dyb_ref[...]
        acc = None
        pending = _dot_nt(dyv, wdn[0:tf, :])
        for j in range(nf):
            cols = slice(j * tf, (j + 1) * tf)
            du = pending
            if j + 1 < nf:
                pending = _dot_nt(dyv, wdn[(j + 1) * tf:(j + 2) * tf, :])
            da = (du * (2.0 * jnp.maximum(a_ref[:, cols].astype(F32), 0.0))).astype(BF16)
            da_ref[:, cols] = da
            part = _dot_nt(da, wup[:, cols])
            acc = part if acc is None else acc + part

        @pl.when(pl.program_id(0) == 0)
        def _():
            dg2_ref[...] = jnp.zeros_like(dg2_ref)

        dx, dg = _rmsnorm_bwd(x1_ref[...], g2_ref[...], acc)
        dx1 = dy_ref[...] + dx
        dx1_ref[...] = dx1
        dx1b_ref[...] = dx1.astype(BF16)
        dg2_ref[...] += _fold8(dg)

    row = pl.BlockSpec((tm, D_MODEL), lambda i: (i, 0))
    wide = pl.BlockSpec((tm, D_FF), lambda i: (i, 0))
    return pl.pallas_call(
        body, name="mlp_bwd", grid=(t // tm,),
        in_specs=[row, wide, _resident((D_FF, D_MODEL)), _resident((D_MODEL, D_FF)), row, row,
                  pl.BlockSpec((1, D_MODEL), lambda i: (0, 0))],
        out_specs=[wide, row, row, pl.BlockSpec((SUBLANES, D_MODEL), lambda i: (0, 0))],
        out_shape=[jax.ShapeDtypeStruct((t, D_FF), BF16), jax.ShapeDtypeStruct((t, D_MODEL), F32),
                   jax.ShapeDtypeStruct((t, D_MODEL), BF16), jax.ShapeDtypeStruct((SUBLANES, D_MODEL), F32)],
        compiler_params=_params(("arbitrary",)),
    )(dyb, a, w_down, w_up, x1, dy, g2)


def _wgrad(name, lhs, rhs_list, rhs_slabs, relu_sq=False, token_block=2048):
    t, m = lhs.shape
    tt = min(t, token_block)
    tmo = min(m, 1024)
    n_slab = sum(rhs_slabs)
    starts = [sum(rhs_slabs[:n]) for n in range(len(rhs_slabs))]
    n_rhs = len(rhs_list)

    def body(*refs):
        l_ref, r_refs, o_ref, acc = refs[0], refs[1:1 + n_rhs], refs[1 + n_rhs], refs[2 + n_rhs]
        k, s = pl.program_id(1), pl.program_id(2)
        lv = l_ref[...]
        if relu_sq:
            lv = jnp.square(jnp.maximum(lv.astype(F32), 0.0)).astype(BF16)

        @pl.when(s == 0)
        def _():
            acc[...] = jnp.zeros_like(acc)

        for n in range(n_rhs):
            @pl.when((k >= starts[n]) & (k < starts[n] + rhs_slabs[n]))
            def _(n=n):
                acc[...] += _dot_tn(lv, r_refs[n][...])

        @pl.when(s == pl.num_programs(2) - 1)
        def _():
            o_ref[...] = acc[...].astype(BF16)

    def rhs_spec(n):
        lo, cnt = starts[n], rhs_slabs[n]

        def index(i, k, s):
            inside = (k >= lo) & (k < lo + cnt)
            return (jnp.where(inside, s, 0), jnp.clip(k - lo, 0, cnt - 1))
        return pl.BlockSpec((tt, D_MODEL), index)

    return pl.pallas_call(
        body, name=name, grid=(m // tmo, n_slab, t // tt),
        in_specs=[pl.BlockSpec((tt, tmo), lambda i, k, s: (s, i))] + [rhs_spec(n) for n in range(n_rhs)],
        out_specs=pl.BlockSpec((tmo, D_MODEL), lambda i, k, s: (i, k)),
        out_shape=jax.ShapeDtypeStruct((m, n_slab * D_MODEL), BF16),
        scratch_shapes=[pltpu.VMEM((tmo, D_MODEL), F32)],
        compiler_params=_params(("parallel", "parallel", "arbitrary")),
    )(lhs, *rhs_list)


def _wgrad_group(name, triples):
    t = triples[0][0].shape[0]
    tt = min(t, 1024)
    counts = [n for _, _, n in triples]
    starts = [sum(counts[:n]) for n in range(len(counts))]
    n_prod = len(triples)

    def inside(n, k):
        return (k >= starts[n]) & (k < starts[n] + counts[n])

    def body(*refs):
        l_refs, r_refs, o_refs = refs[:n_prod], refs[n_prod:2 * n_prod], refs[2 * n_prod:3 * n_prod]
        acc = refs[3 * n_prod]
        k, s = pl.program_id(0), pl.program_id(1)

        @pl.when(s == 0)
        def _():
            acc[...] = jnp.zeros_like(acc)

        for n in range(n_prod):
            @pl.when(inside(n, k))
            def _(n=n):
                acc[...] += _dot_tn(l_refs[n][...], r_refs[n][...])

            @pl.when(inside(n, k) & (s == pl.num_programs(1) - 1))
            def _(n=n):
                o_refs[n][...] = acc[...].astype(BF16)

    def lhs_spec(n):
        return pl.BlockSpec((tt, D_MODEL), lambda k, s: (jnp.where(inside(n, k), s, 0), 0))

    def rhs_spec(n):
        return pl.BlockSpec((tt, D_MODEL), lambda k, s: (jnp.where(inside(n, k), s, 0),
                                                         jnp.clip(k - starts[n], 0, counts[n] - 1)))

    def out_spec(n):
        return pl.BlockSpec((D_MODEL, D_MODEL), lambda k, s: (0, jnp.clip(k - starts[n], 0, counts[n] - 1)))

    return pl.pallas_call(
        body, name=name, grid=(sum(counts), t // tt),
        in_specs=[lhs_spec(n) for n in range(n_prod)] + [rhs_spec(n) for n in range(n_prod)],
        out_specs=[out_spec(n) for n in range(n_prod)],
        out_shape=[jax.ShapeDtypeStruct((D_MODEL, n * D_MODEL), BF16) for n in counts],
        scratch_shapes=[pltpu.VMEM((D_MODEL, D_MODEL), F32)],
        compiler_params=_params(("arbitrary", "arbitrary")),
    )(*[tr[0] for tr in triples], *[tr[1] for tr in triples])


def _mix_bwd(dx1b, gates, ya, yc, w_out, w_ap, w_cp, w_g, rider=None):
    t = dx1b.shape[0]
    tm = min(t, 512)

    def body(dx_ref, g_ref, ya_ref, yc_ref, wout, wap, wcp, wg,
             dgp_ref, dya_ref, dyc_ref, dyat_ref, dycv_ref, dhg_ref, dbg_ref):
        dm = _dot_nt(dx_ref[...], wout[...])
        ga = g_ref[:, :D_MODEL].astype(F32)
        gc = g_ref[:, D_MODEL:].astype(F32)
        dya = (dm * ga).astype(BF16)
        dyc = (dm * gc).astype(BF16)
        dya_ref[...] = dya
        dyc_ref[...] = dyc
        dgpa = dm * ya_ref[...].astype(F32) * ga * (1.0 - ga)
        dgpc = dm * yc_ref[...].astype(F32) * gc * (1.0 - gc)

        @pl.when(pl.program_id(0) == 0)
        def _():
            dbg_ref[...] = jnp.zeros_like(dbg_ref)

        dbg_ref[:, :D_MODEL] += _fold8(dgpa)
        dbg_ref[:, D_MODEL:] += _fold8(dgpc)
        dgpa = dgpa.astype(BF16)
        dgpc = dgpc.astype(BF16)
        dgp_ref[:, :D_MODEL] = dgpa
        dgp_ref[:, D_MODEL:] = dgpc
        dyat_ref[...] = _dot_nt(dya, wap[...]).astype(BF16)
        dycv_ref[...] = _dot_nt(dyc, wcp[...]).astype(BF16)
        dhg_ref[...] = _dot_nt(dgpa, wg[:, :D_MODEL]) + _dot_nt(dgpc, wg[:, D_MODEL:])

    row = pl.BlockSpec((tm, D_MODEL), lambda i: (i, 0))
    row2 = pl.BlockSpec((tm, 2 * D_MODEL), lambda i: (i, 0))
    full = _resident((D_MODEL, D_MODEL))
    return _call(
        body, name="mix_bwd", grid=(t // tm,), args=(dx1b, gates, ya, yc, w_out, w_ap, w_cp, w_g),
        in_specs=[row, row2, row, row, full, full, full, _resident((D_MODEL, 2 * D_MODEL))],
        out_specs=[row2, row, row, row, row, row, pl.BlockSpec((SUBLANES, 2 * D_MODEL), lambda i: (0, 0))],
        out_shape=[jax.ShapeDtypeStruct((t, 2 * D_MODEL), BF16)] + [jax.ShapeDtypeStruct((t, D_MODEL), BF16)] * 4
        + [jax.ShapeDtypeStruct((t, D_MODEL), F32), jax.ShapeDtypeStruct((SUBLANES, 2 * D_MODEL), F32)],
        semantics=("arbitrary",), rider=rider)


def _conv_bwd(dyconv, proj, conv_w, conv_b, rider=None):
    t = proj.shape[0]
    tm = min(t, 512)
    hb = tm // SUBLANES
    last = t // SUBLANES - 1

    def body(dy_ref, dyn_ref, bg_ref, bgn_ref, cg_ref, cgp_ref, xc_ref, xcp_ref, w_ref, b_ref,
             o_ref, dcb_ref, dcw_ref):
        i = pl.program_id(0)
        cg = cg_ref[...].astype(F32)
        xc = xc_ref[...].astype(F32)
        bg = bg_ref[...].astype(F32)
        u = cg * xc
        prev = jnp.where(i > 0, cgp_ref[...].astype(F32) * xcp_ref[...].astype(F32), 0.0)
        u1 = _shift_down(u, prev, 1)
        u2 = _shift_down(u, prev, 2)
        w = w_ref[...]
        conv = b_ref[...] + (w[0:1] * u2 + w[1:2] * u1 + w[2:3] * u)
        dy = dy_ref[...].astype(F32)
        dconv = dy * bg
        nxt = jnp.where(i < pl.num_programs(0) - 1, dyn_ref[...].astype(F32) * bgn_ref[...].astype(F32), 0.0)
        du = w[2:3] * dconv + w[1:2] * _shift_up(dconv, nxt, 1) + w[0:1] * _shift_up(dconv, nxt, 2)
        o_ref[:, :D_MODEL] = (dy * conv).astype(BF16)
        o_ref[:, D_MODEL:2 * D_MODEL] = (du * xc).astype(BF16)
        o_ref[:, 2 * D_MODEL:] = (du * cg).astype(BF16)

        @pl.when(i == 0)
        def _():
            dcb_ref[...] = jnp.zeros_like(dcb_ref)
            dcw_ref[...] = jnp.zeros_like(dcw_ref)

        dcb_ref[...] += _fold8(dconv)
        dcw_ref[0:SUBLANES] += _fold8(dconv * u2)
        dcw_ref[SUBLANES:2 * SUBLANES] += _fold8(dconv * u1)
        dcw_ref[2 * SUBLANES:] += _fold8(dconv * u)

    def prev(col):
        return pl.BlockSpec((SUBLANES, D_MODEL), lambda i: (jnp.maximum(i * hb - 1, 0), col))

    def nxt(col):
        return pl.BlockSpec((SUBLANES, D_MODEL), lambda i: (jnp.minimum((i + 1) * hb, last), col))

    def cur(col):
        return pl.BlockSpec((tm, D_MODEL), lambda i: (i, col))

    return _call(
        body, name="conv_bwd", grid=(t // tm,),
        args=(dyconv, dyconv, proj, proj, proj, proj, proj, proj, conv_w, conv_b),
        in_specs=[cur(0), nxt(0), cur(3), nxt(3), cur(4), prev(4), cur(5), prev(5),
                  pl.BlockSpec((3, D_MODEL), lambda i: (0, 0)), pl.BlockSpec((1, D_MODEL), lambda i: (0, 0))],
        out_specs=[pl.BlockSpec((tm, 3 * D_MODEL), lambda i: (i, 0)),
                   pl.BlockSpec((SUBLANES, D_MODEL), lambda i: (0, 0)),
                   pl.BlockSpec((3 * SUBLANES, D_MODEL), lambda i: (0, 0))],
        out_shape=[jax.ShapeDtypeStruct((t, 3 * D_MODEL), BF16), jax.ShapeDtypeStruct((SUBLANES, D_MODEL), F32),
                   jax.ShapeDtypeStruct((3 * SUBLANES, D_MODEL), F32)],
        semantics=("arbitrary",), rider=rider)


def _attn_bwd(qn, kn, proj, dyattn, y_attn, lse, bias, rider=None):
    t = qn.shape[0]
    nb = t // QB
    v_col0 = 2 * D_MODEL // SLAB

    def body(q_ref, k0, k1, k2, v0, v1, v2, do_ref, o_ref, lse_ref, bias_ref,
             dq_ref, dk_ref, dv_ref, db_ref, acck, accv):
        b = pl.program_id(1)

        @pl.when(b == 0)
        def _():
            acck[...] = jnp.zeros_like(acck)
            accv[...] = jnp.zeros_like(accv)
            db_ref[...] = jnp.zeros_like(db_ref)

        def block(valid):
            head_a = lax.broadcasted_iota(jnp.int32, (1, LANES), 1) < HEAD_DIM

            def window(refs, hp):
                sl = slice(hp * LANES, (hp + 1) * LANES)
                return jnp.concatenate([r[:, sl] for r in refs], axis=0)

            def transposed(x, hh):
                return x.astype(F32).T.astype(BF16)[hh * HEAD_DIM:(hh + 1) * HEAD_DIM]

            def probs(head):
                hp, hh = divmod(head, 2)
                sl = slice(hp * LANES, (hp + 1) * LANES)
                mine = head_a if hh == 0 else jnp.logical_not(head_a)
                k = window((k0, k1, k2), hp)
                s = _dot_nt(jnp.where(mine, k, jnp.zeros_like(k)), q_ref[:, sl]) + bias_ref[head]
                s = s if valid is None else jnp.where(valid, s, NEG_INF)
                return jnp.exp2(s - lse_ref[head:head + 1, :])

            def grads(head, p):
                hp, hh = divmod(head, 2)
                sl = slice(hp * LANES, (hp + 1) * LANES)
                rows = slice(hh * HEAD_DIM, (hh + 1) * HEAD_DIM)
                mine = head_a if hh == 0 else jnp.logical_not(head_a)
                do = do_ref[:, sl]
                v = window((v0, v1, v2), hp)
                delta = jnp.sum((do.astype(F32).T * o_ref[:, sl].astype(F32).T)[rows], axis=0, keepdims=True)
                ds = p * (_dot_nt(jnp.where(mine, v, jnp.zeros_like(v)), do) - delta)
                db_ref[head] += ds
                pb, dsb = p.astype(BF16), ds.astype(BF16)
                dvt = _dot_nt(transposed(do, hh), pb)
                dkt = _dot_nt(transposed(q_ref[:, sl], hh), dsb) * (1.0 / LOG2E)
                dqt = _dot(transposed(window((k0, k1, k2), hp), hh), dsb)
                return dqt, dkt, dvt

            out = []
            pending = probs(0)
            for head in range(2 * PAIRS):
                nxt = probs(head + 1) if head + 1 < 2 * PAIRS else None
                out.append(grads(head, pending))
                pending = nxt
            for hp in range(PAIRS):
                sl = slice(hp * LANES, (hp + 1) * LANES)
                dqt, dkt, dvt = (jnp.concatenate([out[2 * hp][n], out[2 * hp + 1][n]], axis=0) for n in range(3))
                dq_ref[:, sl] = dqt.T.astype(BF16)
                for w in range(3):
                    slot = lax.rem(b + w + 1, 3)
                    cols = slice(w * QB, (w + 1) * QB)
                    if w == 2:
                        acck[hp, slot] = dkt[:, cols]
                        accv[hp, slot] = dvt[:, cols]
                    else:
                        acck[hp, slot] += dkt[:, cols]
                        accv[hp, slot] += dvt[:, cols]

        @pl.when(b < 2)
        def _():
            block(lax.broadcasted_iota(jnp.int32, (KW, 1), 0) >= (2 - b) * QB)

        @pl.when((b >= 2) & (b < nb))
        def _():
            block(None)

        done = lax.rem(b + 1, 3)
        for hp in range(PAIRS):
            sl = slice(hp * LANES, (hp + 1) * LANES)
            dk_ref[:, sl] = acck[hp, done].T.astype(BF16)
            dv_ref[:, sl] = accv[hp, done].T.astype(BF16)

    def cur(p, b):
        return (jnp.minimum(b, nb - 1), p)

    def window(col0):
        return [pl.BlockSpec((QB, SLAB), functools.partial(
            lambda p, b, back: (jnp.maximum(jnp.minimum(b, nb - 1) - back, 0), col0 + p), back=back))
            for back in (2, 1, 0)]

    def done_block(p, b):
        return (jnp.maximum(b - 2, 0), p)

    tile = pl.BlockSpec((2 * PAIRS, KW, QB), lambda p, b: (p, 0, 0))
    here = pl.BlockSpec((QB, SLAB), cur)
    return _call(
        body, name="attn_bwd", grid=(D_MODEL // SLAB, nb + 2),
        args=(qn, kn, kn, kn, proj, proj, proj, dyattn, y_attn, lse, bias),
        in_specs=[here] + window(0) + window(v_col0)
        + [here, here, pl.BlockSpec((2 * PAIRS, QB), lambda p, b: (p, jnp.minimum(b, nb - 1))), tile],
        out_specs=[here, pl.BlockSpec((QB, SLAB), done_block), pl.BlockSpec((QB, SLAB), done_block), tile],
        out_shape=[jax.ShapeDtypeStruct((t, D_MODEL), BF16)] * 3 + [jax.ShapeDtypeStruct((N_HEADS, KW, QB), F32)],
        scratch_shapes=[pltpu.VMEM((PAIRS, 3, LANES, QB), F32), pltpu.VMEM((PAIRS, 3, LANES, QB), F32)],
        semantics=("parallel", "arbitrary"), rider=rider)


def _qknorm_bwd(proj, dqn, dkn, gq, gk):
    t = proj.shape[0]
    tm = min(t, 512)
    scale = HEAD_DIM ** -0.5

    def body(q_ref, k_ref, dqn_ref, dkn_ref, gq_ref, gk_ref, o_ref, dgq_ref, dgk_ref):
        e = _head_sum_matrix()

        @pl.when(pl.program_id(0) == 0)
        def _():
            dgq_ref[...] = jnp.zeros_like(dgq_ref)
            dgk_ref[...] = jnp.zeros_like(dgk_ref)

        for n, (src, dn_ref, g_ref, dg_ref, sc) in enumerate(
                ((q_ref, dqn_ref, gq_ref, dgq_ref, scale), (k_ref, dkn_ref, gk_ref, dgk_ref, 1.0))):
            for s in range(D_MODEL // LANES):
                sl = slice(s * LANES, (s + 1) * LANES)
                xf = src[:, sl].astype(F32)
                r = lax.rsqrt(_head_sums(xf * xf, e) * (1.0 / HEAD_DIM) + EPS)
                xh = xf * r
                dn = dn_ref[:, sl].astype(F32) * sc
                dg_ref[:, sl] += _fold8(dn * xh)
                dxh = dn * g_ref[:, sl]
                mean = _head_sums(dxh * xh, e) * (1.0 / HEAD_DIM)
                o_ref[:, n * D_MODEL + s * LANES:n * D_MODEL + (s + 1) * LANES] = (r * (dxh - xh * mean)).astype(BF16)

    row = pl.BlockSpec((tm, D_MODEL), lambda i: (i, 0))
    vec = pl.BlockSpec((1, D_MODEL), lambda i: (0, 0))
    acc = pl.BlockSpec((SUBLANES, D_MODEL), lambda i: (0, 0))
    return pl.pallas_call(
        body, name="qknorm_bwd", grid=(t // tm,),
        in_specs=[row, pl.BlockSpec((tm, D_MODEL), lambda i: (i, 1)), row, row, vec, vec],
        out_specs=[pl.BlockSpec((tm, 2 * D_MODEL), lambda i: (i, 0)), acc, acc],
        out_shape=[jax.ShapeDtypeStruct((t, 2 * D_MODEL), BF16)] + [jax.ShapeDtypeStruct((SUBLANES, D_MODEL), F32)] * 2,
        compiler_params=_params(("arbitrary",)),
    )(proj, proj, dqn, dkn, gq, gk)


def _in_bwd(dqk, dv, dconv, w_in, dhg, x, g1, dx1, rider=None):
    t = x.shape[0]
    tm = min(t, 512)

    def body(dqk_ref, dv_ref, dc_ref, w_ref, dhg_ref, x_ref, g_ref, dx1_ref, dx_ref, dg_ref):
        acc = dhg_ref[...]
        slab = 0
        for src, n in ((dqk_ref, 2), (dv_ref, 1), (dc_ref, 3)):
            for s in range(n):
                acc = acc + _dot_nt(src[:, s * D_MODEL:(s + 1) * D_MODEL],
                                    w_ref[:, slab * D_MODEL:(slab + 1) * D_MODEL])
                slab += 1

        @pl.when(pl.program_id(0) == 0)
        def _():
            dg_ref[...] = jnp.zeros_like(dg_ref)

        dx, dg = _rmsnorm_bwd(x_ref[...], g_ref[...], acc)
        dx_ref[...] = dx1_ref[...] + dx
        dg_ref[...] += _fold8(dg)

    row = pl.BlockSpec((tm, D_MODEL), lambda i: (i, 0))
    return _call(
        body, name="in_bwd", grid=(t // tm,), args=(dqk, dv, dconv, w_in, dhg, x, g1, dx1),
        in_specs=[pl.BlockSpec((tm, 2 * D_MODEL), lambda i: (i, 0)), row,
                  pl.BlockSpec((tm, 3 * D_MODEL), lambda i: (i, 0)),
                  _resident(w_in.shape), row, row, pl.BlockSpec((1, D_MODEL), lambda i: (0, 0)), row],
        out_specs=[row, pl.BlockSpec((SUBLANES, D_MODEL), lambda i: (0, 0))],
        out_shape=[jax.ShapeDtypeStruct((t, D_MODEL), F32), jax.ShapeDtypeStruct((SUBLANES, D_MODEL), F32)],
        semantics=("arbitrary",), rider=rider)


def _bias_grad_fold(dbias, rider=None):
    def body(d_ref, o_ref):
        jj = lax.broadcasted_iota(jnp.int32, (QB, QB), 0)
        ii = lax.broadcasted_iota(jnp.int32, (QB, QB), 1)
        flip = (jj + ii == QB - 1).astype(BF16)
        low = jj + ii < QB
        pos, neg = [], []
        for w in range(KW // QB):
            x = d_ref[0, QB * w:QB * (w + 1), :]
            hi = x.astype(BF16)
            r1 = x - hi.astype(F32)
            mid = r1.astype(BF16)
            lo = (r1 - mid.astype(F32)).astype(BF16)
            xr = _dot(hi, flip) + _dot(mid, flip) + _dot(lo, flip)
            for keep, acc in ((low, pos), (jnp.logical_not(low), neg)):
                part = pltpu.roll(jnp.where(keep, xr, 0.0), 0, 1, stride=1, stride_axis=0)
                acc.append(jnp.sum(part, axis=0, keepdims=True))
        far = pos[1] + neg[0] + pos[0]
        o_ref[0] = jnp.zeros((SUBLANES, QB), F32)
        o_ref[0, 0:1, :] = neg[2]
        o_ref[0, 1:2, :] = pos[2] + neg[1]
        o_ref[0, 2:3, :] = jnp.broadcast_to(jnp.sum(far, axis=-1, keepdims=True), (1, QB))

    return _call(
        body, name="bias_grad_fold", grid=(N_HEADS,), args=(dbias,),
        in_specs=[pl.BlockSpec((1, KW, QB), lambda h: (h, 0, 0))],
        out_specs=[pl.BlockSpec((1, SUBLANES, QB), lambda h: (h, 0, 0))],
        out_shape=[jax.ShapeDtypeStruct((N_HEADS, SUBLANES, QB), F32)],
        semantics=("parallel",), rider=rider)


def _small_partials(dg1, dgq, dgk, dcb, dcw, dbg, dg2, dbias_fold, loss_tile):
    def head_fold(v):
        acc = v[:, 0:LANES]
        for s in range(1, D_MODEL // LANES):
            acc = acc + v[:, s * LANES:(s + 1) * LANES]
        return acc + pltpu.roll(acc, HEAD_DIM, 1)

    def body(dg1_ref, dgq_ref, dgk_ref, dcb_ref, dcw_ref, dbg_ref, dg2_ref, db_ref, loss_ref, o_ref):
        o_ref[...] = jnp.zeros_like(o_ref)
        o_ref[0:1, :] = jnp.sum(dg1_ref[...], axis=0, keepdims=True)
        o_ref[1:2, 0:LANES] = head_fold(jnp.sum(dgq_ref[...], axis=0, keepdims=True))
        o_ref[2:3, 0:LANES] = head_fold(jnp.sum(dgk_ref[...], axis=0, keepdims=True))
        o_ref[3:4, :] = jnp.sum(dcb_ref[...], axis=0, keepdims=True)
        for j in range(3):
            o_ref[4 + j:5 + j, :] = jnp.sum(dcw_ref[j * SUBLANES:(j + 1) * SUBLANES, :], axis=0, keepdims=True)
        o_ref[7:8, :] = jnp.sum(dbg_ref[:, :D_MODEL], axis=0, keepdims=True)
        o_ref[8:9, :] = jnp.sum(dbg_ref[:, D_MODEL:], axis=0, keepdims=True)
        o_ref[9:10, :] = jnp.sum(dg2_ref[...], axis=0, keepdims=True)
        for h in range(N_HEADS):
            for part in range(3):
                o_ref[10 + h:11 + h, part * QB:(part + 1) * QB] = db_ref[h, part:part + 1, :]
        loss = (0.5 / D_MODEL) * jnp.sum(jnp.sum(loss_ref[...], axis=0, keepdims=True), axis=-1, keepdims=True)
        o_ref[26:27, :] = jnp.broadcast_to(loss, (1, D_MODEL))

    return pl.pallas_call(
        body, name="small_partials",
        out_shape=jax.ShapeDtypeStruct((32, D_MODEL), F32),
        compiler_params=_params(),
    )(dg1, dgq, dgk, dcb, dcw, dbg, dg2, dbias_fold, loss_tile)


MID_AXES = (0, 0, 1, 0)
MLP_AXES = (1, 0)


def _local_step(x, h, target, norm1_g, q_norm_g, k_norm_g, bias, conv_w, conv_b, b_gate, norm2_g,
                w_in, mid_w, mlp_w, distributed):
    g1 = norm1_g.reshape(1, D_MODEL)
    g2 = norm2_g.reshape(1, D_MODEL)
    gq = jnp.tile(q_norm_g, N_HEADS).reshape(1, D_MODEL)
    gk = jnp.tile(k_norm_g, N_HEADS).reshape(1, D_MODEL)
    cb = conv_b.reshape(1, D_MODEL)

    (proj,), got = _in_proj(h, w_in, rider=_Gather(mid_w, MID_AXES) if distributed else None)
    w_ap, w_cp, w_g, w_out = got if distributed else mid_w
    gates, qn, kn, y_conv = _gates_norms_conv(h, proj, w_g, b_gate.reshape(1, 2 * D_MODEL), gq, gk, conv_w, cb)
    (y_attn, lse), got = _attn_fwd(qn, kn, proj, bias, rider=_Gather(mlp_w, MLP_AXES) if distributed else None)
    w_up, w_down = got if distributed else mlp_w
    ya, yc, merged, x1, h2 = _mix_out(y_attn, y_conv, gates, x, w_ap, w_cp, w_out, g2)
    a, dy, dyb, loss_tile = _mlp_fwd(h2, w_up, w_down, x1, target)

    da, dx1, dx1b, dg2 = _mlp_bwd(dyb, a, w_down, w_up, x1, dy, g2)
    gw_down = _wgrad("wgrad_down", a, [dyb], [1], relu_sq=True, token_block=4096)
    gw_up = _wgrad("wgrad_up", h2, [da], [D_FF // D_MODEL])
    (dgp, dya, dyc, dyattn, dyconv, dhg, dbg), mlp_swapped = _mix_bwd(
        dx1b, gates, ya, yc, w_out, w_ap, w_cp, w_g,
        rider=_PairSwap((gw_up, gw_down), MLP_AXES) if distributed else None)
    mid = tuple(_wgrad_group("wgrad_mid", [(y_attn, dya, 1), (y_conv, dyc, 1), (h, dgp, 2), (merged, dx1b, 1)]))
    gw_ap, gw_cp, gw_g, gw_out = mid
    (dconv, dcb, dcw), mid_swapped = _conv_bwd(
        dyconv, proj, conv_w, cb, rider=_PairSwap(mid, MID_AXES) if distributed else None)
    early = mid + (gw_up, gw_down)
    early_sums = (_pair_add(early, tuple(mid_swapped) + tuple(mlp_swapped), MID_AXES + MLP_AXES)
                  if distributed else None)
    (dqn, dkn, dv, dbias), early_shares = _attn_bwd(
        qn, kn, proj, dyattn, y_attn, lse, bias, rider=_ChipScatter(early_sums) if distributed else None)
    dqk, dgq, dgk = _qknorm_bwd(proj, dqn, dkn, gq, gk)
    gw_in = _wgrad("wgrad_in", h, [dqk, dv, dconv], [2, 1, 3])
    (dbias_fold,), in_swapped = _bias_grad_fold(dbias, rider=_PairSwap((gw_in,), (1,)) if distributed else None)
    in_sums = _pair_add((gw_in,), in_swapped, (1,)) if distributed else None
    (dx, dg1), in_shares = _in_bwd(dqk, dv, dconv, w_in, dhg, x, g1, dx1,
                                   rider=_ChipScatter(in_sums) if distributed else None)
    small = _small_partials(dg1, dgq, dgk, dcb, dcw, dbg, dg2, dbias_fold, loss_tile)
    grads = tuple(in_shares) + tuple(early_shares) if distributed else (gw_in,) + early
    return dx, grads, small


def _me():
    return lax.axis_index("x"), lax.axis_index("y"), lax.axis_index("c")


def _peer(me, rel):
    x, y, c = me
    return (1 - x if rel & 4 else x, 1 - y if rel & 2 else y, 1 - c if rel & 1 else c)


def _linear(dev):
    return 4 * dev[0] + 2 * dev[1] + dev[2]


def _block(ref, axis, idx, size):
    return ref.at[pl.ds(idx * size, size), :] if axis == 0 else ref.at[:, pl.ds(idx * size, size)]


def _cast_shards(shards):
    def body(*refs):
        for src, dst in zip(refs[:len(shards)], refs[len(shards):]):
            dst[...] = src[...].astype(BF16)

    return pl.pallas_call(
        body, name="cast_shards",
        out_shape=[jax.ShapeDtypeStruct(s.shape, BF16) for s in shards],
        compiler_params=_params(),
    )(*shards)


class _Gather:
    def __init__(self, shards, axes):
        self.arrays, self.axes, self.n = list(shards), tuple(axes), len(shards)
        self.sizes = [s.shape[axis] for s, axis in zip(shards, axes)]
        self.out_shape = []
        for s, axis in zip(shards, axes):
            shape = (s.shape[0] * N_DEV, s.shape[1]) if axis == 0 else (s.shape[0], s.shape[1] * N_DEV)
            self.out_shape.append(jax.ShapeDtypeStruct(shape, s.dtype))
        self.scratch = [pltpu.SemaphoreType.DMA((self.n, 7)), pltpu.SemaphoreType.DMA((self.n, 7)),
                        pltpu.SemaphoreType.DMA((self.n,))]

    def _copies(self, srcs, outs, sems):
        send_sems, recv_sems, local_sems = sems
        me = _me()
        sibling = _peer(me, 1)
        chips = [_peer(me, rel) for rel in (4, 2, 6)]

        def rows(a, dev):
            return _block(outs[a], self.axes[a], _linear(dev), self.sizes[a])

        def copy(a, k, block_dev, to, src=None):
            return pltpu.make_async_remote_copy(
                src_ref=rows(a, block_dev) if src is None else src, dst_ref=rows(a, block_dev),
                send_sem=send_sems.at[a, k], recv_sem=recv_sems.at[a, k], device_id=to, device_id_type=MESH_T)

        own = [pltpu.make_async_copy(srcs[a], rows(a, me), local_sems.at[a]) for a in range(self.n)]
        first = []
        for a in range(self.n):
            first.append(copy(a, 0, me, sibling, src=srcs[a]))
            for j, chip in enumerate(chips):
                first.append(copy(a, 1 + j, me, chip, src=srcs[a]))
        return me, sibling, chips, copy, own, first

    def start(self, srcs, outs, sems):
        _, _, _, _, own, first = self._copies(srcs, outs, sems)
        for cp in own + first:
            cp.start()

    def finish(self, srcs, outs, sems):
        me, sibling, chips, copy, own, first = self._copies(srcs, outs, sems)
        passed = []
        for a in range(self.n):
            for j, chip in enumerate(chips):
                copy(a, 1 + j, chip, me).wait_recv()
                fwd = copy(a, 4 + j, chip, sibling)
                fwd.start()
                passed.append(fwd)
        for a in range(self.n):
            copy(a, 0, sibling, me).wait_recv()
            for j, chip in enumerate(chips):
                copy(a, 4 + j, _peer(chip, 1), me).wait_recv()
        for cp in first + passed:
            cp.wait_send()
        for cp in own:
            cp.wait()


N_CHIPS = 4


def _shard_shape(g, axis):
    return (g.shape[0] // N_DEV, g.shape[1]) if axis == 0 else (g.shape[0], g.shape[1] // N_DEV)


class _PairSwap:
    def __init__(self, grads, axes):
        self.arrays, self.axes, self.n = list(grads), tuple(axes), len(grads)
        self.sizes = [g.shape[axis] // N_DEV for g, axis in zip(grads, axes)]
        self.out_shape = [jax.ShapeDtypeStruct((N_CHIPS,) + _shard_shape(g, axis), g.dtype)
                          for g, axis in zip(grads, axes)]
        self.scratch = [pltpu.SemaphoreType.DMA((self.n, N_CHIPS)), pltpu.SemaphoreType.DMA((self.n, N_CHIPS))]

    def _copies(self, srcs, outs, sems):
        send_sems, recv_sems = sems
        x, y, c = _me()
        sibling = (x, y, 1 - c)
        copies = []
        for a in range(self.n):
            for chip in range(N_CHIPS):
                owner_idx = 2 * chip + (1 - c)
                copies.append(pltpu.make_async_remote_copy(
                    src_ref=_block(srcs[a], self.axes[a], owner_idx, self.sizes[a]), dst_ref=outs[a].at[chip],
                    send_sem=send_sems.at[a, chip], recv_sem=recv_sems.at[a, chip],
                    device_id=sibling, device_id_type=MESH_T))
        return copies

    def start(self, srcs, outs, sems):
        for cp in self._copies(srcs, outs, sems):
            cp.start()

    def finish(self, srcs, outs, sems):
        for cp in self._copies(srcs, outs, sems):
            cp.wait()


def _pair_add(grads, swapped, axes):
    n = len(grads)
    c_arr = lax.axis_index("c").astype(jnp.int32).reshape(1)

    def body(c_ref, *refs):
        del c_ref
        mine, got, outs = refs[:n], refs[n:2 * n], refs[2 * n:]
        for a in range(n):
            outs[a][0] = (mine[a][...].astype(F32) + got[a][0].astype(F32)).astype(BF16)

    in_specs, out_specs, out_shape = [], [], []
    for g, axis in zip(grads, axes):
        shard = _shard_shape(g, axis)
        if axis == 0:
            in_specs.append(pl.BlockSpec(shard, lambda s, c_ref: (2 * s + c_ref[0], 0)))
        else:
            in_specs.append(pl.BlockSpec(shard, lambda s, c_ref: (0, 2 * s + c_ref[0])))
    for g, axis in zip(grads, axes):
        shard = _shard_shape(g, axis)
        in_specs.append(pl.BlockSpec((1,) + shard, lambda s, c_ref: (s, 0, 0)))
        out_specs.append(pl.BlockSpec((1,) + shard, lambda s, c_ref: (s, 0, 0)))
        out_shape.append(jax.ShapeDtypeStruct((N_CHIPS,) + shard, BF16))
    return pl.pallas_call(
        body, name="pair_add_" + str(n),
        grid_spec=pltpu.PrefetchScalarGridSpec(num_scalar_prefetch=1, grid=(N_CHIPS,), in_specs=in_specs,
                                               out_specs=out_specs),
        out_shape=out_shape, compiler_params=_params(("arbitrary",)),
    )(c_arr, *grads, *swapped)


class _ChipScatter:
    def __init__(self, sums):
        self.arrays, self.n = list(sums), len(sums)
        self.out_shape = [jax.ShapeDtypeStruct(s.shape, s.dtype) for s in sums]
        self.scratch = [pltpu.SemaphoreType.DMA((self.n, 3)), pltpu.SemaphoreType.DMA((self.n, 3)),
                        pltpu.SemaphoreType.DMA((self.n,))]

    def _copies(self, srcs, outs, sems):
        send_sems, recv_sems, local_sems = sems
        me = _me()
        my_chip = 2 * me[0] + me[1]
        own = [pltpu.make_async_copy(srcs[a].at[my_chip], outs[a].at[my_chip], local_sems.at[a])
               for a in range(self.n)]
        sends, recvs = [], []
        for a in range(self.n):
            for k, rel in enumerate((4, 2, 6)):
                peer = _peer(me, rel)
                peer_chip = 2 * peer[0] + peer[1]
                sends.append(pltpu.make_async_remote_copy(
                    src_ref=srcs[a].at[peer_chip], dst_ref=outs[a].at[my_chip],
                    send_sem=send_sems.at[a, k], recv_sem=recv_sems.at[a, k], device_id=peer, device_id_type=MESH_T))
                recvs.append(pltpu.make_async_remote_copy(
                    src_ref=srcs[a].at[my_chip], dst_ref=outs[a].at[peer_chip],
                    send_sem=send_sems.at[a, k], recv_sem=recv_sems.at[a, k], device_id=peer, device_id_type=MESH_T))
        return own, sends, recvs

    def start(self, srcs, outs, sems):
        own, sends, _ = self._copies(srcs, outs, sems)
        for cp in own + sends:
            cp.start()

    def finish(self, srcs, outs, sems):
        own, sends, recvs = self._copies(srcs, outs, sems)
        for cp in recvs:
            cp.wait_recv()
        for cp in sends:
            cp.wait_send()
        for cp in own:
            cp.wait()


def _call(body, *, name, args, in_specs, out_specs, out_shape, grid=(), scratch_shapes=(), semantics=None,
          rider=None):
    if rider is None:
        return pl.pallas_call(
            body, name=name, grid=grid, in_specs=in_specs, out_specs=out_specs, out_shape=out_shape,
            scratch_shapes=list(scratch_shapes), compiler_params=_params(semantics))(*args), None
    n_in, n_out, n_scr, r = len(in_specs), len(out_specs), len(scratch_shapes), rider.n

    def wrapped(*refs):
        ins, r_ins = refs[:n_in], refs[n_in:n_in + r]
        outs = refs[n_in + r:n_in + r + n_out]
        r_outs = refs[n_in + r + n_out:n_in + 2 * r + n_out]
        scr = refs[n_in + 2 * r + n_out:n_in + 2 * r + n_out + n_scr]
        sems = refs[n_in + 2 * r + n_out + n_scr:]
        first, last = None, None
        for ax in range(len(grid)):
            f, l = pl.program_id(ax) == 0, pl.program_id(ax) == pl.num_programs(ax) - 1
            first = f if first is None else first & f
            last = l if last is None else last & l
        if first is None:
            rider.start(r_ins, r_outs, sems)
            body(*ins, *outs, *scr)
            rider.finish(r_ins, r_outs, sems)
            return

        @pl.when(first)
        def _():
            rider.start(r_ins, r_outs, sems)

        body(*ins, *outs, *scr)

        @pl.when(last)
        def _():
            rider.finish(r_ins, r_outs, sems)

    any_spec = pl.BlockSpec(memory_space=pl.ANY)
    out = pl.pallas_call(
        wrapped, name=name, grid=grid, in_specs=list(in_specs) + [any_spec] * r,
        out_specs=list(out_specs) + [any_spec] * r, out_shape=list(out_shape) + rider.out_shape,
        scratch_shapes=list(scratch_shapes) + rider.scratch,
        compiler_params=_params(None if semantics is None else ("arbitrary",) * len(semantics)),
    )(*args, *rider.arrays)
    return out[:n_out], out[n_out:]


def _all_reduce_small(part):
    def body(p_ref, o_ref, slots, send_sems, recv_sems):
        me = _me()
        my_idx = _linear(me)
        slots[my_idx] = p_ref[...]
        sends = []
        for rel in range(1, N_DEV):
            cp = pltpu.make_async_remote_copy(
                src_ref=p_ref, dst_ref=slots.at[my_idx], send_sem=send_sems.at[rel - 1],
                recv_sem=recv_sems.at[rel - 1], device_id=_peer(me, rel), device_id_type=MESH_T)
            cp.start()
            sends.append(cp)
        for rel in range(1, N_DEV):
            frm = _peer(me, rel)
            pltpu.make_async_remote_copy(
                src_ref=p_ref, dst_ref=slots.at[_linear(frm)], send_sem=send_sems.at[rel - 1],
                recv_sem=recv_sems.at[rel - 1], device_id=frm, device_id_type=MESH_T).wait_recv()
        for cp in sends:
            cp.wait_send()
        total = slots[0]
        for d in range(1, N_DEV):
            total = total + slots[d]
        o_ref[...] = total

    return pl.pallas_call(
        body, name="all_reduce_small",
        in_specs=[pl.BlockSpec(memory_space=pltpu.VMEM)], out_specs=pl.BlockSpec(memory_space=pltpu.VMEM),
        out_shape=jax.ShapeDtypeStruct(part.shape, F32),
        scratch_shapes=[pltpu.VMEM((N_DEV,) + part.shape, F32), pltpu.SemaphoreType.DMA((7,)),
                        pltpu.SemaphoreType.DMA((7,))],
        compiler_params=_params(),
    )(part)


def _adamw_math(w, g, m, v):
    m = ADAM_B1 * m + (1.0 - ADAM_B1) * g
    v = ADAM_B2 * v + (1.0 - ADAM_B2) * jnp.square(g)
    m_hat = m / (1.0 - ADAM_B1 ** ADAM_STEP)
    v_hat = v / (1.0 - ADAM_B2 ** ADAM_STEP)
    delta = -ADAM_LR * (m_hat / (jnp.sqrt(v_hat) + ADAM_EPS) + ADAM_WD * w)
    return delta, m, v


ADAMW_STEPS = 4


def _adamw_big(shares, ws, ms, vs, rider=None):
    n = len(ws)

    def body(*refs):
        s_refs, w_refs, m_refs, v_refs = (refs[a * n:(a + 1) * n] for a in range(4))
        outs = refs[4 * n:]
        for a in range(n):
            g = s_refs[a][0].astype(F32)
            for d in range(1, N_CHIPS):
                g = g + s_refs[a][d].astype(F32)
            outs[4 * a][...] = g
            outs[4 * a + 1][...], outs[4 * a + 2][...], outs[4 * a + 3][...] = _adamw_math(
                w_refs[a][...], g, m_refs[a][...], v_refs[a][...])

    def chunk(w):
        return pl.BlockSpec((w.shape[0] // ADAMW_STEPS, w.shape[1]), lambda i: (i, 0))

    def share_chunk(w):
        return pl.BlockSpec((N_CHIPS, w.shape[0] // ADAMW_STEPS, w.shape[1]), lambda i: (0, i, 0))

    out, rider_out = _call(
        body, name="adamw_big", grid=(ADAMW_STEPS,), args=(*shares, *ws, *ms, *vs),
        in_specs=[share_chunk(w) for w in ws] + [chunk(w) for w in ws] * 3,
        out_specs=[chunk(w) for w in ws for _ in range(4)],
        out_shape=[jax.ShapeDtypeStruct(w.shape, F32) for w in ws for _ in range(4)],
        semantics=("parallel",), rider=rider)
    return [tuple(out[4 * a:4 * a + 4]) for a in range(n)], rider_out


def _adamw_small(quads):
    n = len(quads)

    def body(*refs):
        ins, outs = refs[:4 * n], refs[4 * n:]
        for p in range(n):
            g_ref, w_ref, m_ref, v_ref = ins[4 * p:4 * p + 4]
            d_ref, nm_ref, nv_ref = outs[3 * p:3 * p + 3]
            d_ref[...], nm_ref[...], nv_ref[...] = _adamw_math(w_ref[...], g_ref[...], m_ref[...], v_ref[...])

    flat = [a for quad in quads for a in quad]
    out = pl.pallas_call(
        body, name="adamw_small",
        out_shape=[jax.ShapeDtypeStruct(quad[1].shape, F32) for quad in quads for _ in range(3)],
        compiler_params=_params(),
    )(*flat)
    return [tuple(out[3 * p:3 * p + 3]) for p in range(n)]


def kernel(x, norm1_g, w_in, q_norm_g, k_norm_g, rel_bias, conv_w, conv_b, w_attn_proj, w_conv_proj, w_gate, b_gate, w_out, norm2_g, w_up, w_down, loss_target, m_norm1_g, m_w_in, m_q_norm_g, m_k_norm_g, m_rel_bias, m_conv_w, m_conv_b, m_w_attn_proj, m_w_conv_proj, m_w_gate, m_b_gate, m_w_out, m_norm2_g, m_w_up, m_w_down, v_norm1_g, v_w_in, v_q_norm_g, v_k_norm_g, v_rel_bias, v_conv_w, v_conv_b, v_w_attn_proj, v_w_conv_proj, v_w_gate, v_b_gate, v_w_out, v_norm2_g, v_w_up, v_w_down):
    my_idx = _linear(_me())
    big_w = (w_in, w_attn_proj, w_conv_proj, w_gate, w_out, w_up, w_down)
    big_m = (m_w_in, m_w_attn_proj, m_w_conv_proj, m_w_gate, m_w_out, m_w_up, m_w_down)
    big_v = (v_w_in, v_w_attn_proj, v_w_conv_proj, v_w_gate, v_w_out, v_w_up, v_w_down)
    big_names = ("w_in", "w_attn_proj", "w_conv_proj", "w_gate", "w_out", "w_up", "w_down")

    conv_w_tile = jnp.pad(conv_w, ((0, SUBLANES - conv_w.shape[0]), (0, 0)))
    (w_in_shard,) = _cast_shards(big_w[:1])
    (bias, h, *shards), (w_in_full, conv_w_rows) = _bias_tiles(
        rel_bias, x[0], norm1_g.reshape(1, D_MODEL), big_w[1:],
        rider=_Gather((w_in_shard, conv_w_tile), (1, 1)))

    dx, shares, small = _local_step(x[0], h, loss_target[0], norm1_g, q_norm_g, k_norm_g, bias, conv_w_rows[:3],
                                    conv_b, b_gate, norm2_g, w_in_full, tuple(shards[0:4]), tuple(shards[4:6]), True)

    big_out, _ = _adamw_big(shares, big_w, big_m, big_v)
    tot = _all_reduce_small(small)
    g_rel_bias = jnp.concatenate(
        [tot[10:26, :QB][:, ::-1], tot[10:26, QB:2 * QB][:, ::-1], tot[10:26, 2 * QB:2 * QB + 1]], axis=1)
    g_conv_w = lax.dynamic_slice(tot[4:7], (0, my_idx * LANES), (3, LANES))
    small_g = [tot[0:1], tot[1:2, :HEAD_DIM], tot[2:3, :HEAD_DIM], g_rel_bias, g_conv_w, tot[3:4],
               tot[7:9].reshape(1, 2 * D_MODEL), tot[9:10]]
    small_w = (norm1_g, q_norm_g, k_norm_g, rel_bias, conv_w, conv_b, b_gate, norm2_g)
    small_m = (m_norm1_g, m_q_norm_g, m_k_norm_g, m_rel_bias, m_conv_w, m_conv_b, m_b_gate, m_norm2_g)
    small_v = (v_norm1_g, v_q_norm_g, v_k_norm_g, v_rel_bias, v_conv_w, v_conv_b, v_b_gate, v_norm2_g)

    def two_d(a):
        return a.reshape(1, -1) if a.ndim == 1 else a

    small_out = _adamw_small([(g, two_d(w), two_d(m), two_d(v))
                              for g, w, m, v in zip(small_g, small_w, small_m, small_v)])

    order = ("norm1_g", "w_in", "q_norm_g", "k_norm_g", "rel_bias", "conv_w", "conv_b", "w_attn_proj", "w_conv_proj",
             "w_gate", "b_gate", "w_out", "norm2_g", "w_up", "w_down")
    small_names = ("norm1_g", "q_norm_g", "k_norm_g", "rel_bias", "conv_w", "conv_b", "b_gate", "norm2_g")
    res = {}
    for name, (g, d, nm, nv) in zip(big_names, big_out):
        res[name] = (g, d, nm, nv)
    for name, g, w, (d, nm, nv) in zip(small_names, small_g, small_w, small_out):
        res[name] = tuple(a.reshape(w.shape) for a in (g, d, nm, nv))
    loss = tot[26, 0]
    return (loss, dx[None], *[res[n][0] for n in order], *[res[n][1] for n in order],
            *[res[n][2] for n in order], *[res[n][3] for n in order])
```

```python
import functools

import jax
import jax.numpy as jnp
from jax import lax
from jax.experimental import pallas as pl
from jax.experimental.pallas import tpu as pltpu

F32 = jnp.float32
BF16 = jnp.bfloat16

D_MODEL = 1024
N_HEADS = 16
HEAD_DIM = 64
CHUNK = 64
N_PREV_CHUNKS = 8
MAX_REL = 256
D_FF = 4096
EPS = 1e-6
NEG_INF = -1e30
LOG2E = 1.4426950408889634
N_DEV = 8

ADAM_LR = 0.001
ADAM_B1 = 0.9
ADAM_B2 = 0.999
ADAM_EPS = 1e-08
ADAM_WD = 0.01
ADAM_STEP = 10

LANES = 128
SUBLANES = 8
VMEM_LIMIT = 56 * 1024 * 1024
QB = 256
KW = 3 * QB
PAIRS = 4
SLAB = PAIRS * LANES
PAIRS_FWD = 8
SKEW = 1024

MESH_T = pl.DeviceIdType.MESH


def _dot(a, b):
    return jnp.dot(a, b, preferred_element_type=F32)


def _dot_nt(a, b):
    return lax.dot_general(a, b, (((1,), (1,)), ((), ())), preferred_element_type=F32)


def _dot_tn(a, b):
    return lax.dot_general(a, b, (((0,), (0,)), ((), ())), preferred_element_type=F32)


def _params(sem=None):
    return pltpu.CompilerParams(dimension_semantics=sem, vmem_limit_bytes=VMEM_LIMIT)


def _resident(shape):
    return pl.BlockSpec(shape, lambda *_: (0,) * len(shape), pipeline_mode=pl.Buffered(1))


def _fold8(v):
    rows, n = v.shape
    return v.reshape(rows // SUBLANES, SUBLANES, n).sum(axis=0)


def _head_sum_matrix():
    r = lax.broadcasted_iota(jnp.int32, (LANES, LANES), 0) // HEAD_DIM
    c = lax.broadcasted_iota(jnp.int32, (LANES, LANES), 1) // HEAD_DIM
    return (r == c).astype(BF16)


def _head_sums(v, e):
    hi = v.astype(BF16)
    lo = (v - hi.astype(F32)).astype(BF16)
    return _dot(hi, e) + _dot(lo, e)


def _in_proj(h, w_in, rider=None):
    t = h.shape[0]
    tm = min(t, 512)
    n_out = w_in.shape[1]

    def body(h_ref, w_ref, proj_ref):
        h = h_ref[...]
        for k in range(n_out // D_MODEL):
            cols = slice(k * D_MODEL, (k + 1) * D_MODEL)
            proj_ref[:, cols] = _dot(h, w_ref[:, cols]).astype(BF16)

    return _call(
        body, name="in_proj", grid=(t // tm,), args=(h, w_in),
        in_specs=[pl.BlockSpec((tm, D_MODEL), lambda i: (i, 0)), _resident((D_MODEL, n_out))],
        out_specs=[pl.BlockSpec((tm, n_out), lambda i: (i, 0))],
        out_shape=[jax.ShapeDtypeStruct((t, n_out), BF16)],
        semantics=("parallel",), rider=rider)


def _bias_tiles(rel_bias, x, g1, later_shards=(), rider=None):
    t = x.shape[0]
    rows = t // N_HEADS
    n_later = len(later_shards)
    by_dist = jnp.concatenate(
        [rel_bias[:, :2 * MAX_REL], jnp.broadcast_to(rel_bias[:, 2 * MAX_REL:], (N_HEADS, 2 * MAX_REL))], axis=1)
    by_dist = by_dist.reshape(N_HEADS, 1, SKEW)

    def body(f_ref, x_ref, g_ref, *refs):
        src_refs, o_ref, h_ref = refs[:n_later], refs[n_later], refs[n_later + 1]
        dst_refs, buf_refs = refs[n_later + 2:2 * n_later + 2], refs[2 * n_later + 2:3 * n_later + 2]

        def fetch(a):
            return pltpu.make_async_copy(src_refs[a], buf_refs[a], refs[-1].at[a])

        @pl.when(pl.program_id(0) == 1)
        def _():
            for a in range(n_later):
                fetch(a).start()

        @pl.when(pl.program_id(0) == N_HEADS - 2)
        def _():
            for a in range(n_later):
                fetch(a).wait()
                dst_refs[a][...] = buf_refs[a][...].astype(BF16)

        xf = x_ref[...]
        r = lax.rsqrt(jnp.mean(xf * xf, axis=-1, keepdims=True) + EPS)
        h_ref[...] = (xf * r * g_ref[...]).astype(BF16)
        jj = lax.broadcasted_iota(jnp.int32, (QB, QB), 0)
        ii = lax.broadcasted_iota(jnp.int32, (QB, QB), 1)
        for w in range(KW // QB):
            pos = jnp.broadcast_to(f_ref[0, :, KW - QB * w:KW - QB * w + QB], (QB, QB))
            neg = jnp.broadcast_to(f_ref[0, :, KW - QB * (w + 1):KW - QB * w], (QB, QB))
            pos = pltpu.roll(pos, 0, 1, stride=1, stride_axis=0)
            neg = pltpu.roll(neg, 0, 1, stride=1, stride_axis=0)
            tile = jnp.where(ii >= jj, pos, neg)
            kc = (jj + QB * w) // CHUNK
            qc = ii // CHUNK
            band = (kc >= qc) & (kc <= qc + N_PREV_CHUNKS)
            o_ref[0, QB * w:QB * (w + 1), :] = jnp.where(band, tile * LOG2E, NEG_INF)

    def whole(s):
        return pl.BlockSpec(s.shape, lambda h: (0, 0))

    return _call(
        body, name="bias_tiles", grid=(N_HEADS,), args=(by_dist, x, g1, *later_shards),
        in_specs=[pl.BlockSpec((1, 1, SKEW), lambda h: (h, 0, 0)),
                  pl.BlockSpec((rows, D_MODEL), lambda h: (h, 0)),
                  pl.BlockSpec((1, D_MODEL), lambda h: (0, 0))]
        + [pl.BlockSpec(memory_space=pl.ANY)] * n_later,
        out_specs=[pl.BlockSpec((1, KW, QB), lambda h: (h, 0, 0)),
                   pl.BlockSpec((rows, D_MODEL), lambda h: (h, 0))] + [whole(s) for s in later_shards],
        out_shape=[jax.ShapeDtypeStruct((N_HEADS, KW, QB), F32), jax.ShapeDtypeStruct((t, D_MODEL), BF16)]
        + [jax.ShapeDtypeStruct(s.shape, BF16) for s in later_shards],
        scratch_shapes=[pltpu.VMEM(s.shape, F32) for s in later_shards]
        + ([pltpu.SemaphoreType.DMA((n_later,))] if n_later else []),
        semantics=("arbitrary",), rider=rider)


def _window_specs(col0, slab):
    return [pl.BlockSpec((QB, slab), functools.partial(
        lambda p, b, back: (jnp.maximum(b - back, 0), col0 + p), back=back)) for back in (2, 1, 0)]


def _attn_fwd(qn, kn, proj, bias, rider=None):
    t = qn.shape[0]
    nb = t // QB
    pairs = PAIRS_FWD
    slab = pairs * LANES
    v_col0 = 2 * D_MODEL // slab

    def body(q_ref, k0, k1, k2, v0, v1, v2, bias_ref, o_ref, lse_ref):
        b = pl.program_id(1)

        @pl.when(b < 2)
        def _():
            compute(q_ref, k0, k1, k2, v0, v1, v2, bias_ref, o_ref, lse_ref,
                    lax.broadcasted_iota(jnp.int32, (KW, 1), 0) >= (2 - b) * QB)

        @pl.when(b >= 2)
        def _():
            compute(q_ref, k0, k1, k2, v0, v1, v2, bias_ref, o_ref, lse_ref, None)

    def compute(q_ref, k0, k1, k2, v0, v1, v2, bias_ref, o_ref, lse_ref, valid):
        head_a = lax.broadcasted_iota(jnp.int32, (1, LANES), 1) < HEAD_DIM

        def scores(head):
            hp, hh = divmod(head, 2)
            sl = slice(hp * LANES, (hp + 1) * LANES)
            k = jnp.concatenate([k0[:, sl], k1[:, sl], k2[:, sl]], axis=0)
            mine = head_a if hh == 0 else jnp.logical_not(head_a)
            s = _dot_nt(jnp.where(mine, k, jnp.zeros_like(k)), q_ref[:, sl]) + bias_ref[head]
            return s if valid is None else jnp.where(valid, s, NEG_INF)

        def weighted_values(head, s):
            hp, hh = divmod(head, 2)
            sl = slice(hp * LANES, (hp + 1) * LANES)
            v = jnp.concatenate([v0[:, sl], v1[:, sl], v2[:, sl]], axis=0)
            vt = v.astype(F32).T.astype(BF16)[hh * HEAD_DIM:(hh + 1) * HEAD_DIM]
            vt = jnp.concatenate([vt, jnp.ones((SUBLANES, KW), BF16)], axis=0)
            m = jnp.max(s, axis=0, keepdims=True)
            ov = _dot(vt, jnp.exp2(s - m).astype(BF16))
            l = ov[HEAD_DIM:HEAD_DIM + 1]
            return ov[:HEAD_DIM] / l, m + jnp.log2(l)

        outs, lses = [], []
        pending = scores(0)
        for head in range(2 * pairs):
            nxt = scores(head + 1) if head + 1 < 2 * pairs else None
            o, lse = weighted_values(head, pending)
            outs.append(o)
            lses.append(lse)
            pending = nxt
        for hp in range(pairs):
            sl = slice(hp * LANES, (hp + 1) * LANES)
            o_ref[:, sl] = jnp.concatenate([outs[2 * hp], outs[2 * hp + 1]], axis=0).T.astype(BF16)
        lse_ref[...] = jnp.concatenate(lses, axis=0)

    return _call(
        body, name="attn_fwd", grid=(D_MODEL // slab, nb), args=(qn, kn, kn, kn, proj, proj, proj, bias),
        in_specs=[pl.BlockSpec((QB, slab), lambda p, b: (b, p))] + _window_specs(0, slab)
        + _window_specs(v_col0, slab) + [pl.BlockSpec((2 * pairs, KW, QB), lambda p, b: (p, 0, 0))],
        out_specs=[pl.BlockSpec((QB, slab), lambda p, b: (b, p)),
                   pl.BlockSpec((2 * pairs, QB), lambda p, b: (p, b))],
        out_shape=[jax.ShapeDtypeStruct((t, D_MODEL), BF16), jax.ShapeDtypeStruct((N_HEADS, t), F32)],
        semantics=("parallel", "arbitrary"), rider=rider)


def _shift_down(u, halo, n):
    rows = lax.broadcasted_iota(jnp.int32, (u.shape[0], 1), 0)
    out = pltpu.roll(u, n, 0)
    for j in range(n):
        out = jnp.where(rows == j, halo[SUBLANES - n + j:SUBLANES - n + j + 1, :], out)
    return out


def _shift_up(u, halo, n):
    tm = u.shape[0]
    rows = lax.broadcasted_iota(jnp.int32, (tm, 1), 0)
    out = pltpu.roll(u, tm - n, 0)
    for j in range(n):
        out = jnp.where(rows == tm - n + j, halo[j:j + 1, :], out)
    return out


def _gates_norms_conv(h, proj, w_g, b_g, gq, gk, conv_w, conv_b):
    t = h.shape[0]
    tm = min(t, 512)
    hb = tm // SUBLANES
    scale = HEAD_DIM ** -0.5 * LOG2E

    def body(h_ref, wg_ref, bgate_ref, q_ref, k_ref, gq_ref, gk_ref, bg_ref, cg_ref, xc_ref, cgh_ref, xch_ref,
             cw_ref, cb_ref, gates_ref, qn_ref, kn_ref, yc_ref):
        i = pl.program_id(0)
        hv = h_ref[...]
        e = _head_sum_matrix()

        def gate_cols(j, width):
            cols = slice(j * width, (j + 1) * width)
            gates_ref[:, cols] = jax.nn.sigmoid(_dot(hv, wg_ref[:, cols]) + bgate_ref[:, cols]).astype(BF16)

        def head_norm(src, g_ref, dst, sc, s):
            sl = slice(s * LANES, (s + 1) * LANES)
            xf = src[:, sl].astype(F32)
            r = lax.rsqrt(_head_sums(xf * xf, e) * (1.0 / HEAD_DIM) + EPS)
            dst[:, sl] = (xf * r * g_ref[:, sl] * sc).astype(BF16)

        def conv_cols(c, width):
            cols = slice(c * width, (c + 1) * width)
            u = cg_ref[:, cols].astype(F32) * xc_ref[:, cols].astype(F32)
            halo = jnp.where(i > 0, cgh_ref[:, cols].astype(F32) * xch_ref[:, cols].astype(F32), 0.0)
            w = cw_ref[:, cols]
            s = w[0:1] * _shift_down(u, halo, 2) + w[1:2] * _shift_down(u, halo, 1) + w[2:3] * u
            yc_ref[:, cols] = (bg_ref[:, cols].astype(F32) * (cb_ref[:, cols] + s)).astype(BF16)

        n_norm = D_MODEL // LANES
        for j in range(n_norm):
            gate_cols(j, 2 * D_MODEL // n_norm)
            head_norm(q_ref, gq_ref, qn_ref, scale, j)
            head_norm(k_ref, gk_ref, kn_ref, 1.0, j)
            if j % 2 == 1:
                conv_cols(j // 2, 2 * D_MODEL // n_norm)

    def slab(col):
        return pl.BlockSpec((tm, D_MODEL), lambda i: (i, col))

    def prev(col):
        return pl.BlockSpec((SUBLANES, D_MODEL), lambda i: (jnp.maximum(i * hb - 1, 0), col))

    vec = pl.BlockSpec((1, D_MODEL), lambda i: (0, 0))
    row = pl.BlockSpec((tm, D_MODEL), lambda i: (i, 0))
    return pl.pallas_call(
        body, name="gates_norms_conv", grid=(t // tm,),
        in_specs=[row, _resident(w_g.shape), pl.BlockSpec((1, 2 * D_MODEL), lambda i: (0, 0)),
                  slab(0), slab(1), vec, vec, slab(3), slab(4), slab(5), prev(4), prev(5),
                  pl.BlockSpec((3, D_MODEL), lambda i: (0, 0)), vec],
        out_specs=[pl.BlockSpec((tm, 2 * D_MODEL), lambda i: (i, 0)), row, row, row],
        out_shape=[jax.ShapeDtypeStruct((t, 2 * D_MODEL), BF16)] + [jax.ShapeDtypeStruct((t, D_MODEL), BF16)] * 3,
        compiler_params=_params(("parallel",)),
    )(h, w_g, b_g, proj, proj, gq, gk, proj, proj, proj, proj, proj, conv_w, conv_b)


def _mix_out(y_attn, y_conv, gates, x, w_ap, w_cp, w_out, g2):
    t = x.shape[0]
    tm = min(t, 512)

    def body(ya_in, yc_in, g_ref, x_ref, wap, wcp, wout, g2_ref, ya_ref, yc_ref, mg_ref, x1_ref, h2_ref):
        ya = _dot(ya_in[...], wap[...])
        yc = _dot(yc_in[...], wcp[...])
        ya_ref[...] = ya.astype(BF16)
        yc_ref[...] = yc.astype(BF16)
        merged = (g_ref[:, :D_MODEL].astype(F32) * ya + g_ref[:, D_MODEL:].astype(F32) * yc).astype(BF16)
        mg_ref[...] = merged
        x1 = x_ref[...] + _dot(merged, wout[...])
        x1_ref[...] = x1
        r = lax.rsqrt(jnp.mean(x1 * x1, axis=-1, keepdims=True) + EPS)
        h2_ref[...] = (x1 * r * g2_ref[...]).astype(BF16)

    row = pl.BlockSpec((tm, D_MODEL), lambda i: (i, 0))
    full = _resident((D_MODEL, D_MODEL))
    return pl.pallas_call(
        body, name="mix_out", grid=(t // tm,),
        in_specs=[row, row, pl.BlockSpec((tm, 2 * D_MODEL), lambda i: (i, 0)), row, full, full, full,
                  pl.BlockSpec((1, D_MODEL), lambda i: (0, 0))],
        out_specs=[row] * 5,
        out_shape=[jax.ShapeDtypeStruct((t, D_MODEL), BF16)] * 3
        + [jax.ShapeDtypeStruct((t, D_MODEL), F32), jax.ShapeDtypeStruct((t, D_MODEL), BF16)],
        compiler_params=_params(("parallel",)),
    )(y_attn, y_conv, gates, x, w_ap, w_cp, w_out, g2)


def _mlp_fwd(h2, w_up, w_down, x1, target):
    t = h2.shape[0]
    tm = min(t, 512)
    tf = 1024
    nf = D_FF // tf

    def body(h2_ref, wup, wdn, x1_ref, tg_ref, a_ref, dy_ref, dyb_ref, loss_ref):
        h2v = h2_ref[...]
        acc = None
        pending = _dot(h2v, wup[:, 0:tf])
        for j in range(nf):
            cols = slice(j * tf, (j + 1) * tf)
            a = pending
            if j + 1 < nf:
                pending = _dot(h2v, wup[:, (j + 1) * tf:(j + 2) * tf])
            a_ref[:, cols] = a.astype(BF16)
            part = _dot(jnp.square(jnp.maximum(a, 0.0)).astype(BF16), wdn[cols, :])
            acc = part if acc is None else acc + part

        @pl.when(pl.program_id(0) == 0)
        def _():
            loss_ref[...] = jnp.zeros_like(loss_ref)

        diff = x1_ref[...] + acc - tg_ref[...]
        loss_ref[...] += _fold8(diff * diff)
        dy = diff * (1.0 / D_MODEL)
        dy_ref[...] = dy
        dyb_ref[...] = dy.astype(BF16)

    row = pl.BlockSpec((tm, D_MODEL), lambda i: (i, 0))
    return pl.pallas_call(
        body, name="mlp_fwd", grid=(t // tm,),
        in_specs=[row, _resident((D_MODEL, D_FF)), _resident((D_FF, D_MODEL)), row, row],
        out_specs=[pl.BlockSpec((tm, D_FF), lambda i: (i, 0)), row, row,
                   pl.BlockSpec((SUBLANES, D_MODEL), lambda i: (0, 0))],
        out_shape=[jax.ShapeDtypeStruct((t, D_FF), BF16), jax.ShapeDtypeStruct((t, D_MODEL), F32),
                   jax.ShapeDtypeStruct((t, D_MODEL), BF16), jax.ShapeDtypeStruct((SUBLANES, D_MODEL), F32)],
        compiler_params=_params(("arbitrary",)),
    )(h2, w_up, w_down, x1, target)


def _rmsnorm_bwd(xf, g, dh):
    r = lax.rsqrt(jnp.mean(xf * xf, axis=-1, keepdims=True) + EPS)
    xh = xf * r
    dxh = dh * g
    dx = r * (dxh - xh * jnp.mean(dxh * xh, axis=-1, keepdims=True))
    return dx, dh * xh


def _mlp_bwd(dyb, a, w_down, w_up, x1, dy, g2):
    t = dyb.shape[0]
    tm = min(t, 512)
    tf = 1024
    nf = D_FF // tf

    def body(dyb_ref, a_ref, wdn, wup, x1_ref, dy_ref, g2_ref, da_ref, dx1_ref, dx1b_ref, dg2_ref):
        dyv = dyb_ref[...]
        acc = None
        pending = _dot_nt(dyv, wdn[0:tf, :])
        for j in range(nf):
            cols = slice(j * tf, (j + 1) * tf)
            du = pending
            if j + 1 < nf:
                pending = _dot_nt(dyv, wdn[(j + 1) * tf:(j + 2) * tf, :])
            da = (du * (2.0 * jnp.maximum(a_ref[:, cols].astype(F32), 0.0))).astype(BF16)
            da_ref[:, cols] = da
            part = _dot_nt(da, wup[:, cols])
            acc = part if acc is None else acc + part

        @pl.when(pl.program_id(0) == 0)
        def _():
            dg2_ref[...] = jnp.zeros_like(dg2_ref)

        dx, dg = _rmsnorm_bwd(x1_ref[...], g2_ref[...], acc)
        dx1 = dy_ref[...] + dx
        dx1_ref[...] = dx1
        dx1b_ref[...] = dx1.astype(BF16)
        dg2_ref[...] += _fold8(dg)

    row = pl.BlockSpec((tm, D_MODEL), lambda i: (i, 0))
    wide = pl.BlockSpec((tm, D_FF), lambda i: (i, 0))
    return pl.pallas_call(
        body, name="mlp_bwd", grid=(t // tm,),
        in_specs=[row, wide, _resident((D_FF, D_MODEL)), _resident((D_MODEL, D_FF)), row, row,
                  pl.BlockSpec((1, D_MODEL), lambda i: (0, 0))],
        out_specs=[wide, row, row, pl.BlockSpec((SUBLANES, D_MODEL), lambda i: (0, 0))],
        out_shape=[jax.ShapeDtypeStruct((t, D_FF), BF16), jax.ShapeDtypeStruct((t, D_MODEL), F32),
                   jax.ShapeDtypeStruct((t, D_MODEL), BF16), jax.ShapeDtypeStruct((SUBLANES, D_MODEL), F32)],
        compiler_params=_params(("arbitrary",)),
    )(dyb, a, w_down, w_up, x1, dy, g2)


def _wgrad(name, lhs, rhs_list, rhs_slabs, relu_sq=False, token_block=2048):
    t, m = lhs.shape
    tt = min(t, token_block)
    tmo = min(m, 1024)
    n_slab = sum(rhs_slabs)
    starts = [sum(rhs_slabs[:n]) for n in range(len(rhs_slabs))]
    n_rhs = len(rhs_list)

    def body(*refs):
        l_ref, r_refs, o_ref, acc = refs[0], refs[1:1 + n_rhs], refs[1 + n_rhs], refs[2 + n_rhs]
        k, s = pl.program_id(1), pl.program_id(2)
        lv = l_ref[...]
        if relu_sq:
            lv = jnp.square(jnp.maximum(lv.astype(F32), 0.0)).astype(BF16)

        @pl.when(s == 0)
        def _():
            acc[...] = jnp.zeros_like(acc)

        for n in range(n_rhs):
            @pl.when((k >= starts[n]) & (k < starts[n] + rhs_slabs[n]))
            def _(n=n):
                acc[...] += _dot_tn(lv, r_refs[n][...])

        @pl.when(s == pl.num_programs(2) - 1)
        def _():
            o_ref[...] = acc[...].astype(BF16)

    def rhs_spec(n):
        lo, cnt = starts[n], rhs_slabs[n]

        def index(i, k, s):
            inside = (k >= lo) & (k < lo + cnt)
            return (jnp.where(inside, s, 0), jnp.clip(k - lo, 0, cnt - 1))
        return pl.BlockSpec((tt, D_MODEL), index)

    return pl.pallas_call(
        body, name=name, grid=(m // tmo, n_slab, t // tt),
        in_specs=[pl.BlockSpec((tt, tmo), lambda i, k, s: (s, i))] + [rhs_spec(n) for n in range(n_rhs)],
        out_specs=pl.BlockSpec((tmo, D_MODEL), lambda i, k, s: (i, k)),
        out_shape=jax.ShapeDtypeStruct((m, n_slab * D_MODEL), BF16),
        scratch_shapes=[pltpu.VMEM((tmo, D_MODEL), F32)],
        compiler_params=_params(("parallel", "parallel", "arbitrary")),
    )(lhs, *rhs_list)


def _wgrad_group(name, triples):
    t = triples[0][0].shape[0]
    tt = min(t, 1024)
    counts = [n for _, _, n in triples]
    starts = [sum(counts[:n]) for n in range(len(counts))]
    n_prod = len(triples)

    def inside(n, k):
        return (k >= starts[n]) & (k < starts[n] + counts[n])

    def body(*refs):
        l_refs, r_refs, o_refs = refs[:n_prod], refs[n_prod:2 * n_prod], refs[2 * n_prod:3 * n_prod]
        acc = refs[3 * n_prod]
        k, s = pl.program_id(0), pl.program_id(1)

        @pl.when(s == 0)
        def _():
            acc[...] = jnp.zeros_like(acc)

        for n in range(n_prod):
            @pl.when(inside(n, k))
            def _(n=n):
                acc[...] += _dot_tn(l_refs[n][...], r_refs[n][...])

            @pl.when(inside(n, k) & (s == pl.num_programs(1) - 1))
            def _(n=n):
                o_refs[n][...] = acc[...].astype(BF16)

    def lhs_spec(n):
        return pl.BlockSpec((tt, D_MODEL), lambda k, s: (jnp.where(inside(n, k), s, 0), 0))

    def rhs_spec(n):
        return pl.BlockSpec((tt, D_MODEL), lambda k, s: (jnp.where(inside(n, k), s, 0),
                                                         jnp.clip(k - starts[n], 0, counts[n] - 1)))

    def out_spec(n):
        return pl.BlockSpec((D_MODEL, D_MODEL), lambda k, s: (0, jnp.clip(k - starts[n], 0, counts[n] - 1)))

    return pl.pallas_call(
        body, name=name, grid=(sum(counts), t // tt),
        in_specs=[lhs_spec(n) for n in range(n_prod)] + [rhs_spec(n) for n in range(n_prod)],
        out_specs=[out_spec(n) for n in range(n_prod)],
        out_shape=[jax.ShapeDtypeStruct((D_MODEL, n * D_MODEL), BF16) for n in counts],
        scratch_shapes=[pltpu.VMEM((D_MODEL, D_MODEL), F32)],
        compiler_params=_params(("arbitrary", "arbitrary")),
    )(*[tr[0] for tr in triples], *[tr[1] for tr in triples])


def _mix_bwd(dx1b, gates, ya, yc, w_out, w_ap, w_cp, w_g, rider=None):
    t = dx1b.shape[0]
    tm = min(t, 512)

    def body(dx_ref, g_ref, ya_ref, yc_ref, wout, wap, wcp, wg,
             dgp_ref, dya_ref, dyc_ref, dyat_ref, dycv_ref, dhg_ref, dbg_ref):
        dm = _dot_nt(dx_ref[...], wout[...])
        ga = g_ref[:, :D_MODEL].astype(F32)
        gc = g_ref[:, D_MODEL:].astype(F32)
        dya = (dm * ga).astype(BF16)
        dyc = (dm * gc).astype(BF16)
        dya_ref[...] = dya
        dyc_ref[...] = dyc
        dgpa = dm * ya_ref[...].astype(F32) * ga * (1.0 - ga)
        dgpc = dm * yc_ref[...].astype(F32) * gc * (1.0 - gc)

        @pl.when(pl.program_id(0) == 0)
        def _():
            dbg_ref[...] = jnp.zeros_like(dbg_ref)

        dbg_ref[:, :D_MODEL] += _fold8(dgpa)
        dbg_ref[:, D_MODEL:] += _fold8(dgpc)
        dgpa = dgpa.astype(BF16)
        dgpc = dgpc.astype(BF16)
        dgp_ref[:, :D_MODEL] = dgpa
        dgp_ref[:, D_MODEL:] = dgpc
        dyat_ref[...] = _dot_nt(dya, wap[...]).astype(BF16)
        dycv_ref[...] = _dot_nt(dyc, wcp[...]).astype(BF16)
        dhg_ref[...] = _dot_nt(dgpa, wg[:, :D_MODEL]) + _dot_nt(dgpc, wg[:, D_MODEL:])

    row = pl.BlockSpec((tm, D_MODEL), lambda i: (i, 0))
    row2 = pl.BlockSpec((tm, 2 * D_MODEL), lambda i: (i, 0))
    full = _resident((D_MODEL, D_MODEL))
    return _call(
        body, name="mix_bwd", grid=(t // tm,), args=(dx1b, gates, ya, yc, w_out, w_ap, w_cp, w_g),
        in_specs=[row, row2, row, row, full, full, full, _resident((D_MODEL, 2 * D_MODEL))],
        out_specs=[row2, row, row, row, row, row, pl.BlockSpec((SUBLANES, 2 * D_MODEL), lambda i: (0, 0))],
        out_shape=[jax.ShapeDtypeStruct((t, 2 * D_MODEL), BF16)] + [jax.ShapeDtypeStruct((t, D_MODEL), BF16)] * 4
        + [jax.ShapeDtypeStruct((t, D_MODEL), F32), jax.ShapeDtypeStruct((SUBLANES, 2 * D_MODEL), F32)],
        semantics=("arbitrary",), rider=rider)


def _conv_bwd(dyconv, proj, conv_w, conv_b, rider=None):
    t = proj.shape[0]
    tm = min(t, 512)
    hb = tm // SUBLANES
    last = t // SUBLANES - 1

    def body(dy_ref, dyn_ref, bg_ref, bgn_ref, cg_ref, cgp_ref, xc_ref, xcp_ref, w_ref, b_ref,
             o_ref, dcb_ref, dcw_ref):
        i = pl.program_id(0)
        cg = cg_ref[...].astype(F32)
        xc = xc_ref[...].astype(F32)
        bg = bg_ref[...].astype(F32)
        u = cg * xc
        prev = jnp.where(i > 0, cgp_ref[...].astype(F32) * xcp_ref[...].astype(F32), 0.0)
        u1 = _shift_down(u, prev, 1)
        u2 = _shift_down(u, prev, 2)
        w = w_ref[...]
        conv = b_ref[...] + (w[0:1] * u2 + w[1:2] * u1 + w[2:3] * u)
        dy = dy_ref[...].astype(F32)
        dconv = dy * bg
        nxt = jnp.where(i < pl.num_programs(0) - 1, dyn_ref[...].astype(F32) * bgn_ref[...].astype(F32), 0.0)
        du = w[2:3] * dconv + w[1:2] * _shift_up(dconv, nxt, 1) + w[0:1] * _shift_up(dconv, nxt, 2)
        o_ref[:, :D_MODEL] = (dy * conv).astype(BF16)
        o_ref[:, D_MODEL:2 * D_MODEL] = (du * xc).astype(BF16)
        o_ref[:, 2 * D_MODEL:] = (du * cg).astype(BF16)

        @pl.when(i == 0)
        def _():
            dcb_ref[...] = jnp.zeros_like(dcb_ref)
            dcw_ref[...] = jnp.zeros_like(dcw_ref)

        dcb_ref[...] += _fold8(dconv)
        dcw_ref[0:SUBLANES] += _fold8(dconv * u2)
        dcw_ref[SUBLANES:2 * SUBLANES] += _fold8(dconv * u1)
        dcw_ref[2 * SUBLANES:] += _fold8(dconv * u)

    def prev(col):
        return pl.BlockSpec((SUBLANES, D_MODEL), lambda i: (jnp.maximum(i * hb - 1, 0), col))

    def nxt(col):
        return pl.BlockSpec((SUBLANES, D_MODEL), lambda i: (jnp.minimum((i + 1) * hb, last), col))

    def cur(col):
        return pl.BlockSpec((tm, D_MODEL), lambda i: (i, col))

    return _call(
        body, name="conv_bwd", grid=(t // tm,),
        args=(dyconv, dyconv, proj, proj, proj, proj, proj, proj, conv_w, conv_b),
        in_specs=[cur(0), nxt(0), cur(3), nxt(3), cur(4), prev(4), cur(5), prev(5),
                  pl.BlockSpec((3, D_MODEL), lambda i: (0, 0)), pl.BlockSpec((1, D_MODEL), lambda i: (0, 0))],
        out_specs=[pl.BlockSpec((tm, 3 * D_MODEL), lambda i: (i, 0)),
                   pl.BlockSpec((SUBLANES, D_MODEL), lambda i: (0, 0)),
                   pl.BlockSpec((3 * SUBLANES, D_MODEL), lambda i: (0, 0))],
        out_shape=[jax.ShapeDtypeStruct((t, 3 * D_MODEL), BF16), jax.ShapeDtypeStruct((SUBLANES, D_MODEL), F32),
                   jax.ShapeDtypeStruct((3 * SUBLANES, D_MODEL), F32)],
        semantics=("arbitrary",), rider=rider)


def _attn_bwd(qn, kn, proj, dyattn, y_attn, lse, bias, rider=None):
    t = qn.shape[0]
    nb = t // QB
    v_col0 = 2 * D_MODEL // SLAB

    def body(q_ref, k0, k1, k2, v0, v1, v2, do_ref, o_ref, lse_ref, bias_ref,
             dq_ref, dk_ref, dv_ref, db_ref, acck, accv):
        b = pl.program_id(1)

        @pl.when(b == 0)
        def _():
            acck[...] = jnp.zeros_like(acck)
            accv[...] = jnp.zeros_like(accv)
            db_ref[...] = jnp.zeros_like(db_ref)

        def block(valid):
            head_a = lax.broadcasted_iota(jnp.int32, (1, LANES), 1) < HEAD_DIM

            def window(refs, hp):
                sl = slice(hp * LANES, (hp + 1) * LANES)
                return jnp.concatenate([r[:, sl] for r in refs], axis=0)

            def transposed(x, hh):
                return x.astype(F32).T.astype(BF16)[hh * HEAD_DIM:(hh + 1) * HEAD_DIM]

            def probs(head):
                hp, hh = divmod(head, 2)
                sl = slice(hp * LANES, (hp + 1) * LANES)
                mine = head_a if hh == 0 else jnp.logical_not(head_a)
                k = window((k0, k1, k2), hp)
                s = _dot_nt(jnp.where(mine, k, jnp.zeros_like(k)), q_ref[:, sl]) + bias_ref[head]
                s = s if valid is None else jnp.where(valid, s, NEG_INF)
                return jnp.exp2(s - lse_ref[head:head + 1, :])

            def grads(head, p):
                hp, hh = divmod(head, 2)
                sl = slice(hp * LANES, (hp + 1) * LANES)
                rows = slice(hh * HEAD_DIM, (hh + 1) * HEAD_DIM)
                mine = head_a if hh == 0 else jnp.logical_not(head_a)
                do = do_ref[:, sl]
                v = window((v0, v1, v2), hp)
                delta = jnp.sum((do.astype(F32).T * o_ref[:, sl].astype(F32).T)[rows], axis=0, keepdims=True)
                ds = p * (_dot_nt(jnp.where(mine, v, jnp.zeros_like(v)), do) - delta)
                db_ref[head] += ds
                pb, dsb = p.astype(BF16), ds.astype(BF16)
                dvt = _dot_nt(transposed(do, hh), pb)
                dkt = _dot_nt(transposed(q_ref[:, sl], hh), dsb) * (1.0 / LOG2E)
                dqt = _dot(transposed(window((k0, k1, k2), hp), hh), dsb)
                return dqt, dkt, dvt

            out = []
            pending = probs(0)
            for head in range(2 * PAIRS):
                nxt = probs(head + 1) if head + 1 < 2 * PAIRS else None
                out.append(grads(head, pending))
                pending = nxt
            for hp in range(PAIRS):
                sl = slice(hp * LANES, (hp + 1) * LANES)
                dqt, dkt, dvt = (jnp.concatenate([out[2 * hp][n], out[2 * hp + 1][n]], axis=0) for n in range(3))
                dq_ref[:, sl] = dqt.T.astype(BF16)
                for w in range(3):
                    slot = lax.rem(b + w + 1, 3)
                    cols = slice(w * QB, (w + 1) * QB)
                    if w == 2:
                        acck[hp, slot] = dkt[:, cols]
                        accv[hp, slot] = dvt[:, cols]
                    else:
                        acck[hp, slot] += dkt[:, cols]
                        accv[hp, slot] += dvt[:, cols]

        @pl.when(b < 2)
        def _():
            block(lax.broadcasted_iota(jnp.int32, (KW, 1), 0) >= (2 - b) * QB)

        @pl.when((b >= 2) & (b < nb))
        def _():
            block(None)

        done = lax.rem(b + 1, 3)
        for hp in range(PAIRS):
            sl = slice(hp * LANES, (hp + 1) * LANES)
            dk_ref[:, sl] = acck[hp, done].T.astype(BF16)
            dv_ref[:, sl] = accv[hp, done].T.astype(BF16)

    def cur(p, b):
        return (jnp.minimum(b, nb - 1), p)

    def window(col0):
        return [pl.BlockSpec((QB, SLAB), functools.partial(
            lambda p, b, back: (jnp.maximum(jnp.minimum(b, nb - 1) - back, 0), col0 + p), back=back))
            for back in (2, 1, 0)]

    def done_block(p, b):
        return (jnp.maximum(b - 2, 0), p)

    tile = pl.BlockSpec((2 * PAIRS, KW, QB), lambda p, b: (p, 0, 0))
    here = pl.BlockSpec((QB, SLAB), cur)
    return _call(
        body, name="attn_bwd", grid=(D_MODEL // SLAB, nb + 2),
        args=(qn, kn, kn, kn, proj, proj, proj, dyattn, y_attn, lse, bias),
        in_specs=[here] + window(0) + window(v_col0)
        + [here, here, pl.BlockSpec((2 * PAIRS, QB), lambda p, b: (p, jnp.minimum(b, nb - 1))), tile],
        out_specs=[here, pl.BlockSpec((QB, SLAB), done_block), pl.BlockSpec((QB, SLAB), done_block), tile],
        out_shape=[jax.ShapeDtypeStruct((t, D_MODEL), BF16)] * 3 + [jax.ShapeDtypeStruct((N_HEADS, KW, QB), F32)],
        scratch_shapes=[pltpu.VMEM((PAIRS, 3, LANES, QB), F32), pltpu.VMEM((PAIRS, 3, LANES, QB), F32)],
        semantics=("parallel", "arbitrary"), rider=rider)


def _qknorm_bwd(proj, dqn, dkn, gq, gk):
    t = proj.shape[0]
    tm = min(t, 512)
    scale = HEAD_DIM ** -0.5

    def body(q_ref, k_ref, dqn_ref, dkn_ref, gq_ref, gk_ref, o_ref, dgq_ref, dgk_ref):
        e = _head_sum_matrix()

        @pl.when(pl.program_id(0) == 0)
        def _():
            dgq_ref[...] = jnp.zeros_like(dgq_ref)
            dgk_ref[...] = jnp.zeros_like(dgk_ref)

        for n, (src, dn_ref, g_ref, dg_ref, sc) in enumerate(
                ((q_ref, dqn_ref, gq_ref, dgq_ref, scale), (k_ref, dkn_ref, gk_ref, dgk_ref, 1.0))):
            for s in range(D_MODEL // LANES):
                sl = slice(s * LANES, (s + 1) * LANES)
                xf = src[:, sl].astype(F32)
                r = lax.rsqrt(_head_sums(xf * xf, e) * (1.0 / HEAD_DIM) + EPS)
                xh = xf * r
                dn = dn_ref[:, sl].astype(F32) * sc
                dg_ref[:, sl] += _fold8(dn * xh)
                dxh = dn * g_ref[:, sl]
                mean = _head_sums(dxh * xh, e) * (1.0 / HEAD_DIM)
                o_ref[:, n * D_MODEL + s * LANES:n * D_MODEL + (s + 1) * LANES] = (r * (dxh - xh * mean)).astype(BF16)

    row = pl.BlockSpec((tm, D_MODEL), lambda i: (i, 0))
    vec = pl.BlockSpec((1, D_MODEL), lambda i: (0, 0))
    acc = pl.BlockSpec((SUBLANES, D_MODEL), lambda i: (0, 0))
    return pl.pallas_call(
        body, name="qknorm_bwd", grid=(t // tm,),
        in_specs=[row, pl.BlockSpec((tm, D_MODEL), lambda i: (i, 1)), row, row, vec, vec],
        out_specs=[pl.BlockSpec((tm, 2 * D_MODEL), lambda i: (i, 0)), acc, acc],
        out_shape=[jax.ShapeDtypeStruct((t, 2 * D_MODEL), BF16)] + [jax.ShapeDtypeStruct((SUBLANES, D_MODEL), F32)] * 2,
        compiler_params=_params(("arbitrary",)),
    )(proj, proj, dqn, dkn, gq, gk)


def _in_bwd(dqk, dv, dconv, w_in, dhg, x, g1, dx1, rider=None):
    t = x.shape[0]
    tm = min(t, 512)

    def body(dqk_ref, dv_ref, dc_ref, w_ref, dhg_ref, x_ref, g_ref, dx1_ref, dx_ref, dg_ref):
        acc = dhg_ref[...]
        slab = 0
        for src, n in ((dqk_ref, 2), (dv_ref, 1), (dc_ref, 3)):
            for s in range(n):
                acc = acc + _dot_nt(src[:, s * D_MODEL:(s + 1) * D_MODEL],
                                    w_ref[:, slab * D_MODEL:(slab + 1) * D_MODEL])
                slab += 1

        @pl.when(pl.program_id(0) == 0)
        def _():
            dg_ref[...] = jnp.zeros_like(dg_ref)

        dx, dg = _rmsnorm_bwd(x_ref[...], g_ref[...], acc)
        dx_ref[...] = dx1_ref[...] + dx
        dg_ref[...] += _fold8(dg)

    row = pl.BlockSpec((tm, D_MODEL), lambda i: (i, 0))
    return _call(
        body, name="in_bwd", grid=(t // tm,), args=(dqk, dv, dconv, w_in, dhg, x, g1, dx1),
        in_specs=[pl.BlockSpec((tm, 2 * D_MODEL), lambda i: (i, 0)), row,
                  pl.BlockSpec((tm, 3 * D_MODEL), lambda i: (i, 0)),
                  _resident(w_in.shape), row, row, pl.BlockSpec((1, D_MODEL), lambda i: (0, 0)), row],
        out_specs=[row, pl.BlockSpec((SUBLANES, D_MODEL), lambda i: (0, 0))],
        out_shape=[jax.ShapeDtypeStruct((t, D_MODEL), F32), jax.ShapeDtypeStruct((SUBLANES, D_MODEL), F32)],
        semantics=("arbitrary",), rider=rider)


def _bias_grad_fold(dbias, rider=None):
    def body(d_ref, o_ref):
        jj = lax.broadcasted_iota(jnp.int32, (QB, QB), 0)
        ii = lax.broadcasted_iota(jnp.int32, (QB, QB), 1)
        flip = (jj + ii == QB - 1).astype(BF16)
        low = jj + ii < QB
        pos, neg = [], []
        for w in range(KW // QB):
            x = d_ref[0, QB * w:QB * (w + 1), :]
            hi = x.astype(BF16)
            r1 = x - hi.astype(F32)
            mid = r1.astype(BF16)
            lo = (r1 - mid.astype(F32)).astype(BF16)
            xr = _dot(hi, flip) + _dot(mid, flip) + _dot(lo, flip)
            for keep, acc in ((low, pos), (jnp.logical_not(low), neg)):
                part = pltpu.roll(jnp.where(keep, xr, 0.0), 0, 1, stride=1, stride_axis=0)
                acc.append(jnp.sum(part, axis=0, keepdims=True))
        far = pos[1] + neg[0] + pos[0]
        o_ref[0] = jnp.zeros((SUBLANES, QB), F32)
        o_ref[0, 0:1, :] = neg[2]
        o_ref[0, 1:2, :] = pos[2] + neg[1]
        o_ref[0, 2:3, :] = jnp.broadcast_to(jnp.sum(far, axis=-1, keepdims=True), (1, QB))

    return _call(
        body, name="bias_grad_fold", grid=(N_HEADS,), args=(dbias,),
        in_specs=[pl.BlockSpec((1, KW, QB), lambda h: (h, 0, 0))],
        out_specs=[pl.BlockSpec((1, SUBLANES, QB), lambda h: (h, 0, 0))],
        out_shape=[jax.ShapeDtypeStruct((N_HEADS, SUBLANES, QB), F32)],
        semantics=("parallel",), rider=rider)


def _small_partials(dg1, dgq, dgk, dcb, dcw, dbg, dg2, dbias_fold, loss_tile):
    def head_fold(v):
        acc = v[:, 0:LANES]
        for s in range(1, D_MODEL // LANES):
            acc = acc + v[:, s * LANES:(s + 1) * LANES]
        return acc + pltpu.roll(acc, HEAD_DIM, 1)

    def body(dg1_ref, dgq_ref, dgk_ref, dcb_ref, dcw_ref, dbg_ref, dg2_ref, db_ref, loss_ref, o_ref):
        o_ref[...] = jnp.zeros_like(o_ref)
        o_ref[0:1, :] = jnp.sum(dg1_ref[...], axis=0, keepdims=True)
        o_ref[1:2, 0:LANES] = head_fold(jnp.sum(dgq_ref[...], axis=0, keepdims=True))
        o_ref[2:3, 0:LANES] = head_fold(jnp.sum(dgk_ref[...], axis=0, keepdims=True))
        o_ref[3:4, :] = jnp.sum(dcb_ref[...], axis=0, keepdims=True)
        for j in range(3):
            o_ref[4 + j:5 + j, :] = jnp.sum(dcw_ref[j * SUBLANES:(j + 1) * SUBLANES, :], axis=0, keepdims=True)
        o_ref[7:8, :] = jnp.sum(dbg_ref[:, :D_MODEL], axis=0, keepdims=True)
        o_ref[8:9, :] = jnp.sum(dbg_ref[:, D_MODEL:], axis=0, keepdims=True)
        o_ref[9:10, :] = jnp.sum(dg2_ref[...], axis=0, keepdims=True)
        for h in range(N_HEADS):
            for part in range(3):
                o_ref[10 + h:11 + h, part * QB:(part + 1) * QB] = db_ref[h, part:part + 1, :]
        loss = (0.5 / D_MODEL) * jnp.sum(jnp.sum(loss_ref[...], axis=0, keepdims=True), axis=-1, keepdims=True)
        o_ref[26:27, :] = jnp.broadcast_to(loss, (1, D_MODEL))

    return pl.pallas_call(
        body, name="small_partials",
        out_shape=jax.ShapeDtypeStruct((32, D_MODEL), F32),
        compiler_params=_params(),
    )(dg1, dgq, dgk, dcb, dcw, dbg, dg2, dbias_fold, loss_tile)


MID_AXES = (0, 0, 1, 0)
MLP_AXES = (1, 0)


def _local_step(x, h, target, norm1_g, q_norm_g, k_norm_g, bias, conv_w, conv_b, b_gate, norm2_g,
                w_in, mid_w, mlp_w, distributed):
    g1 = norm1_g.reshape(1, D_MODEL)
    g2 = norm2_g.reshape(1, D_MODEL)
    gq = jnp.tile(q_norm_g, N_HEADS).reshape(1, D_MODEL)
    gk = jnp.tile(k_norm_g, N_HEADS).reshape(1, D_MODEL)
    cb = conv_b.reshape(1, D_MODEL)

    (proj,), got = _in_proj(h, w_in, rider=_Gather(mid_w, MID_AXES) if distributed else None)
    w_ap, w_cp, w_g, w_out = got if distributed else mid_w
    gates, qn, kn, y_conv = _gates_norms_conv(h, proj, w_g, b_gate.reshape(1, 2 * D_MODEL), gq, gk, conv_w, cb)
    (y_attn, lse), got = _attn_fwd(qn, kn, proj, bias, rider=_Gather(mlp_w, MLP_AXES) if distributed else None)
    w_up, w_down = got if distributed else mlp_w
    ya, yc, merged, x1, h2 = _mix_out(y_attn, y_conv, gates, x, w_ap, w_cp, w_out, g2)
    a, dy, dyb, loss_tile = _mlp_fwd(h2, w_up, w_down, x1, target)

    da, dx1, dx1b, dg2 = _mlp_bwd(dyb, a, w_down, w_up, x1, dy, g2)
    gw_down = _wgrad("wgrad_down", a, [dyb], [1], relu_sq=True, token_block=4096)
    gw_up = _wgrad("wgrad_up", h2, [da], [D_FF // D_MODEL])
    (dgp, dya, dyc, dyattn, dyconv, dhg, dbg), mlp_swapped = _mix_bwd(
        dx1b, gates, ya, yc, w_out, w_ap, w_cp, w_g,
        rider=_PairSwap((gw_up, gw_down), MLP_AXES) if distributed else None)
    mid = tuple(_wgrad_group("wgrad_mid", [(y_attn, dya, 1), (y_conv, dyc, 1), (h, dgp, 2), (merged, dx1b, 1)]))
    gw_ap, gw_cp, gw_g, gw_out = mid
    (dconv, dcb, dcw), mid_swapped = _conv_bwd(
        dyconv, proj, conv_w, cb, rider=_PairSwap(mid, MID_AXES) if distributed else None)
    early = mid + (gw_up, gw_down)
    early_sums = (_pair_add(early, tuple(mid_swapped) + tuple(mlp_swapped), MID_AXES + MLP_AXES)
                  if distributed else None)
    (dqn, dkn, dv, dbias), early_shares = _attn_bwd(
        qn, kn, proj, dyattn, y_attn, lse, bias, rider=_ChipScatter(early_sums) if distributed else None)
    dqk, dgq, dgk = _qknorm_bwd(proj, dqn, dkn, gq, gk)
    gw_in = _wgrad("wgrad_in", h, [dqk, dv, dconv], [2, 1, 3])
    (dbias_fold,), in_swapped = _bias_grad_fold(dbias, rider=_PairSwap((gw_in,), (1,)) if distributed else None)
    in_sums = _pair_add((gw_in,), in_swapped, (1,)) if distributed else None
    (dx, dg1), in_shares = _in_bwd(dqk, dv, dconv, w_in, dhg, x, g1, dx1,
                                   rider=_ChipScatter(in_sums) if distributed else None)
    small = _small_partials(dg1, dgq, dgk, dcb, dcw, dbg, dg2, dbias_fold, loss_tile)
    grads = tuple(in_shares) + tuple(early_shares) if distributed else (gw_in,) + early
    return dx, grads, small


def _me():
    return lax.axis_index("x"), lax.axis_index("y"), lax.axis_index("c")


def _peer(me, rel):
    x, y, c = me
    return (1 - x if rel & 4 else x, 1 - y if rel & 2 else y, 1 - c if rel & 1 else c)


def _linear(dev):
    return 4 * dev[0] + 2 * dev[1] + dev[2]


def _block(ref, axis, idx, size):
    return ref.at[pl.ds(idx * size, size), :] if axis == 0 else ref.at[:, pl.ds(idx * size, size)]


def _cast_shards(shards):
    def body(*refs):
        for src, dst in zip(refs[:len(shards)], refs[len(shards):]):
            dst[...] = src[...].astype(BF16)

    return pl.pallas_call(
        body, name="cast_shards",
        out_shape=[jax.ShapeDtypeStruct(s.shape, BF16) for s in shards],
        compiler_params=_params(),
    )(*shards)


class _Gather:
    def __init__(self, shards, axes):
        self.arrays, self.axes, self.n = list(shards), tuple(axes), len(shards)
        self.sizes = [s.shape[axis] for s, axis in zip(shards, axes)]
        self.out_shape = []
        for s, axis in zip(shards, axes):
            shape = (s.shape[0] * N_DEV, s.shape[1]) if axis == 0 else (s.shape[0], s.shape[1] * N_DEV)
            self.out_shape.append(jax.ShapeDtypeStruct(shape, s.dtype))
        self.scratch = [pltpu.SemaphoreType.DMA((self.n, 7)), pltpu.SemaphoreType.DMA((self.n, 7)),
                        pltpu.SemaphoreType.DMA((self.n,))]

    def _copies(self, srcs, outs, sems):
        send_sems, recv_sems, local_sems = sems
        me = _me()
        sibling = _peer(me, 1)
        chips = [_peer(me, rel) for rel in (4, 2, 6)]

        def rows(a, dev):
            return _block(outs[a], self.axes[a], _linear(dev), self.sizes[a])

        def copy(a, k, block_dev, to, src=None):
            return pltpu.make_async_remote_copy(
                src_ref=rows(a, block_dev) if src is None else src, dst_ref=rows(a, block_dev),
                send_sem=send_sems.at[a, k], recv_sem=recv_sems.at[a, k], device_id=to, device_id_type=MESH_T)

        own = [pltpu.make_async_copy(srcs[a], rows(a, me), local_sems.at[a]) for a in range(self.n)]
        first = []
        for a in range(self.n):
            first.append(copy(a, 0, me, sibling, src=srcs[a]))
            for j, chip in enumerate(chips):
                first.append(copy(a, 1 + j, me, chip, src=srcs[a]))
        return me, sibling, chips, copy, own, first

    def start(self, srcs, outs, sems):
        _, _, _, _, own, first = self._copies(srcs, outs, sems)
        for cp in own + first:
            cp.start()

    def finish(self, srcs, outs, sems):
        me, sibling, chips, copy, own, first = self._copies(srcs, outs, sems)
        passed = []
        for a in range(self.n):
            for j, chip in enumerate(chips):
                copy(a, 1 + j, chip, me).wait_recv()
                fwd = copy(a, 4 + j, chip, sibling)
                fwd.start()
                passed.append(fwd)
        for a in range(self.n):
            copy(a, 0, sibling, me).wait_recv()
            for j, chip in enumerate(chips):
                copy(a, 4 + j, _peer(chip, 1), me).wait_recv()
        for cp in first + passed:
            cp.wait_send()
        for cp in own:
            cp.wait()


N_CHIPS = 4


def _shard_shape(g, axis):
    return (g.shape[0] // N_DEV, g.shape[1]) if axis == 0 else (g.shape[0], g.shape[1] // N_DEV)


class _PairSwap:
    def __init__(self, grads, axes):
        self.arrays, self.axes, self.n = list(grads), tuple(axes), len(grads)
        self.sizes = [g.shape[axis] // N_DEV for g, axis in zip(grads, axes)]
        self.out_shape = [jax.ShapeDtypeStruct((N_CHIPS,) + _shard_shape(g, axis), g.dtype)
                          for g, axis in zip(grads, axes)]
        self.scratch = [pltpu.SemaphoreType.DMA((self.n, N_CHIPS)), pltpu.SemaphoreType.DMA((self.n, N_CHIPS))]

    def _copies(self, srcs, outs, sems):
        send_sems, recv_sems = sems
        x, y, c = _me()
        sibling = (x, y, 1 - c)
        copies = []
        for a in range(self.n):
            for chip in range(N_CHIPS):
                owner_idx = 2 * chip + (1 - c)
                copies.append(pltpu.make_async_remote_copy(
                    src_ref=_block(srcs[a], self.axes[a], owner_idx, self.sizes[a]), dst_ref=outs[a].at[chip],
                    send_sem=send_sems.at[a, chip], recv_sem=recv_sems.at[a, chip],
                    device_id=sibling, device_id_type=MESH_T))
        return copies

    def start(self, srcs, outs, sems):
        for cp in self._copies(srcs, outs, sems):
            cp.start()

    def finish(self, srcs, outs, sems):
        for cp in self._copies(srcs, outs, sems):
            cp.wait()


def _pair_add(grads, swapped, axes):
    n = len(grads)
    c_arr = lax.axis_index("c").astype(jnp.int32).reshape(1)

    def body(c_ref, *refs):
        del c_ref
        mine, got, outs = refs[:n], refs[n:2 * n], refs[2 * n:]
        for a in range(n):
            outs[a][0] = (mine[a][...].astype(F32) + got[a][0].astype(F32)).astype(BF16)

    in_specs, out_specs, out_shape = [], [], []
    for g, axis in zip(grads, axes):
        shard = _shard_shape(g, axis)
        if axis == 0:
            in_specs.append(pl.BlockSpec(shard, lambda s, c_ref: (2 * s + c_ref[0], 0)))
        else:
            in_specs.append(pl.BlockSpec(shard, lambda s, c_ref: (0, 2 * s + c_ref[0])))
    for g, axis in zip(grads, axes):
        shard = _shard_shape(g, axis)
        in_specs.append(pl.BlockSpec((1,) + shard, lambda s, c_ref: (s, 0, 0)))
        out_specs.append(pl.BlockSpec((1,) + shard, lambda s, c_ref: (s, 0, 0)))
        out_shape.append(jax.ShapeDtypeStruct((N_CHIPS,) + shard, BF16))
    return pl.pallas_call(
        body, name="pair_add_" + str(n),
        grid_spec=pltpu.PrefetchScalarGridSpec(num_scalar_prefetch=1, grid=(N_CHIPS,), in_specs=in_specs,
                                               out_specs=out_specs),
        out_shape=out_shape, compiler_params=_params(("arbitrary",)),
    )(c_arr, *grads, *swapped)


class _ChipScatter:
    def __init__(self, sums):
        self.arrays, self.n = list(sums), len(sums)
        self.out_shape = [jax.ShapeDtypeStruct(s.shape, s.dtype) for s in sums]
        self.scratch = [pltpu.SemaphoreType.DMA((self.n, 3)), pltpu.SemaphoreType.DMA((self.n, 3)),
                        pltpu.SemaphoreType.DMA((self.n,))]

    def _copies(self, srcs, outs, sems):
        send_sems, recv_sems, local_sems = sems
        me = _me()
        my_chip = 2 * me[0] + me[1]
        own = [pltpu.make_async_copy(srcs[a].at[my_chip], outs[a].at[my_chip], local_sems.at[a])
               for a in range(self.n)]
        sends, recvs = [], []
        for a in range(self.n):
            for k, rel in enumerate((4, 2, 6)):
                peer = _peer(me, rel)
                peer_chip = 2 * peer[0] + peer[1]
                sends.append(pltpu.make_async_remote_copy(
                    src_ref=srcs[a].at[peer_chip], dst_ref=outs[a].at[my_chip],
                    send_sem=send_sems.at[a, k], recv_sem=recv_sems.at[a, k], device_id=peer, device_id_type=MESH_T))
                recvs.append(pltpu.make_async_remote_copy(
                    src_ref=srcs[a].at[my_chip], dst_ref=outs[a].at[peer_chip],
                    send_sem=send_sems.at[a, k], recv_sem=recv_sems.at[a, k], device_id=peer, device_id_type=MESH_T))
        return own, sends, recvs

    def start(self, srcs, outs, sems):
        own, sends, _ = self._copies(srcs, outs, sems)
        for cp in own + sends:
            cp.start()

    def finish(self, srcs, outs, sems):
        own, sends, recvs = self._copies(srcs, outs, sems)
        for cp in recvs:
            cp.wait_recv()
        for cp in sends:
            cp.wait_send()
        for cp in own:
            cp.wait()


def _call(body, *, name, args, in_specs, out_specs, out_shape, grid=(), scratch_shapes=(), semantics=None,
          rider=None):
    if rider is None:
        return pl.pallas_call(
            body, name=name, grid=grid, in_specs=in_specs, out_specs=out_specs, out_shape=out_shape,
            scratch_shapes=list(scratch_shapes), compiler_params=_params(semantics))(*args), None
    n_in, n_out, n_scr, r = len(in_specs), len(out_specs), len(scratch_shapes), rider.n

    def wrapped(*refs):
        ins, r_ins = refs[:n_in], refs[n_in:n_in + r]
        outs = refs[n_in + r:n_in + r + n_out]
        r_outs = refs[n_in + r + n_out:n_in + 2 * r + n_out]
        scr = refs[n_in + 2 * r + n_out:n_in + 2 * r + n_out + n_scr]
        sems = refs[n_in + 2 * r + n_out + n_scr:]
        first, last = None, None
        for ax in range(len(grid)):
            f, l = pl.program_id(ax) == 0, pl.program_id(ax) == pl.num_programs(ax) - 1
            first = f if first is None else first & f
            last = l if last is None else last & l
        if first is None:
            rider.start(r_ins, r_outs, sems)
            body(*ins, *outs, *scr)
            rider.finish(r_ins, r_outs, sems)
            return

        @pl.when(first)
        def _():
            rider.start(r_ins, r_outs, sems)

        body(*ins, *outs, *scr)

        @pl.when(last)
        def _():
            rider.finish(r_ins, r_outs, sems)

    any_spec = pl.BlockSpec(memory_space=pl.ANY)
    out = pl.pallas_call(
        wrapped, name=name, grid=grid, in_specs=list(in_specs) + [any_spec] * r,
        out_specs=list(out_specs) + [any_spec] * r, out_shape=list(out_shape) + rider.out_shape,
        scratch_shapes=list(scratch_shapes) + rider.scratch,
        compiler_params=_params(None if semantics is None else ("arbitrary",) * len(semantics)),
    )(*args, *rider.arrays)
    return out[:n_out], out[n_out:]


def _all_reduce_small(part):
    def body(p_ref, o_ref, slots, send_sems, recv_sems):
        me = _me()
        my_idx = _linear(me)
        slots[my_idx] = p_ref[...]
        sends = []
        for rel in range(1, N_DEV):
            cp = pltpu.make_async_remote_copy(
                src_ref=p_ref, dst_ref=slots.at[my_idx], send_sem=send_sems.at[rel - 1],
                recv_sem=recv_sems.at[rel - 1], device_id=_peer(me, rel), device_id_type=MESH_T)
            cp.start()
            sends.append(cp)
        for rel in range(1, N_DEV):
            frm = _peer(me, rel)
            pltpu.make_async_remote_copy(
                src_ref=p_ref, dst_ref=slots.at[_linear(frm)], send_sem=send_sems.at[rel - 1],
                recv_sem=recv_sems.at[rel - 1], device_id=frm, device_id_type=MESH_T).wait_recv()
        for cp in sends:
            cp.wait_send()
        total = slots[0]
        for d in range(1, N_DEV):
            total = total + slots[d]
        o_ref[...] = total

    return pl.pallas_call(
        body, name="all_reduce_small",
        in_specs=[pl.BlockSpec(memory_space=pltpu.VMEM)], out_specs=pl.BlockSpec(memory_space=pltpu.VMEM),
        out_shape=jax.ShapeDtypeStruct(part.shape, F32),
        scratch_shapes=[pltpu.VMEM((N_DEV,) + part.shape, F32), pltpu.SemaphoreType.DMA((7,)),
                        pltpu.SemaphoreType.DMA((7,))],
        compiler_params=_params(),
    )(part)


def _adamw_math(w, g, m, v):
    m = ADAM_B1 * m + (1.0 - ADAM_B1) * g
    v = ADAM_B2 * v + (1.0 - ADAM_B2) * jnp.square(g)
    m_hat = m / (1.0 - ADAM_B1 ** ADAM_STEP)
    v_hat = v / (1.0 - ADAM_B2 ** ADAM_STEP)
    delta = -ADAM_LR * (m_hat / (jnp.sqrt(v_hat) + ADAM_EPS) + ADAM_WD * w)
    return delta, m, v


ADAMW_STEPS = 4


def _adamw_big(shares, ws, ms, vs, rider=None):
    n = len(ws)

    def body(*refs):
        s_refs, w_refs, m_refs, v_refs = (refs[a * n:(a + 1) * n] for a in range(4))
        outs = refs[4 * n:]
        for a in range(n):
            g = s_refs[a][0].astype(F32)
            for d in range(1, N_CHIPS):
                g = g + s_refs[a][d].astype(F32)
            outs[4 * a][...] = g
            outs[4 * a + 1][...], outs[4 * a + 2][...], outs[4 * a + 3][...] = _adamw_math(
                w_refs[a][...], g, m_refs[a][...], v_refs[a][...])

    def chunk(w):
        return pl.BlockSpec((w.shape[0] // ADAMW_STEPS, w.shape[1]), lambda i: (i, 0))

    def share_chunk(w):
        return pl.BlockSpec((N_CHIPS, w.shape[0] // ADAMW_STEPS, w.shape[1]), lambda i: (0, i, 0))

    out, rider_out = _call(
        body, name="adamw_big", grid=(ADAMW_STEPS,), args=(*shares, *ws, *ms, *vs),
        in_specs=[share_chunk(w) for w in ws] + [chunk(w) for w in ws] * 3,
        out_specs=[chunk(w) for w in ws for _ in range(4)],
        out_shape=[jax.ShapeDtypeStruct(w.shape, F32) for w in ws for _ in range(4)],
        semantics=("parallel",), rider=rider)
    return [tuple(out[4 * a:4 * a + 4]) for a in range(n)], rider_out


def _adamw_small(quads):
    n = len(quads)

    def body(*refs):
        ins, outs = refs[:4 * n], refs[4 * n:]
        for p in range(n):
            g_ref, w_ref, m_ref, v_ref = ins[4 * p:4 * p + 4]
            d_ref, nm_ref, nv_ref = outs[3 * p:3 * p + 3]
            d_ref[...], nm_ref[...], nv_ref[...] = _adamw_math(w_ref[...], g_ref[...], m_ref[...], v_ref[...])

    flat = [a for quad in quads for a in quad]
    out = pl.pallas_call(
        body, name="adamw_small",
        out_shape=[jax.ShapeDtypeStruct(quad[1].shape, F32) for quad in quads for _ in range(3)],
        compiler_params=_params(),
    )(*flat)
    return [tuple(out[3 * p:3 * p + 3]) for p in range(n)]


def kernel(x, norm1_g, w_in, q_norm_g, k_norm_g, rel_bias, conv_w, conv_b, w_attn_proj, w_conv_proj, w_gate, b_gate, w_out, norm2_g, w_up, w_down, loss_target, m_norm1_g, m_w_in, m_q_norm_g, m_k_norm_g, m_rel_bias, m_conv_w, m_conv_b, m_w_attn_proj, m_w_conv_proj, m_w_gate, m_b_gate, m_w_out, m_norm2_g, m_w_up, m_w_down, v_norm1_g, v_w_in, v_q_norm_g, v_k_norm_g, v_rel_bias, v_conv_w, v_conv_b, v_w_attn_proj, v_w_conv_proj, v_w_gate, v_b_gate, v_w_out, v_norm2_g, v_w_up, v_w_down):
    my_idx = _linear(_me())
    big_w = (w_in, w_attn_proj, w_conv_proj, w_gate, w_out, w_up, w_down)
    big_m = (m_w_in, m_w_attn_proj, m_w_conv_proj, m_w_gate, m_w_out, m_w_up, m_w_down)
    big_v = (v_w_in, v_w_attn_proj, v_w_conv_proj, v_w_gate, v_w_out, v_w_up, v_w_down)
    big_names = ("w_in", "w_attn_proj", "w_conv_proj", "w_gate", "w_out", "w_up", "w_down")

    conv_w_tile = jnp.pad(conv_w, ((0, SUBLANES - conv_w.shape[0]), (0, 0)))
    (w_in_shard,) = _cast_shards(big_w[:1])
    (bias, h, *shards), (w_in_full, conv_w_rows) = _bias_tiles(
        rel_bias, x[0], norm1_g.reshape(1, D_MODEL), big_w[1:],
        rider=_Gather((w_in_shard, conv_w_tile), (1, 1)))

    dx, shares, small = _local_step(x[0], h, loss_target[0], norm1_g, q_norm_g, k_norm_g, bias, conv_w_rows[:3],
                                    conv_b, b_gate, norm2_g, w_in_full, tuple(shards[0:4]), tuple(shards[4:6]), True)

    big_out, _ = _adamw_big(shares, big_w, big_m, big_v)
    tot = _all_reduce_small(small)
    g_rel_bias = jnp.concatenate(
        [tot[10:26, :QB][:, ::-1], tot[10:26, QB:2 * QB][:, ::-1], tot[10:26, 2 * QB:2 * QB + 1]], axis=1)
    g_conv_w = lax.dynamic_slice(tot[4:7], (0, my_idx * LANES), (3, LANES))
    small_g = [tot[0:1], tot[1:2, :HEAD_DIM], tot[2:3, :HEAD_DIM], g_rel_bias, g_conv_w, tot[3:4],
               tot[7:9].reshape(1, 2 * D_MODEL), tot[9:10]]
    small_w = (norm1_g, q_norm_g, k_norm_g, rel_bias, conv_w, conv_b, b_gate, norm2_g)
    small_m = (m_norm1_g, m_q_norm_g, m_k_norm_g, m_rel_bias, m_conv_w, m_conv_b, m_b_gate, m_norm2_g)
    small_v = (v_norm1_g, v_q_norm_g, v_k_norm_g, v_rel_bias, v_conv_w, v_conv_b, v_b_gate, v_norm2_g)

    def two_d(a):
        return a.reshape(1, -1) if a.ndim == 1 else a

    small_out = _adamw_small([(g, two_d(w), two_d(m), two_d(v))
                              for g, w, m, v in zip(small_g, small_w, small_m, small_v)])

    order = ("norm1_g", "w_in", "q_norm_g", "k_norm_g", "rel_bias", "conv_w", "conv_b", "w_attn_proj", "w_conv_proj",
             "w_gate", "b_gate", "w_out", "norm2_g", "w_up", "w_down")
    small_names = ("norm1_g", "q_norm_g", "k_norm_g", "rel_bias", "conv_w", "conv_b", "b_gate", "norm2_g")
    res = {}
    for name, (g, d, nm, nv) in zip(big_names, big_out):
        res[name] = (g, d, nm, nv)
    for name, g, w, (d, nm, nv) in zip(small_names, small_g, small_w, small_out):
        res[name] = tuple(a.reshape(w.shape) for a in (g, d, nm, nv))
    loss = tot[26, 0]
    return (loss, dx[None], *[res[n][0] for n in order], *[res[n][1] for n in order],
            *[res[n][2] for n in order], *[res[n][3] for n in order])
```

```python
import functools

import jax
import jax.numpy as jnp
from jax import lax
from jax.experimental import pallas as pl
from jax.experimental.pallas import tpu as pltpu

F32 = jnp.float32
BF16 = jnp.bfloat16

D_MODEL = 1024
N_HEADS = 16
HEAD_DIM = 64
CHUNK = 64
N_PREV_CHUNKS = 8
MAX_REL = 256
D_FF = 4096
EPS = 1e-6
NEG_INF = -1e30
LOG2E = 1.4426950408889634
N_DEV = 8

ADAM_LR = 0.001
ADAM_B1 = 0.9
ADAM_B2 = 0.999
ADAM_EPS = 1e-08
ADAM_WD = 0.01
ADAM_STEP = 10

LANES = 128
SUBLANES = 8
VMEM_LIMIT = 56 * 1024 * 1024
QB = 256
KW = 3 * QB
PAIRS = 4
SLAB = PAIRS * LANES
PAIRS_FWD = 8
SKEW = 1024

MESH_T = pl.DeviceIdType.MESH


def _dot(a, b):
    return jnp.dot(a, b, preferred_element_type=F32)


def _dot_nt(a, b):
    return lax.dot_general(a, b, (((1,), (1,)), ((), ())), preferred_element_type=F32)


def _dot_tn(a, b):
    return lax.dot_general(a, b, (((0,), (0,)), ((), ())), preferred_element_type=F32)


def _params(sem=None):
    return pltpu.CompilerParams(dimension_semantics=sem, vmem_limit_bytes=VMEM_LIMIT)


def _resident(shape):
    return pl.BlockSpec(shape, lambda *_: (0,) * len(shape), pipeline_mode=pl.Buffered(1))


def _fold8(v):
    rows, n = v.shape
    return v.reshape(rows // SUBLANES, SUBLANES, n).sum(axis=0)


def _head_sum_matrix():
    r = lax.broadcasted_iota(jnp.int32, (LANES, LANES), 0) // HEAD_DIM
    c = lax.broadcasted_iota(jnp.int32, (LANES, LANES), 1) // HEAD_DIM
    return (r == c).astype(BF16)


def _head_sums(v, e):
    hi = v.astype(BF16)
    lo = (v - hi.astype(F32)).astype(BF16)
    return _dot(hi, e) + _dot(lo, e)


def _in_proj(h, w_in, rider=None):
    t = h.shape[0]
    tm = min(t, 512)
    n_out = w_in.shape[1]

    def body(h_ref, w_ref, proj_ref):
        h = h_ref[...]
        for k in range(n_out // D_MODEL):
            cols = slice(k * D_MODEL, (k + 1) * D_MODEL)
            proj_ref[:, cols] = _dot(h, w_ref[:, cols]).astype(BF16)

    return _call(
        body, name="in_proj", grid=(t // tm,), args=(h, w_in),
        in_specs=[pl.BlockSpec((tm, D_MODEL), lambda i: (i, 0)), _resident((D_MODEL, n_out))],
        out_specs=[pl.BlockSpec((tm, n_out), lambda i: (i, 0))],
        out_shape=[jax.ShapeDtypeStruct((t, n_out), BF16)],
        semantics=("parallel",), rider=rider)


def _bias_tiles(rel_bias, x, g1, later_shards=(), rider=None):
    t = x.shape[0]
    rows = t // N_HEADS
    n_later = len(later_shards)
    by_dist = jnp.concatenate(
        [rel_bias[:, :2 * MAX_REL], jnp.broadcast_to(rel_bias[:, 2 * MAX_REL:], (N_HEADS, 2 * MAX_REL))], axis=1)
    by_dist = by_dist.reshape(N_HEADS, 1, SKEW)

    def body(f_ref, x_ref, g_ref, *refs):
        src_refs, o_ref, h_ref = refs[:n_later], refs[n_later], refs[n_later + 1]
        dst_refs, buf_refs = refs[n_later + 2:2 * n_later + 2], refs[2 * n_later + 2:3 * n_later + 2]

        def fetch(a):
            return pltpu.make_async_copy(src_refs[a], buf_refs[a], refs[-1].at[a])

        @pl.when(pl.program_id(0) == 1)
        def _():
            for a in range(n_later):
                fetch(a).start()

        @pl.when(pl.program_id(0) == N_HEADS - 2)
        def _():
            for a in range(n_later):
                fetch(a).wait()
                dst_refs[a][...] = buf_refs[a][...].astype(BF16)

        xf = x_ref[...]
        r = lax.rsqrt(jnp.mean(xf * xf, axis=-1, keepdims=True) + EPS)
        h_ref[...] = (xf * r * g_ref[...]).astype(BF16)
        jj = lax.broadcasted_iota(jnp.int32, (QB, QB), 0)
        ii = lax.broadcasted_iota(jnp.int32, (QB, QB), 1)
        for w in range(KW // QB):
            pos = jnp.broadcast_to(f_ref[0, :, KW - QB * w:KW - QB * w + QB], (QB, QB))
            neg = jnp.broadcast_to(f_ref[0, :, KW - QB * (w + 1):KW - QB * w], (QB, QB))
            pos = pltpu.roll(pos, 0, 1, stride=1, stride_axis=0)
            neg = pltpu.roll(neg, 0, 1, stride=1, stride_axis=0)
            tile = jnp.where(ii >= jj, pos, neg)
            kc = (jj + QB * w) // CHUNK
            qc = ii // CHUNK
            band = (kc >= qc) & (kc <= qc + N_PREV_CHUNKS)
            o_ref[0, QB * w:QB * (w + 1), :] = jnp.where(band, tile * LOG2E, NEG_INF)

    def whole(s):
        return pl.BlockSpec(s.shape, lambda h: (0, 0))

    return _call(
        body, name="bias_tiles", grid=(N_HEADS,), args=(by_dist, x, g1, *later_shards),
        in_specs=[pl.BlockSpec((1, 1, SKEW), lambda h: (h, 0, 0)),
                  pl.BlockSpec((rows, D_MODEL), lambda h: (h, 0)),
                  pl.BlockSpec((1, D_MODEL), lambda h: (0, 0))]
        + [pl.BlockSpec(memory_space=pl.ANY)] * n_later,
        out_specs=[pl.BlockSpec((1, KW, QB), lambda h: (h, 0, 0)),
                   pl.BlockSpec((rows, D_MODEL), lambda h: (h, 0))] + [whole(s) for s in later_shards],
        out_shape=[jax.ShapeDtypeStruct((N_HEADS, KW, QB), F32), jax.ShapeDtypeStruct((t, D_MODEL), BF16)]
        + [jax.ShapeDtypeStruct(s.shape, BF16) for s in later_shards],
        scratch_shapes=[pltpu.VMEM(s.shape, F32) for s in later_shards]
        + ([pltpu.SemaphoreType.DMA((n_later,))] if n_later else []),
        semantics=("arbitrary",), rider=rider)


def _window_specs(col0, slab):
    return [pl.BlockSpec((QB, slab), functools.partial(
        lambda p, b, back: (jnp.maximum(b - back, 0), col0 + p), back=back)) for back in (2, 1, 0)]


def _attn_fwd(qn, kn, proj, bias, rider=None):
    t = qn.shape[0]
    nb = t // QB
    pairs = PAIRS_FWD
    slab = pairs * LANES
    v_col0 = 2 * D_MODEL // slab

    def body(q_ref, k0, k1, k2, v0, v1, v2, bias_ref, o_ref, lse_ref):
        b = pl.program_id(1)

        @pl.when(b < 2)
        def _():
            compute(q_ref, k0, k1, k2, v0, v1, v2, bias_ref, o_ref, lse_ref,
                    lax.broadcasted_iota(jnp.int32, (KW, 1), 0) >= (2 - b) * QB)

        @pl.when(b >= 2)
        def _():
            compute(q_ref, k0, k1, k2, v0, v1, v2, bias_ref, o_ref, lse_ref, None)

    def compute(q_ref, k0, k1, k2, v0, v1, v2, bias_ref, o_ref, lse_ref, valid):
        head_a = lax.broadcasted_iota(jnp.int32, (1, LANES), 1) < HEAD_DIM

        def scores(head):
            hp, hh = divmod(head, 2)
            sl = slice(hp * LANES, (hp + 1) * LANES)
            k = jnp.concatenate([k0[:, sl], k1[:, sl], k2[:, sl]], axis=0)
            mine = head_a if hh == 0 else jnp.logical_not(head_a)
            s = _dot_nt(jnp.where(mine, k, jnp.zeros_like(k)), q_ref[:, sl]) + bias_ref[head]
            return s if valid is None else jnp.where(valid, s, NEG_INF)

        def weighted_values(head, s):
            hp, hh = divmod(head, 2)
            sl = slice(hp * LANES, (hp + 1) * LANES)
            v = jnp.concatenate([v0[:, sl], v1[:, sl], v2[:, sl]], axis=0)
            vt = v.astype(F32).T.astype(BF16)[hh * HEAD_DIM:(hh + 1) * HEAD_DIM]
            vt = jnp.concatenate([vt, jnp.ones((SUBLANES, KW), BF16)], axis=0)
            m = jnp.max(s, axis=0, keepdims=True)
            ov = _dot(vt, jnp.exp2(s - m).astype(BF16))
            l = ov[HEAD_DIM:HEAD_DIM + 1]
            return ov[:HEAD_DIM] / l, m + jnp.log2(l)

        outs, lses = [], []
        pending = scores(0)
        for head in range(2 * pairs):
            nxt = scores(head + 1) if head + 1 < 2 * pairs else None
            o, lse = weighted_values(head, pending)
            outs.append(o)
            lses.append(lse)
            pending = nxt
        for hp in range(pairs):
            sl = slice(hp * LANES, (hp + 1) * LANES)
            o_ref[:, sl] = jnp.concatenate([outs[2 * hp], outs[2 * hp + 1]], axis=0).T.astype(BF16)
        lse_ref[...] = jnp.concatenate(lses, axis=0)

    return _call(
        body, name="attn_fwd", grid=(D_MODEL // slab, nb), args=(qn, kn, kn, kn, proj, proj, proj, bias),
        in_specs=[pl.BlockSpec((QB, slab), lambda p, b: (b, p))] + _window_specs(0, slab)
        + _window_specs(v_col0, slab) + [pl.BlockSpec((2 * pairs, KW, QB), lambda p, b: (p, 0, 0))],
        out_specs=[pl.BlockSpec((QB, slab), lambda p, b: (b, p)),
                   pl.BlockSpec((2 * pairs, QB), lambda p, b: (p, b))],
        out_shape=[jax.ShapeDtypeStruct((t, D_MODEL), BF16), jax.ShapeDtypeStruct((N_HEADS, t), F32)],
        semantics=("parallel", "arbitrary"), rider=rider)


def _shift_down(u, halo, n):
    rows = lax.broadcasted_iota(jnp.int32, (u.shape[0], 1), 0)
    out = pltpu.roll(u, n, 0)
    for j in range(n):
        out = jnp.where(rows == j, halo[SUBLANES - n + j:SUBLANES - n + j + 1, :], out)
    return out


def _shift_up(u, halo, n):
    tm = u.shape[0]
    rows = lax.broadcasted_iota(jnp.int32, (tm, 1), 0)
    out = pltpu.roll(u, tm - n, 0)
    for j in range(n):
        out = jnp.where(rows == tm - n + j, halo[j:j + 1, :], out)
    return out


def _gates_norms_conv(h, proj, w_g, b_g, gq, gk, conv_w, conv_b):
    t = h.shape[0]
    tm = min(t, 512)
    hb = tm // SUBLANES
    scale = HEAD_DIM ** -0.5 * LOG2E

    def body(h_ref, wg_ref, bgate_ref, q_ref, k_ref, gq_ref, gk_ref, bg_ref, cg_ref, xc_ref, cgh_ref, xch_ref,
             cw_ref, cb_ref, gates_ref, qn_ref, kn_ref, yc_ref):
        i = pl.program_id(0)
        hv = h_ref[...]
        e = _head_sum_matrix()

        def gate_cols(j, width):
            cols = slice(j * width, (j + 1) * width)
            gates_ref[:, cols] = jax.nn.sigmoid(_dot(hv, wg_ref[:, cols]) + bgate_ref[:, cols]).astype(BF16)

        def head_norm(src, g_ref, dst, sc, s):
            sl = slice(s * LANES, (s + 1) * LANES)
            xf = src[:, sl].astype(F32)
            r = lax.rsqrt(_head_sums(xf * xf, e) * (1.0 / HEAD_DIM) + EPS)
            dst[:, sl] = (xf * r * g_ref[:, sl] * sc).astype(BF16)

        def conv_cols(c, width):
            cols = slice(c * width, (c + 1) * width)
            u = cg_ref[:, cols].astype(F32) * xc_ref[:, cols].astype(F32)
            halo = jnp.where(i > 0, cgh_ref[:, cols].astype(F32) * xch_ref[:, cols].astype(F32), 0.0)
            w = cw_ref[:, cols]
            s = w[0:1] * _shift_down(u, halo, 2) + w[1:2] * _shift_down(u, halo, 1) + w[2:3] * u
            yc_ref[:, cols] = (bg_ref[:, cols].astype(F32) * (cb_ref[:, cols] + s)).astype(BF16)

        n_norm = D_MODEL // LANES
        for j in range(n_norm):
            gate_cols(j, 2 * D_MODEL // n_norm)
            head_norm(q_ref, gq_ref, qn_ref, scale, j)
            head_norm(k_ref, gk_ref, kn_ref, 1.0, j)
            if j % 2 == 1:
                conv_cols(j // 2, 2 * D_MODEL // n_norm)

    def slab(col):
        return pl.BlockSpec((tm, D_MODEL), lambda i: (i, col))

    def prev(col):
        return pl.BlockSpec((SUBLANES, D_MODEL), lambda i: (jnp.maximum(i * hb - 1, 0), col))

    vec = pl.BlockSpec((1, D_MODEL), lambda i: (0, 0))
    row = pl.BlockSpec((tm, D_MODEL), lambda i: (i, 0))
    return pl.pallas_call(
        body, name="gates_norms_conv", grid=(t // tm,),
        in_specs=[row, _resident(w_g.shape), pl.BlockSpec((1, 2 * D_MODEL), lambda i: (0, 0)),
                  slab(0), slab(1), vec, vec, slab(3), slab(4), slab(5), prev(4), prev(5),
                  pl.BlockSpec((3, D_MODEL), lambda i: (0, 0)), vec],
        out_specs=[pl.BlockSpec((tm, 2 * D_MODEL), lambda i: (i, 0)), row, row, row],
        out_shape=[jax.ShapeDtypeStruct((t, 2 * D_MODEL), BF16)] + [jax.ShapeDtypeStruct((t, D_MODEL), BF16)] * 3,
        compiler_params=_params(("parallel",)),
    )(h, w_g, b_g, proj, proj, gq, gk, proj, proj, proj, proj, proj, conv_w, conv_b)


def _mix_out(y_attn, y_conv, gates, x, w_ap, w_cp, w_out, g2):
    t = x.shape[0]
    tm = min(t, 512)

    def body(ya_in, yc_in, g_ref, x_ref, wap, wcp, wout, g2_ref, ya_ref, yc_ref, mg_ref, x1_ref, h2_ref):
        ya = _dot(ya_in[...], wap[...])
        yc = _dot(yc_in[...], wcp[...])
        ya_ref[...] = ya.astype(BF16)
        yc_ref[...] = yc.astype(BF16)
        merged = (g_ref[:, :D_MODEL].astype(F32) * ya + g_ref[:, D_MODEL:].astype(F32) * yc).astype(BF16)
        mg_ref[...] = merged
        x1 = x_ref[...] + _dot(merged, wout[...])
        x1_ref[...] = x1
        r = lax.rsqrt(jnp.mean(x1 * x1, axis=-1, keepdims=True) + EPS)
        h2_ref[...] = (x1 * r * g2_ref[...]).astype(BF16)

    row = pl.BlockSpec((tm, D_MODEL), lambda i: (i, 0))
    full = _resident((D_MODEL, D_MODEL))
    return pl.pallas_call(
        body, name="mix_out", grid=(t // tm,),
        in_specs=[row, row, pl.BlockSpec((tm, 2 * D_MODEL), lambda i: (i, 0)), row, full, full, full,
                  pl.BlockSpec((1, D_MODEL), lambda i: (0, 0))],
        out_specs=[row] * 5,
        out_shape=[jax.ShapeDtypeStruct((t, D_MODEL), BF16)] * 3
        + [jax.ShapeDtypeStruct((t, D_MODEL), F32), jax.ShapeDtypeStruct((t, D_MODEL), BF16)],
        compiler_params=_params(("parallel",)),
    )(y_attn, y_conv, gates, x, w_ap, w_cp, w_out, g2)


def _mlp_fwd(h2, w_up, w_down, x1, target):
    t = h2.shape[0]
    tm = min(t, 512)
    tf = 1024
    nf = D_FF // tf

    def body(h2_ref, wup, wdn_hbm, x1_ref, tg_ref, a_ref, dy_ref, dyb_ref, loss_ref, wdn, sem):
        first = pl.program_id(0) == 0
        fetch = pltpu.make_async_copy(wdn_hbm, wdn, sem.at[0])

        @pl.when(first)
        def _():
            fetch.start()

        h2v = h2_ref[...]
        acc = None
        pending = _dot(h2v, wup[:, 0:tf])
        for j in range(nf):
            cols = slice(j * tf, (j + 1) * tf)
            a = pending
            if j + 1 < nf:
                pending = _dot(h2v, wup[:, (j + 1) * tf:(j + 2) * tf])
            a_ref[:, cols] = a.astype(BF16)
            if j == 0:
                @pl.when(first)
                def _():
                    fetch.wait()
            part = _dot(jnp.square(jnp.maximum(a, 0.0)).astype(BF16), wdn[cols, :])
            acc = part if acc is None else acc + part

        @pl.when(pl.program_id(0) == 0)
        def _():
            loss_ref[...] = jnp.zeros_like(loss_ref)

        diff = x1_ref[...] + acc - tg_ref[...]
        loss_ref[...] += _fold8(diff * diff)
        dy = diff * (1.0 / D_MODEL)
        dy_ref[...] = dy
        dyb_ref[...] = dy.astype(BF16)

    row = pl.BlockSpec((tm, D_MODEL), lambda i: (i, 0))
    return pl.pallas_call(
        body, name="mlp_fwd", grid=(t // tm,),
        in_specs=[row, _resident((D_MODEL, D_FF)), pl.BlockSpec(memory_space=pl.ANY), row, row],
        out_specs=[pl.BlockSpec((tm, D_FF), lambda i: (i, 0)), row, row,
                   pl.BlockSpec((SUBLANES, D_MODEL), lambda i: (0, 0))],
        out_shape=[jax.ShapeDtypeStruct((t, D_FF), BF16), jax.ShapeDtypeStruct((t, D_MODEL), F32),
                   jax.ShapeDtypeStruct((t, D_MODEL), BF16), jax.ShapeDtypeStruct((SUBLANES, D_MODEL), F32)],
        scratch_shapes=[pltpu.VMEM((D_FF, D_MODEL), BF16), pltpu.SemaphoreType.DMA((1,))],
        compiler_params=_params(("arbitrary",)),
    )(h2, w_up, w_down, x1, target)


def _rmsnorm_bwd(xf, g, dh):
    r = lax.rsqrt(jnp.mean(xf * xf, axis=-1, keepdims=True) + EPS)
    xh = xf * r
    dxh = dh * g
    dx = r * (dxh - xh * jnp.mean(dxh * xh, axis=-1, keepdims=True))
    return dx, dh * xh


def _mlp_bwd(dyb, a, w_down, w_up, x1, dy, g2):
    t = dyb.shape[0]
    tm = min(t, 512)
    tf = 1024
    nf = D_FF // tf

    def body(dyb_ref, a_ref, wdn, wup_hbm, x1_ref, dy_ref, g2_ref, da_ref, dx1_ref, dx1b_ref, dg2_ref, wup, sem):
        first = pl.program_id(0) == 0
        fetch = pltpu.make_async_copy(wup_hbm, wup, sem.at[0])

        @pl.when(first)
        def _():
            fetch.start()

        dyv = dyb_ref[...]
        acc = None
        pending = _dot_nt(dyv, wdn[0:tf, :])
        for j in range(nf):
            cols = slice(j * tf, (j + 1) * tf)
            du = pending
            if j + 1 < nf:
                pending = _dot_nt(dyv, wdn[(j + 1) * tf:(j + 2) * tf, :])
            da = (du * (2.0 * jnp.maximum(a_ref[:, cols].astype(F32), 0.0))).astype(BF16)
            da_ref[:, cols] = da
            if j == 0:
                @pl.when(first)
                def _():
                    fetch.wait()
            part = _dot_nt(da, wup[:, cols])
            acc = part if acc is None else acc + part

        @pl.when(pl.program_id(0) == 0)
        def _():
            dg2_ref[...] = jnp.zeros_like(dg2_ref)

        dx, dg = _rmsnorm_bwd(x1_ref[...], g2_ref[...], acc)
        dx1 = dy_ref[...] + dx
        dx1_ref[...] = dx1
        dx1b_ref[...] = dx1.astype(BF16)
        dg2_ref[...] += _fold8(dg)

    row = pl.BlockSpec((tm, D_MODEL), lambda i: (i, 0))
    wide = pl.BlockSpec((tm, D_FF), lambda i: (i, 0))
    return pl.pallas_call(
        body, name="mlp_bwd", grid=(t // tm,),
        scratch_shapes=[pltpu.VMEM((D_MODEL, D_FF), BF16), pltpu.SemaphoreType.DMA((1,))],
        in_specs=[row, wide, _resident((D_FF, D_MODEL)), pl.BlockSpec(memory_space=pl.ANY), row, row,
                  pl.BlockSpec((1, D_MODEL), lambda i: (0, 0))],
        out_specs=[wide, row, row, pl.BlockSpec((SUBLANES, D_MODEL), lambda i: (0, 0))],
        out_shape=[jax.ShapeDtypeStruct((t, D_FF), BF16), jax.ShapeDtypeStruct((t, D_MODEL), F32),
                   jax.ShapeDtypeStruct((t, D_MODEL), BF16), jax.ShapeDtypeStruct((SUBLANES, D_MODEL), F32)],
        compiler_params=_params(("arbitrary",)),
    )(dyb, a, w_down, w_up, x1, dy, g2)


def _wgrad(name, lhs, rhs_list, rhs_slabs, relu_sq=False, token_block=2048):
    t, m = lhs.shape
    tt = min(t, token_block)
    tmo = min(m, 1024)
    n_slab = sum(rhs_slabs)
    starts = [sum(rhs_slabs[:n]) for n in range(len(rhs_slabs))]
    n_rhs = len(rhs_list)

    def body(*refs):
        l_ref, r_refs, o_ref, acc = refs[0], refs[1:1 + n_rhs], refs[1 + n_rhs], refs[2 + n_rhs]
        k, s = pl.program_id(1), pl.program_id(2)
        lv = l_ref[...]
        if relu_sq:
            lv = jnp.square(jnp.maximum(lv.astype(F32), 0.0)).astype(BF16)

        @pl.when(s == 0)
        def _():
            acc[...] = jnp.zeros_like(acc)

        for n in range(n_rhs):
            @pl.when((k >= starts[n]) & (k < starts[n] + rhs_slabs[n]))
            def _(n=n):
                acc[...] += _dot_tn(lv, r_refs[n][...])

        @pl.when(s == pl.num_programs(2) - 1)
        def _():
            o_ref[...] = acc[...].astype(BF16)

    def rhs_spec(n):
        lo, cnt = starts[n], rhs_slabs[n]

        def index(i, k, s):
            inside = (k >= lo) & (k < lo + cnt)
            return (jnp.where(inside, s, 0), jnp.clip(k - lo, 0, cnt - 1))
        return pl.BlockSpec((tt, D_MODEL), index)

    return pl.pallas_call(
        body, name=name, grid=(m // tmo, n_slab, t // tt),
        in_specs=[pl.BlockSpec((tt, tmo), lambda i, k, s: (s, i))] + [rhs_spec(n) for n in range(n_rhs)],
        out_specs=pl.BlockSpec((tmo, D_MODEL), lambda i, k, s: (i, k)),
        out_shape=jax.ShapeDtypeStruct((m, n_slab * D_MODEL), BF16),
        scratch_shapes=[pltpu.VMEM((tmo, D_MODEL), F32)],
        compiler_params=_params(("parallel", "parallel", "arbitrary")),
    )(lhs, *rhs_list)


def _wgrad_group(name, triples):
    t = triples[0][0].shape[0]
    tt = min(t, 1024)
    counts = [n for _, _, n in triples]
    starts = [sum(counts[:n]) for n in range(len(counts))]
    n_prod = len(triples)

    def inside(n, k):
        return (k >= starts[n]) & (k < starts[n] + counts[n])

    def body(*refs):
        l_refs, r_refs, o_refs = refs[:n_prod], refs[n_prod:2 * n_prod], refs[2 * n_prod:3 * n_prod]
        acc = refs[3 * n_prod]
        k, s = pl.program_id(0), pl.program_id(1)

        @pl.when(s == 0)
        def _():
            acc[...] = jnp.zeros_like(acc)

        for n in range(n_prod):
            @pl.when(inside(n, k))
            def _(n=n):
                acc[...] += _dot_tn(l_refs[n][...], r_refs[n][...])

            @pl.when(inside(n, k) & (s == pl.num_programs(1) - 1))
            def _(n=n):
                o_refs[n][...] = acc[...].astype(BF16)

    def lhs_spec(n):
        return pl.BlockSpec((tt, D_MODEL), lambda k, s: (jnp.where(inside(n, k), s, 0), 0))

    def rhs_spec(n):
        return pl.BlockSpec((tt, D_MODEL), lambda k, s: (jnp.where(inside(n, k), s, 0),
                                                         jnp.clip(k - starts[n], 0, counts[n] - 1)))

    def out_spec(n):
        return pl.BlockSpec((D_MODEL, D_MODEL), lambda k, s: (0, jnp.clip(k - starts[n], 0, counts[n] - 1)))

    return pl.pallas_call(
        body, name=name, grid=(sum(counts), t // tt),
        in_specs=[lhs_spec(n) for n in range(n_prod)] + [rhs_spec(n) for n in range(n_prod)],
        out_specs=[out_spec(n) for n in range(n_prod)],
        out_shape=[jax.ShapeDtypeStruct((D_MODEL, n * D_MODEL), BF16) for n in counts],
        scratch_shapes=[pltpu.VMEM((D_MODEL, D_MODEL), F32)],
        compiler_params=_params(("arbitrary", "arbitrary")),
    )(*[tr[0] for tr in triples], *[tr[1] for tr in triples])


def _mix_bwd(dx1b, gates, ya, yc, w_out, w_ap, w_cp, w_g, rider=None):
    t = dx1b.shape[0]
    tm = min(t, 512)

    def body(dx_ref, g_ref, ya_ref, yc_ref, wout, wap, wcp, wg,
             dgp_ref, dya_ref, dyc_ref, dyat_ref, dycv_ref, dhg_ref, dbg_ref):
        dm = _dot_nt(dx_ref[...], wout[...])
        ga = g_ref[:, :D_MODEL].astype(F32)
        gc = g_ref[:, D_MODEL:].astype(F32)
        dya = (dm * ga).astype(BF16)
        dyc = (dm * gc).astype(BF16)
        dya_ref[...] = dya
        dyc_ref[...] = dyc
        dgpa = dm * ya_ref[...].astype(F32) * ga * (1.0 - ga)
        dgpc = dm * yc_ref[...].astype(F32) * gc * (1.0 - gc)

        @pl.when(pl.program_id(0) == 0)
        def _():
            dbg_ref[...] = jnp.zeros_like(dbg_ref)

        dbg_ref[:, :D_MODEL] += _fold8(dgpa)
        dbg_ref[:, D_MODEL:] += _fold8(dgpc)
        dgpa = dgpa.astype(BF16)
        dgpc = dgpc.astype(BF16)
        dgp_ref[:, :D_MODEL] = dgpa
        dgp_ref[:, D_MODEL:] = dgpc
        dyat_ref[...] = _dot_nt(dya, wap[...]).astype(BF16)
        dycv_ref[...] = _dot_nt(dyc, wcp[...]).astype(BF16)
        dhg_ref[...] = _dot_nt(dgpa, wg[:, :D_MODEL]) + _dot_nt(dgpc, wg[:, D_MODEL:])

    row = pl.BlockSpec((tm, D_MODEL), lambda i: (i, 0))
    row2 = pl.BlockSpec((tm, 2 * D_MODEL), lambda i: (i, 0))
    full = _resident((D_MODEL, D_MODEL))
    return _call(
        body, name="mix_bwd", grid=(t // tm,), args=(dx1b, gates, ya, yc, w_out, w_ap, w_cp, w_g),
        in_specs=[row, row2, row, row, full, full, full, _resident((D_MODEL, 2 * D_MODEL))],
        out_specs=[row2, row, row, row, row, row, pl.BlockSpec((SUBLANES, 2 * D_MODEL), lambda i: (0, 0))],
        out_shape=[jax.ShapeDtypeStruct((t, 2 * D_MODEL), BF16)] + [jax.ShapeDtypeStruct((t, D_MODEL), BF16)] * 4
        + [jax.ShapeDtypeStruct((t, D_MODEL), F32), jax.ShapeDtypeStruct((SUBLANES, 2 * D_MODEL), F32)],
        semantics=("arbitrary",), rider=rider)


def _conv_bwd(dyconv, proj, conv_w, conv_b, rider=None):
    t = proj.shape[0]
    tm = min(t, 512)
    hb = tm // SUBLANES
    last = t // SUBLANES - 1

    def body(dy_ref, dyn_ref, bg_ref, bgn_ref, cg_ref, cgp_ref, xc_ref, xcp_ref, w_ref, b_ref,
             o_ref, dcb_ref, dcw_ref):
        i = pl.program_id(0)
        cg = cg_ref[...].astype(F32)
        xc = xc_ref[...].astype(F32)
        bg = bg_ref[...].astype(F32)
        u = cg * xc
        prev = jnp.where(i > 0, cgp_ref[...].astype(F32) * xcp_ref[...].astype(F32), 0.0)
        u1 = _shift_down(u, prev, 1)
        u2 = _shift_down(u, prev, 2)
        w = w_ref[...]
        conv = b_ref[...] + (w[0:1] * u2 + w[1:2] * u1 + w[2:3] * u)
        dy = dy_ref[...].astype(F32)
        dconv = dy * bg
        nxt = jnp.where(i < pl.num_programs(0) - 1, dyn_ref[...].astype(F32) * bgn_ref[...].astype(F32), 0.0)
        du = w[2:3] * dconv + w[1:2] * _shift_up(dconv, nxt, 1) + w[0:1] * _shift_up(dconv, nxt, 2)
        o_ref[:, :D_MODEL] = (dy * conv).astype(BF16)
        o_ref[:, D_MODEL:2 * D_MODEL] = (du * xc).astype(BF16)
        o_ref[:, 2 * D_MODEL:] = (du * cg).astype(BF16)

        @pl.when(i == 0)
        def _():
            dcb_ref[...] = jnp.zeros_like(dcb_ref)
            dcw_ref[...] = jnp.zeros_like(dcw_ref)

        dcb_ref[...] += _fold8(dconv)
        dcw_ref[0:SUBLANES] += _fold8(dconv * u2)
        dcw_ref[SUBLANES:2 * SUBLANES] += _fold8(dconv * u1)
        dcw_ref[2 * SUBLANES:] += _fold8(dconv * u)

    def prev(col):
        return pl.BlockSpec((SUBLANES, D_MODEL), lambda i: (jnp.maximum(i * hb - 1, 0), col))

    def nxt(col):
        return pl.BlockSpec((SUBLANES, D_MODEL), lambda i: (jnp.minimum((i + 1) * hb, last), col))

    def cur(col):
        return pl.BlockSpec((tm, D_MODEL), lambda i: (i, col))

    return _call(
        body, name="conv_bwd", grid=(t // tm,),
        args=(dyconv, dyconv, proj, proj, proj, proj, proj, proj, conv_w, conv_b),
        in_specs=[cur(0), nxt(0), cur(3), nxt(3), cur(4), prev(4), cur(5), prev(5),
                  pl.BlockSpec((3, D_MODEL), lambda i: (0, 0)), pl.BlockSpec((1, D_MODEL), lambda i: (0, 0))],
        out_specs=[pl.BlockSpec((tm, 3 * D_MODEL), lambda i: (i, 0)),
                   pl.BlockSpec((SUBLANES, D_MODEL), lambda i: (0, 0)),
                   pl.BlockSpec((3 * SUBLANES, D_MODEL), lambda i: (0, 0))],
        out_shape=[jax.ShapeDtypeStruct((t, 3 * D_MODEL), BF16), jax.ShapeDtypeStruct((SUBLANES, D_MODEL), F32),
                   jax.ShapeDtypeStruct((3 * SUBLANES, D_MODEL), F32)],
        semantics=("arbitrary",), rider=rider)


def _attn_bwd(qn, kn, proj, dyattn, y_attn, lse, bias, rider=None):
    t = qn.shape[0]
    nb = t // QB
    v_col0 = 2 * D_MODEL // SLAB

    def body(q_ref, k0, k1, k2, v0, v1, v2, do_ref, o_ref, lse_ref, bias_ref,
             dq_ref, dk_ref, dv_ref, db_ref, acck, accv):
        b = pl.program_id(1)

        @pl.when(b == 0)
        def _():
            acck[...] = jnp.zeros_like(acck)
            accv[...] = jnp.zeros_like(accv)
            db_ref[...] = jnp.zeros_like(db_ref)

        def block(valid):
            head_a = lax.broadcasted_iota(jnp.int32, (1, LANES), 1) < HEAD_DIM

            def window(refs, hp):
                sl = slice(hp * LANES, (hp + 1) * LANES)
                return jnp.concatenate([r[:, sl] for r in refs], axis=0)

            def transposed(x, hh):
                return x.astype(F32).T.astype(BF16)[hh * HEAD_DIM:(hh + 1) * HEAD_DIM]

            def probs(head):
                hp, hh = divmod(head, 2)
                sl = slice(hp * LANES, (hp + 1) * LANES)
                mine = head_a if hh == 0 else jnp.logical_not(head_a)
                k = window((k0, k1, k2), hp)
                s = _dot_nt(jnp.where(mine, k, jnp.zeros_like(k)), q_ref[:, sl]) + bias_ref[head]
                s = s if valid is None else jnp.where(valid, s, NEG_INF)
                return jnp.exp2(s - lse_ref[head:head + 1, :])

            def grads(head, p):
                hp, hh = divmod(head, 2)
                sl = slice(hp * LANES, (hp + 1) * LANES)
                rows = slice(hh * HEAD_DIM, (hh + 1) * HEAD_DIM)
                mine = head_a if hh == 0 else jnp.logical_not(head_a)
                do = do_ref[:, sl]
                v = window((v0, v1, v2), hp)
                delta = jnp.sum((do.astype(F32).T * o_ref[:, sl].astype(F32).T)[rows], axis=0, keepdims=True)
                ds = p * (_dot_nt(jnp.where(mine, v, jnp.zeros_like(v)), do) - delta)
                db_ref[head] += ds
                pb, dsb = p.astype(BF16), ds.astype(BF16)
                dvt = _dot_nt(transposed(do, hh), pb)
                dkt = _dot_nt(transposed(q_ref[:, sl], hh), dsb) * (1.0 / LOG2E)
                dqt = _dot(transposed(window((k0, k1, k2), hp), hh), dsb)
                return dqt, dkt, dvt

            out = []
            pending = probs(0)
            for head in range(2 * PAIRS):
                nxt = probs(head + 1) if head + 1 < 2 * PAIRS else None
                out.append(grads(head, pending))
                pending = nxt
            for hp in range(PAIRS):
                sl = slice(hp * LANES, (hp + 1) * LANES)
                dqt, dkt, dvt = (jnp.concatenate([out[2 * hp][n], out[2 * hp + 1][n]], axis=0) for n in range(3))
                dq_ref[:, sl] = dqt.T.astype(BF16)
                for w in range(3):
                    slot = lax.rem(b + w + 1, 3)
                    cols = slice(w * QB, (w + 1) * QB)
                    if w == 2:
                        acck[hp, slot] = dkt[:, cols]
                        accv[hp, slot] = dvt[:, cols]
                    else:
                        acck[hp, slot] += dkt[:, cols]
                        accv[hp, slot] += dvt[:, cols]

        @pl.when(b < 2)
        def _():
            block(lax.broadcasted_iota(jnp.int32, (KW, 1), 0) >= (2 - b) * QB)

        @pl.when((b >= 2) & (b < nb))
        def _():
            block(None)

        done = lax.rem(b + 1, 3)
        for hp in range(PAIRS):
            sl = slice(hp * LANES, (hp + 1) * LANES)
            dk_ref[:, sl] = acck[hp, done].T.astype(BF16)
            dv_ref[:, sl] = accv[hp, done].T.astype(BF16)

    def cur(p, b):
        return (jnp.minimum(b, nb - 1), p)

    def window(col0):
        return [pl.BlockSpec((QB, SLAB), functools.partial(
            lambda p, b, back: (jnp.maximum(jnp.minimum(b, nb - 1) - back, 0), col0 + p), back=back))
            for back in (2, 1, 0)]

    def done_block(p, b):
        return (jnp.maximum(b - 2, 0), p)

    tile = pl.BlockSpec((2 * PAIRS, KW, QB), lambda p, b: (p, 0, 0))
    here = pl.BlockSpec((QB, SLAB), cur)
    return _call(
        body, name="attn_bwd", grid=(D_MODEL // SLAB, nb + 2),
        args=(qn, kn, kn, kn, proj, proj, proj, dyattn, y_attn, lse, bias),
        in_specs=[here] + window(0) + window(v_col0)
        + [here, here, pl.BlockSpec((2 * PAIRS, QB), lambda p, b: (p, jnp.minimum(b, nb - 1))), tile],
        out_specs=[here, pl.BlockSpec((QB, SLAB), done_block), pl.BlockSpec((QB, SLAB), done_block), tile],
        out_shape=[jax.ShapeDtypeStruct((t, D_MODEL), BF16)] * 3 + [jax.ShapeDtypeStruct((N_HEADS, KW, QB), F32)],
        scratch_shapes=[pltpu.VMEM((PAIRS, 3, LANES, QB), F32), pltpu.VMEM((PAIRS, 3, LANES, QB), F32)],
        semantics=("parallel", "arbitrary"), rider=rider)


def _qknorm_bwd(proj, dqn, dkn, gq, gk):
    t = proj.shape[0]
    tm = min(t, 512)
    scale = HEAD_DIM ** -0.5

    def body(q_ref, k_ref, dqn_ref, dkn_ref, gq_ref, gk_ref, o_ref, dgq_ref, dgk_ref):
        e = _head_sum_matrix()

        @pl.when(pl.program_id(0) == 0)
        def _():
            dgq_ref[...] = jnp.zeros_like(dgq_ref)
            dgk_ref[...] = jnp.zeros_like(dgk_ref)

        for n, (src, dn_ref, g_ref, dg_ref, sc) in enumerate(
                ((q_ref, dqn_ref, gq_ref, dgq_ref, scale), (k_ref, dkn_ref, gk_ref, dgk_ref, 1.0))):
            for s in range(D_MODEL // LANES):
                sl = slice(s * LANES, (s + 1) * LANES)
                xf = src[:, sl].astype(F32)
                r = lax.rsqrt(_head_sums(xf * xf, e) * (1.0 / HEAD_DIM) + EPS)
                xh = xf * r
                dn = dn_ref[:, sl].astype(F32) * sc
                dg_ref[:, sl] += _fold8(dn * xh)
                dxh = dn * g_ref[:, sl]
                mean = _head_sums(dxh * xh, e) * (1.0 / HEAD_DIM)
                o_ref[:, n * D_MODEL + s * LANES:n * D_MODEL + (s + 1) * LANES] = (r * (dxh - xh * mean)).astype(BF16)

    row = pl.BlockSpec((tm, D_MODEL), lambda i: (i, 0))
    vec = pl.BlockSpec((1, D_MODEL), lambda i: (0, 0))
    acc = pl.BlockSpec((SUBLANES, D_MODEL), lambda i: (0, 0))
    return pl.pallas_call(
        body, name="qknorm_bwd", grid=(t // tm,),
        in_specs=[row, pl.BlockSpec((tm, D_MODEL), lambda i: (i, 1)), row, row, vec, vec],
        out_specs=[pl.BlockSpec((tm, 2 * D_MODEL), lambda i: (i, 0)), acc, acc],
        out_shape=[jax.ShapeDtypeStruct((t, 2 * D_MODEL), BF16)] + [jax.ShapeDtypeStruct((SUBLANES, D_MODEL), F32)] * 2,
        compiler_params=_params(("arbitrary",)),
    )(proj, proj, dqn, dkn, gq, gk)


def _in_bwd(dqk, dv, dconv, w_in, dhg, x, g1, dx1, rider=None):
    t = x.shape[0]
    tm = min(t, 512)

    def body(dqk_ref, dv_ref, dc_ref, w_ref, dhg_ref, x_ref, g_ref, dx1_ref, dx_ref, dg_ref):
        acc = dhg_ref[...]
        slab = 0
        for src, n in ((dqk_ref, 2), (dv_ref, 1), (dc_ref, 3)):
            for s in range(n):
                acc = acc + _dot_nt(src[:, s * D_MODEL:(s + 1) * D_MODEL],
                                    w_ref[:, slab * D_MODEL:(slab + 1) * D_MODEL])
                slab += 1

        @pl.when(pl.program_id(0) == 0)
        def _():
            dg_ref[...] = jnp.zeros_like(dg_ref)

        dx, dg = _rmsnorm_bwd(x_ref[...], g_ref[...], acc)
        dx_ref[...] = dx1_ref[...] + dx
        dg_ref[...] += _fold8(dg)

    row = pl.BlockSpec((tm, D_MODEL), lambda i: (i, 0))
    return _call(
        body, name="in_bwd", grid=(t // tm,), args=(dqk, dv, dconv, w_in, dhg, x, g1, dx1),
        in_specs=[pl.BlockSpec((tm, 2 * D_MODEL), lambda i: (i, 0)), row,
                  pl.BlockSpec((tm, 3 * D_MODEL), lambda i: (i, 0)),
                  _resident(w_in.shape), row, row, pl.BlockSpec((1, D_MODEL), lambda i: (0, 0)), row],
        out_specs=[row, pl.BlockSpec((SUBLANES, D_MODEL), lambda i: (0, 0))],
        out_shape=[jax.ShapeDtypeStruct((t, D_MODEL), F32), jax.ShapeDtypeStruct((SUBLANES, D_MODEL), F32)],
        semantics=("arbitrary",), rider=rider)


def _bias_grad_fold(dbias, rider=None):
    def body(d_ref, o_ref):
        jj = lax.broadcasted_iota(jnp.int32, (QB, QB), 0)
        ii = lax.broadcasted_iota(jnp.int32, (QB, QB), 1)
        flip = (jj + ii == QB - 1).astype(BF16)
        low = jj + ii < QB
        pos, neg = [], []
        for w in range(KW // QB):
            x = d_ref[0, QB * w:QB * (w + 1), :]
            hi = x.astype(BF16)
            r1 = x - hi.astype(F32)
            mid = r1.astype(BF16)
            lo = (r1 - mid.astype(F32)).astype(BF16)
            xr = _dot(hi, flip) + _dot(mid, flip) + _dot(lo, flip)
            for keep, acc in ((low, pos), (jnp.logical_not(low), neg)):
                part = pltpu.roll(jnp.where(keep, xr, 0.0), 0, 1, stride=1, stride_axis=0)
                acc.append(jnp.sum(part, axis=0, keepdims=True))
        far = pos[1] + neg[0] + pos[0]
        o_ref[0] = jnp.zeros((SUBLANES, QB), F32)
        o_ref[0, 0:1, :] = neg[2]
        o_ref[0, 1:2, :] = pos[2] + neg[1]
        o_ref[0, 2:3, :] = jnp.broadcast_to(jnp.sum(far, axis=-1, keepdims=True), (1, QB))

    return _call(
        body, name="bias_grad_fold", grid=(N_HEADS,), args=(dbias,),
        in_specs=[pl.BlockSpec((1, KW, QB), lambda h: (h, 0, 0))],
        out_specs=[pl.BlockSpec((1, SUBLANES, QB), lambda h: (h, 0, 0))],
        out_shape=[jax.ShapeDtypeStruct((N_HEADS, SUBLANES, QB), F32)],
        semantics=("parallel",), rider=rider)


def _small_partials(dg1, dgq, dgk, dcb, dcw, dbg, dg2, dbias_fold, loss_tile):
    def head_fold(v):
        acc = v[:, 0:LANES]
        for s in range(1, D_MODEL // LANES):
            acc = acc + v[:, s * LANES:(s + 1) * LANES]
        return acc + pltpu.roll(acc, HEAD_DIM, 1)

    def body(dg1_ref, dgq_ref, dgk_ref, dcb_ref, dcw_ref, dbg_ref, dg2_ref, db_ref, loss_ref, o_ref):
        o_ref[...] = jnp.zeros_like(o_ref)
        o_ref[0:1, :] = jnp.sum(dg1_ref[...], axis=0, keepdims=True)
        o_ref[1:2, 0:LANES] = head_fold(jnp.sum(dgq_ref[...], axis=0, keepdims=True))
        o_ref[2:3, 0:LANES] = head_fold(jnp.sum(dgk_ref[...], axis=0, keepdims=True))
        o_ref[3:4, :] = jnp.sum(dcb_ref[...], axis=0, keepdims=True)
        for j in range(3):
            o_ref[4 + j:5 + j, :] = jnp.sum(dcw_ref[j * SUBLANES:(j + 1) * SUBLANES, :], axis=0, keepdims=True)
        o_ref[7:8, :] = jnp.sum(dbg_ref[:, :D_MODEL], axis=0, keepdims=True)
        o_ref[8:9, :] = jnp.sum(dbg_ref[:, D_MODEL:], axis=0, keepdims=True)
        o_ref[9:10, :] = jnp.sum(dg2_ref[...], axis=0, keepdims=True)
        for h in range(N_HEADS):
            for part in range(3):
                o_ref[10 + h:11 + h, part * QB:(part + 1) * QB] = db_ref[h, part:part + 1, :]
        loss = (0.5 / D_MODEL) * jnp.sum(jnp.sum(loss_ref[...], axis=0, keepdims=True), axis=-1, keepdims=True)
        o_ref[26:27, :] = jnp.broadcast_to(loss, (1, D_MODEL))

    return pl.pallas_call(
        body, name="small_partials",
        out_shape=jax.ShapeDtypeStruct((32, D_MODEL), F32),
        compiler_params=_params(),
    )(dg1, dgq, dgk, dcb, dcw, dbg, dg2, dbias_fold, loss_tile)


MID_AXES = (0, 0, 1, 0)
MLP_AXES = (1, 0)


def _local_step(x, h, target, norm1_g, q_norm_g, k_norm_g, bias, conv_w, conv_b, b_gate, norm2_g,
                w_in, mid_w, mlp_w, distributed):
    g1 = norm1_g.reshape(1, D_MODEL)
    g2 = norm2_g.reshape(1, D_MODEL)
    gq = jnp.tile(q_norm_g, N_HEADS).reshape(1, D_MODEL)
    gk = jnp.tile(k_norm_g, N_HEADS).reshape(1, D_MODEL)
    cb = conv_b.reshape(1, D_MODEL)

    (proj,), got = _in_proj(h, w_in, rider=_Gather(mid_w, MID_AXES) if distributed else None)
    w_ap, w_cp, w_g, w_out = got if distributed else mid_w
    gates, qn, kn, y_conv = _gates_norms_conv(h, proj, w_g, b_gate.reshape(1, 2 * D_MODEL), gq, gk, conv_w, cb)
    (y_attn, lse), got = _attn_fwd(qn, kn, proj, bias, rider=_Gather(mlp_w, MLP_AXES) if distributed else None)
    w_up, w_down = got if distributed else mlp_w
    ya, yc, merged, x1, h2 = _mix_out(y_attn, y_conv, gates, x, w_ap, w_cp, w_out, g2)
    a, dy, dyb, loss_tile = _mlp_fwd(h2, w_up, w_down, x1, target)

    da, dx1, dx1b, dg2 = _mlp_bwd(dyb, a, w_down, w_up, x1, dy, g2)
    gw_down = _wgrad("wgrad_down", a, [dyb], [1], relu_sq=True, token_block=4096)
    gw_up = _wgrad("wgrad_up", h2, [da], [D_FF // D_MODEL])
    (dgp, dya, dyc, dyattn, dyconv, dhg, dbg), mlp_swapped = _mix_bwd(
        dx1b, gates, ya, yc, w_out, w_ap, w_cp, w_g,
        rider=_PairSwap((gw_up, gw_down), MLP_AXES) if distributed else None)
    mid = tuple(_wgrad_group("wgrad_mid", [(y_attn, dya, 1), (y_conv, dyc, 1), (h, dgp, 2), (merged, dx1b, 1)]))
    gw_ap, gw_cp, gw_g, gw_out = mid
    (dconv, dcb, dcw), mid_swapped = _conv_bwd(
        dyconv, proj, conv_w, cb, rider=_PairSwap(mid, MID_AXES) if distributed else None)
    early = mid + (gw_up, gw_down)
    early_sums = (_pair_add(early, tuple(mid_swapped) + tuple(mlp_swapped), MID_AXES + MLP_AXES)
                  if distributed else None)
    (dqn, dkn, dv, dbias), early_shares = _attn_bwd(
        qn, kn, proj, dyattn, y_attn, lse, bias, rider=_ChipScatter(early_sums) if distributed else None)
    dqk, dgq, dgk = _qknorm_bwd(proj, dqn, dkn, gq, gk)
    gw_in = _wgrad("wgrad_in", h, [dqk, dv, dconv], [2, 1, 3])
    (dbias_fold,), in_swapped = _bias_grad_fold(dbias, rider=_PairSwap((gw_in,), (1,)) if distributed else None)
    in_sums = _pair_add((gw_in,), in_swapped, (1,)) if distributed else None
    (dx, dg1), in_shares = _in_bwd(dqk, dv, dconv, w_in, dhg, x, g1, dx1,
                                   rider=_ChipScatter(in_sums) if distributed else None)
    small = _small_partials(dg1, dgq, dgk, dcb, dcw, dbg, dg2, dbias_fold, loss_tile)
    grads = tuple(in_shares) + tuple(early_shares) if distributed else (gw_in,) + early
    return dx, grads, small


def _me():
    return lax.axis_index("x"), lax.axis_index("y"), lax.axis_index("c")


def _peer(me, rel):
    x, y, c = me
    return (1 - x if rel & 4 else x, 1 - y if rel & 2 else y, 1 - c if rel & 1 else c)


def _linear(dev):
    return 4 * dev[0] + 2 * dev[1] + dev[2]


def _block(ref, axis, idx, size):
    return ref.at[pl.ds(idx * size, size), :] if axis == 0 else ref.at[:, pl.ds(idx * size, size)]


def _cast_shards(shards):
    def body(*refs):
        for src, dst in zip(refs[:len(shards)], refs[len(shards):]):
            dst[...] = src[...].astype(BF16)

    return pl.pallas_call(
        body, name="cast_shards",
        out_shape=[jax.ShapeDtypeStruct(s.shape, BF16) for s in shards],
        compiler_params=_params(),
    )(*shards)


class _Gather:
    def __init__(self, shards, axes):
        self.arrays, self.axes, self.n = list(shards), tuple(axes), len(shards)
        self.sizes = [s.shape[axis] for s, axis in zip(shards, axes)]
        self.out_shape = []
        for s, axis in zip(shards, axes):
            shape = (s.shape[0] * N_DEV, s.shape[1]) if axis == 0 else (s.shape[0], s.shape[1] * N_DEV)
            self.out_shape.append(jax.ShapeDtypeStruct(shape, s.dtype))
        self.scratch = [pltpu.SemaphoreType.DMA((self.n, 7)), pltpu.SemaphoreType.DMA((self.n, 7)),
                        pltpu.SemaphoreType.DMA((self.n,))]

    def _copies(self, srcs, outs, sems):
        send_sems, recv_sems, local_sems = sems
        me = _me()
        sibling = _peer(me, 1)
        chips = [_peer(me, rel) for rel in (4, 2, 6)]

        def rows(a, dev):
            return _block(outs[a], self.axes[a], _linear(dev), self.sizes[a])

        def copy(a, k, block_dev, to, src=None):
            return pltpu.make_async_remote_copy(
                src_ref=rows(a, block_dev) if src is None else src, dst_ref=rows(a, block_dev),
                send_sem=send_sems.at[a, k], recv_sem=recv_sems.at[a, k], device_id=to, device_id_type=MESH_T)

        own = [pltpu.make_async_copy(srcs[a], rows(a, me), local_sems.at[a]) for a in range(self.n)]
        first = []
        for a in range(self.n):
            first.append(copy(a, 0, me, sibling, src=srcs[a]))
            for j, chip in enumerate(chips):
                first.append(copy(a, 1 + j, me, chip, src=srcs[a]))
        return me, sibling, chips, copy, own, first

    def start(self, srcs, outs, sems):
        _, _, _, _, own, first = self._copies(srcs, outs, sems)
        for cp in own + first:
            cp.start()

    def finish(self, srcs, outs, sems):
        me, sibling, chips, copy, own, first = self._copies(srcs, outs, sems)
        passed = []
        for a in range(self.n):
            for j, chip in enumerate(chips):
                copy(a, 1 + j, chip, me).wait_recv()
                fwd = copy(a, 4 + j, chip, sibling)
                fwd.start()
                passed.append(fwd)
        for a in range(self.n):
            copy(a, 0, sibling, me).wait_recv()
            for j, chip in enumerate(chips):
                copy(a, 4 + j, _peer(chip, 1), me).wait_recv()
        for cp in first + passed:
            cp.wait_send()
        for cp in own:
            cp.wait()


N_CHIPS = 4


def _shard_shape(g, axis):
    return (g.shape[0] // N_DEV, g.shape[1]) if axis == 0 else (g.shape[0], g.shape[1] // N_DEV)


class _PairSwap:
    def __init__(self, grads, axes):
        self.arrays, self.axes, self.n = list(grads), tuple(axes), len(grads)
        self.sizes = [g.shape[axis] // N_DEV for g, axis in zip(grads, axes)]
        self.out_shape = [jax.ShapeDtypeStruct((N_CHIPS,) + _shard_shape(g, axis), g.dtype)
                          for g, axis in zip(grads, axes)]
        self.scratch = [pltpu.SemaphoreType.DMA((self.n, N_CHIPS)), pltpu.SemaphoreType.DMA((self.n, N_CHIPS))]

    def _copies(self, srcs, outs, sems):
        send_sems, recv_sems = sems
        x, y, c = _me()
        sibling = (x, y, 1 - c)
        copies = []
        for a in range(self.n):
            for chip in range(N_CHIPS):
                owner_idx = 2 * chip + (1 - c)
                copies.append(pltpu.make_async_remote_copy(
                    src_ref=_block(srcs[a], self.axes[a], owner_idx, self.sizes[a]), dst_ref=outs[a].at[chip],
                    send_sem=send_sems.at[a, chip], recv_sem=recv_sems.at[a, chip],
                    device_id=sibling, device_id_type=MESH_T))
        return copies

    def start(self, srcs, outs, sems):
        for cp in self._copies(srcs, outs, sems):
            cp.start()

    def finish(self, srcs, outs, sems):
        for cp in self._copies(srcs, outs, sems):
            cp.wait()


def _pair_add(grads, swapped, axes):
    n = len(grads)
    c_arr = lax.axis_index("c").astype(jnp.int32).reshape(1)

    def body(c_ref, *refs):
        del c_ref
        mine, got, outs = refs[:n], refs[n:2 * n], refs[2 * n:]
        for a in range(n):
            outs[a][0] = (mine[a][...].astype(F32) + got[a][0].astype(F32)).astype(BF16)

    in_specs, out_specs, out_shape = [], [], []
    for g, axis in zip(grads, axes):
        shard = _shard_shape(g, axis)
        if axis == 0:
            in_specs.append(pl.BlockSpec(shard, lambda s, c_ref: (2 * s + c_ref[0], 0)))
        else:
            in_specs.append(pl.BlockSpec(shard, lambda s, c_ref: (0, 2 * s + c_ref[0])))
    for g, axis in zip(grads, axes):
        shard = _shard_shape(g, axis)
        in_specs.append(pl.BlockSpec((1,) + shard, lambda s, c_ref: (s, 0, 0)))
        out_specs.append(pl.BlockSpec((1,) + shard, lambda s, c_ref: (s, 0, 0)))
        out_shape.append(jax.ShapeDtypeStruct((N_CHIPS,) + shard, BF16))
    return pl.pallas_call(
        body, name="pair_add_" + str(n),
        grid_spec=pltpu.PrefetchScalarGridSpec(num_scalar_prefetch=1, grid=(N_CHIPS,), in_specs=in_specs,
                                               out_specs=out_specs),
        out_shape=out_shape, compiler_params=_params(("arbitrary",)),
    )(c_arr, *grads, *swapped)


class _ChipScatter:
    def __init__(self, sums):
        self.arrays, self.n = list(sums), len(sums)
        self.out_shape = [jax.ShapeDtypeStruct(s.shape, s.dtype) for s in sums]
        self.scratch = [pltpu.SemaphoreType.DMA((self.n, 3)), pltpu.SemaphoreType.DMA((self.n, 3)),
                        pltpu.SemaphoreType.DMA((self.n,))]

    def _copies(self, srcs, outs, sems):
        send_sems, recv_sems, local_sems = sems
        me = _me()
        my_chip = 2 * me[0] + me[1]
        own = [pltpu.make_async_copy(srcs[a].at[my_chip], outs[a].at[my_chip], local_sems.at[a])
               for a in range(self.n)]
        sends, recvs = [], []
        for a in range(self.n):
            for k, rel in enumerate((4, 2, 6)):
                peer = _peer(me, rel)
                peer_chip = 2 * peer[0] + peer[1]
                sends.append(pltpu.make_async_remote_copy(
                    src_ref=srcs[a].at[peer_chip], dst_ref=outs[a].at[my_chip],
                    send_sem=send_sems.at[a, k], recv_sem=recv_sems.at[a, k], device_id=peer, device_id_type=MESH_T))
                recvs.append(pltpu.make_async_remote_copy(
                    src_ref=srcs[a].at[my_chip], dst_ref=outs[a].at[peer_chip],
                    send_sem=send_sems.at[a, k], recv_sem=recv_sems.at[a, k], device_id=peer, device_id_type=MESH_T))
        return own, sends, recvs

    def start(self, srcs, outs, sems):
        own, sends, _ = self._copies(srcs, outs, sems)
        for cp in own + sends:
            cp.start()

    def finish(self, srcs, outs, sems):
        own, sends, recvs = self._copies(srcs, outs, sems)
        for cp in recvs:
            cp.wait_recv()
        for cp in sends:
            cp.wait_send()
        for cp in own:
            cp.wait()


def _call(body, *, name, args, in_specs, out_specs, out_shape, grid=(), scratch_shapes=(), semantics=None,
          rider=None):
    if rider is None:
        return pl.pallas_call(
            body, name=name, grid=grid, in_specs=in_specs, out_specs=out_specs, out_shape=out_shape,
            scratch_shapes=list(scratch_shapes), compiler_params=_params(semantics))(*args), None
    n_in, n_out, n_scr, r = len(in_specs), len(out_specs), len(scratch_shapes), rider.n

    def wrapped(*refs):
        ins, r_ins = refs[:n_in], refs[n_in:n_in + r]
        outs = refs[n_in + r:n_in + r + n_out]
        r_outs = refs[n_in + r + n_out:n_in + 2 * r + n_out]
        scr = refs[n_in + 2 * r + n_out:n_in + 2 * r + n_out + n_scr]
        sems = refs[n_in + 2 * r + n_out + n_scr:]
        first, last = None, None
        for ax in range(len(grid)):
            f, l = pl.program_id(ax) == 0, pl.program_id(ax) == pl.num_programs(ax) - 1
            first = f if first is None else first & f
            last = l if last is None else last & l
        if first is None:
            rider.start(r_ins, r_outs, sems)
            body(*ins, *outs, *scr)
            rider.finish(r_ins, r_outs, sems)
            return

        @pl.when(first)
        def _():
            rider.start(r_ins, r_outs, sems)

        body(*ins, *outs, *scr)

        @pl.when(last)
        def _():
            rider.finish(r_ins, r_outs, sems)

    any_spec = pl.BlockSpec(memory_space=pl.ANY)
    out = pl.pallas_call(
        wrapped, name=name, grid=grid, in_specs=list(in_specs) + [any_spec] * r,
        out_specs=list(out_specs) + [any_spec] * r, out_shape=list(out_shape) + rider.out_shape,
        scratch_shapes=list(scratch_shapes) + rider.scratch,
        compiler_params=_params(None if semantics is None else ("arbitrary",) * len(semantics)),
    )(*args, *rider.arrays)
    return out[:n_out], out[n_out:]


def _all_reduce_small(part):
    def body(p_ref, o_ref, slots, send_sems, recv_sems):
        me = _me()
        my_idx = _linear(me)
        slots[my_idx] = p_ref[...]
        sends = []
        for rel in range(1, N_DEV):
            cp = pltpu.make_async_remote_copy(
                src_ref=p_ref, dst_ref=slots.at[my_idx], send_sem=send_sems.at[rel - 1],
                recv_sem=recv_sems.at[rel - 1], device_id=_peer(me, rel), device_id_type=MESH_T)
            cp.start()
            sends.append(cp)
        for rel in range(1, N_DEV):
            frm = _peer(me, rel)
            pltpu.make_async_remote_copy(
                src_ref=p_ref, dst_ref=slots.at[_linear(frm)], send_sem=send_sems.at[rel - 1],
                recv_sem=recv_sems.at[rel - 1], device_id=frm, device_id_type=MESH_T).wait_recv()
        for cp in sends:
            cp.wait_send()
        total = slots[0]
        for d in range(1, N_DEV):
            total = total + slots[d]
        o_ref[...] = total

    return pl.pallas_call(
        body, name="all_reduce_small",
        in_specs=[pl.BlockSpec(memory_space=pltpu.VMEM)], out_specs=pl.BlockSpec(memory_space=pltpu.VMEM),
        out_shape=jax.ShapeDtypeStruct(part.shape, F32),
        scratch_shapes=[pltpu.VMEM((N_DEV,) + part.shape, F32), pltpu.SemaphoreType.DMA((7,)),
                        pltpu.SemaphoreType.DMA((7,))],
        compiler_params=_params(),
    )(part)


def _adamw_math(w, g, m, v):
    m = ADAM_B1 * m + (1.0 - ADAM_B1) * g
    v = ADAM_B2 * v + (1.0 - ADAM_B2) * jnp.square(g)
    m_hat = m / (1.0 - ADAM_B1 ** ADAM_STEP)
    v_hat = v / (1.0 - ADAM_B2 ** ADAM_STEP)
    delta = -ADAM_LR * (m_hat / (jnp.sqrt(v_hat) + ADAM_EPS) + ADAM_WD * w)
    return delta, m, v


ADAMW_STEPS = 4


def _adamw_big(shares, ws, ms, vs, rider=None):
    n = len(ws)

    def body(*refs):
        s_refs, w_refs, m_refs, v_refs = (refs[a * n:(a + 1) * n] for a in range(4))
        outs = refs[4 * n:]
        for a in range(n):
            g = s_refs[a][0].astype(F32)
            for d in range(1, N_CHIPS):
                g = g + s_refs[a][d].astype(F32)
            outs[4 * a][...] = g
            outs[4 * a + 1][...], outs[4 * a + 2][...], outs[4 * a + 3][...] = _adamw_math(
                w_refs[a][...], g, m_refs[a][...], v_refs[a][...])

    def chunk(w):
        return pl.BlockSpec((w.shape[0] // ADAMW_STEPS, w.shape[1]), lambda i: (i, 0))

    def share_chunk(w):
        return pl.BlockSpec((N_CHIPS, w.shape[0] // ADAMW_STEPS, w.shape[1]), lambda i: (0, i, 0))

    out, rider_out = _call(
        body, name="adamw_big", grid=(ADAMW_STEPS,), args=(*shares, *ws, *ms, *vs),
        in_specs=[share_chunk(w) for w in ws] + [chunk(w) for w in ws] * 3,
        out_specs=[chunk(w) for w in ws for _ in range(4)],
        out_shape=[jax.ShapeDtypeStruct(w.shape, F32) for w in ws for _ in range(4)],
        semantics=("parallel",), rider=rider)
    return [tuple(out[4 * a:4 * a + 4]) for a in range(n)], rider_out


def _adamw_small(quads):
    n = len(quads)

    def body(*refs):
        ins, outs = refs[:4 * n], refs[4 * n:]
        for p in range(n):
            g_ref, w_ref, m_ref, v_ref = ins[4 * p:4 * p + 4]
            d_ref, nm_ref, nv_ref = outs[3 * p:3 * p + 3]
            d_ref[...], nm_ref[...], nv_ref[...] = _adamw_math(w_ref[...], g_ref[...], m_ref[...], v_ref[...])

    flat = [a for quad in quads for a in quad]
    out = pl.pallas_call(
        body, name="adamw_small",
        out_shape=[jax.ShapeDtypeStruct(quad[1].shape, F32) for quad in quads for _ in range(3)],
        compiler_params=_params(),
    )(*flat)
    return [tuple(out[3 * p:3 * p + 3]) for p in range(n)]


def kernel(x, norm1_g, w_in, q_norm_g, k_norm_g, rel_bias, conv_w, conv_b, w_attn_proj, w_conv_proj, w_gate, b_gate, w_out, norm2_g, w_up, w_down, loss_target, m_norm1_g, m_w_in, m_q_norm_g, m_k_norm_g, m_rel_bias, m_conv_w, m_conv_b, m_w_attn_proj, m_w_conv_proj, m_w_gate, m_b_gate, m_w_out, m_norm2_g, m_w_up, m_w_down, v_norm1_g, v_w_in, v_q_norm_g, v_k_norm_g, v_rel_bias, v_conv_w, v_conv_b, v_w_attn_proj, v_w_conv_proj, v_w_gate, v_b_gate, v_w_out, v_norm2_g, v_w_up, v_w_down):
    my_idx = _linear(_me())
    big_w = (w_in, w_attn_proj, w_conv_proj, w_gate, w_out, w_up, w_down)
    big_m = (m_w_in, m_w_attn_proj, m_w_conv_proj, m_w_gate, m_w_out, m_w_up, m_w_down)
    big_v = (v_w_in, v_w_attn_proj, v_w_conv_proj, v_w_gate, v_w_out, v_w_up, v_w_down)
    big_names = ("w_in", "w_attn_proj", "w_conv_proj", "w_gate", "w_out", "w_up", "w_down")

    conv_w_tile = jnp.pad(conv_w, ((0, SUBLANES - conv_w.shape[0]), (0, 0)))
    (w_in_shard,) = _cast_shards(big_w[:1])
    (bias, h, *shards), (w_in_full, conv_w_rows) = _bias_tiles(
        rel_bias, x[0], norm1_g.reshape(1, D_MODEL), big_w[1:],
        rider=_Gather((w_in_shard, conv_w_tile), (1, 1)))

    dx, shares, small = _local_step(x[0], h, loss_target[0], norm1_g, q_norm_g, k_norm_g, bias, conv_w_rows[:3],
                                    conv_b, b_gate, norm2_g, w_in_full, tuple(shards[0:4]), tuple(shards[4:6]), True)

    big_out, _ = _adamw_big(shares, big_w, big_m, big_v)
    tot = _all_reduce_small(small)
    g_rel_bias = jnp.concatenate(
        [tot[10:26, :QB][:, ::-1], tot[10:26, QB:2 * QB][:, ::-1], tot[10:26, 2 * QB:2 * QB + 1]], axis=1)
    g_conv_w = lax.dynamic_slice(tot[4:7], (0, my_idx * LANES), (3, LANES))
    small_g = [tot[0:1], tot[1:2, :HEAD_DIM], tot[2:3, :HEAD_DIM], g_rel_bias, g_conv_w, tot[3:4],
               tot[7:9].reshape(1, 2 * D_MODEL), tot[9:10]]
    small_w = (norm1_g, q_norm_g, k_norm_g, rel_bias, conv_w, conv_b, b_gate, norm2_g)
    small_m = (m_norm1_g, m_q_norm_g, m_k_norm_g, m_rel_bias, m_conv_w, m_conv_b, m_b_gate, m_norm2_g)
    small_v = (v_norm1_g, v_q_norm_g, v_k_norm_g, v_rel_bias, v_conv_w, v_conv_b, v_b_gate, v_norm2_g)

    def two_d(a):
        return a.reshape(1, -1) if a.ndim == 1 else a

    small_out = _adamw_small([(g, two_d(w), two_d(m), two_d(v))
                              for g, w, m, v in zip(small_g, small_w, small_m, small_v)])

    order = ("norm1_g", "w_in", "q_norm_g", "k_norm_g", "rel_bias", "conv_w", "conv_b", "w_attn_proj", "w_conv_proj",
             "w_gate", "b_gate", "w_out", "norm2_g", "w_up", "w_down")
    small_names = ("norm1_g", "q_norm_g", "k_norm_g", "rel_bias", "conv_w", "conv_b", "b_gate", "norm2_g")
    res = {}
    for name, (g, d, nm, nv) in zip(big_names, big_out):
        res[name] = (g, d, nm, nv)
    for name, g, w, (d, nm, nv) in zip(small_names, small_g, small_w, small_out):
        res[name] = tuple(a.reshape(w.shape) for a in (g, d, nm, nv))
    loss = tot[26, 0]
    return (loss, dx[None], *[res[n][0] for n in order], *[res[n][1] for n in order],
            *[res[n][2] for n in order], *[res[n][3] for n in order])
```
